```python
import jax, jax.numpy as jnp
from jax import lax
import numpy as np

D_MODEL = 1024
BATCH = 4
SEQ = 4096
DEPTH = 1

HEAD_DIM = D_MODEL // 16
CONV_HEADS = 8
FFT_GROUPS = 8
CONV_CH = CONV_HEADS * HEAD_DIM
FFT_CH = FFT_GROUPS * HEAD_DIM
MIX_WIDTH = CONV_CH + FFT_CH
MIX_HEADS = CONV_HEADS + FFT_GROUPS
IN_COLS = 3 * CONV_CH + FFT_CH
CONV_K = 3
MEM_LEN = 256
XA_HEADS = 4
XA_HEAD_DIM = D_MODEL // XA_HEADS
N_GROUPS = 4
EXPERTS_PER_GROUP = 8
N_EXPERTS = N_GROUPS * EXPERTS_PER_GROUP
TOP_K = 2
D_EXPERT = D_MODEL // 2
ROUTE_BLOCK = 128
EPS = 1e-6

kernel_name = "hybrid_conv_fourier_xattn_hmoe_encoder"


def rmsnorm(x, g):
    xf = x.astype(jnp.float32)
    y = xf * lax.rsqrt(jnp.mean(xf * xf, axis=-1, keepdims=True) + EPS)
    return (y * g.astype(jnp.float32)).astype(x.dtype)


def parallel_mixer(h, w_in, conv_w, conv_b, head_norm_g, w_out):
    B, S, _ = h.shape
    u = h @ w_in
    b_gate, c_gate, v, u_f = jnp.split(u, [CONV_CH, 2 * CONV_CH, 3 * CONV_CH], axis=-1)
    z = lax.conv_general_dilated(
        c_gate * v, conv_w[:, None, :], window_strides=(1,),
        padding=[(CONV_K // 2, CONV_K // 2)],
        dimension_numbers=("NWC", "WIO", "NWC"),
        feature_group_count=CONV_CH) + conv_b
    y_conv = (b_gate * z).reshape(B, S, CONV_HEADS, HEAD_DIM)
    uf = u_f.reshape(B, S, FFT_GROUPS, HEAD_DIM).astype(jnp.float32)
    y_fft = jnp.fft.fftn(uf, axes=(1, 3), norm="ortho").real.astype(h.dtype)
    heads = jnp.concatenate([y_conv, y_fft], axis=2)
    heads = rmsnorm(heads, head_norm_g)
    return heads.reshape(B, S, MIX_WIDTH) @ w_out


def memory_cross_attention(h, m, w_q, w_kv, w_o):
    B, S, _ = h.shape
    q = (h @ w_q).reshape(B, S, XA_HEADS, XA_HEAD_DIM)
    k, v = jnp.split(m @ w_kv, 2, axis=-1)
    k = k.reshape(B, -1, XA_HEADS, XA_HEAD_DIM)
    v = v.reshape(B, -1, XA_HEADS, XA_HEAD_DIM)
    s = jnp.einsum("bqhd,bkhd->bhqk", q, k).astype(jnp.float32) * (XA_HEAD_DIM ** -0.5)
    p = jax.nn.softmax(s, axis=-1).astype(v.dtype)
    o = jnp.einsum("bhqk,bkhd->bqhd", p, v).reshape(B, S, D_MODEL)
    return o @ w_o


def hierarchical_moe(h, w_rg, b_rg, w_re, b_re, w_gate, w_up, w_down):
    B, S, D = h.shape
    T = B * S
    hf = h.reshape(T, D)
    g_prob = jax.nn.softmax((hf @ w_rg).astype(jnp.float32) + b_rg.astype(jnp.float32), axis=-1)
    g_idx = jnp.argmax(g_prob, axis=-1).astype(jnp.int32)
    g_w = jnp.take_along_axis(g_prob, g_idx[:, None], axis=-1)
    e_logits = ((hf @ w_re).astype(jnp.float32) + b_re.astype(jnp.float32)).reshape(T, N_GROUPS, EXPERTS_PER_GROUP)
    e_logits = jnp.take_along_axis(e_logits, g_idx[:, None, None], axis=1)[:, 0]
    e_prob = jax.nn.softmax(e_logits, axis=-1)
    top_p, top_e = lax.top_k(e_prob, TOP_K)
    gate = (g_w * top_p / jnp.sum(top_p, axis=-1, keepdims=True)).reshape(-1)
    eid = (g_idx[:, None] * EXPERTS_PER_GROUP + top_e.astype(jnp.int32)).reshape(-1)
    tok = jnp.arange(T * TOP_K, dtype=jnp.int32) // TOP_K
    order = jnp.argsort(eid)
    se = eid[order]
    counts = jnp.bincount(eid, length=N_EXPERTS)
    starts = jnp.cumsum(counts) - counts
    padded = (counts + ROUTE_BLOCK - 1) // ROUTE_BLOCK * ROUTE_BLOCK
    pend = jnp.cumsum(padded)
    pstart = pend - padded
    dest = pstart[se] + jnp.arange(T * TOP_K, dtype=jnp.int32) - starts[se]
    n_blk = -(-(T * TOP_K + N_EXPERTS * ROUTE_BLOCK) // ROUTE_BLOCK)
    P = n_blk * ROUTE_BLOCK
    slot_tok = jnp.zeros((P,), jnp.int32).at[dest].set(tok[order])
    slot_w = jnp.zeros((P,), jnp.float32).at[dest].set(gate[order])
    blk_start = jnp.arange(n_blk, dtype=pend.dtype) * ROUTE_BLOCK
    blk_e = jnp.minimum(jnp.searchsorted(pend, blk_start, side="right"), N_EXPERTS - 1)
    xs = hf[slot_tok].reshape(n_blk, ROUTE_BLOCK, D)

    def expert_block(args):
        xb, e = args
        a = xb @ w_gate[e]
        b = xb @ w_up[e]
        return (jax.nn.silu(a) * b) @ w_down[e]

    ys = lax.map(expert_block, (xs, blk_e)).reshape(P, D)
    out = jnp.zeros((T, D), h.dtype).at[slot_tok].add(ys * slot_w[:, None].astype(ys.dtype))
    return out.reshape(B, S, D)


def setup_inputs(seed: int = 0) -> dict:
    key = jax.random.key(seed)
    ks = jax.random.split(key, 24)
    L, D = DEPTH, D_MODEL
    nrm = jax.random.normal
    f32 = jnp.float32

    def gain(k, shape):
        return jnp.ones(shape, f32) + 0.01 * nrm(k, shape, f32)

    return {
        "x": nrm(ks[0], (BATCH, SEQ, D), f32),
        "mem": nrm(ks[1], (BATCH, MEM_LEN, D), f32),
        "norm_mix_g": gain(ks[2], (L, D)),
        "w_in": nrm(ks[3], (L, D, IN_COLS), f32) * D ** -0.5,
        "conv_w": nrm(ks[4], (L, CONV_K, CONV_CH), f32) * CONV_K ** -0.5,
        "conv_b": 0.01 * nrm(ks[5], (L, CONV_CH), f32),
        "head_norm_g": gain(ks[6], (L, MIX_HEADS, HEAD_DIM)),
        "w_out": nrm(ks[7], (L, MIX_WIDTH, D), f32) * MIX_WIDTH ** -0.5,
        "norm_xa_g": gain(ks[8], (L, D)),
        "norm_mem_g": gain(ks[9], (L, D)),
        "w_q": nrm(ks[10], (L, D, D), f32) * D ** -0.5,
        "w_kv": nrm(ks[11], (L, D, 2 * D), f32) * D ** -0.5,
        "w_o": nrm(ks[12], (L, D, D), f32) * D ** -0.5,
        "norm_ffn_g": gain(ks[13], (L, D)),
        "w_route_group": nrm(ks[14], (L, D, N_GROUPS), f32) * D ** -0.5,
        "b_route_group": 0.01 * nrm(ks[15], (L, N_GROUPS), f32),
        "w_route_expert": nrm(ks[16], (L, D, N_EXPERTS), f32) * D ** -0.5,
        "b_route_expert": 0.01 * nrm(ks[17], (L, N_EXPERTS), f32),
        "w_gate": nrm(ks[18], (L, N_EXPERTS, D, D_EXPERT), f32) * D ** -0.5,
        "w_up": nrm(ks[19], (L, N_EXPERTS, D, D_EXPERT), f32) * D ** -0.5,
        "w_down": nrm(ks[20], (L, N_EXPERTS, D_EXPERT, D), f32) * D_EXPERT ** -0.5,
        "final_norm_g": gain(ks[21], (D,)),
    }


def reference(x, mem, norm_mix_g, w_in, conv_w, conv_b, head_norm_g, w_out,
              norm_xa_g, norm_mem_g, w_q, w_kv, w_o,
              norm_ffn_g, w_route_group, b_route_group, w_route_expert, b_route_expert,
              w_gate, w_up, w_down, final_norm_g):
    for l in range(DEPTH):
        x = x + parallel_mixer(rmsnorm(x, norm_mix_g[l]), w_in[l], conv_w[l], conv_b[l],
                               head_norm_g[l], w_out[l])
        x = x + memory_cross_attention(rmsnorm(x, norm_xa_g[l]), rmsnorm(mem, norm_mem_g[l]),
                                       w_q[l], w_kv[l], w_o[l])
        x = x + hierarchical_moe(rmsnorm(x, norm_ffn_g[l]), w_route_group[l], b_route_group[l],
                                 w_route_expert[l], b_route_expert[l],
                                 w_gate[l], w_up[l], w_down[l])
    return rmsnorm(x, final_norm_g)
```

```python
import functools
import math

import numpy as np
import jax
import jax.numpy as jnp
from jax import lax
from jax.experimental import pallas as pl
from jax.experimental.pallas import tpu as pltpu

F32 = jnp.float32
BF16 = jnp.bfloat16
I32 = jnp.int32

D_MODEL = 1024
HEAD_DIM = 64
CONV_CH = 512
FFT_CH = 512
IN_COLS = 3 * CONV_CH + FFT_CH
MEM_LEN = 256
XA_HEADS = 4
XA_HEAD_DIM = D_MODEL // XA_HEADS
N_GROUPS = 4
EXPERTS_PER_GROUP = 8
N_EXPERTS = 32
TOP_K = 2
D_EXPERT = 512
EPS = 1e-6

FFT_N1 = 16
FFT_N2 = 256
FFT_K1_PER_STEP = 4

MIX_ROWS = 512
TRUNK_ROWS = 512
ROUTE_COLS = 512
ROW_DMA_TOKENS = 256
EXPERT_ROWS = 256
ROUTER_ROWS = 128
NEG_BIG = -1e30

VMEM_LIMIT = 56 * 1024 * 1024


def _rms(x, g):
    return x * lax.rsqrt(jnp.mean(x * x, axis=-1, keepdims=True) + EPS) * g


def _dot(a, b):
    return jnp.dot(a, b, preferred_element_type=F32)


def _dot_nt(a, b):
    return lax.dot_general(a, b, (((1,), (1,)), ((), ())), preferred_element_type=F32)


def _group_mean_matrix():
    g = np.kron(np.eye(FFT_CH // HEAD_DIM), np.full((HEAD_DIM, HEAD_DIM), 1.0 / HEAD_DIM))
    return jnp.asarray(g, dtype=BF16)


def _fft_stage2_matrices(seq):
    k1 = np.arange(FFT_N1)[:, None, None]
    k2 = np.arange(FFT_N2)[None, :, None]
    s2 = np.arange(FFT_N2)[None, None, :]
    ang = 2.0 * np.pi * ((s2 * (k1 + FFT_N1 * k2)) % seq) / seq
    c, s = np.cos(ang), np.sin(ang)
    top = np.concatenate([c, s], axis=2)
    bot = np.concatenate([-s, c], axis=2)
    return jnp.asarray(np.concatenate([top, bot], axis=1), dtype=BF16)


def _fft_channel_matrix(seq):
    c = np.arange(HEAD_DIM)
    ang = 2.0 * np.pi * ((c[:, None] * c[None, :]) % HEAD_DIM) / HEAD_DIM
    scale = 1.0 / math.sqrt(seq * HEAD_DIM)
    eye = np.eye(FFT_CH // HEAD_DIM)
    cs = np.concatenate([np.kron(eye, np.cos(ang)), np.kron(eye, np.sin(ang))], axis=0) * scale
    return jnp.asarray(cs, dtype=BF16)


def _strict_upper(n):
    return jnp.asarray(np.triu(np.ones((n, n)), k=1), dtype=BF16)


def _kv_kernel(mem_ref, g_ref, w_ref, o_ref):
    h = _rms(mem_ref[...], g_ref[...]).astype(BF16)
    o_ref[...] = _dot(h, w_ref[...].astype(BF16)).astype(BF16)


def _kv_proj(mem2d, g, w_kv):
    rows = mem2d.shape[0]
    cols = w_kv.shape[1]
    cb = 512
    return pl.pallas_call(
        _kv_kernel,
        grid=(cols // cb,),
        in_specs=[
            pl.BlockSpec((rows, D_MODEL), lambda j: (0, 0)),
            pl.BlockSpec((1, D_MODEL), lambda j: (0, 0)),
            pl.BlockSpec((D_MODEL, cb), lambda j: (0, j)),
        ],
        out_specs=pl.BlockSpec((rows, cb), lambda j: (0, j)),
        out_shape=jax.ShapeDtypeStruct((rows, cols), BF16),
        compiler_params=pltpu.CompilerParams(vmem_limit_bytes=VMEM_LIMIT),
        name="kv_proj",
    )(mem2d, g, w_kv)


def _mixer_in_kernel(x_ref, xp_ref, xn_ref, g_ref, w_ref, cw_ref, cb_ref, hg_ref, gm_ref,
                     conv_ref, uf_ref):
    i = pl.program_id(1)
    n_i = pl.num_programs(1)
    rows = x_ref.shape[0]
    g = g_ref[...]
    h = _rms(x_ref[...], g).astype(BF16)
    u = _dot(h, w_ref[...])
    b_gate = u[:, :CONV_CH]
    cv = u[:, CONV_CH:2 * CONV_CH] * u[:, 2 * CONV_CH:3 * CONV_CH]
    uf_ref[...] = u[:, 3 * CONV_CH:].astype(BF16)

    hh = jnp.concatenate([_rms(xp_ref[...], g), _rms(xn_ref[...], g)], axis=0).astype(BF16)
    uh = _dot(hh, w_ref[:, CONV_CH:3 * CONV_CH])
    cvh = uh[:, :CONV_CH] * uh[:, CONV_CH:]
    cv_prev = cvh[7:8, :] * jnp.where(i == 0, 0.0, 1.0)
    cv_next = cvh[8:9, :] * jnp.where(i == n_i - 1, 0.0, 1.0)

    row = lax.broadcasted_iota(I32, cv.shape, 0)
    cv_up = jnp.where(row == 0, cv_prev, pltpu.roll(cv, 1, 0))
    cv_dn = jnp.where(row == rows - 1, cv_next, pltpu.roll(cv, rows - 1, 0))
    z = cw_ref[0:1, :] * cv_up + cw_ref[1:2, :] * cv + cw_ref[2:3, :] * cv_dn + cb_ref[...]
    y = b_gate * z
    ms = _dot((y * y).astype(BF16), gm_ref[...])
    conv_ref[...] = (y * lax.rsqrt(ms + EPS) * hg_ref[...]).astype(BF16)


def _mixer_in(x2d, g, w_in, conv_w, conv_b, hg_conv, gm, batch, seq):
    n_i = seq // MIX_ROWS
    t = x2d.shape[0]
    r8 = MIX_ROWS // 8
    last8 = t // 8 - 1
    return pl.pallas_call(
        _mixer_in_kernel,
        grid=(batch, n_i),
        in_specs=[
            pl.BlockSpec((MIX_ROWS, D_MODEL), lambda b, i: (b * n_i + i, 0)),
            pl.BlockSpec((8, D_MODEL), lambda b, i: (jnp.maximum((b * n_i + i) * r8 - 1, 0), 0)),
            pl.BlockSpec((8, D_MODEL), lambda b, i: (jnp.minimum((b * n_i + i + 1) * r8, last8), 0)),
            pl.BlockSpec((1, D_MODEL), lambda b, i: (0, 0)),
            pl.BlockSpec((D_MODEL, IN_COLS), lambda b, i: (0, 0)),
            pl.BlockSpec((3, CONV_CH), lambda b, i: (0, 0)),
            pl.BlockSpec((1, CONV_CH), lambda b, i: (0, 0)),
            pl.BlockSpec((1, CONV_CH), lambda b, i: (0, 0)),
            pl.BlockSpec((CONV_CH, CONV_CH), lambda b, i: (0, 0)),
        ],
        out_specs=[
            pl.BlockSpec((MIX_ROWS, CONV_CH), lambda b, i: (b * n_i + i, 0)),
            pl.BlockSpec((MIX_ROWS, FFT_CH), lambda b, i: (b * n_i + i, 0)),
        ],
        out_shape=[
            jax.ShapeDtypeStruct((t, CONV_CH), BF16),
            jax.ShapeDtypeStruct((t, FFT_CH), BF16),
        ],
        compiler_params=pltpu.CompilerParams(vmem_limit_bytes=VMEM_LIMIT),
        name="mixer_in",
    )(x2d, x2d, x2d, g, w_in, conv_w, conv_b, hg_conv, gm)


_S1_ROWS = 16
_S1_LANES = 128


def _lincomb(terms):
    acc = None
    for coef, val in terms:
        if abs(coef) < 1e-12:
            continue
        if abs(coef - 1.0) < 1e-12:
            term, neg = val, False
        elif abs(coef + 1.0) < 1e-12:
            term, neg = val, True
        else:
            term, neg = coef * val, False
        if acc is None:
            acc = -term if neg else term
        else:
            acc = acc - term if neg else acc + term
    return acc


def _fft_stage1(x_ref, a_ref):
    half = FFT_N1 // 2
    cos = [[math.cos(2 * math.pi * ((k * j) % FFT_N1) / FFT_N1) for j in range(FFT_N1)] for k in range(FFT_N1)]
    sin = [[math.sin(2 * math.pi * ((k * j) % FFT_N1) / FFT_N1) for j in range(FFT_N1)] for k in range(FFT_N1)]

    def body(r, carry):
        r0 = pl.multiple_of(r * _S1_ROWS, _S1_ROWS)
        rows_re = pl.ds(r0, _S1_ROWS)
        rows_im = pl.ds(r0 + FFT_N2, _S1_ROWS)
        for lc in range(0, FFT_CH, _S1_LANES):
            lanes = slice(lc, lc + _S1_LANES)
            xs = [x_ref[j, rows_re, lanes].astype(F32) for j in range(FFT_N1)]
            ev = [None] + [xs[j] + xs[FFT_N1 - j] for j in range(1, half)]
            od = [None] + [xs[j] - xs[FFT_N1 - j] for j in range(1, half)]
            for k in range(half + 1):
                re = _lincomb([(1.0, xs[0]), (cos[k][half], xs[half])]
                              + [(cos[k][j], ev[j]) for j in range(1, half)])
                a_ref[k, rows_re, lanes] = re.astype(BF16)
                if k in (0, half):
                    zero = jnp.zeros_like(re).astype(BF16)
                    a_ref[k, rows_im, lanes] = zero
                else:
                    im = _lincomb([(-sin[k][j], od[j]) for j in range(1, half)])
                    a_ref[k, rows_im, lanes] = im.astype(BF16)
                    a_ref[FFT_N1 - k, rows_re, lanes] = re.astype(BF16)
                    a_ref[FFT_N1 - k, rows_im, lanes] = (-im).astype(BF16)
        return carry

    lax.fori_loop(0, FFT_N2 // _S1_ROWS, body, 0)


def _fourier_kernel(x_ref, m2_ref, cs_ref, gm_ref, hg_ref, o_ref, a_ref):
    j = pl.program_id(1)

    @pl.when(j == 0)
    def _():
        _fft_stage1(x_ref, a_ref)

    for kk in range(FFT_K1_PER_STEP):
        k1 = j * FFT_K1_PER_STEP + kk
        ri = _dot(m2_ref[kk], a_ref[k1])
        re = ri[:FFT_N2].astype(BF16)
        im = ri[FFT_N2:].astype(BF16)
        y = _dot(re, cs_ref[:FFT_CH, :]) + _dot(im, cs_ref[FFT_CH:, :])
        ms = _dot((y * y).astype(BF16), gm_ref[...])
        o_ref[:, kk * FFT_CH:(kk + 1) * FFT_CH] = (y * lax.rsqrt(ms + EPS) * hg_ref[...]).astype(BF16)


def _fourier(uf, m2, cs, gm, hg_fft, batch, seq):
    assert seq == FFT_N1 * FFT_N2
    x4 = uf.reshape(batch, FFT_N1, FFT_N2, FFT_CH)
    out = pl.pallas_call(
        _fourier_kernel,
        grid=(batch, FFT_N1 // FFT_K1_PER_STEP),
        in_specs=[
            pl.BlockSpec((None, FFT_N1, FFT_N2, FFT_CH), lambda b, j: (b, 0, 0, 0)),
            pl.BlockSpec((FFT_K1_PER_STEP, 2 * FFT_N2, 2 * FFT_N2), lambda b, j: (j, 0, 0)),
            pl.BlockSpec((2 * FFT_CH, FFT_CH), lambda b, j: (0, 0)),
            pl.BlockSpec((FFT_CH, FFT_CH), lambda b, j: (0, 0)),
            pl.BlockSpec((1, FFT_CH), lambda b, j: (0, 0)),
        ],
        out_specs=pl.BlockSpec((None, FFT_N2, FFT_K1_PER_STEP * FFT_CH), lambda b, j: (b, 0, j)),
        out_shape=jax.ShapeDtypeStruct((batch, FFT_N2, FFT_N1 * FFT_CH), BF16),
        scratch_shapes=[pltpu.VMEM((FFT_N1, 2 * FFT_N2, FFT_CH), BF16)],
        compiler_params=pltpu.CompilerParams(
            dimension_semantics=("arbitrary", "arbitrary"), vmem_limit_bytes=VMEM_LIMIT),
        name="fourier",
    )(x4, m2, cs, gm, hg_fft)
    return out.reshape(batch * seq, FFT_CH)


def _trunk_kernel(x_ref, conv_ref, fft_ref, wout_ref, gxa_ref, wq_ref, k_ref, v_ref, wo_ref,
                  gffn_ref, wr_ref, br_ref, x2_ref, h3_ref, lg_ref):
    x1 = x_ref[...] + _dot(conv_ref[...], wout_ref[:CONV_CH, :]) + _dot(fft_ref[...], wout_ref[CONV_CH:, :])
    h2 = _rms(x1, gxa_ref[...]).astype(BF16)
    q = _dot(h2, wq_ref[...]).astype(BF16)
    outs = []
    for hd in range(XA_HEADS):
        cols = slice(hd * XA_HEAD_DIM, (hd + 1) * XA_HEAD_DIM)
        s = _dot_nt(q[:, cols], k_ref[:, cols]) * (XA_HEAD_DIM ** -0.5)
        s = s - jnp.max(s, axis=-1, keepdims=True)
        p = jnp.exp(s)
        p = p / jnp.sum(p, axis=-1, keepdims=True)
        outs.append(_dot(p.astype(BF16), v_ref[:, cols]).astype(BF16))
    o = jnp.concatenate(outs, axis=-1)
    x2 = x1 + _dot(o, wo_ref[...])
    x2_ref[...] = x2
    h3 = _rms(x2, gffn_ref[...])
    h3_ref[...] = h3
    lg_ref[...] = _dot_nt(wr_ref[...], h3.astype(BF16)) + br_ref[...]


def _trunk(x2d, conv_n, fft_n, w_out, g_xa, w_q, kv, w_o, g_ffn, w_r_t, b_r, seq):
    t = x2d.shape[0]
    n_per_batch = seq // TRUNK_ROWS
    const = lambda i: (0, 0)
    return pl.pallas_call(
        _trunk_kernel,
        grid=(t // TRUNK_ROWS,),
        in_specs=[
            pl.BlockSpec((TRUNK_ROWS, D_MODEL), lambda i: (i, 0)),
            pl.BlockSpec((TRUNK_ROWS, CONV_CH), lambda i: (i, 0)),
            pl.BlockSpec((TRUNK_ROWS, FFT_CH), lambda i: (i, 0)),
            pl.BlockSpec((D_MODEL, D_MODEL), const),
            pl.BlockSpec((1, D_MODEL), const),
            pl.BlockSpec((D_MODEL, D_MODEL), const),
            pl.BlockSpec((MEM_LEN, D_MODEL), lambda i: (i // n_per_batch, 0)),
            pl.BlockSpec((MEM_LEN, D_MODEL), lambda i: (i // n_per_batch, 1)),
            pl.BlockSpec((D_MODEL, D_MODEL), const),
            pl.BlockSpec((1, D_MODEL), const),
            pl.BlockSpec((ROUTER_ROWS, D_MODEL), const),
            pl.BlockSpec((ROUTER_ROWS, 1), const),
        ],
        out_specs=[
            pl.BlockSpec((TRUNK_ROWS, D_MODEL), lambda i: (i, 0)),
            pl.BlockSpec((TRUNK_ROWS, D_MODEL), lambda i: (i, 0)),
            pl.BlockSpec((ROUTER_ROWS, TRUNK_ROWS), lambda i: (0, i)),
        ],
        out_shape=[
            jax.ShapeDtypeStruct((t, D_MODEL), F32),
            jax.ShapeDtypeStruct((t, D_MODEL), F32),
            jax.ShapeDtypeStruct((ROUTER_ROWS, t), F32),
        ],
        compiler_params=pltpu.CompilerParams(vmem_limit_bytes=VMEM_LIMIT),
        name="trunk",
    )(x2d, conv_n, fft_n, w_out, g_xa, w_q, kv, kv, w_o, g_ffn, w_r_t, b_r)


def _first_index_of_max(vals, vmax, row):
    return jnp.min(jnp.where(vals == vmax, row, vals.shape[0]), axis=0, keepdims=True)


def _route_kernel(lg_ref, tri_ref, idx_ref, gate_ref, cnt_ref, carry_ref):
    step = pl.program_id(0)

    @pl.when(step == 0)
    def _():
        carry_ref[...] = jnp.zeros_like(carry_ref)

    cols = lg_ref.shape[1]
    row8 = lax.broadcasted_iota(I32, (EXPERTS_PER_GROUP, cols), 0)

    gl = lg_ref[0:8, :]
    gmax = jnp.max(gl, axis=0, keepdims=True)
    g_w = 1.0 / jnp.sum(jnp.exp(gl - gmax), axis=0, keepdims=True)
    g_idx = _first_index_of_max(gl, gmax, row8)

    el = lg_ref[8:16, :]
    for g in range(1, N_GROUPS):
        el = jnp.where(g_idx == g, lg_ref[8 + 8 * g:16 + 8 * g, :], el)
    emax = jnp.max(el, axis=0, keepdims=True)
    ee = jnp.exp(el - emax)
    e_prob = ee / jnp.sum(ee, axis=0, keepdims=True)
    p1 = jnp.max(e_prob, axis=0, keepdims=True)
    i1 = _first_index_of_max(e_prob, p1, row8)
    rest = jnp.where(row8 == i1, -1.0, e_prob)
    p2 = jnp.max(rest, axis=0, keepdims=True)
    i2 = _first_index_of_max(rest, p2, row8)
    denom = p1 + p2
    gate_ref[0:1, :] = g_w * p1 / denom
    gate_ref[1:2, :] = g_w * p2 / denom
    e1 = g_idx * EXPERTS_PER_GROUP + i1
    e2 = g_idx * EXPERTS_PER_GROUP + i2

    row32 = lax.broadcasted_iota(I32, (N_EXPERTS, cols), 0)
    hit1 = row32 == e1
    hit2 = row32 == e2
    onehot = jnp.where(hit1 | hit2, 1.0, 0.0)
    before = _dot(onehot.astype(BF16), tri_ref[...]) + carry_ref[:, 0:1]
    r1 = jnp.sum(jnp.where(hit1, before, 0.0), axis=0, keepdims=True)
    r2 = jnp.sum(jnp.where(hit2, before, 0.0), axis=0, keepdims=True)
    idx_ref[0:1, :] = e1
    idx_ref[1:2, :] = e2
    idx_ref[2:3, :] = r1.astype(I32)
    idx_ref[3:4, :] = r2.astype(I32)
    idx_ref[4:8, :] = jnp.zeros((4, cols), I32)
    gate_ref[2:8, :] = jnp.zeros((6, cols), F32)
    carry_ref[...] = carry_ref[...] + jnp.sum(onehot, axis=1, keepdims=True)
    cnt_ref[...] = carry_ref[...].astype(I32)


def _route(logits_t, tri):
    t = logits_t.shape[1]
    return pl.pallas_call(
        _route_kernel,
        grid=(t // ROUTE_COLS,),
        in_specs=[
            pl.BlockSpec((ROUTER_ROWS, ROUTE_COLS), lambda i: (0, i)),
            pl.BlockSpec((ROUTE_COLS, ROUTE_COLS), lambda i: (0, 0)),
        ],
        out_specs=[
            pl.BlockSpec((8, ROUTE_COLS), lambda i: (0, i)),
            pl.BlockSpec((8, ROUTE_COLS), lambda i: (0, i)),
            pl.BlockSpec((N_EXPERTS, 128), lambda i: (0, 0)),
        ],
        out_shape=[
            jax.ShapeDtypeStruct((8, t), I32),
            jax.ShapeDtypeStruct((8, t), F32),
            jax.ShapeDtypeStruct((N_EXPERTS, 128), I32),
        ],
        scratch_shapes=[pltpu.VMEM((N_EXPERTS, 128), F32)],
        compiler_params=pltpu.CompilerParams(dimension_semantics=("arbitrary",)),
        name="route",
    )(logits_t, tri)


def _dispatch_kernel(dest_ref, h_ref, xs_in_ref, xs_ref, sem):
    del xs_in_ref
    i = pl.program_id(0)
    n_tok = h_ref.shape[0]
    t_total = dest_ref.shape[0] // TOP_K

    def issue(r, carry):
        for k in range(TOP_K):
            d = dest_ref[k * t_total + i * n_tok + r]
            pltpu.make_async_copy(h_ref.at[pl.ds(r, 1), :], xs_ref.at[pl.ds(d, 1), :], sem).start()
        return carry

    lax.fori_loop(0, n_tok, issue, 0)
    for k in range(TOP_K):
        pltpu.make_async_copy(h_ref, xs_ref.at[pl.ds(0, n_tok), :], sem).wait()


def _dispatch(dest_flat, h3, xs_init):
    t = h3.shape[0]
    return pl.pallas_call(
        _dispatch_kernel,
        grid_spec=pltpu.PrefetchScalarGridSpec(
            num_scalar_prefetch=1,
            grid=(t // ROW_DMA_TOKENS,),
            in_specs=[
                pl.BlockSpec((ROW_DMA_TOKENS, D_MODEL), lambda i, dest: (i, 0)),
                pl.BlockSpec(memory_space=pl.ANY),
            ],
            out_specs=pl.BlockSpec(memory_space=pl.ANY),
            scratch_shapes=[pltpu.SemaphoreType.DMA(())],
        ),
        out_shape=jax.ShapeDtypeStruct(xs_init.shape, xs_init.dtype),
        input_output_aliases={2: 0},
        compiler_params=pltpu.CompilerParams(dimension_semantics=("arbitrary",)),
        name="dispatch",
    )(dest_flat, h3, xs_init)


def _experts_kernel(blk_e_ref, nblk_ref, xs_ref, wg_ref, wu_ref, wd_ref, ys_ref):
    del blk_e_ref

    used = pl.program_id(0) < nblk_ref[0]

    @pl.when(used)
    def _():
        x = xs_ref[...].astype(BF16)
        a = _dot(x, wg_ref[...].astype(BF16))
        b = _dot(x, wu_ref[...].astype(BF16))
        hmid = (a * jax.nn.sigmoid(a) * b).astype(BF16)
        ys_ref[...] = _dot(hmid, wd_ref[...].astype(BF16))

    @pl.when(jnp.logical_not(used))
    def _():
        ys_ref[...] = jnp.zeros_like(ys_ref)


def _experts(blk_e, nblk, xs, w_gate, w_up, w_down):
    p = xs.shape[0]
    row_blk = lambda i, be, nb: (jnp.minimum(i, nb[0] - 1), 0)
    w_blk = lambda i, be, nb: (be[i], 0, 0)
    return pl.pallas_call(
        _experts_kernel,
        grid_spec=pltpu.PrefetchScalarGridSpec(
            num_scalar_prefetch=2,
            grid=(p // EXPERT_ROWS,),
            in_specs=[
                pl.BlockSpec((EXPERT_ROWS, D_MODEL), row_blk),
                pl.BlockSpec((None, D_MODEL, D_EXPERT), w_blk),
                pl.BlockSpec((None, D_MODEL, D_EXPERT), w_blk),
                pl.BlockSpec((None, D_EXPERT, D_MODEL), w_blk),
            ],
            out_specs=pl.BlockSpec((EXPERT_ROWS, D_MODEL), lambda i, be, nb: (i, 0)),
        ),
        out_shape=jax.ShapeDtypeStruct((p, D_MODEL), F32),
        compiler_params=pltpu.CompilerParams(
            dimension_semantics=("arbitrary",), vmem_limit_bytes=VMEM_LIMIT),
        name="experts",
    )(blk_e, nblk, xs, w_gate, w_up, w_down)


def _combine_kernel(dest_ref, x2_ref, gate_ref, g_ref, ys_ref, o_ref, ybuf, sem):
    i = pl.program_id(0)
    n_tok = x2_ref.shape[0]
    t_total = dest_ref.shape[0] // TOP_K

    def issue(r, carry):
        for k in range(TOP_K):
            d = dest_ref[k * t_total + i * n_tok + r]
            pltpu.make_async_copy(ys_ref.at[pl.ds(d, 1), :], ybuf.at[k, pl.ds(r, 1), :], sem).start()
        return carry

    lax.fori_loop(0, n_tok, issue, 0)
    for k in range(TOP_K):
        pltpu.make_async_copy(ys_ref.at[pl.ds(0, n_tok), :], ybuf.at[k], sem).wait()
    gates = gate_ref[...]
    y = x2_ref[...] + (gates[:, 0:1] * ybuf[0] + gates[:, 1:2] * ybuf[1])
    o_ref[...] = _rms(y, g_ref[...])


def _combine(dest_flat, x2, gates_tk, g_final, ys):
    t = x2.shape[0]
    return pl.pallas_call(
        _combine_kernel,
        grid_spec=pltpu.PrefetchScalarGridSpec(
            num_scalar_prefetch=1,
            grid=(t // ROW_DMA_TOKENS,),
            in_specs=[
                pl.BlockSpec((ROW_DMA_TOKENS, D_MODEL), lambda i, dest: (i, 0)),
                pl.BlockSpec((ROW_DMA_TOKENS, TOP_K), lambda i, dest: (i, 0)),
                pl.BlockSpec((1, D_MODEL), lambda i, dest: (0, 0)),
                pl.BlockSpec(memory_space=pl.ANY),
            ],
            out_specs=pl.BlockSpec((ROW_DMA_TOKENS, D_MODEL), lambda i, dest: (i, 0)),
            scratch_shapes=[
                pltpu.VMEM((TOP_K, ROW_DMA_TOKENS, D_MODEL), F32),
                pltpu.SemaphoreType.DMA(()),
            ],
        ),
        out_shape=jax.ShapeDtypeStruct((t, D_MODEL), F32),
        compiler_params=pltpu.CompilerParams(dimension_semantics=("arbitrary",)),
        name="combine",
    )(dest_flat, x2, gates_tk, g_final, ys)


def _router_params(w_rg, b_rg, w_re, b_re):
    w = jnp.zeros((ROUTER_ROWS, D_MODEL), F32)
    w = w.at[0:N_GROUPS].set(w_rg.T).at[8:8 + N_EXPERTS].set(w_re.T)
    b = jnp.zeros((ROUTER_ROWS,), F32)
    b = b.at[0:N_GROUPS].set(b_rg).at[N_GROUPS:8].set(NEG_BIG).at[8:8 + N_EXPERTS].set(b_re)
    return w.astype(BF16), b.reshape(ROUTER_ROWS, 1)


def _layer(x2d, mem2d, batch, seq, norm_mix_g, w_in, conv_w, conv_b, head_norm_g, w_out,
           norm_xa_g, norm_mem_g, w_q, w_kv, w_o, norm_ffn_g, w_rg, b_rg, w_re, b_re,
           w_gate, w_up, w_down, out_norm_g):
    t = x2d.shape[0]
    row = lambda v: v.reshape(1, -1)
    hg = head_norm_g.reshape(-1)
    gm = _group_mean_matrix()

    kv = _kv_proj(mem2d, row(norm_mem_g), w_kv)
    conv_n, uf = _mixer_in(x2d, row(norm_mix_g), w_in.astype(BF16), conv_w, row(conv_b),
                           row(hg[:CONV_CH]), gm, batch, seq)
    fft_n = _fourier(uf.reshape(batch, seq, FFT_CH), _fft_stage2_matrices(seq), _fft_channel_matrix(seq),
                     gm, row(hg[CONV_CH:]), batch, seq)
    w_r_t, b_r = _router_params(w_rg, b_rg, w_re, b_re)
    x2, h3, logits_t = _trunk(x2d, conv_n, fft_n, w_out.astype(BF16), row(norm_xa_g), w_q.astype(BF16),
                              kv, w_o.astype(BF16), row(norm_ffn_g), w_r_t, b_r, seq)
    idx, gates, counts = _route(logits_t, _strict_upper(ROUTE_COLS))

    cnt = counts[:, 0]
    padded = (cnt + EXPERT_ROWS - 1) // EXPERT_ROWS * EXPERT_ROWS
    pend = jnp.cumsum(padded)
    pstart = pend - padded
    dest = (pstart[idx[0:2]] + idx[2:4]).reshape(-1)
    n_rows = t * TOP_K + N_EXPERTS * EXPERT_ROWS
    n_blk = n_rows // EXPERT_ROWS
    blk_start = jnp.arange(n_blk, dtype=I32) * EXPERT_ROWS
    blk_e = jnp.minimum(jnp.searchsorted(pend, blk_start, side="right"), N_EXPERTS - 1).astype(I32)
    nblk = (pend[-1:] // EXPERT_ROWS).astype(I32)

    xs = _dispatch(dest, h3, jnp.zeros((n_rows, D_MODEL), F32))
    ys = _experts(blk_e, nblk, xs, w_gate, w_up, w_down)
    return _combine(dest, x2, gates[0:2].T, row(out_norm_g), ys)


def kernel(x, mem, norm_mix_g, w_in, conv_w, conv_b, head_norm_g, w_out, norm_xa_g, norm_mem_g, w_q, w_kv,
           w_o, norm_ffn_g, w_route_group, b_route_group, w_route_expert, b_route_expert, w_gate, w_up,
           w_down, final_norm_g):
    batch, seq, _ = x.shape
    depth = norm_mix_g.shape[0]
    assert depth == 1, "the final norm is fused into the last layer's combine kernel"
    x2d = x.reshape(batch * seq, D_MODEL)
    mem2d = mem.reshape(batch * MEM_LEN, D_MODEL)
    l = 0
    out = _layer(x2d, mem2d, batch, seq, norm_mix_g[l], w_in[l], conv_w[l], conv_b[l], head_norm_g[l],
                 w_out[l], norm_xa_g[l], norm_mem_g[l], w_q[l], w_kv[l], w_o[l], norm_ffn_g[l],
                 w_route_group[l], b_route_group[l], w_route_expert[l], b_route_expert[l],
                 w_gate[l], w_up[l], w_down[l], final_norm_g)
    return out.reshape(batch, seq, D_MODEL)
```

```python
import functools
import math

import numpy as np
import jax
import jax.numpy as jnp
from jax import lax
from jax.experimental import pallas as pl
from jax.experimental.pallas import tpu as pltpu

F32 = jnp.float32
BF16 = jnp.bfloat16
I32 = jnp.int32

D_MODEL = 1024
HEAD_DIM = 64
CONV_CH = 512
FFT_CH = 512
IN_COLS = 3 * CONV_CH + FFT_CH
MEM_LEN = 256
XA_HEADS = 4
XA_HEAD_DIM = D_MODEL // XA_HEADS
N_GROUPS = 4
EXPERTS_PER_GROUP = 8
N_EXPERTS = 32
TOP_K = 2
D_EXPERT = 512
EPS = 1e-6

FFT_N1 = 16
FFT_N2 = 256
FFT_K1_PER_STEP = 4

MIX_ROWS = 512
TRUNK_ROWS = 512
ROUTE_COLS = 512
ROW_DMA_TOKENS = 256
EXPERT_ROWS = 256
ROUTER_ROWS = 128
LANES = 128
NEG_BIG = -1e30

VMEM_LIMIT = 56 * 1024 * 1024


def _rms(x, g):
    return x * lax.rsqrt(jnp.mean(x * x, axis=-1, keepdims=True) + EPS) * g


def _dot(a, b):
    return jnp.dot(a, b, preferred_element_type=F32)


def _dot_nt(a, b):
    return lax.dot_general(a, b, (((1,), (1,)), ((), ())), preferred_element_type=F32)


def _group_mean_matrix():
    g = np.kron(np.eye(FFT_CH // HEAD_DIM), np.full((HEAD_DIM, HEAD_DIM), 1.0 / HEAD_DIM))
    return jnp.asarray(g, dtype=BF16)


def _fft_stage2_matrices(seq):
    k1 = np.arange(FFT_N1)[:, None, None]
    k2 = np.arange(FFT_N2)[None, :, None]
    s2 = np.arange(FFT_N2)[None, None, :]
    ang = 2.0 * np.pi * ((s2 * (k1 + FFT_N1 * k2)) % seq) / seq
    c, s = np.cos(ang), np.sin(ang)
    top = np.concatenate([c, s], axis=2)
    bot = np.concatenate([-s, c], axis=2)
    return jnp.asarray(np.concatenate([top, bot], axis=1), dtype=BF16)


def _fft_channel_matrix(seq):
    c = np.arange(HEAD_DIM)
    ang = 2.0 * np.pi * ((c[:, None] * c[None, :]) % HEAD_DIM) / HEAD_DIM
    scale = 1.0 / math.sqrt(seq * HEAD_DIM)
    eye = np.eye(FFT_CH // HEAD_DIM)
    cs = np.concatenate([np.kron(eye, np.cos(ang)), np.kron(eye, np.sin(ang))], axis=0) * scale
    return jnp.asarray(cs, dtype=BF16)


def _strict_upper(n):
    return jnp.asarray(np.triu(np.ones((n, n)), k=1), dtype=BF16)


def _kv_kernel(mem_ref, g_ref, w_ref, o_ref):
    h = _rms(mem_ref[...], g_ref[...]).astype(BF16)
    o_ref[...] = _dot(h, w_ref[...].astype(BF16)).astype(BF16)


def _kv_proj(mem2d, g, w_kv):
    rows = mem2d.shape[0]
    cols = w_kv.shape[1]
    cb = 512
    return pl.pallas_call(
        _kv_kernel,
        grid=(cols // cb,),
        in_specs=[
            pl.BlockSpec((rows, D_MODEL), lambda j: (0, 0)),
            pl.BlockSpec((1, D_MODEL), lambda j: (0, 0)),
            pl.BlockSpec((D_MODEL, cb), lambda j: (0, j)),
        ],
        out_specs=pl.BlockSpec((rows, cb), lambda j: (0, j)),
        out_shape=jax.ShapeDtypeStruct((rows, cols), BF16),
        compiler_params=pltpu.CompilerParams(vmem_limit_bytes=VMEM_LIMIT),
        name="kv_proj",
    )(mem2d, g, w_kv)


def _mixer_in_kernel(x_ref, xp_ref, xn_ref, g_ref, w_ref, cw_ref, cb_ref, hg_ref, gm_ref,
                     conv_ref, uf_ref):
    i = pl.program_id(1)
    n_i = pl.num_programs(1)
    rows = x_ref.shape[0]
    g = g_ref[...]
    h = _rms(x_ref[...], g).astype(BF16)
    u = _dot(h, w_ref[...])
    b_gate = u[:, :CONV_CH]
    cv = u[:, CONV_CH:2 * CONV_CH] * u[:, 2 * CONV_CH:3 * CONV_CH]
    uf_ref[...] = u[:, 3 * CONV_CH:].astype(BF16)

    hh = jnp.concatenate([_rms(xp_ref[...], g), _rms(xn_ref[...], g)], axis=0).astype(BF16)
    uh = _dot(hh, w_ref[:, CONV_CH:3 * CONV_CH])
    cvh = uh[:, :CONV_CH] * uh[:, CONV_CH:]
    cv_prev = cvh[7:8, :] * jnp.where(i == 0, 0.0, 1.0)
    cv_next = cvh[8:9, :] * jnp.where(i == n_i - 1, 0.0, 1.0)

    row = lax.broadcasted_iota(I32, cv.shape, 0)
    cv_up = jnp.where(row == 0, cv_prev, pltpu.roll(cv, 1, 0))
    cv_dn = jnp.where(row == rows - 1, cv_next, pltpu.roll(cv, rows - 1, 0))
    z = cw_ref[0:1, :] * cv_up + cw_ref[1:2, :] * cv + cw_ref[2:3, :] * cv_dn + cb_ref[...]
    y = b_gate * z
    ms = _dot((y * y).astype(BF16), gm_ref[...])
    conv_ref[...] = (y * lax.rsqrt(ms + EPS) * hg_ref[...]).astype(BF16)


def _mixer_in(x2d, g, w_in, conv_w, conv_b, hg_conv, gm, batch, seq):
    n_i = seq // MIX_ROWS
    t = x2d.shape[0]
    r8 = MIX_ROWS // 8
    last8 = t // 8 - 1
    return pl.pallas_call(
        _mixer_in_kernel,
        grid=(batch, n_i),
        in_specs=[
            pl.BlockSpec((MIX_ROWS, D_MODEL), lambda b, i: (b * n_i + i, 0)),
            pl.BlockSpec((8, D_MODEL), lambda b, i: (jnp.maximum((b * n_i + i) * r8 - 1, 0), 0)),
            pl.BlockSpec((8, D_MODEL), lambda b, i: (jnp.minimum((b * n_i + i + 1) * r8, last8), 0)),
            pl.BlockSpec((1, D_MODEL), lambda b, i: (0, 0)),
            pl.BlockSpec((D_MODEL, IN_COLS), lambda b, i: (0, 0)),
            pl.BlockSpec((3, CONV_CH), lambda b, i: (0, 0)),
            pl.BlockSpec((1, CONV_CH), lambda b, i: (0, 0)),
            pl.BlockSpec((1, CONV_CH), lambda b, i: (0, 0)),
            pl.BlockSpec((CONV_CH, CONV_CH), lambda b, i: (0, 0)),
        ],
        out_specs=[
            pl.BlockSpec((MIX_ROWS, CONV_CH), lambda b, i: (b * n_i + i, 0)),
            pl.BlockSpec((MIX_ROWS, FFT_CH), lambda b, i: (b * n_i + i, 0)),
        ],
        out_shape=[
            jax.ShapeDtypeStruct((t, CONV_CH), BF16),
            jax.ShapeDtypeStruct((t, FFT_CH), BF16),
        ],
        compiler_params=pltpu.CompilerParams(vmem_limit_bytes=VMEM_LIMIT),
        name="mixer_in",
    )(x2d, x2d, x2d, g, w_in, conv_w, conv_b, hg_conv, gm)


_S1_ROWS = 16
_S1_LANES = 128


def _lincomb(terms):
    acc = None
    for coef, val in terms:
        if abs(coef) < 1e-12:
            continue
        if abs(coef - 1.0) < 1e-12:
            term, neg = val, False
        elif abs(coef + 1.0) < 1e-12:
            term, neg = val, True
        else:
            term, neg = coef * val, False
        if acc is None:
            acc = -term if neg else term
        else:
            acc = acc - term if neg else acc + term
    return acc


def _fft_stage1(x_ref, a_ref):
    half = FFT_N1 // 2
    cos = [[math.cos(2 * math.pi * ((k * j) % FFT_N1) / FFT_N1) for j in range(FFT_N1)] for k in range(FFT_N1)]
    sin = [[math.sin(2 * math.pi * ((k * j) % FFT_N1) / FFT_N1) for j in range(FFT_N1)] for k in range(FFT_N1)]

    def body(r, carry):
        r0 = pl.multiple_of(r * _S1_ROWS, _S1_ROWS)
        rows_re = pl.ds(r0, _S1_ROWS)
        rows_im = pl.ds(r0 + FFT_N2, _S1_ROWS)
        for lc in range(0, FFT_CH, _S1_LANES):
            lanes = slice(lc, lc + _S1_LANES)
            xs = [x_ref[j, rows_re, lanes].astype(F32) for j in range(FFT_N1)]
            ev = [None] + [xs[j] + xs[FFT_N1 - j] for j in range(1, half)]
            od = [None] + [xs[j] - xs[FFT_N1 - j] for j in range(1, half)]
            for k in range(half + 1):
                re = _lincomb([(1.0, xs[0]), (cos[k][half], xs[half])]
                              + [(cos[k][j], ev[j]) for j in range(1, half)])
                a_ref[k, rows_re, lanes] = re.astype(BF16)
                if k in (0, half):
                    zero = jnp.zeros_like(re).astype(BF16)
                    a_ref[k, rows_im, lanes] = zero
                else:
                    im = _lincomb([(-sin[k][j], od[j]) for j in range(1, half)])
                    a_ref[k, rows_im, lanes] = im.astype(BF16)
                    a_ref[FFT_N1 - k, rows_re, lanes] = re.astype(BF16)
                    a_ref[FFT_N1 - k, rows_im, lanes] = (-im).astype(BF16)
        return carry

    lax.fori_loop(0, FFT_N2 // _S1_ROWS, body, 0)


def _fourier_kernel(x_ref, m2_ref, cs_ref, gm_ref, hg_ref, o_ref, a_ref, y_ref):
    j = pl.program_id(1)

    @pl.when(j == 0)
    def _():
        _fft_stage1(x_ref, a_ref)

    for kk in range(FFT_K1_PER_STEP):
        k1 = j * FFT_K1_PER_STEP + kk
        ri = _dot(m2_ref[kk], a_ref[k1])
        re = ri[:FFT_N2].astype(BF16)
        im = ri[FFT_N2:].astype(BF16)
        y = _dot(re, cs_ref[:FFT_CH, :]) + _dot(im, cs_ref[FFT_CH:, :])
        ms = _dot((y * y).astype(BF16), gm_ref[...])
        yn = y * lax.rsqrt(ms + EPS) * hg_ref[...]
        for c in range(FFT_CH // LANES):
            y_ref[c, pl.ds(k1, FFT_N2, stride=FFT_N1), :] = yn[:, c * LANES:(c + 1) * LANES]

    @pl.when(j == pl.num_programs(1) - 1)
    def _():
        for c in range(FFT_CH // LANES):
            o_ref[:, c * LANES:(c + 1) * LANES] = y_ref[c].astype(BF16)


def _fourier(uf, m2, cs, gm, hg_fft, batch, seq):
    assert seq == FFT_N1 * FFT_N2
    x4 = uf.reshape(batch, FFT_N1, FFT_N2, FFT_CH)
    out = pl.pallas_call(
        _fourier_kernel,
        grid=(batch, FFT_N1 // FFT_K1_PER_STEP),
        in_specs=[
            pl.BlockSpec((None, FFT_N1, FFT_N2, FFT_CH), lambda b, j: (b, 0, 0, 0)),
            pl.BlockSpec((FFT_K1_PER_STEP, 2 * FFT_N2, 2 * FFT_N2), lambda b, j: (j, 0, 0)),
            pl.BlockSpec((2 * FFT_CH, FFT_CH), lambda b, j: (0, 0)),
            pl.BlockSpec((FFT_CH, FFT_CH), lambda b, j: (0, 0)),
            pl.BlockSpec((1, FFT_CH), lambda b, j: (0, 0)),
        ],
        out_specs=pl.BlockSpec((seq, FFT_CH), lambda b, j: (b, 0)),
        out_shape=jax.ShapeDtypeStruct((batch * seq, FFT_CH), BF16),
        scratch_shapes=[
            pltpu.VMEM((FFT_N1, 2 * FFT_N2, FFT_CH), BF16),
            pltpu.VMEM((FFT_CH // LANES, seq, LANES), F32),
        ],
        compiler_params=pltpu.CompilerParams(
            dimension_semantics=("arbitrary", "arbitrary"), vmem_limit_bytes=VMEM_LIMIT),
        name="fourier",
    )(x4, m2, cs, gm, hg_fft)
    return out


def _trunk_kernel(x_ref, conv_ref, fft_ref, wout_ref, gxa_ref, wq_ref, k_ref, v_ref, wo_ref,
                  gffn_ref, wr_ref, br_ref, x2_ref, h3_ref, lg_ref):
    x1 = x_ref[...] + _dot(conv_ref[...], wout_ref[:CONV_CH, :]) + _dot(fft_ref[...], wout_ref[CONV_CH:, :])
    h2 = _rms(x1, gxa_ref[...]).astype(BF16)
    q = _dot(h2, wq_ref[...]).astype(BF16)
    outs = []
    for hd in range(XA_HEADS):
        cols = slice(hd * XA_HEAD_DIM, (hd + 1) * XA_HEAD_DIM)
        s = _dot_nt(q[:, cols], k_ref[:, cols]) * (XA_HEAD_DIM ** -0.5)
        s = s - jnp.max(s, axis=-1, keepdims=True)
        p = jnp.exp(s)
        p = p / jnp.sum(p, axis=-1, keepdims=True)
        outs.append(_dot(p.astype(BF16), v_ref[:, cols]).astype(BF16))
    o = jnp.concatenate(outs, axis=-1)
    x2 = x1 + _dot(o, wo_ref[...])
    x2_ref[...] = x2
    h3 = _rms(x2, gffn_ref[...])
    h3_ref[...] = h3
    lg_ref[...] = _dot_nt(wr_ref[...], h3.astype(BF16)) + br_ref[...]


def _trunk(x2d, conv_n, fft_n, w_out, g_xa, w_q, kv, w_o, g_ffn, w_r_t, b_r, seq):
    t = x2d.shape[0]
    n_per_batch = seq // TRUNK_ROWS
    const = lambda i: (0, 0)
    return pl.pallas_call(
        _trunk_kernel,
        grid=(t // TRUNK_ROWS,),
        in_specs=[
            pl.BlockSpec((TRUNK_ROWS, D_MODEL), lambda i: (i, 0)),
            pl.BlockSpec((TRUNK_ROWS, CONV_CH), lambda i: (i, 0)),
            pl.BlockSpec((TRUNK_ROWS, FFT_CH), lambda i: (i, 0)),
            pl.BlockSpec((D_MODEL, D_MODEL), const),
            pl.BlockSpec((1, D_MODEL), const),
            pl.BlockSpec((D_MODEL, D_MODEL), const),
            pl.BlockSpec((MEM_LEN, D_MODEL), lambda i: (i // n_per_batch, 0)),
            pl.BlockSpec((MEM_LEN, D_MODEL), lambda i: (i // n_per_batch, 1)),
            pl.BlockSpec((D_MODEL, D_MODEL), const),
            pl.BlockSpec((1, D_MODEL), const),
            pl.BlockSpec((ROUTER_ROWS, D_MODEL), const),
            pl.BlockSpec((ROUTER_ROWS, 1), const),
        ],
        out_specs=[
            pl.BlockSpec((TRUNK_ROWS, D_MODEL), lambda i: (i, 0)),
            pl.BlockSpec((TRUNK_ROWS, D_MODEL), lambda i: (i, 0)),
            pl.BlockSpec((ROUTER_ROWS, TRUNK_ROWS), lambda i: (0, i)),
        ],
        out_shape=[
            jax.ShapeDtypeStruct((t, D_MODEL), F32),
            jax.ShapeDtypeStruct((t, D_MODEL), F32),
            jax.ShapeDtypeStruct((ROUTER_ROWS, t), F32),
        ],
        compiler_params=pltpu.CompilerParams(vmem_limit_bytes=VMEM_LIMIT),
        name="trunk",
    )(x2d, conv_n, fft_n, w_out, g_xa, w_q, kv, kv, w_o, g_ffn, w_r_t, b_r)


def _first_index_of_max(vals, vmax, row):
    return jnp.min(jnp.where(vals == vmax, row, vals.shape[0]), axis=0, keepdims=True)


def _route_kernel(lg_ref, tri_ref, idx_ref, gate_ref, cnt_ref, carry_ref):
    step = pl.program_id(0)

    @pl.when(step == 0)
    def _():
        carry_ref[...] = jnp.zeros_like(carry_ref)

    cols = lg_ref.shape[1]
    row8 = lax.broadcasted_iota(I32, (EXPERTS_PER_GROUP, cols), 0)

    gl = lg_ref[0:8, :]
    gmax = jnp.max(gl, axis=0, keepdims=True)
    g_w = 1.0 / jnp.sum(jnp.exp(gl - gmax), axis=0, keepdims=True)
    g_idx = _first_index_of_max(gl, gmax, row8)

    el = lg_ref[8:16, :]
    for g in range(1, N_GROUPS):
        el = jnp.where(g_idx == g, lg_ref[8 + 8 * g:16 + 8 * g, :], el)
    emax = jnp.max(el, axis=0, keepdims=True)
    ee = jnp.exp(el - emax)
    e_prob = ee / jnp.sum(ee, axis=0, keepdims=True)
    p1 = jnp.max(e_prob, axis=0, keepdims=True)
    i1 = _first_index_of_max(e_prob, p1, row8)
    rest = jnp.where(row8 == i1, -1.0, e_prob)
    p2 = jnp.max(rest, axis=0, keepdims=True)
    i2 = _first_index_of_max(rest, p2, row8)
    denom = p1 + p2
    gate_ref[0:1, :] = g_w * p1 / denom
    gate_ref[1:2, :] = g_w * p2 / denom
    e1 = g_idx * EXPERTS_PER_GROUP + i1
    e2 = g_idx * EXPERTS_PER_GROUP + i2

    row32 = lax.broadcasted_iota(I32, (N_EXPERTS, cols), 0)
    hit1 = row32 == e1
    hit2 = row32 == e2
    onehot = jnp.where(hit1 | hit2, 1.0, 0.0)
    before = _dot(onehot.astype(BF16), tri_ref[...]) + carry_ref[:, 0:1]
    r1 = jnp.sum(jnp.where(hit1, before, 0.0), axis=0, keepdims=True)
    r2 = jnp.sum(jnp.where(hit2, before, 0.0), axis=0, keepdims=True)
    idx_ref[0:1, :] = e1
    idx_ref[1:2, :] = e2
    idx_ref[2:3, :] = r1.astype(I32)
    idx_ref[3:4, :] = r2.astype(I32)
    idx_ref[4:8, :] = jnp.zeros((4, cols), I32)
    gate_ref[2:8, :] = jnp.zeros((6, cols), F32)
    carry_ref[...] = carry_ref[...] + jnp.sum(onehot, axis=1, keepdims=True)
    cnt_ref[...] = carry_ref[...].astype(I32)


def _route(logits_t, tri):
    t = logits_t.shape[1]
    return pl.pallas_call(
        _route_kernel,
        grid=(t // ROUTE_COLS,),
        in_specs=[
            pl.BlockSpec((ROUTER_ROWS, ROUTE_COLS), lambda i: (0, i)),
            pl.BlockSpec((ROUTE_COLS, ROUTE_COLS), lambda i: (0, 0)),
        ],
        out_specs=[
            pl.BlockSpec((8, ROUTE_COLS), lambda i: (0, i)),
            pl.BlockSpec((8, ROUTE_COLS), lambda i: (0, i)),
            pl.BlockSpec((N_EXPERTS, 128), lambda i: (0, 0)),
        ],
        out_shape=[
            jax.ShapeDtypeStruct((8, t), I32),
            jax.ShapeDtypeStruct((8, t), F32),
            jax.ShapeDtypeStruct((N_EXPERTS, 128), I32),
        ],
        scratch_shapes=[pltpu.VMEM((N_EXPERTS, 128), F32)],
        compiler_params=pltpu.CompilerParams(dimension_semantics=("arbitrary",)),
        name="route",
    )(logits_t, tri)


def _dispatch_kernel(dest_ref, h_ref, xs_in_ref, xs_ref, sem):
    del xs_in_ref
    i = pl.program_id(0)
    n_tok = h_ref.shape[0]
    t_total = dest_ref.shape[0] // TOP_K

    def issue(r, carry):
        for k in range(TOP_K):
            d = dest_ref[k * t_total + i * n_tok + r]
            pltpu.make_async_copy(h_ref.at[pl.ds(r, 1), :], xs_ref.at[pl.ds(d, 1), :], sem).start()
        return carry

    lax.fori_loop(0, n_tok, issue, 0)
    for k in range(TOP_K):
        pltpu.make_async_copy(h_ref, xs_ref.at[pl.ds(0, n_tok), :], sem).wait()


def _dispatch(dest_flat, h3, xs_init):
    t = h3.shape[0]
    return pl.pallas_call(
        _dispatch_kernel,
        grid_spec=pltpu.PrefetchScalarGridSpec(
            num_scalar_prefetch=1,
            grid=(t // ROW_DMA_TOKENS,),
            in_specs=[
                pl.BlockSpec((ROW_DMA_TOKENS, D_MODEL), lambda i, dest: (i, 0)),
                pl.BlockSpec(memory_space=pl.ANY),
            ],
            out_specs=pl.BlockSpec(memory_space=pl.ANY),
            scratch_shapes=[pltpu.SemaphoreType.DMA(())],
        ),
        out_shape=jax.ShapeDtypeStruct(xs_init.shape, xs_init.dtype),
        input_output_aliases={2: 0},
        compiler_params=pltpu.CompilerParams(dimension_semantics=("arbitrary",)),
        name="dispatch",
    )(dest_flat, h3, xs_init)


def _experts_kernel(blk_e_ref, nblk_ref, xs_ref, wg_ref, wu_ref, wd_ref, ys_ref):
    del blk_e_ref

    used = pl.program_id(0) < nblk_ref[0]

    @pl.when(used)
    def _():
        x = xs_ref[...].astype(BF16)
        a = _dot(x, wg_ref[...].astype(BF16))
        b = _dot(x, wu_ref[...].astype(BF16))
        hmid = (a * jax.nn.sigmoid(a) * b).astype(BF16)
        ys_ref[...] = _dot(hmid, wd_ref[...].astype(BF16))

    @pl.when(jnp.logical_not(used))
    def _():
        ys_ref[...] = jnp.zeros_like(ys_ref)


def _experts(blk_e, nblk, xs, w_gate, w_up, w_down):
    p = xs.shape[0]
    row_blk = lambda i, be, nb: (jnp.minimum(i, nb[0] - 1), 0)
    w_blk = lambda i, be, nb: (be[i], 0, 0)
    return pl.pallas_call(
        _experts_kernel,
        grid_spec=pltpu.PrefetchScalarGridSpec(
            num_scalar_prefetch=2,
            grid=(p // EXPERT_ROWS,),
            in_specs=[
                pl.BlockSpec((EXPERT_ROWS, D_MODEL), row_blk),
                pl.BlockSpec((None, D_MODEL, D_EXPERT), w_blk),
                pl.BlockSpec((None, D_MODEL, D_EXPERT), w_blk),
                pl.BlockSpec((None, D_EXPERT, D_MODEL), w_blk),
            ],
            out_specs=pl.BlockSpec((EXPERT_ROWS, D_MODEL), lambda i, be, nb: (i, 0)),
        ),
        out_shape=jax.ShapeDtypeStruct((p, D_MODEL), F32),
        compiler_params=pltpu.CompilerParams(
            dimension_semantics=("arbitrary",), vmem_limit_bytes=VMEM_LIMIT),
        name="experts",
    )(blk_e, nblk, xs, w_gate, w_up, w_down)


def _combine_kernel(dest_ref, x2_ref, gate_ref, g_ref, ys_ref, o_ref, ybuf, sem):
    i = pl.program_id(0)
    n_tok = x2_ref.shape[0]
    t_total = dest_ref.shape[0] // TOP_K

    def issue(r, carry):
        for k in range(TOP_K):
            d = dest_ref[k * t_total + i * n_tok + r]
            pltpu.make_async_copy(ys_ref.at[pl.ds(d, 1), :], ybuf.at[k, pl.ds(r, 1), :], sem).start()
        return carry

    lax.fori_loop(0, n_tok, issue, 0)
    for k in range(TOP_K):
        pltpu.make_async_copy(ys_ref.at[pl.ds(0, n_tok), :], ybuf.at[k], sem).wait()
    gates = gate_ref[...]
    y = x2_ref[...] + (gates[:, 0:1] * ybuf[0] + gates[:, 1:2] * ybuf[1])
    o_ref[...] = _rms(y, g_ref[...])


def _combine(dest_flat, x2, gates_tk, g_final, ys):
    t = x2.shape[0]
    return pl.pallas_call(
        _combine_kernel,
        grid_spec=pltpu.PrefetchScalarGridSpec(
            num_scalar_prefetch=1,
            grid=(t // ROW_DMA_TOKENS,),
            in_specs=[
                pl.BlockSpec((ROW_DMA_TOKENS, D_MODEL), lambda i, dest: (i, 0)),
                pl.BlockSpec((ROW_DMA_TOKENS, TOP_K), lambda i, dest: (i, 0)),
                pl.BlockSpec((1, D_MODEL), lambda i, dest: (0, 0)),
                pl.BlockSpec(memory_space=pl.ANY),
            ],
            out_specs=pl.BlockSpec((ROW_DMA_TOKENS, D_MODEL), lambda i, dest: (i, 0)),
            scratch_shapes=[
                pltpu.VMEM((TOP_K, ROW_DMA_TOKENS, D_MODEL), F32),
                pltpu.SemaphoreType.DMA(()),
            ],
        ),
        out_shape=jax.ShapeDtypeStruct((t, D_MODEL), F32),
        compiler_params=pltpu.CompilerParams(dimension_semantics=("arbitrary",)),
        name="combine",
    )(dest_flat, x2, gates_tk, g_final, ys)


def _router_params(w_rg, b_rg, w_re, b_re):
    w = jnp.zeros((ROUTER_ROWS, D_MODEL), F32)
    w = w.at[0:N_GROUPS].set(w_rg.T).at[8:8 + N_EXPERTS].set(w_re.T)
    b = jnp.zeros((ROUTER_ROWS,), F32)
    b = b.at[0:N_GROUPS].set(b_rg).at[N_GROUPS:8].set(NEG_BIG).at[8:8 + N_EXPERTS].set(b_re)
    return w.astype(BF16), b.reshape(ROUTER_ROWS, 1)


def _layer(x2d, mem2d, batch, seq, norm_mix_g, w_in, conv_w, conv_b, head_norm_g, w_out,
           norm_xa_g, norm_mem_g, w_q, w_kv, w_o, norm_ffn_g, w_rg, b_rg, w_re, b_re,
           w_gate, w_up, w_down, out_norm_g):
    t = x2d.shape[0]
    row = lambda v: v.reshape(1, -1)
    hg = head_norm_g.reshape(-1)
    gm = _group_mean_matrix()

    kv = _kv_proj(mem2d, row(norm_mem_g), w_kv)
    conv_n, uf = _mixer_in(x2d, row(norm_mix_g), w_in.astype(BF16), conv_w, row(conv_b),
                           row(hg[:CONV_CH]), gm, batch, seq)
    fft_n = _fourier(uf.reshape(batch, seq, FFT_CH), _fft_stage2_matrices(seq), _fft_channel_matrix(seq),
                     gm, row(hg[CONV_CH:]), batch, seq)
    w_r_t, b_r = _router_params(w_rg, b_rg, w_re, b_re)
    x2, h3, logits_t = _trunk(x2d, conv_n, fft_n, w_out.astype(BF16), row(norm_xa_g), w_q.astype(BF16),
                              kv, w_o.astype(BF16), row(norm_ffn_g), w_r_t, b_r, seq)
    idx, gates, counts = _route(logits_t, _strict_upper(ROUTE_COLS))

    cnt = counts[:, 0]
    padded = (cnt + EXPERT_ROWS - 1) // EXPERT_ROWS * EXPERT_ROWS
    pend = jnp.cumsum(padded)
    pstart = pend - padded
    experts = jnp.arange(N_EXPERTS, dtype=I32)[:, None, None]
    seg_start = jnp.sum(jnp.where(idx[None, 0:2] == experts, pstart[:, None, None], 0), axis=0)
    dest = (seg_start + idx[2:4]).reshape(-1)
    n_rows = t * TOP_K + N_EXPERTS * EXPERT_ROWS
    n_blk = n_rows // EXPERT_ROWS
    blk_start = jnp.arange(n_blk, dtype=I32) * EXPERT_ROWS
    blk_e = jnp.sum((pend[None, :] <= blk_start[:, None]).astype(I32), axis=1)
    blk_e = jnp.minimum(blk_e, N_EXPERTS - 1)
    nblk = (pend[-1:] // EXPERT_ROWS).astype(I32)

    xs = _dispatch(dest, h3, jnp.zeros((n_rows, D_MODEL), F32))
    ys = _experts(blk_e, nblk, xs, w_gate, w_up, w_down)
    return _combine(dest, x2, gates[0:2].T, row(out_norm_g), ys)


def kernel(x, mem, norm_mix_g, w_in, conv_w, conv_b, head_norm_g, w_out, norm_xa_g, norm_mem_g, w_q, w_kv,
           w_o, norm_ffn_g, w_route_group, b_route_group, w_route_expert, b_route_expert, w_gate, w_up,
           w_down, final_norm_g):
    batch, seq, _ = x.shape
    depth = norm_mix_g.shape[0]
    assert depth == 1, "the final norm is fused into the last layer's combine kernel"
    x2d = x.reshape(batch * seq, D_MODEL)
    mem2d = mem.reshape(batch * MEM_LEN, D_MODEL)
    l = 0
    out = _layer(x2d, mem2d, batch, seq, norm_mix_g[l], w_in[l], conv_w[l], conv_b[l], head_norm_g[l],
                 w_out[l], norm_xa_g[l], norm_mem_g[l], w_q[l], w_kv[l], w_o[l], norm_ffn_g[l],
                 w_route_group[l], b_route_group[l], w_route_expert[l], b_route_expert[l],
                 w_gate[l], w_up[l], w_down[l], final_norm_g)
    return out.reshape(batch, seq, D_MODEL)
```

```python
import functools
import math

import numpy as np
import jax
import jax.numpy as jnp
from jax import lax
from jax.experimental import pallas as pl
from jax.experimental.pallas import tpu as pltpu

F32 = jnp.float32
BF16 = jnp.bfloat16
I32 = jnp.int32

D_MODEL = 1024
HALF = D_MODEL // 2
HEAD_DIM = 64
CONV_CH = 512
FFT_CH = 512
IN_COLS = 3 * CONV_CH + FFT_CH
MEM_LEN = 256
XA_HEADS = 4
XA_HEAD_DIM = D_MODEL // XA_HEADS
N_GROUPS = 4
EXPERTS_PER_GROUP = 8
N_EXPERTS = 32
TOP_K = 2
D_EXPERT = 512
EPS = 1e-6

FFT_N1 = 16
FFT_N2 = 256
FFT_K1_PER_STEP = 4

LANES = 128
GRANULE = 8
MIX_ROWS = 512
TRUNK_ROWS = 512
LOCAL_ROWS = TOP_K * TRUNK_ROWS + N_EXPERTS * GRANULE
LOCAL_GRANULES = LOCAL_ROWS // GRANULE
EXPERT_ROWS = 256
ROUTER_ROWS = 128
NEG_BIG = -1e30
HI16 = -65536

VMEM_LIMIT = 56 * 1024 * 1024


def _rms(x, g):
    return x * lax.rsqrt(jnp.mean(x * x, axis=-1, keepdims=True) + EPS) * g


def _dot(a, b):
    return jnp.dot(a, b, preferred_element_type=F32)


def _dot_nt(a, b):
    return lax.dot_general(a, b, (((1,), (1,)), ((), ())), preferred_element_type=F32)


def _pack_halves(left_f32, right_f32):
    lb = lax.bitcast_convert_type(left_f32, I32)
    rb = lax.shift_right_logical(lax.bitcast_convert_type(right_f32, I32), jnp.int32(16))
    return lb | rb


def _unpack_halves(packed_i32):
    left = lax.bitcast_convert_type(packed_i32 & jnp.int32(HI16), F32)
    right = lax.bitcast_convert_type(lax.shift_left(packed_i32, jnp.int32(16)), F32)
    return left.astype(BF16), right.astype(BF16)


def _group_mean_matrix():
    g = np.kron(np.eye(FFT_CH // HEAD_DIM), np.full((HEAD_DIM, HEAD_DIM), 1.0 / HEAD_DIM))
    return jnp.asarray(g, dtype=BF16)


def _fft_stage2_matrices(seq):
    k1 = np.arange(FFT_N1)[:, None, None]
    k2 = np.arange(FFT_N2)[None, :, None]
    s2 = np.arange(FFT_N2)[None, None, :]
    ang = 2.0 * np.pi * ((s2 * (k1 + FFT_N1 * k2)) % seq) / seq
    c, s = np.cos(ang), np.sin(ang)
    top = np.concatenate([c, s], axis=2)
    bot = np.concatenate([-s, c], axis=2)
    return jnp.asarray(np.concatenate([top, bot], axis=1), dtype=BF16)


def _fft_channel_matrix(seq):
    c = np.arange(HEAD_DIM)
    ang = 2.0 * np.pi * ((c[:, None] * c[None, :]) % HEAD_DIM) / HEAD_DIM
    scale = 1.0 / math.sqrt(seq * HEAD_DIM)
    eye = np.eye(FFT_CH // HEAD_DIM)
    cs = np.concatenate([np.kron(eye, np.cos(ang)), np.kron(eye, np.sin(ang))], axis=0) * scale
    return jnp.asarray(cs, dtype=BF16)


def _strict_upper(n):
    return jnp.asarray(np.triu(np.ones((n, n)), k=1), dtype=BF16)


def _strict_lower(n):
    return jnp.asarray(np.tril(np.ones((n, n)), k=-1), dtype=BF16)


def _kv_kernel(mem_ref, g_ref, w_ref, o_ref):
    h = _rms(mem_ref[...], g_ref[...]).astype(BF16)
    o_ref[...] = _dot(h, w_ref[...].astype(BF16)).astype(BF16)


def _kv_proj(mem2d, g, w_kv):
    rows = mem2d.shape[0]
    cols = w_kv.shape[1]
    cb = 512
    return pl.pallas_call(
        _kv_kernel,
        grid=(cols // cb,),
        in_specs=[
            pl.BlockSpec((rows, D_MODEL), lambda j: (0, 0)),
            pl.BlockSpec((1, D_MODEL), lambda j: (0, 0)),
            pl.BlockSpec((D_MODEL, cb), lambda j: (0, j)),
        ],
        out_specs=pl.BlockSpec((rows, cb), lambda j: (0, j)),
        out_shape=jax.ShapeDtypeStruct((rows, cols), BF16),
        compiler_params=pltpu.CompilerParams(vmem_limit_bytes=VMEM_LIMIT),
        name="kv_proj",
    )(mem2d, g, w_kv)


def _mixer_in_kernel(x_ref, xp_ref, xn_ref, g_ref, w_ref, cw_ref, cb_ref, hg_ref, gm_ref,
                     conv_ref, uf_ref):
    i = pl.program_id(1)
    n_i = pl.num_programs(1)
    rows = x_ref.shape[0]
    g = g_ref[...]
    h = _rms(x_ref[...], g).astype(BF16)
    u = _dot(h, w_ref[...])
    b_gate = u[:, :CONV_CH]
    cv = u[:, CONV_CH:2 * CONV_CH] * u[:, 2 * CONV_CH:3 * CONV_CH]
    uf_ref[...] = u[:, 3 * CONV_CH:].astype(BF16)

    hh = jnp.concatenate([_rms(xp_ref[...], g), _rms(xn_ref[...], g)], axis=0).astype(BF16)
    uh = _dot(hh, w_ref[:, CONV_CH:3 * CONV_CH])
    cvh = uh[:, :CONV_CH] * uh[:, CONV_CH:]
    cv_prev = cvh[7:8, :] * jnp.where(i == 0, 0.0, 1.0)
    cv_next = cvh[8:9, :] * jnp.where(i == n_i - 1, 0.0, 1.0)

    row = lax.broadcasted_iota(I32, cv.shape, 0)
    cv_up = jnp.where(row == 0, cv_prev, pltpu.roll(cv, 1, 0))
    cv_dn = jnp.where(row == rows - 1, cv_next, pltpu.roll(cv, rows - 1, 0))
    z = cw_ref[0:1, :] * cv_up + cw_ref[1:2, :] * cv + cw_ref[2:3, :] * cv_dn + cb_ref[...]
    y = b_gate * z
    ms = _dot((y * y).astype(BF16), gm_ref[...])
    conv_ref[...] = (y * lax.rsqrt(ms + EPS) * hg_ref[...]).astype(BF16)


def _mixer_in(x2d, g, w_in, conv_w, conv_b, hg_conv, gm, batch, seq):
    n_i = seq // MIX_ROWS
    t = x2d.shape[0]
    r8 = MIX_ROWS // 8
    last8 = t // 8 - 1
    return pl.pallas_call(
        _mixer_in_kernel,
        grid=(batch, n_i),
        in_specs=[
            pl.BlockSpec((MIX_ROWS, D_MODEL), lambda b, i: (b * n_i + i, 0)),
            pl.BlockSpec((8, D_MODEL), lambda b, i: (jnp.maximum((b * n_i + i) * r8 - 1, 0), 0)),
            pl.BlockSpec((8, D_MODEL), lambda b, i: (jnp.minimum((b * n_i + i + 1) * r8, last8), 0)),
            pl.BlockSpec((1, D_MODEL), lambda b, i: (0, 0)),
            pl.BlockSpec((D_MODEL, IN_COLS), lambda b, i: (0, 0)),
            pl.BlockSpec((3, CONV_CH), lambda b, i: (0, 0)),
            pl.BlockSpec((1, CONV_CH), lambda b, i: (0, 0)),
            pl.BlockSpec((1, CONV_CH), lambda b, i: (0, 0)),
            pl.BlockSpec((CONV_CH, CONV_CH), lambda b, i: (0, 0)),
        ],
        out_specs=[
            pl.BlockSpec((MIX_ROWS, CONV_CH), lambda b, i: (b * n_i + i, 0)),
            pl.BlockSpec((MIX_ROWS, FFT_CH), lambda b, i: (b * n_i + i, 0)),
        ],
        out_shape=[
            jax.ShapeDtypeStruct((t, CONV_CH), BF16),
            jax.ShapeDtypeStruct((t, FFT_CH), BF16),
        ],
        compiler_params=pltpu.CompilerParams(vmem_limit_bytes=VMEM_LIMIT),
        name="mixer_in",
    )(x2d, x2d, x2d, g, w_in, conv_w, conv_b, hg_conv, gm)


_S1_ROWS = 16
_S1_LANES = 128


def _lincomb(terms):
    acc = None
    for coef, val in terms:
        if abs(coef) < 1e-12:
            continue
        if abs(coef - 1.0) < 1e-12:
            term, neg = val, False
        elif abs(coef + 1.0) < 1e-12:
            term, neg = val, True
        else:
            term, neg = coef * val, False
        if acc is None:
            acc = -term if neg else term
        else:
            acc = acc - term if neg else acc + term
    return acc


def _fft_stage1(x_ref, a_ref):
    half = FFT_N1 // 2
    cos = [[math.cos(2 * math.pi * ((k * j) % FFT_N1) / FFT_N1) for j in range(FFT_N1)] for k in range(FFT_N1)]
    sin = [[math.sin(2 * math.pi * ((k * j) % FFT_N1) / FFT_N1) for j in range(FFT_N1)] for k in range(FFT_N1)]

    def body(r, carry):
        r0 = pl.multiple_of(r * _S1_ROWS, _S1_ROWS)
        rows_re = pl.ds(r0, _S1_ROWS)
        rows_im = pl.ds(r0 + FFT_N2, _S1_ROWS)
        for lc in range(0, FFT_CH, _S1_LANES):
            lanes = slice(lc, lc + _S1_LANES)
            xs = [x_ref[j, rows_re, lanes].astype(F32) for j in range(FFT_N1)]
            ev = [None] + [xs[j] + xs[FFT_N1 - j] for j in range(1, half)]
            od = [None] + [xs[j] - xs[FFT_N1 - j] for j in range(1, half)]
            for k in range(half + 1):
                re = _lincomb([(1.0, xs[0]), (cos[k][half], xs[half])]
                              + [(cos[k][j], ev[j]) for j in range(1, half)])
                a_ref[k, rows_re, lanes] = re.astype(BF16)
                if k in (0, half):
                    zero = jnp.zeros_like(re).astype(BF16)
                    a_ref[k, rows_im, lanes] = zero
                else:
                    im = _lincomb([(-sin[k][j], od[j]) for j in range(1, half)])
                    a_ref[k, rows_im, lanes] = im.astype(BF16)
                    a_ref[FFT_N1 - k, rows_re, lanes] = re.astype(BF16)
                    a_ref[FFT_N1 - k, rows_im, lanes] = (-im).astype(BF16)
        return carry

    lax.fori_loop(0, FFT_N2 // _S1_ROWS, body, 0)


def _fourier_kernel(x_ref, m2_ref, cs_ref, gm_ref, hg_ref, o_ref, a_ref, y_ref):
    j = pl.program_id(1)

    @pl.when(j == 0)
    def _():
        _fft_stage1(x_ref, a_ref)

    for kk in range(FFT_K1_PER_STEP):
        k1 = j * FFT_K1_PER_STEP + kk
        ri = _dot(m2_ref[kk], a_ref[k1])
        re = ri[:FFT_N2].astype(BF16)
        im = ri[FFT_N2:].astype(BF16)
        y = _dot(re, cs_ref[:FFT_CH, :]) + _dot(im, cs_ref[FFT_CH:, :])
        ms = _dot((y * y).astype(BF16), gm_ref[...])
        yn = y * lax.rsqrt(ms + EPS) * hg_ref[...]
        for c in range(FFT_CH // LANES):
            y_ref[c, pl.ds(k1, FFT_N2, stride=FFT_N1), :] = yn[:, c * LANES:(c + 1) * LANES]

    @pl.when(j == pl.num_programs(1) - 1)
    def _():
        for c in range(FFT_CH // LANES):
            o_ref[:, c * LANES:(c + 1) * LANES] = y_ref[c].astype(BF16)


def _fourier(uf, m2, cs, gm, hg_fft, batch, seq):
    assert seq == FFT_N1 * FFT_N2
    x4 = uf.reshape(batch, FFT_N1, FFT_N2, FFT_CH)
    out = pl.pallas_call(
        _fourier_kernel,
        grid=(batch, FFT_N1 // FFT_K1_PER_STEP),
        in_specs=[
            pl.BlockSpec((None, FFT_N1, FFT_N2, FFT_CH), lambda b, j: (b, 0, 0, 0)),
            pl.BlockSpec((FFT_K1_PER_STEP, 2 * FFT_N2, 2 * FFT_N2), lambda b, j: (j, 0, 0)),
            pl.BlockSpec((2 * FFT_CH, FFT_CH), lambda b, j: (0, 0)),
            pl.BlockSpec((FFT_CH, FFT_CH), lambda b, j: (0, 0)),
            pl.BlockSpec((1, FFT_CH), lambda b, j: (0, 0)),
        ],
        out_specs=pl.BlockSpec((seq, FFT_CH), lambda b, j: (b, 0)),
        out_shape=jax.ShapeDtypeStruct((batch * seq, FFT_CH), BF16),
        scratch_shapes=[
            pltpu.VMEM((FFT_N1, 2 * FFT_N2, FFT_CH), BF16),
            pltpu.VMEM((FFT_CH // LANES, seq, LANES), F32),
        ],
        compiler_params=pltpu.CompilerParams(
            dimension_semantics=("arbitrary", "arbitrary"), vmem_limit_bytes=VMEM_LIMIT),
        name="fourier",
    )(x4, m2, cs, gm, hg_fft)
    return out


def _first_index_of_max(vals, vmax, row):
    return jnp.min(jnp.where(vals == vmax, row, vals.shape[0]), axis=0, keepdims=True)


def _route(lg):
    cols = lg.shape[1]
    row8 = lax.broadcasted_iota(I32, (EXPERTS_PER_GROUP, cols), 0)
    gl = lg[0:8, :]
    gmax = jnp.max(gl, axis=0, keepdims=True)
    g_w = 1.0 / jnp.sum(jnp.exp(gl - gmax), axis=0, keepdims=True)
    g_idx = _first_index_of_max(gl, gmax, row8)

    el = lg[8:16, :]
    for g in range(1, N_GROUPS):
        el = jnp.where(g_idx == g, lg[8 + 8 * g:16 + 8 * g, :], el)
    emax = jnp.max(el, axis=0, keepdims=True)
    ee = jnp.exp(el - emax)
    e_prob = ee / jnp.sum(ee, axis=0, keepdims=True)
    p1 = jnp.max(e_prob, axis=0, keepdims=True)
    i1 = _first_index_of_max(e_prob, p1, row8)
    rest = jnp.where(row8 == i1, -1.0, e_prob)
    p2 = jnp.max(rest, axis=0, keepdims=True)
    i2 = _first_index_of_max(rest, p2, row8)
    denom = p1 + p2
    e1 = g_idx * EXPERTS_PER_GROUP + i1
    e2 = g_idx * EXPERTS_PER_GROUP + i2
    return e1, e2, g_w * p1 / denom, g_w * p2 / denom


def _trunk_kernel(x_ref, conv_ref, fft_ref, wout_ref, gxa_ref, wq_ref, k_ref, v_ref, wo_ref,
                  gffn_ref, wr_ref, br_ref, tri_ref, ltri_ref,
                  x2_ref, xs_ref, pos_ref, gate_ref, cnt_ref):
    x1 = x_ref[...] + _dot(conv_ref[...], wout_ref[:CONV_CH, :]) + _dot(fft_ref[...], wout_ref[CONV_CH:, :])
    h2 = _rms(x1, gxa_ref[...]).astype(BF16)
    q = _dot(h2, wq_ref[...]).astype(BF16)
    outs = []
    for hd in range(XA_HEADS):
        cols = slice(hd * XA_HEAD_DIM, (hd + 1) * XA_HEAD_DIM)
        s = _dot_nt(q[:, cols], k_ref[:, cols]) * (XA_HEAD_DIM ** -0.5)
        s = s - jnp.max(s, axis=-1, keepdims=True)
        p = jnp.exp(s)
        p = p / jnp.sum(p, axis=-1, keepdims=True)
        outs.append(_dot(p.astype(BF16), v_ref[:, cols]).astype(BF16))
    o = jnp.concatenate(outs, axis=-1)
    x2 = x1 + _dot(o, wo_ref[...])
    x2_ref[...] = x2
    h3 = _rms(x2, gffn_ref[...]).astype(BF16)
    lg = _dot_nt(wr_ref[...], h3) + br_ref[...]

    e1, e2, gate1, gate2 = _route(lg)
    n_tok = lg.shape[1]
    row32 = lax.broadcasted_iota(I32, (N_EXPERTS, n_tok), 0)
    hit1 = row32 == e1
    hit2 = row32 == e2
    onehot = jnp.where(hit1 | hit2, 1.0, 0.0)
    before = _dot(onehot.astype(BF16), tri_ref[...])
    cnt = jnp.sum(onehot, axis=1, keepdims=True).astype(I32)
    piece = jnp.left_shift(jnp.right_shift(cnt + (GRANULE - 1), 3), 3)
    piece_b = jnp.broadcast_to(piece.astype(F32), (N_EXPERTS, LANES)).astype(BF16)
    start = _dot(ltri_ref[...], piece_b)[:, 0:1]
    slot = before + start
    pos1 = jnp.sum(jnp.where(hit1, slot, 0.0), axis=0, keepdims=True).astype(I32)
    pos2 = jnp.sum(jnp.where(hit2, slot, 0.0), axis=0, keepdims=True).astype(I32)

    r = lax.broadcasted_iota(I32, (LOCAL_ROWS, n_tok), 0)
    perm = jnp.where((r == pos1) | (r == pos2), 1.0, 0.0).astype(BF16)
    xs_ref[...] = _pack_halves(_dot(perm, h3[:, :HALF]), _dot(perm, h3[:, HALF:]))

    pos_ref[0:1, :] = pos1
    pos_ref[1:2, :] = pos2
    pos_ref[2:8, :] = jnp.zeros((6, n_tok), I32)
    gate_ref[0:1, :] = gate1
    gate_ref[1:2, :] = gate2
    gate_ref[2:8, :] = jnp.zeros((6, n_tok), F32)
    cnt_ref[...] = jnp.broadcast_to(cnt, (N_EXPERTS, LANES))


def _trunk(x2d, conv_n, fft_n, w_out, g_xa, w_q, kv, w_o, g_ffn, w_r_t, b_r, tri, ltri, seq):
    t = x2d.shape[0]
    n_tiles = t // TRUNK_ROWS
    n_per_batch = seq // TRUNK_ROWS
    const = lambda i: (0, 0)
    return pl.pallas_call(
        _trunk_kernel,
        grid=(n_tiles,),
        in_specs=[
            pl.BlockSpec((TRUNK_ROWS, D_MODEL), lambda i: (i, 0)),
            pl.BlockSpec((TRUNK_ROWS, CONV_CH), lambda i: (i, 0)),
            pl.BlockSpec((TRUNK_ROWS, FFT_CH), lambda i: (i, 0)),
            pl.BlockSpec((D_MODEL, D_MODEL), const),
            pl.BlockSpec((1, D_MODEL), const),
            pl.BlockSpec((D_MODEL, D_MODEL), const),
            pl.BlockSpec((MEM_LEN, D_MODEL), lambda i: (i // n_per_batch, 0)),
            pl.BlockSpec((MEM_LEN, D_MODEL), lambda i: (i // n_per_batch, 1)),
            pl.BlockSpec((D_MODEL, D_MODEL), const),
            pl.BlockSpec((1, D_MODEL), const),
            pl.BlockSpec((ROUTER_ROWS, D_MODEL), const),
            pl.BlockSpec((ROUTER_ROWS, 1), const),
            pl.BlockSpec((TRUNK_ROWS, TRUNK_ROWS), const),
            pl.BlockSpec((N_EXPERTS, N_EXPERTS), const),
        ],
        out_specs=[
            pl.BlockSpec((TRUNK_ROWS, D_MODEL), lambda i: (i, 0)),
            pl.BlockSpec((LOCAL_ROWS, HALF), lambda i: (i, 0)),
            pl.BlockSpec((8, TRUNK_ROWS), lambda i: (0, i)),
            pl.BlockSpec((8, TRUNK_ROWS), lambda i: (0, i)),
            pl.BlockSpec((None, N_EXPERTS, LANES), lambda i: (i, 0, 0)),
        ],
        out_shape=[
            jax.ShapeDtypeStruct((t, D_MODEL), F32),
            jax.ShapeDtypeStruct((n_tiles * LOCAL_ROWS, HALF), I32),
            jax.ShapeDtypeStruct((8, t), I32),
            jax.ShapeDtypeStruct((8, t), F32),
            jax.ShapeDtypeStruct((n_tiles, N_EXPERTS, LANES), I32),
        ],
        compiler_params=pltpu.CompilerParams(vmem_limit_bytes=VMEM_LIMIT),
        name="trunk",
    )(x2d, conv_n, fft_n, w_out, g_xa, w_q, kv, kv, w_o, g_ffn, w_r_t, b_r, tri, ltri)


def _granule(ref, row):
    return ref.at[pl.ds(pl.multiple_of(row, GRANULE), GRANULE), :]


def _regroup_kernel(glob_ref, ngran_ref, zrow_ref, zcnt_ref, zblk_ref, src_ref, zeros_ref, out_ref,
                    sem, zsem, *, to_global):
    i = pl.program_id(0)

    def copy(q):
        loc = i * LOCAL_ROWS + q * GRANULE
        glob = glob_ref[i * LOCAL_GRANULES + q]
        if to_global:
            return pltpu.make_async_copy(_granule(src_ref, loc), _granule(out_ref, glob), sem)
        return pltpu.make_async_copy(_granule(src_ref, glob), _granule(out_ref, loc), sem)

    def zero_copy(j, g):
        return pltpu.make_async_copy(zeros_ref.at[pl.ds(0, GRANULE), :],
                                     _granule(out_ref, zrow_ref[j] + g * GRANULE), zsem)

    def zero_block_copy(b):
        return pltpu.make_async_copy(
            zeros_ref, out_ref.at[pl.ds(pl.multiple_of(b * EXPERT_ROWS, EXPERT_ROWS), EXPERT_ROWS), :], zsem)

    def for_each_zero(fn):
        def per_list(j, carry):
            lax.fori_loop(0, zcnt_ref[j], lambda g, c: (fn(zero_copy(j, g)), c)[1], 0)
            return carry
        lax.fori_loop(0, zrow_ref.shape[0], per_list, 0)
        if to_global:
            n_blocks = out_ref.shape[0] // EXPERT_ROWS
            lax.fori_loop(zblk_ref[0], n_blocks, lambda b, c: (fn(zero_block_copy(b)), c)[1], 0)

    @pl.when(i == 0)
    def _():
        for_each_zero(lambda cp: cp.start())

    n = ngran_ref[i]
    lax.fori_loop(0, n, lambda q, c: (copy(q).start(), c)[1], 0)
    lax.fori_loop(0, n, lambda q, c: (copy(q).wait(), c)[1], 0)

    @pl.when(i == pl.num_programs(0) - 1)
    def _():
        for_each_zero(lambda cp: cp.wait())


def _regroup(glob, ngran, zrow, zcnt, zblk, src, zeros, out_rows, to_global, name):
    n_tiles = ngran.shape[0]
    return pl.pallas_call(
        functools.partial(_regroup_kernel, to_global=to_global),
        grid_spec=pltpu.PrefetchScalarGridSpec(
            num_scalar_prefetch=5,
            grid=(n_tiles,),
            in_specs=[pl.BlockSpec(memory_space=pl.ANY), pl.BlockSpec(memory_space=pl.ANY)],
            out_specs=pl.BlockSpec(memory_space=pl.ANY),
            scratch_shapes=[pltpu.SemaphoreType.DMA(()), pltpu.SemaphoreType.DMA(())],
        ),
        out_shape=jax.ShapeDtypeStruct((out_rows, HALF), I32),
        compiler_params=pltpu.CompilerParams(dimension_semantics=("arbitrary",)),
        name=name,
    )(glob, ngran, zrow, zcnt, zblk, src, zeros)


def _experts_kernel(blk_e_ref, nblk_ref, xs_ref, wg_ref, wu_ref, wd_ref, ys_ref):
    del blk_e_ref
    used = pl.program_id(0) < nblk_ref[0]

    @pl.when(used)
    def _():
        xl, xr = _unpack_halves(xs_ref[...])
        a = _dot(xl, wg_ref[:HALF, :].astype(BF16)) + _dot(xr, wg_ref[HALF:, :].astype(BF16))
        b = _dot(xl, wu_ref[:HALF, :].astype(BF16)) + _dot(xr, wu_ref[HALF:, :].astype(BF16))
        hmid = (a * jax.nn.sigmoid(a) * b).astype(BF16)
        y = _dot(hmid, wd_ref[...].astype(BF16))
        ys_ref[...] = _pack_halves(y[:, :HALF].astype(BF16).astype(F32), y[:, HALF:].astype(BF16).astype(F32))

    @pl.when(jnp.logical_not(used))
    def _():
        ys_ref[...] = jnp.zeros_like(ys_ref)


def _experts(blk_e, nblk, xs, w_gate, w_up, w_down):
    p = xs.shape[0]
    row_blk = lambda i, be, nb: (jnp.minimum(i, nb[0] - 1), 0)
    w_blk = lambda i, be, nb: (be[i], 0, 0)
    return pl.pallas_call(
        _experts_kernel,
        grid_spec=pltpu.PrefetchScalarGridSpec(
            num_scalar_prefetch=2,
            grid=(p // EXPERT_ROWS,),
            in_specs=[
                pl.BlockSpec((EXPERT_ROWS, HALF), row_blk),
                pl.BlockSpec((None, D_MODEL, D_EXPERT), w_blk),
                pl.BlockSpec((None, D_MODEL, D_EXPERT), w_blk),
                pl.BlockSpec((None, D_EXPERT, D_MODEL), w_blk),
            ],
            out_specs=pl.BlockSpec((EXPERT_ROWS, HALF), lambda i, be, nb: (i, 0)),
        ),
        out_shape=jax.ShapeDtypeStruct((p, HALF), I32),
        compiler_params=pltpu.CompilerParams(
            dimension_semantics=("arbitrary",), vmem_limit_bytes=VMEM_LIMIT),
        name="experts",
    )(blk_e, nblk, xs, w_gate, w_up, w_down)


def _combine_kernel(x2_ref, pos_ref, gate_ref, g_ref, ys_ref, o_ref):
    n_tok = x2_ref.shape[0]
    pos = pos_ref[...]
    gates = gate_ref[...]
    col = lax.broadcasted_iota(I32, (n_tok, LOCAL_ROWS), 1)
    w = (jnp.where(col == pos[:, 0:1], gates[:, 0:1], 0.0)
         + jnp.where(col == pos[:, 1:2], gates[:, 1:2], 0.0)).astype(BF16)
    yl, yr = _unpack_halves(ys_ref[...])
    moe = jnp.concatenate([_dot(w, yl), _dot(w, yr)], axis=-1)
    o_ref[...] = _rms(x2_ref[...] + moe, g_ref[...])


def _combine(x2, pos_tk, gates_tk, g_final, ys_loc):
    t = x2.shape[0]
    return pl.pallas_call(
        _combine_kernel,
        grid=(t // TRUNK_ROWS,),
        in_specs=[
            pl.BlockSpec((TRUNK_ROWS, D_MODEL), lambda i: (i, 0)),
            pl.BlockSpec((TRUNK_ROWS, TOP_K), lambda i: (i, 0)),
            pl.BlockSpec((TRUNK_ROWS, TOP_K), lambda i: (i, 0)),
            pl.BlockSpec((1, D_MODEL), lambda i: (0, 0)),
            pl.BlockSpec((LOCAL_ROWS, HALF), lambda i: (i, 0)),
        ],
        out_specs=pl.BlockSpec((TRUNK_ROWS, D_MODEL), lambda i: (i, 0)),
        out_shape=jax.ShapeDtypeStruct((t, D_MODEL), F32),
        compiler_params=pltpu.CompilerParams(vmem_limit_bytes=VMEM_LIMIT),
        name="combine",
    )(x2, pos_tk, gates_tk, g_final, ys_loc)


def _router_params(w_rg, b_rg, w_re, b_re):
    w = jnp.zeros((ROUTER_ROWS, D_MODEL), F32)
    w = w.at[0:N_GROUPS].set(w_rg.T).at[8:8 + N_EXPERTS].set(w_re.T)
    b = jnp.zeros((ROUTER_ROWS,), F32)
    b = b.at[0:N_GROUPS].set(b_rg).at[N_GROUPS:8].set(NEG_BIG).at[8:8 + N_EXPERTS].set(b_re)
    return w.astype(BF16), b.reshape(ROUTER_ROWS, 1)


def _plan(cnt, n_global_rows):
    n_tiles = cnt.shape[0]
    piece = (cnt + GRANULE - 1) // GRANULE * GRANULE
    lend = jnp.cumsum(piece, axis=1)
    lstart = lend - piece
    tot = jnp.sum(piece, axis=0)
    padded = (tot + EXPERT_ROWS - 1) // EXPERT_ROWS * EXPERT_ROWS
    pend = jnp.cumsum(padded)
    pstart = pend - padded
    gstart = pstart[None, :] + jnp.cumsum(piece, axis=0) - piece

    q_row = jnp.arange(LOCAL_GRANULES, dtype=I32) * GRANULE
    e_of_q = jnp.sum((lend[:, None, :] <= q_row[None, :, None]).astype(I32), axis=2)
    e_of_q = jnp.minimum(e_of_q, N_EXPERTS - 1)
    onehot = e_of_q[:, :, None] == jnp.arange(N_EXPERTS, dtype=I32)
    glob = jnp.sum(jnp.where(onehot, (gstart - lstart)[:, None, :], 0), axis=2) + q_row[None, :]
    used = lend[:, -1]
    ngran = used // GRANULE

    n_blk = n_global_rows // EXPERT_ROWS
    blk_start = jnp.arange(n_blk, dtype=I32) * EXPERT_ROWS
    blk_e = jnp.minimum(jnp.sum((pend[None, :] <= blk_start[:, None]).astype(I32), axis=1), N_EXPERTS - 1)
    nblk = pend[-1:] // EXPERT_ROWS
    as_i32 = lambda v: v.astype(I32)
    return dict(
        glob=as_i32(glob.reshape(-1)), ngran=as_i32(ngran), blk_e=as_i32(blk_e), nblk=as_i32(nblk),
        gz_row=as_i32(pstart + tot), gz_cnt=as_i32((padded - tot) // GRANULE),
        lz_row=as_i32(jnp.arange(n_tiles, dtype=I32) * LOCAL_ROWS + used),
        lz_cnt=as_i32((LOCAL_ROWS - used) // GRANULE),
    )


def _layer(x2d, mem2d, batch, seq, norm_mix_g, w_in, conv_w, conv_b, head_norm_g, w_out,
           norm_xa_g, norm_mem_g, w_q, w_kv, w_o, norm_ffn_g, w_rg, b_rg, w_re, b_re,
           w_gate, w_up, w_down, out_norm_g):
    t = x2d.shape[0]
    n_tiles = t // TRUNK_ROWS
    row = lambda v: v.reshape(1, -1)
    hg = head_norm_g.reshape(-1)
    gm = _group_mean_matrix()

    kv = _kv_proj(mem2d, row(norm_mem_g), w_kv)
    conv_n, uf = _mixer_in(x2d, row(norm_mix_g), w_in.astype(BF16), conv_w, row(conv_b),
                           row(hg[:CONV_CH]), gm, batch, seq)
    fft_n = _fourier(uf.reshape(batch, seq, FFT_CH), _fft_stage2_matrices(seq), _fft_channel_matrix(seq),
                     gm, row(hg[CONV_CH:]), batch, seq)
    w_r_t, b_r = _router_params(w_rg, b_rg, w_re, b_re)
    x2, xs_loc, pos, gates, cnt = _trunk(
        x2d, conv_n, fft_n, w_out.astype(BF16), row(norm_xa_g), w_q.astype(BF16), kv, w_o.astype(BF16),
        row(norm_ffn_g), w_r_t, b_r, _strict_upper(TRUNK_ROWS), _strict_lower(N_EXPERTS), seq)

    max_rows = n_tiles * LOCAL_ROWS + N_EXPERTS * (EXPERT_ROWS - GRANULE)
    n_global_rows = -(-max_rows // EXPERT_ROWS) * EXPERT_ROWS
    plan = _plan(cnt[:, :, 0], n_global_rows)
    zeros = jnp.zeros((EXPERT_ROWS, HALF), I32)

    xs = _regroup(plan["glob"], plan["ngran"], plan["gz_row"], plan["gz_cnt"], plan["nblk"],
                  xs_loc, zeros, n_global_rows, True, "regroup_in")
    ys = _experts(plan["blk_e"], plan["nblk"], xs, w_gate, w_up, w_down)
    ys_loc = _regroup(plan["glob"], plan["ngran"], plan["lz_row"], plan["lz_cnt"], plan["nblk"],
                      ys, zeros, n_tiles * LOCAL_ROWS, False, "regroup_out")
    return _combine(x2, pos[0:TOP_K].T, gates[0:TOP_K].T, row(out_norm_g), ys_loc)


def kernel(x, mem, norm_mix_g, w_in, conv_w, conv_b, head_norm_g, w_out, norm_xa_g, norm_mem_g, w_q, w_kv,
           w_o, norm_ffn_g, w_route_group, b_route_group, w_route_expert, b_route_expert, w_gate, w_up,
           w_down, final_norm_g):
    batch, seq, _ = x.shape
    depth = norm_mix_g.shape[0]
    assert depth == 1, "the final norm is fused into the last layer's combine kernel"
    x2d = x.reshape(batch * seq, D_MODEL)
    mem2d = mem.reshape(batch * MEM_LEN, D_MODEL)
    l = 0
    out = _layer(x2d, mem2d, batch, seq, norm_mix_g[l], w_in[l], conv_w[l], conv_b[l], head_norm_g[l],
                 w_out[l], norm_xa_g[l], norm_mem_g[l], w_q[l], w_kv[l], w_o[l], norm_ffn_g[l],
                 w_route_group[l], b_route_group[l], w_route_expert[l], b_route_expert[l],
                 w_gate[l], w_up[l], w_down[l], final_norm_g)
    return out.reshape(batch, seq, D_MODEL)
```

```python
import math

import numpy as np
import jax
import jax.numpy as jnp
from jax import lax
from jax.experimental import pallas as pl
from jax.experimental.pallas import tpu as pltpu

F32 = jnp.float32
BF16 = jnp.bfloat16
I32 = jnp.int32

D_MODEL = 1024
HALF = D_MODEL // 2
HEAD_DIM = 64
CONV_CH = 512
FFT_CH = 512
IN_COLS = 3 * CONV_CH + FFT_CH
MEM_LEN = 256
XA_HEADS = 4
XA_HEAD_DIM = D_MODEL // XA_HEADS
N_GROUPS = 4
EXPERTS_PER_GROUP = 8
N_EXPERTS = 32
TOP_K = 2
D_EXPERT = 512
EPS = 1e-6

FFT_N1 = 16
FFT_N2 = 256
FFT_K1_PER_STEP = 4

LANES = 128
GRANULE = 8
MIX_ROWS = 512
TRUNK_ROWS = 512
LOCAL_ROWS = TOP_K * TRUNK_ROWS + N_EXPERTS * GRANULE
LOCAL_GRANULES = LOCAL_ROWS // GRANULE
EXPERT_ROWS = 256
ROUTER_ROWS = 128
NEG_BIG = -1e30
HI16 = -65536

VMEM_LIMIT = 56 * 1024 * 1024


def _rms(x, g):
    return x * lax.rsqrt(jnp.mean(x * x, axis=-1, keepdims=True) + EPS) * g


def _dot(a, b):
    return jnp.dot(a, b, preferred_element_type=F32)


def _dot_nt(a, b):
    return lax.dot_general(a, b, (((1,), (1,)), ((), ())), preferred_element_type=F32)


def _pack_halves(left_f32, right_f32):
    lb = lax.bitcast_convert_type(left_f32, I32)
    rb = lax.shift_right_logical(lax.bitcast_convert_type(right_f32, I32), jnp.int32(16))
    return lb | rb


def _unpack_halves(packed_i32):
    left = lax.bitcast_convert_type(packed_i32 & jnp.int32(HI16), F32)
    right = lax.bitcast_convert_type(lax.shift_left(packed_i32, jnp.int32(16)), F32)
    return left.astype(BF16), right.astype(BF16)


def _group_mean_matrix():
    g = np.kron(np.eye(FFT_CH // HEAD_DIM), np.full((HEAD_DIM, HEAD_DIM), 1.0 / HEAD_DIM))
    return jnp.asarray(g, dtype=BF16)


def _fft_stage2_matrices(seq):
    k1 = np.arange(FFT_N1)[:, None, None]
    k2 = np.arange(FFT_N2)[None, :, None]
    s2 = np.arange(FFT_N2)[None, None, :]
    ang = 2.0 * np.pi * ((s2 * (k1 + FFT_N1 * k2)) % seq) / seq
    c, s = np.cos(ang), np.sin(ang)
    top = np.concatenate([c, s], axis=2)
    bot = np.concatenate([-s, c], axis=2)
    return jnp.asarray(np.concatenate([top, bot], axis=1), dtype=BF16)


def _fft_channel_matrix(seq):
    c = np.arange(HEAD_DIM)
    ang = 2.0 * np.pi * ((c[:, None] * c[None, :]) % HEAD_DIM) / HEAD_DIM
    scale = 1.0 / math.sqrt(seq * HEAD_DIM)
    eye = np.eye(FFT_CH // HEAD_DIM)
    cs = np.concatenate([np.kron(eye, np.cos(ang)), np.kron(eye, np.sin(ang))], axis=0) * scale
    return jnp.asarray(cs, dtype=BF16)


def _strict_upper(n):
    return jnp.asarray(np.triu(np.ones((n, n)), k=1), dtype=BF16)


def _strict_lower(n):
    return jnp.asarray(np.tril(np.ones((n, n)), k=-1), dtype=BF16)


def _kv_kernel(mem_ref, g_ref, w_ref, o_ref):
    h = _rms(mem_ref[...], g_ref[...]).astype(BF16)
    o_ref[...] = _dot(h, w_ref[...].astype(BF16)).astype(BF16)


def _kv_proj(mem2d, g, w_kv):
    rows = mem2d.shape[0]
    cols = w_kv.shape[1]
    cb = 512
    return pl.pallas_call(
        _kv_kernel,
        grid=(cols // cb,),
        in_specs=[
            pl.BlockSpec((rows, D_MODEL), lambda j: (0, 0)),
            pl.BlockSpec((1, D_MODEL), lambda j: (0, 0)),
            pl.BlockSpec((D_MODEL, cb), lambda j: (0, j)),
        ],
        out_specs=pl.BlockSpec((rows, cb), lambda j: (0, j)),
        out_shape=jax.ShapeDtypeStruct((rows, cols), BF16),
        compiler_params=pltpu.CompilerParams(vmem_limit_bytes=VMEM_LIMIT),
        name="kv_proj",
    )(mem2d, g, w_kv)


def _mixer_in_kernel(x_ref, xp_ref, xn_ref, g_ref, w_ref, cw_ref, cb_ref, hg_ref, gm_ref,
                     conv_ref, uf_ref):
    i = pl.program_id(1)
    n_i = pl.num_programs(1)
    rows = x_ref.shape[0]
    g = g_ref[...]
    h = _rms(x_ref[...], g).astype(BF16)
    u = _dot(h, w_ref[...])
    b_gate = u[:, :CONV_CH]
    cv = u[:, CONV_CH:2 * CONV_CH] * u[:, 2 * CONV_CH:3 * CONV_CH]
    uf_ref[...] = u[:, 3 * CONV_CH:].astype(BF16)

    hh = jnp.concatenate([_rms(xp_ref[...], g), _rms(xn_ref[...], g)], axis=0).astype(BF16)
    uh = _dot(hh, w_ref[:, CONV_CH:3 * CONV_CH])
    cvh = uh[:, :CONV_CH] * uh[:, CONV_CH:]
    cv_prev = cvh[7:8, :] * jnp.where(i == 0, 0.0, 1.0)
    cv_next = cvh[8:9, :] * jnp.where(i == n_i - 1, 0.0, 1.0)

    row = lax.broadcasted_iota(I32, cv.shape, 0)
    cv_up = jnp.where(row == 0, cv_prev, pltpu.roll(cv, 1, 0))
    cv_dn = jnp.where(row == rows - 1, cv_next, pltpu.roll(cv, rows - 1, 0))
    z = cw_ref[0:1, :] * cv_up + cw_ref[1:2, :] * cv + cw_ref[2:3, :] * cv_dn + cb_ref[...]
    y = b_gate * z
    ms = _dot((y * y).astype(BF16), gm_ref[...])
    conv_ref[...] = (y * lax.rsqrt(ms + EPS) * hg_ref[...]).astype(BF16)


def _mixer_in(x2d, g, w_in, conv_w, conv_b, hg_conv, gm, batch, seq):
    n_i = seq // MIX_ROWS
    t = x2d.shape[0]
    r8 = MIX_ROWS // 8
    last8 = t // 8 - 1
    return pl.pallas_call(
        _mixer_in_kernel,
        grid=(batch, n_i),
        in_specs=[
            pl.BlockSpec((MIX_ROWS, D_MODEL), lambda b, i: (b * n_i + i, 0)),
            pl.BlockSpec((8, D_MODEL), lambda b, i: (jnp.maximum((b * n_i + i) * r8 - 1, 0), 0)),
            pl.BlockSpec((8, D_MODEL), lambda b, i: (jnp.minimum((b * n_i + i + 1) * r8, last8), 0)),
            pl.BlockSpec((1, D_MODEL), lambda b, i: (0, 0)),
            pl.BlockSpec((D_MODEL, IN_COLS), lambda b, i: (0, 0)),
            pl.BlockSpec((3, CONV_CH), lambda b, i: (0, 0)),
            pl.BlockSpec((1, CONV_CH), lambda b, i: (0, 0)),
            pl.BlockSpec((1, CONV_CH), lambda b, i: (0, 0)),
            pl.BlockSpec((CONV_CH, CONV_CH), lambda b, i: (0, 0)),
        ],
        out_specs=[
            pl.BlockSpec((MIX_ROWS, CONV_CH), lambda b, i: (b * n_i + i, 0)),
            pl.BlockSpec((MIX_ROWS, FFT_CH), lambda b, i: (b * n_i + i, 0)),
        ],
        out_shape=[
            jax.ShapeDtypeStruct((t, CONV_CH), BF16),
            jax.ShapeDtypeStruct((t, FFT_CH), BF16),
        ],
        compiler_params=pltpu.CompilerParams(vmem_limit_bytes=VMEM_LIMIT),
        name="mixer_in",
    )(x2d, x2d, x2d, g, w_in, conv_w, conv_b, hg_conv, gm)


_S1_ROWS = 16
_S1_LANES = 128


def _lincomb(terms):
    acc = None
    for coef, val in terms:
        if abs(coef) < 1e-12:
            continue
        if abs(coef - 1.0) < 1e-12:
            term, neg = val, False
        elif abs(coef + 1.0) < 1e-12:
            term, neg = val, True
        else:
            term, neg = coef * val, False
        if acc is None:
            acc = -term if neg else term
        else:
            acc = acc - term if neg else acc + term
    return acc


def _fft_stage1(x_ref, a_ref):
    half = FFT_N1 // 2
    cos = [[math.cos(2 * math.pi * ((k * j) % FFT_N1) / FFT_N1) for j in range(FFT_N1)] for k in range(FFT_N1)]
    sin = [[math.sin(2 * math.pi * ((k * j) % FFT_N1) / FFT_N1) for j in range(FFT_N1)] for k in range(FFT_N1)]

    def body(r, carry):
        r0 = pl.multiple_of(r * _S1_ROWS, _S1_ROWS)
        rows_re = pl.ds(r0, _S1_ROWS)
        rows_im = pl.ds(r0 + FFT_N2, _S1_ROWS)
        for lc in range(0, FFT_CH, _S1_LANES):
            lanes = slice(lc, lc + _S1_LANES)
            xs = [x_ref[j, rows_re, lanes].astype(F32) for j in range(FFT_N1)]
            ev = [None] + [xs[j] + xs[FFT_N1 - j] for j in range(1, half)]
            od = [None] + [xs[j] - xs[FFT_N1 - j] for j in range(1, half)]
            for k in range(half + 1):
                re = _lincomb([(1.0, xs[0]), (cos[k][half], xs[half])]
                              + [(cos[k][j], ev[j]) for j in range(1, half)])
                a_ref[k, rows_re, lanes] = re.astype(BF16)
                if k in (0, half):
                    zero = jnp.zeros_like(re).astype(BF16)
                    a_ref[k, rows_im, lanes] = zero
                else:
                    im = _lincomb([(-sin[k][j], od[j]) for j in range(1, half)])
                    a_ref[k, rows_im, lanes] = im.astype(BF16)
                    a_ref[FFT_N1 - k, rows_re, lanes] = re.astype(BF16)
                    a_ref[FFT_N1 - k, rows_im, lanes] = (-im).astype(BF16)
        return carry

    lax.fori_loop(0, FFT_N2 // _S1_ROWS, body, 0)


def _fourier_kernel(x_ref, m2_ref, cs_ref, gm_ref, hg_ref, o_ref, a_ref, y_ref):
    j = pl.program_id(1)

    @pl.when(j == 0)
    def _():
        _fft_stage1(x_ref, a_ref)

    for kk in range(FFT_K1_PER_STEP):
        k1 = j * FFT_K1_PER_STEP + kk
        ri = _dot(m2_ref[kk], a_ref[k1])
        re = ri[:FFT_N2].astype(BF16)
        im = ri[FFT_N2:].astype(BF16)
        y = _dot(re, cs_ref[:FFT_CH, :]) + _dot(im, cs_ref[FFT_CH:, :])
        ms = _dot((y * y).astype(BF16), gm_ref[...])
        yn = y * lax.rsqrt(ms + EPS) * hg_ref[...]
        for c in range(FFT_CH // LANES):
            y_ref[c, pl.ds(k1, FFT_N2, stride=FFT_N1), :] = yn[:, c * LANES:(c + 1) * LANES]

    @pl.when(j == pl.num_programs(1) - 1)
    def _():
        for c in range(FFT_CH // LANES):
            o_ref[:, c * LANES:(c + 1) * LANES] = y_ref[c].astype(BF16)


def _fourier(uf, m2, cs, gm, hg_fft, batch, seq):
    assert seq == FFT_N1 * FFT_N2
    x4 = uf.reshape(batch, FFT_N1, FFT_N2, FFT_CH)
    out = pl.pallas_call(
        _fourier_kernel,
        grid=(batch, FFT_N1 // FFT_K1_PER_STEP),
        in_specs=[
            pl.BlockSpec((None, FFT_N1, FFT_N2, FFT_CH), lambda b, j: (b, 0, 0, 0)),
            pl.BlockSpec((FFT_K1_PER_STEP, 2 * FFT_N2, 2 * FFT_N2), lambda b, j: (j, 0, 0)),
            pl.BlockSpec((2 * FFT_CH, FFT_CH), lambda b, j: (0, 0)),
            pl.BlockSpec((FFT_CH, FFT_CH), lambda b, j: (0, 0)),
            pl.BlockSpec((1, FFT_CH), lambda b, j: (0, 0)),
        ],
        out_specs=pl.BlockSpec((seq, FFT_CH), lambda b, j: (b, 0)),
        out_shape=jax.ShapeDtypeStruct((batch * seq, FFT_CH), BF16),
        scratch_shapes=[
            pltpu.VMEM((FFT_N1, 2 * FFT_N2, FFT_CH), BF16),
            pltpu.VMEM((FFT_CH // LANES, seq, LANES), F32),
        ],
        compiler_params=pltpu.CompilerParams(
            dimension_semantics=("arbitrary", "arbitrary"), vmem_limit_bytes=VMEM_LIMIT),
        name="fourier",
    )(x4, m2, cs, gm, hg_fft)
    return out


def _first_index_of_max(vals, vmax, row):
    return jnp.min(jnp.where(vals == vmax, row, vals.shape[0]), axis=0, keepdims=True)


def _route(lg):
    cols = lg.shape[1]
    row8 = lax.broadcasted_iota(I32, (EXPERTS_PER_GROUP, cols), 0)
    gl = lg[0:8, :]
    gmax = jnp.max(gl, axis=0, keepdims=True)
    g_w = 1.0 / jnp.sum(jnp.exp(gl - gmax), axis=0, keepdims=True)
    g_idx = _first_index_of_max(gl, gmax, row8)

    el = lg[8:16, :]
    for g in range(1, N_GROUPS):
        el = jnp.where(g_idx == g, lg[8 + 8 * g:16 + 8 * g, :], el)
    emax = jnp.max(el, axis=0, keepdims=True)
    ee = jnp.exp(el - emax)
    e_prob = ee / jnp.sum(ee, axis=0, keepdims=True)
    p1 = jnp.max(e_prob, axis=0, keepdims=True)
    i1 = _first_index_of_max(e_prob, p1, row8)
    rest = jnp.where(row8 == i1, -1.0, e_prob)
    p2 = jnp.max(rest, axis=0, keepdims=True)
    i2 = _first_index_of_max(rest, p2, row8)
    denom = p1 + p2
    e1 = g_idx * EXPERTS_PER_GROUP + i1
    e2 = g_idx * EXPERTS_PER_GROUP + i2
    return e1, e2, g_w * p1 / denom, g_w * p2 / denom


def _trunk_kernel(x_ref, conv_ref, fft_ref, wout_ref, gxa_ref, wq_ref, k_ref, v_ref, wo_ref,
                  gffn_ref, wr_ref, br_ref, tri_ref, ltri_ref,
                  x2_ref, xs_ref, pos_ref, gate_ref, cnt_ref):
    x1 = x_ref[...] + _dot(conv_ref[...], wout_ref[:CONV_CH, :]) + _dot(fft_ref[...], wout_ref[CONV_CH:, :])
    h2 = _rms(x1, gxa_ref[...]).astype(BF16)
    q = _dot(h2, wq_ref[...]).astype(BF16)
    outs = []
    for hd in range(XA_HEADS):
        cols = slice(hd * XA_HEAD_DIM, (hd + 1) * XA_HEAD_DIM)
        s = _dot_nt(q[:, cols], k_ref[:, cols]) * (XA_HEAD_DIM ** -0.5)
        s = s - jnp.max(s, axis=-1, keepdims=True)
        p = jnp.exp(s)
        p = p / jnp.sum(p, axis=-1, keepdims=True)
        outs.append(_dot(p.astype(BF16), v_ref[:, cols]).astype(BF16))
    o = jnp.concatenate(outs, axis=-1)
    x2 = x1 + _dot(o, wo_ref[...])
    x2_ref[...] = x2
    h3 = _rms(x2, gffn_ref[...]).astype(BF16)
    lg = _dot_nt(wr_ref[...], h3) + br_ref[...]

    e1, e2, gate1, gate2 = _route(lg)
    n_tok = lg.shape[1]
    row32 = lax.broadcasted_iota(I32, (N_EXPERTS, n_tok), 0)
    hit1 = row32 == e1
    hit2 = row32 == e2
    onehot = jnp.where(hit1 | hit2, 1.0, 0.0)
    before = _dot(onehot.astype(BF16), tri_ref[...])
    cnt = jnp.sum(onehot, axis=1, keepdims=True).astype(I32)
    piece = jnp.left_shift(jnp.right_shift(cnt + (GRANULE - 1), 3), 3)
    piece_b = jnp.broadcast_to(piece.astype(F32), (N_EXPERTS, LANES)).astype(BF16)
    start = _dot(ltri_ref[...], piece_b)[:, 0:1]
    slot = before + start
    pos1 = jnp.sum(jnp.where(hit1, slot, 0.0), axis=0, keepdims=True).astype(I32)
    pos2 = jnp.sum(jnp.where(hit2, slot, 0.0), axis=0, keepdims=True).astype(I32)

    r = lax.broadcasted_iota(I32, (LOCAL_ROWS, n_tok), 0)
    perm = jnp.where((r == pos1) | (r == pos2), 1.0, 0.0).astype(BF16)
    xs_ref[...] = _pack_halves(_dot(perm, h3[:, :HALF]), _dot(perm, h3[:, HALF:]))

    pos_ref[0:1, :] = pos1
    pos_ref[1:2, :] = pos2
    pos_ref[2:8, :] = jnp.zeros((6, n_tok), I32)
    gate_ref[0:1, :] = gate1
    gate_ref[1:2, :] = gate2
    gate_ref[2:8, :] = jnp.zeros((6, n_tok), F32)
    cnt_ref[...] = jnp.broadcast_to(cnt, (N_EXPERTS, LANES))


def _trunk(x2d, conv_n, fft_n, w_out, g_xa, w_q, kv, w_o, g_ffn, w_r_t, b_r, tri, ltri, seq):
    t = x2d.shape[0]
    n_tiles = t // TRUNK_ROWS
    n_per_batch = seq // TRUNK_ROWS
    const = lambda i: (0, 0)
    return pl.pallas_call(
        _trunk_kernel,
        grid=(n_tiles,),
        in_specs=[
            pl.BlockSpec((TRUNK_ROWS, D_MODEL), lambda i: (i, 0)),
            pl.BlockSpec((TRUNK_ROWS, CONV_CH), lambda i: (i, 0)),
            pl.BlockSpec((TRUNK_ROWS, FFT_CH), lambda i: (i, 0)),
            pl.BlockSpec((D_MODEL, D_MODEL), const),
            pl.BlockSpec((1, D_MODEL), const),
            pl.BlockSpec((D_MODEL, D_MODEL), const),
            pl.BlockSpec((MEM_LEN, D_MODEL), lambda i: (i // n_per_batch, 0)),
            pl.BlockSpec((MEM_LEN, D_MODEL), lambda i: (i // n_per_batch, 1)),
            pl.BlockSpec((D_MODEL, D_MODEL), const),
            pl.BlockSpec((1, D_MODEL), const),
            pl.BlockSpec((ROUTER_ROWS, D_MODEL), const),
            pl.BlockSpec((ROUTER_ROWS, 1), const),
            pl.BlockSpec((TRUNK_ROWS, TRUNK_ROWS), const),
            pl.BlockSpec((N_EXPERTS, N_EXPERTS), const),
        ],
        out_specs=[
            pl.BlockSpec((TRUNK_ROWS, D_MODEL), lambda i: (i, 0)),
            pl.BlockSpec((LOCAL_ROWS, HALF), lambda i: (i, 0)),
            pl.BlockSpec((8, TRUNK_ROWS), lambda i: (0, i)),
            pl.BlockSpec((8, TRUNK_ROWS), lambda i: (0, i)),
            pl.BlockSpec((None, N_EXPERTS, LANES), lambda i: (i, 0, 0)),
        ],
        out_shape=[
            jax.ShapeDtypeStruct((t, D_MODEL), F32),
            jax.ShapeDtypeStruct((n_tiles * LOCAL_ROWS, HALF), I32),
            jax.ShapeDtypeStruct((8, t), I32),
            jax.ShapeDtypeStruct((8, t), F32),
            jax.ShapeDtypeStruct((n_tiles, N_EXPERTS, LANES), I32),
        ],
        compiler_params=pltpu.CompilerParams(vmem_limit_bytes=VMEM_LIMIT),
        name="trunk",
    )(x2d, conv_n, fft_n, w_out, g_xa, w_q, kv, kv, w_o, g_ffn, w_r_t, b_r, tri, ltri)


def _granule(ref, row):
    return ref.at[pl.ds(pl.multiple_of(row, GRANULE), GRANULE), :]


def _regroup_in_kernel(glob_ref, ngran_ref, zrow_ref, zcnt_ref, zblk_ref, src_ref, out_ref, zeros_ref, sem, zsem):
    i = pl.program_id(0)

    def copy(q):
        return pltpu.make_async_copy(_granule(src_ref, q * GRANULE),
                                     _granule(out_ref, glob_ref[i * LOCAL_GRANULES + q]), sem)

    def zero_copy(e, g):
        return pltpu.make_async_copy(zeros_ref.at[pl.ds(0, GRANULE), :],
                                     _granule(out_ref, zrow_ref[e] + g * GRANULE), zsem)

    def zero_block_copy(b):
        return pltpu.make_async_copy(
            zeros_ref, out_ref.at[pl.ds(pl.multiple_of(b * EXPERT_ROWS, EXPERT_ROWS), EXPERT_ROWS), :], zsem)

    def for_each_zero(fn):
        def per_expert(e, carry):
            lax.fori_loop(0, zcnt_ref[e], lambda g, c: (fn(zero_copy(e, g)), c)[1], 0)
            return carry
        lax.fori_loop(0, N_EXPERTS, per_expert, 0)
        lax.fori_loop(zblk_ref[0], out_ref.shape[0] // EXPERT_ROWS,
                      lambda b, c: (fn(zero_block_copy(b)), c)[1], 0)

    @pl.when(i == 0)
    def _():
        zeros_ref[...] = jnp.zeros_like(zeros_ref)
        for_each_zero(lambda cp: cp.start())

    n = ngran_ref[i]
    lax.fori_loop(0, n, lambda q, c: (copy(q).start(), c)[1], 0)
    lax.fori_loop(0, n, lambda q, c: (copy(q).wait(), c)[1], 0)

    @pl.when(i == pl.num_programs(0) - 1)
    def _():
        for_each_zero(lambda cp: cp.wait())


def _regroup_in(glob, ngran, zrow, zcnt, zblk, xs_loc, out_rows):
    n_tiles = ngran.shape[0]
    return pl.pallas_call(
        _regroup_in_kernel,
        grid_spec=pltpu.PrefetchScalarGridSpec(
            num_scalar_prefetch=5,
            grid=(n_tiles,),
            in_specs=[pl.BlockSpec((LOCAL_ROWS, HALF), lambda i, *_: (i, 0))],
            out_specs=pl.BlockSpec(memory_space=pl.ANY),
            scratch_shapes=[
                pltpu.VMEM((EXPERT_ROWS, HALF), I32),
                pltpu.SemaphoreType.DMA(()),
                pltpu.SemaphoreType.DMA(()),
            ],
        ),
        out_shape=jax.ShapeDtypeStruct((out_rows, HALF), I32),
        compiler_params=pltpu.CompilerParams(dimension_semantics=("arbitrary",)),
        name="regroup_in",
    )(glob, ngran, zrow, zcnt, zblk, xs_loc)


def _experts_kernel(blk_e_ref, nblk_ref, xs_ref, wg_ref, wu_ref, wd_ref, ys_ref):
    del blk_e_ref
    used = pl.program_id(0) < nblk_ref[0]

    @pl.when(used)
    def _():
        xl, xr = _unpack_halves(xs_ref[...])
        a = _dot(xl, wg_ref[:HALF, :].astype(BF16)) + _dot(xr, wg_ref[HALF:, :].astype(BF16))
        b = _dot(xl, wu_ref[:HALF, :].astype(BF16)) + _dot(xr, wu_ref[HALF:, :].astype(BF16))
        hmid = (a * jax.nn.sigmoid(a) * b).astype(BF16)
        y = _dot(hmid, wd_ref[...].astype(BF16))
        ys_ref[...] = _pack_halves(y[:, :HALF].astype(BF16).astype(F32), y[:, HALF:].astype(BF16).astype(F32))

    @pl.when(jnp.logical_not(used))
    def _():
        ys_ref[...] = jnp.zeros_like(ys_ref)


def _experts(blk_e, nblk, xs, w_gate, w_up, w_down):
    p = xs.shape[0]
    row_blk = lambda i, be, nb: (jnp.minimum(i, nb[0] - 1), 0)
    w_blk = lambda i, be, nb: (be[i], 0, 0)
    return pl.pallas_call(
        _experts_kernel,
        grid_spec=pltpu.PrefetchScalarGridSpec(
            num_scalar_prefetch=2,
            grid=(p // EXPERT_ROWS,),
            in_specs=[
                pl.BlockSpec((EXPERT_ROWS, HALF), row_blk),
                pl.BlockSpec((None, D_MODEL, D_EXPERT), w_blk),
                pl.BlockSpec((None, D_MODEL, D_EXPERT), w_blk),
                pl.BlockSpec((None, D_EXPERT, D_MODEL), w_blk),
            ],
            out_specs=pl.BlockSpec((EXPERT_ROWS, HALF), lambda i, be, nb: (i, 0)),
        ),
        out_shape=jax.ShapeDtypeStruct((p, HALF), I32),
        compiler_params=pltpu.CompilerParams(
            dimension_semantics=("arbitrary",), vmem_limit_bytes=VMEM_LIMIT),
        name="experts",
    )(blk_e, nblk, xs, w_gate, w_up, w_down)


def _combine_kernel(glob_ref, ngran_ref, x2_ref, pos_ref, gate_ref, g_ref, ys_ref, o_ref, ybuf, sems):
    i = pl.program_id(0)
    n_tiles = pl.num_programs(0)
    slot = i % 2

    def copy(tile, buf, q):
        return pltpu.make_async_copy(_granule(ys_ref, glob_ref[tile * LOCAL_GRANULES + q]),
                                     ybuf.at[buf, pl.ds(pl.multiple_of(q * GRANULE, GRANULE), GRANULE), :],
                                     sems.at[buf])

    def start_gather(tile, buf):
        lax.fori_loop(0, ngran_ref[tile], lambda q, c: (copy(tile, buf, q).start(), c)[1], 0)

        def zero_tail(q, c):
            ybuf[buf, pl.ds(pl.multiple_of(q * GRANULE, GRANULE), GRANULE), :] = jnp.zeros((GRANULE, HALF), I32)
            return c
        lax.fori_loop(ngran_ref[tile], LOCAL_GRANULES, zero_tail, 0)

    @pl.when(i == 0)
    def _():
        start_gather(0, 0)

    @pl.when(i + 1 < n_tiles)
    def _():
        start_gather(i + 1, 1 - slot)

    lax.fori_loop(0, ngran_ref[i], lambda q, c: (copy(i, slot, q).wait(), c)[1], 0)

    n_tok = x2_ref.shape[0]
    pos = pos_ref[...]
    gates = gate_ref[...]
    col = lax.broadcasted_iota(I32, (n_tok, LOCAL_ROWS), 1)
    w = (jnp.where(col == pos[:, 0:1], gates[:, 0:1], 0.0)
         + jnp.where(col == pos[:, 1:2], gates[:, 1:2], 0.0)).astype(BF16)
    yl, yr = _unpack_halves(ybuf[slot])
    moe = jnp.concatenate([_dot(w, yl), _dot(w, yr)], axis=-1)
    o_ref[...] = _rms(x2_ref[...] + moe, g_ref[...])


def _combine(glob, ngran, x2, pos_tk, gates_tk, g_final, ys):
    t = x2.shape[0]
    return pl.pallas_call(
        _combine_kernel,
        grid_spec=pltpu.PrefetchScalarGridSpec(
            num_scalar_prefetch=2,
            grid=(t // TRUNK_ROWS,),
            in_specs=[
                pl.BlockSpec((TRUNK_ROWS, D_MODEL), lambda i, *_: (i, 0)),
                pl.BlockSpec((TRUNK_ROWS, TOP_K), lambda i, *_: (i, 0)),
                pl.BlockSpec((TRUNK_ROWS, TOP_K), lambda i, *_: (i, 0)),
                pl.BlockSpec((1, D_MODEL), lambda i, *_: (0, 0)),
                pl.BlockSpec(memory_space=pl.ANY),
            ],
            out_specs=pl.BlockSpec((TRUNK_ROWS, D_MODEL), lambda i, *_: (i, 0)),
            scratch_shapes=[
                pltpu.VMEM((2, LOCAL_ROWS, HALF), I32),
                pltpu.SemaphoreType.DMA((2,)),
            ],
        ),
        out_shape=jax.ShapeDtypeStruct((t, D_MODEL), F32),
        compiler_params=pltpu.CompilerParams(
            dimension_semantics=("arbitrary",), vmem_limit_bytes=VMEM_LIMIT),
        name="combine",
    )(glob, ngran, x2, pos_tk, gates_tk, g_final, ys)


def _router_params(w_rg, b_rg, w_re, b_re):
    w = jnp.zeros((ROUTER_ROWS, D_MODEL), F32)
    w = w.at[0:N_GROUPS].set(w_rg.T).at[8:8 + N_EXPERTS].set(w_re.T)
    b = jnp.zeros((ROUTER_ROWS,), F32)
    b = b.at[0:N_GROUPS].set(b_rg).at[N_GROUPS:8].set(NEG_BIG).at[8:8 + N_EXPERTS].set(b_re)
    return w.astype(BF16), b.reshape(ROUTER_ROWS, 1)


def _plan(cnt, n_global_rows):
    piece = (cnt + GRANULE - 1) // GRANULE * GRANULE
    lend = jnp.cumsum(piece, axis=1)
    lstart = lend - piece
    tot = jnp.sum(piece, axis=0)
    padded = (tot + EXPERT_ROWS - 1) // EXPERT_ROWS * EXPERT_ROWS
    pend = jnp.cumsum(padded)
    pstart = pend - padded
    gstart = pstart[None, :] + jnp.cumsum(piece, axis=0) - piece

    q_row = jnp.arange(LOCAL_GRANULES, dtype=I32) * GRANULE
    e_of_q = jnp.sum((lend[:, None, :] <= q_row[None, :, None]).astype(I32), axis=2)
    e_of_q = jnp.minimum(e_of_q, N_EXPERTS - 1)
    onehot = e_of_q[:, :, None] == jnp.arange(N_EXPERTS, dtype=I32)
    glob = jnp.sum(jnp.where(onehot, (gstart - lstart)[:, None, :], 0), axis=2) + q_row[None, :]
    used = lend[:, -1]
    ngran = used // GRANULE

    n_blk = n_global_rows // EXPERT_ROWS
    blk_start = jnp.arange(n_blk, dtype=I32) * EXPERT_ROWS
    blk_e = jnp.minimum(jnp.sum((pend[None, :] <= blk_start[:, None]).astype(I32), axis=1), N_EXPERTS - 1)
    nblk = pend[-1:] // EXPERT_ROWS
    as_i32 = lambda v: v.astype(I32)
    return dict(
        glob=as_i32(glob.reshape(-1)), ngran=as_i32(ngran), blk_e=as_i32(blk_e), nblk=as_i32(nblk),
        gz_row=as_i32(pstart + tot), gz_cnt=as_i32((padded - tot) // GRANULE),
    )


def _layer(x2d, mem2d, batch, seq, norm_mix_g, w_in, conv_w, conv_b, head_norm_g, w_out,
           norm_xa_g, norm_mem_g, w_q, w_kv, w_o, norm_ffn_g, w_rg, b_rg, w_re, b_re,
           w_gate, w_up, w_down, out_norm_g):
    t = x2d.shape[0]
    n_tiles = t // TRUNK_ROWS
    row = lambda v: v.reshape(1, -1)
    hg = head_norm_g.reshape(-1)
    gm = _group_mean_matrix()

    kv = _kv_proj(mem2d, row(norm_mem_g), w_kv)
    conv_n, uf = _mixer_in(x2d, row(norm_mix_g), w_in.astype(BF16), conv_w, row(conv_b),
                           row(hg[:CONV_CH]), gm, batch, seq)
    fft_n = _fourier(uf.reshape(batch, seq, FFT_CH), _fft_stage2_matrices(seq), _fft_channel_matrix(seq),
                     gm, row(hg[CONV_CH:]), batch, seq)
    w_r_t, b_r = _router_params(w_rg, b_rg, w_re, b_re)
    x2, xs_loc, pos, gates, cnt = _trunk(
        x2d, conv_n, fft_n, w_out.astype(BF16), row(norm_xa_g), w_q.astype(BF16), kv, w_o.astype(BF16),
        row(norm_ffn_g), w_r_t, b_r, _strict_upper(TRUNK_ROWS), _strict_lower(N_EXPERTS), seq)

    max_rows = n_tiles * LOCAL_ROWS + N_EXPERTS * (EXPERT_ROWS - GRANULE)
    n_global_rows = -(-max_rows // EXPERT_ROWS) * EXPERT_ROWS
    plan = _plan(cnt[:, :, 0], n_global_rows)
    xs = _regroup_in(plan["glob"], plan["ngran"], plan["gz_row"], plan["gz_cnt"], plan["nblk"],
                     xs_loc, n_global_rows)
    ys = _experts(plan["blk_e"], plan["nblk"], xs, w_gate, w_up, w_down)
    return _combine(plan["glob"], plan["ngran"], x2, pos[0:TOP_K].T, gates[0:TOP_K].T, row(out_norm_g), ys)


def kernel(x, mem, norm_mix_g, w_in, conv_w, conv_b, head_norm_g, w_out, norm_xa_g, norm_mem_g, w_q, w_kv,
           w_o, norm_ffn_g, w_route_group, b_route_group, w_route_expert, b_route_expert, w_gate, w_up,
           w_down, final_norm_g):
    batch, seq, _ = x.shape
    depth = norm_mix_g.shape[0]
    assert depth == 1, "the final norm is fused into the last layer's combine kernel"
    x2d = x.reshape(batch * seq, D_MODEL)
    mem2d = mem.reshape(batch * MEM_LEN, D_MODEL)
    l = 0
    out = _layer(x2d, mem2d, batch, seq, norm_mix_g[l], w_in[l], conv_w[l], conv_b[l], head_norm_g[l],
                 w_out[l], norm_xa_g[l], norm_mem_g[l], w_q[l], w_kv[l], w_o[l], norm_ffn_g[l],
                 w_route_group[l], b_route_group[l], w_route_expert[l], b_route_expert[l],
                 w_gate[l], w_up[l], w_down[l], final_norm_g)
    return out.reshape(batch, seq, D_MODEL)
```

```python
import math

import numpy as np
import jax
import jax.numpy as jnp
from jax import lax
from jax.experimental import pallas as pl
from jax.experimental.pallas import tpu as pltpu

F32 = jnp.float32
BF16 = jnp.bfloat16
I32 = jnp.int32

D_MODEL = 1024
HALF = D_MODEL // 2
HEAD_DIM = 64
CONV_CH = 512
FFT_CH = 512
IN_COLS = 3 * CONV_CH + FFT_CH
MEM_LEN = 256
XA_HEADS = 4
XA_HEAD_DIM = D_MODEL // XA_HEADS
N_GROUPS = 4
EXPERTS_PER_GROUP = 8
N_EXPERTS = 32
TOP_K = 2
D_EXPERT = 512
EPS = 1e-6

FFT_N1 = 16
FFT_N2 = 256
FFT_K1_PER_STEP = 4

LANES = 128
GRANULE = 8
MIX_ROWS = 512
TRUNK_ROWS = 512
LOCAL_ROWS = TOP_K * TRUNK_ROWS + N_EXPERTS * GRANULE
LOCAL_GRANULES = LOCAL_ROWS // GRANULE
EXPERT_ROWS = 256
ROUTER_ROWS = 128
NEG_BIG = -1e30
HI16 = -65536

VMEM_LIMIT = 56 * 1024 * 1024


def _rms(x, g):
    return x * lax.rsqrt(jnp.mean(x * x, axis=-1, keepdims=True) + EPS) * g


def _dot(a, b):
    return jnp.dot(a, b, preferred_element_type=F32)


def _dot_nt(a, b):
    return lax.dot_general(a, b, (((1,), (1,)), ((), ())), preferred_element_type=F32)


def _pack_halves(left_f32, right_f32):
    lb = lax.bitcast_convert_type(left_f32, I32)
    rb = lax.shift_right_logical(lax.bitcast_convert_type(right_f32, I32), jnp.int32(16))
    return lb | rb


def _unpack_halves(packed_i32):
    left = lax.bitcast_convert_type(packed_i32 & jnp.int32(HI16), F32)
    right = lax.bitcast_convert_type(lax.shift_left(packed_i32, jnp.int32(16)), F32)
    return left.astype(BF16), right.astype(BF16)


def _group_mean_matrix():
    g = np.kron(np.eye(FFT_CH // HEAD_DIM), np.full((HEAD_DIM, HEAD_DIM), 1.0 / HEAD_DIM))
    return jnp.asarray(g, dtype=BF16)


def _fft_stage2_matrices(seq):
    k1 = np.arange(FFT_N1)[:, None, None]
    k2 = np.arange(FFT_N2)[None, :, None]
    s2 = np.arange(FFT_N2)[None, None, :]
    ang = 2.0 * np.pi * ((s2 * (k1 + FFT_N1 * k2)) % seq) / seq
    c, s = np.cos(ang), np.sin(ang)
    top = np.concatenate([c, s], axis=2)
    bot = np.concatenate([-s, c], axis=2)
    return jnp.asarray(np.concatenate([top, bot], axis=1), dtype=BF16)


def _fft_channel_matrix(seq):
    c = np.arange(HEAD_DIM)
    ang = 2.0 * np.pi * ((c[:, None] * c[None, :]) % HEAD_DIM) / HEAD_DIM
    scale = 1.0 / math.sqrt(seq * HEAD_DIM)
    eye = np.eye(FFT_CH // HEAD_DIM)
    cs = np.concatenate([np.kron(eye, np.cos(ang)), np.kron(eye, np.sin(ang))], axis=0) * scale
    return jnp.asarray(cs, dtype=BF16)


def _strict_upper(n):
    return jnp.asarray(np.triu(np.ones((n, n)), k=1), dtype=BF16)


def _strict_lower(n):
    return jnp.asarray(np.tril(np.ones((n, n)), k=-1), dtype=BF16)


def _kv_kernel(mem_ref, g_ref, w_ref, o_ref):
    h = _rms(mem_ref[...], g_ref[...]).astype(BF16)
    o_ref[...] = _dot(h, w_ref[...].astype(BF16)).astype(BF16)


def _kv_proj(mem2d, g, w_kv):
    rows = mem2d.shape[0]
    cols = w_kv.shape[1]
    cb = 512
    return pl.pallas_call(
        _kv_kernel,
        grid=(cols // cb,),
        in_specs=[
            pl.BlockSpec((rows, D_MODEL), lambda j: (0, 0)),
            pl.BlockSpec((1, D_MODEL), lambda j: (0, 0)),
            pl.BlockSpec((D_MODEL, cb), lambda j: (0, j)),
        ],
        out_specs=pl.BlockSpec((rows, cb), lambda j: (0, j)),
        out_shape=jax.ShapeDtypeStruct((rows, cols), BF16),
        compiler_params=pltpu.CompilerParams(vmem_limit_bytes=VMEM_LIMIT),
        name="kv_proj",
    )(mem2d, g, w_kv)


def _mixer_in_kernel(x_ref, xp_ref, xn_ref, g_ref, w_ref, cw_ref, cb_ref, hg_ref, gm_ref,
                     conv_ref, uf_ref):
    i = pl.program_id(1)
    n_i = pl.num_programs(1)
    rows = x_ref.shape[0]
    g = g_ref[...]
    h = _rms(x_ref[...], g).astype(BF16)
    u = _dot(h, w_ref[...])
    b_gate = u[:, :CONV_CH]
    cv = u[:, CONV_CH:2 * CONV_CH] * u[:, 2 * CONV_CH:3 * CONV_CH]
    uf_ref[...] = u[:, 3 * CONV_CH:].astype(BF16)

    hh = jnp.concatenate([_rms(xp_ref[...], g), _rms(xn_ref[...], g)], axis=0).astype(BF16)
    uh = _dot(hh, w_ref[:, CONV_CH:3 * CONV_CH])
    cvh = uh[:, :CONV_CH] * uh[:, CONV_CH:]
    cv_prev = cvh[7:8, :] * jnp.where(i == 0, 0.0, 1.0)
    cv_next = cvh[8:9, :] * jnp.where(i == n_i - 1, 0.0, 1.0)

    row = lax.broadcasted_iota(I32, cv.shape, 0)
    cv_up = jnp.where(row == 0, cv_prev, pltpu.roll(cv, 1, 0))
    cv_dn = jnp.where(row == rows - 1, cv_next, pltpu.roll(cv, rows - 1, 0))
    z = cw_ref[0:1, :] * cv_up + cw_ref[1:2, :] * cv + cw_ref[2:3, :] * cv_dn + cb_ref[...]
    y = b_gate * z
    ms = _dot((y * y).astype(BF16), gm_ref[...])
    conv_ref[...] = (y * lax.rsqrt(ms + EPS) * hg_ref[...]).astype(BF16)


def _mixer_in(x2d, g, w_in, conv_w, conv_b, hg_conv, gm, batch, seq):
    n_i = seq // MIX_ROWS
    t = x2d.shape[0]
    r8 = MIX_ROWS // 8
    last8 = t // 8 - 1
    return pl.pallas_call(
        _mixer_in_kernel,
        grid=(batch, n_i),
        in_specs=[
            pl.BlockSpec((MIX_ROWS, D_MODEL), lambda b, i: (b * n_i + i, 0)),
            pl.BlockSpec((8, D_MODEL), lambda b, i: (jnp.maximum((b * n_i + i) * r8 - 1, 0), 0)),
            pl.BlockSpec((8, D_MODEL), lambda b, i: (jnp.minimum((b * n_i + i + 1) * r8, last8), 0)),
            pl.BlockSpec((1, D_MODEL), lambda b, i: (0, 0)),
            pl.BlockSpec((D_MODEL, IN_COLS), lambda b, i: (0, 0)),
            pl.BlockSpec((3, CONV_CH), lambda b, i: (0, 0)),
            pl.BlockSpec((1, CONV_CH), lambda b, i: (0, 0)),
            pl.BlockSpec((1, CONV_CH), lambda b, i: (0, 0)),
            pl.BlockSpec((CONV_CH, CONV_CH), lambda b, i: (0, 0)),
        ],
        out_specs=[
            pl.BlockSpec((MIX_ROWS, CONV_CH), lambda b, i: (b * n_i + i, 0)),
            pl.BlockSpec((MIX_ROWS, FFT_CH), lambda b, i: (b * n_i + i, 0)),
        ],
        out_shape=[
            jax.ShapeDtypeStruct((t, CONV_CH), BF16),
            jax.ShapeDtypeStruct((t, FFT_CH), BF16),
        ],
        compiler_params=pltpu.CompilerParams(vmem_limit_bytes=VMEM_LIMIT),
        name="mixer_in",
    )(x2d, x2d, x2d, g, w_in, conv_w, conv_b, hg_conv, gm)


_S1_ROWS = 16
_S1_LANES = 128


def _lincomb(terms):
    acc = None
    for coef, val in terms:
        if abs(coef) < 1e-12:
            continue
        if abs(coef - 1.0) < 1e-12:
            term, neg = val, False
        elif abs(coef + 1.0) < 1e-12:
            term, neg = val, True
        else:
            term, neg = coef * val, False
        if acc is None:
            acc = -term if neg else term
        else:
            acc = acc - term if neg else acc + term
    return acc


def _fft_stage1(x_ref, a_ref):
    half = FFT_N1 // 2
    cos = [[math.cos(2 * math.pi * ((k * j) % FFT_N1) / FFT_N1) for j in range(FFT_N1)] for k in range(FFT_N1)]
    sin = [[math.sin(2 * math.pi * ((k * j) % FFT_N1) / FFT_N1) for j in range(FFT_N1)] for k in range(FFT_N1)]

    def body(r, carry):
        r0 = pl.multiple_of(r * _S1_ROWS, _S1_ROWS)
        rows_re = pl.ds(r0, _S1_ROWS)
        rows_im = pl.ds(r0 + FFT_N2, _S1_ROWS)
        for lc in range(0, FFT_CH, _S1_LANES):
            lanes = slice(lc, lc + _S1_LANES)
            xs = [x_ref[j, rows_re, lanes].astype(F32) for j in range(FFT_N1)]
            ev = [None] + [xs[j] + xs[FFT_N1 - j] for j in range(1, half)]
            od = [None] + [xs[j] - xs[FFT_N1 - j] for j in range(1, half)]
            for k in range(half + 1):
                re = _lincomb([(1.0, xs[0]), (cos[k][half], xs[half])]
                              + [(cos[k][j], ev[j]) for j in range(1, half)])
                a_ref[k, rows_re, lanes] = re.astype(BF16)
                if k in (0, half):
                    zero = jnp.zeros_like(re).astype(BF16)
                    a_ref[k, rows_im, lanes] = zero
                else:
                    im = _lincomb([(-sin[k][j], od[j]) for j in range(1, half)])
                    a_ref[k, rows_im, lanes] = im.astype(BF16)
                    a_ref[FFT_N1 - k, rows_re, lanes] = re.astype(BF16)
                    a_ref[FFT_N1 - k, rows_im, lanes] = (-im).astype(BF16)
        return carry

    lax.fori_loop(0, FFT_N2 // _S1_ROWS, body, 0)


def _fourier_kernel(x_ref, m2_ref, cs_ref, gm_ref, hg_ref, o_ref, a_ref, y_ref):
    j = pl.program_id(1)

    @pl.when(j == 0)
    def _():
        _fft_stage1(x_ref, a_ref)

    for kk in range(FFT_K1_PER_STEP):
        k1 = j * FFT_K1_PER_STEP + kk
        ri = _dot(m2_ref[kk], a_ref[k1])
        re = ri[:FFT_N2].astype(BF16)
        im = ri[FFT_N2:].astype(BF16)
        y = _dot(re, cs_ref[:FFT_CH, :]) + _dot(im, cs_ref[FFT_CH:, :])
        ms = _dot((y * y).astype(BF16), gm_ref[...])
        yn = y * lax.rsqrt(ms + EPS) * hg_ref[...]
        for c in range(FFT_CH // LANES):
            y_ref[c, pl.ds(k1, FFT_N2, stride=FFT_N1), :] = yn[:, c * LANES:(c + 1) * LANES]

    @pl.when(j == pl.num_programs(1) - 1)
    def _():
        for c in range(FFT_CH // LANES):
            o_ref[:, c * LANES:(c + 1) * LANES] = y_ref[c].astype(BF16)


def _fourier(uf, m2, cs, gm, hg_fft, batch, seq):
    assert seq == FFT_N1 * FFT_N2
    x4 = uf.reshape(batch, FFT_N1, FFT_N2, FFT_CH)
    out = pl.pallas_call(
        _fourier_kernel,
        grid=(batch, FFT_N1 // FFT_K1_PER_STEP),
        in_specs=[
            pl.BlockSpec((None, FFT_N1, FFT_N2, FFT_CH), lambda b, j: (b, 0, 0, 0)),
            pl.BlockSpec((FFT_K1_PER_STEP, 2 * FFT_N2, 2 * FFT_N2), lambda b, j: (j, 0, 0)),
            pl.BlockSpec((2 * FFT_CH, FFT_CH), lambda b, j: (0, 0)),
            pl.BlockSpec((FFT_CH, FFT_CH), lambda b, j: (0, 0)),
            pl.BlockSpec((1, FFT_CH), lambda b, j: (0, 0)),
        ],
        out_specs=pl.BlockSpec((seq, FFT_CH), lambda b, j: (b, 0)),
        out_shape=jax.ShapeDtypeStruct((batch * seq, FFT_CH), BF16),
        scratch_shapes=[
            pltpu.VMEM((FFT_N1, 2 * FFT_N2, FFT_CH), BF16),
            pltpu.VMEM((FFT_CH // LANES, seq, LANES), F32),
        ],
        compiler_params=pltpu.CompilerParams(
            dimension_semantics=("arbitrary", "arbitrary"), vmem_limit_bytes=VMEM_LIMIT),
        name="fourier",
    )(x4, m2, cs, gm, hg_fft)
    return out


def _first_index_of_max(vals, vmax, row):
    return jnp.min(jnp.where(vals == vmax, row, vals.shape[0]), axis=0, keepdims=True)


def _route(lg):
    cols = lg.shape[1]
    row8 = lax.broadcasted_iota(I32, (EXPERTS_PER_GROUP, cols), 0)
    gl = lg[0:8, :]
    gmax = jnp.max(gl, axis=0, keepdims=True)
    g_w = 1.0 / jnp.sum(jnp.exp(gl - gmax), axis=0, keepdims=True)
    g_idx = _first_index_of_max(gl, gmax, row8)

    el = lg[8:16, :]
    for g in range(1, N_GROUPS):
        el = jnp.where(g_idx == g, lg[8 + 8 * g:16 + 8 * g, :], el)
    emax = jnp.max(el, axis=0, keepdims=True)
    ee = jnp.exp(el - emax)
    e_prob = ee / jnp.sum(ee, axis=0, keepdims=True)
    p1 = jnp.max(e_prob, axis=0, keepdims=True)
    i1 = _first_index_of_max(e_prob, p1, row8)
    rest = jnp.where(row8 == i1, -1.0, e_prob)
    p2 = jnp.max(rest, axis=0, keepdims=True)
    i2 = _first_index_of_max(rest, p2, row8)
    denom = p1 + p2
    e1 = g_idx * EXPERTS_PER_GROUP + i1
    e2 = g_idx * EXPERTS_PER_GROUP + i2
    return e1, e2, g_w * p1 / denom, g_w * p2 / denom


def _trunk_kernel(x_ref, conv_ref, fft_ref, wout_ref, gxa_ref, wq_ref, k_ref, v_ref, wo_ref,
                  gffn_ref, wr_ref, br_ref, tri_ref, ltri_ref,
                  x2_ref, xs_ref, pos_ref, gate_ref, cnt_ref):
    x1 = x_ref[...] + _dot(conv_ref[...], wout_ref[:CONV_CH, :]) + _dot(fft_ref[...], wout_ref[CONV_CH:, :])
    h2 = _rms(x1, gxa_ref[...]).astype(BF16)
    q = _dot(h2, wq_ref[...]).astype(BF16)
    outs = []
    for hd in range(XA_HEADS):
        cols = slice(hd * XA_HEAD_DIM, (hd + 1) * XA_HEAD_DIM)
        s = _dot_nt(q[:, cols], k_ref[:, cols]) * (XA_HEAD_DIM ** -0.5)
        s = s - jnp.max(s, axis=-1, keepdims=True)
        p = jnp.exp(s)
        p = p / jnp.sum(p, axis=-1, keepdims=True)
        outs.append(_dot(p.astype(BF16), v_ref[:, cols]).astype(BF16))
    o = jnp.concatenate(outs, axis=-1)
    x2 = x1 + _dot(o, wo_ref[...])
    x2_ref[...] = x2
    h3 = _rms(x2, gffn_ref[...]).astype(BF16)
    lg = _dot_nt(wr_ref[...], h3) + br_ref[...]

    e1, e2, gate1, gate2 = _route(lg)
    n_tok = lg.shape[1]
    row32 = lax.broadcasted_iota(I32, (N_EXPERTS, n_tok), 0)
    hit1 = row32 == e1
    hit2 = row32 == e2
    onehot = jnp.where(hit1 | hit2, 1.0, 0.0)
    before = _dot(onehot.astype(BF16), tri_ref[...])
    cnt = jnp.sum(onehot, axis=1, keepdims=True).astype(I32)
    piece = jnp.left_shift(jnp.right_shift(cnt + (GRANULE - 1), 3), 3)
    piece_b = jnp.broadcast_to(piece.astype(F32), (N_EXPERTS, LANES)).astype(BF16)
    start = _dot(ltri_ref[...], piece_b)[:, 0:1]
    slot = before + start
    pos1 = jnp.sum(jnp.where(hit1, slot, 0.0), axis=0, keepdims=True).astype(I32)
    pos2 = jnp.sum(jnp.where(hit2, slot, 0.0), axis=0, keepdims=True).astype(I32)

    r = lax.broadcasted_iota(I32, (LOCAL_ROWS, n_tok), 0)
    perm = jnp.where((r == pos1) | (r == pos2), 1.0, 0.0).astype(BF16)
    xs_ref[...] = _pack_halves(_dot(perm, h3[:, :HALF]), _dot(perm, h3[:, HALF:]))

    pos_ref[0:1, :] = pos1
    pos_ref[1:2, :] = pos2
    pos_ref[2:8, :] = jnp.zeros((6, n_tok), I32)
    gate_ref[0:1, :] = gate1
    gate_ref[1:2, :] = gate2
    gate_ref[2:8, :] = jnp.zeros((6, n_tok), F32)
    cnt_ref[...] = jnp.broadcast_to(cnt, (N_EXPERTS, LANES))


def _trunk(x2d, conv_n, fft_n, w_out, g_xa, w_q, kv, w_o, g_ffn, w_r_t, b_r, tri, ltri, seq):
    t = x2d.shape[0]
    n_tiles = t // TRUNK_ROWS
    n_per_batch = seq // TRUNK_ROWS
    const = lambda i: (0, 0)
    return pl.pallas_call(
        _trunk_kernel,
        grid=(n_tiles,),
        in_specs=[
            pl.BlockSpec((TRUNK_ROWS, D_MODEL), lambda i: (i, 0)),
            pl.BlockSpec((TRUNK_ROWS, CONV_CH), lambda i: (i, 0)),
            pl.BlockSpec((TRUNK_ROWS, FFT_CH), lambda i: (i, 0)),
            pl.BlockSpec((D_MODEL, D_MODEL), const),
            pl.BlockSpec((1, D_MODEL), const),
            pl.BlockSpec((D_MODEL, D_MODEL), const),
            pl.BlockSpec((MEM_LEN, D_MODEL), lambda i: (i // n_per_batch, 0)),
            pl.BlockSpec((MEM_LEN, D_MODEL), lambda i: (i // n_per_batch, 1)),
            pl.BlockSpec((D_MODEL, D_MODEL), const),
            pl.BlockSpec((1, D_MODEL), const),
            pl.BlockSpec((ROUTER_ROWS, D_MODEL), const),
            pl.BlockSpec((ROUTER_ROWS, 1), const),
            pl.BlockSpec((TRUNK_ROWS, TRUNK_ROWS), const),
            pl.BlockSpec((N_EXPERTS, N_EXPERTS), const),
        ],
        out_specs=[
            pl.BlockSpec((TRUNK_ROWS, D_MODEL), lambda i: (i, 0)),
            pl.BlockSpec((LOCAL_ROWS, HALF), lambda i: (i, 0)),
            pl.BlockSpec((8, TRUNK_ROWS), lambda i: (0, i)),
            pl.BlockSpec((8, TRUNK_ROWS), lambda i: (0, i)),
            pl.BlockSpec((None, N_EXPERTS, LANES), lambda i: (i, 0, 0)),
        ],
        out_shape=[
            jax.ShapeDtypeStruct((t, D_MODEL), F32),
            jax.ShapeDtypeStruct((n_tiles * LOCAL_ROWS, HALF), I32),
            jax.ShapeDtypeStruct((8, t), I32),
            jax.ShapeDtypeStruct((8, t), F32),
            jax.ShapeDtypeStruct((n_tiles, N_EXPERTS, LANES), I32),
        ],
        compiler_params=pltpu.CompilerParams(vmem_limit_bytes=VMEM_LIMIT),
        name="trunk",
    )(x2d, conv_n, fft_n, w_out, g_xa, w_q, kv, kv, w_o, g_ffn, w_r_t, b_r, tri, ltri)


def _granule(ref, row):
    return ref.at[pl.ds(pl.multiple_of(row, GRANULE), GRANULE), :]


def _regroup_in_kernel(glob_ref, ngran_ref, zrow_ref, zcnt_ref, zblk_ref, src_ref, out_ref, zeros_ref, sem, zsem):
    i = pl.program_id(0)

    def copy(q):
        return pltpu.make_async_copy(_granule(src_ref, q * GRANULE),
                                     _granule(out_ref, glob_ref[i * LOCAL_GRANULES + q]), sem)

    def zero_copy(e, g):
        return pltpu.make_async_copy(zeros_ref.at[pl.ds(0, GRANULE), :],
                                     _granule(out_ref, zrow_ref[e] + g * GRANULE), zsem)

    def zero_block_copy(b):
        return pltpu.make_async_copy(
            zeros_ref, out_ref.at[pl.ds(pl.multiple_of(b * EXPERT_ROWS, EXPERT_ROWS), EXPERT_ROWS), :], zsem)

    def for_each_zero(fn):
        def per_expert(e, carry):
            lax.fori_loop(0, zcnt_ref[e], lambda g, c: (fn(zero_copy(e, g)), c)[1], 0)
            return carry
        lax.fori_loop(0, N_EXPERTS, per_expert, 0)
        lax.fori_loop(zblk_ref[0], out_ref.shape[0] // EXPERT_ROWS,
                      lambda b, c: (fn(zero_block_copy(b)), c)[1], 0)

    @pl.when(i == 0)
    def _():
        zeros_ref[...] = jnp.zeros_like(zeros_ref)
        for_each_zero(lambda cp: cp.start())

    n = ngran_ref[i]
    lax.fori_loop(0, n, lambda q, c: (copy(q).start(), c)[1], 0)
    lax.fori_loop(0, n, lambda q, c: (copy(q).wait(), c)[1], 0)

    @pl.when(i == pl.num_programs(0) - 1)
    def _():
        for_each_zero(lambda cp: cp.wait())


def _regroup_in(glob, ngran, zrow, zcnt, zblk, xs_loc, out_rows):
    n_tiles = ngran.shape[0]
    return pl.pallas_call(
        _regroup_in_kernel,
        grid_spec=pltpu.PrefetchScalarGridSpec(
            num_scalar_prefetch=5,
            grid=(n_tiles,),
            in_specs=[pl.BlockSpec((LOCAL_ROWS, HALF), lambda i, *_: (i, 0))],
            out_specs=pl.BlockSpec(memory_space=pl.ANY),
            scratch_shapes=[
                pltpu.VMEM((EXPERT_ROWS, HALF), I32),
                pltpu.SemaphoreType.DMA(()),
                pltpu.SemaphoreType.DMA(()),
            ],
        ),
        out_shape=jax.ShapeDtypeStruct((out_rows, HALF), I32),
        compiler_params=pltpu.CompilerParams(dimension_semantics=("arbitrary",)),
        name="regroup_in",
    )(glob, ngran, zrow, zcnt, zblk, xs_loc)


def _experts_kernel(blk_e_ref, first_ref, next_e_ref, slot_ref, nblk_ref,
                    xs_ref, wg_hbm, wu_hbm, wd_hbm, ys_ref, wg_buf, wu_buf, wd_buf, sems):
    i = pl.program_id(0)

    def fetch(e, s):
        return (pltpu.make_async_copy(wg_hbm.at[e], wg_buf.at[s], sems.at[0, s]),
                pltpu.make_async_copy(wu_hbm.at[e], wu_buf.at[s], sems.at[1, s]),
                pltpu.make_async_copy(wd_hbm.at[e], wd_buf.at[s], sems.at[2, s]))

    @pl.when(i == 0)
    def _():
        for cp in fetch(blk_e_ref[0], 0):
            cp.start()

    @pl.when(i < nblk_ref[0])
    def _():
        s = slot_ref[i]

        @pl.when(first_ref[i] == 1)
        def _():
            for cp in fetch(blk_e_ref[i], s):
                cp.wait()

            @pl.when(next_e_ref[i] >= 0)
            def _():
                for cp in fetch(next_e_ref[i], 1 - s):
                    cp.start()

        xl, xr = _unpack_halves(xs_ref[...])
        a = _dot(xl, wg_buf[s, :HALF, :].astype(BF16)) + _dot(xr, wg_buf[s, HALF:, :].astype(BF16))
        b = _dot(xl, wu_buf[s, :HALF, :].astype(BF16)) + _dot(xr, wu_buf[s, HALF:, :].astype(BF16))
        hmid = (a * jax.nn.sigmoid(a) * b).astype(BF16)
        y = _dot(hmid, wd_buf[s].astype(BF16))
        ys_ref[...] = _pack_halves(y[:, :HALF].astype(BF16).astype(F32), y[:, HALF:].astype(BF16).astype(F32))


def _experts(plan, xs, w_gate, w_up, w_down):
    p = xs.shape[0]
    row_blk = lambda i, be, fi, ne, sl, nb: (jnp.minimum(i, nb[0] - 1), 0)
    return pl.pallas_call(
        _experts_kernel,
        grid_spec=pltpu.PrefetchScalarGridSpec(
            num_scalar_prefetch=5,
            grid=(p // EXPERT_ROWS,),
            in_specs=[
                pl.BlockSpec((EXPERT_ROWS, HALF), row_blk),
                pl.BlockSpec(memory_space=pl.ANY),
                pl.BlockSpec(memory_space=pl.ANY),
                pl.BlockSpec(memory_space=pl.ANY),
            ],
            out_specs=pl.BlockSpec((EXPERT_ROWS, HALF), row_blk),
            scratch_shapes=[
                pltpu.VMEM((2, D_MODEL, D_EXPERT), F32),
                pltpu.VMEM((2, D_MODEL, D_EXPERT), F32),
                pltpu.VMEM((2, D_EXPERT, D_MODEL), F32),
                pltpu.SemaphoreType.DMA((3, 2)),
            ],
        ),
        out_shape=jax.ShapeDtypeStruct((p, HALF), I32),
        input_output_aliases={5: 0},
        compiler_params=pltpu.CompilerParams(
            dimension_semantics=("arbitrary",), vmem_limit_bytes=VMEM_LIMIT),
        name="experts",
    )(plan["blk_e"], plan["first"], plan["next_e"], plan["slot"], plan["nblk"], xs, w_gate, w_up, w_down)


def _combine_kernel(glob_ref, ngran_ref, x2_ref, pos_ref, gate_ref, g_ref, ys_ref, o_ref, ybuf, sems):
    i = pl.program_id(0)
    n_tiles = pl.num_programs(0)
    slot = i % 2

    def copy(tile, buf, q):
        return pltpu.make_async_copy(_granule(ys_ref, glob_ref[tile * LOCAL_GRANULES + q]),
                                     ybuf.at[buf, pl.ds(pl.multiple_of(q * GRANULE, GRANULE), GRANULE), :],
                                     sems.at[buf])

    def start_gather(tile, buf):
        lax.fori_loop(0, ngran_ref[tile], lambda q, c: (copy(tile, buf, q).start(), c)[1], 0)

        def zero_tail(q, c):
            ybuf[buf, pl.ds(pl.multiple_of(q * GRANULE, GRANULE), GRANULE), :] = jnp.zeros((GRANULE, HALF), I32)
            return c
        lax.fori_loop(ngran_ref[tile], LOCAL_GRANULES, zero_tail, 0)

    @pl.when(i == 0)
    def _():
        start_gather(0, 0)

    @pl.when(i + 1 < n_tiles)
    def _():
        start_gather(i + 1, 1 - slot)

    lax.fori_loop(0, ngran_ref[i], lambda q, c: (copy(i, slot, q).wait(), c)[1], 0)

    n_tok = x2_ref.shape[0]
    pos = pos_ref[...]
    gates = gate_ref[...]
    col = lax.broadcasted_iota(I32, (n_tok, LOCAL_ROWS), 1)
    w = (jnp.where(col == pos[:, 0:1], gates[:, 0:1], 0.0)
         + jnp.where(col == pos[:, 1:2], gates[:, 1:2], 0.0)).astype(BF16)
    yl, yr = _unpack_halves(ybuf[slot])
    moe = jnp.concatenate([_dot(w, yl), _dot(w, yr)], axis=-1)
    o_ref[...] = _rms(x2_ref[...] + moe, g_ref[...])


def _combine(glob, ngran, x2, pos_tk, gates_tk, g_final, ys):
    t = x2.shape[0]
    return pl.pallas_call(
        _combine_kernel,
        grid_spec=pltpu.PrefetchScalarGridSpec(
            num_scalar_prefetch=2,
            grid=(t // TRUNK_ROWS,),
            in_specs=[
                pl.BlockSpec((TRUNK_ROWS, D_MODEL), lambda i, *_: (i, 0)),
                pl.BlockSpec((TRUNK_ROWS, TOP_K), lambda i, *_: (i, 0)),
                pl.BlockSpec((TRUNK_ROWS, TOP_K), lambda i, *_: (i, 0)),
                pl.BlockSpec((1, D_MODEL), lambda i, *_: (0, 0)),
                pl.BlockSpec(memory_space=pl.ANY),
            ],
            out_specs=pl.BlockSpec((TRUNK_ROWS, D_MODEL), lambda i, *_: (i, 0)),
            scratch_shapes=[
                pltpu.VMEM((2, LOCAL_ROWS, HALF), I32),
                pltpu.SemaphoreType.DMA((2,)),
            ],
        ),
        out_shape=jax.ShapeDtypeStruct((t, D_MODEL), F32),
        compiler_params=pltpu.CompilerParams(
            dimension_semantics=("arbitrary",), vmem_limit_bytes=VMEM_LIMIT),
        name="combine",
    )(glob, ngran, x2, pos_tk, gates_tk, g_final, ys)


def _router_params(w_rg, b_rg, w_re, b_re):
    w = jnp.zeros((ROUTER_ROWS, D_MODEL), F32)
    w = w.at[0:N_GROUPS].set(w_rg.T).at[8:8 + N_EXPERTS].set(w_re.T)
    b = jnp.zeros((ROUTER_ROWS,), F32)
    b = b.at[0:N_GROUPS].set(b_rg).at[N_GROUPS:8].set(NEG_BIG).at[8:8 + N_EXPERTS].set(b_re)
    return w.astype(BF16), b.reshape(ROUTER_ROWS, 1)


def _plan(cnt, n_global_rows):
    piece = (cnt + GRANULE - 1) // GRANULE * GRANULE
    lend = jnp.cumsum(piece, axis=1)
    lstart = lend - piece
    tot = jnp.sum(piece, axis=0)
    padded = (tot + EXPERT_ROWS - 1) // EXPERT_ROWS * EXPERT_ROWS
    pend = jnp.cumsum(padded)
    pstart = pend - padded
    gstart = pstart[None, :] + jnp.cumsum(piece, axis=0) - piece

    q_row = jnp.arange(LOCAL_GRANULES, dtype=I32) * GRANULE
    e_of_q = jnp.sum((lend[:, None, :] <= q_row[None, :, None]).astype(I32), axis=2)
    e_of_q = jnp.minimum(e_of_q, N_EXPERTS - 1)
    onehot = e_of_q[:, :, None] == jnp.arange(N_EXPERTS, dtype=I32)
    glob = jnp.sum(jnp.where(onehot, (gstart - lstart)[:, None, :], 0), axis=2) + q_row[None, :]
    used = lend[:, -1]
    ngran = used // GRANULE

    n_blk = n_global_rows // EXPERT_ROWS
    blk_start = jnp.arange(n_blk, dtype=I32) * EXPERT_ROWS
    blk_e = jnp.minimum(jnp.sum((pend[None, :] <= blk_start[:, None]).astype(I32), axis=1), N_EXPERTS - 1)
    nblk = pend[-1:] // EXPERT_ROWS
    blk = jnp.arange(n_blk, dtype=I32)
    valid = blk < nblk
    change = jnp.concatenate([jnp.ones((1,), bool), blk_e[1:] != blk_e[:-1]])
    slot = (jnp.cumsum(change.astype(I32)) - 1) % 2
    later = (blk_e[None, :] > blk_e[:, None]) & valid[None, :]
    next_e = jnp.min(jnp.where(later, blk_e[None, :], N_EXPERTS), axis=1)
    next_e = jnp.where(next_e == N_EXPERTS, -1, next_e)
    as_i32 = lambda v: v.astype(I32)
    return dict(
        glob=as_i32(glob.reshape(-1)), ngran=as_i32(ngran), blk_e=as_i32(blk_e), nblk=as_i32(nblk),
        first=as_i32(change & valid), slot=as_i32(slot), next_e=as_i32(next_e),
        gz_row=as_i32(pstart + tot), gz_cnt=as_i32((padded - tot) // GRANULE),
    )


def _layer(x2d, mem2d, batch, seq, norm_mix_g, w_in, conv_w, conv_b, head_norm_g, w_out,
           norm_xa_g, norm_mem_g, w_q, w_kv, w_o, norm_ffn_g, w_rg, b_rg, w_re, b_re,
           w_gate, w_up, w_down, out_norm_g):
    t = x2d.shape[0]
    n_tiles = t // TRUNK_ROWS
    row = lambda v: v.reshape(1, -1)
    hg = head_norm_g.reshape(-1)
    gm = _group_mean_matrix()

    kv = _kv_proj(mem2d, row(norm_mem_g), w_kv)
    conv_n, uf = _mixer_in(x2d, row(norm_mix_g), w_in.astype(BF16), conv_w, row(conv_b),
                           row(hg[:CONV_CH]), gm, batch, seq)
    fft_n = _fourier(uf.reshape(batch, seq, FFT_CH), _fft_stage2_matrices(seq), _fft_channel_matrix(seq),
                     gm, row(hg[CONV_CH:]), batch, seq)
    w_r_t, b_r = _router_params(w_rg, b_rg, w_re, b_re)
    x2, xs_loc, pos, gates, cnt = _trunk(
        x2d, conv_n, fft_n, w_out.astype(BF16), row(norm_xa_g), w_q.astype(BF16), kv, w_o.astype(BF16),
        row(norm_ffn_g), w_r_t, b_r, _strict_upper(TRUNK_ROWS), _strict_lower(N_EXPERTS), seq)

    max_rows = n_tiles * LOCAL_ROWS + N_EXPERTS * (EXPERT_ROWS - GRANULE)
    n_global_rows = -(-max_rows // EXPERT_ROWS) * EXPERT_ROWS
    plan = _plan(cnt[:, :, 0], n_global_rows)
    xs = _regroup_in(plan["glob"], plan["ngran"], plan["gz_row"], plan["gz_cnt"], plan["nblk"],
                     xs_loc, n_global_rows)
    ys = _experts(plan, xs, w_gate, w_up, w_down)
    return _combine(plan["glob"], plan["ngran"], x2, pos[0:TOP_K].T, gates[0:TOP_K].T, row(out_norm_g), ys)


def kernel(x, mem, norm_mix_g, w_in, conv_w, conv_b, head_norm_g, w_out, norm_xa_g, norm_mem_g, w_q, w_kv,
           w_o, norm_ffn_g, w_route_group, b_route_group, w_route_expert, b_route_expert, w_gate, w_up,
           w_down, final_norm_g):
    batch, seq, _ = x.shape
    depth = norm_mix_g.shape[0]
    assert depth == 1, "the final norm is fused into the last layer's combine kernel"
    x2d = x.reshape(batch * seq, D_MODEL)
    mem2d = mem.reshape(batch * MEM_LEN, D_MODEL)
    l = 0
    out = _layer(x2d, mem2d, batch, seq, norm_mix_g[l], w_in[l], conv_w[l], conv_b[l], head_norm_g[l],
                 w_out[l], norm_xa_g[l], norm_mem_g[l], w_q[l], w_kv[l], w_o[l], norm_ffn_g[l],
                 w_route_group[l], b_route_group[l], w_route_expert[l], b_route_expert[l],
                 w_gate[l], w_up[l], w_down[l], final_norm_g)
    return out.reshape(batch, seq, D_MODEL)
```

```python
import math

import numpy as np
import jax
import jax.numpy as jnp
from jax import lax
from jax.experimental import pallas as pl
from jax.experimental.pallas import tpu as pltpu

F32 = jnp.float32
BF16 = jnp.bfloat16
I32 = jnp.int32

D_MODEL = 1024
HALF = D_MODEL // 2
HEAD_DIM = 64
CONV_CH = 512
FFT_CH = 512
IN_COLS = 3 * CONV_CH + FFT_CH
MEM_LEN = 256
XA_HEADS = 4
XA_HEAD_DIM = D_MODEL // XA_HEADS
N_GROUPS = 4
EXPERTS_PER_GROUP = 8
N_EXPERTS = 32
TOP_K = 2
D_EXPERT = 512
EPS = 1e-6

FFT_N1 = 16
FFT_N2 = 256
FFT_K1_PER_STEP = 4

LANES = 128
GRANULE = 8
MIX_ROWS = 512
TRUNK_ROWS = 512
LOCAL_ROWS = TOP_K * TRUNK_ROWS + N_EXPERTS * GRANULE
LOCAL_GRANULES = LOCAL_ROWS // GRANULE
EXPERT_ROWS = 256
ROUTER_ROWS = 128
NEG_BIG = -1e30
HI16 = -65536

VMEM_LIMIT = 56 * 1024 * 1024


def _rms(x, g):
    return x * lax.rsqrt(jnp.mean(x * x, axis=-1, keepdims=True) + EPS) * g


def _dot(a, b):
    return jnp.dot(a, b, preferred_element_type=F32)


def _dot_nt(a, b):
    return lax.dot_general(a, b, (((1,), (1,)), ((), ())), preferred_element_type=F32)


def _pack_halves(left_f32, right_f32):
    lb = lax.bitcast_convert_type(left_f32, I32)
    rb = lax.shift_right_logical(lax.bitcast_convert_type(right_f32, I32), jnp.int32(16))
    return lb | rb


def _unpack_halves(packed_i32):
    left = lax.bitcast_convert_type(packed_i32 & jnp.int32(HI16), F32)
    right = lax.bitcast_convert_type(lax.shift_left(packed_i32, jnp.int32(16)), F32)
    return left.astype(BF16), right.astype(BF16)


def _group_mean_matrix():
    g = np.kron(np.eye(FFT_CH // HEAD_DIM), np.full((HEAD_DIM, HEAD_DIM), 1.0 / HEAD_DIM))
    return jnp.asarray(g, dtype=BF16)


def _fft_stage2_matrices(seq):
    k1 = np.arange(FFT_N1)[:, None, None]
    k2 = np.arange(FFT_N2)[None, :, None]
    s2 = np.arange(FFT_N2)[None, None, :]
    ang = 2.0 * np.pi * ((s2 * (k1 + FFT_N1 * k2)) % seq) / seq
    c, s = np.cos(ang), np.sin(ang)
    top = np.concatenate([c, s], axis=2)
    bot = np.concatenate([-s, c], axis=2)
    return jnp.asarray(np.concatenate([top, bot], axis=1), dtype=BF16)


def _fft_channel_matrix(seq):
    c = np.arange(HEAD_DIM)
    ang = 2.0 * np.pi * ((c[:, None] * c[None, :]) % HEAD_DIM) / HEAD_DIM
    scale = 1.0 / math.sqrt(seq * HEAD_DIM)
    eye = np.eye(FFT_CH // HEAD_DIM)
    cs = np.concatenate([np.kron(eye, np.cos(ang)), np.kron(eye, np.sin(ang))], axis=0) * scale
    return jnp.asarray(cs, dtype=BF16)


def _strict_upper(n):
    return jnp.asarray(np.triu(np.ones((n, n)), k=1), dtype=BF16)


def _strict_lower(n):
    return jnp.asarray(np.tril(np.ones((n, n)), k=-1), dtype=BF16)


def _kv_kernel(mem_ref, g_ref, w_ref, o_ref):
    h = _rms(mem_ref[...], g_ref[...]).astype(BF16)
    o_ref[...] = _dot(h, w_ref[...].astype(BF16)).astype(BF16)


def _kv_proj(mem2d, g, w_kv):
    rows = mem2d.shape[0]
    cols = w_kv.shape[1]
    cb = 512
    return pl.pallas_call(
        _kv_kernel,
        grid=(cols // cb,),
        in_specs=[
            pl.BlockSpec((rows, D_MODEL), lambda j: (0, 0)),
            pl.BlockSpec((1, D_MODEL), lambda j: (0, 0)),
            pl.BlockSpec((D_MODEL, cb), lambda j: (0, j)),
        ],
        out_specs=pl.BlockSpec((rows, cb), lambda j: (0, j)),
        out_shape=jax.ShapeDtypeStruct((rows, cols), BF16),
        compiler_params=pltpu.CompilerParams(vmem_limit_bytes=VMEM_LIMIT),
        name="kv_proj",
    )(mem2d, g, w_kv)


def _mixer_in_kernel(x_ref, xp_ref, xn_ref, g_ref, w_ref, cw_ref, cb_ref, hg_ref, gm_ref,
                     conv_ref, uf_ref):
    i = pl.program_id(1)
    n_i = pl.num_programs(1)
    rows = x_ref.shape[0]
    g = g_ref[...]
    h = _rms(x_ref[...], g).astype(BF16)
    u = _dot(h, w_ref[...])
    b_gate = u[:, :CONV_CH]
    cv = u[:, CONV_CH:2 * CONV_CH] * u[:, 2 * CONV_CH:3 * CONV_CH]
    uf_ref[...] = u[:, 3 * CONV_CH:].astype(BF16)

    hh = jnp.concatenate([_rms(xp_ref[...], g), _rms(xn_ref[...], g)], axis=0).astype(BF16)
    uh = _dot(hh, w_ref[:, CONV_CH:3 * CONV_CH])
    cvh = uh[:, :CONV_CH] * uh[:, CONV_CH:]
    cv_prev = cvh[7:8, :] * jnp.where(i == 0, 0.0, 1.0)
    cv_next = cvh[8:9, :] * jnp.where(i == n_i - 1, 0.0, 1.0)

    row = lax.broadcasted_iota(I32, cv.shape, 0)
    cv_up = jnp.where(row == 0, cv_prev, pltpu.roll(cv, 1, 0))
    cv_dn = jnp.where(row == rows - 1, cv_next, pltpu.roll(cv, rows - 1, 0))
    z = cw_ref[0:1, :] * cv_up + cw_ref[1:2, :] * cv + cw_ref[2:3, :] * cv_dn + cb_ref[...]
    y = b_gate * z
    ms = _dot((y * y).astype(BF16), gm_ref[...])
    conv_ref[...] = (y * lax.rsqrt(ms + EPS) * hg_ref[...]).astype(BF16)


def _mixer_in(x2d, g, w_in, conv_w, conv_b, hg_conv, gm, batch, seq):
    n_i = seq // MIX_ROWS
    t = x2d.shape[0]
    r8 = MIX_ROWS // 8
    last8 = t // 8 - 1
    return pl.pallas_call(
        _mixer_in_kernel,
        grid=(batch, n_i),
        in_specs=[
            pl.BlockSpec((MIX_ROWS, D_MODEL), lambda b, i: (b * n_i + i, 0)),
            pl.BlockSpec((8, D_MODEL), lambda b, i: (jnp.maximum((b * n_i + i) * r8 - 1, 0), 0)),
            pl.BlockSpec((8, D_MODEL), lambda b, i: (jnp.minimum((b * n_i + i + 1) * r8, last8), 0)),
            pl.BlockSpec((1, D_MODEL), lambda b, i: (0, 0)),
            pl.BlockSpec((D_MODEL, IN_COLS), lambda b, i: (0, 0)),
            pl.BlockSpec((3, CONV_CH), lambda b, i: (0, 0)),
            pl.BlockSpec((1, CONV_CH), lambda b, i: (0, 0)),
            pl.BlockSpec((1, CONV_CH), lambda b, i: (0, 0)),
            pl.BlockSpec((CONV_CH, CONV_CH), lambda b, i: (0, 0)),
        ],
        out_specs=[
            pl.BlockSpec((MIX_ROWS, CONV_CH), lambda b, i: (b * n_i + i, 0)),
            pl.BlockSpec((MIX_ROWS, FFT_CH), lambda b, i: (b * n_i + i, 0)),
        ],
        out_shape=[
            jax.ShapeDtypeStruct((t, CONV_CH), BF16),
            jax.ShapeDtypeStruct((t, FFT_CH), BF16),
        ],
        compiler_params=pltpu.CompilerParams(vmem_limit_bytes=VMEM_LIMIT),
        name="mixer_in",
    )(x2d, x2d, x2d, g, w_in, conv_w, conv_b, hg_conv, gm)


_S1_ROWS = 16
_S1_LANES = 128


def _lincomb(terms):
    acc = None
    for coef, val in terms:
        if abs(coef) < 1e-12:
            continue
        if abs(coef - 1.0) < 1e-12:
            term, neg = val, False
        elif abs(coef + 1.0) < 1e-12:
            term, neg = val, True
        else:
            term, neg = coef * val, False
        if acc is None:
            acc = -term if neg else term
        else:
            acc = acc - term if neg else acc + term
    return acc


def _fft_stage1(x_ref, a_ref):
    half = FFT_N1 // 2
    cos = [[math.cos(2 * math.pi * ((k * j) % FFT_N1) / FFT_N1) for j in range(FFT_N1)] for k in range(FFT_N1)]
    sin = [[math.sin(2 * math.pi * ((k * j) % FFT_N1) / FFT_N1) for j in range(FFT_N1)] for k in range(FFT_N1)]

    def body(r, carry):
        r0 = pl.multiple_of(r * _S1_ROWS, _S1_ROWS)
        rows_re = pl.ds(r0, _S1_ROWS)
        rows_im = pl.ds(r0 + FFT_N2, _S1_ROWS)
        for lc in range(0, FFT_CH, _S1_LANES):
            lanes = slice(lc, lc + _S1_LANES)
            xs = [x_ref[j, rows_re, lanes].astype(F32) for j in range(FFT_N1)]
            ev = [None] + [xs[j] + xs[FFT_N1 - j] for j in range(1, half)]
            od = [None] + [xs[j] - xs[FFT_N1 - j] for j in range(1, half)]
            for k in range(half + 1):
                re = _lincomb([(1.0, xs[0]), (cos[k][half], xs[half])]
                              + [(cos[k][j], ev[j]) for j in range(1, half)])
                a_ref[k, rows_re, lanes] = re.astype(BF16)
                if k in (0, half):
                    zero = jnp.zeros_like(re).astype(BF16)
                    a_ref[k, rows_im, lanes] = zero
                else:
                    im = _lincomb([(-sin[k][j], od[j]) for j in range(1, half)])
                    a_ref[k, rows_im, lanes] = im.astype(BF16)
                    a_ref[FFT_N1 - k, rows_re, lanes] = re.astype(BF16)
                    a_ref[FFT_N1 - k, rows_im, lanes] = (-im).astype(BF16)
        return carry

    lax.fori_loop(0, FFT_N2 // _S1_ROWS, body, 0)


def _fourier_kernel(x_ref, m2_ref, cs_ref, gm_ref, hg_ref, o_ref, a_ref, y_ref):
    j = pl.program_id(1)

    @pl.when(j == 0)
    def _():
        _fft_stage1(x_ref, a_ref)

    for kk in range(FFT_K1_PER_STEP):
        k1 = j * FFT_K1_PER_STEP + kk
        ri = _dot(m2_ref[kk], a_ref[k1])
        re = ri[:FFT_N2].astype(BF16)
        im = ri[FFT_N2:].astype(BF16)
        y = _dot(re, cs_ref[:FFT_CH, :]) + _dot(im, cs_ref[FFT_CH:, :])
        ms = _dot((y * y).astype(BF16), gm_ref[...])
        yn = y * lax.rsqrt(ms + EPS) * hg_ref[...]
        for c in range(FFT_CH // LANES):
            y_ref[c, pl.ds(k1, FFT_N2, stride=FFT_N1), :] = yn[:, c * LANES:(c + 1) * LANES]

    @pl.when(j == pl.num_programs(1) - 1)
    def _():
        for c in range(FFT_CH // LANES):
            o_ref[:, c * LANES:(c + 1) * LANES] = y_ref[c].astype(BF16)


def _fourier(uf, m2, cs, gm, hg_fft, batch, seq):
    assert seq == FFT_N1 * FFT_N2
    x4 = uf.reshape(batch, FFT_N1, FFT_N2, FFT_CH)
    out = pl.pallas_call(
        _fourier_kernel,
        grid=(batch, FFT_N1 // FFT_K1_PER_STEP),
        in_specs=[
            pl.BlockSpec((None, FFT_N1, FFT_N2, FFT_CH), lambda b, j: (b, 0, 0, 0)),
            pl.BlockSpec((FFT_K1_PER_STEP, 2 * FFT_N2, 2 * FFT_N2), lambda b, j: (j, 0, 0)),
            pl.BlockSpec((2 * FFT_CH, FFT_CH), lambda b, j: (0, 0)),
            pl.BlockSpec((FFT_CH, FFT_CH), lambda b, j: (0, 0)),
            pl.BlockSpec((1, FFT_CH), lambda b, j: (0, 0)),
        ],
        out_specs=pl.BlockSpec((seq, FFT_CH), lambda b, j: (b, 0)),
        out_shape=jax.ShapeDtypeStruct((batch * seq, FFT_CH), BF16),
        scratch_shapes=[
            pltpu.VMEM((FFT_N1, 2 * FFT_N2, FFT_CH), BF16),
            pltpu.VMEM((FFT_CH // LANES, seq, LANES), F32),
        ],
        compiler_params=pltpu.CompilerParams(
            dimension_semantics=("arbitrary", "arbitrary"), vmem_limit_bytes=VMEM_LIMIT),
        name="fourier",
    )(x4, m2, cs, gm, hg_fft)
    return out


def _first_index_of_max(vals, vmax, row):
    return jnp.min(jnp.where(vals == vmax, row, vals.shape[0]), axis=0, keepdims=True)


def _route(lg):
    cols = lg.shape[1]
    row8 = lax.broadcasted_iota(I32, (EXPERTS_PER_GROUP, cols), 0)
    gl = lg[0:8, :]
    gmax = jnp.max(gl, axis=0, keepdims=True)
    g_w = 1.0 / jnp.sum(jnp.exp(gl - gmax), axis=0, keepdims=True)
    g_idx = _first_index_of_max(gl, gmax, row8)

    el = lg[8:16, :]
    for g in range(1, N_GROUPS):
        el = jnp.where(g_idx == g, lg[8 + 8 * g:16 + 8 * g, :], el)
    emax = jnp.max(el, axis=0, keepdims=True)
    ee = jnp.exp(el - emax)
    e_prob = ee / jnp.sum(ee, axis=0, keepdims=True)
    p1 = jnp.max(e_prob, axis=0, keepdims=True)
    i1 = _first_index_of_max(e_prob, p1, row8)
    rest = jnp.where(row8 == i1, -1.0, e_prob)
    p2 = jnp.max(rest, axis=0, keepdims=True)
    i2 = _first_index_of_max(rest, p2, row8)
    denom = p1 + p2
    e1 = g_idx * EXPERTS_PER_GROUP + i1
    e2 = g_idx * EXPERTS_PER_GROUP + i2
    return e1, e2, g_w * p1 / denom, g_w * p2 / denom


def _trunk_kernel(x_ref, conv_ref, fft_ref, wout_ref, gxa_ref, wq_ref, k_ref, v_ref, wo_ref,
                  gffn_ref, wr_ref, br_ref, tri_ref, ltri_ref,
                  x2_ref, xs_ref, pos_ref, gate_ref, cnt_ref):
    x1 = x_ref[...] + _dot(conv_ref[...], wout_ref[:CONV_CH, :]) + _dot(fft_ref[...], wout_ref[CONV_CH:, :])
    h2 = _rms(x1, gxa_ref[...]).astype(BF16)
    q = _dot(h2, wq_ref[...]).astype(BF16)
    outs = []
    for hd in range(XA_HEADS):
        cols = slice(hd * XA_HEAD_DIM, (hd + 1) * XA_HEAD_DIM)
        s = _dot_nt(q[:, cols], k_ref[:, cols]) * (XA_HEAD_DIM ** -0.5)
        s = s - jnp.max(s, axis=-1, keepdims=True)
        p = jnp.exp(s)
        p = p / jnp.sum(p, axis=-1, keepdims=True)
        outs.append(_dot(p.astype(BF16), v_ref[:, cols]).astype(BF16))
    o = jnp.concatenate(outs, axis=-1)
    x2 = x1 + _dot(o, wo_ref[...])
    x2_ref[...] = x2
    h3 = _rms(x2, gffn_ref[...]).astype(BF16)
    lg = _dot_nt(wr_ref[...], h3) + br_ref[...]

    e1, e2, gate1, gate2 = _route(lg)
    n_tok = lg.shape[1]
    row32 = lax.broadcasted_iota(I32, (N_EXPERTS, n_tok), 0)
    hit1 = row32 == e1
    hit2 = row32 == e2
    onehot = jnp.where(hit1 | hit2, 1.0, 0.0)
    before = _dot(onehot.astype(BF16), tri_ref[...])
    cnt = jnp.sum(onehot, axis=1, keepdims=True).astype(I32)
    piece = jnp.left_shift(jnp.right_shift(cnt + (GRANULE - 1), 3), 3)
    piece_b = jnp.broadcast_to(piece.astype(F32), (N_EXPERTS, LANES)).astype(BF16)
    start = _dot(ltri_ref[...], piece_b)[:, 0:1]
    slot = before + start
    pos1 = jnp.sum(jnp.where(hit1, slot, 0.0), axis=0, keepdims=True).astype(I32)
    pos2 = jnp.sum(jnp.where(hit2, slot, 0.0), axis=0, keepdims=True).astype(I32)

    r = lax.broadcasted_iota(I32, (LOCAL_ROWS, n_tok), 0)
    perm = jnp.where((r == pos1) | (r == pos2), 1.0, 0.0).astype(BF16)
    xs_ref[...] = _pack_halves(_dot(perm, h3[:, :HALF]), _dot(perm, h3[:, HALF:]))

    pos_ref[0:1, :] = pos1
    pos_ref[1:2, :] = pos2
    pos_ref[2:8, :] = jnp.zeros((6, n_tok), I32)
    gate_ref[0:1, :] = gate1
    gate_ref[1:2, :] = gate2
    gate_ref[2:8, :] = jnp.zeros((6, n_tok), F32)
    cnt_ref[...] = jnp.broadcast_to(cnt, (N_EXPERTS, LANES))


def _trunk(x2d, conv_n, fft_n, w_out, g_xa, w_q, kv, w_o, g_ffn, w_r_t, b_r, tri, ltri, seq):
    t = x2d.shape[0]
    n_tiles = t // TRUNK_ROWS
    n_per_batch = seq // TRUNK_ROWS
    const = lambda i: (0, 0)
    return pl.pallas_call(
        _trunk_kernel,
        grid=(n_tiles,),
        in_specs=[
            pl.BlockSpec((TRUNK_ROWS, D_MODEL), lambda i: (i, 0)),
            pl.BlockSpec((TRUNK_ROWS, CONV_CH), lambda i: (i, 0)),
            pl.BlockSpec((TRUNK_ROWS, FFT_CH), lambda i: (i, 0)),
            pl.BlockSpec((D_MODEL, D_MODEL), const),
            pl.BlockSpec((1, D_MODEL), const),
            pl.BlockSpec((D_MODEL, D_MODEL), const),
            pl.BlockSpec((MEM_LEN, D_MODEL), lambda i: (i // n_per_batch, 0)),
            pl.BlockSpec((MEM_LEN, D_MODEL), lambda i: (i // n_per_batch, 1)),
            pl.BlockSpec((D_MODEL, D_MODEL), const),
            pl.BlockSpec((1, D_MODEL), const),
            pl.BlockSpec((ROUTER_ROWS, D_MODEL), const),
            pl.BlockSpec((ROUTER_ROWS, 1), const),
            pl.BlockSpec((TRUNK_ROWS, TRUNK_ROWS), const),
            pl.BlockSpec((N_EXPERTS, N_EXPERTS), const),
        ],
        out_specs=[
            pl.BlockSpec((TRUNK_ROWS, D_MODEL), lambda i: (i, 0)),
            pl.BlockSpec((LOCAL_ROWS, HALF), lambda i: (i, 0)),
            pl.BlockSpec((8, TRUNK_ROWS), lambda i: (0, i)),
            pl.BlockSpec((8, TRUNK_ROWS), lambda i: (0, i)),
            pl.BlockSpec((None, N_EXPERTS, LANES), lambda i: (i, 0, 0)),
        ],
        out_shape=[
            jax.ShapeDtypeStruct((t, D_MODEL), F32),
            jax.ShapeDtypeStruct((n_tiles * LOCAL_ROWS, HALF), I32),
            jax.ShapeDtypeStruct((8, t), I32),
            jax.ShapeDtypeStruct((8, t), F32),
            jax.ShapeDtypeStruct((n_tiles, N_EXPERTS, LANES), I32),
        ],
        compiler_params=pltpu.CompilerParams(vmem_limit_bytes=VMEM_LIMIT),
        name="trunk",
    )(x2d, conv_n, fft_n, w_out, g_xa, w_q, kv, kv, w_o, g_ffn, w_r_t, b_r, tri, ltri)


def _granule(ref, row):
    return ref.at[pl.ds(pl.multiple_of(row, GRANULE), GRANULE), :]


def _experts_kernel(blk_e_ref, first_ref, next_e_ref, slot_ref, nblk_ref, row_ref, ngran_ref,
                    xs_hbm, wg_hbm, wu_hbm, wd_hbm, ys_hbm,
                    wg_buf, wu_buf, wd_buf, xbuf, obuf, wsems, xsems, osems):
    i = pl.program_id(0)
    n = nblk_ref[0]
    gpb = EXPERT_ROWS // GRANULE

    def fetch_weights(e, s):
        return (pltpu.make_async_copy(wg_hbm.at[e], wg_buf.at[s], wsems.at[0, s]),
                pltpu.make_async_copy(wu_hbm.at[e], wu_buf.at[s], wsems.at[1, s]),
                pltpu.make_async_copy(wd_hbm.at[e], wd_buf.at[s], wsems.at[2, s]))

    def start_in(b, s):
        for g in range(gpb):
            @pl.when(g < ngran_ref[b])
            def _():
                pltpu.make_async_copy(_granule(xs_hbm, row_ref[b * gpb + g]),
                                      xbuf.at[s, pl.ds(g * GRANULE, GRANULE), :], xsems.at[s]).start()

    def start_out(b, s):
        for g in range(gpb):
            @pl.when(g < ngran_ref[b])
            def _():
                pltpu.make_async_copy(obuf.at[s, pl.ds(g * GRANULE, GRANULE), :],
                                      _granule(ys_hbm, row_ref[b * gpb + g]), osems.at[s]).start()

    def wait_in(b, s):
        rows = ngran_ref[b] * GRANULE
        pltpu.make_async_copy(xs_hbm.at[pl.ds(0, rows), :], xbuf.at[s, pl.ds(0, rows), :], xsems.at[s]).wait()

    def wait_out(b, s):
        rows = ngran_ref[b] * GRANULE
        pltpu.make_async_copy(obuf.at[s, pl.ds(0, rows), :], ys_hbm.at[pl.ds(0, rows), :], osems.at[s]).wait()

    @pl.when(i == 0)
    def _():
        xbuf[...] = jnp.zeros_like(xbuf)
        for cp in fetch_weights(blk_e_ref[0], 0):
            cp.start()
        start_in(0, 0)

    @pl.when(i < n)
    def _():
        s = i % 2
        ws = slot_ref[i]

        @pl.when(i + 1 < n)
        def _():
            start_in(i + 1, 1 - s)

        @pl.when(first_ref[i] == 1)
        def _():
            for cp in fetch_weights(blk_e_ref[i], ws):
                cp.wait()

            @pl.when(next_e_ref[i] >= 0)
            def _():
                for cp in fetch_weights(next_e_ref[i], 1 - ws):
                    cp.start()

        wait_in(i, s)
        xl, xr = _unpack_halves(xbuf[s])
        a = _dot(xl, wg_buf[ws, :HALF, :].astype(BF16)) + _dot(xr, wg_buf[ws, HALF:, :].astype(BF16))
        b = _dot(xl, wu_buf[ws, :HALF, :].astype(BF16)) + _dot(xr, wu_buf[ws, HALF:, :].astype(BF16))
        hmid = (a * jax.nn.sigmoid(a) * b).astype(BF16)
        y = _dot(hmid, wd_buf[ws].astype(BF16))

        @pl.when(i >= 2)
        def _():
            wait_out(i - 2, s)

        obuf[s] = _pack_halves(y[:, :HALF].astype(BF16).astype(F32), y[:, HALF:].astype(BF16).astype(F32))
        start_out(i, s)

        @pl.when(i == n - 1)
        def _():
            @pl.when(i >= 1)
            def _():
                wait_out(i - 1, 1 - s)
            wait_out(i, s)


def _experts(plan, xs_loc, w_gate, w_up, w_down):
    n_blk = plan["blk_e"].shape[0]
    hbm = pl.BlockSpec(memory_space=pl.ANY)
    return pl.pallas_call(
        _experts_kernel,
        grid_spec=pltpu.PrefetchScalarGridSpec(
            num_scalar_prefetch=7,
            grid=(n_blk,),
            in_specs=[hbm, hbm, hbm, hbm],
            out_specs=hbm,
            scratch_shapes=[
                pltpu.VMEM((2, D_MODEL, D_EXPERT), F32),
                pltpu.VMEM((2, D_MODEL, D_EXPERT), F32),
                pltpu.VMEM((2, D_EXPERT, D_MODEL), F32),
                pltpu.VMEM((2, EXPERT_ROWS, HALF), I32),
                pltpu.VMEM((2, EXPERT_ROWS, HALF), I32),
                pltpu.SemaphoreType.DMA((3, 2)),
                pltpu.SemaphoreType.DMA((2,)),
                pltpu.SemaphoreType.DMA((2,)),
            ],
        ),
        out_shape=jax.ShapeDtypeStruct(xs_loc.shape, I32),
        input_output_aliases={7: 0},
        compiler_params=pltpu.CompilerParams(
            dimension_semantics=("arbitrary",), vmem_limit_bytes=VMEM_LIMIT),
        name="experts",
    )(plan["blk_e"], plan["first"], plan["next_e"], plan["slot"], plan["nblk"], plan["row"], plan["ngran"],
      xs_loc, w_gate, w_up, w_down)


def _combine_kernel(x2_ref, pos_ref, gate_ref, g_ref, ys_ref, o_ref):
    n_tok = x2_ref.shape[0]
    pos = pos_ref[...]
    gates = gate_ref[...]
    col = lax.broadcasted_iota(I32, (n_tok, LOCAL_ROWS), 1)
    w = (jnp.where(col == pos[:, 0:1], gates[:, 0:1], 0.0)
         + jnp.where(col == pos[:, 1:2], gates[:, 1:2], 0.0)).astype(BF16)
    yl, yr = _unpack_halves(ys_ref[...])
    moe = jnp.concatenate([_dot(w, yl), _dot(w, yr)], axis=-1)
    o_ref[...] = _rms(x2_ref[...] + moe, g_ref[...])


def _combine(x2, pos_tk, gates_tk, g_final, ys_loc):
    t = x2.shape[0]
    return pl.pallas_call(
        _combine_kernel,
        grid=(t // TRUNK_ROWS,),
        in_specs=[
            pl.BlockSpec((TRUNK_ROWS, D_MODEL), lambda i: (i, 0)),
            pl.BlockSpec((TRUNK_ROWS, TOP_K), lambda i: (i, 0)),
            pl.BlockSpec((TRUNK_ROWS, TOP_K), lambda i: (i, 0)),
            pl.BlockSpec((1, D_MODEL), lambda i: (0, 0)),
            pl.BlockSpec((LOCAL_ROWS, HALF), lambda i: (i, 0)),
        ],
        out_specs=pl.BlockSpec((TRUNK_ROWS, D_MODEL), lambda i: (i, 0)),
        out_shape=jax.ShapeDtypeStruct((t, D_MODEL), F32),
        compiler_params=pltpu.CompilerParams(vmem_limit_bytes=VMEM_LIMIT),
        name="combine",
    )(x2, pos_tk, gates_tk, g_final, ys_loc)


def _router_params(w_rg, b_rg, w_re, b_re):
    w = jnp.zeros((ROUTER_ROWS, D_MODEL), F32)
    w = w.at[0:N_GROUPS].set(w_rg.T).at[8:8 + N_EXPERTS].set(w_re.T)
    b = jnp.zeros((ROUTER_ROWS,), F32)
    b = b.at[0:N_GROUPS].set(b_rg).at[N_GROUPS:8].set(NEG_BIG).at[8:8 + N_EXPERTS].set(b_re)
    return w.astype(BF16), b.reshape(ROUTER_ROWS, 1)


def _plan(cnt, n_global_rows):
    n_tiles = cnt.shape[0]
    piece = (cnt + GRANULE - 1) // GRANULE * GRANULE
    lend = jnp.cumsum(piece, axis=1)
    lstart = lend - piece
    tot = jnp.sum(piece, axis=0)
    padded = (tot + EXPERT_ROWS - 1) // EXPERT_ROWS * EXPERT_ROWS
    pend = jnp.cumsum(padded)
    pstart = pend - padded
    cum_tiles = jnp.cumsum(piece, axis=0)

    n_blk = n_global_rows // EXPERT_ROWS
    blk_start = jnp.arange(n_blk, dtype=I32) * EXPERT_ROWS
    blk_e = jnp.minimum(jnp.sum((pend[None, :] <= blk_start[:, None]).astype(I32), axis=1), N_EXPERTS - 1)
    nblk = pend[-1:] // EXPERT_ROWS

    of_blk_e = blk_e[:, None] == jnp.arange(N_EXPERTS, dtype=I32)[None, :]
    pick = lambda table: jnp.sum(jnp.where(of_blk_e[:, None, :], table[None, :, :], 0), axis=2)
    pick1 = lambda vec: jnp.sum(jnp.where(of_blk_e, vec[None, :], 0), axis=1)
    off = (blk_start - pick1(pstart))[:, None] + jnp.arange(EXPERT_ROWS // GRANULE, dtype=I32)[None, :] * GRANULE
    real = (off < pick1(tot)[:, None]) & (blk_start < pend[-1])[:, None]
    ngran = jnp.sum(real.astype(I32), axis=1)
    cum_b = pick(cum_tiles)
    tile_of = jnp.minimum(jnp.sum((cum_b[:, None, :] <= off[:, :, None]).astype(I32), axis=2), n_tiles - 1)
    base = (jnp.arange(n_tiles, dtype=I32) * LOCAL_ROWS)[None, :] + pick(lstart) - (cum_b - pick(piece))
    of_tile = tile_of[:, :, None] == jnp.arange(n_tiles, dtype=I32)
    row = jnp.sum(jnp.where(of_tile, base[:, None, :], 0), axis=2) + off
    row = jnp.where(real, row, 0)
    blk = jnp.arange(n_blk, dtype=I32)
    valid = blk < nblk
    change = jnp.concatenate([jnp.ones((1,), bool), blk_e[1:] != blk_e[:-1]])
    slot = (jnp.cumsum(change.astype(I32)) - 1) % 2
    later = (blk_e[None, :] > blk_e[:, None]) & valid[None, :]
    next_e = jnp.min(jnp.where(later, blk_e[None, :], N_EXPERTS), axis=1)
    next_e = jnp.where(next_e == N_EXPERTS, -1, next_e)
    as_i32 = lambda v: v.astype(I32)
    return dict(
        row=as_i32(row.reshape(-1)), ngran=as_i32(ngran), blk_e=as_i32(blk_e), nblk=as_i32(nblk),
        first=as_i32(change & valid), slot=as_i32(slot), next_e=as_i32(next_e),
    )


def _layer(x2d, mem2d, batch, seq, norm_mix_g, w_in, conv_w, conv_b, head_norm_g, w_out,
           norm_xa_g, norm_mem_g, w_q, w_kv, w_o, norm_ffn_g, w_rg, b_rg, w_re, b_re,
           w_gate, w_up, w_down, out_norm_g):
    t = x2d.shape[0]
    n_tiles = t // TRUNK_ROWS
    row = lambda v: v.reshape(1, -1)
    hg = head_norm_g.reshape(-1)
    gm = _group_mean_matrix()

    kv = _kv_proj(mem2d, row(norm_mem_g), w_kv)
    conv_n, uf = _mixer_in(x2d, row(norm_mix_g), w_in.astype(BF16), conv_w, row(conv_b),
                           row(hg[:CONV_CH]), gm, batch, seq)
    fft_n = _fourier(uf.reshape(batch, seq, FFT_CH), _fft_stage2_matrices(seq), _fft_channel_matrix(seq),
                     gm, row(hg[CONV_CH:]), batch, seq)
    w_r_t, b_r = _router_params(w_rg, b_rg, w_re, b_re)
    x2, xs_loc, pos, gates, cnt = _trunk(
        x2d, conv_n, fft_n, w_out.astype(BF16), row(norm_xa_g), w_q.astype(BF16), kv, w_o.astype(BF16),
        row(norm_ffn_g), w_r_t, b_r, _strict_upper(TRUNK_ROWS), _strict_lower(N_EXPERTS), seq)

    max_rows = n_tiles * LOCAL_ROWS + N_EXPERTS * (EXPERT_ROWS - GRANULE)
    n_global_rows = -(-max_rows // EXPERT_ROWS) * EXPERT_ROWS
    plan = _plan(cnt[:, :, 0], n_global_rows)
    ys_loc = _experts(plan, xs_loc, w_gate, w_up, w_down)
    return _combine(x2, pos[0:TOP_K].T, gates[0:TOP_K].T, row(out_norm_g), ys_loc)


def kernel(x, mem, norm_mix_g, w_in, conv_w, conv_b, head_norm_g, w_out, norm_xa_g, norm_mem_g, w_q, w_kv,
           w_o, norm_ffn_g, w_route_group, b_route_group, w_route_expert, b_route_expert, w_gate, w_up,
           w_down, final_norm_g):
    batch, seq, _ = x.shape
    depth = norm_mix_g.shape[0]
    assert depth == 1, "the final norm is fused into the last layer's combine kernel"
    x2d = x.reshape(batch * seq, D_MODEL)
    mem2d = mem.reshape(batch * MEM_LEN, D_MODEL)
    l = 0
    out = _layer(x2d, mem2d, batch, seq, norm_mix_g[l], w_in[l], conv_w[l], conv_b[l], head_norm_g[l],
                 w_out[l], norm_xa_g[l], norm_mem_g[l], w_q[l], w_kv[l], w_o[l], norm_ffn_g[l],
                 w_route_group[l], b_route_group[l], w_route_expert[l], b_route_expert[l],
                 w_gate[l], w_up[l], w_down[l], final_norm_g)
    return out.reshape(batch, seq, D_MODEL)
```

```python
import math

import numpy as np
import jax
import jax.numpy as jnp
from jax import lax
from jax.experimental import pallas as pl
from jax.experimental.pallas import tpu as pltpu

F32 = jnp.float32
BF16 = jnp.bfloat16
I32 = jnp.int32

D_MODEL = 1024
HALF = D_MODEL // 2
HEAD_DIM = 64
CONV_CH = 512
FFT_CH = 512
IN_COLS = 3 * CONV_CH + FFT_CH
MEM_LEN = 256
XA_HEADS = 4
XA_HEAD_DIM = D_MODEL // XA_HEADS
N_GROUPS = 4
EXPERTS_PER_GROUP = 8
N_EXPERTS = 32
TOP_K = 2
D_EXPERT = 512
EPS = 1e-6

FFT_N1 = 16
FFT_N2 = 256
FFT_K1_PER_STEP = 4

LANES = 128
GRANULE = 8
MIX_ROWS = 512
TRUNK_ROWS = 512
LOCAL_ROWS = TOP_K * TRUNK_ROWS + N_EXPERTS * GRANULE
LOCAL_GRANULES = LOCAL_ROWS // GRANULE
EXPERT_ROWS = 256
ROUTER_ROWS = 128
NEG_BIG = -1e30
HI16 = -65536

VMEM_LIMIT = 56 * 1024 * 1024


def _rms(x, g):
    return x * lax.rsqrt(jnp.mean(x * x, axis=-1, keepdims=True) + EPS) * g


def _dot(a, b):
    return jnp.dot(a, b, preferred_element_type=F32)


def _dot_nt(a, b):
    return lax.dot_general(a, b, (((1,), (1,)), ((), ())), preferred_element_type=F32)


def _pack_halves(left_f32, right_f32):
    lb = lax.bitcast_convert_type(left_f32, I32)
    rb = lax.shift_right_logical(lax.bitcast_convert_type(right_f32, I32), jnp.int32(16))
    return lb | rb


def _unpack_halves(packed_i32):
    left = lax.bitcast_convert_type(packed_i32 & jnp.int32(HI16), F32)
    right = lax.bitcast_convert_type(lax.shift_left(packed_i32, jnp.int32(16)), F32)
    return left.astype(BF16), right.astype(BF16)


def _group_mean_matrix():
    g = np.kron(np.eye(FFT_CH // HEAD_DIM), np.full((HEAD_DIM, HEAD_DIM), 1.0 / HEAD_DIM))
    return jnp.asarray(g, dtype=BF16)


def _fft_stage2_matrices(seq):
    k1 = np.arange(FFT_N1)[:, None, None]
    k2 = np.arange(FFT_N2)[None, :, None]
    s2 = np.arange(FFT_N2)[None, None, :]
    ang = 2.0 * np.pi * ((s2 * (k1 + FFT_N1 * k2)) % seq) / seq
    c, s = np.cos(ang), np.sin(ang)
    top = np.concatenate([c, s], axis=2)
    bot = np.concatenate([-s, c], axis=2)
    return jnp.asarray(np.concatenate([top, bot], axis=1), dtype=BF16)


def _fft_channel_matrix(seq):
    c = np.arange(HEAD_DIM)
    ang = 2.0 * np.pi * ((c[:, None] * c[None, :]) % HEAD_DIM) / HEAD_DIM
    scale = 1.0 / math.sqrt(seq * HEAD_DIM)
    eye = np.eye(FFT_CH // HEAD_DIM)
    cs = np.concatenate([np.kron(eye, np.cos(ang)), np.kron(eye, np.sin(ang))], axis=0) * scale
    return jnp.asarray(cs, dtype=BF16)


def _strict_upper(n):
    return jnp.asarray(np.triu(np.ones((n, n)), k=1), dtype=BF16)


def _strict_lower(n):
    return jnp.asarray(np.tril(np.ones((n, n)), k=-1), dtype=BF16)


def _kv_kernel(mem_ref, g_ref, w_ref, o_ref):
    h = _rms(mem_ref[...], g_ref[...]).astype(BF16)
    o_ref[...] = _dot(h, w_ref[...].astype(BF16)).astype(BF16)


def _kv_proj(mem2d, g, w_kv):
    rows = mem2d.shape[0]
    cols = w_kv.shape[1]
    cb = 512
    return pl.pallas_call(
        _kv_kernel,
        grid=(cols // cb,),
        in_specs=[
            pl.BlockSpec((rows, D_MODEL), lambda j: (0, 0)),
            pl.BlockSpec((1, D_MODEL), lambda j: (0, 0)),
            pl.BlockSpec((D_MODEL, cb), lambda j: (0, j)),
        ],
        out_specs=pl.BlockSpec((rows, cb), lambda j: (0, j)),
        out_shape=jax.ShapeDtypeStruct((rows, cols), BF16),
        compiler_params=pltpu.CompilerParams(vmem_limit_bytes=VMEM_LIMIT),
        name="kv_proj",
    )(mem2d, g, w_kv)


def _mixer_in_kernel(x_ref, xp_ref, xn_ref, g_ref, w_ref, cw_ref, cb_ref, hg_ref, gm_ref,
                     conv_ref, uf_ref):
    i = pl.program_id(1)
    n_i = pl.num_programs(1)
    rows = x_ref.shape[0]
    g = g_ref[...]
    h = _rms(x_ref[...], g).astype(BF16)
    u = _dot(h, w_ref[...])
    b_gate = u[:, :CONV_CH]
    cv = u[:, CONV_CH:2 * CONV_CH] * u[:, 2 * CONV_CH:3 * CONV_CH]
    uf_ref[...] = u[:, 3 * CONV_CH:].astype(BF16)

    hh = jnp.concatenate([_rms(xp_ref[...], g), _rms(xn_ref[...], g)], axis=0).astype(BF16)
    uh = _dot(hh, w_ref[:, CONV_CH:3 * CONV_CH])
    cvh = uh[:, :CONV_CH] * uh[:, CONV_CH:]
    cv_prev = cvh[7:8, :] * jnp.where(i == 0, 0.0, 1.0)
    cv_next = cvh[8:9, :] * jnp.where(i == n_i - 1, 0.0, 1.0)

    row = lax.broadcasted_iota(I32, cv.shape, 0)
    cv_up = jnp.where(row == 0, cv_prev, pltpu.roll(cv, 1, 0))
    cv_dn = jnp.where(row == rows - 1, cv_next, pltpu.roll(cv, rows - 1, 0))
    z = cw_ref[0:1, :] * cv_up + cw_ref[1:2, :] * cv + cw_ref[2:3, :] * cv_dn + cb_ref[...]
    y = b_gate * z
    ms = _dot((y * y).astype(BF16), gm_ref[...])
    conv_ref[...] = (y * lax.rsqrt(ms + EPS) * hg_ref[...]).astype(BF16)


def _mixer_in(x2d, g, w_in, conv_w, conv_b, hg_conv, gm, batch, seq):
    n_i = seq // MIX_ROWS
    t = x2d.shape[0]
    r8 = MIX_ROWS // 8
    last8 = t // 8 - 1
    return pl.pallas_call(
        _mixer_in_kernel,
        grid=(batch, n_i),
        in_specs=[
            pl.BlockSpec((MIX_ROWS, D_MODEL), lambda b, i: (b * n_i + i, 0)),
            pl.BlockSpec((8, D_MODEL), lambda b, i: (jnp.maximum((b * n_i + i) * r8 - 1, 0), 0)),
            pl.BlockSpec((8, D_MODEL), lambda b, i: (jnp.minimum((b * n_i + i + 1) * r8, last8), 0)),
            pl.BlockSpec((1, D_MODEL), lambda b, i: (0, 0)),
            pl.BlockSpec((D_MODEL, IN_COLS), lambda b, i: (0, 0)),
            pl.BlockSpec((3, CONV_CH), lambda b, i: (0, 0)),
            pl.BlockSpec((1, CONV_CH), lambda b, i: (0, 0)),
            pl.BlockSpec((1, CONV_CH), lambda b, i: (0, 0)),
            pl.BlockSpec((CONV_CH, CONV_CH), lambda b, i: (0, 0)),
        ],
        out_specs=[
            pl.BlockSpec((MIX_ROWS, CONV_CH), lambda b, i: (b * n_i + i, 0)),
            pl.BlockSpec((MIX_ROWS, FFT_CH), lambda b, i: (b * n_i + i, 0)),
        ],
        out_shape=[
            jax.ShapeDtypeStruct((t, CONV_CH), BF16),
            jax.ShapeDtypeStruct((t, FFT_CH), BF16),
        ],
        compiler_params=pltpu.CompilerParams(vmem_limit_bytes=VMEM_LIMIT),
        name="mixer_in",
    )(x2d, x2d, x2d, g, w_in, conv_w, conv_b, hg_conv, gm)


_S1_ROWS = 16
_S1_LANES = 128


def _lincomb(terms):
    acc = None
    for coef, val in terms:
        if abs(coef) < 1e-12:
            continue
        if abs(coef - 1.0) < 1e-12:
            term, neg = val, False
        elif abs(coef + 1.0) < 1e-12:
            term, neg = val, True
        else:
            term, neg = coef * val, False
        if acc is None:
            acc = -term if neg else term
        else:
            acc = acc - term if neg else acc + term
    return acc


def _fft_stage1(x_ref, a_ref):
    half = FFT_N1 // 2
    cos = [[math.cos(2 * math.pi * ((k * j) % FFT_N1) / FFT_N1) for j in range(FFT_N1)] for k in range(FFT_N1)]
    sin = [[math.sin(2 * math.pi * ((k * j) % FFT_N1) / FFT_N1) for j in range(FFT_N1)] for k in range(FFT_N1)]

    def body(r, carry):
        r0 = pl.multiple_of(r * _S1_ROWS, _S1_ROWS)
        rows_re = pl.ds(r0, _S1_ROWS)
        rows_im = pl.ds(r0 + FFT_N2, _S1_ROWS)
        for lc in range(0, FFT_CH, _S1_LANES):
            lanes = slice(lc, lc + _S1_LANES)
            xs = [x_ref[j, rows_re, lanes].astype(F32) for j in range(FFT_N1)]
            ev = [None] + [xs[j] + xs[FFT_N1 - j] for j in range(1, half)]
            od = [None] + [xs[j] - xs[FFT_N1 - j] for j in range(1, half)]
            for k in range(half + 1):
                re = _lincomb([(1.0, xs[0]), (cos[k][half], xs[half])]
                              + [(cos[k][j], ev[j]) for j in range(1, half)])
                a_ref[k, rows_re, lanes] = re.astype(BF16)
                if k in (0, half):
                    zero = jnp.zeros_like(re).astype(BF16)
                    a_ref[k, rows_im, lanes] = zero
                else:
                    im = _lincomb([(-sin[k][j], od[j]) for j in range(1, half)])
                    a_ref[k, rows_im, lanes] = im.astype(BF16)
                    a_ref[FFT_N1 - k, rows_re, lanes] = re.astype(BF16)
                    a_ref[FFT_N1 - k, rows_im, lanes] = (-im).astype(BF16)
        return carry

    lax.fori_loop(0, FFT_N2 // _S1_ROWS, body, 0)


def _fourier_kernel(x_ref, m2_ref, cs_ref, gm_ref, hg_ref, o_ref, a_ref, y_ref):
    j = pl.program_id(1)

    @pl.when(j == 0)
    def _():
        _fft_stage1(x_ref, a_ref)

    for kk in range(FFT_K1_PER_STEP):
        k1 = j * FFT_K1_PER_STEP + kk
        ri = _dot(m2_ref[kk], a_ref[k1])
        re = ri[:FFT_N2].astype(BF16)
        im = ri[FFT_N2:].astype(BF16)
        y = _dot(re, cs_ref[:FFT_CH, :]) + _dot(im, cs_ref[FFT_CH:, :])
        ms = _dot((y * y).astype(BF16), gm_ref[...])
        yn = y * lax.rsqrt(ms + EPS) * hg_ref[...]
        for c in range(FFT_CH // LANES):
            y_ref[c, pl.ds(k1, FFT_N2, stride=FFT_N1), :] = yn[:, c * LANES:(c + 1) * LANES]

    @pl.when(j == pl.num_programs(1) - 1)
    def _():
        for c in range(FFT_CH // LANES):
            o_ref[:, c * LANES:(c + 1) * LANES] = y_ref[c].astype(BF16)


def _fourier(uf, m2, cs, gm, hg_fft, batch, seq):
    assert seq == FFT_N1 * FFT_N2
    x4 = uf.reshape(batch, FFT_N1, FFT_N2, FFT_CH)
    out = pl.pallas_call(
        _fourier_kernel,
        grid=(batch, FFT_N1 // FFT_K1_PER_STEP),
        in_specs=[
            pl.BlockSpec((None, FFT_N1, FFT_N2, FFT_CH), lambda b, j: (b, 0, 0, 0)),
            pl.BlockSpec((FFT_K1_PER_STEP, 2 * FFT_N2, 2 * FFT_N2), lambda b, j: (j, 0, 0)),
            pl.BlockSpec((2 * FFT_CH, FFT_CH), lambda b, j: (0, 0)),
            pl.BlockSpec((FFT_CH, FFT_CH), lambda b, j: (0, 0)),
            pl.BlockSpec((1, FFT_CH), lambda b, j: (0, 0)),
        ],
        out_specs=pl.BlockSpec((seq, FFT_CH), lambda b, j: (b, 0)),
        out_shape=jax.ShapeDtypeStruct((batch * seq, FFT_CH), BF16),
        scratch_shapes=[
            pltpu.VMEM((FFT_N1, 2 * FFT_N2, FFT_CH), BF16),
            pltpu.VMEM((FFT_CH // LANES, seq, LANES), F32),
        ],
        compiler_params=pltpu.CompilerParams(
            dimension_semantics=("arbitrary", "arbitrary"), vmem_limit_bytes=VMEM_LIMIT),
        name="fourier",
    )(x4, m2, cs, gm, hg_fft)
    return out


def _first_index_of_max(vals, vmax, row):
    return jnp.min(jnp.where(vals == vmax, row, vals.shape[0]), axis=0, keepdims=True)


def _route(lg):
    cols = lg.shape[1]
    row8 = lax.broadcasted_iota(I32, (EXPERTS_PER_GROUP, cols), 0)
    gl = lg[0:8, :]
    gmax = jnp.max(gl, axis=0, keepdims=True)
    g_w = 1.0 / jnp.sum(jnp.exp(gl - gmax), axis=0, keepdims=True)
    g_idx = _first_index_of_max(gl, gmax, row8)

    el = lg[8:16, :]
    for g in range(1, N_GROUPS):
        el = jnp.where(g_idx == g, lg[8 + 8 * g:16 + 8 * g, :], el)
    emax = jnp.max(el, axis=0, keepdims=True)
    ee = jnp.exp(el - emax)
    e_prob = ee / jnp.sum(ee, axis=0, keepdims=True)
    p1 = jnp.max(e_prob, axis=0, keepdims=True)
    i1 = _first_index_of_max(e_prob, p1, row8)
    rest = jnp.where(row8 == i1, -1.0, e_prob)
    p2 = jnp.max(rest, axis=0, keepdims=True)
    i2 = _first_index_of_max(rest, p2, row8)
    denom = p1 + p2
    e1 = g_idx * EXPERTS_PER_GROUP + i1
    e2 = g_idx * EXPERTS_PER_GROUP + i2
    return e1, e2, g_w * p1 / denom, g_w * p2 / denom


def _trunk_kernel(x_ref, conv_ref, fft_ref, wout_ref, gxa_ref, wq_ref, k_ref, v_ref, wo_ref,
                  gffn_ref, wr_ref, br_ref, tri_ref, ltri_ref,
                  x2_ref, xs_ref, pos_ref, gate_ref, cnt_ref):
    x1 = x_ref[...] + _dot(conv_ref[...], wout_ref[:CONV_CH, :]) + _dot(fft_ref[...], wout_ref[CONV_CH:, :])
    h2 = _rms(x1, gxa_ref[...]).astype(BF16)
    q = _dot(h2, wq_ref[...]).astype(BF16)
    outs = []
    for hd in range(XA_HEADS):
        cols = slice(hd * XA_HEAD_DIM, (hd + 1) * XA_HEAD_DIM)
        s = _dot_nt(q[:, cols], k_ref[:, cols]) * (XA_HEAD_DIM ** -0.5)
        s = s - jnp.max(s, axis=-1, keepdims=True)
        p = jnp.exp(s)
        p = p / jnp.sum(p, axis=-1, keepdims=True)
        outs.append(_dot(p.astype(BF16), v_ref[:, cols]).astype(BF16))
    o = jnp.concatenate(outs, axis=-1)
    x2 = x1 + _dot(o, wo_ref[...])
    x2_ref[...] = x2
    h3 = _rms(x2, gffn_ref[...]).astype(BF16)
    lg = _dot_nt(wr_ref[...], h3) + br_ref[...]

    e1, e2, gate1, gate2 = _route(lg)
    n_tok = lg.shape[1]
    row32 = lax.broadcasted_iota(I32, (N_EXPERTS, n_tok), 0)
    hit1 = row32 == e1
    hit2 = row32 == e2
    onehot = jnp.where(hit1 | hit2, 1.0, 0.0)
    before = _dot(onehot.astype(BF16), tri_ref[...])
    cnt = jnp.sum(onehot, axis=1, keepdims=True).astype(I32)
    piece = jnp.left_shift(jnp.right_shift(cnt + (GRANULE - 1), 3), 3)
    piece_b = jnp.broadcast_to(piece.astype(F32), (N_EXPERTS, LANES)).astype(BF16)
    start = _dot(ltri_ref[...], piece_b)[:, 0:1]
    slot = before + start
    pos1 = jnp.sum(jnp.where(hit1, slot, 0.0), axis=0, keepdims=True).astype(I32)
    pos2 = jnp.sum(jnp.where(hit2, slot, 0.0), axis=0, keepdims=True).astype(I32)

    r = lax.broadcasted_iota(I32, (LOCAL_ROWS, n_tok), 0)
    perm = jnp.where((r == pos1) | (r == pos2), 1.0, 0.0).astype(BF16)
    xs_ref[...] = _pack_halves(_dot(perm, h3[:, :HALF]), _dot(perm, h3[:, HALF:]))

    pos_ref[0:1, :] = pos1
    pos_ref[1:2, :] = pos2
    pos_ref[2:8, :] = jnp.zeros((6, n_tok), I32)
    gate_ref[0:1, :] = gate1
    gate_ref[1:2, :] = gate2
    gate_ref[2:8, :] = jnp.zeros((6, n_tok), F32)
    cnt_ref[...] = jnp.broadcast_to(cnt, (N_EXPERTS, LANES))


def _trunk(x2d, conv_n, fft_n, w_out, g_xa, w_q, kv, w_o, g_ffn, w_r_t, b_r, tri, ltri, seq):
    t = x2d.shape[0]
    n_tiles = t // TRUNK_ROWS
    n_per_batch = seq // TRUNK_ROWS
    const = lambda i: (0, 0)
    return pl.pallas_call(
        _trunk_kernel,
        grid=(n_tiles,),
        in_specs=[
            pl.BlockSpec((TRUNK_ROWS, D_MODEL), lambda i: (i, 0)),
            pl.BlockSpec((TRUNK_ROWS, CONV_CH), lambda i: (i, 0)),
            pl.BlockSpec((TRUNK_ROWS, FFT_CH), lambda i: (i, 0)),
            pl.BlockSpec((D_MODEL, D_MODEL), const),
            pl.BlockSpec((1, D_MODEL), const),
            pl.BlockSpec((D_MODEL, D_MODEL), const),
            pl.BlockSpec((MEM_LEN, D_MODEL), lambda i: (i // n_per_batch, 0)),
            pl.BlockSpec((MEM_LEN, D_MODEL), lambda i: (i // n_per_batch, 1)),
            pl.BlockSpec((D_MODEL, D_MODEL), const),
            pl.BlockSpec((1, D_MODEL), const),
            pl.BlockSpec((ROUTER_ROWS, D_MODEL), const),
            pl.BlockSpec((ROUTER_ROWS, 1), const),
            pl.BlockSpec((TRUNK_ROWS, TRUNK_ROWS), const),
            pl.BlockSpec((N_EXPERTS, N_EXPERTS), const),
        ],
        out_specs=[
            pl.BlockSpec((TRUNK_ROWS, D_MODEL), lambda i: (i, 0)),
            pl.BlockSpec((LOCAL_ROWS, HALF), lambda i: (i, 0)),
            pl.BlockSpec((8, TRUNK_ROWS), lambda i: (0, i)),
            pl.BlockSpec((8, TRUNK_ROWS), lambda i: (0, i)),
            pl.BlockSpec((None, N_EXPERTS, LANES), lambda i: (i, 0, 0)),
        ],
        out_shape=[
            jax.ShapeDtypeStruct((t, D_MODEL), F32),
            jax.ShapeDtypeStruct((n_tiles * LOCAL_ROWS, HALF), I32),
            jax.ShapeDtypeStruct((8, t), I32),
            jax.ShapeDtypeStruct((8, t), F32),
            jax.ShapeDtypeStruct((n_tiles, N_EXPERTS, LANES), I32),
        ],
        compiler_params=pltpu.CompilerParams(vmem_limit_bytes=VMEM_LIMIT),
        name="trunk",
    )(x2d, conv_n, fft_n, w_out, g_xa, w_q, kv, kv, w_o, g_ffn, w_r_t, b_r, tri, ltri)


def _granule(ref, row):
    return ref.at[pl.ds(pl.multiple_of(row, GRANULE), GRANULE), :]


def _experts_kernel(blk_e_ref, first_ref, next_e_ref, slot_ref, nblk_ref, row_ref, ngran_ref,
                    xs_hbm, wg_hbm, wu_hbm, wd_hbm, ys_hbm,
                    wg_buf, wu_buf, wd_buf, xbuf0, xbuf1, obuf0, obuf1, wsems, xsems, osems):
    i = pl.program_id(0)
    n = nblk_ref[0]
    gpb = EXPERT_ROWS // GRANULE
    xbufs, obufs = (xbuf0, xbuf1), (obuf0, obuf1)

    def fetch_weights(e, s):
        return (pltpu.make_async_copy(wg_hbm.at[e], wg_buf.at[s], wsems.at[0, s]),
                pltpu.make_async_copy(wu_hbm.at[e], wu_buf.at[s], wsems.at[1, s]),
                pltpu.make_async_copy(wd_hbm.at[e], wd_buf.at[s], wsems.at[2, s]))

    def start_in(b, p, count):
        for g in range(gpb):
            @pl.when(g < count)
            def _():
                pltpu.make_async_copy(_granule(xs_hbm, row_ref[b * gpb + g]),
                                      xbufs[p].at[pl.ds(g * GRANULE, GRANULE), :], xsems.at[p]).start()

    def start_out(b, p, count):
        for g in range(gpb):
            @pl.when(g < count)
            def _():
                pltpu.make_async_copy(obufs[p].at[pl.ds(g * GRANULE, GRANULE), :],
                                      _granule(ys_hbm, row_ref[b * gpb + g]), osems.at[p]).start()

    def wait_in(b, p):
        rows = ngran_ref[b] * GRANULE
        pltpu.make_async_copy(xs_hbm.at[pl.ds(0, rows), :], xbufs[p].at[pl.ds(0, rows), :], xsems.at[p]).wait()

    def wait_out(b, p):
        rows = ngran_ref[b] * GRANULE
        pltpu.make_async_copy(obufs[p].at[pl.ds(0, rows), :], ys_hbm.at[pl.ds(0, rows), :], osems.at[p]).wait()

    @pl.when(i == 0)
    def _():
        for buf in xbufs:
            buf[...] = jnp.zeros_like(buf)
        for cp in fetch_weights(blk_e_ref[0], 0):
            cp.start()
        start_in(0, 0, ngran_ref[0])

    def step(p):
        ws = slot_ref[i]

        @pl.when(first_ref[i] == 1)
        def _():
            for cp in fetch_weights(blk_e_ref[i], ws):
                cp.wait()

            @pl.when(next_e_ref[i] >= 0)
            def _():
                for cp in fetch_weights(next_e_ref[i], 1 - ws):
                    cp.start()

        @pl.when(i >= 2)
        def _():
            wait_out(i - 2, p)

        prev_b = jnp.maximum(i - 1, 0)
        next_b = jnp.minimum(i + 1, pl.num_programs(0) - 1)
        prev_count = jnp.where(i >= 1, ngran_ref[prev_b], 0)
        next_count = jnp.where(i + 1 < n, ngran_ref[next_b], 0)

        wait_in(i, p)
        xl, xr = _unpack_halves(xbufs[p][...])
        a = _dot(xl, wg_buf[ws, :HALF, :].astype(BF16)) + _dot(xr, wg_buf[ws, HALF:, :].astype(BF16))
        b = _dot(xl, wu_buf[ws, :HALF, :].astype(BF16)) + _dot(xr, wu_buf[ws, HALF:, :].astype(BF16))
        hmid = (a * jax.nn.sigmoid(a) * b).astype(BF16)
        start_out(prev_b, 1 - p, prev_count)
        start_in(next_b, 1 - p, next_count)
        y = _dot(hmid, wd_buf[ws].astype(BF16))
        obufs[p][...] = _pack_halves(y[:, :HALF].astype(BF16).astype(F32), y[:, HALF:].astype(BF16).astype(F32))

        @pl.when(i == n - 1)
        def _():
            start_out(i, p, ngran_ref[i])

            @pl.when(i >= 1)
            def _():
                wait_out(i - 1, 1 - p)
            wait_out(i, p)

    for parity in range(2):
        pl.when((i < n) & (i % 2 == parity))(lambda parity=parity: step(parity))


def _experts(plan, xs_loc, w_gate, w_up, w_down):
    n_blk = plan["blk_e"].shape[0]
    hbm = pl.BlockSpec(memory_space=pl.ANY)
    return pl.pallas_call(
        _experts_kernel,
        grid_spec=pltpu.PrefetchScalarGridSpec(
            num_scalar_prefetch=7,
            grid=(n_blk,),
            in_specs=[hbm, hbm, hbm, hbm],
            out_specs=hbm,
            scratch_shapes=[
                pltpu.VMEM((2, D_MODEL, D_EXPERT), F32),
                pltpu.VMEM((2, D_MODEL, D_EXPERT), F32),
                pltpu.VMEM((2, D_EXPERT, D_MODEL), F32),
                pltpu.VMEM((EXPERT_ROWS, HALF), I32),
                pltpu.VMEM((EXPERT_ROWS, HALF), I32),
                pltpu.VMEM((EXPERT_ROWS, HALF), I32),
                pltpu.VMEM((EXPERT_ROWS, HALF), I32),
                pltpu.SemaphoreType.DMA((3, 2)),
                pltpu.SemaphoreType.DMA((2,)),
                pltpu.SemaphoreType.DMA((2,)),
            ],
        ),
        out_shape=jax.ShapeDtypeStruct(xs_loc.shape, I32),
        input_output_aliases={7: 0},
        compiler_params=pltpu.CompilerParams(
            dimension_semantics=("arbitrary",), vmem_limit_bytes=VMEM_LIMIT),
        name="experts",
    )(plan["blk_e"], plan["first"], plan["next_e"], plan["slot"], plan["nblk"], plan["row"], plan["ngran"],
      xs_loc, w_gate, w_up, w_down)


def _combine_kernel(x2_ref, pos_ref, gate_ref, g_ref, ys_ref, o_ref):
    n_tok = x2_ref.shape[0]
    pos = pos_ref[...]
    gates = gate_ref[...]
    col = lax.broadcasted_iota(I32, (n_tok, LOCAL_ROWS), 1)
    w = (jnp.where(col == pos[:, 0:1], gates[:, 0:1], 0.0)
         + jnp.where(col == pos[:, 1:2], gates[:, 1:2], 0.0)).astype(BF16)
    yl, yr = _unpack_halves(ys_ref[...])
    moe = jnp.concatenate([_dot(w, yl), _dot(w, yr)], axis=-1)
    o_ref[...] = _rms(x2_ref[...] + moe, g_ref[...])


def _combine(x2, pos_tk, gates_tk, g_final, ys_loc):
    t = x2.shape[0]
    return pl.pallas_call(
        _combine_kernel,
        grid=(t // TRUNK_ROWS,),
        in_specs=[
            pl.BlockSpec((TRUNK_ROWS, D_MODEL), lambda i: (i, 0)),
            pl.BlockSpec((TRUNK_ROWS, TOP_K), lambda i: (i, 0)),
            pl.BlockSpec((TRUNK_ROWS, TOP_K), lambda i: (i, 0)),
            pl.BlockSpec((1, D_MODEL), lambda i: (0, 0)),
            pl.BlockSpec((LOCAL_ROWS, HALF), lambda i: (i, 0)),
        ],
        out_specs=pl.BlockSpec((TRUNK_ROWS, D_MODEL), lambda i: (i, 0)),
        out_shape=jax.ShapeDtypeStruct((t, D_MODEL), F32),
        compiler_params=pltpu.CompilerParams(vmem_limit_bytes=VMEM_LIMIT),
        name="combine",
    )(x2, pos_tk, gates_tk, g_final, ys_loc)


def _router_params(w_rg, b_rg, w_re, b_re):
    w = jnp.zeros((ROUTER_ROWS, D_MODEL), F32)
    w = w.at[0:N_GROUPS].set(w_rg.T).at[8:8 + N_EXPERTS].set(w_re.T)
    b = jnp.zeros((ROUTER_ROWS,), F32)
    b = b.at[0:N_GROUPS].set(b_rg).at[N_GROUPS:8].set(NEG_BIG).at[8:8 + N_EXPERTS].set(b_re)
    return w.astype(BF16), b.reshape(ROUTER_ROWS, 1)


def _plan(cnt, n_global_rows):
    n_tiles = cnt.shape[0]
    piece = (cnt + GRANULE - 1) // GRANULE * GRANULE
    lend = jnp.cumsum(piece, axis=1)
    lstart = lend - piece
    tot = jnp.sum(piece, axis=0)
    padded = (tot + EXPERT_ROWS - 1) // EXPERT_ROWS * EXPERT_ROWS
    pend = jnp.cumsum(padded)
    pstart = pend - padded
    cum_tiles = jnp.cumsum(piece, axis=0)

    n_blk = n_global_rows // EXPERT_ROWS
    blk_start = jnp.arange(n_blk, dtype=I32) * EXPERT_ROWS
    blk_e = jnp.minimum(jnp.sum((pend[None, :] <= blk_start[:, None]).astype(I32), axis=1), N_EXPERTS - 1)
    nblk = pend[-1:] // EXPERT_ROWS

    of_blk_e = blk_e[:, None] == jnp.arange(N_EXPERTS, dtype=I32)[None, :]
    pick = lambda table: jnp.sum(jnp.where(of_blk_e[:, None, :], table[None, :, :], 0), axis=2)
    pick1 = lambda vec: jnp.sum(jnp.where(of_blk_e, vec[None, :], 0), axis=1)
    off = (blk_start - pick1(pstart))[:, None] + jnp.arange(EXPERT_ROWS // GRANULE, dtype=I32)[None, :] * GRANULE
    real = (off < pick1(tot)[:, None]) & (blk_start < pend[-1])[:, None]
    ngran = jnp.sum(real.astype(I32), axis=1)
    cum_b = pick(cum_tiles)
    tile_of = jnp.minimum(jnp.sum((cum_b[:, None, :] <= off[:, :, None]).astype(I32), axis=2), n_tiles - 1)
    base = (jnp.arange(n_tiles, dtype=I32) * LOCAL_ROWS)[None, :] + pick(lstart) - (cum_b - pick(piece))
    of_tile = tile_of[:, :, None] == jnp.arange(n_tiles, dtype=I32)
    row = jnp.sum(jnp.where(of_tile, base[:, None, :], 0), axis=2) + off
    row = jnp.where(real, row, 0)
    blk = jnp.arange(n_blk, dtype=I32)
    valid = blk < nblk
    change = jnp.concatenate([jnp.ones((1,), bool), blk_e[1:] != blk_e[:-1]])
    slot = (jnp.cumsum(change.astype(I32)) - 1) % 2
    later = (blk_e[None, :] > blk_e[:, None]) & valid[None, :]
    next_e = jnp.min(jnp.where(later, blk_e[None, :], N_EXPERTS), axis=1)
    next_e = jnp.where(next_e == N_EXPERTS, -1, next_e)
    as_i32 = lambda v: v.astype(I32)
    return dict(
        row=as_i32(row.reshape(-1)), ngran=as_i32(ngran), blk_e=as_i32(blk_e), nblk=as_i32(nblk),
        first=as_i32(change & valid), slot=as_i32(slot), next_e=as_i32(next_e),
    )


def _layer(x2d, mem2d, batch, seq, norm_mix_g, w_in, conv_w, conv_b, head_norm_g, w_out,
           norm_xa_g, norm_mem_g, w_q, w_kv, w_o, norm_ffn_g, w_rg, b_rg, w_re, b_re,
           w_gate, w_up, w_down, out_norm_g):
    t = x2d.shape[0]
    n_tiles = t // TRUNK_ROWS
    row = lambda v: v.reshape(1, -1)
    hg = head_norm_g.reshape(-1)
    gm = _group_mean_matrix()

    kv = _kv_proj(mem2d, row(norm_mem_g), w_kv)
    conv_n, uf = _mixer_in(x2d, row(norm_mix_g), w_in.astype(BF16), conv_w, row(conv_b),
                           row(hg[:CONV_CH]), gm, batch, seq)
    fft_n = _fourier(uf.reshape(batch, seq, FFT_CH), _fft_stage2_matrices(seq), _fft_channel_matrix(seq),
                     gm, row(hg[CONV_CH:]), batch, seq)
    w_r_t, b_r = _router_params(w_rg, b_rg, w_re, b_re)
    x2, xs_loc, pos, gates, cnt = _trunk(
        x2d, conv_n, fft_n, w_out.astype(BF16), row(norm_xa_g), w_q.astype(BF16), kv, w_o.astype(BF16),
        row(norm_ffn_g), w_r_t, b_r, _strict_upper(TRUNK_ROWS), _strict_lower(N_EXPERTS), seq)

    max_rows = n_tiles * LOCAL_ROWS + N_EXPERTS * (EXPERT_ROWS - GRANULE)
    n_global_rows = -(-max_rows // EXPERT_ROWS) * EXPERT_ROWS
    plan = _plan(cnt[:, :, 0], n_global_rows)
    ys_loc = _experts(plan, xs_loc, w_gate, w_up, w_down)
    return _combine(x2, pos[0:TOP_K].T, gates[0:TOP_K].T, row(out_norm_g), ys_loc)


def kernel(x, mem, norm_mix_g, w_in, conv_w, conv_b, head_norm_g, w_out, norm_xa_g, norm_mem_g, w_q, w_kv,
           w_o, norm_ffn_g, w_route_group, b_route_group, w_route_expert, b_route_expert, w_gate, w_up,
           w_down, final_norm_g):
    batch, seq, _ = x.shape
    depth = norm_mix_g.shape[0]
    assert depth == 1, "the final norm is fused into the last layer's combine kernel"
    x2d = x.reshape(batch * seq, D_MODEL)
    mem2d = mem.reshape(batch * MEM_LEN, D_MODEL)
    l = 0
    out = _layer(x2d, mem2d, batch, seq, norm_mix_g[l], w_in[l], conv_w[l], conv_b[l], head_norm_g[l],
                 w_out[l], norm_xa_g[l], norm_mem_g[l], w_q[l], w_kv[l], w_o[l], norm_ffn_g[l],
                 w_route_group[l], b_route_group[l], w_route_expert[l], b_route_expert[l],
                 w_gate[l], w_up[l], w_down[l], final_norm_g)
    return out.reshape(batch, seq, D_MODEL)
```

```python
import math

import numpy as np
import jax
import jax.numpy as jnp
from jax import lax
from jax.experimental import pallas as pl
from jax.experimental.pallas import tpu as pltpu

F32 = jnp.float32
BF16 = jnp.bfloat16
I32 = jnp.int32

D_MODEL = 1024
HALF = D_MODEL // 2
HEAD_DIM = 64
CONV_CH = 512
FFT_CH = 512
IN_COLS = 3 * CONV_CH + FFT_CH
MEM_LEN = 256
XA_HEADS = 4
XA_HEAD_DIM = D_MODEL // XA_HEADS
N_GROUPS = 4
EXPERTS_PER_GROUP = 8
N_EXPERTS = 32
TOP_K = 2
D_EXPERT = 512
EPS = 1e-6

FFT_N1 = 16
FFT_N2 = 256
FFT_K1_PER_STEP = 4

LANES = 128
MXU_COLS = 256
GRANULE = 8
MIX_ROWS = 512
TRUNK_ROWS = 512
LOCAL_ROWS = TOP_K * TRUNK_ROWS + N_EXPERTS * GRANULE
LOCAL_GRANULES = LOCAL_ROWS // GRANULE
EXPERT_ROWS = 256
ROUTER_ROWS = 128
NEG_BIG = -1e30
HI16 = -65536

VMEM_LIMIT = 56 * 1024 * 1024


def _rms(x, g):
    return x * lax.rsqrt(jnp.mean(x * x, axis=-1, keepdims=True) + EPS) * g


def _dot(a, b):
    return jnp.dot(a, b, preferred_element_type=F32)


def _dot_nt(a, b):
    return lax.dot_general(a, b, (((1,), (1,)), ((), ())), preferred_element_type=F32)


def _pack_halves(left_f32, right_f32):
    lb = lax.bitcast_convert_type(left_f32, I32)
    rb = lax.shift_right_logical(lax.bitcast_convert_type(right_f32, I32), jnp.int32(16))
    return lb | rb


def _unpack_halves(packed_i32):
    left = lax.bitcast_convert_type(packed_i32 & jnp.int32(HI16), F32)
    right = lax.bitcast_convert_type(lax.shift_left(packed_i32, jnp.int32(16)), F32)
    return left.astype(BF16), right.astype(BF16)


def _group_mean_matrix():
    g = np.kron(np.eye(MXU_COLS // HEAD_DIM), np.full((HEAD_DIM, HEAD_DIM), 1.0 / HEAD_DIM))
    return jnp.asarray(g, dtype=BF16)


def _head_mean_square(y, gm):
    sq = (y * y).astype(BF16)
    return jnp.concatenate([_dot(sq[:, c:c + MXU_COLS], gm) for c in range(0, y.shape[1], MXU_COLS)], axis=1)


def _fft_stage2_matrices(seq):
    k1 = np.arange(FFT_N1)[:, None, None]
    k2 = np.arange(FFT_N2)[None, :, None]
    s2 = np.arange(FFT_N2)[None, None, :]
    ang = 2.0 * np.pi * ((s2 * (k1 + FFT_N1 * k2)) % seq) / seq
    c, s = np.cos(ang), np.sin(ang)
    top = np.concatenate([c, s], axis=2)
    bot = np.concatenate([-s, c], axis=2)
    return jnp.asarray(np.concatenate([top, bot], axis=1), dtype=BF16)


def _fft_channel_matrix(seq):
    c = np.arange(HEAD_DIM)
    ang = 2.0 * np.pi * ((c[:, None] * c[None, :]) % HEAD_DIM) / HEAD_DIM
    scale = 1.0 / math.sqrt(seq * HEAD_DIM)
    eye = np.eye(MXU_COLS // HEAD_DIM)
    cs = np.concatenate([np.kron(eye, np.cos(ang)), np.kron(eye, np.sin(ang))], axis=0) * scale
    return jnp.asarray(cs, dtype=BF16)


def _strict_upper(n):
    return jnp.asarray(np.triu(np.ones((n, n)), k=1), dtype=BF16)


def _strict_lower(n):
    return jnp.asarray(np.tril(np.ones((n, n)), k=-1), dtype=BF16)


def _kv_kernel(mem_ref, g_ref, w_ref, o_ref):
    h = _rms(mem_ref[...], g_ref[...]).astype(BF16)
    o_ref[...] = _dot(h, w_ref[...].astype(BF16)).astype(BF16)


def _kv_proj(mem2d, g, w_kv):
    rows = mem2d.shape[0]
    cols = w_kv.shape[1]
    cb = 512
    return pl.pallas_call(
        _kv_kernel,
        grid=(cols // cb,),
        in_specs=[
            pl.BlockSpec((rows, D_MODEL), lambda j: (0, 0)),
            pl.BlockSpec((1, D_MODEL), lambda j: (0, 0)),
            pl.BlockSpec((D_MODEL, cb), lambda j: (0, j)),
        ],
        out_specs=pl.BlockSpec((rows, cb), lambda j: (0, j)),
        out_shape=jax.ShapeDtypeStruct((rows, cols), BF16),
        compiler_params=pltpu.CompilerParams(vmem_limit_bytes=VMEM_LIMIT),
        name="kv_proj",
    )(mem2d, g, w_kv)


def _mixer_in_kernel(x_ref, xp_ref, xn_ref, g_ref, w_ref, cw_ref, cb_ref, hg_ref, gm_ref,
                     conv_ref, uf_ref):
    i = pl.program_id(1)
    n_i = pl.num_programs(1)
    rows = x_ref.shape[0]
    g = g_ref[...]
    h = _rms(x_ref[...], g).astype(BF16)
    u_cv = _dot(h, w_ref[:, CONV_CH:3 * CONV_CH])
    cv = u_cv[:, :CONV_CH] * u_cv[:, CONV_CH:]

    hh = jnp.concatenate([_rms(xp_ref[...], g), _rms(xn_ref[...], g)], axis=0).astype(BF16)
    uh = _dot(hh, w_ref[:, CONV_CH:3 * CONV_CH])
    cvh = uh[:, :CONV_CH] * uh[:, CONV_CH:]
    cv_prev = cvh[7:8, :] * jnp.where(i == 0, 0.0, 1.0)
    cv_next = cvh[8:9, :] * jnp.where(i == n_i - 1, 0.0, 1.0)

    row = lax.broadcasted_iota(I32, cv.shape, 0)
    cv_up = jnp.where(row == 0, cv_prev, pltpu.roll(cv, 1, 0))
    cv_dn = jnp.where(row == rows - 1, cv_next, pltpu.roll(cv, rows - 1, 0))
    z = cw_ref[0:1, :] * cv_up + cw_ref[1:2, :] * cv + cw_ref[2:3, :] * cv_dn + cb_ref[...]
    uf_ref[...] = _dot(h, w_ref[:, 3 * CONV_CH:]).astype(BF16)
    y = _dot(h, w_ref[:, :CONV_CH]) * z
    ms = _head_mean_square(y, gm_ref[...])
    conv_ref[...] = (y * lax.rsqrt(ms + EPS) * hg_ref[...]).astype(BF16)


def _mixer_in(x2d, g, w_in, conv_w, conv_b, hg_conv, gm, batch, seq):
    n_i = seq // MIX_ROWS
    t = x2d.shape[0]
    r8 = MIX_ROWS // 8
    last8 = t // 8 - 1
    return pl.pallas_call(
        _mixer_in_kernel,
        grid=(batch, n_i),
        in_specs=[
            pl.BlockSpec((MIX_ROWS, D_MODEL), lambda b, i: (b * n_i + i, 0)),
            pl.BlockSpec((8, D_MODEL), lambda b, i: (jnp.maximum((b * n_i + i) * r8 - 1, 0), 0)),
            pl.BlockSpec((8, D_MODEL), lambda b, i: (jnp.minimum((b * n_i + i + 1) * r8, last8), 0)),
            pl.BlockSpec((1, D_MODEL), lambda b, i: (0, 0)),
            pl.BlockSpec((D_MODEL, IN_COLS), lambda b, i: (0, 0)),
            pl.BlockSpec((3, CONV_CH), lambda b, i: (0, 0)),
            pl.BlockSpec((1, CONV_CH), lambda b, i: (0, 0)),
            pl.BlockSpec((1, CONV_CH), lambda b, i: (0, 0)),
            pl.BlockSpec((MXU_COLS, MXU_COLS), lambda b, i: (0, 0)),
        ],
        out_specs=[
            pl.BlockSpec((MIX_ROWS, CONV_CH), lambda b, i: (b * n_i + i, 0)),
            pl.BlockSpec((MIX_ROWS, FFT_CH), lambda b, i: (b * n_i + i, 0)),
        ],
        out_shape=[
            jax.ShapeDtypeStruct((t, CONV_CH), BF16),
            jax.ShapeDtypeStruct((t, FFT_CH), BF16),
        ],
        compiler_params=pltpu.CompilerParams(vmem_limit_bytes=VMEM_LIMIT),
        name="mixer_in",
    )(x2d, x2d, x2d, g, w_in, conv_w, conv_b, hg_conv, gm)


_S1_ROWS = 16
_S1_LANES = 128


def _lincomb(terms):
    acc = None
    for coef, val in terms:
        if abs(coef) < 1e-12:
            continue
        if abs(coef - 1.0) < 1e-12:
            term, neg = val, False
        elif abs(coef + 1.0) < 1e-12:
            term, neg = val, True
        else:
            term, neg = coef * val, False
        if acc is None:
            acc = -term if neg else term
        else:
            acc = acc - term if neg else acc + term
    return acc


def _fft_stage1(x_ref, a_ref):
    half = FFT_N1 // 2
    cos = [[math.cos(2 * math.pi * ((k * j) % FFT_N1) / FFT_N1) for j in range(FFT_N1)] for k in range(FFT_N1)]
    sin = [[math.sin(2 * math.pi * ((k * j) % FFT_N1) / FFT_N1) for j in range(FFT_N1)] for k in range(FFT_N1)]

    def body(r, carry):
        r0 = pl.multiple_of(r * _S1_ROWS, _S1_ROWS)
        rows_re = pl.ds(r0, _S1_ROWS)
        rows_im = pl.ds(r0 + FFT_N2, _S1_ROWS)
        for lc in range(0, FFT_CH, _S1_LANES):
            lanes = slice(lc, lc + _S1_LANES)
            xs = [x_ref[j, rows_re, lanes].astype(F32) for j in range(FFT_N1)]
            ev = [None] + [xs[j] + xs[FFT_N1 - j] for j in range(1, half)]
            od = [None] + [xs[j] - xs[FFT_N1 - j] for j in range(1, half)]
            for k in range(half + 1):
                re = _lincomb([(1.0, xs[0]), (cos[k][half], xs[half])]
                              + [(cos[k][j], ev[j]) for j in range(1, half)])
                a_ref[k, rows_re, lanes] = re.astype(BF16)
                if k in (0, half):
                    zero = jnp.zeros_like(re).astype(BF16)
                    a_ref[k, rows_im, lanes] = zero
                else:
                    im = _lincomb([(-sin[k][j], od[j]) for j in range(1, half)])
                    a_ref[k, rows_im, lanes] = im.astype(BF16)
                    a_ref[FFT_N1 - k, rows_re, lanes] = re.astype(BF16)
                    a_ref[FFT_N1 - k, rows_im, lanes] = (-im).astype(BF16)
        return carry

    lax.fori_loop(0, FFT_N2 // _S1_ROWS, body, 0)


def _fourier_kernel(x_ref, m2_ref, cs_ref, gm_ref, hg_ref, o_ref, a_ref, y_ref):
    j = pl.program_id(1)

    @pl.when(j == 0)
    def _():
        _fft_stage1(x_ref, a_ref)

    ris = [_dot(m2_ref[kk], a_ref[j * FFT_K1_PER_STEP + kk]) for kk in range(FFT_K1_PER_STEP)]
    re = jnp.concatenate([ri[:FFT_N2] for ri in ris], axis=0).astype(BF16)
    im = jnp.concatenate([ri[FFT_N2:] for ri in ris], axis=0).astype(BF16)
    y = jnp.concatenate(
        [_dot(re[:, c:c + MXU_COLS], cs_ref[:MXU_COLS, :]) + _dot(im[:, c:c + MXU_COLS], cs_ref[MXU_COLS:, :])
         for c in range(0, FFT_CH, MXU_COLS)], axis=1)
    yn = y * lax.rsqrt(_head_mean_square(y, gm_ref[...]) + EPS) * hg_ref[...]
    for kk in range(FFT_K1_PER_STEP):
        k1 = j * FFT_K1_PER_STEP + kk
        for c in range(FFT_CH // LANES):
            y_ref[c, pl.ds(k1, FFT_N2, stride=FFT_N1), :] = yn[kk * FFT_N2:(kk + 1) * FFT_N2,
                                                               c * LANES:(c + 1) * LANES]

    @pl.when(j == pl.num_programs(1) - 1)
    def _():
        for c in range(FFT_CH // LANES):
            o_ref[:, c * LANES:(c + 1) * LANES] = y_ref[c].astype(BF16)


def _fourier(uf, m2, cs, gm, hg_fft, batch, seq):
    assert seq == FFT_N1 * FFT_N2
    x4 = uf.reshape(batch, FFT_N1, FFT_N2, FFT_CH)
    out = pl.pallas_call(
        _fourier_kernel,
        grid=(batch, FFT_N1 // FFT_K1_PER_STEP),
        in_specs=[
            pl.BlockSpec((None, FFT_N1, FFT_N2, FFT_CH), lambda b, j: (b, 0, 0, 0)),
            pl.BlockSpec((FFT_K1_PER_STEP, 2 * FFT_N2, 2 * FFT_N2), lambda b, j: (j, 0, 0)),
            pl.BlockSpec((2 * MXU_COLS, MXU_COLS), lambda b, j: (0, 0)),
            pl.BlockSpec((MXU_COLS, MXU_COLS), lambda b, j: (0, 0)),
            pl.BlockSpec((1, FFT_CH), lambda b, j: (0, 0)),
        ],
        out_specs=pl.BlockSpec((seq, FFT_CH), lambda b, j: (b, 0)),
        out_shape=jax.ShapeDtypeStruct((batch * seq, FFT_CH), BF16),
        scratch_shapes=[
            pltpu.VMEM((FFT_N1, 2 * FFT_N2, FFT_CH), BF16),
            pltpu.VMEM((FFT_CH // LANES, seq, LANES), F32),
        ],
        compiler_params=pltpu.CompilerParams(
            dimension_semantics=("arbitrary", "arbitrary"), vmem_limit_bytes=VMEM_LIMIT),
        name="fourier",
    )(x4, m2, cs, gm, hg_fft)
    return out


def _first_index_of_max(vals, vmax, row):
    return jnp.min(jnp.where(vals == vmax, row, vals.shape[0]), axis=0, keepdims=True)


def _route(lg):
    cols = lg.shape[1]
    row8 = lax.broadcasted_iota(I32, (EXPERTS_PER_GROUP, cols), 0)
    gl = lg[0:8, :]
    gmax = jnp.max(gl, axis=0, keepdims=True)
    g_w = 1.0 / jnp.sum(jnp.exp(gl - gmax), axis=0, keepdims=True)
    g_idx = _first_index_of_max(gl, gmax, row8)

    el = lg[8:16, :]
    for g in range(1, N_GROUPS):
        el = jnp.where(g_idx == g, lg[8 + 8 * g:16 + 8 * g, :], el)
    emax = jnp.max(el, axis=0, keepdims=True)
    ee = jnp.exp(el - emax)
    e_prob = ee / jnp.sum(ee, axis=0, keepdims=True)
    p1 = jnp.max(e_prob, axis=0, keepdims=True)
    i1 = _first_index_of_max(e_prob, p1, row8)
    rest = jnp.where(row8 == i1, -1.0, e_prob)
    p2 = jnp.max(rest, axis=0, keepdims=True)
    i2 = _first_index_of_max(rest, p2, row8)
    denom = p1 + p2
    e1 = g_idx * EXPERTS_PER_GROUP + i1
    e2 = g_idx * EXPERTS_PER_GROUP + i2
    return e1, e2, g_w * p1 / denom, g_w * p2 / denom


def _trunk_kernel(x_ref, conv_ref, fft_ref, wout_ref, gxa_ref, wq_ref, k_ref, v_ref, wo_ref,
                  gffn_ref, wr_ref, br_ref, tri_ref, ltri_ref,
                  x2_ref, xs_ref, pos_ref, gate_ref, cnt_ref, h3_s, lg_s):
    @pl.when(pl.program_id(0) == 0)
    def _():
        h3_s[...] = jnp.zeros_like(h3_s)
        lg_s[...] = jnp.zeros_like(lg_s)

    h3 = h3_s[...]
    lg = lg_s[...]
    n_tok = lg.shape[1]

    x1 = x_ref[...] + _dot(conv_ref[...], wout_ref[:CONV_CH, :]) + _dot(fft_ref[...], wout_ref[CONV_CH:, :])

    e1, e2, gate1, gate2 = _route(lg)
    row32 = lax.broadcasted_iota(I32, (N_EXPERTS, n_tok), 0)
    hit1 = row32 == e1
    hit2 = row32 == e2
    onehot = jnp.where(hit1 | hit2, 1.0, 0.0)
    before = _dot(onehot.astype(BF16), tri_ref[...])
    cnt = jnp.sum(onehot, axis=1, keepdims=True).astype(I32)
    piece = jnp.left_shift(jnp.right_shift(cnt + (GRANULE - 1), 3), 3)
    piece_b = jnp.broadcast_to(piece.astype(F32), (N_EXPERTS, LANES)).astype(BF16)
    start = _dot(ltri_ref[...], piece_b)[:, 0:1]
    slot = before + start
    pos1 = jnp.sum(jnp.where(hit1, slot, 0.0), axis=0, keepdims=True).astype(I32)
    pos2 = jnp.sum(jnp.where(hit2, slot, 0.0), axis=0, keepdims=True).astype(I32)
    pos_ref[0:1, :] = pos1
    pos_ref[1:2, :] = pos2
    pos_ref[2:8, :] = jnp.zeros((6, n_tok), I32)
    gate_ref[0:1, :] = gate1
    gate_ref[1:2, :] = gate2
    gate_ref[2:8, :] = jnp.zeros((6, n_tok), F32)
    cnt_ref[...] = jnp.broadcast_to(cnt, (N_EXPERTS, LANES))

    h2 = _rms(x1, gxa_ref[...]).astype(BF16)
    q = _dot(h2, wq_ref[...]).astype(BF16)

    r = lax.broadcasted_iota(I32, (LOCAL_ROWS, n_tok), 0)
    perm = jnp.where((r == pos1) | (r == pos2), 1.0, 0.0).astype(BF16)

    outs = []
    for hd in range(XA_HEADS):
        cols = slice(hd * XA_HEAD_DIM, (hd + 1) * XA_HEAD_DIM)
        s = _dot_nt(q[:, cols], k_ref[:, cols]) * (XA_HEAD_DIM ** -0.5)
        s = s - jnp.max(s, axis=-1, keepdims=True)
        p = jnp.exp(s)
        p = p / jnp.sum(p, axis=-1, keepdims=True)
        outs.append(_dot(p.astype(BF16), v_ref[:, cols]).astype(BF16))
        if hd == 1:
            sorted_left = _dot(perm, h3[:, :HALF])

    o = jnp.concatenate(outs, axis=-1)
    x2 = x1 + _dot(o, wo_ref[...])
    x2_ref[...] = x2
    sorted_right = _dot(perm, h3[:, HALF:])
    h3_next = _rms(x2, gffn_ref[...]).astype(BF16)
    xs_ref[...] = _pack_halves(sorted_left, sorted_right)
    h3_s[...] = h3_next
    lg_s[...] = _dot_nt(wr_ref[...], h3_next) + br_ref[...]


def _trunk(x2d, conv_n, fft_n, w_out, g_xa, w_q, kv, w_o, g_ffn, w_r_t, b_r, tri, ltri, seq):
    t = x2d.shape[0]
    n_tiles = t // TRUNK_ROWS
    n_per_batch = seq // TRUNK_ROWS
    const = lambda i: (0, 0)
    dense = lambda i: jnp.minimum(i, n_tiles - 1)
    routed = lambda i: jnp.maximum(i - 1, 0)
    return pl.pallas_call(
        _trunk_kernel,
        grid=(n_tiles + 1,),
        in_specs=[
            pl.BlockSpec((TRUNK_ROWS, D_MODEL), lambda i: (dense(i), 0)),
            pl.BlockSpec((TRUNK_ROWS, CONV_CH), lambda i: (dense(i), 0)),
            pl.BlockSpec((TRUNK_ROWS, FFT_CH), lambda i: (dense(i), 0)),
            pl.BlockSpec((D_MODEL, D_MODEL), const),
            pl.BlockSpec((1, D_MODEL), const),
            pl.BlockSpec((D_MODEL, D_MODEL), const),
            pl.BlockSpec((MEM_LEN, D_MODEL), lambda i: (dense(i) // n_per_batch, 0)),
            pl.BlockSpec((MEM_LEN, D_MODEL), lambda i: (dense(i) // n_per_batch, 1)),
            pl.BlockSpec((D_MODEL, D_MODEL), const),
            pl.BlockSpec((1, D_MODEL), const),
            pl.BlockSpec((ROUTER_ROWS, D_MODEL), const),
            pl.BlockSpec((ROUTER_ROWS, 1), const),
            pl.BlockSpec((TRUNK_ROWS, TRUNK_ROWS), const),
            pl.BlockSpec((N_EXPERTS, N_EXPERTS), const),
        ],
        out_specs=[
            pl.BlockSpec((TRUNK_ROWS, D_MODEL), lambda i: (dense(i), 0)),
            pl.BlockSpec((LOCAL_ROWS, HALF), lambda i: (routed(i), 0)),
            pl.BlockSpec((8, TRUNK_ROWS), lambda i: (0, routed(i))),
            pl.BlockSpec((8, TRUNK_ROWS), lambda i: (0, routed(i))),
            pl.BlockSpec((None, N_EXPERTS, LANES), lambda i: (routed(i), 0, 0)),
        ],
        out_shape=[
            jax.ShapeDtypeStruct((t, D_MODEL), F32),
            jax.ShapeDtypeStruct((n_tiles * LOCAL_ROWS, HALF), I32),
            jax.ShapeDtypeStruct((8, t), I32),
            jax.ShapeDtypeStruct((8, t), F32),
            jax.ShapeDtypeStruct((n_tiles, N_EXPERTS, LANES), I32),
        ],
        scratch_shapes=[
            pltpu.VMEM((TRUNK_ROWS, D_MODEL), BF16),
            pltpu.VMEM((ROUTER_ROWS, TRUNK_ROWS), F32),
        ],
        compiler_params=pltpu.CompilerParams(
            dimension_semantics=("arbitrary",), vmem_limit_bytes=VMEM_LIMIT),
        name="trunk",
    )(x2d, conv_n, fft_n, w_out, g_xa, w_q, kv, kv, w_o, g_ffn, w_r_t, b_r, tri, ltri)


def _granule(ref, row):
    return ref.at[pl.ds(pl.multiple_of(row, GRANULE), GRANULE), :]


def _experts_kernel(blk_e_ref, first_ref, next_e_ref, slot_ref, nblk_ref, row_ref, ngran_ref,
                    xs_hbm, wg_hbm, wu_hbm, wd_hbm, ys_hbm,
                    wg_buf, wu_buf, wd_buf, xbuf, obuf, wsems, xsems, osems):
    n = nblk_ref[0]
    gpb = EXPERT_ROWS // GRANULE

    def fetch_weights(e, s):
        return (pltpu.make_async_copy(wg_hbm.at[e], wg_buf.at[s], wsems.at[0, s]),
                pltpu.make_async_copy(wu_hbm.at[e], wu_buf.at[s], wsems.at[1, s]),
                pltpu.make_async_copy(wd_hbm.at[e], wd_buf.at[s], wsems.at[2, s]))

    def start_in(b, s):
        count = ngran_ref[b]
        for g in range(gpb):
            @pl.when(g < count)
            def _():
                pltpu.make_async_copy(_granule(xs_hbm, row_ref[b * gpb + g]),
                                      xbuf.at[s, pl.ds(g * GRANULE, GRANULE), :], xsems.at[s]).start()

    def start_out(b, s):
        count = ngran_ref[b]
        for g in range(gpb):
            @pl.when(g < count)
            def _():
                pltpu.make_async_copy(obuf.at[s, pl.ds(g * GRANULE, GRANULE), :],
                                      _granule(ys_hbm, row_ref[b * gpb + g]), osems.at[s]).start()

    def wait_in(b, s):
        rows = ngran_ref[b] * GRANULE
        pltpu.make_async_copy(xs_hbm.at[pl.ds(0, rows), :], xbuf.at[s, pl.ds(0, rows), :], xsems.at[s]).wait()

    def wait_out(b, s):
        rows = ngran_ref[b] * GRANULE
        pltpu.make_async_copy(obuf.at[s, pl.ds(0, rows), :], ys_hbm.at[pl.ds(0, rows), :], osems.at[s]).wait()

    xbuf[...] = jnp.zeros_like(xbuf)
    for cp in fetch_weights(blk_e_ref[0], 0):
        cp.start()
    start_in(0, 0)

    def block(i, carry):
        s = i % 2
        ws = slot_ref[i]

        @pl.when(i + 1 < n)
        def _():
            start_in(i + 1, 1 - s)

        @pl.when(first_ref[i] == 1)
        def _():
            for cp in fetch_weights(blk_e_ref[i], ws):
                cp.wait()

            @pl.when(next_e_ref[i] >= 0)
            def _():
                for cp in fetch_weights(next_e_ref[i], 1 - ws):
                    cp.start()

        wait_in(i, s)
        xl, xr = _unpack_halves(xbuf[s])
        a = _dot(xl, wg_buf[ws, :HALF, :].astype(BF16)) + _dot(xr, wg_buf[ws, HALF:, :].astype(BF16))
        b = _dot(xl, wu_buf[ws, :HALF, :].astype(BF16)) + _dot(xr, wu_buf[ws, HALF:, :].astype(BF16))
        hmid = (a * jax.nn.sigmoid(a) * b).astype(BF16)
        y = _dot(hmid, wd_buf[ws].astype(BF16))

        @pl.when(i >= 2)
        def _():
            wait_out(i - 2, s)

        obuf[s] = _pack_halves(y[:, :HALF].astype(BF16).astype(F32), y[:, HALF:].astype(BF16).astype(F32))
        start_out(i, s)

        @pl.when(i == n - 1)
        def _():
            @pl.when(i >= 1)
            def _():
                wait_out(i - 1, 1 - s)
            wait_out(i, s)
        return carry

    lax.fori_loop(0, n, block, 0)


def _experts(plan, xs_loc, w_gate, w_up, w_down):
    n_blk = plan["blk_e"].shape[0]
    hbm = pl.BlockSpec(memory_space=pl.ANY)
    return pl.pallas_call(
        _experts_kernel,
        grid_spec=pltpu.PrefetchScalarGridSpec(
            num_scalar_prefetch=7,
            grid=(1,),
            in_specs=[hbm, hbm, hbm, hbm],
            out_specs=hbm,
            scratch_shapes=[
                pltpu.VMEM((2, D_MODEL, D_EXPERT), F32),
                pltpu.VMEM((2, D_MODEL, D_EXPERT), F32),
                pltpu.VMEM((2, D_EXPERT, D_MODEL), F32),
                pltpu.VMEM((2, EXPERT_ROWS, HALF), I32),
                pltpu.VMEM((2, EXPERT_ROWS, HALF), I32),
                pltpu.SemaphoreType.DMA((3, 2)),
                pltpu.SemaphoreType.DMA((2,)),
                pltpu.SemaphoreType.DMA((2,)),
            ],
        ),
        out_shape=jax.ShapeDtypeStruct(xs_loc.shape, I32),
        input_output_aliases={7: 0},
        compiler_params=pltpu.CompilerParams(
            dimension_semantics=("arbitrary",), vmem_limit_bytes=VMEM_LIMIT),
        name="experts",
    )(plan["blk_e"], plan["first"], plan["next_e"], plan["slot"], plan["nblk"], plan["row"], plan["ngran"],
      xs_loc, w_gate, w_up, w_down)


def _combine_kernel(x2_ref, pos_ref, gate_ref, g_ref, ys_ref, o_ref):
    n_tok = x2_ref.shape[0]
    pos = pos_ref[...]
    gates = gate_ref[...]
    col = lax.broadcasted_iota(I32, (n_tok, LOCAL_ROWS), 1)
    w = (jnp.where(col == pos[:, 0:1], gates[:, 0:1], 0.0)
         + jnp.where(col == pos[:, 1:2], gates[:, 1:2], 0.0)).astype(BF16)
    yl, yr = _unpack_halves(ys_ref[...])
    moe = jnp.concatenate([_dot(w, yl), _dot(w, yr)], axis=-1)
    o_ref[...] = _rms(x2_ref[...] + moe, g_ref[...])


def _combine(x2, pos_tk, gates_tk, g_final, ys_loc):
    t = x2.shape[0]
    return pl.pallas_call(
        _combine_kernel,
        grid=(t // TRUNK_ROWS,),
        in_specs=[
            pl.BlockSpec((TRUNK_ROWS, D_MODEL), lambda i: (i, 0)),
            pl.BlockSpec((TRUNK_ROWS, TOP_K), lambda i: (i, 0)),
            pl.BlockSpec((TRUNK_ROWS, TOP_K), lambda i: (i, 0)),
            pl.BlockSpec((1, D_MODEL), lambda i: (0, 0)),
            pl.BlockSpec((LOCAL_ROWS, HALF), lambda i: (i, 0)),
        ],
        out_specs=pl.BlockSpec((TRUNK_ROWS, D_MODEL), lambda i: (i, 0)),
        out_shape=jax.ShapeDtypeStruct((t, D_MODEL), F32),
        compiler_params=pltpu.CompilerParams(vmem_limit_bytes=VMEM_LIMIT),
        name="combine",
    )(x2, pos_tk, gates_tk, g_final, ys_loc)


def _router_params(w_rg, b_rg, w_re, b_re):
    w = jnp.zeros((ROUTER_ROWS, D_MODEL), F32)
    w = w.at[0:N_GROUPS].set(w_rg.T).at[8:8 + N_EXPERTS].set(w_re.T)
    b = jnp.zeros((ROUTER_ROWS,), F32)
    b = b.at[0:N_GROUPS].set(b_rg).at[N_GROUPS:8].set(NEG_BIG).at[8:8 + N_EXPERTS].set(b_re)
    return w.astype(BF16), b.reshape(ROUTER_ROWS, 1)


def _plan(cnt, n_global_rows):
    n_tiles = cnt.shape[0]
    piece = (cnt + GRANULE - 1) // GRANULE * GRANULE
    lend = jnp.cumsum(piece, axis=1)
    lstart = lend - piece
    tot = jnp.sum(piece, axis=0)
    padded = (tot + EXPERT_ROWS - 1) // EXPERT_ROWS * EXPERT_ROWS
    pend = jnp.cumsum(padded)
    pstart = pend - padded
    cum_tiles = jnp.cumsum(piece, axis=0)

    n_blk = n_global_rows // EXPERT_ROWS
    blk_start = jnp.arange(n_blk, dtype=I32) * EXPERT_ROWS
    blk_e = jnp.minimum(jnp.sum((pend[None, :] <= blk_start[:, None]).astype(I32), axis=1), N_EXPERTS - 1)
    nblk = pend[-1:] // EXPERT_ROWS

    of_blk_e = blk_e[:, None] == jnp.arange(N_EXPERTS, dtype=I32)[None, :]
    pick = lambda table: jnp.sum(jnp.where(of_blk_e[:, None, :], table[None, :, :], 0), axis=2)
    pick1 = lambda vec: jnp.sum(jnp.where(of_blk_e, vec[None, :], 0), axis=1)
    off = (blk_start - pick1(pstart))[:, None] + jnp.arange(EXPERT_ROWS // GRANULE, dtype=I32)[None, :] * GRANULE
    real = (off < pick1(tot)[:, None]) & (blk_start < pend[-1])[:, None]
    ngran = jnp.sum(real.astype(I32), axis=1)
    cum_b = pick(cum_tiles)
    tile_of = jnp.minimum(jnp.sum((cum_b[:, None, :] <= off[:, :, None]).astype(I32), axis=2), n_tiles - 1)
    base = (jnp.arange(n_tiles, dtype=I32) * LOCAL_ROWS)[None, :] + pick(lstart) - (cum_b - pick(piece))
    of_tile = tile_of[:, :, None] == jnp.arange(n_tiles, dtype=I32)
    row = jnp.sum(jnp.where(of_tile, base[:, None, :], 0), axis=2) + off
    row = jnp.where(real, row, 0)
    blk = jnp.arange(n_blk, dtype=I32)
    valid = blk < nblk
    change = jnp.concatenate([jnp.ones((1,), bool), blk_e[1:] != blk_e[:-1]])
    slot = (jnp.cumsum(change.astype(I32)) - 1) % 2
    later = (blk_e[None, :] > blk_e[:, None]) & valid[None, :]
    next_e = jnp.min(jnp.where(later, blk_e[None, :], N_EXPERTS), axis=1)
    next_e = jnp.where(next_e == N_EXPERTS, -1, next_e)
    as_i32 = lambda v: v.astype(I32)
    return dict(
        row=as_i32(row.reshape(-1)), ngran=as_i32(ngran), blk_e=as_i32(blk_e), nblk=as_i32(nblk),
        first=as_i32(change & valid), slot=as_i32(slot), next_e=as_i32(next_e),
    )


def _layer(x2d, mem2d, batch, seq, norm_mix_g, w_in, conv_w, conv_b, head_norm_g, w_out,
           norm_xa_g, norm_mem_g, w_q, w_kv, w_o, norm_ffn_g, w_rg, b_rg, w_re, b_re,
           w_gate, w_up, w_down, out_norm_g):
    t = x2d.shape[0]
    n_tiles = t // TRUNK_ROWS
    row = lambda v: v.reshape(1, -1)
    hg = head_norm_g.reshape(-1)
    gm = _group_mean_matrix()

    kv = _kv_proj(mem2d, row(norm_mem_g), w_kv)
    conv_n, uf = _mixer_in(x2d, row(norm_mix_g), w_in.astype(BF16), conv_w, row(conv_b),
                           row(hg[:CONV_CH]), gm, batch, seq)
    fft_n = _fourier(uf.reshape(batch, seq, FFT_CH), _fft_stage2_matrices(seq), _fft_channel_matrix(seq),
                     gm, row(hg[CONV_CH:]), batch, seq)
    w_r_t, b_r = _router_params(w_rg, b_rg, w_re, b_re)
    x2, xs_loc, pos, gates, cnt = _trunk(
        x2d, conv_n, fft_n, w_out.astype(BF16), row(norm_xa_g), w_q.astype(BF16), kv, w_o.astype(BF16),
        row(norm_ffn_g), w_r_t, b_r, _strict_upper(TRUNK_ROWS), _strict_lower(N_EXPERTS), seq)

    max_rows = n_tiles * LOCAL_ROWS + N_EXPERTS * (EXPERT_ROWS - GRANULE)
    n_global_rows = -(-max_rows // EXPERT_ROWS) * EXPERT_ROWS
    plan = _plan(cnt[:, :, 0], n_global_rows)
    ys_loc = _experts(plan, xs_loc, w_gate, w_up, w_down)
    return _combine(x2, pos[0:TOP_K].T, gates[0:TOP_K].T, row(out_norm_g), ys_loc)


def kernel(x, mem, norm_mix_g, w_in, conv_w, conv_b, head_norm_g, w_out, norm_xa_g, norm_mem_g, w_q, w_kv,
           w_o, norm_ffn_g, w_route_group, b_route_group, w_route_expert, b_route_expert, w_gate, w_up,
           w_down, final_norm_g):
    batch, seq, _ = x.shape
    depth = norm_mix_g.shape[0]
    assert depth == 1, "the final norm is fused into the last layer's combine kernel"
    x2d = x.reshape(batch * seq, D_MODEL)
    mem2d = mem.reshape(batch * MEM_LEN, D_MODEL)
    l = 0
    out = _layer(x2d, mem2d, batch, seq, norm_mix_g[l], w_in[l], conv_w[l], conv_b[l], head_norm_g[l],
                 w_out[l], norm_xa_g[l], norm_mem_g[l], w_q[l], w_kv[l], w_o[l], norm_ffn_g[l],
                 w_route_group[l], b_route_group[l], w_route_expert[l], b_route_expert[l],
                 w_gate[l], w_up[l], w_down[l], final_norm_g)
    return out.reshape(batch, seq, D_MODEL)
```

```python
import math

import numpy as np
import jax
import jax.numpy as jnp
from jax import lax
from jax.experimental import pallas as pl
from jax.experimental.pallas import tpu as pltpu

F32 = jnp.float32
BF16 = jnp.bfloat16
I32 = jnp.int32

D_MODEL = 1024
HALF = D_MODEL // 2
HEAD_DIM = 64
CONV_CH = 512
FFT_CH = 512
IN_COLS = 3 * CONV_CH + FFT_CH
MEM_LEN = 256
XA_HEADS = 4
XA_HEAD_DIM = D_MODEL // XA_HEADS
N_GROUPS = 4
EXPERTS_PER_GROUP = 8
N_EXPERTS = 32
TOP_K = 2
D_EXPERT = 512
EPS = 1e-6

FFT_N1 = 16
FFT_N2 = 256
FFT_K1_PER_STEP = 4

LANES = 128
MXU_COLS = 256
GRANULE = 8
MIX_ROWS = 512
TRUNK_ROWS = 512
LOCAL_ROWS = TOP_K * TRUNK_ROWS + N_EXPERTS * GRANULE
LOCAL_GRANULES = LOCAL_ROWS // GRANULE
EXPERT_ROWS = 256
ROUTER_ROWS = 128
NEG_BIG = -1e30
HI16 = -65536

VMEM_LIMIT = 56 * 1024 * 1024


def _rms(x, g):
    return x * lax.rsqrt(jnp.mean(x * x, axis=-1, keepdims=True) + EPS) * g


def _dot(a, b):
    return jnp.dot(a, b, preferred_element_type=F32)


def _dot_nt(a, b):
    return lax.dot_general(a, b, (((1,), (1,)), ((), ())), preferred_element_type=F32)


def _dot_tn(a, b):
    return lax.dot_general(a, b, (((0,), (0,)), ((), ())), preferred_element_type=F32)


def _pack_halves(left_f32, right_f32):
    lb = lax.bitcast_convert_type(left_f32, I32)
    rb = lax.shift_right_logical(lax.bitcast_convert_type(right_f32, I32), jnp.int32(16))
    return lb | rb


def _unpack_halves(packed_i32):
    left = lax.bitcast_convert_type(packed_i32 & jnp.int32(HI16), F32)
    right = lax.bitcast_convert_type(lax.shift_left(packed_i32, jnp.int32(16)), F32)
    return left.astype(BF16), right.astype(BF16)


def _group_mean_matrix():
    g = np.kron(np.eye(MXU_COLS // HEAD_DIM), np.full((HEAD_DIM, HEAD_DIM), 1.0 / HEAD_DIM))
    return jnp.asarray(g, dtype=BF16)


def _head_mean_square(y, gm):
    sq = (y * y).astype(BF16)
    return jnp.concatenate([_dot(sq[:, c:c + MXU_COLS], gm) for c in range(0, y.shape[1], MXU_COLS)], axis=1)


def _fft_stage2_matrices(seq):
    k1 = np.arange(FFT_N1)[:, None, None]
    k2 = np.arange(FFT_N2)[None, :, None]
    s2 = np.arange(FFT_N2)[None, None, :]
    ang = 2.0 * np.pi * ((s2 * (k1 + FFT_N1 * k2)) % seq) / seq
    c, s = np.cos(ang), np.sin(ang)
    top = np.concatenate([c, s], axis=2)
    bot = np.concatenate([-s, c], axis=2)
    return jnp.asarray(np.concatenate([top, bot], axis=1), dtype=BF16)


def _fft_channel_matrix(seq):
    c = np.arange(HEAD_DIM)
    ang = 2.0 * np.pi * ((c[:, None] * c[None, :]) % HEAD_DIM) / HEAD_DIM
    scale = 1.0 / math.sqrt(seq * HEAD_DIM)
    eye = np.eye(MXU_COLS // HEAD_DIM)
    cs = np.concatenate([np.kron(eye, np.cos(ang)), np.kron(eye, np.sin(ang))], axis=0) * scale
    return jnp.asarray(cs, dtype=BF16)


def _strict_upper(n):
    return jnp.asarray(np.triu(np.ones((n, n)), k=1), dtype=BF16)


def _strict_lower(n):
    return jnp.asarray(np.tril(np.ones((n, n)), k=-1), dtype=BF16)


def _kv_kernel(mem_ref, g_ref, w_ref, o_ref):
    h = _rms(mem_ref[...], g_ref[...]).astype(BF16)
    o_ref[...] = _dot(h, w_ref[...].astype(BF16)).astype(BF16)


def _kv_proj(mem2d, g, w_kv):
    rows = mem2d.shape[0]
    cols = w_kv.shape[1]
    cb = 512
    return pl.pallas_call(
        _kv_kernel,
        grid=(cols // cb,),
        in_specs=[
            pl.BlockSpec((rows, D_MODEL), lambda j: (0, 0)),
            pl.BlockSpec((1, D_MODEL), lambda j: (0, 0)),
            pl.BlockSpec((D_MODEL, cb), lambda j: (0, j)),
        ],
        out_specs=pl.BlockSpec((rows, cb), lambda j: (0, j)),
        out_shape=jax.ShapeDtypeStruct((rows, cols), BF16),
        compiler_params=pltpu.CompilerParams(vmem_limit_bytes=VMEM_LIMIT),
        name="kv_proj",
    )(mem2d, g, w_kv)


def _mixer_in_kernel(x_ref, xp_ref, xn_ref, g_ref, w_ref, cw_ref, cb_ref, hg_ref, gm_ref,
                     conv_ref, uf_ref):
    i = pl.program_id(1)
    n_i = pl.num_programs(1)
    rows = x_ref.shape[0]
    g = g_ref[...]
    h = _rms(x_ref[...], g).astype(BF16)
    u_cv = _dot(h, w_ref[:, CONV_CH:3 * CONV_CH])
    cv = u_cv[:, :CONV_CH] * u_cv[:, CONV_CH:]

    hh = jnp.concatenate([_rms(xp_ref[...], g), _rms(xn_ref[...], g)], axis=0).astype(BF16)
    uh = _dot(hh, w_ref[:, CONV_CH:3 * CONV_CH])
    cvh = uh[:, :CONV_CH] * uh[:, CONV_CH:]
    cv_prev = cvh[7:8, :] * jnp.where(i == 0, 0.0, 1.0)
    cv_next = cvh[8:9, :] * jnp.where(i == n_i - 1, 0.0, 1.0)

    row = lax.broadcasted_iota(I32, cv.shape, 0)
    cv_up = jnp.where(row == 0, cv_prev, pltpu.roll(cv, 1, 0))
    cv_dn = jnp.where(row == rows - 1, cv_next, pltpu.roll(cv, rows - 1, 0))
    z = cw_ref[0:1, :] * cv_up + cw_ref[1:2, :] * cv + cw_ref[2:3, :] * cv_dn + cb_ref[...]
    uf_ref[...] = _dot(h, w_ref[:, 3 * CONV_CH:]).astype(BF16)
    y = _dot(h, w_ref[:, :CONV_CH]) * z
    ms = _head_mean_square(y, gm_ref[...])
    conv_ref[...] = (y * lax.rsqrt(ms + EPS) * hg_ref[...]).astype(BF16)


def _mixer_in(x2d, g, w_in, conv_w, conv_b, hg_conv, gm, batch, seq):
    n_i = seq // MIX_ROWS
    t = x2d.shape[0]
    r8 = MIX_ROWS // 8
    last8 = t // 8 - 1
    return pl.pallas_call(
        _mixer_in_kernel,
        grid=(batch, n_i),
        in_specs=[
            pl.BlockSpec((MIX_ROWS, D_MODEL), lambda b, i: (b * n_i + i, 0)),
            pl.BlockSpec((8, D_MODEL), lambda b, i: (jnp.maximum((b * n_i + i) * r8 - 1, 0), 0)),
            pl.BlockSpec((8, D_MODEL), lambda b, i: (jnp.minimum((b * n_i + i + 1) * r8, last8), 0)),
            pl.BlockSpec((1, D_MODEL), lambda b, i: (0, 0)),
            pl.BlockSpec((D_MODEL, IN_COLS), lambda b, i: (0, 0)),
            pl.BlockSpec((3, CONV_CH), lambda b, i: (0, 0)),
            pl.BlockSpec((1, CONV_CH), lambda b, i: (0, 0)),
            pl.BlockSpec((1, CONV_CH), lambda b, i: (0, 0)),
            pl.BlockSpec((MXU_COLS, MXU_COLS), lambda b, i: (0, 0)),
        ],
        out_specs=[
            pl.BlockSpec((MIX_ROWS, CONV_CH), lambda b, i: (b * n_i + i, 0)),
            pl.BlockSpec((MIX_ROWS, FFT_CH), lambda b, i: (b * n_i + i, 0)),
        ],
        out_shape=[
            jax.ShapeDtypeStruct((t, CONV_CH), BF16),
            jax.ShapeDtypeStruct((t, FFT_CH), BF16),
        ],
        compiler_params=pltpu.CompilerParams(vmem_limit_bytes=VMEM_LIMIT),
        name="mixer_in",
    )(x2d, x2d, x2d, g, w_in, conv_w, conv_b, hg_conv, gm)


_S1_ROWS = 16
_S1_LANES = 128


def _lincomb(terms):
    acc = None
    for coef, val in terms:
        if abs(coef) < 1e-12:
            continue
        if abs(coef - 1.0) < 1e-12:
            term, neg = val, False
        elif abs(coef + 1.0) < 1e-12:
            term, neg = val, True
        else:
            term, neg = coef * val, False
        if acc is None:
            acc = -term if neg else term
        else:
            acc = acc - term if neg else acc + term
    return acc


def _fft_stage1(x_ref, a_ref):
    half = FFT_N1 // 2
    cos = [[math.cos(2 * math.pi * ((k * j) % FFT_N1) / FFT_N1) for j in range(FFT_N1)] for k in range(FFT_N1)]
    sin = [[math.sin(2 * math.pi * ((k * j) % FFT_N1) / FFT_N1) for j in range(FFT_N1)] for k in range(FFT_N1)]

    def body(r, carry):
        r0 = pl.multiple_of(r * _S1_ROWS, _S1_ROWS)
        rows_re = pl.ds(r0, _S1_ROWS)
        rows_im = pl.ds(r0 + FFT_N2, _S1_ROWS)
        for lc in range(0, FFT_CH, _S1_LANES):
            lanes = slice(lc, lc + _S1_LANES)
            xs = [x_ref[j, rows_re, lanes].astype(F32) for j in range(FFT_N1)]
            ev = [None] + [xs[j] + xs[FFT_N1 - j] for j in range(1, half)]
            od = [None] + [xs[j] - xs[FFT_N1 - j] for j in range(1, half)]
            for k in range(half + 1):
                re = _lincomb([(1.0, xs[0]), (cos[k][half], xs[half])]
                              + [(cos[k][j], ev[j]) for j in range(1, half)])
                a_ref[k, rows_re, lanes] = re.astype(BF16)
                if k in (0, half):
                    zero = jnp.zeros_like(re).astype(BF16)
                    a_ref[k, rows_im, lanes] = zero
                else:
                    im = _lincomb([(-sin[k][j], od[j]) for j in range(1, half)])
                    a_ref[k, rows_im, lanes] = im.astype(BF16)
                    a_ref[FFT_N1 - k, rows_re, lanes] = re.astype(BF16)
                    a_ref[FFT_N1 - k, rows_im, lanes] = (-im).astype(BF16)
        return carry

    lax.fori_loop(0, FFT_N2 // _S1_ROWS, body, 0)


def _fourier_kernel(x_ref, m2_ref, cs_ref, gm_ref, hg_ref, o_ref, a_ref, y_ref):
    j = pl.program_id(1)

    @pl.when(j == 0)
    def _():
        _fft_stage1(x_ref, a_ref)

    ris = [_dot(m2_ref[kk], a_ref[j * FFT_K1_PER_STEP + kk]) for kk in range(FFT_K1_PER_STEP)]
    re = jnp.concatenate([ri[:FFT_N2] for ri in ris], axis=0).astype(BF16)
    im = jnp.concatenate([ri[FFT_N2:] for ri in ris], axis=0).astype(BF16)
    y = jnp.concatenate(
        [_dot(re[:, c:c + MXU_COLS], cs_ref[:MXU_COLS, :]) + _dot(im[:, c:c + MXU_COLS], cs_ref[MXU_COLS:, :])
         for c in range(0, FFT_CH, MXU_COLS)], axis=1)
    yn = y * lax.rsqrt(_head_mean_square(y, gm_ref[...]) + EPS) * hg_ref[...]
    for kk in range(FFT_K1_PER_STEP):
        k1 = j * FFT_K1_PER_STEP + kk
        for c in range(FFT_CH // LANES):
            y_ref[c, pl.ds(k1, FFT_N2, stride=FFT_N1), :] = yn[kk * FFT_N2:(kk + 1) * FFT_N2,
                                                               c * LANES:(c + 1) * LANES]

    @pl.when(j == pl.num_programs(1) - 1)
    def _():
        for c in range(FFT_CH // LANES):
            o_ref[:, c * LANES:(c + 1) * LANES] = y_ref[c].astype(BF16)


def _fourier(uf, m2, cs, gm, hg_fft, batch, seq):
    assert seq == FFT_N1 * FFT_N2
    x4 = uf.reshape(batch, FFT_N1, FFT_N2, FFT_CH)
    out = pl.pallas_call(
        _fourier_kernel,
        grid=(batch, FFT_N1 // FFT_K1_PER_STEP),
        in_specs=[
            pl.BlockSpec((None, FFT_N1, FFT_N2, FFT_CH), lambda b, j: (b, 0, 0, 0)),
            pl.BlockSpec((FFT_K1_PER_STEP, 2 * FFT_N2, 2 * FFT_N2), lambda b, j: (j, 0, 0)),
            pl.BlockSpec((2 * MXU_COLS, MXU_COLS), lambda b, j: (0, 0)),
            pl.BlockSpec((MXU_COLS, MXU_COLS), lambda b, j: (0, 0)),
            pl.BlockSpec((1, FFT_CH), lambda b, j: (0, 0)),
        ],
        out_specs=pl.BlockSpec((seq, FFT_CH), lambda b, j: (b, 0)),
        out_shape=jax.ShapeDtypeStruct((batch * seq, FFT_CH), BF16),
        scratch_shapes=[
            pltpu.VMEM((FFT_N1, 2 * FFT_N2, FFT_CH), BF16),
            pltpu.VMEM((FFT_CH // LANES, seq, LANES), F32),
        ],
        compiler_params=pltpu.CompilerParams(
            dimension_semantics=("arbitrary", "arbitrary"), vmem_limit_bytes=VMEM_LIMIT),
        name="fourier",
    )(x4, m2, cs, gm, hg_fft)
    return out


def _first_index_of_max(vals, vmax, row):
    return jnp.min(jnp.where(vals == vmax, row, vals.shape[0]), axis=0, keepdims=True)


def _route(lg):
    cols = lg.shape[1]
    row8 = lax.broadcasted_iota(I32, (EXPERTS_PER_GROUP, cols), 0)
    gl = lg[0:8, :]
    gmax = jnp.max(gl, axis=0, keepdims=True)
    g_w = 1.0 / jnp.sum(jnp.exp(gl - gmax), axis=0, keepdims=True)
    g_idx = _first_index_of_max(gl, gmax, row8)

    el = lg[8:16, :]
    for g in range(1, N_GROUPS):
        el = jnp.where(g_idx == g, lg[8 + 8 * g:16 + 8 * g, :], el)
    emax = jnp.max(el, axis=0, keepdims=True)
    ee = jnp.exp(el - emax)
    e_prob = ee / jnp.sum(ee, axis=0, keepdims=True)
    p1 = jnp.max(e_prob, axis=0, keepdims=True)
    i1 = _first_index_of_max(e_prob, p1, row8)
    rest = jnp.where(row8 == i1, -1.0, e_prob)
    p2 = jnp.max(rest, axis=0, keepdims=True)
    i2 = _first_index_of_max(rest, p2, row8)
    denom = p1 + p2
    e1 = g_idx * EXPERTS_PER_GROUP + i1
    e2 = g_idx * EXPERTS_PER_GROUP + i2
    return e1, e2, g_w * p1 / denom, g_w * p2 / denom


def _trunk_kernel(x_ref, conv_ref, fft_ref, wout_ref, gxa_ref, wq_ref, k_ref, v_ref, wo_ref,
                  gffn_ref, wr_ref, br_ref, tri_ref, ltri_ref,
                  x2_ref, xs_ref, pos_ref, gate_ref, cnt_ref, h3_s, lg_s):
    @pl.when(pl.program_id(0) == 0)
    def _():
        h3_s[...] = jnp.zeros_like(h3_s)
        lg_s[...] = jnp.zeros_like(lg_s)

    h3 = h3_s[...]
    lg = lg_s[...]
    n_tok = lg.shape[1]

    x1 = x_ref[...] + _dot(conv_ref[...], wout_ref[:CONV_CH, :]) + _dot(fft_ref[...], wout_ref[CONV_CH:, :])

    e1, e2, gate1, gate2 = _route(lg)
    row32 = lax.broadcasted_iota(I32, (N_EXPERTS, n_tok), 0)
    hit1 = row32 == e1
    hit2 = row32 == e2
    onehot = jnp.where(hit1 | hit2, 1.0, 0.0)
    before = _dot(onehot.astype(BF16), tri_ref[...])
    cnt = jnp.sum(onehot, axis=1, keepdims=True).astype(I32)
    piece = jnp.left_shift(jnp.right_shift(cnt + (GRANULE - 1), 3), 3)
    piece_b = jnp.broadcast_to(piece.astype(F32), (N_EXPERTS, LANES)).astype(BF16)
    start = _dot(ltri_ref[...], piece_b)[:, 0:1]
    slot = before + start
    pos1 = jnp.sum(jnp.where(hit1, slot, 0.0), axis=0, keepdims=True).astype(I32)
    pos2 = jnp.sum(jnp.where(hit2, slot, 0.0), axis=0, keepdims=True).astype(I32)
    pos_ref[0:1, :] = pos1
    pos_ref[1:2, :] = pos2
    pos_ref[2:8, :] = jnp.zeros((6, n_tok), I32)
    gate_ref[0:1, :] = gate1
    gate_ref[1:2, :] = gate2
    gate_ref[2:8, :] = jnp.zeros((6, n_tok), F32)
    cnt_ref[...] = jnp.broadcast_to(cnt, (N_EXPERTS, LANES))

    h2 = _rms(x1, gxa_ref[...]).astype(BF16)
    q = _dot(h2, wq_ref[...]).astype(BF16)

    r = lax.broadcasted_iota(I32, (LOCAL_ROWS, n_tok), 0)
    perm = jnp.where((r == pos1) | (r == pos2), 1.0, 0.0).astype(BF16)

    outs = []
    for hd in range(XA_HEADS):
        cols = slice(hd * XA_HEAD_DIM, (hd + 1) * XA_HEAD_DIM)
        s = _dot_nt(q[:, cols], k_ref[:, cols]) * (XA_HEAD_DIM ** -0.5)
        s = s - jnp.max(s, axis=-1, keepdims=True)
        p = jnp.exp(s)
        p = p / jnp.sum(p, axis=-1, keepdims=True)
        outs.append(_dot(p.astype(BF16), v_ref[:, cols]).astype(BF16))
        if hd == 1:
            sorted_left = _dot(perm, h3[:, :HALF])

    o = jnp.concatenate(outs, axis=-1)
    x2 = x1 + _dot(o, wo_ref[...])
    x2_ref[...] = x2
    sorted_right = _dot(perm, h3[:, HALF:])
    h3_next = _rms(x2, gffn_ref[...]).astype(BF16)
    xs_ref[...] = _pack_halves(sorted_left, sorted_right)
    h3_s[...] = h3_next
    lg_s[...] = _dot_nt(wr_ref[...], h3_next) + br_ref[...]


def _trunk(x2d, conv_n, fft_n, w_out, g_xa, w_q, kv, w_o, g_ffn, w_r_t, b_r, tri, ltri, seq):
    t = x2d.shape[0]
    n_tiles = t // TRUNK_ROWS
    n_per_batch = seq // TRUNK_ROWS
    const = lambda i: (0, 0)
    dense = lambda i: jnp.minimum(i, n_tiles - 1)
    routed = lambda i: jnp.maximum(i - 1, 0)
    return pl.pallas_call(
        _trunk_kernel,
        grid=(n_tiles + 1,),
        in_specs=[
            pl.BlockSpec((TRUNK_ROWS, D_MODEL), lambda i: (dense(i), 0)),
            pl.BlockSpec((TRUNK_ROWS, CONV_CH), lambda i: (dense(i), 0)),
            pl.BlockSpec((TRUNK_ROWS, FFT_CH), lambda i: (dense(i), 0)),
            pl.BlockSpec((D_MODEL, D_MODEL), const),
            pl.BlockSpec((1, D_MODEL), const),
            pl.BlockSpec((D_MODEL, D_MODEL), const),
            pl.BlockSpec((MEM_LEN, D_MODEL), lambda i: (dense(i) // n_per_batch, 0)),
            pl.BlockSpec((MEM_LEN, D_MODEL), lambda i: (dense(i) // n_per_batch, 1)),
            pl.BlockSpec((D_MODEL, D_MODEL), const),
            pl.BlockSpec((1, D_MODEL), const),
            pl.BlockSpec((ROUTER_ROWS, D_MODEL), const),
            pl.BlockSpec((ROUTER_ROWS, 1), const),
            pl.BlockSpec((TRUNK_ROWS, TRUNK_ROWS), const),
            pl.BlockSpec((N_EXPERTS, N_EXPERTS), const),
        ],
        out_specs=[
            pl.BlockSpec((TRUNK_ROWS, D_MODEL), lambda i: (dense(i), 0)),
            pl.BlockSpec((LOCAL_ROWS, HALF), lambda i: (routed(i), 0)),
            pl.BlockSpec((8, TRUNK_ROWS), lambda i: (0, routed(i))),
            pl.BlockSpec((8, TRUNK_ROWS), lambda i: (0, routed(i))),
            pl.BlockSpec((None, N_EXPERTS, LANES), lambda i: (routed(i), 0, 0)),
        ],
        out_shape=[
            jax.ShapeDtypeStruct((t, D_MODEL), F32),
            jax.ShapeDtypeStruct((n_tiles * LOCAL_ROWS, HALF), I32),
            jax.ShapeDtypeStruct((8, t), I32),
            jax.ShapeDtypeStruct((8, t), F32),
            jax.ShapeDtypeStruct((n_tiles, N_EXPERTS, LANES), I32),
        ],
        scratch_shapes=[
            pltpu.VMEM((TRUNK_ROWS, D_MODEL), BF16),
            pltpu.VMEM((ROUTER_ROWS, TRUNK_ROWS), F32),
        ],
        compiler_params=pltpu.CompilerParams(
            dimension_semantics=("arbitrary",), vmem_limit_bytes=VMEM_LIMIT),
        name="trunk",
    )(x2d, conv_n, fft_n, w_out, g_xa, w_q, kv, kv, w_o, g_ffn, w_r_t, b_r, tri, ltri)


def _granule(ref, row):
    return ref.at[pl.ds(pl.multiple_of(row, GRANULE), GRANULE), :]


def _experts_kernel(blk_e_ref, first_ref, next_e_ref, slot_ref, nblk_ref, row_ref, ngran_ref,
                    xs_hbm, wg_hbm, wu_hbm, wd_hbm, ys_hbm,
                    wg_buf, wu_buf, wd_buf, xbuf, obuf, wsems, xsems, osems):
    n = nblk_ref[0]
    gpb = EXPERT_ROWS // GRANULE

    def fetch_weights(e, s):
        return (pltpu.make_async_copy(wg_hbm.at[e], wg_buf.at[s], wsems.at[0, s]),
                pltpu.make_async_copy(wu_hbm.at[e], wu_buf.at[s], wsems.at[1, s]),
                pltpu.make_async_copy(wd_hbm.at[e], wd_buf.at[s], wsems.at[2, s]))

    def start_in(b, s):
        count = ngran_ref[b]
        for g in range(gpb):
            @pl.when(g < count)
            def _():
                pltpu.make_async_copy(_granule(xs_hbm, row_ref[b * gpb + g]),
                                      xbuf.at[s, pl.ds(g * GRANULE, GRANULE), :], xsems.at[s]).start()

    def start_out(b, s):
        count = ngran_ref[b]
        for g in range(gpb):
            @pl.when(g < count)
            def _():
                pltpu.make_async_copy(obuf.at[s, pl.ds(g * GRANULE, GRANULE), :],
                                      _granule(ys_hbm, row_ref[b * gpb + g]), osems.at[s]).start()

    def wait_in(b, s):
        rows = ngran_ref[b] * GRANULE
        pltpu.make_async_copy(xs_hbm.at[pl.ds(0, rows), :], xbuf.at[s, pl.ds(0, rows), :], xsems.at[s]).wait()

    def wait_out(b, s):
        rows = ngran_ref[b] * GRANULE
        pltpu.make_async_copy(obuf.at[s, pl.ds(0, rows), :], ys_hbm.at[pl.ds(0, rows), :], osems.at[s]).wait()

    xbuf[...] = jnp.zeros_like(xbuf)
    for cp in fetch_weights(blk_e_ref[0], 0):
        cp.start()
    start_in(0, 0)

    def block(i, carry):
        s = i % 2
        ws = slot_ref[i]

        @pl.when(i + 1 < n)
        def _():
            start_in(i + 1, 1 - s)

        @pl.when(first_ref[i] == 1)
        def _():
            for cp in fetch_weights(blk_e_ref[i], ws):
                cp.wait()

            @pl.when(next_e_ref[i] >= 0)
            def _():
                for cp in fetch_weights(next_e_ref[i], 1 - ws):
                    cp.start()

        wait_in(i, s)
        xl, xr = _unpack_halves(xbuf[s])
        a = _dot(xl, wg_buf[ws, :HALF, :].astype(BF16)) + _dot(xr, wg_buf[ws, HALF:, :].astype(BF16))
        b = _dot(xl, wu_buf[ws, :HALF, :].astype(BF16)) + _dot(xr, wu_buf[ws, HALF:, :].astype(BF16))
        hmid = (a * jax.nn.sigmoid(a) * b).astype(BF16)
        y = _dot(hmid, wd_buf[ws].astype(BF16))

        @pl.when(i >= 2)
        def _():
            wait_out(i - 2, s)

        obuf[s] = _pack_halves(y[:, :HALF].astype(BF16).astype(F32), y[:, HALF:].astype(BF16).astype(F32))
        start_out(i, s)

        @pl.when(i == n - 1)
        def _():
            @pl.when(i >= 1)
            def _():
                wait_out(i - 1, 1 - s)
            wait_out(i, s)
        return carry

    lax.fori_loop(0, n, block, 0)


def _experts(plan, xs_loc, w_gate, w_up, w_down):
    n_blk = plan["blk_e"].shape[0]
    hbm = pl.BlockSpec(memory_space=pl.ANY)
    return pl.pallas_call(
        _experts_kernel,
        grid_spec=pltpu.PrefetchScalarGridSpec(
            num_scalar_prefetch=7,
            grid=(1,),
            in_specs=[hbm, hbm, hbm, hbm],
            out_specs=hbm,
            scratch_shapes=[
                pltpu.VMEM((2, D_MODEL, D_EXPERT), F32),
                pltpu.VMEM((2, D_MODEL, D_EXPERT), F32),
                pltpu.VMEM((2, D_EXPERT, D_MODEL), F32),
                pltpu.VMEM((2, EXPERT_ROWS, HALF), I32),
                pltpu.VMEM((2, EXPERT_ROWS, HALF), I32),
                pltpu.SemaphoreType.DMA((3, 2)),
                pltpu.SemaphoreType.DMA((2,)),
                pltpu.SemaphoreType.DMA((2,)),
            ],
        ),
        out_shape=jax.ShapeDtypeStruct(xs_loc.shape, I32),
        input_output_aliases={7: 0},
        compiler_params=pltpu.CompilerParams(
            dimension_semantics=("arbitrary",), vmem_limit_bytes=VMEM_LIMIT),
        name="experts",
    )(plan["blk_e"], plan["first"], plan["next_e"], plan["slot"], plan["nblk"], plan["row"], plan["ngran"],
      xs_loc, w_gate, w_up, w_down)


def _combine_kernel(x2_ref, pos_ref, gate_ref, g_ref, ys_ref, o_ref):
    n_tok = x2_ref.shape[0]
    r = lax.broadcasted_iota(I32, (LOCAL_ROWS, n_tok), 0)
    w_t = (jnp.where(r == pos_ref[0:1, :], gate_ref[0:1, :], 0.0)
           + jnp.where(r == pos_ref[1:2, :], gate_ref[1:2, :], 0.0)).astype(BF16)
    yl, yr = _unpack_halves(ys_ref[...])
    moe = jnp.concatenate([_dot_tn(w_t, yl), _dot_tn(w_t, yr)], axis=-1)
    o_ref[...] = _rms(x2_ref[...] + moe, g_ref[...])


def _combine(x2, pos_tk, gates_tk, g_final, ys_loc):
    t = x2.shape[0]
    return pl.pallas_call(
        _combine_kernel,
        grid=(t // TRUNK_ROWS,),
        in_specs=[
            pl.BlockSpec((TRUNK_ROWS, D_MODEL), lambda i: (i, 0)),
            pl.BlockSpec((8, TRUNK_ROWS), lambda i: (0, i)),
            pl.BlockSpec((8, TRUNK_ROWS), lambda i: (0, i)),
            pl.BlockSpec((1, D_MODEL), lambda i: (0, 0)),
            pl.BlockSpec((LOCAL_ROWS, HALF), lambda i: (i, 0)),
        ],
        out_specs=pl.BlockSpec((TRUNK_ROWS, D_MODEL), lambda i: (i, 0)),
        out_shape=jax.ShapeDtypeStruct((t, D_MODEL), F32),
        compiler_params=pltpu.CompilerParams(vmem_limit_bytes=VMEM_LIMIT),
        name="combine",
    )(x2, pos_tk, gates_tk, g_final, ys_loc)


def _router_params(w_rg, b_rg, w_re, b_re):
    w = jnp.zeros((ROUTER_ROWS, D_MODEL), F32)
    w = w.at[0:N_GROUPS].set(w_rg.T).at[8:8 + N_EXPERTS].set(w_re.T)
    b = jnp.zeros((ROUTER_ROWS,), F32)
    b = b.at[0:N_GROUPS].set(b_rg).at[N_GROUPS:8].set(NEG_BIG).at[8:8 + N_EXPERTS].set(b_re)
    return w.astype(BF16), b.reshape(ROUTER_ROWS, 1)


def _plan(cnt, n_global_rows):
    n_tiles = cnt.shape[0]
    piece = (cnt + GRANULE - 1) // GRANULE * GRANULE
    lend = jnp.cumsum(piece, axis=1)
    lstart = lend - piece
    tot = jnp.sum(piece, axis=0)
    padded = (tot + EXPERT_ROWS - 1) // EXPERT_ROWS * EXPERT_ROWS
    pend = jnp.cumsum(padded)
    pstart = pend - padded
    cum_tiles = jnp.cumsum(piece, axis=0)

    n_blk = n_global_rows // EXPERT_ROWS
    blk_start = jnp.arange(n_blk, dtype=I32) * EXPERT_ROWS
    blk_e = jnp.minimum(jnp.sum((pend[None, :] <= blk_start[:, None]).astype(I32), axis=1), N_EXPERTS - 1)
    nblk = pend[-1:] // EXPERT_ROWS

    of_blk_e = blk_e[:, None] == jnp.arange(N_EXPERTS, dtype=I32)[None, :]
    pick = lambda table: jnp.sum(jnp.where(of_blk_e[:, None, :], table[None, :, :], 0), axis=2)
    pick1 = lambda vec: jnp.sum(jnp.where(of_blk_e, vec[None, :], 0), axis=1)
    off = (blk_start - pick1(pstart))[:, None] + jnp.arange(EXPERT_ROWS // GRANULE, dtype=I32)[None, :] * GRANULE
    real = (off < pick1(tot)[:, None]) & (blk_start < pend[-1])[:, None]
    ngran = jnp.sum(real.astype(I32), axis=1)
    cum_b = pick(cum_tiles)
    tile_of = jnp.minimum(jnp.sum((cum_b[:, None, :] <= off[:, :, None]).astype(I32), axis=2), n_tiles - 1)
    base = (jnp.arange(n_tiles, dtype=I32) * LOCAL_ROWS)[None, :] + pick(lstart) - (cum_b - pick(piece))
    of_tile = tile_of[:, :, None] == jnp.arange(n_tiles, dtype=I32)
    row = jnp.sum(jnp.where(of_tile, base[:, None, :], 0), axis=2) + off
    row = jnp.where(real, row, 0)
    blk = jnp.arange(n_blk, dtype=I32)
    valid = blk < nblk
    change = jnp.concatenate([jnp.ones((1,), bool), blk_e[1:] != blk_e[:-1]])
    slot = (jnp.cumsum(change.astype(I32)) - 1) % 2
    later = (blk_e[None, :] > blk_e[:, None]) & valid[None, :]
    next_e = jnp.min(jnp.where(later, blk_e[None, :], N_EXPERTS), axis=1)
    next_e = jnp.where(next_e == N_EXPERTS, -1, next_e)
    as_i32 = lambda v: v.astype(I32)
    return dict(
        row=as_i32(row.reshape(-1)), ngran=as_i32(ngran), blk_e=as_i32(blk_e), nblk=as_i32(nblk),
        first=as_i32(change & valid), slot=as_i32(slot), next_e=as_i32(next_e),
    )


def _layer(x2d, mem2d, batch, seq, norm_mix_g, w_in, conv_w, conv_b, head_norm_g, w_out,
           norm_xa_g, norm_mem_g, w_q, w_kv, w_o, norm_ffn_g, w_rg, b_rg, w_re, b_re,
           w_gate, w_up, w_down, out_norm_g):
    t = x2d.shape[0]
    n_tiles = t // TRUNK_ROWS
    row = lambda v: v.reshape(1, -1)
    hg = head_norm_g.reshape(-1)
    gm = _group_mean_matrix()

    kv = _kv_proj(mem2d, row(norm_mem_g), w_kv)
    conv_n, uf = _mixer_in(x2d, row(norm_mix_g), w_in.astype(BF16), conv_w, row(conv_b),
                           row(hg[:CONV_CH]), gm, batch, seq)
    fft_n = _fourier(uf.reshape(batch, seq, FFT_CH), _fft_stage2_matrices(seq), _fft_channel_matrix(seq),
                     gm, row(hg[CONV_CH:]), batch, seq)
    w_r_t, b_r = _router_params(w_rg, b_rg, w_re, b_re)
    x2, xs_loc, pos, gates, cnt = _trunk(
        x2d, conv_n, fft_n, w_out.astype(BF16), row(norm_xa_g), w_q.astype(BF16), kv, w_o.astype(BF16),
        row(norm_ffn_g), w_r_t, b_r, _strict_upper(TRUNK_ROWS), _strict_lower(N_EXPERTS), seq)

    max_rows = n_tiles * LOCAL_ROWS + N_EXPERTS * (EXPERT_ROWS - GRANULE)
    n_global_rows = -(-max_rows // EXPERT_ROWS) * EXPERT_ROWS
    plan = _plan(cnt[:, :, 0], n_global_rows)
    ys_loc = _experts(plan, xs_loc, w_gate, w_up, w_down)
    return _combine(x2, pos, gates, row(out_norm_g), ys_loc)


def kernel(x, mem, norm_mix_g, w_in, conv_w, conv_b, head_norm_g, w_out, norm_xa_g, norm_mem_g, w_q, w_kv,
           w_o, norm_ffn_g, w_route_group, b_route_group, w_route_expert, b_route_expert, w_gate, w_up,
           w_down, final_norm_g):
    batch, seq, _ = x.shape
    depth = norm_mix_g.shape[0]
    assert depth == 1, "the final norm is fused into the last layer's combine kernel"
    x2d = x.reshape(batch * seq, D_MODEL)
    mem2d = mem.reshape(batch * MEM_LEN, D_MODEL)
    l = 0
    out = _layer(x2d, mem2d, batch, seq, norm_mix_g[l], w_in[l], conv_w[l], conv_b[l], head_norm_g[l],
                 w_out[l], norm_xa_g[l], norm_mem_g[l], w_q[l], w_kv[l], w_o[l], norm_ffn_g[l],
                 w_route_group[l], b_route_group[l], w_route_expert[l], b_route_expert[l],
                 w_gate[l], w_up[l], w_down[l], final_norm_g)
    return out.reshape(batch, seq, D_MODEL)
```

```python
import math

import numpy as np
import jax
import jax.numpy as jnp
from jax import lax
from jax.experimental import pallas as pl
from jax.experimental.pallas import tpu as pltpu

F32 = jnp.float32
BF16 = jnp.bfloat16
I32 = jnp.int32

D_MODEL = 1024
HALF = D_MODEL // 2
HEAD_DIM = 64
CONV_CH = 512
FFT_CH = 512
IN_COLS = 3 * CONV_CH + FFT_CH
MEM_LEN = 256
XA_HEADS = 4
XA_HEAD_DIM = D_MODEL // XA_HEADS
N_GROUPS = 4
EXPERTS_PER_GROUP = 8
N_EXPERTS = 32
TOP_K = 2
D_EXPERT = 512
EPS = 1e-6

FFT_N1 = 16
FFT_N2 = 256
FFT_K1_PER_STEP = 4

LANES = 128
MXU_COLS = 256
GRANULE = 8
MIX_ROWS = 512
TRUNK_ROWS = 512
LOCAL_ROWS = TOP_K * TRUNK_ROWS + N_EXPERTS * GRANULE
LOCAL_GRANULES = LOCAL_ROWS // GRANULE
EXPERT_ROWS = 256
X_SLOTS = 3
ROUTER_ROWS = 128
NEG_BIG = -1e30
HI16 = -65536

VMEM_LIMIT = 56 * 1024 * 1024


def _rms(x, g):
    return x * lax.rsqrt(jnp.mean(x * x, axis=-1, keepdims=True) + EPS) * g


def _dot(a, b):
    return jnp.dot(a, b, preferred_element_type=F32)


def _dot_nt(a, b):
    return lax.dot_general(a, b, (((1,), (1,)), ((), ())), preferred_element_type=F32)


def _dot_tn(a, b):
    return lax.dot_general(a, b, (((0,), (0,)), ((), ())), preferred_element_type=F32)


def _pack_halves(left_f32, right_f32):
    lb = lax.bitcast_convert_type(left_f32, I32)
    rb = lax.shift_right_logical(lax.bitcast_convert_type(right_f32, I32), jnp.int32(16))
    return lb | rb


def _unpack_halves(packed_i32):
    left = lax.bitcast_convert_type(packed_i32 & jnp.int32(HI16), F32)
    right = lax.bitcast_convert_type(lax.shift_left(packed_i32, jnp.int32(16)), F32)
    return left.astype(BF16), right.astype(BF16)


def _group_mean_matrix():
    g = np.kron(np.eye(MXU_COLS // HEAD_DIM), np.full((HEAD_DIM, HEAD_DIM), 1.0 / HEAD_DIM))
    return jnp.asarray(g, dtype=BF16)


def _head_mean_square(y, gm):
    sq = (y * y).astype(BF16)
    return jnp.concatenate([_dot(sq[:, c:c + MXU_COLS], gm) for c in range(0, y.shape[1], MXU_COLS)], axis=1)


def _fft_stage2_matrices(seq):
    k1 = np.arange(FFT_N1)[:, None, None]
    k2 = np.arange(FFT_N2)[None, :, None]
    s2 = np.arange(FFT_N2)[None, None, :]
    ang = 2.0 * np.pi * ((s2 * (k1 + FFT_N1 * k2)) % seq) / seq
    c, s = np.cos(ang), np.sin(ang)
    top = np.concatenate([c, s], axis=2)
    bot = np.concatenate([-s, c], axis=2)
    return jnp.asarray(np.concatenate([top, bot], axis=1), dtype=BF16)


def _fft_channel_matrix(seq):
    c = np.arange(HEAD_DIM)
    ang = 2.0 * np.pi * ((c[:, None] * c[None, :]) % HEAD_DIM) / HEAD_DIM
    scale = 1.0 / math.sqrt(seq * HEAD_DIM)
    eye = np.eye(MXU_COLS // HEAD_DIM)
    cs = np.concatenate([np.kron(eye, np.cos(ang)), np.kron(eye, np.sin(ang))], axis=0) * scale
    return jnp.asarray(cs, dtype=BF16)


def _strict_upper(n):
    return jnp.asarray(np.triu(np.ones((n, n)), k=1), dtype=BF16)


def _strict_lower(n):
    return jnp.asarray(np.tril(np.ones((n, n)), k=-1), dtype=BF16)


def _kv_kernel(mem_ref, g_ref, w_ref, o_ref):
    h = _rms(mem_ref[...], g_ref[...]).astype(BF16)
    o_ref[...] = _dot(h, w_ref[...].astype(BF16)).astype(BF16)


def _kv_proj(mem2d, g, w_kv):
    rows = mem2d.shape[0]
    cols = w_kv.shape[1]
    cb = 512
    return pl.pallas_call(
        _kv_kernel,
        grid=(cols // cb,),
        in_specs=[
            pl.BlockSpec((rows, D_MODEL), lambda j: (0, 0)),
            pl.BlockSpec((1, D_MODEL), lambda j: (0, 0)),
            pl.BlockSpec((D_MODEL, cb), lambda j: (0, j)),
        ],
        out_specs=pl.BlockSpec((rows, cb), lambda j: (0, j)),
        out_shape=jax.ShapeDtypeStruct((rows, cols), BF16),
        compiler_params=pltpu.CompilerParams(vmem_limit_bytes=VMEM_LIMIT),
        name="kv_proj",
    )(mem2d, g, w_kv)


def _mixer_in_kernel(x_ref, xp_ref, xn_ref, g_ref, w_ref, cw_ref, cb_ref, hg_ref, gm_ref,
                     conv_ref, uf_ref):
    i = pl.program_id(1)
    n_i = pl.num_programs(1)
    rows = x_ref.shape[0]
    g = g_ref[...]
    h = _rms(x_ref[...], g).astype(BF16)
    u_cv = _dot(h, w_ref[:, CONV_CH:3 * CONV_CH])
    cv = u_cv[:, :CONV_CH] * u_cv[:, CONV_CH:]

    hh = jnp.concatenate([_rms(xp_ref[...], g), _rms(xn_ref[...], g)], axis=0).astype(BF16)
    uh = _dot(hh, w_ref[:, CONV_CH:3 * CONV_CH])
    cvh = uh[:, :CONV_CH] * uh[:, CONV_CH:]
    cv_prev = cvh[7:8, :] * jnp.where(i == 0, 0.0, 1.0)
    cv_next = cvh[8:9, :] * jnp.where(i == n_i - 1, 0.0, 1.0)

    row = lax.broadcasted_iota(I32, cv.shape, 0)
    cv_up = jnp.where(row == 0, cv_prev, pltpu.roll(cv, 1, 0))
    cv_dn = jnp.where(row == rows - 1, cv_next, pltpu.roll(cv, rows - 1, 0))
    z = cw_ref[0:1, :] * cv_up + cw_ref[1:2, :] * cv + cw_ref[2:3, :] * cv_dn + cb_ref[...]
    uf_ref[...] = _dot(h, w_ref[:, 3 * CONV_CH:]).astype(BF16)
    y = _dot(h, w_ref[:, :CONV_CH]) * z
    ms = _head_mean_square(y, gm_ref[...])
    conv_ref[...] = (y * lax.rsqrt(ms + EPS) * hg_ref[...]).astype(BF16)


def _mixer_in(x2d, g, w_in, conv_w, conv_b, hg_conv, gm, batch, seq):
    n_i = seq // MIX_ROWS
    t = x2d.shape[0]
    r8 = MIX_ROWS // 8
    last8 = t // 8 - 1
    return pl.pallas_call(
        _mixer_in_kernel,
        grid=(batch, n_i),
        in_specs=[
            pl.BlockSpec((MIX_ROWS, D_MODEL), lambda b, i: (b * n_i + i, 0)),
            pl.BlockSpec((8, D_MODEL), lambda b, i: (jnp.maximum((b * n_i + i) * r8 - 1, 0), 0)),
            pl.BlockSpec((8, D_MODEL), lambda b, i: (jnp.minimum((b * n_i + i + 1) * r8, last8), 0)),
            pl.BlockSpec((1, D_MODEL), lambda b, i: (0, 0)),
            pl.BlockSpec((D_MODEL, IN_COLS), lambda b, i: (0, 0)),
            pl.BlockSpec((3, CONV_CH), lambda b, i: (0, 0)),
            pl.BlockSpec((1, CONV_CH), lambda b, i: (0, 0)),
            pl.BlockSpec((1, CONV_CH), lambda b, i: (0, 0)),
            pl.BlockSpec((MXU_COLS, MXU_COLS), lambda b, i: (0, 0)),
        ],
        out_specs=[
            pl.BlockSpec((MIX_ROWS, CONV_CH), lambda b, i: (b * n_i + i, 0)),
            pl.BlockSpec((MIX_ROWS, FFT_CH), lambda b, i: (b * n_i + i, 0)),
        ],
        out_shape=[
            jax.ShapeDtypeStruct((t, CONV_CH), BF16),
            jax.ShapeDtypeStruct((t, FFT_CH), BF16),
        ],
        compiler_params=pltpu.CompilerParams(vmem_limit_bytes=VMEM_LIMIT),
        name="mixer_in",
    )(x2d, x2d, x2d, g, w_in, conv_w, conv_b, hg_conv, gm)


_S1_ROWS = 16
_S1_LANES = 128


def _lincomb(terms):
    acc = None
    for coef, val in terms:
        if abs(coef) < 1e-12:
            continue
        if abs(coef - 1.0) < 1e-12:
            term, neg = val, False
        elif abs(coef + 1.0) < 1e-12:
            term, neg = val, True
        else:
            term, neg = coef * val, False
        if acc is None:
            acc = -term if neg else term
        else:
            acc = acc - term if neg else acc + term
    return acc


def _fft_stage1(x_ref, a_ref):
    half = FFT_N1 // 2
    cos = [[math.cos(2 * math.pi * ((k * j) % FFT_N1) / FFT_N1) for j in range(FFT_N1)] for k in range(FFT_N1)]
    sin = [[math.sin(2 * math.pi * ((k * j) % FFT_N1) / FFT_N1) for j in range(FFT_N1)] for k in range(FFT_N1)]

    def body(r, carry):
        r0 = pl.multiple_of(r * _S1_ROWS, _S1_ROWS)
        rows_re = pl.ds(r0, _S1_ROWS)
        rows_im = pl.ds(r0 + FFT_N2, _S1_ROWS)
        for lc in range(0, FFT_CH, _S1_LANES):
            lanes = slice(lc, lc + _S1_LANES)
            xs = [x_ref[j, rows_re, lanes].astype(F32) for j in range(FFT_N1)]
            ev = [None] + [xs[j] + xs[FFT_N1 - j] for j in range(1, half)]
            od = [None] + [xs[j] - xs[FFT_N1 - j] for j in range(1, half)]
            for k in range(half + 1):
                re = _lincomb([(1.0, xs[0]), (cos[k][half], xs[half])]
                              + [(cos[k][j], ev[j]) for j in range(1, half)])
                a_ref[k, rows_re, lanes] = re.astype(BF16)
                if k in (0, half):
                    zero = jnp.zeros_like(re).astype(BF16)
                    a_ref[k, rows_im, lanes] = zero
                else:
                    im = _lincomb([(-sin[k][j], od[j]) for j in range(1, half)])
                    a_ref[k, rows_im, lanes] = im.astype(BF16)
                    a_ref[FFT_N1 - k, rows_re, lanes] = re.astype(BF16)
                    a_ref[FFT_N1 - k, rows_im, lanes] = (-im).astype(BF16)
        return carry

    lax.fori_loop(0, FFT_N2 // _S1_ROWS, body, 0)


def _fourier_kernel(x_ref, m2_ref, cs_ref, gm_ref, hg_ref, o_ref, a_ref, y_ref):
    j = pl.program_id(1)

    @pl.when(j == 0)
    def _():
        _fft_stage1(x_ref, a_ref)

    ris = [_dot(m2_ref[kk], a_ref[j * FFT_K1_PER_STEP + kk]) for kk in range(FFT_K1_PER_STEP)]
    re = jnp.concatenate([ri[:FFT_N2] for ri in ris], axis=0).astype(BF16)
    im = jnp.concatenate([ri[FFT_N2:] for ri in ris], axis=0).astype(BF16)
    y = jnp.concatenate(
        [_dot(re[:, c:c + MXU_COLS], cs_ref[:MXU_COLS, :]) + _dot(im[:, c:c + MXU_COLS], cs_ref[MXU_COLS:, :])
         for c in range(0, FFT_CH, MXU_COLS)], axis=1)
    yn = y * lax.rsqrt(_head_mean_square(y, gm_ref[...]) + EPS) * hg_ref[...]
    for kk in range(FFT_K1_PER_STEP):
        k1 = j * FFT_K1_PER_STEP + kk
        for c in range(FFT_CH // LANES):
            y_ref[c, pl.ds(k1, FFT_N2, stride=FFT_N1), :] = yn[kk * FFT_N2:(kk + 1) * FFT_N2,
                                                               c * LANES:(c + 1) * LANES]

    @pl.when(j == pl.num_programs(1) - 1)
    def _():
        for c in range(FFT_CH // LANES):
            o_ref[:, c * LANES:(c + 1) * LANES] = y_ref[c].astype(BF16)


def _fourier(uf, m2, cs, gm, hg_fft, batch, seq):
    assert seq == FFT_N1 * FFT_N2
    x4 = uf.reshape(batch, FFT_N1, FFT_N2, FFT_CH)
    out = pl.pallas_call(
        _fourier_kernel,
        grid=(batch, FFT_N1 // FFT_K1_PER_STEP),
        in_specs=[
            pl.BlockSpec((None, FFT_N1, FFT_N2, FFT_CH), lambda b, j: (b, 0, 0, 0)),
            pl.BlockSpec((FFT_K1_PER_STEP, 2 * FFT_N2, 2 * FFT_N2), lambda b, j: (j, 0, 0)),
            pl.BlockSpec((2 * MXU_COLS, MXU_COLS), lambda b, j: (0, 0)),
            pl.BlockSpec((MXU_COLS, MXU_COLS), lambda b, j: (0, 0)),
            pl.BlockSpec((1, FFT_CH), lambda b, j: (0, 0)),
        ],
        out_specs=pl.BlockSpec((seq, FFT_CH), lambda b, j: (b, 0)),
        out_shape=jax.ShapeDtypeStruct((batch * seq, FFT_CH), BF16),
        scratch_shapes=[
            pltpu.VMEM((FFT_N1, 2 * FFT_N2, FFT_CH), BF16),
            pltpu.VMEM((FFT_CH // LANES, seq, LANES), F32),
        ],
        compiler_params=pltpu.CompilerParams(
            dimension_semantics=("arbitrary", "arbitrary"), vmem_limit_bytes=VMEM_LIMIT),
        name="fourier",
    )(x4, m2, cs, gm, hg_fft)
    return out


def _first_index_of_max(vals, vmax, row):
    return jnp.min(jnp.where(vals == vmax, row, vals.shape[0]), axis=0, keepdims=True)


def _route(lg):
    cols = lg.shape[1]
    row8 = lax.broadcasted_iota(I32, (EXPERTS_PER_GROUP, cols), 0)
    gl = lg[0:8, :]
    gmax = jnp.max(gl, axis=0, keepdims=True)
    g_w = 1.0 / jnp.sum(jnp.exp(gl - gmax), axis=0, keepdims=True)
    g_idx = _first_index_of_max(gl, gmax, row8)

    el = lg[8:16, :]
    for g in range(1, N_GROUPS):
        el = jnp.where(g_idx == g, lg[8 + 8 * g:16 + 8 * g, :], el)
    emax = jnp.max(el, axis=0, keepdims=True)
    ee = jnp.exp(el - emax)
    e_prob = ee / jnp.sum(ee, axis=0, keepdims=True)
    p1 = jnp.max(e_prob, axis=0, keepdims=True)
    i1 = _first_index_of_max(e_prob, p1, row8)
    rest = jnp.where(row8 == i1, -1.0, e_prob)
    p2 = jnp.max(rest, axis=0, keepdims=True)
    i2 = _first_index_of_max(rest, p2, row8)
    denom = p1 + p2
    e1 = g_idx * EXPERTS_PER_GROUP + i1
    e2 = g_idx * EXPERTS_PER_GROUP + i2
    return e1, e2, g_w * p1 / denom, g_w * p2 / denom


def _trunk_kernel(x_ref, conv_ref, fft_ref, wout_ref, gxa_ref, wq_ref, k_ref, v_ref, wo_ref,
                  gffn_ref, wr_ref, br_ref, tri_ref, ltri_ref,
                  x2_ref, xs_ref, pos_ref, gate_ref, cnt_ref, h3_s, lg_s):
    @pl.when(pl.program_id(0) == 0)
    def _():
        h3_s[...] = jnp.zeros_like(h3_s)
        lg_s[...] = jnp.zeros_like(lg_s)

    h3 = h3_s[...]
    lg = lg_s[...]
    n_tok = lg.shape[1]

    x1 = x_ref[...] + _dot(conv_ref[...], wout_ref[:CONV_CH, :]) + _dot(fft_ref[...], wout_ref[CONV_CH:, :])

    e1, e2, gate1, gate2 = _route(lg)
    row32 = lax.broadcasted_iota(I32, (N_EXPERTS, n_tok), 0)
    hit1 = row32 == e1
    hit2 = row32 == e2
    onehot = jnp.where(hit1 | hit2, 1.0, 0.0)
    before = _dot(onehot.astype(BF16), tri_ref[...])
    cnt = jnp.sum(onehot, axis=1, keepdims=True).astype(I32)
    piece = jnp.left_shift(jnp.right_shift(cnt + (GRANULE - 1), 3), 3)
    piece_b = jnp.broadcast_to(piece.astype(F32), (N_EXPERTS, LANES)).astype(BF16)
    start = _dot(ltri_ref[...], piece_b)[:, 0:1]
    slot = before + start
    pos1 = jnp.sum(jnp.where(hit1, slot, 0.0), axis=0, keepdims=True).astype(I32)
    pos2 = jnp.sum(jnp.where(hit2, slot, 0.0), axis=0, keepdims=True).astype(I32)
    pos_ref[0:1, :] = pos1
    pos_ref[1:2, :] = pos2
    pos_ref[2:8, :] = jnp.zeros((6, n_tok), I32)
    gate_ref[0:1, :] = gate1
    gate_ref[1:2, :] = gate2
    gate_ref[2:8, :] = jnp.zeros((6, n_tok), F32)
    cnt_ref[...] = jnp.broadcast_to(cnt, (N_EXPERTS, LANES))

    h2 = _rms(x1, gxa_ref[...]).astype(BF16)
    q = _dot(h2, wq_ref[...]).astype(BF16)

    r = lax.broadcasted_iota(I32, (LOCAL_ROWS, n_tok), 0)
    perm = jnp.where((r == pos1) | (r == pos2), 1.0, 0.0).astype(BF16)

    outs = []
    for hd in range(XA_HEADS):
        cols = slice(hd * XA_HEAD_DIM, (hd + 1) * XA_HEAD_DIM)
        s = _dot_nt(q[:, cols], k_ref[:, cols]) * (XA_HEAD_DIM ** -0.5)
        s = s - jnp.max(s, axis=-1, keepdims=True)
        p = jnp.exp(s)
        p = p / jnp.sum(p, axis=-1, keepdims=True)
        outs.append(_dot(p.astype(BF16), v_ref[:, cols]).astype(BF16))
        if hd == 1:
            sorted_left = _dot(perm, h3[:, :HALF])

    o = jnp.concatenate(outs, axis=-1)
    x2 = x1 + _dot(o, wo_ref[...])
    x2_ref[...] = x2
    sorted_right = _dot(perm, h3[:, HALF:])
    h3_next = _rms(x2, gffn_ref[...]).astype(BF16)
    xs_ref[...] = _pack_halves(sorted_left, sorted_right)
    h3_s[...] = h3_next
    lg_s[...] = _dot_nt(wr_ref[...], h3_next) + br_ref[...]


def _trunk(x2d, conv_n, fft_n, w_out, g_xa, w_q, kv, w_o, g_ffn, w_r_t, b_r, tri, ltri, seq):
    t = x2d.shape[0]
    n_tiles = t // TRUNK_ROWS
    n_per_batch = seq // TRUNK_ROWS
    const = lambda i: (0, 0)
    dense = lambda i: jnp.minimum(i, n_tiles - 1)
    routed = lambda i: jnp.maximum(i - 1, 0)
    return pl.pallas_call(
        _trunk_kernel,
        grid=(n_tiles + 1,),
        in_specs=[
            pl.BlockSpec((TRUNK_ROWS, D_MODEL), lambda i: (dense(i), 0)),
            pl.BlockSpec((TRUNK_ROWS, CONV_CH), lambda i: (dense(i), 0)),
            pl.BlockSpec((TRUNK_ROWS, FFT_CH), lambda i: (dense(i), 0)),
            pl.BlockSpec((D_MODEL, D_MODEL), const),
            pl.BlockSpec((1, D_MODEL), const),
            pl.BlockSpec((D_MODEL, D_MODEL), const),
            pl.BlockSpec((MEM_LEN, D_MODEL), lambda i: (dense(i) // n_per_batch, 0)),
            pl.BlockSpec((MEM_LEN, D_MODEL), lambda i: (dense(i) // n_per_batch, 1)),
            pl.BlockSpec((D_MODEL, D_MODEL), const),
            pl.BlockSpec((1, D_MODEL), const),
            pl.BlockSpec((ROUTER_ROWS, D_MODEL), const),
            pl.BlockSpec((ROUTER_ROWS, 1), const),
            pl.BlockSpec((TRUNK_ROWS, TRUNK_ROWS), const),
            pl.BlockSpec((N_EXPERTS, N_EXPERTS), const),
        ],
        out_specs=[
            pl.BlockSpec((TRUNK_ROWS, D_MODEL), lambda i: (dense(i), 0)),
            pl.BlockSpec((LOCAL_ROWS, HALF), lambda i: (routed(i), 0)),
            pl.BlockSpec((8, TRUNK_ROWS), lambda i: (0, routed(i))),
            pl.BlockSpec((8, TRUNK_ROWS), lambda i: (0, routed(i))),
            pl.BlockSpec((None, N_EXPERTS, LANES), lambda i: (routed(i), 0, 0)),
        ],
        out_shape=[
            jax.ShapeDtypeStruct((t, D_MODEL), F32),
            jax.ShapeDtypeStruct((n_tiles * LOCAL_ROWS, HALF), I32),
            jax.ShapeDtypeStruct((8, t), I32),
            jax.ShapeDtypeStruct((8, t), F32),
            jax.ShapeDtypeStruct((n_tiles, N_EXPERTS, LANES), I32),
        ],
        scratch_shapes=[
            pltpu.VMEM((TRUNK_ROWS, D_MODEL), BF16),
            pltpu.VMEM((ROUTER_ROWS, TRUNK_ROWS), F32),
        ],
        compiler_params=pltpu.CompilerParams(
            dimension_semantics=("arbitrary",), vmem_limit_bytes=VMEM_LIMIT),
        name="trunk",
    )(x2d, conv_n, fft_n, w_out, g_xa, w_q, kv, kv, w_o, g_ffn, w_r_t, b_r, tri, ltri)


def _granule(ref, row):
    return ref.at[pl.ds(pl.multiple_of(row, GRANULE), GRANULE), :]


def _experts_kernel(blk_e_ref, first_ref, next_e_ref, slot_ref, nblk_ref, row_ref, ngran_ref,
                    xs_hbm, wg_hbm, wu_hbm, wd_hbm, ys_hbm,
                    wg_buf, wu_buf, wd_buf, xbuf, obuf, wsems, xsems, osems):
    n = nblk_ref[0]
    gpb = EXPERT_ROWS // GRANULE

    def fetch_weights(e, s):
        return (pltpu.make_async_copy(wg_hbm.at[e], wg_buf.at[s], wsems.at[0, s]),
                pltpu.make_async_copy(wu_hbm.at[e], wu_buf.at[s], wsems.at[1, s]),
                pltpu.make_async_copy(wd_hbm.at[e], wd_buf.at[s], wsems.at[2, s]))

    def start_in(b, s, count):
        for g in range(gpb):
            @pl.when(g < count)
            def _():
                pltpu.make_async_copy(_granule(xs_hbm, row_ref[b * gpb + g]),
                                      xbuf.at[s, pl.ds(g * GRANULE, GRANULE), :], xsems.at[s]).start()

    def start_out(b, s):
        count = ngran_ref[b]
        for g in range(gpb):
            @pl.when(g < count)
            def _():
                pltpu.make_async_copy(obuf.at[s, pl.ds(g * GRANULE, GRANULE), :],
                                      _granule(ys_hbm, row_ref[b * gpb + g]), osems.at[s]).start()

    def wait_in(b, s):
        rows = ngran_ref[b] * GRANULE
        pltpu.make_async_copy(xs_hbm.at[pl.ds(0, rows), :], xbuf.at[s, pl.ds(0, rows), :], xsems.at[s]).wait()

    def wait_out(b, s):
        rows = ngran_ref[b] * GRANULE
        pltpu.make_async_copy(obuf.at[s, pl.ds(0, rows), :], ys_hbm.at[pl.ds(0, rows), :], osems.at[s]).wait()

    xbuf[...] = jnp.zeros_like(xbuf)
    for cp in fetch_weights(blk_e_ref[0], 0):
        cp.start()
    last = blk_e_ref.shape[0] - 1
    start_in(0, 0, ngran_ref[0])
    start_in(1, 1, jnp.where(n > 1, ngran_ref[1], 0))

    def block(i, carry):
        xs = lax.rem(i, X_SLOTS)
        os = i % 2
        ws = slot_ref[i]

        @pl.when(first_ref[i] == 1)
        def _():
            for cp in fetch_weights(blk_e_ref[i], ws):
                cp.wait()

            @pl.when(next_e_ref[i] >= 0)
            def _():
                for cp in fetch_weights(next_e_ref[i], 1 - ws):
                    cp.start()

        @pl.when(i >= 2)
        def _():
            wait_out(i - 2, os)

        ahead = jnp.minimum(i + 2, last)
        ahead_count = jnp.where(i + 2 < n, ngran_ref[ahead], 0)

        wait_in(i, xs)
        xl, xr = _unpack_halves(xbuf[xs])
        a = _dot(xl, wg_buf[ws, :HALF, :].astype(BF16)) + _dot(xr, wg_buf[ws, HALF:, :].astype(BF16))
        b = _dot(xl, wu_buf[ws, :HALF, :].astype(BF16)) + _dot(xr, wu_buf[ws, HALF:, :].astype(BF16))
        hmid = (a * jax.nn.sigmoid(a) * b).astype(BF16)
        y = _dot(hmid, wd_buf[ws].astype(BF16))
        obuf[os] = _pack_halves(y[:, :HALF].astype(BF16).astype(F32), y[:, HALF:].astype(BF16).astype(F32))
        start_out(i, os)
        start_in(ahead, lax.rem(i + 2, X_SLOTS), ahead_count)
        return carry

    lax.fori_loop(0, n, block, 0)

    @pl.when(n >= 2)
    def _():
        wait_out(n - 2, n % 2)
    wait_out(n - 1, (n - 1) % 2)


def _experts(plan, xs_loc, w_gate, w_up, w_down):
    n_blk = plan["blk_e"].shape[0]
    hbm = pl.BlockSpec(memory_space=pl.ANY)
    return pl.pallas_call(
        _experts_kernel,
        grid_spec=pltpu.PrefetchScalarGridSpec(
            num_scalar_prefetch=7,
            grid=(1,),
            in_specs=[hbm, hbm, hbm, hbm],
            out_specs=hbm,
            scratch_shapes=[
                pltpu.VMEM((2, D_MODEL, D_EXPERT), F32),
                pltpu.VMEM((2, D_MODEL, D_EXPERT), F32),
                pltpu.VMEM((2, D_EXPERT, D_MODEL), F32),
                pltpu.VMEM((X_SLOTS, EXPERT_ROWS, HALF), I32),
                pltpu.VMEM((2, EXPERT_ROWS, HALF), I32),
                pltpu.SemaphoreType.DMA((3, 2)),
                pltpu.SemaphoreType.DMA((X_SLOTS,)),
                pltpu.SemaphoreType.DMA((2,)),
            ],
        ),
        out_shape=jax.ShapeDtypeStruct(xs_loc.shape, I32),
        input_output_aliases={7: 0},
        compiler_params=pltpu.CompilerParams(
            dimension_semantics=("arbitrary",), vmem_limit_bytes=VMEM_LIMIT),
        name="experts",
    )(plan["blk_e"], plan["first"], plan["next_e"], plan["slot"], plan["nblk"], plan["row"], plan["ngran"],
      xs_loc, w_gate, w_up, w_down)


def _combine_kernel(x2_ref, pos_ref, gate_ref, g_ref, ys_ref, o_ref):
    n_tok = x2_ref.shape[0]
    r = lax.broadcasted_iota(I32, (LOCAL_ROWS, n_tok), 0)
    w_t = (jnp.where(r == pos_ref[0:1, :], gate_ref[0:1, :], 0.0)
           + jnp.where(r == pos_ref[1:2, :], gate_ref[1:2, :], 0.0)).astype(BF16)
    yl, yr = _unpack_halves(ys_ref[...])
    moe = jnp.concatenate([_dot_tn(w_t, yl), _dot_tn(w_t, yr)], axis=-1)
    o_ref[...] = _rms(x2_ref[...] + moe, g_ref[...])


def _combine(x2, pos_tk, gates_tk, g_final, ys_loc):
    t = x2.shape[0]
    return pl.pallas_call(
        _combine_kernel,
        grid=(t // TRUNK_ROWS,),
        in_specs=[
            pl.BlockSpec((TRUNK_ROWS, D_MODEL), lambda i: (i, 0)),
            pl.BlockSpec((8, TRUNK_ROWS), lambda i: (0, i)),
            pl.BlockSpec((8, TRUNK_ROWS), lambda i: (0, i)),
            pl.BlockSpec((1, D_MODEL), lambda i: (0, 0)),
            pl.BlockSpec((LOCAL_ROWS, HALF), lambda i: (i, 0)),
        ],
        out_specs=pl.BlockSpec((TRUNK_ROWS, D_MODEL), lambda i: (i, 0)),
        out_shape=jax.ShapeDtypeStruct((t, D_MODEL), F32),
        compiler_params=pltpu.CompilerParams(vmem_limit_bytes=VMEM_LIMIT),
        name="combine",
    )(x2, pos_tk, gates_tk, g_final, ys_loc)


def _router_params(w_rg, b_rg, w_re, b_re):
    w = jnp.zeros((ROUTER_ROWS, D_MODEL), F32)
    w = w.at[0:N_GROUPS].set(w_rg.T).at[8:8 + N_EXPERTS].set(w_re.T)
    b = jnp.zeros((ROUTER_ROWS,), F32)
    b = b.at[0:N_GROUPS].set(b_rg).at[N_GROUPS:8].set(NEG_BIG).at[8:8 + N_EXPERTS].set(b_re)
    return w.astype(BF16), b.reshape(ROUTER_ROWS, 1)


def _plan(cnt, n_global_rows):
    n_tiles = cnt.shape[0]
    piece = (cnt + GRANULE - 1) // GRANULE * GRANULE
    lend = jnp.cumsum(piece, axis=1)
    lstart = lend - piece
    tot = jnp.sum(piece, axis=0)
    padded = (tot + EXPERT_ROWS - 1) // EXPERT_ROWS * EXPERT_ROWS
    pend = jnp.cumsum(padded)
    pstart = pend - padded
    cum_tiles = jnp.cumsum(piece, axis=0)

    n_blk = n_global_rows // EXPERT_ROWS
    blk_start = jnp.arange(n_blk, dtype=I32) * EXPERT_ROWS
    blk_e = jnp.minimum(jnp.sum((pend[None, :] <= blk_start[:, None]).astype(I32), axis=1), N_EXPERTS - 1)
    nblk = pend[-1:] // EXPERT_ROWS

    of_blk_e = blk_e[:, None] == jnp.arange(N_EXPERTS, dtype=I32)[None, :]
    pick = lambda table: jnp.sum(jnp.where(of_blk_e[:, None, :], table[None, :, :], 0), axis=2)
    pick1 = lambda vec: jnp.sum(jnp.where(of_blk_e, vec[None, :], 0), axis=1)
    off = (blk_start - pick1(pstart))[:, None] + jnp.arange(EXPERT_ROWS // GRANULE, dtype=I32)[None, :] * GRANULE
    real = (off < pick1(tot)[:, None]) & (blk_start < pend[-1])[:, None]
    ngran = jnp.sum(real.astype(I32), axis=1)
    cum_b = pick(cum_tiles)
    tile_of = jnp.minimum(jnp.sum((cum_b[:, None, :] <= off[:, :, None]).astype(I32), axis=2), n_tiles - 1)
    base = (jnp.arange(n_tiles, dtype=I32) * LOCAL_ROWS)[None, :] + pick(lstart) - (cum_b - pick(piece))
    of_tile = tile_of[:, :, None] == jnp.arange(n_tiles, dtype=I32)
    row = jnp.sum(jnp.where(of_tile, base[:, None, :], 0), axis=2) + off
    row = jnp.where(real, row, 0)
    blk = jnp.arange(n_blk, dtype=I32)
    valid = blk < nblk
    change = jnp.concatenate([jnp.ones((1,), bool), blk_e[1:] != blk_e[:-1]])
    slot = (jnp.cumsum(change.astype(I32)) - 1) % 2
    later = (blk_e[None, :] > blk_e[:, None]) & valid[None, :]
    next_e = jnp.min(jnp.where(later, blk_e[None, :], N_EXPERTS), axis=1)
    next_e = jnp.where(next_e == N_EXPERTS, -1, next_e)
    as_i32 = lambda v: v.astype(I32)
    return dict(
        row=as_i32(row.reshape(-1)), ngran=as_i32(ngran), blk_e=as_i32(blk_e), nblk=as_i32(nblk),
        first=as_i32(change & valid), slot=as_i32(slot), next_e=as_i32(next_e),
    )


def _layer(x2d, mem2d, batch, seq, norm_mix_g, w_in, conv_w, conv_b, head_norm_g, w_out,
           norm_xa_g, norm_mem_g, w_q, w_kv, w_o, norm_ffn_g, w_rg, b_rg, w_re, b_re,
           w_gate, w_up, w_down, out_norm_g):
    t = x2d.shape[0]
    n_tiles = t // TRUNK_ROWS
    row = lambda v: v.reshape(1, -1)
    hg = head_norm_g.reshape(-1)
    gm = _group_mean_matrix()

    kv = _kv_proj(mem2d, row(norm_mem_g), w_kv)
    conv_n, uf = _mixer_in(x2d, row(norm_mix_g), w_in.astype(BF16), conv_w, row(conv_b),
                           row(hg[:CONV_CH]), gm, batch, seq)
    fft_n = _fourier(uf.reshape(batch, seq, FFT_CH), _fft_stage2_matrices(seq), _fft_channel_matrix(seq),
                     gm, row(hg[CONV_CH:]), batch, seq)
    w_r_t, b_r = _router_params(w_rg, b_rg, w_re, b_re)
    x2, xs_loc, pos, gates, cnt = _trunk(
        x2d, conv_n, fft_n, w_out.astype(BF16), row(norm_xa_g), w_q.astype(BF16), kv, w_o.astype(BF16),
        row(norm_ffn_g), w_r_t, b_r, _strict_upper(TRUNK_ROWS), _strict_lower(N_EXPERTS), seq)

    max_rows = n_tiles * LOCAL_ROWS + N_EXPERTS * (EXPERT_ROWS - GRANULE)
    n_global_rows = -(-max_rows // EXPERT_ROWS) * EXPERT_ROWS
    plan = _plan(cnt[:, :, 0], n_global_rows)
    ys_loc = _experts(plan, xs_loc, w_gate, w_up, w_down)
    return _combine(x2, pos, gates, row(out_norm_g), ys_loc)


def kernel(x, mem, norm_mix_g, w_in, conv_w, conv_b, head_norm_g, w_out, norm_xa_g, norm_mem_g, w_q, w_kv,
           w_o, norm_ffn_g, w_route_group, b_route_group, w_route_expert, b_route_expert, w_gate, w_up,
           w_down, final_norm_g):
    batch, seq, _ = x.shape
    depth = norm_mix_g.shape[0]
    assert depth == 1, "the final norm is fused into the last layer's combine kernel"
    x2d = x.reshape(batch * seq, D_MODEL)
    mem2d = mem.reshape(batch * MEM_LEN, D_MODEL)
    l = 0
    out = _layer(x2d, mem2d, batch, seq, norm_mix_g[l], w_in[l], conv_w[l], conv_b[l], head_norm_g[l],
                 w_out[l], norm_xa_g[l], norm_mem_g[l], w_q[l], w_kv[l], w_o[l], norm_ffn_g[l],
                 w_route_group[l], b_route_group[l], w_route_expert[l], b_route_expert[l],
                 w_gate[l], w_up[l], w_down[l], final_norm_g)
    return out.reshape(batch, seq, D_MODEL)
```

```python
import math

import numpy as np
import jax
import jax.numpy as jnp
from jax import lax
from jax.experimental import pallas as pl
from jax.experimental.pallas import tpu as pltpu

F32 = jnp.float32
BF16 = jnp.bfloat16
I32 = jnp.int32

D_MODEL = 1024
HALF = D_MODEL // 2
HEAD_DIM = 64
CONV_CH = 512
FFT_CH = 512
IN_COLS = 3 * CONV_CH + FFT_CH
MEM_LEN = 256
XA_HEADS = 4
XA_HEAD_DIM = D_MODEL // XA_HEADS
N_GROUPS = 4
EXPERTS_PER_GROUP = 8
N_EXPERTS = 32
TOP_K = 2
D_EXPERT = 512
EPS = 1e-6

FFT_N1 = 16
FFT_N2 = 256
FFT_K1_PER_STEP = 4

LANES = 128
MXU_COLS = 256
GRANULE = 8
MIX_ROWS = 512
TRUNK_ROWS = 512
LOCAL_ROWS = TOP_K * TRUNK_ROWS + N_EXPERTS * GRANULE
LOCAL_GRANULES = LOCAL_ROWS // GRANULE
EXPERT_ROWS = 256
X_SLOTS = 3
WEIGHT_DMA_PRIORITY = 1
ROUTER_ROWS = 128
NEG_BIG = -1e30
HI16 = -65536

VMEM_LIMIT = 56 * 1024 * 1024


def _rms(x, g):
    return x * lax.rsqrt(jnp.mean(x * x, axis=-1, keepdims=True) + EPS) * g


def _dot(a, b):
    return jnp.dot(a, b, preferred_element_type=F32)


def _dot_nt(a, b):
    return lax.dot_general(a, b, (((1,), (1,)), ((), ())), preferred_element_type=F32)


def _dot_tn(a, b):
    return lax.dot_general(a, b, (((0,), (0,)), ((), ())), preferred_element_type=F32)


def _pack_halves(left_f32, right_f32):
    lb = lax.bitcast_convert_type(left_f32, I32)
    rb = lax.shift_right_logical(lax.bitcast_convert_type(right_f32, I32), jnp.int32(16))
    return lb | rb


def _unpack_halves(packed_i32):
    left = lax.bitcast_convert_type(packed_i32 & jnp.int32(HI16), F32)
    right = lax.bitcast_convert_type(lax.shift_left(packed_i32, jnp.int32(16)), F32)
    return left.astype(BF16), right.astype(BF16)


def _group_mean_matrix():
    g = np.kron(np.eye(MXU_COLS // HEAD_DIM), np.full((HEAD_DIM, HEAD_DIM), 1.0 / HEAD_DIM))
    return jnp.asarray(g, dtype=BF16)


def _head_mean_square(y, gm):
    sq = (y * y).astype(BF16)
    return jnp.concatenate([_dot(sq[:, c:c + MXU_COLS], gm) for c in range(0, y.shape[1], MXU_COLS)], axis=1)


def _fft_stage2_matrices(seq):
    k1 = np.arange(FFT_N1)[:, None, None]
    k2 = np.arange(FFT_N2)[None, :, None]
    s2 = np.arange(FFT_N2)[None, None, :]
    ang = 2.0 * np.pi * ((s2 * (k1 + FFT_N1 * k2)) % seq) / seq
    c, s = np.cos(ang), np.sin(ang)
    top = np.concatenate([c, s], axis=2)
    bot = np.concatenate([-s, c], axis=2)
    return jnp.asarray(np.concatenate([top, bot], axis=1), dtype=BF16)


def _fft_channel_matrix(seq):
    c = np.arange(HEAD_DIM)
    ang = 2.0 * np.pi * ((c[:, None] * c[None, :]) % HEAD_DIM) / HEAD_DIM
    scale = 1.0 / math.sqrt(seq * HEAD_DIM)
    eye = np.eye(MXU_COLS // HEAD_DIM)
    cs = np.concatenate([np.kron(eye, np.cos(ang)), np.kron(eye, np.sin(ang))], axis=0) * scale
    return jnp.asarray(cs, dtype=BF16)


def _strict_upper(n):
    return jnp.asarray(np.triu(np.ones((n, n)), k=1), dtype=BF16)


def _strict_lower(n):
    return jnp.asarray(np.tril(np.ones((n, n)), k=-1), dtype=BF16)


def _kv_kernel(mem_ref, g_ref, w_ref, o_ref):
    h = _rms(mem_ref[...], g_ref[...]).astype(BF16)
    o_ref[...] = _dot(h, w_ref[...].astype(BF16)).astype(BF16)


def _kv_proj(mem2d, g, w_kv):
    rows = mem2d.shape[0]
    cols = w_kv.shape[1]
    cb = 512
    return pl.pallas_call(
        _kv_kernel,
        grid=(cols // cb,),
        in_specs=[
            pl.BlockSpec((rows, D_MODEL), lambda j: (0, 0)),
            pl.BlockSpec((1, D_MODEL), lambda j: (0, 0)),
            pl.BlockSpec((D_MODEL, cb), lambda j: (0, j)),
        ],
        out_specs=pl.BlockSpec((rows, cb), lambda j: (0, j)),
        out_shape=jax.ShapeDtypeStruct((rows, cols), BF16),
        compiler_params=pltpu.CompilerParams(vmem_limit_bytes=VMEM_LIMIT),
        name="kv_proj",
    )(mem2d, g, w_kv)


def _mixer_in_kernel(x_ref, xp_ref, xn_ref, g_ref, w_ref, cw_ref, cb_ref, hg_ref, gm_ref,
                     conv_ref, uf_ref):
    i = pl.program_id(1)
    n_i = pl.num_programs(1)
    rows = x_ref.shape[0]
    g = g_ref[...]
    h = _rms(x_ref[...], g).astype(BF16)
    u_cv = _dot(h, w_ref[:, CONV_CH:3 * CONV_CH])
    cv = u_cv[:, :CONV_CH] * u_cv[:, CONV_CH:]

    hh = jnp.concatenate([_rms(xp_ref[...], g), _rms(xn_ref[...], g)], axis=0).astype(BF16)
    uh = _dot(hh, w_ref[:, CONV_CH:3 * CONV_CH])
    cvh = uh[:, :CONV_CH] * uh[:, CONV_CH:]
    cv_prev = cvh[7:8, :] * jnp.where(i == 0, 0.0, 1.0)
    cv_next = cvh[8:9, :] * jnp.where(i == n_i - 1, 0.0, 1.0)

    row = lax.broadcasted_iota(I32, cv.shape, 0)
    cv_up = jnp.where(row == 0, cv_prev, pltpu.roll(cv, 1, 0))
    cv_dn = jnp.where(row == rows - 1, cv_next, pltpu.roll(cv, rows - 1, 0))
    z = cw_ref[0:1, :] * cv_up + cw_ref[1:2, :] * cv + cw_ref[2:3, :] * cv_dn + cb_ref[...]
    uf_ref[...] = _dot(h, w_ref[:, 3 * CONV_CH:]).astype(BF16)
    y = _dot(h, w_ref[:, :CONV_CH]) * z
    ms = _head_mean_square(y, gm_ref[...])
    conv_ref[...] = (y * lax.rsqrt(ms + EPS) * hg_ref[...]).astype(BF16)


def _mixer_in(x2d, g, w_in, conv_w, conv_b, hg_conv, gm, batch, seq):
    n_i = seq // MIX_ROWS
    t = x2d.shape[0]
    r8 = MIX_ROWS // 8
    last8 = t // 8 - 1
    return pl.pallas_call(
        _mixer_in_kernel,
        grid=(batch, n_i),
        in_specs=[
            pl.BlockSpec((MIX_ROWS, D_MODEL), lambda b, i: (b * n_i + i, 0)),
            pl.BlockSpec((8, D_MODEL), lambda b, i: (jnp.maximum((b * n_i + i) * r8 - 1, 0), 0)),
            pl.BlockSpec((8, D_MODEL), lambda b, i: (jnp.minimum((b * n_i + i + 1) * r8, last8), 0)),
            pl.BlockSpec((1, D_MODEL), lambda b, i: (0, 0)),
            pl.BlockSpec((D_MODEL, IN_COLS), lambda b, i: (0, 0)),
            pl.BlockSpec((3, CONV_CH), lambda b, i: (0, 0)),
            pl.BlockSpec((1, CONV_CH), lambda b, i: (0, 0)),
            pl.BlockSpec((1, CONV_CH), lambda b, i: (0, 0)),
            pl.BlockSpec((MXU_COLS, MXU_COLS), lambda b, i: (0, 0)),
        ],
        out_specs=[
            pl.BlockSpec((MIX_ROWS, CONV_CH), lambda b, i: (b * n_i + i, 0)),
            pl.BlockSpec((MIX_ROWS, FFT_CH), lambda b, i: (b * n_i + i, 0)),
        ],
        out_shape=[
            jax.ShapeDtypeStruct((t, CONV_CH), BF16),
            jax.ShapeDtypeStruct((t, FFT_CH), BF16),
        ],
        compiler_params=pltpu.CompilerParams(vmem_limit_bytes=VMEM_LIMIT),
        name="mixer_in",
    )(x2d, x2d, x2d, g, w_in, conv_w, conv_b, hg_conv, gm)


_S1_ROWS = 16
_S1_LANES = 128


def _lincomb(terms):
    acc = None
    for coef, val in terms:
        if abs(coef) < 1e-12:
            continue
        if abs(coef - 1.0) < 1e-12:
            term, neg = val, False
        elif abs(coef + 1.0) < 1e-12:
            term, neg = val, True
        else:
            term, neg = coef * val, False
        if acc is None:
            acc = -term if neg else term
        else:
            acc = acc - term if neg else acc + term
    return acc


def _fft_stage1(x_ref, a_ref):
    half = FFT_N1 // 2
    cos = [[math.cos(2 * math.pi * ((k * j) % FFT_N1) / FFT_N1) for j in range(FFT_N1)] for k in range(FFT_N1)]
    sin = [[math.sin(2 * math.pi * ((k * j) % FFT_N1) / FFT_N1) for j in range(FFT_N1)] for k in range(FFT_N1)]

    def body(r, carry):
        r0 = pl.multiple_of(r * _S1_ROWS, _S1_ROWS)
        rows_re = pl.ds(r0, _S1_ROWS)
        rows_im = pl.ds(r0 + FFT_N2, _S1_ROWS)
        for lc in range(0, FFT_CH, _S1_LANES):
            lanes = slice(lc, lc + _S1_LANES)
            xs = [x_ref[j, rows_re, lanes].astype(F32) for j in range(FFT_N1)]
            ev = [None] + [xs[j] + xs[FFT_N1 - j] for j in range(1, half)]
            od = [None] + [xs[j] - xs[FFT_N1 - j] for j in range(1, half)]
            for k in range(half + 1):
                re = _lincomb([(1.0, xs[0]), (cos[k][half], xs[half])]
                              + [(cos[k][j], ev[j]) for j in range(1, half)])
                a_ref[k, rows_re, lanes] = re.astype(BF16)
                if k in (0, half):
                    zero = jnp.zeros_like(re).astype(BF16)
                    a_ref[k, rows_im, lanes] = zero
                else:
                    im = _lincomb([(-sin[k][j], od[j]) for j in range(1, half)])
                    a_ref[k, rows_im, lanes] = im.astype(BF16)
                    a_ref[FFT_N1 - k, rows_re, lanes] = re.astype(BF16)
                    a_ref[FFT_N1 - k, rows_im, lanes] = (-im).astype(BF16)
        return carry

    lax.fori_loop(0, FFT_N2 // _S1_ROWS, body, 0)


def _fourier_kernel(x_ref, m2_ref, cs_ref, gm_ref, hg_ref, o_ref, a_ref, y_ref):
    j = pl.program_id(1)

    @pl.when(j == 0)
    def _():
        _fft_stage1(x_ref, a_ref)

    ris = [_dot(m2_ref[kk], a_ref[j * FFT_K1_PER_STEP + kk]) for kk in range(FFT_K1_PER_STEP)]
    re = jnp.concatenate([ri[:FFT_N2] for ri in ris], axis=0).astype(BF16)
    im = jnp.concatenate([ri[FFT_N2:] for ri in ris], axis=0).astype(BF16)
    y = jnp.concatenate(
        [_dot(re[:, c:c + MXU_COLS], cs_ref[:MXU_COLS, :]) + _dot(im[:, c:c + MXU_COLS], cs_ref[MXU_COLS:, :])
         for c in range(0, FFT_CH, MXU_COLS)], axis=1)
    yn = y * lax.rsqrt(_head_mean_square(y, gm_ref[...]) + EPS) * hg_ref[...]
    for kk in range(FFT_K1_PER_STEP):
        k1 = j * FFT_K1_PER_STEP + kk
        for c in range(FFT_CH // LANES):
            y_ref[c, pl.ds(k1, FFT_N2, stride=FFT_N1), :] = yn[kk * FFT_N2:(kk + 1) * FFT_N2,
                                                               c * LANES:(c + 1) * LANES]

    @pl.when(j == pl.num_programs(1) - 1)
    def _():
        for c in range(FFT_CH // LANES):
            o_ref[:, c * LANES:(c + 1) * LANES] = y_ref[c].astype(BF16)


def _fourier(uf, m2, cs, gm, hg_fft, batch, seq):
    assert seq == FFT_N1 * FFT_N2
    x4 = uf.reshape(batch, FFT_N1, FFT_N2, FFT_CH)
    out = pl.pallas_call(
        _fourier_kernel,
        grid=(batch, FFT_N1 // FFT_K1_PER_STEP),
        in_specs=[
            pl.BlockSpec((None, FFT_N1, FFT_N2, FFT_CH), lambda b, j: (b, 0, 0, 0)),
            pl.BlockSpec((FFT_K1_PER_STEP, 2 * FFT_N2, 2 * FFT_N2), lambda b, j: (j, 0, 0)),
            pl.BlockSpec((2 * MXU_COLS, MXU_COLS), lambda b, j: (0, 0)),
            pl.BlockSpec((MXU_COLS, MXU_COLS), lambda b, j: (0, 0)),
            pl.BlockSpec((1, FFT_CH), lambda b, j: (0, 0)),
        ],
        out_specs=pl.BlockSpec((seq, FFT_CH), lambda b, j: (b, 0)),
        out_shape=jax.ShapeDtypeStruct((batch * seq, FFT_CH), BF16),
        scratch_shapes=[
            pltpu.VMEM((FFT_N1, 2 * FFT_N2, FFT_CH), BF16),
            pltpu.VMEM((FFT_CH // LANES, seq, LANES), F32),
        ],
        compiler_params=pltpu.CompilerParams(
            dimension_semantics=("arbitrary", "arbitrary"), vmem_limit_bytes=VMEM_LIMIT),
        name="fourier",
    )(x4, m2, cs, gm, hg_fft)
    return out


def _first_index_of_max(vals, vmax, row):
    return jnp.min(jnp.where(vals == vmax, row, vals.shape[0]), axis=0, keepdims=True)


def _route(lg):
    cols = lg.shape[1]
    row8 = lax.broadcasted_iota(I32, (EXPERTS_PER_GROUP, cols), 0)
    gl = lg[0:8, :]
    gmax = jnp.max(gl, axis=0, keepdims=True)
    g_w = 1.0 / jnp.sum(jnp.exp(gl - gmax), axis=0, keepdims=True)
    g_idx = _first_index_of_max(gl, gmax, row8)

    el = lg[8:16, :]
    for g in range(1, N_GROUPS):
        el = jnp.where(g_idx == g, lg[8 + 8 * g:16 + 8 * g, :], el)
    emax = jnp.max(el, axis=0, keepdims=True)
    ee = jnp.exp(el - emax)
    e_prob = ee / jnp.sum(ee, axis=0, keepdims=True)
    p1 = jnp.max(e_prob, axis=0, keepdims=True)
    i1 = _first_index_of_max(e_prob, p1, row8)
    rest = jnp.where(row8 == i1, -1.0, e_prob)
    p2 = jnp.max(rest, axis=0, keepdims=True)
    i2 = _first_index_of_max(rest, p2, row8)
    denom = p1 + p2
    e1 = g_idx * EXPERTS_PER_GROUP + i1
    e2 = g_idx * EXPERTS_PER_GROUP + i2
    return e1, e2, g_w * p1 / denom, g_w * p2 / denom


def _trunk_kernel(x_ref, conv_ref, fft_ref, wout_ref, gxa_ref, wq_ref, k_ref, v_ref, wo_ref,
                  gffn_ref, wr_ref, br_ref, tri_ref, ltri_ref,
                  x2_ref, xs_ref, pos_ref, gate_ref, cnt_ref, h3_s, lg_s):
    @pl.when(pl.program_id(0) == 0)
    def _():
        h3_s[...] = jnp.zeros_like(h3_s)
        lg_s[...] = jnp.zeros_like(lg_s)

    h3 = h3_s[...]
    lg = lg_s[...]
    n_tok = lg.shape[1]

    x1 = x_ref[...] + _dot(jnp.concatenate([conv_ref[...], fft_ref[...]], axis=1), wout_ref[...])

    e1, e2, gate1, gate2 = _route(lg)
    row32 = lax.broadcasted_iota(I32, (N_EXPERTS, n_tok), 0)
    hit1 = row32 == e1
    hit2 = row32 == e2
    onehot = jnp.where(hit1 | hit2, 1.0, 0.0)
    before = _dot(onehot.astype(BF16), tri_ref[...])
    cnt = jnp.sum(onehot, axis=1, keepdims=True).astype(I32)
    piece = jnp.left_shift(jnp.right_shift(cnt + (GRANULE - 1), 3), 3)
    piece_b = jnp.broadcast_to(piece.astype(F32), (N_EXPERTS, LANES)).astype(BF16)
    start = _dot(ltri_ref[...], piece_b)[:, 0:1]
    slot = before + start
    pos1 = jnp.sum(jnp.where(hit1, slot, 0.0), axis=0, keepdims=True).astype(I32)
    pos2 = jnp.sum(jnp.where(hit2, slot, 0.0), axis=0, keepdims=True).astype(I32)
    pos_ref[0:1, :] = pos1
    pos_ref[1:2, :] = pos2
    pos_ref[2:8, :] = jnp.zeros((6, n_tok), I32)
    gate_ref[0:1, :] = gate1
    gate_ref[1:2, :] = gate2
    gate_ref[2:8, :] = jnp.zeros((6, n_tok), F32)
    cnt_ref[...] = jnp.broadcast_to(cnt, (N_EXPERTS, LANES))

    h2 = _rms(x1, gxa_ref[...]).astype(BF16)
    q = _dot(h2, wq_ref[...]).astype(BF16)

    r = lax.broadcasted_iota(I32, (LOCAL_ROWS, n_tok), 0)
    perm = jnp.where((r == pos1) | (r == pos2), 1.0, 0.0).astype(BF16)

    outs = []
    for hd in range(XA_HEADS):
        cols = slice(hd * XA_HEAD_DIM, (hd + 1) * XA_HEAD_DIM)
        s = _dot_nt(q[:, cols], k_ref[:, cols]) * (XA_HEAD_DIM ** -0.5)
        s = s - jnp.max(s, axis=-1, keepdims=True)
        p = jnp.exp(s)
        p = p / jnp.sum(p, axis=-1, keepdims=True)
        outs.append(_dot(p.astype(BF16), v_ref[:, cols]).astype(BF16))
        if hd == 1:
            sorted_left = _dot(perm, h3[:, :HALF])

    o = jnp.concatenate(outs, axis=-1)
    x2 = x1 + _dot(o, wo_ref[...])
    x2_ref[...] = x2
    sorted_right = _dot(perm, h3[:, HALF:])
    h3_next = _rms(x2, gffn_ref[...]).astype(BF16)
    xs_ref[...] = _pack_halves(sorted_left, sorted_right)
    h3_s[...] = h3_next
    lg_s[...] = _dot_nt(wr_ref[...], h3_next) + br_ref[...]


def _trunk(x2d, conv_n, fft_n, w_out, g_xa, w_q, kv, w_o, g_ffn, w_r_t, b_r, tri, ltri, seq):
    t = x2d.shape[0]
    n_tiles = t // TRUNK_ROWS
    n_per_batch = seq // TRUNK_ROWS
    const = lambda i: (0, 0)
    dense = lambda i: jnp.minimum(i, n_tiles - 1)
    routed = lambda i: jnp.maximum(i - 1, 0)
    return pl.pallas_call(
        _trunk_kernel,
        grid=(n_tiles + 1,),
        in_specs=[
            pl.BlockSpec((TRUNK_ROWS, D_MODEL), lambda i: (dense(i), 0)),
            pl.BlockSpec((TRUNK_ROWS, CONV_CH), lambda i: (dense(i), 0)),
            pl.BlockSpec((TRUNK_ROWS, FFT_CH), lambda i: (dense(i), 0)),
            pl.BlockSpec((D_MODEL, D_MODEL), const),
            pl.BlockSpec((1, D_MODEL), const),
            pl.BlockSpec((D_MODEL, D_MODEL), const),
            pl.BlockSpec((MEM_LEN, D_MODEL), lambda i: (dense(i) // n_per_batch, 0)),
            pl.BlockSpec((MEM_LEN, D_MODEL), lambda i: (dense(i) // n_per_batch, 1)),
            pl.BlockSpec((D_MODEL, D_MODEL), const),
            pl.BlockSpec((1, D_MODEL), const),
            pl.BlockSpec((ROUTER_ROWS, D_MODEL), const),
            pl.BlockSpec((ROUTER_ROWS, 1), const),
            pl.BlockSpec((TRUNK_ROWS, TRUNK_ROWS), const),
            pl.BlockSpec((N_EXPERTS, N_EXPERTS), const),
        ],
        out_specs=[
            pl.BlockSpec((TRUNK_ROWS, D_MODEL), lambda i: (dense(i), 0)),
            pl.BlockSpec((LOCAL_ROWS, HALF), lambda i: (routed(i), 0)),
            pl.BlockSpec((8, TRUNK_ROWS), lambda i: (0, routed(i))),
            pl.BlockSpec((8, TRUNK_ROWS), lambda i: (0, routed(i))),
            pl.BlockSpec((None, N_EXPERTS, LANES), lambda i: (routed(i), 0, 0)),
        ],
        out_shape=[
            jax.ShapeDtypeStruct((t, D_MODEL), F32),
            jax.ShapeDtypeStruct((n_tiles * LOCAL_ROWS, HALF), I32),
            jax.ShapeDtypeStruct((8, t), I32),
            jax.ShapeDtypeStruct((8, t), F32),
            jax.ShapeDtypeStruct((n_tiles, N_EXPERTS, LANES), I32),
        ],
        scratch_shapes=[
            pltpu.VMEM((TRUNK_ROWS, D_MODEL), BF16),
            pltpu.VMEM((ROUTER_ROWS, TRUNK_ROWS), F32),
        ],
        compiler_params=pltpu.CompilerParams(
            dimension_semantics=("arbitrary",), vmem_limit_bytes=VMEM_LIMIT),
        name="trunk",
    )(x2d, conv_n, fft_n, w_out, g_xa, w_q, kv, kv, w_o, g_ffn, w_r_t, b_r, tri, ltri)


def _experts_kernel(blk_e_ref, first_ref, next_e_ref, slot_ref, nblk_ref, gran_ref, ngran_ref,
                    xs_hbm, wg_hbm, wu_hbm, wd_hbm, ys_hbm,
                    wg_buf, wu_buf, wd_buf, xbuf, obuf, wsems, xsems, osems):
    n = nblk_ref[0]
    gpb = EXPERT_ROWS // GRANULE

    def fetch_weights(e, s):
        return (pltpu.make_async_copy(wg_hbm.at[e], wg_buf.at[s], wsems.at[0, s]),
                pltpu.make_async_copy(wu_hbm.at[e], wu_buf.at[s], wsems.at[1, s]),
                pltpu.make_async_copy(wd_hbm.at[e], wd_buf.at[s], wsems.at[2, s]))

    def start_in(b, s, count):
        for g in range(gpb):
            @pl.when(g < count)
            def _():
                pltpu.make_async_copy(xs_hbm.at[gran_ref[b * gpb + g]], xbuf.at[s, g],
                                      xsems.at[s]).start(priority=g % 2)

    def start_out(b, s):
        count = ngran_ref[b]
        for g in range(gpb):
            @pl.when(g < count)
            def _():
                pltpu.make_async_copy(obuf.at[s, g], ys_hbm.at[gran_ref[b * gpb + g]],
                                      osems.at[s]).start(priority=g % 2)

    def wait_in(b, s):
        count = ngran_ref[b]
        pltpu.make_async_copy(xs_hbm.at[pl.ds(0, count)], xbuf.at[s, pl.ds(0, count)], xsems.at[s]).wait()

    def wait_out(b, s):
        count = ngran_ref[b]
        pltpu.make_async_copy(obuf.at[s, pl.ds(0, count)], ys_hbm.at[pl.ds(0, count)], osems.at[s]).wait()

    xbuf[...] = jnp.zeros_like(xbuf)
    for cp in fetch_weights(blk_e_ref[0], 0):
        cp.start(priority=WEIGHT_DMA_PRIORITY)
    last = blk_e_ref.shape[0] - 1
    start_in(0, 0, ngran_ref[0])
    start_in(1, 1, jnp.where(n > 1, ngran_ref[1], 0))

    def block(i, carry):
        xs = lax.rem(i, X_SLOTS)
        os = i % 2
        ws = slot_ref[i]

        @pl.when(first_ref[i] == 1)
        def _():
            for cp in fetch_weights(blk_e_ref[i], ws):
                cp.wait()

            @pl.when(next_e_ref[i] >= 0)
            def _():
                for cp in fetch_weights(next_e_ref[i], 1 - ws):
                    cp.start(priority=WEIGHT_DMA_PRIORITY)

        @pl.when(i >= 2)
        def _():
            wait_out(i - 2, os)

        ahead = jnp.minimum(i + 2, last)
        ahead_count = jnp.where(i + 2 < n, ngran_ref[ahead], 0)

        wait_in(i, xs)
        xl, xr = _unpack_halves(xbuf[xs].reshape(EXPERT_ROWS, HALF))
        x = jnp.concatenate([xl, xr], axis=1)
        a = _dot(x, wg_buf[ws].astype(BF16))
        b = _dot(x, wu_buf[ws].astype(BF16))
        hmid = (a * jax.nn.sigmoid(a) * b).astype(BF16)
        y = _dot(hmid, wd_buf[ws].astype(BF16))
        packed = _pack_halves(y[:, :HALF].astype(BF16).astype(F32), y[:, HALF:].astype(BF16).astype(F32))
        obuf[os] = packed.reshape(gpb, GRANULE, HALF)
        start_out(i, os)
        start_in(ahead, lax.rem(i + 2, X_SLOTS), ahead_count)
        return carry

    lax.fori_loop(0, n, block, 0)

    @pl.when(n >= 2)
    def _():
        wait_out(n - 2, n % 2)
    wait_out(n - 1, (n - 1) % 2)


def _experts(plan, xs_loc, w_gate, w_up, w_down):
    granules = xs_loc.reshape(-1, GRANULE, HALF)
    hbm = pl.BlockSpec(memory_space=pl.ANY)
    return pl.pallas_call(
        _experts_kernel,
        grid_spec=pltpu.PrefetchScalarGridSpec(
            num_scalar_prefetch=7,
            grid=(1,),
            in_specs=[hbm, hbm, hbm, hbm],
            out_specs=hbm,
            scratch_shapes=[
                pltpu.VMEM((2, D_MODEL, D_EXPERT), F32),
                pltpu.VMEM((2, D_MODEL, D_EXPERT), F32),
                pltpu.VMEM((2, D_EXPERT, D_MODEL), F32),
                pltpu.VMEM((X_SLOTS, EXPERT_ROWS // GRANULE, GRANULE, HALF), I32),
                pltpu.VMEM((2, EXPERT_ROWS // GRANULE, GRANULE, HALF), I32),
                pltpu.SemaphoreType.DMA((3, 2)),
                pltpu.SemaphoreType.DMA((X_SLOTS,)),
                pltpu.SemaphoreType.DMA((2,)),
            ],
        ),
        out_shape=jax.ShapeDtypeStruct(granules.shape, I32),
        input_output_aliases={7: 0},
        compiler_params=pltpu.CompilerParams(
            dimension_semantics=("arbitrary",), vmem_limit_bytes=VMEM_LIMIT),
        name="experts",
    )(plan["blk_e"], plan["first"], plan["next_e"], plan["slot"], plan["nblk"], plan["gran"], plan["ngran"],
      granules, w_gate, w_up, w_down).reshape(xs_loc.shape)


def _combine_kernel(x2_ref, pos_ref, gate_ref, g_ref, ys_ref, o_ref):
    n_tok = x2_ref.shape[0]
    r = lax.broadcasted_iota(I32, (LOCAL_ROWS, n_tok), 0)
    w_t = (jnp.where(r == pos_ref[0:1, :], gate_ref[0:1, :], 0.0)
           + jnp.where(r == pos_ref[1:2, :], gate_ref[1:2, :], 0.0)).astype(BF16)
    yl, yr = _unpack_halves(ys_ref[...])
    moe = jnp.concatenate([_dot_tn(w_t, yl), _dot_tn(w_t, yr)], axis=-1)
    o_ref[...] = _rms(x2_ref[...] + moe, g_ref[...])


def _combine(x2, pos_tk, gates_tk, g_final, ys_loc):
    t = x2.shape[0]
    return pl.pallas_call(
        _combine_kernel,
        grid=(t // TRUNK_ROWS,),
        in_specs=[
            pl.BlockSpec((TRUNK_ROWS, D_MODEL), lambda i: (i, 0)),
            pl.BlockSpec((8, TRUNK_ROWS), lambda i: (0, i)),
            pl.BlockSpec((8, TRUNK_ROWS), lambda i: (0, i)),
            pl.BlockSpec((1, D_MODEL), lambda i: (0, 0)),
            pl.BlockSpec((LOCAL_ROWS, HALF), lambda i: (i, 0)),
        ],
        out_specs=pl.BlockSpec((TRUNK_ROWS, D_MODEL), lambda i: (i, 0)),
        out_shape=jax.ShapeDtypeStruct((t, D_MODEL), F32),
        compiler_params=pltpu.CompilerParams(vmem_limit_bytes=VMEM_LIMIT),
        name="combine",
    )(x2, pos_tk, gates_tk, g_final, ys_loc)


def _router_params(w_rg, b_rg, w_re, b_re):
    w = jnp.zeros((ROUTER_ROWS, D_MODEL), F32)
    w = w.at[0:N_GROUPS].set(w_rg.T).at[8:8 + N_EXPERTS].set(w_re.T)
    b = jnp.zeros((ROUTER_ROWS,), F32)
    b = b.at[0:N_GROUPS].set(b_rg).at[N_GROUPS:8].set(NEG_BIG).at[8:8 + N_EXPERTS].set(b_re)
    return w.astype(BF16), b.reshape(ROUTER_ROWS, 1)


def _plan(cnt, n_global_rows):
    n_tiles = cnt.shape[0]
    piece = (cnt + GRANULE - 1) // GRANULE * GRANULE
    lend = jnp.cumsum(piece, axis=1)
    lstart = lend - piece
    tot = jnp.sum(piece, axis=0)
    padded = (tot + EXPERT_ROWS - 1) // EXPERT_ROWS * EXPERT_ROWS
    pend = jnp.cumsum(padded)
    pstart = pend - padded
    cum_tiles = jnp.cumsum(piece, axis=0)

    n_blk = n_global_rows // EXPERT_ROWS
    blk_start = jnp.arange(n_blk, dtype=I32) * EXPERT_ROWS
    blk_e = jnp.minimum(jnp.sum((pend[None, :] <= blk_start[:, None]).astype(I32), axis=1), N_EXPERTS - 1)
    nblk = pend[-1:] // EXPERT_ROWS

    of_blk_e = blk_e[:, None] == jnp.arange(N_EXPERTS, dtype=I32)[None, :]
    pick = lambda table: jnp.sum(jnp.where(of_blk_e[:, None, :], table[None, :, :], 0), axis=2)
    pick1 = lambda vec: jnp.sum(jnp.where(of_blk_e, vec[None, :], 0), axis=1)
    off = (blk_start - pick1(pstart))[:, None] + jnp.arange(EXPERT_ROWS // GRANULE, dtype=I32)[None, :] * GRANULE
    real = (off < pick1(tot)[:, None]) & (blk_start < pend[-1])[:, None]
    ngran = jnp.sum(real.astype(I32), axis=1)
    cum_b = pick(cum_tiles)
    tile_of = jnp.minimum(jnp.sum((cum_b[:, None, :] <= off[:, :, None]).astype(I32), axis=2), n_tiles - 1)
    base = (jnp.arange(n_tiles, dtype=I32) * LOCAL_ROWS)[None, :] + pick(lstart) - (cum_b - pick(piece))
    of_tile = tile_of[:, :, None] == jnp.arange(n_tiles, dtype=I32)
    row = jnp.sum(jnp.where(of_tile, base[:, None, :], 0), axis=2) + off
    row = jnp.where(real, row, 0)
    blk = jnp.arange(n_blk, dtype=I32)
    valid = blk < nblk
    change = jnp.concatenate([jnp.ones((1,), bool), blk_e[1:] != blk_e[:-1]])
    slot = (jnp.cumsum(change.astype(I32)) - 1) % 2
    later = (blk_e[None, :] > blk_e[:, None]) & valid[None, :]
    next_e = jnp.min(jnp.where(later, blk_e[None, :], N_EXPERTS), axis=1)
    next_e = jnp.where(next_e == N_EXPERTS, -1, next_e)
    as_i32 = lambda v: v.astype(I32)
    return dict(
        gran=as_i32(row.reshape(-1) // GRANULE), ngran=as_i32(ngran), blk_e=as_i32(blk_e), nblk=as_i32(nblk),
        first=as_i32(change & valid), slot=as_i32(slot), next_e=as_i32(next_e),
    )


def _layer(x2d, mem2d, batch, seq, norm_mix_g, w_in, conv_w, conv_b, head_norm_g, w_out,
           norm_xa_g, norm_mem_g, w_q, w_kv, w_o, norm_ffn_g, w_rg, b_rg, w_re, b_re,
           w_gate, w_up, w_down, out_norm_g):
    t = x2d.shape[0]
    n_tiles = t // TRUNK_ROWS
    row = lambda v: v.reshape(1, -1)
    hg = head_norm_g.reshape(-1)
    gm = _group_mean_matrix()

    kv = _kv_proj(mem2d, row(norm_mem_g), w_kv)
    conv_n, uf = _mixer_in(x2d, row(norm_mix_g), w_in.astype(BF16), conv_w, row(conv_b),
                           row(hg[:CONV_CH]), gm, batch, seq)
    fft_n = _fourier(uf.reshape(batch, seq, FFT_CH), _fft_stage2_matrices(seq), _fft_channel_matrix(seq),
                     gm, row(hg[CONV_CH:]), batch, seq)
    w_r_t, b_r = _router_params(w_rg, b_rg, w_re, b_re)
    x2, xs_loc, pos, gates, cnt = _trunk(
        x2d, conv_n, fft_n, w_out.astype(BF16), row(norm_xa_g), w_q.astype(BF16), kv, w_o.astype(BF16),
        row(norm_ffn_g), w_r_t, b_r, _strict_upper(TRUNK_ROWS), _strict_lower(N_EXPERTS), seq)

    max_rows = n_tiles * LOCAL_ROWS + N_EXPERTS * (EXPERT_ROWS - GRANULE)
    n_global_rows = -(-max_rows // EXPERT_ROWS) * EXPERT_ROWS
    plan = _plan(cnt[:, :, 0], n_global_rows)
    ys_loc = _experts(plan, xs_loc, w_gate, w_up, w_down)
    return _combine(x2, pos, gates, row(out_norm_g), ys_loc)


def kernel(x, mem, norm_mix_g, w_in, conv_w, conv_b, head_norm_g, w_out, norm_xa_g, norm_mem_g, w_q, w_kv,
           w_o, norm_ffn_g, w_route_group, b_route_group, w_route_expert, b_route_expert, w_gate, w_up,
           w_down, final_norm_g):
    batch, seq, _ = x.shape
    depth = norm_mix_g.shape[0]
    assert depth == 1, "the final norm is fused into the last layer's combine kernel"
    x2d = x.reshape(batch * seq, D_MODEL)
    mem2d = mem.reshape(batch * MEM_LEN, D_MODEL)
    l = 0
    out = _layer(x2d, mem2d, batch, seq, norm_mix_g[l], w_in[l], conv_w[l], conv_b[l], head_norm_g[l],
                 w_out[l], norm_xa_g[l], norm_mem_g[l], w_q[l], w_kv[l], w_o[l], norm_ffn_g[l],
                 w_route_group[l], b_route_group[l], w_route_expert[l], b_route_expert[l],
                 w_gate[l], w_up[l], w_down[l], final_norm_g)
    return out.reshape(batch, seq, D_MODEL)
```

```python
import math

import numpy as np
import jax
import jax.numpy as jnp
from jax import lax
from jax.experimental import pallas as pl
from jax.experimental.pallas import tpu as pltpu

F32 = jnp.float32
BF16 = jnp.bfloat16
I32 = jnp.int32

D_MODEL = 1024
HALF = D_MODEL // 2
HEAD_DIM = 64
CONV_CH = 512
FFT_CH = 512
IN_COLS = 3 * CONV_CH + FFT_CH
MEM_LEN = 256
XA_HEADS = 4
XA_HEAD_DIM = D_MODEL // XA_HEADS
N_GROUPS = 4
EXPERTS_PER_GROUP = 8
N_EXPERTS = 32
TOP_K = 2
D_EXPERT = 512
EPS = 1e-6

FFT_N1 = 16
FFT_N2 = 256
FFT_K1_PER_STEP = 4

LANES = 128
MXU_COLS = 256
GRANULE = 8
MIX_ROWS = 512
TRUNK_ROWS = 512
LOCAL_ROWS = TOP_K * TRUNK_ROWS + N_EXPERTS * GRANULE
LOCAL_GRANULES = LOCAL_ROWS // GRANULE
EXPERT_ROWS = 256
X_SLOTS = 3
ROUTER_ROWS = 128
NEG_BIG = -1e30
HI16 = -65536

VMEM_LIMIT = 56 * 1024 * 1024


def _rms(x, g):
    return x * lax.rsqrt(jnp.mean(x * x, axis=-1, keepdims=True) + EPS) * g


def _dot(a, b):
    return jnp.dot(a, b, preferred_element_type=F32)


def _dot_nt(a, b):
    return lax.dot_general(a, b, (((1,), (1,)), ((), ())), preferred_element_type=F32)


def _dot_tn(a, b):
    return lax.dot_general(a, b, (((0,), (0,)), ((), ())), preferred_element_type=F32)


def _pack_halves(left_f32, right_f32):
    lb = lax.bitcast_convert_type(left_f32, I32)
    rb = lax.shift_right_logical(lax.bitcast_convert_type(right_f32, I32), jnp.int32(16))
    return lb | rb


def _unpack_halves(packed_i32):
    left = lax.bitcast_convert_type(packed_i32 & jnp.int32(HI16), F32)
    right = lax.bitcast_convert_type(lax.shift_left(packed_i32, jnp.int32(16)), F32)
    return left.astype(BF16), right.astype(BF16)


def _group_mean_matrix():
    g = np.kron(np.eye(MXU_COLS // HEAD_DIM), np.full((HEAD_DIM, HEAD_DIM), 1.0 / HEAD_DIM))
    return jnp.asarray(g, dtype=BF16)


def _head_mean_square(y, gm):
    sq = (y * y).astype(BF16)
    return jnp.concatenate([_dot(sq[:, c:c + MXU_COLS], gm) for c in range(0, y.shape[1], MXU_COLS)], axis=1)


def _fft_stage2_matrices(seq):
    k1 = np.arange(FFT_N1)[:, None, None]
    k2 = np.arange(FFT_N2)[None, :, None]
    s2 = np.arange(FFT_N2)[None, None, :]
    ang = 2.0 * np.pi * ((s2 * (k1 + FFT_N1 * k2)) % seq) / seq
    c, s = np.cos(ang), np.sin(ang)
    top = np.concatenate([c, s], axis=2)
    bot = np.concatenate([-s, c], axis=2)
    return jnp.asarray(np.concatenate([top, bot], axis=1), dtype=BF16)


def _fft_channel_matrix(seq):
    c = np.arange(HEAD_DIM)
    ang = 2.0 * np.pi * ((c[:, None] * c[None, :]) % HEAD_DIM) / HEAD_DIM
    scale = 1.0 / math.sqrt(seq * HEAD_DIM)
    eye = np.eye(MXU_COLS // HEAD_DIM)
    cs = np.concatenate([np.kron(eye, np.cos(ang)), np.kron(eye, np.sin(ang))], axis=0) * scale
    return jnp.asarray(cs, dtype=BF16)


def _strict_upper(n):
    return jnp.asarray(np.triu(np.ones((n, n)), k=1), dtype=BF16)


def _strict_lower(n):
    return jnp.asarray(np.tril(np.ones((n, n)), k=-1), dtype=BF16)


def _kv_kernel(mem_ref, g_ref, w_ref, o_ref):
    h = _rms(mem_ref[...], g_ref[...]).astype(BF16)
    o_ref[...] = _dot(h, w_ref[...].astype(BF16)).astype(BF16)


def _kv_proj(mem2d, g, w_kv):
    rows = mem2d.shape[0]
    cols = w_kv.shape[1]
    cb = 512
    return pl.pallas_call(
        _kv_kernel,
        grid=(cols // cb,),
        in_specs=[
            pl.BlockSpec((rows, D_MODEL), lambda j: (0, 0)),
            pl.BlockSpec((1, D_MODEL), lambda j: (0, 0)),
            pl.BlockSpec((D_MODEL, cb), lambda j: (0, j)),
        ],
        out_specs=pl.BlockSpec((rows, cb), lambda j: (0, j)),
        out_shape=jax.ShapeDtypeStruct((rows, cols), BF16),
        compiler_params=pltpu.CompilerParams(vmem_limit_bytes=VMEM_LIMIT),
        name="kv_proj",
    )(mem2d, g, w_kv)


def _mixer_in_kernel(x_ref, xp_ref, xn_ref, g_ref, w_ref, cw_ref, cb_ref, hg_ref, gm_ref,
                     conv_ref, uf_ref):
    i = pl.program_id(1)
    n_i = pl.num_programs(1)
    rows = x_ref.shape[0]
    g = g_ref[...]
    h = _rms(x_ref[...], g).astype(BF16)
    u_cv = _dot(h, w_ref[:, CONV_CH:3 * CONV_CH])
    cv = u_cv[:, :CONV_CH] * u_cv[:, CONV_CH:]

    hh = jnp.concatenate([_rms(xp_ref[...], g), _rms(xn_ref[...], g)], axis=0).astype(BF16)
    uh = _dot(hh, w_ref[:, CONV_CH:3 * CONV_CH])
    cvh = uh[:, :CONV_CH] * uh[:, CONV_CH:]
    cv_prev = cvh[7:8, :] * jnp.where(i == 0, 0.0, 1.0)
    cv_next = cvh[8:9, :] * jnp.where(i == n_i - 1, 0.0, 1.0)

    row = lax.broadcasted_iota(I32, cv.shape, 0)
    cv_up = jnp.where(row == 0, cv_prev, pltpu.roll(cv, 1, 0))
    cv_dn = jnp.where(row == rows - 1, cv_next, pltpu.roll(cv, rows - 1, 0))
    z = cw_ref[0:1, :] * cv_up + cw_ref[1:2, :] * cv + cw_ref[2:3, :] * cv_dn + cb_ref[...]
    uf_ref[...] = _dot(h, w_ref[:, 3 * CONV_CH:]).astype(BF16)
    y = _dot(h, w_ref[:, :CONV_CH]) * z
    ms = _head_mean_square(y, gm_ref[...])
    conv_ref[...] = (y * lax.rsqrt(ms + EPS) * hg_ref[...]).astype(BF16)


def _mixer_in(x2d, g, w_in, conv_w, conv_b, hg_conv, gm, batch, seq):
    n_i = seq // MIX_ROWS
    t = x2d.shape[0]
    r8 = MIX_ROWS // 8
    last8 = t // 8 - 1
    return pl.pallas_call(
        _mixer_in_kernel,
        grid=(batch, n_i),
        in_specs=[
            pl.BlockSpec((MIX_ROWS, D_MODEL), lambda b, i: (b * n_i + i, 0)),
            pl.BlockSpec((8, D_MODEL), lambda b, i: (jnp.maximum((b * n_i + i) * r8 - 1, 0), 0)),
            pl.BlockSpec((8, D_MODEL), lambda b, i: (jnp.minimum((b * n_i + i + 1) * r8, last8), 0)),
            pl.BlockSpec((1, D_MODEL), lambda b, i: (0, 0)),
            pl.BlockSpec((D_MODEL, IN_COLS), lambda b, i: (0, 0)),
            pl.BlockSpec((3, CONV_CH), lambda b, i: (0, 0)),
            pl.BlockSpec((1, CONV_CH), lambda b, i: (0, 0)),
            pl.BlockSpec((1, CONV_CH), lambda b, i: (0, 0)),
            pl.BlockSpec((MXU_COLS, MXU_COLS), lambda b, i: (0, 0)),
        ],
        out_specs=[
            pl.BlockSpec((MIX_ROWS, CONV_CH), lambda b, i: (b * n_i + i, 0)),
            pl.BlockSpec((MIX_ROWS, FFT_CH), lambda b, i: (b * n_i + i, 0)),
        ],
        out_shape=[
            jax.ShapeDtypeStruct((t, CONV_CH), BF16),
            jax.ShapeDtypeStruct((t, FFT_CH), BF16),
        ],
        compiler_params=pltpu.CompilerParams(vmem_limit_bytes=VMEM_LIMIT),
        name="mixer_in",
    )(x2d, x2d, x2d, g, w_in, conv_w, conv_b, hg_conv, gm)


_S1_ROWS = 16
_S1_LANES = 128


def _lincomb(terms):
    acc = None
    for coef, val in terms:
        if abs(coef) < 1e-12:
            continue
        if abs(coef - 1.0) < 1e-12:
            term, neg = val, False
        elif abs(coef + 1.0) < 1e-12:
            term, neg = val, True
        else:
            term, neg = coef * val, False
        if acc is None:
            acc = -term if neg else term
        else:
            acc = acc - term if neg else acc + term
    return acc


def _fft_stage1(x_ref, a_ref):
    half = FFT_N1 // 2
    cos = [[math.cos(2 * math.pi * ((k * j) % FFT_N1) / FFT_N1) for j in range(FFT_N1)] for k in range(FFT_N1)]
    sin = [[math.sin(2 * math.pi * ((k * j) % FFT_N1) / FFT_N1) for j in range(FFT_N1)] for k in range(FFT_N1)]

    def body(r, carry):
        r0 = pl.multiple_of(r * _S1_ROWS, _S1_ROWS)
        rows_re = pl.ds(r0, _S1_ROWS)
        rows_im = pl.ds(r0 + FFT_N2, _S1_ROWS)
        for lc in range(0, FFT_CH, _S1_LANES):
            lanes = slice(lc, lc + _S1_LANES)
            xs = [x_ref[j, rows_re, lanes].astype(F32) for j in range(FFT_N1)]
            ev = [None] + [xs[j] + xs[FFT_N1 - j] for j in range(1, half)]
            od = [None] + [xs[j] - xs[FFT_N1 - j] for j in range(1, half)]
            for k in range(half + 1):
                re = _lincomb([(1.0, xs[0]), (cos[k][half], xs[half])]
                              + [(cos[k][j], ev[j]) for j in range(1, half)])
                a_ref[k, rows_re, lanes] = re.astype(BF16)
                if k in (0, half):
                    zero = jnp.zeros_like(re).astype(BF16)
                    a_ref[k, rows_im, lanes] = zero
                else:
                    im = _lincomb([(-sin[k][j], od[j]) for j in range(1, half)])
                    a_ref[k, rows_im, lanes] = im.astype(BF16)
                    a_ref[FFT_N1 - k, rows_re, lanes] = re.astype(BF16)
                    a_ref[FFT_N1 - k, rows_im, lanes] = (-im).astype(BF16)
        return carry

    lax.fori_loop(0, FFT_N2 // _S1_ROWS, body, 0)


def _fourier_kernel(x_ref, m2_ref, cs_ref, gm_ref, hg_ref, o_ref, a_ref, y_ref):
    j = pl.program_id(1)

    @pl.when(j == 0)
    def _():
        _fft_stage1(x_ref, a_ref)

    ris = [_dot(m2_ref[kk], a_ref[j * FFT_K1_PER_STEP + kk]) for kk in range(FFT_K1_PER_STEP)]
    re = jnp.concatenate([ri[:FFT_N2] for ri in ris], axis=0).astype(BF16)
    im = jnp.concatenate([ri[FFT_N2:] for ri in ris], axis=0).astype(BF16)
    y = jnp.concatenate(
        [_dot(re[:, c:c + MXU_COLS], cs_ref[:MXU_COLS, :]) + _dot(im[:, c:c + MXU_COLS], cs_ref[MXU_COLS:, :])
         for c in range(0, FFT_CH, MXU_COLS)], axis=1)
    yn = y * lax.rsqrt(_head_mean_square(y, gm_ref[...]) + EPS) * hg_ref[...]
    for kk in range(FFT_K1_PER_STEP):
        k1 = j * FFT_K1_PER_STEP + kk
        for c in range(FFT_CH // LANES):
            y_ref[c, pl.ds(k1, FFT_N2, stride=FFT_N1), :] = yn[kk * FFT_N2:(kk + 1) * FFT_N2,
                                                               c * LANES:(c + 1) * LANES]

    @pl.when(j == pl.num_programs(1) - 1)
    def _():
        for c in range(FFT_CH // LANES):
            o_ref[:, c * LANES:(c + 1) * LANES] = y_ref[c].astype(BF16)


def _fourier(uf, m2, cs, gm, hg_fft, batch, seq):
    assert seq == FFT_N1 * FFT_N2
    x4 = uf.reshape(batch, FFT_N1, FFT_N2, FFT_CH)
    out = pl.pallas_call(
        _fourier_kernel,
        grid=(batch, FFT_N1 // FFT_K1_PER_STEP),
        in_specs=[
            pl.BlockSpec((None, FFT_N1, FFT_N2, FFT_CH), lambda b, j: (b, 0, 0, 0)),
            pl.BlockSpec((FFT_K1_PER_STEP, 2 * FFT_N2, 2 * FFT_N2), lambda b, j: (j, 0, 0)),
            pl.BlockSpec((2 * MXU_COLS, MXU_COLS), lambda b, j: (0, 0)),
            pl.BlockSpec((MXU_COLS, MXU_COLS), lambda b, j: (0, 0)),
            pl.BlockSpec((1, FFT_CH), lambda b, j: (0, 0)),
        ],
        out_specs=pl.BlockSpec((seq, FFT_CH), lambda b, j: (b, 0)),
        out_shape=jax.ShapeDtypeStruct((batch * seq, FFT_CH), BF16),
        scratch_shapes=[
            pltpu.VMEM((FFT_N1, 2 * FFT_N2, FFT_CH), BF16),
            pltpu.VMEM((FFT_CH // LANES, seq, LANES), F32),
        ],
        compiler_params=pltpu.CompilerParams(
            dimension_semantics=("arbitrary", "arbitrary"), vmem_limit_bytes=VMEM_LIMIT),
        name="fourier",
    )(x4, m2, cs, gm, hg_fft)
    return out


def _first_index_of_max(vals, vmax, row):
    return jnp.min(jnp.where(vals == vmax, row, vals.shape[0]), axis=0, keepdims=True)


def _route(lg):
    cols = lg.shape[1]
    row8 = lax.broadcasted_iota(I32, (EXPERTS_PER_GROUP, cols), 0)
    gl = lg[0:8, :]
    gmax = jnp.max(gl, axis=0, keepdims=True)
    g_w = 1.0 / jnp.sum(jnp.exp(gl - gmax), axis=0, keepdims=True)
    g_idx = _first_index_of_max(gl, gmax, row8)

    el = lg[8:16, :]
    for g in range(1, N_GROUPS):
        el = jnp.where(g_idx == g, lg[8 + 8 * g:16 + 8 * g, :], el)
    emax = jnp.max(el, axis=0, keepdims=True)
    ee = jnp.exp(el - emax)
    e_prob = ee / jnp.sum(ee, axis=0, keepdims=True)
    p1 = jnp.max(e_prob, axis=0, keepdims=True)
    i1 = _first_index_of_max(e_prob, p1, row8)
    rest = jnp.where(row8 == i1, -1.0, e_prob)
    p2 = jnp.max(rest, axis=0, keepdims=True)
    i2 = _first_index_of_max(rest, p2, row8)
    denom = p1 + p2
    e1 = g_idx * EXPERTS_PER_GROUP + i1
    e2 = g_idx * EXPERTS_PER_GROUP + i2
    return e1, e2, g_w * p1 / denom, g_w * p2 / denom


def _trunk_kernel(x_ref, conv_ref, fft_ref, wout_ref, gxa_ref, wq_ref, k_ref, v_ref, wo_ref,
                  gffn_ref, wr_ref, br_ref, tri_ref, ltri_ref,
                  x2_ref, xs_ref, pos_ref, gate_ref, cnt_ref, h3_s, lg_s):
    @pl.when(pl.program_id(0) == 0)
    def _():
        h3_s[...] = jnp.zeros_like(h3_s)
        lg_s[...] = jnp.zeros_like(lg_s)

    h3 = h3_s[...]
    lg = lg_s[...]
    n_tok = lg.shape[1]

    x1 = x_ref[...] + _dot(jnp.concatenate([conv_ref[...], fft_ref[...]], axis=1), wout_ref[...])

    e1, e2, gate1, gate2 = _route(lg)
    row32 = lax.broadcasted_iota(I32, (N_EXPERTS, n_tok), 0)
    hit1 = row32 == e1
    hit2 = row32 == e2
    onehot = jnp.where(hit1 | hit2, 1.0, 0.0)
    before = _dot(onehot.astype(BF16), tri_ref[...])
    cnt = jnp.sum(onehot, axis=1, keepdims=True).astype(I32)
    piece = jnp.left_shift(jnp.right_shift(cnt + (GRANULE - 1), 3), 3)
    piece_b = jnp.broadcast_to(piece.astype(F32), (N_EXPERTS, LANES)).astype(BF16)
    start = _dot(ltri_ref[...], piece_b)[:, 0:1]
    slot = before + start
    pos1 = jnp.sum(jnp.where(hit1, slot, 0.0), axis=0, keepdims=True).astype(I32)
    pos2 = jnp.sum(jnp.where(hit2, slot, 0.0), axis=0, keepdims=True).astype(I32)
    pos_ref[0:1, :] = pos1
    pos_ref[1:2, :] = pos2
    pos_ref[2:8, :] = jnp.zeros((6, n_tok), I32)
    gate_ref[0:1, :] = gate1
    gate_ref[1:2, :] = gate2
    gate_ref[2:8, :] = jnp.zeros((6, n_tok), F32)
    cnt_ref[...] = jnp.broadcast_to(cnt, (N_EXPERTS, LANES))

    h2 = _rms(x1, gxa_ref[...]).astype(BF16)
    q = _dot(h2, wq_ref[...]).astype(BF16)

    r = lax.broadcasted_iota(I32, (LOCAL_ROWS, n_tok), 0)
    perm = jnp.where((r == pos1) | (r == pos2), 1.0, 0.0).astype(BF16)

    outs = []
    for hd in range(XA_HEADS):
        cols = slice(hd * XA_HEAD_DIM, (hd + 1) * XA_HEAD_DIM)
        s = _dot_nt(q[:, cols], k_ref[:, cols]) * (XA_HEAD_DIM ** -0.5)
        s = s - jnp.max(s, axis=-1, keepdims=True)
        p = jnp.exp(s)
        p = p / jnp.sum(p, axis=-1, keepdims=True)
        outs.append(_dot(p.astype(BF16), v_ref[:, cols]).astype(BF16))
        if hd == 1:
            sorted_left = _dot(perm, h3[:, :HALF])

    o = jnp.concatenate(outs, axis=-1)
    x2 = x1 + _dot(o, wo_ref[...])
    x2_ref[...] = x2
    sorted_right = _dot(perm, h3[:, HALF:])
    h3_next = _rms(x2, gffn_ref[...]).astype(BF16)
    xs_ref[...] = _pack_halves(sorted_left, sorted_right)
    h3_s[...] = h3_next
    lg_s[...] = _dot_nt(wr_ref[...], h3_next) + br_ref[...]


def _trunk(x2d, conv_n, fft_n, w_out, g_xa, w_q, kv, w_o, g_ffn, w_r_t, b_r, tri, ltri, seq):
    t = x2d.shape[0]
    n_tiles = t // TRUNK_ROWS
    n_per_batch = seq // TRUNK_ROWS
    const = lambda i: (0, 0)
    dense = lambda i: jnp.minimum(i, n_tiles - 1)
    routed = lambda i: jnp.maximum(i - 1, 0)
    return pl.pallas_call(
        _trunk_kernel,
        grid=(n_tiles + 1,),
        in_specs=[
            pl.BlockSpec((TRUNK_ROWS, D_MODEL), lambda i: (dense(i), 0)),
            pl.BlockSpec((TRUNK_ROWS, CONV_CH), lambda i: (dense(i), 0)),
            pl.BlockSpec((TRUNK_ROWS, FFT_CH), lambda i: (dense(i), 0)),
            pl.BlockSpec((D_MODEL, D_MODEL), const),
            pl.BlockSpec((1, D_MODEL), const),
            pl.BlockSpec((D_MODEL, D_MODEL), const),
            pl.BlockSpec((MEM_LEN, D_MODEL), lambda i: (dense(i) // n_per_batch, 0)),
            pl.BlockSpec((MEM_LEN, D_MODEL), lambda i: (dense(i) // n_per_batch, 1)),
            pl.BlockSpec((D_MODEL, D_MODEL), const),
            pl.BlockSpec((1, D_MODEL), const),
            pl.BlockSpec((ROUTER_ROWS, D_MODEL), const),
            pl.BlockSpec((ROUTER_ROWS, 1), const),
            pl.BlockSpec((TRUNK_ROWS, TRUNK_ROWS), const),
            pl.BlockSpec((N_EXPERTS, N_EXPERTS), const),
        ],
        out_specs=[
            pl.BlockSpec((TRUNK_ROWS, D_MODEL), lambda i: (dense(i), 0)),
            pl.BlockSpec((LOCAL_ROWS, HALF), lambda i: (routed(i), 0)),
            pl.BlockSpec((8, TRUNK_ROWS), lambda i: (0, routed(i))),
            pl.BlockSpec((8, TRUNK_ROWS), lambda i: (0, routed(i))),
            pl.BlockSpec((None, N_EXPERTS, LANES), lambda i: (routed(i), 0, 0)),
        ],
        out_shape=[
            jax.ShapeDtypeStruct((t, D_MODEL), F32),
            jax.ShapeDtypeStruct((n_tiles * LOCAL_ROWS, HALF), I32),
            jax.ShapeDtypeStruct((8, t), I32),
            jax.ShapeDtypeStruct((8, t), F32),
            jax.ShapeDtypeStruct((n_tiles, N_EXPERTS, LANES), I32),
        ],
        scratch_shapes=[
            pltpu.VMEM((TRUNK_ROWS, D_MODEL), BF16),
            pltpu.VMEM((ROUTER_ROWS, TRUNK_ROWS), F32),
        ],
        compiler_params=pltpu.CompilerParams(
            dimension_semantics=("arbitrary",), vmem_limit_bytes=VMEM_LIMIT),
        name="trunk",
    )(x2d, conv_n, fft_n, w_out, g_xa, w_q, kv, kv, w_o, g_ffn, w_r_t, b_r, tri, ltri)


def _experts_kernel(blk_e_ref, first_ref, next_e_ref, slot_ref, nblk_ref,
                    run_src_ref, run_off_ref, run_len_ref, nrun_ref, ngran_ref,
                    xs_hbm, wg_hbm, wu_hbm, wd_hbm, ys_hbm,
                    wg_buf, wu_buf, wd_buf, xbuf, obuf, wsems, xsems, osems):
    n = nblk_ref[0]
    gpb = EXPERT_ROWS // GRANULE

    def fetch_weights(e, s):
        return (pltpu.make_async_copy(wg_hbm.at[e], wg_buf.at[s], wsems.at[0, s]),
                pltpu.make_async_copy(wu_hbm.at[e], wu_buf.at[s], wsems.at[1, s]),
                pltpu.make_async_copy(wd_hbm.at[e], wd_buf.at[s], wsems.at[2, s]))

    def run_copies(b, r, s):
        k = b * gpb + r
        length = run_len_ref[k]
        hbm_rows = pl.ds(run_src_ref[k], length)
        blk_rows = pl.ds(run_off_ref[k], length)
        return (pltpu.make_async_copy(xs_hbm.at[hbm_rows], xbuf.at[s, blk_rows], xsems.at[s]),
                pltpu.make_async_copy(obuf.at[s, blk_rows], ys_hbm.at[hbm_rows], osems.at[s]))

    def start_in(b, s, runs):
        lax.fori_loop(0, runs, lambda r, c: (run_copies(b, r, s)[0].start(), c)[1], 0)

    def start_out(b, s):
        lax.fori_loop(0, nrun_ref[b], lambda r, c: (run_copies(b, r, s)[1].start(), c)[1], 0)

    def wait_in(b, s):
        count = ngran_ref[b]
        pltpu.make_async_copy(xs_hbm.at[pl.ds(0, count)], xbuf.at[s, pl.ds(0, count)], xsems.at[s]).wait()

    def wait_out(b, s):
        count = ngran_ref[b]
        pltpu.make_async_copy(obuf.at[s, pl.ds(0, count)], ys_hbm.at[pl.ds(0, count)], osems.at[s]).wait()

    xbuf[...] = jnp.zeros_like(xbuf)
    for cp in fetch_weights(blk_e_ref[0], 0):
        cp.start()
    last = blk_e_ref.shape[0] - 1
    start_in(0, 0, nrun_ref[0])
    start_in(1, 1, jnp.where(n > 1, nrun_ref[1], 0))

    def block(i, carry):
        xs = lax.rem(i, X_SLOTS)
        os = i % 2
        ws = slot_ref[i]

        @pl.when(first_ref[i] == 1)
        def _():
            for cp in fetch_weights(blk_e_ref[i], ws):
                cp.wait()

            @pl.when(next_e_ref[i] >= 0)
            def _():
                for cp in fetch_weights(next_e_ref[i], 1 - ws):
                    cp.start()

        @pl.when(i >= 2)
        def _():
            wait_out(i - 2, os)

        ahead = jnp.minimum(i + 2, last)
        start_in(ahead, lax.rem(i + 2, X_SLOTS), jnp.where(i + 2 < n, nrun_ref[ahead], 0))

        wait_in(i, xs)
        xl, xr = _unpack_halves(xbuf[xs].reshape(EXPERT_ROWS, HALF))
        x = jnp.concatenate([xl, xr], axis=1)
        a = _dot(x, wg_buf[ws].astype(BF16))
        b = _dot(x, wu_buf[ws].astype(BF16))
        hmid = (a * jax.nn.sigmoid(a) * b).astype(BF16)
        y = _dot(hmid, wd_buf[ws].astype(BF16))
        packed = _pack_halves(y[:, :HALF].astype(BF16).astype(F32), y[:, HALF:].astype(BF16).astype(F32))
        obuf[os] = packed.reshape(gpb, GRANULE, HALF)
        start_out(i, os)
        return carry

    lax.fori_loop(0, n, block, 0)

    @pl.when(n >= 2)
    def _():
        wait_out(n - 2, n % 2)
    wait_out(n - 1, (n - 1) % 2)


def _experts(plan, xs_loc, w_gate, w_up, w_down):
    granules = xs_loc.reshape(-1, GRANULE, HALF)
    hbm = pl.BlockSpec(memory_space=pl.ANY)
    return pl.pallas_call(
        _experts_kernel,
        grid_spec=pltpu.PrefetchScalarGridSpec(
            num_scalar_prefetch=10,
            grid=(1,),
            in_specs=[hbm, hbm, hbm, hbm],
            out_specs=hbm,
            scratch_shapes=[
                pltpu.VMEM((2, D_MODEL, D_EXPERT), F32),
                pltpu.VMEM((2, D_MODEL, D_EXPERT), F32),
                pltpu.VMEM((2, D_EXPERT, D_MODEL), F32),
                pltpu.VMEM((X_SLOTS, EXPERT_ROWS // GRANULE, GRANULE, HALF), I32),
                pltpu.VMEM((2, EXPERT_ROWS // GRANULE, GRANULE, HALF), I32),
                pltpu.SemaphoreType.DMA((3, 2)),
                pltpu.SemaphoreType.DMA((X_SLOTS,)),
                pltpu.SemaphoreType.DMA((2,)),
            ],
        ),
        out_shape=jax.ShapeDtypeStruct(granules.shape, I32),
        input_output_aliases={10: 0},
        compiler_params=pltpu.CompilerParams(
            dimension_semantics=("arbitrary",), vmem_limit_bytes=VMEM_LIMIT),
        name="experts",
    )(plan["blk_e"], plan["first"], plan["next_e"], plan["slot"], plan["nblk"],
      plan["run_src"], plan["run_off"], plan["run_len"], plan["nrun"], plan["ngran"],
      granules, w_gate, w_up, w_down).reshape(xs_loc.shape)


def _combine_kernel(x2_ref, pos_ref, gate_ref, g_ref, ys_ref, o_ref):
    n_tok = x2_ref.shape[0]
    r = lax.broadcasted_iota(I32, (LOCAL_ROWS, n_tok), 0)
    w_t = (jnp.where(r == pos_ref[0:1, :], gate_ref[0:1, :], 0.0)
           + jnp.where(r == pos_ref[1:2, :], gate_ref[1:2, :], 0.0)).astype(BF16)
    yl, yr = _unpack_halves(ys_ref[...])
    moe = jnp.concatenate([_dot_tn(w_t, yl), _dot_tn(w_t, yr)], axis=-1)
    o_ref[...] = _rms(x2_ref[...] + moe, g_ref[...])


def _combine(x2, pos_tk, gates_tk, g_final, ys_loc):
    t = x2.shape[0]
    return pl.pallas_call(
        _combine_kernel,
        grid=(t // TRUNK_ROWS,),
        in_specs=[
            pl.BlockSpec((TRUNK_ROWS, D_MODEL), lambda i: (i, 0)),
            pl.BlockSpec((8, TRUNK_ROWS), lambda i: (0, i)),
            pl.BlockSpec((8, TRUNK_ROWS), lambda i: (0, i)),
            pl.BlockSpec((1, D_MODEL), lambda i: (0, 0)),
            pl.BlockSpec((LOCAL_ROWS, HALF), lambda i: (i, 0)),
        ],
        out_specs=pl.BlockSpec((TRUNK_ROWS, D_MODEL), lambda i: (i, 0)),
        out_shape=jax.ShapeDtypeStruct((t, D_MODEL), F32),
        compiler_params=pltpu.CompilerParams(vmem_limit_bytes=VMEM_LIMIT),
        name="combine",
    )(x2, pos_tk, gates_tk, g_final, ys_loc)


def _router_params(w_rg, b_rg, w_re, b_re):
    w = jnp.zeros((ROUTER_ROWS, D_MODEL), F32)
    w = w.at[0:N_GROUPS].set(w_rg.T).at[8:8 + N_EXPERTS].set(w_re.T)
    b = jnp.zeros((ROUTER_ROWS,), F32)
    b = b.at[0:N_GROUPS].set(b_rg).at[N_GROUPS:8].set(NEG_BIG).at[8:8 + N_EXPERTS].set(b_re)
    return w.astype(BF16), b.reshape(ROUTER_ROWS, 1)


def _plan(cnt, n_global_rows):
    n_tiles = cnt.shape[0]
    piece = (cnt + GRANULE - 1) // GRANULE * GRANULE
    lend = jnp.cumsum(piece, axis=1)
    lstart = lend - piece
    tot = jnp.sum(piece, axis=0)
    padded = (tot + EXPERT_ROWS - 1) // EXPERT_ROWS * EXPERT_ROWS
    pend = jnp.cumsum(padded)
    pstart = pend - padded
    cum_tiles = jnp.cumsum(piece, axis=0)

    n_blk = n_global_rows // EXPERT_ROWS
    blk_start = jnp.arange(n_blk, dtype=I32) * EXPERT_ROWS
    blk_e = jnp.minimum(jnp.sum((pend[None, :] <= blk_start[:, None]).astype(I32), axis=1), N_EXPERTS - 1)
    nblk = pend[-1:] // EXPERT_ROWS

    of_blk_e = blk_e[:, None] == jnp.arange(N_EXPERTS, dtype=I32)[None, :]
    pick = lambda table: jnp.sum(jnp.where(of_blk_e[:, None, :], table[None, :, :], 0), axis=2)
    pick1 = lambda vec: jnp.sum(jnp.where(of_blk_e, vec[None, :], 0), axis=1)
    off = (blk_start - pick1(pstart))[:, None] + jnp.arange(EXPERT_ROWS // GRANULE, dtype=I32)[None, :] * GRANULE
    real = (off < pick1(tot)[:, None]) & (blk_start < pend[-1])[:, None]
    ngran = jnp.sum(real.astype(I32), axis=1)
    cum_b = pick(cum_tiles)
    tile_of = jnp.minimum(jnp.sum((cum_b[:, None, :] <= off[:, :, None]).astype(I32), axis=2), n_tiles - 1)
    base = (jnp.arange(n_tiles, dtype=I32) * LOCAL_ROWS)[None, :] + pick(lstart) - (cum_b - pick(piece))
    of_tile = tile_of[:, :, None] == jnp.arange(n_tiles, dtype=I32)
    row = jnp.sum(jnp.where(of_tile, base[:, None, :], 0), axis=2) + off
    gran = jnp.where(real, row, 0) // GRANULE
    g_idx = jnp.arange(EXPERT_ROWS // GRANULE, dtype=I32)
    follows = jnp.concatenate([jnp.zeros((n_blk, 1), bool), gran[:, 1:] == gran[:, :-1] + 1], axis=1)
    run_start = real & jnp.logical_not(follows)
    run_of = jnp.cumsum(run_start.astype(I32), axis=1) - 1
    in_run = (run_of[:, :, None] == g_idx[None, None, :]) & real[:, :, None]
    at_start = in_run & run_start[:, :, None]
    run_len = jnp.sum(in_run.astype(I32), axis=1)
    run_src = jnp.sum(jnp.where(at_start, gran[:, :, None], 0), axis=1)
    run_off = jnp.sum(jnp.where(at_start, g_idx[None, :, None], 0), axis=1)
    nrun = jnp.sum(run_start.astype(I32), axis=1)
    blk = jnp.arange(n_blk, dtype=I32)
    valid = blk < nblk
    change = jnp.concatenate([jnp.ones((1,), bool), blk_e[1:] != blk_e[:-1]])
    slot = (jnp.cumsum(change.astype(I32)) - 1) % 2
    later = (blk_e[None, :] > blk_e[:, None]) & valid[None, :]
    next_e = jnp.min(jnp.where(later, blk_e[None, :], N_EXPERTS), axis=1)
    next_e = jnp.where(next_e == N_EXPERTS, -1, next_e)
    as_i32 = lambda v: v.astype(I32)
    return dict(
        run_src=as_i32(run_src.reshape(-1)), run_off=as_i32(run_off.reshape(-1)),
        run_len=as_i32(run_len.reshape(-1)), nrun=as_i32(nrun),
        ngran=as_i32(ngran), blk_e=as_i32(blk_e), nblk=as_i32(nblk),
        first=as_i32(change & valid), slot=as_i32(slot), next_e=as_i32(next_e),
    )


def _layer(x2d, mem2d, batch, seq, norm_mix_g, w_in, conv_w, conv_b, head_norm_g, w_out,
           norm_xa_g, norm_mem_g, w_q, w_kv, w_o, norm_ffn_g, w_rg, b_rg, w_re, b_re,
           w_gate, w_up, w_down, out_norm_g):
    t = x2d.shape[0]
    n_tiles = t // TRUNK_ROWS
    row = lambda v: v.reshape(1, -1)
    hg = head_norm_g.reshape(-1)
    gm = _group_mean_matrix()

    kv = _kv_proj(mem2d, row(norm_mem_g), w_kv)
    conv_n, uf = _mixer_in(x2d, row(norm_mix_g), w_in.astype(BF16), conv_w, row(conv_b),
                           row(hg[:CONV_CH]), gm, batch, seq)
    fft_n = _fourier(uf.reshape(batch, seq, FFT_CH), _fft_stage2_matrices(seq), _fft_channel_matrix(seq),
                     gm, row(hg[CONV_CH:]), batch, seq)
    w_r_t, b_r = _router_params(w_rg, b_rg, w_re, b_re)
    x2, xs_loc, pos, gates, cnt = _trunk(
        x2d, conv_n, fft_n, w_out.astype(BF16), row(norm_xa_g), w_q.astype(BF16), kv, w_o.astype(BF16),
        row(norm_ffn_g), w_r_t, b_r, _strict_upper(TRUNK_ROWS), _strict_lower(N_EXPERTS), seq)

    max_rows = n_tiles * LOCAL_ROWS + N_EXPERTS * (EXPERT_ROWS - GRANULE)
    n_global_rows = -(-max_rows // EXPERT_ROWS) * EXPERT_ROWS
    plan = _plan(cnt[:, :, 0], n_global_rows)
    ys_loc = _experts(plan, xs_loc, w_gate, w_up, w_down)
    return _combine(x2, pos, gates, row(out_norm_g), ys_loc)


def kernel(x, mem, norm_mix_g, w_in, conv_w, conv_b, head_norm_g, w_out, norm_xa_g, norm_mem_g, w_q, w_kv,
           w_o, norm_ffn_g, w_route_group, b_route_group, w_route_expert, b_route_expert, w_gate, w_up,
           w_down, final_norm_g):
    batch, seq, _ = x.shape
    depth = norm_mix_g.shape[0]
    assert depth == 1, "the final norm is fused into the last layer's combine kernel"
    x2d = x.reshape(batch * seq, D_MODEL)
    mem2d = mem.reshape(batch * MEM_LEN, D_MODEL)
    l = 0
    out = _layer(x2d, mem2d, batch, seq, norm_mix_g[l], w_in[l], conv_w[l], conv_b[l], head_norm_g[l],
                 w_out[l], norm_xa_g[l], norm_mem_g[l], w_q[l], w_kv[l], w_o[l], norm_ffn_g[l],
                 w_route_group[l], b_route_group[l], w_route_expert[l], b_route_expert[l],
                 w_gate[l], w_up[l], w_down[l], final_norm_g)
    return out.reshape(batch, seq, D_MODEL)
```

```python
import functools
import math

import numpy as np
import jax
import jax.numpy as jnp
from jax import lax
from jax.experimental import pallas as pl
from jax.experimental.pallas import tpu as pltpu

F32 = jnp.float32
BF16 = jnp.bfloat16
I32 = jnp.int32

D_MODEL = 1024
HALF = D_MODEL // 2
HEAD_DIM = 64
CONV_CH = 512
FFT_CH = 512
IN_COLS = 3 * CONV_CH + FFT_CH
MEM_LEN = 256
XA_HEADS = 4
XA_HEAD_DIM = D_MODEL // XA_HEADS
N_GROUPS = 4
EXPERTS_PER_GROUP = 8
N_EXPERTS = 32
TOP_K = 2
D_EXPERT = 512
EPS = 1e-6

FFT_N1 = 16
FFT_N2 = 256
FFT_K1_PER_STEP = 4

LANES = 128
MXU_COLS = 256
GRANULE = 8
MIX_ROWS = 512
TRUNK_ROWS = 512
LOCAL_ROWS = TOP_K * TRUNK_ROWS + N_EXPERTS * GRANULE
LOCAL_GRANULES = LOCAL_ROWS // GRANULE
EXPERT_ROWS = 256
X_SLOTS = 3
ROUTER_ROWS = 128
NEG_BIG = -1e30
HI16 = -65536

VMEM_LIMIT = 56 * 1024 * 1024


def _rms(x, g):
    return x * lax.rsqrt(jnp.mean(x * x, axis=-1, keepdims=True) + EPS) * g


def _dot(a, b):
    return jnp.dot(a, b, preferred_element_type=F32)


def _dot_nt(a, b):
    return lax.dot_general(a, b, (((1,), (1,)), ((), ())), preferred_element_type=F32)


def _dot_tn(a, b):
    return lax.dot_general(a, b, (((0,), (0,)), ((), ())), preferred_element_type=F32)


def _pack_halves(left_f32, right_f32):
    lb = lax.bitcast_convert_type(left_f32, I32)
    rb = lax.shift_right_logical(lax.bitcast_convert_type(right_f32, I32), jnp.int32(16))
    return lb | rb


def _unpack_halves(packed_i32):
    left = lax.bitcast_convert_type(packed_i32 & jnp.int32(HI16), F32)
    right = lax.bitcast_convert_type(lax.shift_left(packed_i32, jnp.int32(16)), F32)
    return left.astype(BF16), right.astype(BF16)


def _group_mean_matrix():
    g = np.kron(np.eye(MXU_COLS // HEAD_DIM), np.full((HEAD_DIM, HEAD_DIM), 1.0 / HEAD_DIM))
    return jnp.asarray(g, dtype=BF16)


def _head_mean_square(y, gm):
    sq = (y * y).astype(BF16)
    return jnp.concatenate([_dot(sq[:, c:c + MXU_COLS], gm) for c in range(0, y.shape[1], MXU_COLS)], axis=1)


def _fft_stage2_matrices(seq):
    k1 = np.arange(FFT_N1)[:, None, None]
    k2 = np.arange(FFT_N2)[None, :, None]
    s2 = np.arange(FFT_N2)[None, None, :]
    ang = 2.0 * np.pi * ((s2 * (k1 + FFT_N1 * k2)) % seq) / seq
    c, s = np.cos(ang), np.sin(ang)
    top = np.concatenate([c, s], axis=2)
    bot = np.concatenate([-s, c], axis=2)
    return jnp.asarray(np.concatenate([top, bot], axis=1), dtype=BF16)


def _fft_channel_matrix(seq):
    c = np.arange(HEAD_DIM)
    ang = 2.0 * np.pi * ((c[:, None] * c[None, :]) % HEAD_DIM) / HEAD_DIM
    scale = 1.0 / math.sqrt(seq * HEAD_DIM)
    eye = np.eye(MXU_COLS // HEAD_DIM)
    cs = np.concatenate([np.kron(eye, np.cos(ang)), np.kron(eye, np.sin(ang))], axis=0) * scale
    return jnp.asarray(cs, dtype=BF16)


def _strict_upper(n):
    return jnp.asarray(np.triu(np.ones((n, n)), k=1), dtype=BF16)


def _strict_lower(n):
    return jnp.asarray(np.tril(np.ones((n, n)), k=-1), dtype=BF16)


def _kv_kernel(mem_ref, g_ref, w_ref, o_ref):
    h = _rms(mem_ref[...], g_ref[...]).astype(BF16)
    o_ref[...] = _dot(h, w_ref[...].astype(BF16)).astype(BF16)


def _kv_proj(mem2d, g, w_kv):
    rows = mem2d.shape[0]
    cols = w_kv.shape[1]
    cb = 512
    return pl.pallas_call(
        _kv_kernel,
        grid=(cols // cb,),
        in_specs=[
            pl.BlockSpec((rows, D_MODEL), lambda j: (0, 0)),
            pl.BlockSpec((1, D_MODEL), lambda j: (0, 0)),
            pl.BlockSpec((D_MODEL, cb), lambda j: (0, j)),
        ],
        out_specs=pl.BlockSpec((rows, cb), lambda j: (0, j)),
        out_shape=jax.ShapeDtypeStruct((rows, cols), BF16),
        compiler_params=pltpu.CompilerParams(vmem_limit_bytes=VMEM_LIMIT),
        name="kv_proj",
    )(mem2d, g, w_kv)


def _mixer_in_kernel(x_ref, xp_ref, xn_ref, g_ref, w_ref, cw_ref, cb_ref, hg_ref, gm_ref,
                     conv_ref, uf_ref):
    i = pl.program_id(1)
    n_i = pl.num_programs(1)
    rows = x_ref.shape[0]
    g = g_ref[...]
    h = _rms(x_ref[...], g).astype(BF16)
    u_cv = _dot(h, w_ref[:, CONV_CH:3 * CONV_CH])
    cv = u_cv[:, :CONV_CH] * u_cv[:, CONV_CH:]

    hh = jnp.concatenate([_rms(xp_ref[...], g), _rms(xn_ref[...], g)], axis=0).astype(BF16)
    uh = _dot(hh, w_ref[:, CONV_CH:3 * CONV_CH])
    cvh = uh[:, :CONV_CH] * uh[:, CONV_CH:]
    cv_prev = cvh[7:8, :] * jnp.where(i == 0, 0.0, 1.0)
    cv_next = cvh[8:9, :] * jnp.where(i == n_i - 1, 0.0, 1.0)

    row = lax.broadcasted_iota(I32, cv.shape, 0)
    cv_up = jnp.where(row == 0, cv_prev, pltpu.roll(cv, 1, 0))
    cv_dn = jnp.where(row == rows - 1, cv_next, pltpu.roll(cv, rows - 1, 0))
    z = cw_ref[0:1, :] * cv_up + cw_ref[1:2, :] * cv + cw_ref[2:3, :] * cv_dn + cb_ref[...]
    uf_ref[...] = _dot(h, w_ref[:, 3 * CONV_CH:]).astype(BF16)
    y = _dot(h, w_ref[:, :CONV_CH]) * z
    ms = _head_mean_square(y, gm_ref[...])
    conv_ref[...] = (y * lax.rsqrt(ms + EPS) * hg_ref[...]).astype(BF16)


def _mixer_in(x2d, g, w_in, conv_w, conv_b, hg_conv, gm, batch, seq):
    n_i = seq // MIX_ROWS
    t = x2d.shape[0]
    r8 = MIX_ROWS // 8
    last8 = t // 8 - 1
    return pl.pallas_call(
        _mixer_in_kernel,
        grid=(batch, n_i),
        in_specs=[
            pl.BlockSpec((MIX_ROWS, D_MODEL), lambda b, i: (b * n_i + i, 0)),
            pl.BlockSpec((8, D_MODEL), lambda b, i: (jnp.maximum((b * n_i + i) * r8 - 1, 0), 0)),
            pl.BlockSpec((8, D_MODEL), lambda b, i: (jnp.minimum((b * n_i + i + 1) * r8, last8), 0)),
            pl.BlockSpec((1, D_MODEL), lambda b, i: (0, 0)),
            pl.BlockSpec((D_MODEL, IN_COLS), lambda b, i: (0, 0)),
            pl.BlockSpec((3, CONV_CH), lambda b, i: (0, 0)),
            pl.BlockSpec((1, CONV_CH), lambda b, i: (0, 0)),
            pl.BlockSpec((1, CONV_CH), lambda b, i: (0, 0)),
            pl.BlockSpec((MXU_COLS, MXU_COLS), lambda b, i: (0, 0)),
        ],
        out_specs=[
            pl.BlockSpec((MIX_ROWS, CONV_CH), lambda b, i: (b * n_i + i, 0)),
            pl.BlockSpec((MIX_ROWS, FFT_CH), lambda b, i: (b * n_i + i, 0)),
        ],
        out_shape=[
            jax.ShapeDtypeStruct((t, CONV_CH), BF16),
            jax.ShapeDtypeStruct((t, FFT_CH), BF16),
        ],
        compiler_params=pltpu.CompilerParams(vmem_limit_bytes=VMEM_LIMIT),
        name="mixer_in",
    )(x2d, x2d, x2d, g, w_in, conv_w, conv_b, hg_conv, gm)


_S1_ROWS = 16
_S1_LANES = 128


def _lincomb(terms):
    acc = None
    for coef, val in terms:
        if abs(coef) < 1e-12:
            continue
        if abs(coef - 1.0) < 1e-12:
            term, neg = val, False
        elif abs(coef + 1.0) < 1e-12:
            term, neg = val, True
        else:
            term, neg = coef * val, False
        if acc is None:
            acc = -term if neg else term
        else:
            acc = acc - term if neg else acc + term
    return acc


def _fft_stage1(x_ref, a_ref):
    half = FFT_N1 // 2
    cos = [[math.cos(2 * math.pi * ((k * j) % FFT_N1) / FFT_N1) for j in range(FFT_N1)] for k in range(FFT_N1)]
    sin = [[math.sin(2 * math.pi * ((k * j) % FFT_N1) / FFT_N1) for j in range(FFT_N1)] for k in range(FFT_N1)]

    def body(r, carry):
        r0 = pl.multiple_of(r * _S1_ROWS, _S1_ROWS)
        rows_re = pl.ds(r0, _S1_ROWS)
        rows_im = pl.ds(r0 + FFT_N2, _S1_ROWS)
        for lc in range(0, FFT_CH, _S1_LANES):
            lanes = slice(lc, lc + _S1_LANES)
            xs = [x_ref[j, rows_re, lanes].astype(F32) for j in range(FFT_N1)]
            ev = [None] + [xs[j] + xs[FFT_N1 - j] for j in range(1, half)]
            od = [None] + [xs[j] - xs[FFT_N1 - j] for j in range(1, half)]
            for k in range(half + 1):
                re = _lincomb([(1.0, xs[0]), (cos[k][half], xs[half])]
                              + [(cos[k][j], ev[j]) for j in range(1, half)])
                a_ref[k, rows_re, lanes] = re.astype(BF16)
                if k in (0, half):
                    zero = jnp.zeros_like(re).astype(BF16)
                    a_ref[k, rows_im, lanes] = zero
                else:
                    im = _lincomb([(-sin[k][j], od[j]) for j in range(1, half)])
                    a_ref[k, rows_im, lanes] = im.astype(BF16)
                    a_ref[FFT_N1 - k, rows_re, lanes] = re.astype(BF16)
                    a_ref[FFT_N1 - k, rows_im, lanes] = (-im).astype(BF16)
        return carry

    lax.fori_loop(0, FFT_N2 // _S1_ROWS, body, 0)


def _fourier_kernel(x_ref, m2_ref, cs_ref, gm_ref, hg_ref, o_ref, a_ref, y_ref):
    j = pl.program_id(1)

    @pl.when(j == 0)
    def _():
        _fft_stage1(x_ref, a_ref)

    ris = [_dot(m2_ref[kk], a_ref[j * FFT_K1_PER_STEP + kk]) for kk in range(FFT_K1_PER_STEP)]
    re = jnp.concatenate([ri[:FFT_N2] for ri in ris], axis=0).astype(BF16)
    im = jnp.concatenate([ri[FFT_N2:] for ri in ris], axis=0).astype(BF16)
    y = jnp.concatenate(
        [_dot(re[:, c:c + MXU_COLS], cs_ref[:MXU_COLS, :]) + _dot(im[:, c:c + MXU_COLS], cs_ref[MXU_COLS:, :])
         for c in range(0, FFT_CH, MXU_COLS)], axis=1)
    yn = y * lax.rsqrt(_head_mean_square(y, gm_ref[...]) + EPS) * hg_ref[...]
    for kk in range(FFT_K1_PER_STEP):
        k1 = j * FFT_K1_PER_STEP + kk
        for c in range(FFT_CH // LANES):
            y_ref[c, pl.ds(k1, FFT_N2, stride=FFT_N1), :] = yn[kk * FFT_N2:(kk + 1) * FFT_N2,
                                                               c * LANES:(c + 1) * LANES]

    @pl.when(j == pl.num_programs(1) - 1)
    def _():
        for c in range(FFT_CH // LANES):
            o_ref[:, c * LANES:(c + 1) * LANES] = y_ref[c].astype(BF16)


def _fourier(uf, m2, cs, gm, hg_fft, batch, seq):
    assert seq == FFT_N1 * FFT_N2
    x4 = uf.reshape(batch, FFT_N1, FFT_N2, FFT_CH)
    out = pl.pallas_call(
        _fourier_kernel,
        grid=(batch, FFT_N1 // FFT_K1_PER_STEP),
        in_specs=[
            pl.BlockSpec((None, FFT_N1, FFT_N2, FFT_CH), lambda b, j: (b, 0, 0, 0)),
            pl.BlockSpec((FFT_K1_PER_STEP, 2 * FFT_N2, 2 * FFT_N2), lambda b, j: (j, 0, 0)),
            pl.BlockSpec((2 * MXU_COLS, MXU_COLS), lambda b, j: (0, 0)),
            pl.BlockSpec((MXU_COLS, MXU_COLS), lambda b, j: (0, 0)),
            pl.BlockSpec((1, FFT_CH), lambda b, j: (0, 0)),
        ],
        out_specs=pl.BlockSpec((seq, FFT_CH), lambda b, j: (b, 0)),
        out_shape=jax.ShapeDtypeStruct((batch * seq, FFT_CH), BF16),
        scratch_shapes=[
            pltpu.VMEM((FFT_N1, 2 * FFT_N2, FFT_CH), BF16),
            pltpu.VMEM((FFT_CH // LANES, seq, LANES), F32),
        ],
        compiler_params=pltpu.CompilerParams(
            dimension_semantics=("arbitrary", "arbitrary"), vmem_limit_bytes=VMEM_LIMIT),
        name="fourier",
    )(x4, m2, cs, gm, hg_fft)
    return out


def _first_index_of_max(vals, vmax, row):
    return jnp.min(jnp.where(vals == vmax, row, vals.shape[0]), axis=0, keepdims=True)


def _route(lg):
    cols = lg.shape[1]
    row8 = lax.broadcasted_iota(I32, (EXPERTS_PER_GROUP, cols), 0)
    gl = lg[0:8, :]
    gmax = jnp.max(gl, axis=0, keepdims=True)
    g_w = 1.0 / jnp.sum(jnp.exp(gl - gmax), axis=0, keepdims=True)
    g_idx = _first_index_of_max(gl, gmax, row8)

    el = lg[8:16, :]
    for g in range(1, N_GROUPS):
        el = jnp.where(g_idx == g, lg[8 + 8 * g:16 + 8 * g, :], el)
    emax = jnp.max(el, axis=0, keepdims=True)
    ee = jnp.exp(el - emax)
    e_prob = ee / jnp.sum(ee, axis=0, keepdims=True)
    p1 = jnp.max(e_prob, axis=0, keepdims=True)
    i1 = _first_index_of_max(e_prob, p1, row8)
    rest = jnp.where(row8 == i1, -1.0, e_prob)
    p2 = jnp.max(rest, axis=0, keepdims=True)
    i2 = _first_index_of_max(rest, p2, row8)
    denom = p1 + p2
    e1 = g_idx * EXPERTS_PER_GROUP + i1
    e2 = g_idx * EXPERTS_PER_GROUP + i2
    return e1, e2, g_w * p1 / denom, g_w * p2 / denom


def _trunk_kernel(x_ref, conv_ref, fft_ref, wout_ref, gxa_ref, wq_ref, k_ref, v_ref, wo_ref,
                  gffn_ref, wr_ref, br_ref, tri_ref, ltri_ref,
                  x2_ref, xs_ref, pos_ref, gate_ref, cnt_ref, h3_s, lg_s):
    @pl.when(pl.program_id(0) == 0)
    def _():
        h3_s[...] = jnp.zeros_like(h3_s)
        lg_s[...] = jnp.zeros_like(lg_s)

    h3 = h3_s[...]
    lg = lg_s[...]
    n_tok = lg.shape[1]

    x1 = x_ref[...] + _dot(jnp.concatenate([conv_ref[...], fft_ref[...]], axis=1), wout_ref[...])

    e1, e2, gate1, gate2 = _route(lg)
    row32 = lax.broadcasted_iota(I32, (N_EXPERTS, n_tok), 0)
    hit1 = row32 == e1
    hit2 = row32 == e2
    onehot = jnp.where(hit1 | hit2, 1.0, 0.0)
    before = _dot(onehot.astype(BF16), tri_ref[...])
    cnt = jnp.sum(onehot, axis=1, keepdims=True).astype(I32)
    piece = jnp.left_shift(jnp.right_shift(cnt + (GRANULE - 1), 3), 3)
    piece_b = jnp.broadcast_to(piece.astype(F32), (N_EXPERTS, LANES)).astype(BF16)
    start = _dot(ltri_ref[...], piece_b)[:, 0:1]
    slot = before + start
    pos1 = jnp.sum(jnp.where(hit1, slot, 0.0), axis=0, keepdims=True).astype(I32)
    pos2 = jnp.sum(jnp.where(hit2, slot, 0.0), axis=0, keepdims=True).astype(I32)
    pos_ref[0:1, :] = pos1
    pos_ref[1:2, :] = pos2
    pos_ref[2:8, :] = jnp.zeros((6, n_tok), I32)
    gate_ref[0:1, :] = gate1
    gate_ref[1:2, :] = gate2
    gate_ref[2:8, :] = jnp.zeros((6, n_tok), F32)
    cnt_ref[...] = jnp.broadcast_to(cnt, (N_EXPERTS, LANES))

    h2 = _rms(x1, gxa_ref[...]).astype(BF16)
    q = _dot(h2, wq_ref[...]).astype(BF16)

    r = lax.broadcasted_iota(I32, (LOCAL_ROWS, n_tok), 0)
    perm = jnp.where((r == pos1) | (r == pos2), 1.0, 0.0).astype(BF16)

    outs = []
    for hd in range(XA_HEADS):
        cols = slice(hd * XA_HEAD_DIM, (hd + 1) * XA_HEAD_DIM)
        s = _dot_nt(q[:, cols], k_ref[:, cols]) * (XA_HEAD_DIM ** -0.5)
        s = s - jnp.max(s, axis=-1, keepdims=True)
        p = jnp.exp(s)
        p = p / jnp.sum(p, axis=-1, keepdims=True)
        outs.append(_dot(p.astype(BF16), v_ref[:, cols]).astype(BF16))
        if hd == 1:
            sorted_left = _dot(perm, h3[:, :HALF])

    o = jnp.concatenate(outs, axis=-1)
    x2 = x1 + _dot(o, wo_ref[...])
    x2_ref[...] = x2
    sorted_right = _dot(perm, h3[:, HALF:])
    h3_next = _rms(x2, gffn_ref[...]).astype(BF16)
    xs_ref[...] = _pack_halves(sorted_left, sorted_right)
    h3_s[...] = h3_next
    lg_s[...] = _dot_nt(wr_ref[...], h3_next) + br_ref[...]


def _trunk(x2d, conv_n, fft_n, w_out, g_xa, w_q, kv, w_o, g_ffn, w_r_t, b_r, tri, ltri, seq):
    t = x2d.shape[0]
    n_tiles = t // TRUNK_ROWS
    n_per_batch = seq // TRUNK_ROWS
    const = lambda i: (0, 0)
    dense = lambda i: jnp.minimum(i, n_tiles - 1)
    routed = lambda i: jnp.maximum(i - 1, 0)
    return pl.pallas_call(
        _trunk_kernel,
        grid=(n_tiles + 1,),
        in_specs=[
            pl.BlockSpec((TRUNK_ROWS, D_MODEL), lambda i: (dense(i), 0)),
            pl.BlockSpec((TRUNK_ROWS, CONV_CH), lambda i: (dense(i), 0)),
            pl.BlockSpec((TRUNK_ROWS, FFT_CH), lambda i: (dense(i), 0)),
            pl.BlockSpec((D_MODEL, D_MODEL), const),
            pl.BlockSpec((1, D_MODEL), const),
            pl.BlockSpec((D_MODEL, D_MODEL), const),
            pl.BlockSpec((MEM_LEN, D_MODEL), lambda i: (dense(i) // n_per_batch, 0)),
            pl.BlockSpec((MEM_LEN, D_MODEL), lambda i: (dense(i) // n_per_batch, 1)),
            pl.BlockSpec((D_MODEL, D_MODEL), const),
            pl.BlockSpec((1, D_MODEL), const),
            pl.BlockSpec((ROUTER_ROWS, D_MODEL), const),
            pl.BlockSpec((ROUTER_ROWS, 1), const),
            pl.BlockSpec((TRUNK_ROWS, TRUNK_ROWS), const),
            pl.BlockSpec((N_EXPERTS, N_EXPERTS), const),
        ],
        out_specs=[
            pl.BlockSpec((TRUNK_ROWS, D_MODEL), lambda i: (dense(i), 0)),
            pl.BlockSpec((LOCAL_ROWS, HALF), lambda i: (routed(i), 0)),
            pl.BlockSpec((8, TRUNK_ROWS), lambda i: (0, routed(i))),
            pl.BlockSpec((8, TRUNK_ROWS), lambda i: (0, routed(i))),
            pl.BlockSpec((None, N_EXPERTS, LANES), lambda i: (routed(i), 0, 0)),
        ],
        out_shape=[
            jax.ShapeDtypeStruct((t, D_MODEL), F32),
            jax.ShapeDtypeStruct((n_tiles * LOCAL_ROWS, HALF), I32),
            jax.ShapeDtypeStruct((8, t), I32),
            jax.ShapeDtypeStruct((8, t), F32),
            jax.ShapeDtypeStruct((n_tiles, N_EXPERTS, LANES), I32),
        ],
        scratch_shapes=[
            pltpu.VMEM((TRUNK_ROWS, D_MODEL), BF16),
            pltpu.VMEM((ROUTER_ROWS, TRUNK_ROWS), F32),
        ],
        compiler_params=pltpu.CompilerParams(
            dimension_semantics=("arbitrary",), vmem_limit_bytes=VMEM_LIMIT),
        name="trunk",
    )(x2d, conv_n, fft_n, w_out, g_xa, w_q, kv, kv, w_o, g_ffn, w_r_t, b_r, tri, ltri)


_PLAN_PER_BLOCK = ("blk_e", "first", "next_e", "slot", "nrun", "ngran")
_PLAN_PER_RUN = ("run_src", "run_off", "run_len")


def _plan_layout(n_blk):
    gpb = EXPERT_ROWS // GRANULE
    sizes = [("nblk", 1)] + [(k, n_blk) for k in _PLAN_PER_BLOCK] + [(k, n_blk * gpb) for k in _PLAN_PER_RUN]
    offsets, at = {}, 0
    for name, size in sizes:
        offsets[name] = at
        at += size
    return offsets


class _TableView:
    def __init__(self, ref, offset):
        self._ref, self._offset = ref, offset

    def __getitem__(self, i):
        return self._ref[self._offset + i]


def _experts_kernel(plan_ref, xs_hbm, wg_hbm, wu_hbm, wd_hbm, ys_hbm,
                    wg_buf, wu_buf, wd_buf, xbuf, obuf, wsems, xsems, osems, *, n_blk):
    layout = _plan_layout(n_blk)
    (blk_e_ref, first_ref, next_e_ref, slot_ref, nblk_ref, run_src_ref, run_off_ref, run_len_ref, nrun_ref,
     ngran_ref) = (_TableView(plan_ref, layout[k]) for k in (
         "blk_e", "first", "next_e", "slot", "nblk", "run_src", "run_off", "run_len", "nrun", "ngran"))
    _experts_body(blk_e_ref, first_ref, next_e_ref, slot_ref, nblk_ref,
                  run_src_ref, run_off_ref, run_len_ref, nrun_ref, ngran_ref, n_blk - 1,
                  xs_hbm, wg_hbm, wu_hbm, wd_hbm, ys_hbm,
                  wg_buf, wu_buf, wd_buf, xbuf, obuf, wsems, xsems, osems)


def _experts_body(blk_e_ref, first_ref, next_e_ref, slot_ref, nblk_ref,
                  run_src_ref, run_off_ref, run_len_ref, nrun_ref, ngran_ref, last,
                  xs_hbm, wg_hbm, wu_hbm, wd_hbm, ys_hbm,
                  wg_buf, wu_buf, wd_buf, xbuf, obuf, wsems, xsems, osems):
    n = nblk_ref[0]
    gpb = EXPERT_ROWS // GRANULE

    def fetch_weights(e, s):
        return (pltpu.make_async_copy(wg_hbm.at[e], wg_buf.at[s], wsems.at[0, s]),
                pltpu.make_async_copy(wu_hbm.at[e], wu_buf.at[s], wsems.at[1, s]),
                pltpu.make_async_copy(wd_hbm.at[e], wd_buf.at[s], wsems.at[2, s]))

    def run_copies(b, r, s):
        k = b * gpb + r
        length = run_len_ref[k]
        hbm_rows = pl.ds(run_src_ref[k], length)
        blk_rows = pl.ds(run_off_ref[k], length)
        return (pltpu.make_async_copy(xs_hbm.at[hbm_rows], xbuf.at[s, blk_rows], xsems.at[s]),
                pltpu.make_async_copy(obuf.at[s, blk_rows], ys_hbm.at[hbm_rows], osems.at[s]))

    def start_in(b, s, runs):
        lax.fori_loop(0, runs, lambda r, c: (run_copies(b, r, s)[0].start(), c)[1], 0)

    def start_out(b, s):
        lax.fori_loop(0, nrun_ref[b], lambda r, c: (run_copies(b, r, s)[1].start(), c)[1], 0)

    def wait_in(b, s):
        count = ngran_ref[b]
        pltpu.make_async_copy(xs_hbm.at[pl.ds(0, count)], xbuf.at[s, pl.ds(0, count)], xsems.at[s]).wait()

    def wait_out(b, s):
        count = ngran_ref[b]
        pltpu.make_async_copy(obuf.at[s, pl.ds(0, count)], ys_hbm.at[pl.ds(0, count)], osems.at[s]).wait()

    xbuf[...] = jnp.zeros_like(xbuf)
    for cp in fetch_weights(blk_e_ref[0], 0):
        cp.start()
    start_in(0, 0, nrun_ref[0])
    start_in(1, 1, jnp.where(n > 1, nrun_ref[1], 0))

    def block(i, carry):
        xs = lax.rem(i, X_SLOTS)
        os = i % 2
        ws = slot_ref[i]

        @pl.when(first_ref[i] == 1)
        def _():
            for cp in fetch_weights(blk_e_ref[i], ws):
                cp.wait()

            @pl.when(next_e_ref[i] >= 0)
            def _():
                for cp in fetch_weights(next_e_ref[i], 1 - ws):
                    cp.start()

        @pl.when(i >= 2)
        def _():
            wait_out(i - 2, os)

        ahead = jnp.minimum(i + 2, last)
        start_in(ahead, lax.rem(i + 2, X_SLOTS), jnp.where(i + 2 < n, nrun_ref[ahead], 0))

        wait_in(i, xs)
        xl, xr = _unpack_halves(xbuf[xs].reshape(EXPERT_ROWS, HALF))
        x = jnp.concatenate([xl, xr], axis=1)
        a = _dot(x, wg_buf[ws].astype(BF16))
        b = _dot(x, wu_buf[ws].astype(BF16))
        hmid = (a * jax.nn.sigmoid(a) * b).astype(BF16)
        y = _dot(hmid, wd_buf[ws].astype(BF16))
        packed = _pack_halves(y[:, :HALF].astype(BF16).astype(F32), y[:, HALF:].astype(BF16).astype(F32))
        obuf[os] = packed.reshape(gpb, GRANULE, HALF)
        start_out(i, os)
        return carry

    lax.fori_loop(0, n, block, 0)

    @pl.when(n >= 2)
    def _():
        wait_out(n - 2, n % 2)
    wait_out(n - 1, (n - 1) % 2)


def _experts(plan_table, n_blk, xs_loc, w_gate, w_up, w_down):
    granules = xs_loc.reshape(-1, GRANULE, HALF)
    hbm = pl.BlockSpec(memory_space=pl.ANY)
    return pl.pallas_call(
        functools.partial(_experts_kernel, n_blk=n_blk),
        grid_spec=pltpu.PrefetchScalarGridSpec(
            num_scalar_prefetch=1,
            grid=(1,),
            in_specs=[hbm, hbm, hbm, hbm],
            out_specs=hbm,
            scratch_shapes=[
                pltpu.VMEM((2, D_MODEL, D_EXPERT), F32),
                pltpu.VMEM((2, D_MODEL, D_EXPERT), F32),
                pltpu.VMEM((2, D_EXPERT, D_MODEL), F32),
                pltpu.VMEM((X_SLOTS, EXPERT_ROWS // GRANULE, GRANULE, HALF), I32),
                pltpu.VMEM((2, EXPERT_ROWS // GRANULE, GRANULE, HALF), I32),
                pltpu.SemaphoreType.DMA((3, 2)),
                pltpu.SemaphoreType.DMA((X_SLOTS,)),
                pltpu.SemaphoreType.DMA((2,)),
            ],
        ),
        out_shape=jax.ShapeDtypeStruct(granules.shape, I32),
        input_output_aliases={1: 0},
        compiler_params=pltpu.CompilerParams(
            dimension_semantics=("arbitrary",), vmem_limit_bytes=VMEM_LIMIT),
        name="experts",
    )(plan_table, granules, w_gate, w_up, w_down).reshape(xs_loc.shape)


def _combine_kernel(x2_ref, pos_ref, gate_ref, g_ref, ys_ref, o_ref):
    n_tok = x2_ref.shape[0]
    r = lax.broadcasted_iota(I32, (LOCAL_ROWS, n_tok), 0)
    w_t = (jnp.where(r == pos_ref[0:1, :], gate_ref[0:1, :], 0.0)
           + jnp.where(r == pos_ref[1:2, :], gate_ref[1:2, :], 0.0)).astype(BF16)
    yl, yr = _unpack_halves(ys_ref[...])
    moe = jnp.concatenate([_dot_tn(w_t, yl), _dot_tn(w_t, yr)], axis=-1)
    o_ref[...] = _rms(x2_ref[...] + moe, g_ref[...])


def _combine(x2, pos_tk, gates_tk, g_final, ys_loc):
    t = x2.shape[0]
    return pl.pallas_call(
        _combine_kernel,
        grid=(t // TRUNK_ROWS,),
        in_specs=[
            pl.BlockSpec((TRUNK_ROWS, D_MODEL), lambda i: (i, 0)),
            pl.BlockSpec((8, TRUNK_ROWS), lambda i: (0, i)),
            pl.BlockSpec((8, TRUNK_ROWS), lambda i: (0, i)),
            pl.BlockSpec((1, D_MODEL), lambda i: (0, 0)),
            pl.BlockSpec((LOCAL_ROWS, HALF), lambda i: (i, 0)),
        ],
        out_specs=pl.BlockSpec((TRUNK_ROWS, D_MODEL), lambda i: (i, 0)),
        out_shape=jax.ShapeDtypeStruct((t, D_MODEL), F32),
        compiler_params=pltpu.CompilerParams(vmem_limit_bytes=VMEM_LIMIT),
        name="combine",
    )(x2, pos_tk, gates_tk, g_final, ys_loc)


def _router_params(w_rg, b_rg, w_re, b_re):
    w = jnp.zeros((ROUTER_ROWS, D_MODEL), F32)
    w = w.at[0:N_GROUPS].set(w_rg.T).at[8:8 + N_EXPERTS].set(w_re.T)
    b = jnp.zeros((ROUTER_ROWS,), F32)
    b = b.at[0:N_GROUPS].set(b_rg).at[N_GROUPS:8].set(NEG_BIG).at[8:8 + N_EXPERTS].set(b_re)
    return w.astype(BF16), b.reshape(ROUTER_ROWS, 1)


def _plan(cnt, n_global_rows):
    n_tiles = cnt.shape[0]
    piece = (cnt + GRANULE - 1) // GRANULE * GRANULE
    lend = jnp.cumsum(piece, axis=1)
    lstart = lend - piece
    tot = jnp.sum(piece, axis=0)
    padded = (tot + EXPERT_ROWS - 1) // EXPERT_ROWS * EXPERT_ROWS
    pend = jnp.cumsum(padded)
    pstart = pend - padded
    cum_tiles = jnp.cumsum(piece, axis=0)

    n_blk = n_global_rows // EXPERT_ROWS
    blk_start = jnp.arange(n_blk, dtype=I32) * EXPERT_ROWS
    blk_e = jnp.minimum(jnp.sum((pend[None, :] <= blk_start[:, None]).astype(I32), axis=1), N_EXPERTS - 1)
    nblk = pend[-1:] // EXPERT_ROWS

    of_blk_e = blk_e[:, None] == jnp.arange(N_EXPERTS, dtype=I32)[None, :]
    pick = lambda table: jnp.sum(jnp.where(of_blk_e[:, None, :], table[None, :, :], 0), axis=2)
    pick1 = lambda vec: jnp.sum(jnp.where(of_blk_e, vec[None, :], 0), axis=1)
    off = (blk_start - pick1(pstart))[:, None] + jnp.arange(EXPERT_ROWS // GRANULE, dtype=I32)[None, :] * GRANULE
    real = (off < pick1(tot)[:, None]) & (blk_start < pend[-1])[:, None]
    ngran = jnp.sum(real.astype(I32), axis=1)
    cum_b = pick(cum_tiles)
    tile_of = jnp.minimum(jnp.sum((cum_b[:, None, :] <= off[:, :, None]).astype(I32), axis=2), n_tiles - 1)
    base = (jnp.arange(n_tiles, dtype=I32) * LOCAL_ROWS)[None, :] + pick(lstart) - (cum_b - pick(piece))
    of_tile = tile_of[:, :, None] == jnp.arange(n_tiles, dtype=I32)
    row = jnp.sum(jnp.where(of_tile, base[:, None, :], 0), axis=2) + off
    gran = jnp.where(real, row, 0) // GRANULE
    g_idx = jnp.arange(EXPERT_ROWS // GRANULE, dtype=I32)
    follows = jnp.concatenate([jnp.zeros((n_blk, 1), bool), gran[:, 1:] == gran[:, :-1] + 1], axis=1)
    run_start = real & jnp.logical_not(follows)
    run_of = jnp.cumsum(run_start.astype(I32), axis=1) - 1
    in_run = (run_of[:, :, None] == g_idx[None, None, :]) & real[:, :, None]
    at_start = in_run & run_start[:, :, None]
    run_len = jnp.sum(in_run.astype(I32), axis=1)
    run_src = jnp.sum(jnp.where(at_start, gran[:, :, None], 0), axis=1)
    run_off = jnp.sum(jnp.where(at_start, g_idx[None, :, None], 0), axis=1)
    nrun = jnp.sum(run_start.astype(I32), axis=1)
    blk = jnp.arange(n_blk, dtype=I32)
    valid = blk < nblk
    change = jnp.concatenate([jnp.ones((1,), bool), blk_e[1:] != blk_e[:-1]])
    slot = (jnp.cumsum(change.astype(I32)) - 1) % 2
    later = (blk_e[None, :] > blk_e[:, None]) & valid[None, :]
    next_e = jnp.min(jnp.where(later, blk_e[None, :], N_EXPERTS), axis=1)
    next_e = jnp.where(next_e == N_EXPERTS, -1, next_e)
    tables = dict(
        nblk=nblk, blk_e=blk_e, first=change & valid, next_e=next_e, slot=slot, nrun=nrun, ngran=ngran,
        run_src=run_src.reshape(-1), run_off=run_off.reshape(-1), run_len=run_len.reshape(-1),
    )
    order = ("nblk",) + _PLAN_PER_BLOCK + _PLAN_PER_RUN
    return jnp.concatenate([tables[k].astype(I32) for k in order]), n_blk


def _layer(x2d, mem2d, batch, seq, norm_mix_g, w_in, conv_w, conv_b, head_norm_g, w_out,
           norm_xa_g, norm_mem_g, w_q, w_kv, w_o, norm_ffn_g, w_rg, b_rg, w_re, b_re,
           w_gate, w_up, w_down, out_norm_g):
    t = x2d.shape[0]
    n_tiles = t // TRUNK_ROWS
    row = lambda v: v.reshape(1, -1)
    hg = head_norm_g.reshape(-1)
    gm = _group_mean_matrix()

    kv = _kv_proj(mem2d, row(norm_mem_g), w_kv)
    conv_n, uf = _mixer_in(x2d, row(norm_mix_g), w_in.astype(BF16), conv_w, row(conv_b),
                           row(hg[:CONV_CH]), gm, batch, seq)
    fft_n = _fourier(uf.reshape(batch, seq, FFT_CH), _fft_stage2_matrices(seq), _fft_channel_matrix(seq),
                     gm, row(hg[CONV_CH:]), batch, seq)
    w_r_t, b_r = _router_params(w_rg, b_rg, w_re, b_re)
    x2, xs_loc, pos, gates, cnt = _trunk(
        x2d, conv_n, fft_n, w_out.astype(BF16), row(norm_xa_g), w_q.astype(BF16), kv, w_o.astype(BF16),
        row(norm_ffn_g), w_r_t, b_r, _strict_upper(TRUNK_ROWS), _strict_lower(N_EXPERTS), seq)

    max_rows = n_tiles * LOCAL_ROWS + N_EXPERTS * (EXPERT_ROWS - GRANULE)
    n_global_rows = -(-max_rows // EXPERT_ROWS) * EXPERT_ROWS
    plan_table, n_blk = _plan(cnt[:, :, 0], n_global_rows)
    ys_loc = _experts(plan_table, n_blk, xs_loc, w_gate, w_up, w_down)
    return _combine(x2, pos, gates, row(out_norm_g), ys_loc)


def kernel(x, mem, norm_mix_g, w_in, conv_w, conv_b, head_norm_g, w_out, norm_xa_g, norm_mem_g, w_q, w_kv,
           w_o, norm_ffn_g, w_route_group, b_route_group, w_route_expert, b_route_expert, w_gate, w_up,
           w_down, final_norm_g):
    batch, seq, _ = x.shape
    depth = norm_mix_g.shape[0]
    assert depth == 1, "the final norm is fused into the last layer's combine kernel"
    x2d = x.reshape(batch * seq, D_MODEL)
    mem2d = mem.reshape(batch * MEM_LEN, D_MODEL)
    l = 0
    out = _layer(x2d, mem2d, batch, seq, norm_mix_g[l], w_in[l], conv_w[l], conv_b[l], head_norm_g[l],
                 w_out[l], norm_xa_g[l], norm_mem_g[l], w_q[l], w_kv[l], w_o[l], norm_ffn_g[l],
                 w_route_group[l], b_route_group[l], w_route_expert[l], b_route_expert[l],
                 w_gate[l], w_up[l], w_down[l], final_norm_g)
    return out.reshape(batch, seq, D_MODEL)
```

```python
import functools
import math

import numpy as np
import jax
import jax.numpy as jnp
from jax import lax
from jax.experimental import pallas as pl
from jax.experimental.pallas import tpu as pltpu

F32 = jnp.float32
BF16 = jnp.bfloat16
I32 = jnp.int32

D_MODEL = 1024
HALF = D_MODEL // 2
HEAD_DIM = 64
CONV_CH = 512
FFT_CH = 512
IN_COLS = 3 * CONV_CH + FFT_CH
MEM_LEN = 256
XA_HEADS = 4
XA_HEAD_DIM = D_MODEL // XA_HEADS
N_GROUPS = 4
EXPERTS_PER_GROUP = 8
N_EXPERTS = 32
TOP_K = 2
D_EXPERT = 512
EPS = 1e-6

FFT_N1 = 16
FFT_N2 = 256
FFT_K1_PER_STEP = 4

LANES = 128
MXU_COLS = 256
GRANULE = 8
MIX_ROWS = 512
TRUNK_ROWS = 512
LOCAL_ROWS = TOP_K * TRUNK_ROWS + N_EXPERTS * GRANULE
LOCAL_GRANULES = LOCAL_ROWS // GRANULE
EXPERT_PAD = 256
EXPERT_ROWS = 512
X_SLOTS = 3
ROUTER_ROWS = 128
NEG_BIG = -1e30
HI16 = -65536

VMEM_LIMIT = 56 * 1024 * 1024


def _rms(x, g):
    return x * lax.rsqrt(jnp.mean(x * x, axis=-1, keepdims=True) + EPS) * g


def _dot(a, b):
    return jnp.dot(a, b, preferred_element_type=F32)


def _dot_nt(a, b):
    return lax.dot_general(a, b, (((1,), (1,)), ((), ())), preferred_element_type=F32)


def _dot_tn(a, b):
    return lax.dot_general(a, b, (((0,), (0,)), ((), ())), preferred_element_type=F32)


def _pack_halves(left_f32, right_f32):
    lb = lax.bitcast_convert_type(left_f32, I32)
    rb = lax.shift_right_logical(lax.bitcast_convert_type(right_f32, I32), jnp.int32(16))
    return lb | rb


def _unpack_halves(packed_i32):
    left = lax.bitcast_convert_type(packed_i32 & jnp.int32(HI16), F32)
    right = lax.bitcast_convert_type(lax.shift_left(packed_i32, jnp.int32(16)), F32)
    return left.astype(BF16), right.astype(BF16)


def _group_mean_matrix():
    g = np.kron(np.eye(MXU_COLS // HEAD_DIM), np.full((HEAD_DIM, HEAD_DIM), 1.0 / HEAD_DIM))
    return jnp.asarray(g, dtype=BF16)


def _head_mean_square(y, gm):
    sq = (y * y).astype(BF16)
    return jnp.concatenate([_dot(sq[:, c:c + MXU_COLS], gm) for c in range(0, y.shape[1], MXU_COLS)], axis=1)


def _fft_stage2_matrices(seq):
    k1 = np.arange(FFT_N1)[:, None, None]
    k2 = np.arange(FFT_N2)[None, :, None]
    s2 = np.arange(FFT_N2)[None, None, :]
    ang = 2.0 * np.pi * ((s2 * (k1 + FFT_N1 * k2)) % seq) / seq
    c, s = np.cos(ang), np.sin(ang)
    top = np.concatenate([c, s], axis=2)
    bot = np.concatenate([-s, c], axis=2)
    return jnp.asarray(np.concatenate([top, bot], axis=1), dtype=BF16)


def _fft_channel_matrix(seq):
    c = np.arange(HEAD_DIM)
    ang = 2.0 * np.pi * ((c[:, None] * c[None, :]) % HEAD_DIM) / HEAD_DIM
    scale = 1.0 / math.sqrt(seq * HEAD_DIM)
    eye = np.eye(MXU_COLS // HEAD_DIM)
    cs = np.concatenate([np.kron(eye, np.cos(ang)), np.kron(eye, np.sin(ang))], axis=0) * scale
    return jnp.asarray(cs, dtype=BF16)


def _strict_upper(n):
    return jnp.asarray(np.triu(np.ones((n, n)), k=1), dtype=BF16)


def _strict_lower(n):
    return jnp.asarray(np.tril(np.ones((n, n)), k=-1), dtype=BF16)


def _kv_kernel(mem_ref, g_ref, w_ref, o_ref):
    h = _rms(mem_ref[...], g_ref[...]).astype(BF16)
    o_ref[...] = _dot(h, w_ref[...].astype(BF16)).astype(BF16)


def _kv_proj(mem2d, g, w_kv):
    rows = mem2d.shape[0]
    cols = w_kv.shape[1]
    cb = 512
    return pl.pallas_call(
        _kv_kernel,
        grid=(cols // cb,),
        in_specs=[
            pl.BlockSpec((rows, D_MODEL), lambda j: (0, 0)),
            pl.BlockSpec((1, D_MODEL), lambda j: (0, 0)),
            pl.BlockSpec((D_MODEL, cb), lambda j: (0, j)),
        ],
        out_specs=pl.BlockSpec((rows, cb), lambda j: (0, j)),
        out_shape=jax.ShapeDtypeStruct((rows, cols), BF16),
        compiler_params=pltpu.CompilerParams(vmem_limit_bytes=VMEM_LIMIT),
        name="kv_proj",
    )(mem2d, g, w_kv)


def _mixer_in_kernel(x_ref, xp_ref, xn_ref, g_ref, w_ref, cw_ref, cb_ref, hg_ref, gm_ref,
                     conv_ref, uf_ref):
    i = pl.program_id(1)
    n_i = pl.num_programs(1)
    rows = x_ref.shape[0]
    g = g_ref[...]
    h = _rms(x_ref[...], g).astype(BF16)
    u_cv = _dot(h, w_ref[:, CONV_CH:3 * CONV_CH])
    cv = u_cv[:, :CONV_CH] * u_cv[:, CONV_CH:]

    hh = jnp.concatenate([_rms(xp_ref[...], g), _rms(xn_ref[...], g)], axis=0).astype(BF16)
    uh = _dot(hh, w_ref[:, CONV_CH:3 * CONV_CH])
    cvh = uh[:, :CONV_CH] * uh[:, CONV_CH:]
    cv_prev = cvh[7:8, :] * jnp.where(i == 0, 0.0, 1.0)
    cv_next = cvh[8:9, :] * jnp.where(i == n_i - 1, 0.0, 1.0)

    row = lax.broadcasted_iota(I32, cv.shape, 0)
    cv_up = jnp.where(row == 0, cv_prev, pltpu.roll(cv, 1, 0))
    cv_dn = jnp.where(row == rows - 1, cv_next, pltpu.roll(cv, rows - 1, 0))
    z = cw_ref[0:1, :] * cv_up + cw_ref[1:2, :] * cv + cw_ref[2:3, :] * cv_dn + cb_ref[...]
    uf_ref[...] = _dot(h, w_ref[:, 3 * CONV_CH:]).astype(BF16)
    y = _dot(h, w_ref[:, :CONV_CH]) * z
    ms = _head_mean_square(y, gm_ref[...])
    conv_ref[...] = (y * lax.rsqrt(ms + EPS) * hg_ref[...]).astype(BF16)


def _mixer_in(x2d, g, w_in, conv_w, conv_b, hg_conv, gm, batch, seq):
    n_i = seq // MIX_ROWS
    t = x2d.shape[0]
    r8 = MIX_ROWS // 8
    last8 = t // 8 - 1
    return pl.pallas_call(
        _mixer_in_kernel,
        grid=(batch, n_i),
        in_specs=[
            pl.BlockSpec((MIX_ROWS, D_MODEL), lambda b, i: (b * n_i + i, 0)),
            pl.BlockSpec((8, D_MODEL), lambda b, i: (jnp.maximum((b * n_i + i) * r8 - 1, 0), 0)),
            pl.BlockSpec((8, D_MODEL), lambda b, i: (jnp.minimum((b * n_i + i + 1) * r8, last8), 0)),
            pl.BlockSpec((1, D_MODEL), lambda b, i: (0, 0)),
            pl.BlockSpec((D_MODEL, IN_COLS), lambda b, i: (0, 0)),
            pl.BlockSpec((3, CONV_CH), lambda b, i: (0, 0)),
            pl.BlockSpec((1, CONV_CH), lambda b, i: (0, 0)),
            pl.BlockSpec((1, CONV_CH), lambda b, i: (0, 0)),
            pl.BlockSpec((MXU_COLS, MXU_COLS), lambda b, i: (0, 0)),
        ],
        out_specs=[
            pl.BlockSpec((MIX_ROWS, CONV_CH), lambda b, i: (b * n_i + i, 0)),
            pl.BlockSpec((MIX_ROWS, FFT_CH), lambda b, i: (b * n_i + i, 0)),
        ],
        out_shape=[
            jax.ShapeDtypeStruct((t, CONV_CH), BF16),
            jax.ShapeDtypeStruct((t, FFT_CH), BF16),
        ],
        compiler_params=pltpu.CompilerParams(vmem_limit_bytes=VMEM_LIMIT),
        name="mixer_in",
    )(x2d, x2d, x2d, g, w_in, conv_w, conv_b, hg_conv, gm)


_S1_ROWS = 16
_S1_LANES = 128


def _lincomb(terms):
    acc = None
    for coef, val in terms:
        if abs(coef) < 1e-12:
            continue
        if abs(coef - 1.0) < 1e-12:
            term, neg = val, False
        elif abs(coef + 1.0) < 1e-12:
            term, neg = val, True
        else:
            term, neg = coef * val, False
        if acc is None:
            acc = -term if neg else term
        else:
            acc = acc - term if neg else acc + term
    return acc


def _fft_stage1(x_ref, a_ref):
    half = FFT_N1 // 2
    cos = [[math.cos(2 * math.pi * ((k * j) % FFT_N1) / FFT_N1) for j in range(FFT_N1)] for k in range(FFT_N1)]
    sin = [[math.sin(2 * math.pi * ((k * j) % FFT_N1) / FFT_N1) for j in range(FFT_N1)] for k in range(FFT_N1)]

    def body(r, carry):
        r0 = pl.multiple_of(r * _S1_ROWS, _S1_ROWS)
        rows_re = pl.ds(r0, _S1_ROWS)
        rows_im = pl.ds(r0 + FFT_N2, _S1_ROWS)
        for lc in range(0, FFT_CH, _S1_LANES):
            lanes = slice(lc, lc + _S1_LANES)
            xs = [x_ref[j, rows_re, lanes].astype(F32) for j in range(FFT_N1)]
            ev = [None] + [xs[j] + xs[FFT_N1 - j] for j in range(1, half)]
            od = [None] + [xs[j] - xs[FFT_N1 - j] for j in range(1, half)]
            for k in range(half + 1):
                re = _lincomb([(1.0, xs[0]), (cos[k][half], xs[half])]
                              + [(cos[k][j], ev[j]) for j in range(1, half)])
                a_ref[k, rows_re, lanes] = re.astype(BF16)
                if k in (0, half):
                    zero = jnp.zeros_like(re).astype(BF16)
                    a_ref[k, rows_im, lanes] = zero
                else:
                    im = _lincomb([(-sin[k][j], od[j]) for j in range(1, half)])
                    a_ref[k, rows_im, lanes] = im.astype(BF16)
                    a_ref[FFT_N1 - k, rows_re, lanes] = re.astype(BF16)
                    a_ref[FFT_N1 - k, rows_im, lanes] = (-im).astype(BF16)
        return carry

    lax.fori_loop(0, FFT_N2 // _S1_ROWS, body, 0)


def _fourier_kernel(x_ref, m2_ref, cs_ref, gm_ref, hg_ref, o_ref, a_ref, y_ref):
    j = pl.program_id(1)

    @pl.when(j == 0)
    def _():
        _fft_stage1(x_ref, a_ref)

    ris = [_dot(m2_ref[kk], a_ref[j * FFT_K1_PER_STEP + kk]) for kk in range(FFT_K1_PER_STEP)]
    re = jnp.concatenate([ri[:FFT_N2] for ri in ris], axis=0).astype(BF16)
    im = jnp.concatenate([ri[FFT_N2:] for ri in ris], axis=0).astype(BF16)
    y = jnp.concatenate(
        [_dot(re[:, c:c + MXU_COLS], cs_ref[:MXU_COLS, :]) + _dot(im[:, c:c + MXU_COLS], cs_ref[MXU_COLS:, :])
         for c in range(0, FFT_CH, MXU_COLS)], axis=1)
    yn = y * lax.rsqrt(_head_mean_square(y, gm_ref[...]) + EPS) * hg_ref[...]
    for kk in range(FFT_K1_PER_STEP):
        k1 = j * FFT_K1_PER_STEP + kk
        for c in range(FFT_CH // LANES):
            y_ref[c, pl.ds(k1, FFT_N2, stride=FFT_N1), :] = yn[kk * FFT_N2:(kk + 1) * FFT_N2,
                                                               c * LANES:(c + 1) * LANES]

    @pl.when(j == pl.num_programs(1) - 1)
    def _():
        for c in range(FFT_CH // LANES):
            o_ref[:, c * LANES:(c + 1) * LANES] = y_ref[c].astype(BF16)


def _fourier(uf, m2, cs, gm, hg_fft, batch, seq):
    assert seq == FFT_N1 * FFT_N2
    x4 = uf.reshape(batch, FFT_N1, FFT_N2, FFT_CH)
    out = pl.pallas_call(
        _fourier_kernel,
        grid=(batch, FFT_N1 // FFT_K1_PER_STEP),
        in_specs=[
            pl.BlockSpec((None, FFT_N1, FFT_N2, FFT_CH), lambda b, j: (b, 0, 0, 0)),
            pl.BlockSpec((FFT_K1_PER_STEP, 2 * FFT_N2, 2 * FFT_N2), lambda b, j: (j, 0, 0)),
            pl.BlockSpec((2 * MXU_COLS, MXU_COLS), lambda b, j: (0, 0)),
            pl.BlockSpec((MXU_COLS, MXU_COLS), lambda b, j: (0, 0)),
            pl.BlockSpec((1, FFT_CH), lambda b, j: (0, 0)),
        ],
        out_specs=pl.BlockSpec((seq, FFT_CH), lambda b, j: (b, 0)),
        out_shape=jax.ShapeDtypeStruct((batch * seq, FFT_CH), BF16),
        scratch_shapes=[
            pltpu.VMEM((FFT_N1, 2 * FFT_N2, FFT_CH), BF16),
            pltpu.VMEM((FFT_CH // LANES, seq, LANES), F32),
        ],
        compiler_params=pltpu.CompilerParams(
            dimension_semantics=("arbitrary", "arbitrary"), vmem_limit_bytes=VMEM_LIMIT),
        name="fourier",
    )(x4, m2, cs, gm, hg_fft)
    return out


def _first_index_of_max(vals, vmax, row):
    return jnp.min(jnp.where(vals == vmax, row, vals.shape[0]), axis=0, keepdims=True)


def _route(lg):
    cols = lg.shape[1]
    row8 = lax.broadcasted_iota(I32, (EXPERTS_PER_GROUP, cols), 0)
    gl = lg[0:8, :]
    gmax = jnp.max(gl, axis=0, keepdims=True)
    g_w = 1.0 / jnp.sum(jnp.exp(gl - gmax), axis=0, keepdims=True)
    g_idx = _first_index_of_max(gl, gmax, row8)

    el = lg[8:16, :]
    for g in range(1, N_GROUPS):
        el = jnp.where(g_idx == g, lg[8 + 8 * g:16 + 8 * g, :], el)
    emax = jnp.max(el, axis=0, keepdims=True)
    ee = jnp.exp(el - emax)
    e_prob = ee / jnp.sum(ee, axis=0, keepdims=True)
    p1 = jnp.max(e_prob, axis=0, keepdims=True)
    i1 = _first_index_of_max(e_prob, p1, row8)
    rest = jnp.where(row8 == i1, -1.0, e_prob)
    p2 = jnp.max(rest, axis=0, keepdims=True)
    i2 = _first_index_of_max(rest, p2, row8)
    denom = p1 + p2
    e1 = g_idx * EXPERTS_PER_GROUP + i1
    e2 = g_idx * EXPERTS_PER_GROUP + i2
    return e1, e2, g_w * p1 / denom, g_w * p2 / denom


def _trunk_kernel(x_ref, conv_ref, fft_ref, wout_ref, gxa_ref, wq_ref, k_ref, v_ref, wo_ref,
                  gffn_ref, wr_ref, br_ref, tri_ref, ltri_ref,
                  x2_ref, xs_ref, pos_ref, gate_ref, cnt_ref, h3_s, lg_s):
    @pl.when(pl.program_id(0) == 0)
    def _():
        h3_s[...] = jnp.zeros_like(h3_s)
        lg_s[...] = jnp.zeros_like(lg_s)

    h3 = h3_s[...]
    lg = lg_s[...]
    n_tok = lg.shape[1]

    x1 = x_ref[...] + _dot(jnp.concatenate([conv_ref[...], fft_ref[...]], axis=1), wout_ref[...])

    e1, e2, gate1, gate2 = _route(lg)
    row32 = lax.broadcasted_iota(I32, (N_EXPERTS, n_tok), 0)
    hit1 = row32 == e1
    hit2 = row32 == e2
    onehot = jnp.where(hit1 | hit2, 1.0, 0.0)
    before = _dot(onehot.astype(BF16), tri_ref[...])
    cnt = jnp.sum(onehot, axis=1, keepdims=True).astype(I32)
    piece = jnp.left_shift(jnp.right_shift(cnt + (GRANULE - 1), 3), 3)
    piece_b = jnp.broadcast_to(piece.astype(F32), (N_EXPERTS, LANES)).astype(BF16)
    start = _dot(ltri_ref[...], piece_b)[:, 0:1]
    slot = before + start
    pos1 = jnp.sum(jnp.where(hit1, slot, 0.0), axis=0, keepdims=True).astype(I32)
    pos2 = jnp.sum(jnp.where(hit2, slot, 0.0), axis=0, keepdims=True).astype(I32)
    pos_ref[0:1, :] = pos1
    pos_ref[1:2, :] = pos2
    pos_ref[2:8, :] = jnp.zeros((6, n_tok), I32)
    gate_ref[0:1, :] = gate1
    gate_ref[1:2, :] = gate2
    gate_ref[2:8, :] = jnp.zeros((6, n_tok), F32)
    cnt_ref[...] = jnp.broadcast_to(cnt, (N_EXPERTS, LANES))

    h2 = _rms(x1, gxa_ref[...]).astype(BF16)
    q = _dot(h2, wq_ref[...]).astype(BF16)

    r = lax.broadcasted_iota(I32, (LOCAL_ROWS, n_tok), 0)
    perm = jnp.where((r == pos1) | (r == pos2), 1.0, 0.0).astype(BF16)

    outs = []
    for hd in range(XA_HEADS):
        cols = slice(hd * XA_HEAD_DIM, (hd + 1) * XA_HEAD_DIM)
        s = _dot_nt(q[:, cols], k_ref[:, cols]) * (XA_HEAD_DIM ** -0.5)
        s = s - jnp.max(s, axis=-1, keepdims=True)
        p = jnp.exp(s)
        p = p / jnp.sum(p, axis=-1, keepdims=True)
        outs.append(_dot(p.astype(BF16), v_ref[:, cols]).astype(BF16))
        if hd == 1:
            sorted_left = _dot(perm, h3[:, :HALF])

    o = jnp.concatenate(outs, axis=-1)
    x2 = x1 + _dot(o, wo_ref[...])
    x2_ref[...] = x2
    sorted_right = _dot(perm, h3[:, HALF:])
    h3_next = _rms(x2, gffn_ref[...]).astype(BF16)
    xs_ref[...] = _pack_halves(sorted_left, sorted_right)
    h3_s[...] = h3_next
    lg_s[...] = _dot_nt(wr_ref[...], h3_next) + br_ref[...]


def _trunk(x2d, conv_n, fft_n, w_out, g_xa, w_q, kv, w_o, g_ffn, w_r_t, b_r, tri, ltri, seq):
    t = x2d.shape[0]
    n_tiles = t // TRUNK_ROWS
    n_per_batch = seq // TRUNK_ROWS
    const = lambda i: (0, 0)
    dense = lambda i: jnp.minimum(i, n_tiles - 1)
    routed = lambda i: jnp.maximum(i - 1, 0)
    return pl.pallas_call(
        _trunk_kernel,
        grid=(n_tiles + 1,),
        in_specs=[
            pl.BlockSpec((TRUNK_ROWS, D_MODEL), lambda i: (dense(i), 0)),
            pl.BlockSpec((TRUNK_ROWS, CONV_CH), lambda i: (dense(i), 0)),
            pl.BlockSpec((TRUNK_ROWS, FFT_CH), lambda i: (dense(i), 0)),
            pl.BlockSpec((D_MODEL, D_MODEL), const),
            pl.BlockSpec((1, D_MODEL), const),
            pl.BlockSpec((D_MODEL, D_MODEL), const),
            pl.BlockSpec((MEM_LEN, D_MODEL), lambda i: (dense(i) // n_per_batch, 0)),
            pl.BlockSpec((MEM_LEN, D_MODEL), lambda i: (dense(i) // n_per_batch, 1)),
            pl.BlockSpec((D_MODEL, D_MODEL), const),
            pl.BlockSpec((1, D_MODEL), const),
            pl.BlockSpec((ROUTER_ROWS, D_MODEL), const),
            pl.BlockSpec((ROUTER_ROWS, 1), const),
            pl.BlockSpec((TRUNK_ROWS, TRUNK_ROWS), const),
            pl.BlockSpec((N_EXPERTS, N_EXPERTS), const),
        ],
        out_specs=[
            pl.BlockSpec((TRUNK_ROWS, D_MODEL), lambda i: (dense(i), 0)),
            pl.BlockSpec((LOCAL_ROWS, HALF), lambda i: (routed(i), 0)),
            pl.BlockSpec((8, TRUNK_ROWS), lambda i: (0, routed(i))),
            pl.BlockSpec((8, TRUNK_ROWS), lambda i: (0, routed(i))),
            pl.BlockSpec((None, N_EXPERTS, LANES), lambda i: (routed(i), 0, 0)),
        ],
        out_shape=[
            jax.ShapeDtypeStruct((t, D_MODEL), F32),
            jax.ShapeDtypeStruct((n_tiles * LOCAL_ROWS, HALF), I32),
            jax.ShapeDtypeStruct((8, t), I32),
            jax.ShapeDtypeStruct((8, t), F32),
            jax.ShapeDtypeStruct((n_tiles, N_EXPERTS, LANES), I32),
        ],
        scratch_shapes=[
            pltpu.VMEM((TRUNK_ROWS, D_MODEL), BF16),
            pltpu.VMEM((ROUTER_ROWS, TRUNK_ROWS), F32),
        ],
        compiler_params=pltpu.CompilerParams(
            dimension_semantics=("arbitrary",), vmem_limit_bytes=VMEM_LIMIT),
        name="trunk",
    )(x2d, conv_n, fft_n, w_out, g_xa, w_q, kv, kv, w_o, g_ffn, w_r_t, b_r, tri, ltri)


_PLAN_PER_BLOCK = ("blk_e", "first", "next_e", "slot", "nrun", "ngran")
_PLAN_PER_RUN = ("run_src", "run_off", "run_len")


def _plan_layout(n_blk):
    gpb = EXPERT_ROWS // GRANULE
    sizes = [("nblk", 1)] + [(k, n_blk) for k in _PLAN_PER_BLOCK] + [(k, n_blk * gpb) for k in _PLAN_PER_RUN]
    offsets, at = {}, 0
    for name, size in sizes:
        offsets[name] = at
        at += size
    return offsets


class _TableView:
    def __init__(self, ref, offset):
        self._ref, self._offset = ref, offset

    def __getitem__(self, i):
        return self._ref[self._offset + i]


def _experts_kernel(plan_ref, xs_hbm, wg_hbm, wu_hbm, wd_hbm, ys_hbm,
                    wg_buf, wu_buf, wd_buf, xbuf, obuf, wsems, xsems, osems, *, n_blk):
    layout = _plan_layout(n_blk)
    (blk_e_ref, first_ref, next_e_ref, slot_ref, nblk_ref, run_src_ref, run_off_ref, run_len_ref, nrun_ref,
     ngran_ref) = (_TableView(plan_ref, layout[k]) for k in (
         "blk_e", "first", "next_e", "slot", "nblk", "run_src", "run_off", "run_len", "nrun", "ngran"))
    _experts_body(blk_e_ref, first_ref, next_e_ref, slot_ref, nblk_ref,
                  run_src_ref, run_off_ref, run_len_ref, nrun_ref, ngran_ref, n_blk - 1,
                  xs_hbm, wg_hbm, wu_hbm, wd_hbm, ys_hbm,
                  wg_buf, wu_buf, wd_buf, xbuf, obuf, wsems, xsems, osems)


def _experts_body(blk_e_ref, first_ref, next_e_ref, slot_ref, nblk_ref,
                  run_src_ref, run_off_ref, run_len_ref, nrun_ref, ngran_ref, last,
                  xs_hbm, wg_hbm, wu_hbm, wd_hbm, ys_hbm,
                  wg_buf, wu_buf, wd_buf, xbuf, obuf, wsems, xsems, osems):
    n = nblk_ref[0]
    gpb = EXPERT_ROWS // GRANULE

    def fetch_weights(e, s):
        return (pltpu.make_async_copy(wg_hbm.at[e], wg_buf.at[s], wsems.at[0, s]),
                pltpu.make_async_copy(wu_hbm.at[e], wu_buf.at[s], wsems.at[1, s]),
                pltpu.make_async_copy(wd_hbm.at[e], wd_buf.at[s], wsems.at[2, s]))

    def run_copies(b, r, s):
        k = b * gpb + r
        length = run_len_ref[k]
        hbm_rows = pl.ds(run_src_ref[k], length)
        blk_rows = pl.ds(run_off_ref[k], length)
        return (pltpu.make_async_copy(xs_hbm.at[hbm_rows], xbuf.at[s, blk_rows], xsems.at[s]),
                pltpu.make_async_copy(obuf.at[s, blk_rows], ys_hbm.at[hbm_rows], osems.at[s]))

    def start_in(b, s, runs):
        lax.fori_loop(0, runs, lambda r, c: (run_copies(b, r, s)[0].start(), c)[1], 0)

    def start_out(b, s):
        lax.fori_loop(0, nrun_ref[b], lambda r, c: (run_copies(b, r, s)[1].start(), c)[1], 0)

    def wait_in(b, s):
        count = ngran_ref[b]
        pltpu.make_async_copy(xs_hbm.at[pl.ds(0, count)], xbuf.at[s, pl.ds(0, count)], xsems.at[s]).wait()

    def wait_out(b, s):
        count = ngran_ref[b]
        pltpu.make_async_copy(obuf.at[s, pl.ds(0, count)], ys_hbm.at[pl.ds(0, count)], osems.at[s]).wait()

    xbuf[...] = jnp.zeros_like(xbuf)
    for cp in fetch_weights(blk_e_ref[0], 0):
        cp.start()
    start_in(0, 0, nrun_ref[0])
    start_in(1, 1, jnp.where(n > 1, nrun_ref[1], 0))

    def block(i, carry):
        xs = lax.rem(i, X_SLOTS)
        os = i % 2
        ws = slot_ref[i]

        @pl.when(first_ref[i] == 1)
        def _():
            for cp in fetch_weights(blk_e_ref[i], ws):
                cp.wait()

            @pl.when(next_e_ref[i] >= 0)
            def _():
                for cp in fetch_weights(next_e_ref[i], 1 - ws):
                    cp.start()

        @pl.when(i >= 2)
        def _():
            wait_out(i - 2, os)

        ahead = jnp.minimum(i + 2, last)
        start_in(ahead, lax.rem(i + 2, X_SLOTS), jnp.where(i + 2 < n, nrun_ref[ahead], 0))

        wait_in(i, xs)

        def mlp(rows):
            granules = rows // GRANULE
            xl, xr = _unpack_halves(xbuf[xs, :granules].reshape(rows, HALF))
            x = jnp.concatenate([xl, xr], axis=1)
            a = _dot(x, wg_buf[ws].astype(BF16))
            b = _dot(x, wu_buf[ws].astype(BF16))
            hmid = (a * jax.nn.sigmoid(a) * b).astype(BF16)
            y = _dot(hmid, wd_buf[ws].astype(BF16))
            packed = _pack_halves(y[:, :HALF].astype(BF16).astype(F32), y[:, HALF:].astype(BF16).astype(F32))
            obuf[os, :granules] = packed.reshape(granules, GRANULE, HALF)

        short = ngran_ref[i] <= EXPERT_PAD // GRANULE
        pl.when(short)(lambda: mlp(EXPERT_PAD))
        pl.when(jnp.logical_not(short))(lambda: mlp(EXPERT_ROWS))
        start_out(i, os)
        return carry

    lax.fori_loop(0, n, block, 0)

    @pl.when(n >= 2)
    def _():
        wait_out(n - 2, n % 2)
    wait_out(n - 1, (n - 1) % 2)


def _experts(plan_table, n_blk, xs_loc, w_gate, w_up, w_down):
    granules = xs_loc.reshape(-1, GRANULE, HALF)
    hbm = pl.BlockSpec(memory_space=pl.ANY)
    return pl.pallas_call(
        functools.partial(_experts_kernel, n_blk=n_blk),
        grid_spec=pltpu.PrefetchScalarGridSpec(
            num_scalar_prefetch=1,
            grid=(1,),
            in_specs=[hbm, hbm, hbm, hbm],
            out_specs=hbm,
            scratch_shapes=[
                pltpu.VMEM((2, D_MODEL, D_EXPERT), F32),
                pltpu.VMEM((2, D_MODEL, D_EXPERT), F32),
                pltpu.VMEM((2, D_EXPERT, D_MODEL), F32),
                pltpu.VMEM((X_SLOTS, EXPERT_ROWS // GRANULE, GRANULE, HALF), I32),
                pltpu.VMEM((2, EXPERT_ROWS // GRANULE, GRANULE, HALF), I32),
                pltpu.SemaphoreType.DMA((3, 2)),
                pltpu.SemaphoreType.DMA((X_SLOTS,)),
                pltpu.SemaphoreType.DMA((2,)),
            ],
        ),
        out_shape=jax.ShapeDtypeStruct(granules.shape, I32),
        input_output_aliases={1: 0},
        compiler_params=pltpu.CompilerParams(
            dimension_semantics=("arbitrary",), vmem_limit_bytes=VMEM_LIMIT),
        name="experts",
    )(plan_table, granules, w_gate, w_up, w_down).reshape(xs_loc.shape)


def _combine_kernel(x2_ref, pos_ref, gate_ref, g_ref, ys_ref, o_ref):
    n_tok = x2_ref.shape[0]
    r = lax.broadcasted_iota(I32, (LOCAL_ROWS, n_tok), 0)
    w_t = (jnp.where(r == pos_ref[0:1, :], gate_ref[0:1, :], 0.0)
           + jnp.where(r == pos_ref[1:2, :], gate_ref[1:2, :], 0.0)).astype(BF16)
    yl, yr = _unpack_halves(ys_ref[...])
    moe = jnp.concatenate([_dot_tn(w_t, yl), _dot_tn(w_t, yr)], axis=-1)
    o_ref[...] = _rms(x2_ref[...] + moe, g_ref[...])


def _combine(x2, pos_tk, gates_tk, g_final, ys_loc):
    t = x2.shape[0]
    return pl.pallas_call(
        _combine_kernel,
        grid=(t // TRUNK_ROWS,),
        in_specs=[
            pl.BlockSpec((TRUNK_ROWS, D_MODEL), lambda i: (i, 0)),
            pl.BlockSpec((8, TRUNK_ROWS), lambda i: (0, i)),
            pl.BlockSpec((8, TRUNK_ROWS), lambda i: (0, i)),
            pl.BlockSpec((1, D_MODEL), lambda i: (0, 0)),
            pl.BlockSpec((LOCAL_ROWS, HALF), lambda i: (i, 0)),
        ],
        out_specs=pl.BlockSpec((TRUNK_ROWS, D_MODEL), lambda i: (i, 0)),
        out_shape=jax.ShapeDtypeStruct((t, D_MODEL), F32),
        compiler_params=pltpu.CompilerParams(vmem_limit_bytes=VMEM_LIMIT),
        name="combine",
    )(x2, pos_tk, gates_tk, g_final, ys_loc)


def _router_params(w_rg, b_rg, w_re, b_re):
    w = jnp.zeros((ROUTER_ROWS, D_MODEL), F32)
    w = w.at[0:N_GROUPS].set(w_rg.T).at[8:8 + N_EXPERTS].set(w_re.T)
    b = jnp.zeros((ROUTER_ROWS,), F32)
    b = b.at[0:N_GROUPS].set(b_rg).at[N_GROUPS:8].set(NEG_BIG).at[8:8 + N_EXPERTS].set(b_re)
    return w.astype(BF16), b.reshape(ROUTER_ROWS, 1)


def _plan(cnt, max_padded_rows):
    n_tiles = cnt.shape[0]
    piece = (cnt + GRANULE - 1) // GRANULE * GRANULE
    lend = jnp.cumsum(piece, axis=1)
    lstart = lend - piece
    tot = jnp.sum(piece, axis=0)
    padded = (tot + EXPERT_PAD - 1) // EXPERT_PAD * EXPERT_PAD
    cum_tiles = jnp.cumsum(piece, axis=0)

    per_expert = (padded + EXPERT_ROWS - 1) // EXPERT_ROWS
    blk_end = jnp.cumsum(per_expert)
    n_blk = (max_padded_rows // EXPERT_PAD + N_EXPERTS + 1) // 2
    blk = jnp.arange(n_blk, dtype=I32)
    blk_e = jnp.minimum(jnp.sum((blk_end[None, :] <= blk[:, None]).astype(I32), axis=1), N_EXPERTS - 1)
    nblk = blk_end[-1:]
    valid = blk < nblk

    of_blk_e = blk_e[:, None] == jnp.arange(N_EXPERTS, dtype=I32)[None, :]
    pick = lambda table: jnp.sum(jnp.where(of_blk_e[:, None, :], table[None, :, :], 0), axis=2)
    pick1 = lambda vec: jnp.sum(jnp.where(of_blk_e, vec[None, :], 0), axis=1)
    seg_off = (blk - pick1(blk_end - per_expert)) * EXPERT_ROWS
    off = seg_off[:, None] + jnp.arange(EXPERT_ROWS // GRANULE, dtype=I32)[None, :] * GRANULE
    real = (off < pick1(tot)[:, None]) & valid[:, None]
    ngran = jnp.sum(real.astype(I32), axis=1)
    cum_b = pick(cum_tiles)
    tile_of = jnp.minimum(jnp.sum((cum_b[:, None, :] <= off[:, :, None]).astype(I32), axis=2), n_tiles - 1)
    base = (jnp.arange(n_tiles, dtype=I32) * LOCAL_ROWS)[None, :] + pick(lstart) - (cum_b - pick(piece))
    of_tile = tile_of[:, :, None] == jnp.arange(n_tiles, dtype=I32)
    row = jnp.sum(jnp.where(of_tile, base[:, None, :], 0), axis=2) + off
    gran = jnp.where(real, row, 0) // GRANULE
    g_idx = jnp.arange(EXPERT_ROWS // GRANULE, dtype=I32)
    follows = jnp.concatenate([jnp.zeros((n_blk, 1), bool), gran[:, 1:] == gran[:, :-1] + 1], axis=1)
    run_start = real & jnp.logical_not(follows)
    run_of = jnp.cumsum(run_start.astype(I32), axis=1) - 1
    in_run = (run_of[:, :, None] == g_idx[None, None, :]) & real[:, :, None]
    at_start = in_run & run_start[:, :, None]
    run_len = jnp.sum(in_run.astype(I32), axis=1)
    run_src = jnp.sum(jnp.where(at_start, gran[:, :, None], 0), axis=1)
    run_off = jnp.sum(jnp.where(at_start, g_idx[None, :, None], 0), axis=1)
    nrun = jnp.sum(run_start.astype(I32), axis=1)
    change =jnp.concatenate([jnp.ones((1,), bool), blk_e[1:] != blk_e[:-1]])
    slot = (jnp.cumsum(change.astype(I32)) - 1) % 2
    later = (blk_e[None, :] > blk_e[:, None]) & valid[None, :]
    next_e = jnp.min(jnp.where(later, blk_e[None, :], N_EXPERTS), axis=1)
    next_e = jnp.where(next_e == N_EXPERTS, -1, next_e)
    tables = dict(
        nblk=nblk, blk_e=blk_e, first=change & valid, next_e=next_e, slot=slot, nrun=nrun, ngran=ngran,
        run_src=run_src.reshape(-1), run_off=run_off.reshape(-1), run_len=run_len.reshape(-1),
    )
    order = ("nblk",) + _PLAN_PER_BLOCK + _PLAN_PER_RUN
    return jnp.concatenate([tables[k].astype(I32) for k in order]), n_blk


def _layer(x2d, mem2d, batch, seq, norm_mix_g, w_in, conv_w, conv_b, head_norm_g, w_out,
           norm_xa_g, norm_mem_g, w_q, w_kv, w_o, norm_ffn_g, w_rg, b_rg, w_re, b_re,
           w_gate, w_up, w_down, out_norm_g):
    t = x2d.shape[0]
    n_tiles = t // TRUNK_ROWS
    row = lambda v: v.reshape(1, -1)
    hg = head_norm_g.reshape(-1)
    gm = _group_mean_matrix()

    kv = _kv_proj(mem2d, row(norm_mem_g), w_kv)
    conv_n, uf = _mixer_in(x2d, row(norm_mix_g), w_in.astype(BF16), conv_w, row(conv_b),
                           row(hg[:CONV_CH]), gm, batch, seq)
    fft_n = _fourier(uf.reshape(batch, seq, FFT_CH), _fft_stage2_matrices(seq), _fft_channel_matrix(seq),
                     gm, row(hg[CONV_CH:]), batch, seq)
    w_r_t, b_r = _router_params(w_rg, b_rg, w_re, b_re)
    x2, xs_loc, pos, gates, cnt = _trunk(
        x2d, conv_n, fft_n, w_out.astype(BF16), row(norm_xa_g), w_q.astype(BF16), kv, w_o.astype(BF16),
        row(norm_ffn_g), w_r_t, b_r, _strict_upper(TRUNK_ROWS), _strict_lower(N_EXPERTS), seq)

    max_rows = n_tiles * LOCAL_ROWS + N_EXPERTS * (EXPERT_PAD - GRANULE)
    n_global_rows = -(-max_rows // EXPERT_PAD) * EXPERT_PAD
    plan_table, n_blk = _plan(cnt[:, :, 0], n_global_rows)
    ys_loc = _experts(plan_table, n_blk, xs_loc, w_gate, w_up, w_down)
    return _combine(x2, pos, gates, row(out_norm_g), ys_loc)


def kernel(x, mem, norm_mix_g, w_in, conv_w, conv_b, head_norm_g, w_out, norm_xa_g, norm_mem_g, w_q, w_kv,
           w_o, norm_ffn_g, w_route_group, b_route_group, w_route_expert, b_route_expert, w_gate, w_up,
           w_down, final_norm_g):
    batch, seq, _ = x.shape
    depth = norm_mix_g.shape[0]
    assert depth == 1, "the final norm is fused into the last layer's combine kernel"
    x2d = x.reshape(batch * seq, D_MODEL)
    mem2d = mem.reshape(batch * MEM_LEN, D_MODEL)
    l = 0
    out = _layer(x2d, mem2d, batch, seq, norm_mix_g[l], w_in[l], conv_w[l], conv_b[l], head_norm_g[l],
                 w_out[l], norm_xa_g[l], norm_mem_g[l], w_q[l], w_kv[l], w_o[l], norm_ffn_g[l],
                 w_route_group[l], b_route_group[l], w_route_expert[l], b_route_expert[l],
                 w_gate[l], w_up[l], w_down[l], final_norm_g)
    return out.reshape(batch, seq, D_MODEL)
```

```python
import functools
import math

import numpy as np
import jax
import jax.numpy as jnp
from jax import lax
from jax.experimental import pallas as pl
from jax.experimental.pallas import tpu as pltpu

F32 = jnp.float32
BF16 = jnp.bfloat16
I32 = jnp.int32

D_MODEL = 1024
HALF = D_MODEL // 2
HEAD_DIM = 64
CONV_CH = 512
FFT_CH = 512
IN_COLS = 3 * CONV_CH + FFT_CH
MEM_LEN = 256
XA_HEADS = 4
XA_HEAD_DIM = D_MODEL // XA_HEADS
N_GROUPS = 4
EXPERTS_PER_GROUP = 8
N_EXPERTS = 32
TOP_K = 2
D_EXPERT = 512
EPS = 1e-6

FFT_N1 = 16
FFT_N2 = 256
FFT_K1_PER_STEP = 4

LANES = 128
MXU_COLS = 256
GRANULE = 8
MIX_ROWS = 512
TRUNK_ROWS = 512
LOCAL_ROWS = TOP_K * TRUNK_ROWS + N_EXPERTS * GRANULE
LOCAL_GRANULES = LOCAL_ROWS // GRANULE
EXPERT_PAD = 256
EXPERT_ROWS = 1024
X_SLOTS = 3
ROUTER_ROWS = 128
NEG_BIG = -1e30
HI16 = -65536

VMEM_LIMIT = 56 * 1024 * 1024


def _rms(x, g):
    return x * lax.rsqrt(jnp.mean(x * x, axis=-1, keepdims=True) + EPS) * g


def _dot(a, b):
    return jnp.dot(a, b, preferred_element_type=F32)


def _dot_nt(a, b):
    return lax.dot_general(a, b, (((1,), (1,)), ((), ())), preferred_element_type=F32)


def _dot_tn(a, b):
    return lax.dot_general(a, b, (((0,), (0,)), ((), ())), preferred_element_type=F32)


def _pack_halves(left_f32, right_f32):
    lb = lax.bitcast_convert_type(left_f32, I32)
    rb = lax.shift_right_logical(lax.bitcast_convert_type(right_f32, I32), jnp.int32(16))
    return lb | rb


def _unpack_halves(packed_i32):
    left = lax.bitcast_convert_type(packed_i32 & jnp.int32(HI16), F32)
    right = lax.bitcast_convert_type(lax.shift_left(packed_i32, jnp.int32(16)), F32)
    return left.astype(BF16), right.astype(BF16)


def _group_mean_matrix():
    g = np.kron(np.eye(MXU_COLS // HEAD_DIM), np.full((HEAD_DIM, HEAD_DIM), 1.0 / HEAD_DIM))
    return jnp.asarray(g, dtype=BF16)


def _head_mean_square(y, gm):
    sq = (y * y).astype(BF16)
    return jnp.concatenate([_dot(sq[:, c:c + MXU_COLS], gm) for c in range(0, y.shape[1], MXU_COLS)], axis=1)


def _fft_stage2_matrices(seq):
    k1 = np.arange(FFT_N1)[:, None, None]
    k2 = np.arange(FFT_N2)[None, :, None]
    s2 = np.arange(FFT_N2)[None, None, :]
    ang = 2.0 * np.pi * ((s2 * (k1 + FFT_N1 * k2)) % seq) / seq
    c, s = np.cos(ang), np.sin(ang)
    top = np.concatenate([c, s], axis=2)
    bot = np.concatenate([-s, c], axis=2)
    return jnp.asarray(np.concatenate([top, bot], axis=1), dtype=BF16)


def _fft_channel_matrix(seq):
    c = np.arange(HEAD_DIM)
    ang = 2.0 * np.pi * ((c[:, None] * c[None, :]) % HEAD_DIM) / HEAD_DIM
    scale = 1.0 / math.sqrt(seq * HEAD_DIM)
    eye = np.eye(MXU_COLS // HEAD_DIM)
    cs = np.concatenate([np.kron(eye, np.cos(ang)), np.kron(eye, np.sin(ang))], axis=0) * scale
    return jnp.asarray(cs, dtype=BF16)


def _strict_upper(n):
    return jnp.asarray(np.triu(np.ones((n, n)), k=1), dtype=BF16)


def _strict_lower(n):
    return jnp.asarray(np.tril(np.ones((n, n)), k=-1), dtype=BF16)


def _kv_kernel(mem_ref, g_ref, w_ref, o_ref):
    h = _rms(mem_ref[...], g_ref[...]).astype(BF16)
    o_ref[...] = _dot(h, w_ref[...].astype(BF16)).astype(BF16)


def _kv_proj(mem2d, g, w_kv):
    rows = mem2d.shape[0]
    cols = w_kv.shape[1]
    cb = 512
    return pl.pallas_call(
        _kv_kernel,
        grid=(cols // cb,),
        in_specs=[
            pl.BlockSpec((rows, D_MODEL), lambda j: (0, 0)),
            pl.BlockSpec((1, D_MODEL), lambda j: (0, 0)),
            pl.BlockSpec((D_MODEL, cb), lambda j: (0, j)),
        ],
        out_specs=pl.BlockSpec((rows, cb), lambda j: (0, j)),
        out_shape=jax.ShapeDtypeStruct((rows, cols), BF16),
        compiler_params=pltpu.CompilerParams(vmem_limit_bytes=VMEM_LIMIT),
        name="kv_proj",
    )(mem2d, g, w_kv)


def _mixer_in_kernel(x_ref, xp_ref, xn_ref, g_ref, w_ref, cw_ref, cb_ref, hg_ref, gm_ref,
                     conv_ref, uf_ref):
    i = pl.program_id(1)
    n_i = pl.num_programs(1)
    rows = x_ref.shape[0]
    g = g_ref[...]
    h = _rms(x_ref[...], g).astype(BF16)
    u_cv = _dot(h, w_ref[:, CONV_CH:3 * CONV_CH])
    cv = u_cv[:, :CONV_CH] * u_cv[:, CONV_CH:]

    hh = jnp.concatenate([_rms(xp_ref[...], g), _rms(xn_ref[...], g)], axis=0).astype(BF16)
    uh = _dot(hh, w_ref[:, CONV_CH:3 * CONV_CH])
    cvh = uh[:, :CONV_CH] * uh[:, CONV_CH:]
    cv_prev = cvh[7:8, :] * jnp.where(i == 0, 0.0, 1.0)
    cv_next = cvh[8:9, :] * jnp.where(i == n_i - 1, 0.0, 1.0)

    row = lax.broadcasted_iota(I32, cv.shape, 0)
    cv_up = jnp.where(row == 0, cv_prev, pltpu.roll(cv, 1, 0))
    cv_dn = jnp.where(row == rows - 1, cv_next, pltpu.roll(cv, rows - 1, 0))
    z = cw_ref[0:1, :] * cv_up + cw_ref[1:2, :] * cv + cw_ref[2:3, :] * cv_dn + cb_ref[...]
    uf_ref[...] = _dot(h, w_ref[:, 3 * CONV_CH:]).astype(BF16)
    y = _dot(h, w_ref[:, :CONV_CH]) * z
    ms = _head_mean_square(y, gm_ref[...])
    conv_ref[...] = (y * lax.rsqrt(ms + EPS) * hg_ref[...]).astype(BF16)


def _mixer_in(x2d, g, w_in, conv_w, conv_b, hg_conv, gm, batch, seq):
    n_i = seq // MIX_ROWS
    t = x2d.shape[0]
    r8 = MIX_ROWS // 8
    last8 = t // 8 - 1
    return pl.pallas_call(
        _mixer_in_kernel,
        grid=(batch, n_i),
        in_specs=[
            pl.BlockSpec((MIX_ROWS, D_MODEL), lambda b, i: (b * n_i + i, 0)),
            pl.BlockSpec((8, D_MODEL), lambda b, i: (jnp.maximum((b * n_i + i) * r8 - 1, 0), 0)),
            pl.BlockSpec((8, D_MODEL), lambda b, i: (jnp.minimum((b * n_i + i + 1) * r8, last8), 0)),
            pl.BlockSpec((1, D_MODEL), lambda b, i: (0, 0)),
            pl.BlockSpec((D_MODEL, IN_COLS), lambda b, i: (0, 0)),
            pl.BlockSpec((3, CONV_CH), lambda b, i: (0, 0)),
            pl.BlockSpec((1, CONV_CH), lambda b, i: (0, 0)),
            pl.BlockSpec((1, CONV_CH), lambda b, i: (0, 0)),
            pl.BlockSpec((MXU_COLS, MXU_COLS), lambda b, i: (0, 0)),
        ],
        out_specs=[
            pl.BlockSpec((MIX_ROWS, CONV_CH), lambda b, i: (b * n_i + i, 0)),
            pl.BlockSpec((MIX_ROWS, FFT_CH), lambda b, i: (b * n_i + i, 0)),
        ],
        out_shape=[
            jax.ShapeDtypeStruct((t, CONV_CH), BF16),
            jax.ShapeDtypeStruct((t, FFT_CH), BF16),
        ],
        compiler_params=pltpu.CompilerParams(vmem_limit_bytes=VMEM_LIMIT),
        name="mixer_in",
    )(x2d, x2d, x2d, g, w_in, conv_w, conv_b, hg_conv, gm)


_S1_ROWS = 16
_S1_LANES = 128


def _lincomb(terms):
    acc = None
    for coef, val in terms:
        if abs(coef) < 1e-12:
            continue
        if abs(coef - 1.0) < 1e-12:
            term, neg = val, False
        elif abs(coef + 1.0) < 1e-12:
            term, neg = val, True
        else:
            term, neg = coef * val, False
        if acc is None:
            acc = -term if neg else term
        else:
            acc = acc - term if neg else acc + term
    return acc


def _fft_stage1(x_ref, a_ref):
    half = FFT_N1 // 2
    cos = [[math.cos(2 * math.pi * ((k * j) % FFT_N1) / FFT_N1) for j in range(FFT_N1)] for k in range(FFT_N1)]
    sin = [[math.sin(2 * math.pi * ((k * j) % FFT_N1) / FFT_N1) for j in range(FFT_N1)] for k in range(FFT_N1)]

    def body(r, carry):
        r0 = pl.multiple_of(r * _S1_ROWS, _S1_ROWS)
        rows_re = pl.ds(r0, _S1_ROWS)
        rows_im = pl.ds(r0 + FFT_N2, _S1_ROWS)
        for lc in range(0, FFT_CH, _S1_LANES):
            lanes = slice(lc, lc + _S1_LANES)
            xs = [x_ref[j, rows_re, lanes].astype(F32) for j in range(FFT_N1)]
            ev = [None] + [xs[j] + xs[FFT_N1 - j] for j in range(1, half)]
            od = [None] + [xs[j] - xs[FFT_N1 - j] for j in range(1, half)]
            for k in range(half + 1):
                re = _lincomb([(1.0, xs[0]), (cos[k][half], xs[half])]
                              + [(cos[k][j], ev[j]) for j in range(1, half)])
                a_ref[k, rows_re, lanes] = re.astype(BF16)
                if k in (0, half):
                    zero = jnp.zeros_like(re).astype(BF16)
                    a_ref[k, rows_im, lanes] = zero
                else:
                    im = _lincomb([(-sin[k][j], od[j]) for j in range(1, half)])
                    a_ref[k, rows_im, lanes] = im.astype(BF16)
                    a_ref[FFT_N1 - k, rows_re, lanes] = re.astype(BF16)
                    a_ref[FFT_N1 - k, rows_im, lanes] = (-im).astype(BF16)
        return carry

    lax.fori_loop(0, FFT_N2 // _S1_ROWS, body, 0)


def _fourier_kernel(x_ref, m2_ref, cs_ref, gm_ref, hg_ref, o_ref, a_ref, y_ref):
    j = pl.program_id(1)

    @pl.when(j == 0)
    def _():
        _fft_stage1(x_ref, a_ref)

    ris = [_dot(m2_ref[kk], a_ref[j * FFT_K1_PER_STEP + kk]) for kk in range(FFT_K1_PER_STEP)]
    re = jnp.concatenate([ri[:FFT_N2] for ri in ris], axis=0).astype(BF16)
    im = jnp.concatenate([ri[FFT_N2:] for ri in ris], axis=0).astype(BF16)
    y = jnp.concatenate(
        [_dot(re[:, c:c + MXU_COLS], cs_ref[:MXU_COLS, :]) + _dot(im[:, c:c + MXU_COLS], cs_ref[MXU_COLS:, :])
         for c in range(0, FFT_CH, MXU_COLS)], axis=1)
    yn = y * lax.rsqrt(_head_mean_square(y, gm_ref[...]) + EPS) * hg_ref[...]
    for kk in range(FFT_K1_PER_STEP):
        k1 = j * FFT_K1_PER_STEP + kk
        for c in range(FFT_CH // LANES):
            y_ref[c, pl.ds(k1, FFT_N2, stride=FFT_N1), :] = yn[kk * FFT_N2:(kk + 1) * FFT_N2,
                                                               c * LANES:(c + 1) * LANES]

    @pl.when(j == pl.num_programs(1) - 1)
    def _():
        for c in range(FFT_CH // LANES):
            o_ref[:, c * LANES:(c + 1) * LANES] = y_ref[c].astype(BF16)


def _fourier(uf, m2, cs, gm, hg_fft, batch, seq):
    assert seq == FFT_N1 * FFT_N2
    x4 = uf.reshape(batch, FFT_N1, FFT_N2, FFT_CH)
    out = pl.pallas_call(
        _fourier_kernel,
        grid=(batch, FFT_N1 // FFT_K1_PER_STEP),
        in_specs=[
            pl.BlockSpec((None, FFT_N1, FFT_N2, FFT_CH), lambda b, j: (b, 0, 0, 0)),
            pl.BlockSpec((FFT_K1_PER_STEP, 2 * FFT_N2, 2 * FFT_N2), lambda b, j: (j, 0, 0)),
            pl.BlockSpec((2 * MXU_COLS, MXU_COLS), lambda b, j: (0, 0)),
            pl.BlockSpec((MXU_COLS, MXU_COLS), lambda b, j: (0, 0)),
            pl.BlockSpec((1, FFT_CH), lambda b, j: (0, 0)),
        ],
        out_specs=pl.BlockSpec((seq, FFT_CH), lambda b, j: (b, 0)),
        out_shape=jax.ShapeDtypeStruct((batch * seq, FFT_CH), BF16),
        scratch_shapes=[
            pltpu.VMEM((FFT_N1, 2 * FFT_N2, FFT_CH), BF16),
            pltpu.VMEM((FFT_CH // LANES, seq, LANES), F32),
        ],
        compiler_params=pltpu.CompilerParams(
            dimension_semantics=("arbitrary", "arbitrary"), vmem_limit_bytes=VMEM_LIMIT),
        name="fourier",
    )(x4, m2, cs, gm, hg_fft)
    return out


def _first_index_of_max(vals, vmax, row):
    return jnp.min(jnp.where(vals == vmax, row, vals.shape[0]), axis=0, keepdims=True)


def _route(lg):
    cols = lg.shape[1]
    row8 = lax.broadcasted_iota(I32, (EXPERTS_PER_GROUP, cols), 0)
    gl = lg[0:8, :]
    gmax = jnp.max(gl, axis=0, keepdims=True)
    g_w = 1.0 / jnp.sum(jnp.exp(gl - gmax), axis=0, keepdims=True)
    g_idx = _first_index_of_max(gl, gmax, row8)

    el = lg[8:16, :]
    for g in range(1, N_GROUPS):
        el = jnp.where(g_idx == g, lg[8 + 8 * g:16 + 8 * g, :], el)
    emax = jnp.max(el, axis=0, keepdims=True)
    ee = jnp.exp(el - emax)
    e_prob = ee / jnp.sum(ee, axis=0, keepdims=True)
    p1 = jnp.max(e_prob, axis=0, keepdims=True)
    i1 = _first_index_of_max(e_prob, p1, row8)
    rest = jnp.where(row8 == i1, -1.0, e_prob)
    p2 = jnp.max(rest, axis=0, keepdims=True)
    i2 = _first_index_of_max(rest, p2, row8)
    denom = p1 + p2
    e1 = g_idx * EXPERTS_PER_GROUP + i1
    e2 = g_idx * EXPERTS_PER_GROUP + i2
    return e1, e2, g_w * p1 / denom, g_w * p2 / denom


def _trunk_kernel(x_ref, conv_ref, fft_ref, wout_ref, gxa_ref, wq_ref, k_ref, v_ref, wo_ref,
                  gffn_ref, wr_ref, br_ref, tri_ref, ltri_ref,
                  x2_ref, xs_ref, pos_ref, gate_ref, cnt_ref, h3_s, lg_s):
    @pl.when(pl.program_id(0) == 0)
    def _():
        h3_s[...] = jnp.zeros_like(h3_s)
        lg_s[...] = jnp.zeros_like(lg_s)

    h3 = h3_s[...]
    lg = lg_s[...]
    n_tok = lg.shape[1]

    x1 = x_ref[...] + _dot(jnp.concatenate([conv_ref[...], fft_ref[...]], axis=1), wout_ref[...])

    e1, e2, gate1, gate2 = _route(lg)
    row32 = lax.broadcasted_iota(I32, (N_EXPERTS, n_tok), 0)
    hit1 = row32 == e1
    hit2 = row32 == e2
    onehot = jnp.where(hit1 | hit2, 1.0, 0.0)
    before = _dot(onehot.astype(BF16), tri_ref[...])
    cnt = jnp.sum(onehot, axis=1, keepdims=True).astype(I32)
    piece = jnp.left_shift(jnp.right_shift(cnt + (GRANULE - 1), 3), 3)
    piece_b = jnp.broadcast_to(piece.astype(F32), (N_EXPERTS, LANES)).astype(BF16)
    start = _dot(ltri_ref[...], piece_b)[:, 0:1]
    slot = before + start
    pos1 = jnp.sum(jnp.where(hit1, slot, 0.0), axis=0, keepdims=True).astype(I32)
    pos2 = jnp.sum(jnp.where(hit2, slot, 0.0), axis=0, keepdims=True).astype(I32)
    pos_ref[0:1, :] = pos1
    pos_ref[1:2, :] = pos2
    pos_ref[2:8, :] = jnp.zeros((6, n_tok), I32)
    gate_ref[0:1, :] = gate1
    gate_ref[1:2, :] = gate2
    gate_ref[2:8, :] = jnp.zeros((6, n_tok), F32)
    cnt_ref[...] = jnp.broadcast_to(cnt, (N_EXPERTS, LANES))

    h2 = _rms(x1, gxa_ref[...]).astype(BF16)
    q = _dot(h2, wq_ref[...]).astype(BF16)

    r = lax.broadcasted_iota(I32, (LOCAL_ROWS, n_tok), 0)
    perm = jnp.where((r == pos1) | (r == pos2), 1.0, 0.0).astype(BF16)

    outs = []
    for hd in range(XA_HEADS):
        cols = slice(hd * XA_HEAD_DIM, (hd + 1) * XA_HEAD_DIM)
        s = _dot_nt(q[:, cols], k_ref[:, cols]) * (XA_HEAD_DIM ** -0.5)
        s = s - jnp.max(s, axis=-1, keepdims=True)
        p = jnp.exp(s)
        p = p / jnp.sum(p, axis=-1, keepdims=True)
        outs.append(_dot(p.astype(BF16), v_ref[:, cols]).astype(BF16))
        if hd == 1:
            sorted_left = _dot(perm, h3[:, :HALF])

    o = jnp.concatenate(outs, axis=-1)
    x2 = x1 + _dot(o, wo_ref[...])
    x2_ref[...] = x2
    sorted_right = _dot(perm, h3[:, HALF:])
    h3_next = _rms(x2, gffn_ref[...]).astype(BF16)
    xs_ref[...] = _pack_halves(sorted_left, sorted_right)
    h3_s[...] = h3_next
    lg_s[...] = _dot_nt(wr_ref[...], h3_next) + br_ref[...]


def _trunk(x2d, conv_n, fft_n, w_out, g_xa, w_q, kv, w_o, g_ffn, w_r_t, b_r, tri, ltri, seq):
    t = x2d.shape[0]
    n_tiles = t // TRUNK_ROWS
    n_per_batch = seq // TRUNK_ROWS
    const = lambda i: (0, 0)
    dense = lambda i: jnp.minimum(i, n_tiles - 1)
    routed = lambda i: jnp.maximum(i - 1, 0)
    return pl.pallas_call(
        _trunk_kernel,
        grid=(n_tiles + 1,),
        in_specs=[
            pl.BlockSpec((TRUNK_ROWS, D_MODEL), lambda i: (dense(i), 0)),
            pl.BlockSpec((TRUNK_ROWS, CONV_CH), lambda i: (dense(i), 0)),
            pl.BlockSpec((TRUNK_ROWS, FFT_CH), lambda i: (dense(i), 0)),
            pl.BlockSpec((D_MODEL, D_MODEL), const),
            pl.BlockSpec((1, D_MODEL), const),
            pl.BlockSpec((D_MODEL, D_MODEL), const),
            pl.BlockSpec((MEM_LEN, D_MODEL), lambda i: (dense(i) // n_per_batch, 0)),
            pl.BlockSpec((MEM_LEN, D_MODEL), lambda i: (dense(i) // n_per_batch, 1)),
            pl.BlockSpec((D_MODEL, D_MODEL), const),
            pl.BlockSpec((1, D_MODEL), const),
            pl.BlockSpec((ROUTER_ROWS, D_MODEL), const),
            pl.BlockSpec((ROUTER_ROWS, 1), const),
            pl.BlockSpec((TRUNK_ROWS, TRUNK_ROWS), const),
            pl.BlockSpec((N_EXPERTS, N_EXPERTS), const),
        ],
        out_specs=[
            pl.BlockSpec((TRUNK_ROWS, D_MODEL), lambda i: (dense(i), 0)),
            pl.BlockSpec((LOCAL_ROWS, HALF), lambda i: (routed(i), 0)),
            pl.BlockSpec((8, TRUNK_ROWS), lambda i: (0, routed(i))),
            pl.BlockSpec((8, TRUNK_ROWS), lambda i: (0, routed(i))),
            pl.BlockSpec((None, N_EXPERTS, LANES), lambda i: (routed(i), 0, 0)),
        ],
        out_shape=[
            jax.ShapeDtypeStruct((t, D_MODEL), F32),
            jax.ShapeDtypeStruct((n_tiles * LOCAL_ROWS, HALF), I32),
            jax.ShapeDtypeStruct((8, t), I32),
            jax.ShapeDtypeStruct((8, t), F32),
            jax.ShapeDtypeStruct((n_tiles, N_EXPERTS, LANES), I32),
        ],
        scratch_shapes=[
            pltpu.VMEM((TRUNK_ROWS, D_MODEL), BF16),
            pltpu.VMEM((ROUTER_ROWS, TRUNK_ROWS), F32),
        ],
        compiler_params=pltpu.CompilerParams(
            dimension_semantics=("arbitrary",), vmem_limit_bytes=VMEM_LIMIT),
        name="trunk",
    )(x2d, conv_n, fft_n, w_out, g_xa, w_q, kv, kv, w_o, g_ffn, w_r_t, b_r, tri, ltri)


_PLAN_PER_BLOCK = ("blk_e", "first", "next_e", "slot", "nrun", "ngran")
_PLAN_PER_RUN = ("run_src", "run_off", "run_len")


def _plan_layout(n_blk):
    gpb = EXPERT_ROWS // GRANULE
    sizes = [("nblk", 1)] + [(k, n_blk) for k in _PLAN_PER_BLOCK] + [(k, n_blk * gpb) for k in _PLAN_PER_RUN]
    offsets, at = {}, 0
    for name, size in sizes:
        offsets[name] = at
        at += size
    return offsets


class _TableView:
    def __init__(self, ref, offset):
        self._ref, self._offset = ref, offset

    def __getitem__(self, i):
        return self._ref[self._offset + i]


def _experts_kernel(plan_ref, xs_hbm, wg_hbm, wu_hbm, wd_hbm, ys_hbm,
                    wg_buf, wu_buf, wd_buf, xbuf, obuf, wsems, xsems, osems, *, n_blk):
    layout = _plan_layout(n_blk)
    (blk_e_ref, first_ref, next_e_ref, slot_ref, nblk_ref, run_src_ref, run_off_ref, run_len_ref, nrun_ref,
     ngran_ref) = (_TableView(plan_ref, layout[k]) for k in (
         "blk_e", "first", "next_e", "slot", "nblk", "run_src", "run_off", "run_len", "nrun", "ngran"))
    _experts_body(blk_e_ref, first_ref, next_e_ref, slot_ref, nblk_ref,
                  run_src_ref, run_off_ref, run_len_ref, nrun_ref, ngran_ref, n_blk - 1,
                  xs_hbm, wg_hbm, wu_hbm, wd_hbm, ys_hbm,
                  wg_buf, wu_buf, wd_buf, xbuf, obuf, wsems, xsems, osems)


def _experts_body(blk_e_ref, first_ref, next_e_ref, slot_ref, nblk_ref,
                  run_src_ref, run_off_ref, run_len_ref, nrun_ref, ngran_ref, last,
                  xs_hbm, wg_hbm, wu_hbm, wd_hbm, ys_hbm,
                  wg_buf, wu_buf, wd_buf, xbuf, obuf, wsems, xsems, osems):
    n = nblk_ref[0]
    gpb = EXPERT_ROWS // GRANULE

    def fetch_weights(e, s):
        return (pltpu.make_async_copy(wg_hbm.at[e], wg_buf.at[s], wsems.at[0, s]),
                pltpu.make_async_copy(wu_hbm.at[e], wu_buf.at[s], wsems.at[1, s]),
                pltpu.make_async_copy(wd_hbm.at[e], wd_buf.at[s], wsems.at[2, s]))

    def run_copies(b, r, s):
        k = b * gpb + r
        length = run_len_ref[k]
        hbm_rows = pl.ds(run_src_ref[k], length)
        blk_rows = pl.ds(run_off_ref[k], length)
        return (pltpu.make_async_copy(xs_hbm.at[hbm_rows], xbuf.at[s, blk_rows], xsems.at[s]),
                pltpu.make_async_copy(obuf.at[s, blk_rows], ys_hbm.at[hbm_rows], osems.at[s]))

    def start_in(b, s, runs):
        lax.fori_loop(0, runs, lambda r, c: (run_copies(b, r, s)[0].start(), c)[1], 0)

    def start_out(b, s):
        lax.fori_loop(0, nrun_ref[b], lambda r, c: (run_copies(b, r, s)[1].start(), c)[1], 0)

    def wait_in(b, s):
        count = ngran_ref[b]
        pltpu.make_async_copy(xs_hbm.at[pl.ds(0, count)], xbuf.at[s, pl.ds(0, count)], xsems.at[s]).wait()

    def wait_out(b, s):
        count = ngran_ref[b]
        pltpu.make_async_copy(obuf.at[s, pl.ds(0, count)], ys_hbm.at[pl.ds(0, count)], osems.at[s]).wait()

    xbuf[...] = jnp.zeros_like(xbuf)
    for cp in fetch_weights(blk_e_ref[0], 0):
        cp.start()
    start_in(0, 0, nrun_ref[0])
    start_in(1, 1, jnp.where(n > 1, nrun_ref[1], 0))

    def block(i, carry):
        xs = lax.rem(i, X_SLOTS)
        os = i % 2
        ws = slot_ref[i]

        @pl.when(first_ref[i] == 1)
        def _():
            for cp in fetch_weights(blk_e_ref[i], ws):
                cp.wait()

            @pl.when(next_e_ref[i] >= 0)
            def _():
                for cp in fetch_weights(next_e_ref[i], 1 - ws):
                    cp.start()

        @pl.when(i >= 2)
        def _():
            wait_out(i - 2, os)

        ahead = jnp.minimum(i + 2, last)
        start_in(ahead, lax.rem(i + 2, X_SLOTS), jnp.where(i + 2 < n, nrun_ref[ahead], 0))

        wait_in(i, xs)

        def mlp(rows):
            granules = rows // GRANULE
            xl, xr = _unpack_halves(xbuf[xs, :granules].reshape(rows, HALF))
            x = jnp.concatenate([xl, xr], axis=1)
            a = _dot(x, wg_buf[ws].astype(BF16))
            b = _dot(x, wu_buf[ws].astype(BF16))
            hmid = (a * jax.nn.sigmoid(a) * b).astype(BF16)
            y = _dot(hmid, wd_buf[ws].astype(BF16))
            packed = _pack_halves(y[:, :HALF].astype(BF16).astype(F32), y[:, HALF:].astype(BF16).astype(F32))
            obuf[os, :granules] = packed.reshape(granules, GRANULE, HALF)

        pads = -(-ngran_ref[i] // (EXPERT_PAD // GRANULE))
        for k in range(1, EXPERT_ROWS // EXPERT_PAD + 1):
            pl.when(pads == k)(functools.partial(mlp, k * EXPERT_PAD))
        start_out(i, os)
        return carry

    lax.fori_loop(0, n, block, 0)

    @pl.when(n >= 2)
    def _():
        wait_out(n - 2, n % 2)
    wait_out(n - 1, (n - 1) % 2)


def _experts(plan_table, n_blk, xs_loc, w_gate, w_up, w_down):
    granules = xs_loc.reshape(-1, GRANULE, HALF)
    hbm = pl.BlockSpec(memory_space=pl.ANY)
    return pl.pallas_call(
        functools.partial(_experts_kernel, n_blk=n_blk),
        grid_spec=pltpu.PrefetchScalarGridSpec(
            num_scalar_prefetch=1,
            grid=(1,),
            in_specs=[hbm, hbm, hbm, hbm],
            out_specs=hbm,
            scratch_shapes=[
                pltpu.VMEM((2, D_MODEL, D_EXPERT), F32),
                pltpu.VMEM((2, D_MODEL, D_EXPERT), F32),
                pltpu.VMEM((2, D_EXPERT, D_MODEL), F32),
                pltpu.VMEM((X_SLOTS, EXPERT_ROWS // GRANULE, GRANULE, HALF), I32),
                pltpu.VMEM((2, EXPERT_ROWS // GRANULE, GRANULE, HALF), I32),
                pltpu.SemaphoreType.DMA((3, 2)),
                pltpu.SemaphoreType.DMA((X_SLOTS,)),
                pltpu.SemaphoreType.DMA((2,)),
            ],
        ),
        out_shape=jax.ShapeDtypeStruct(granules.shape, I32),
        input_output_aliases={1: 0},
        compiler_params=pltpu.CompilerParams(
            dimension_semantics=("arbitrary",), vmem_limit_bytes=VMEM_LIMIT),
        name="experts",
    )(plan_table, granules, w_gate, w_up, w_down).reshape(xs_loc.shape)


def _combine_kernel(x2_ref, pos_ref, gate_ref, g_ref, ys_ref, o_ref):
    n_tok = x2_ref.shape[0]
    r = lax.broadcasted_iota(I32, (LOCAL_ROWS, n_tok), 0)
    w_t = (jnp.where(r == pos_ref[0:1, :], gate_ref[0:1, :], 0.0)
           + jnp.where(r == pos_ref[1:2, :], gate_ref[1:2, :], 0.0)).astype(BF16)
    yl, yr = _unpack_halves(ys_ref[...])
    moe = jnp.concatenate([_dot_tn(w_t, yl), _dot_tn(w_t, yr)], axis=-1)
    o_ref[...] = _rms(x2_ref[...] + moe, g_ref[...])


def _combine(x2, pos_tk, gates_tk, g_final, ys_loc):
    t = x2.shape[0]
    return pl.pallas_call(
        _combine_kernel,
        grid=(t // TRUNK_ROWS,),
        in_specs=[
            pl.BlockSpec((TRUNK_ROWS, D_MODEL), lambda i: (i, 0)),
            pl.BlockSpec((8, TRUNK_ROWS), lambda i: (0, i)),
            pl.BlockSpec((8, TRUNK_ROWS), lambda i: (0, i)),
            pl.BlockSpec((1, D_MODEL), lambda i: (0, 0)),
            pl.BlockSpec((LOCAL_ROWS, HALF), lambda i: (i, 0)),
        ],
        out_specs=pl.BlockSpec((TRUNK_ROWS, D_MODEL), lambda i: (i, 0)),
        out_shape=jax.ShapeDtypeStruct((t, D_MODEL), F32),
        compiler_params=pltpu.CompilerParams(vmem_limit_bytes=VMEM_LIMIT),
        name="combine",
    )(x2, pos_tk, gates_tk, g_final, ys_loc)


def _router_params(w_rg, b_rg, w_re, b_re):
    w = jnp.zeros((ROUTER_ROWS, D_MODEL), F32)
    w = w.at[0:N_GROUPS].set(w_rg.T).at[8:8 + N_EXPERTS].set(w_re.T)
    b = jnp.zeros((ROUTER_ROWS,), F32)
    b = b.at[0:N_GROUPS].set(b_rg).at[N_GROUPS:8].set(NEG_BIG).at[8:8 + N_EXPERTS].set(b_re)
    return w.astype(BF16), b.reshape(ROUTER_ROWS, 1)


def _plan(cnt, max_padded_rows):
    n_tiles = cnt.shape[0]
    piece = (cnt + GRANULE - 1) // GRANULE * GRANULE
    lend = jnp.cumsum(piece, axis=1)
    lstart = lend - piece
    tot = jnp.sum(piece, axis=0)
    padded = (tot + EXPERT_PAD - 1) // EXPERT_PAD * EXPERT_PAD
    cum_tiles = jnp.cumsum(piece, axis=0)

    per_expert = (padded + EXPERT_ROWS - 1) // EXPERT_ROWS
    blk_end = jnp.cumsum(per_expert)
    pads_per_blk = EXPERT_ROWS // EXPERT_PAD
    n_blk = (max_padded_rows // EXPERT_PAD + N_EXPERTS * (pads_per_blk - 1)) // pads_per_blk + 1
    blk = jnp.arange(n_blk, dtype=I32)
    blk_e = jnp.minimum(jnp.sum((blk_end[None, :] <= blk[:, None]).astype(I32), axis=1), N_EXPERTS - 1)
    nblk = blk_end[-1:]
    valid = blk < nblk

    of_blk_e = blk_e[:, None] == jnp.arange(N_EXPERTS, dtype=I32)[None, :]
    pick = lambda table: jnp.sum(jnp.where(of_blk_e[:, None, :], table[None, :, :], 0), axis=2)
    pick1 = lambda vec: jnp.sum(jnp.where(of_blk_e, vec[None, :], 0), axis=1)
    seg_off = (blk - pick1(blk_end - per_expert)) * EXPERT_ROWS
    off = seg_off[:, None] + jnp.arange(EXPERT_ROWS // GRANULE, dtype=I32)[None, :] * GRANULE
    real = (off < pick1(tot)[:, None]) & valid[:, None]
    ngran = jnp.sum(real.astype(I32), axis=1)
    cum_b = pick(cum_tiles)
    tile_of = jnp.minimum(jnp.sum((cum_b[:, None, :] <= off[:, :, None]).astype(I32), axis=2), n_tiles - 1)
    base = (jnp.arange(n_tiles, dtype=I32) * LOCAL_ROWS)[None, :] + pick(lstart) - (cum_b - pick(piece))
    of_tile = tile_of[:, :, None] == jnp.arange(n_tiles, dtype=I32)
    row = jnp.sum(jnp.where(of_tile, base[:, None, :], 0), axis=2) + off
    gran = jnp.where(real, row, 0) // GRANULE
    g_idx = jnp.arange(EXPERT_ROWS // GRANULE, dtype=I32)
    follows = jnp.concatenate([jnp.zeros((n_blk, 1), bool), gran[:, 1:] == gran[:, :-1] + 1], axis=1)
    run_start = real & jnp.logical_not(follows)
    run_of = jnp.cumsum(run_start.astype(I32), axis=1) - 1
    in_run = (run_of[:, :, None] == g_idx[None, None, :]) & real[:, :, None]
    at_start = in_run & run_start[:, :, None]
    run_len = jnp.sum(in_run.astype(I32), axis=1)
    run_src = jnp.sum(jnp.where(at_start, gran[:, :, None], 0), axis=1)
    run_off = jnp.sum(jnp.where(at_start, g_idx[None, :, None], 0), axis=1)
    nrun = jnp.sum(run_start.astype(I32), axis=1)
    change =jnp.concatenate([jnp.ones((1,), bool), blk_e[1:] != blk_e[:-1]])
    slot = (jnp.cumsum(change.astype(I32)) - 1) % 2
    later = (blk_e[None, :] > blk_e[:, None]) & valid[None, :]
    next_e = jnp.min(jnp.where(later, blk_e[None, :], N_EXPERTS), axis=1)
    next_e = jnp.where(next_e == N_EXPERTS, -1, next_e)
    tables = dict(
        nblk=nblk, blk_e=blk_e, first=change & valid, next_e=next_e, slot=slot, nrun=nrun, ngran=ngran,
        run_src=run_src.reshape(-1), run_off=run_off.reshape(-1), run_len=run_len.reshape(-1),
    )
    order = ("nblk",) + _PLAN_PER_BLOCK + _PLAN_PER_RUN
    return jnp.concatenate([tables[k].astype(I32) for k in order]), n_blk


def _layer(x2d, mem2d, batch, seq, norm_mix_g, w_in, conv_w, conv_b, head_norm_g, w_out,
           norm_xa_g, norm_mem_g, w_q, w_kv, w_o, norm_ffn_g, w_rg, b_rg, w_re, b_re,
           w_gate, w_up, w_down, out_norm_g):
    t = x2d.shape[0]
    n_tiles = t // TRUNK_ROWS
    row = lambda v: v.reshape(1, -1)
    hg = head_norm_g.reshape(-1)
    gm = _group_mean_matrix()

    kv = _kv_proj(mem2d, row(norm_mem_g), w_kv)
    conv_n, uf = _mixer_in(x2d, row(norm_mix_g), w_in.astype(BF16), conv_w, row(conv_b),
                           row(hg[:CONV_CH]), gm, batch, seq)
    fft_n = _fourier(uf.reshape(batch, seq, FFT_CH), _fft_stage2_matrices(seq), _fft_channel_matrix(seq),
                     gm, row(hg[CONV_CH:]), batch, seq)
    w_r_t, b_r = _router_params(w_rg, b_rg, w_re, b_re)
    x2, xs_loc, pos, gates, cnt = _trunk(
        x2d, conv_n, fft_n, w_out.astype(BF16), row(norm_xa_g), w_q.astype(BF16), kv, w_o.astype(BF16),
        row(norm_ffn_g), w_r_t, b_r, _strict_upper(TRUNK_ROWS), _strict_lower(N_EXPERTS), seq)

    max_rows = n_tiles * LOCAL_ROWS + N_EXPERTS * (EXPERT_PAD - GRANULE)
    n_global_rows = -(-max_rows // EXPERT_PAD) * EXPERT_PAD
    plan_table, n_blk = _plan(cnt[:, :, 0], n_global_rows)
    ys_loc = _experts(plan_table, n_blk, xs_loc, w_gate, w_up, w_down)
    return _combine(x2, pos, gates, row(out_norm_g), ys_loc)


def kernel(x, mem, norm_mix_g, w_in, conv_w, conv_b, head_norm_g, w_out, norm_xa_g, norm_mem_g, w_q, w_kv,
           w_o, norm_ffn_g, w_route_group, b_route_group, w_route_expert, b_route_expert, w_gate, w_up,
           w_down, final_norm_g):
    batch, seq, _ = x.shape
    depth = norm_mix_g.shape[0]
    assert depth == 1, "the final norm is fused into the last layer's combine kernel"
    x2d = x.reshape(batch * seq, D_MODEL)
    mem2d = mem.reshape(batch * MEM_LEN, D_MODEL)
    l = 0
    out = _layer(x2d, mem2d, batch, seq, norm_mix_g[l], w_in[l], conv_w[l], conv_b[l], head_norm_g[l],
                 w_out[l], norm_xa_g[l], norm_mem_g[l], w_q[l], w_kv[l], w_o[l], norm_ffn_g[l],
                 w_route_group[l], b_route_group[l], w_route_expert[l], b_route_expert[l],
                 w_gate[l], w_up[l], w_down[l], final_norm_g)
    return out.reshape(batch, seq, D_MODEL)
```

```python
import functools
import math

import numpy as np
import jax
import jax.numpy as jnp
from jax import lax
from jax.experimental import pallas as pl
from jax.experimental.pallas import tpu as pltpu

F32 = jnp.float32
BF16 = jnp.bfloat16
I32 = jnp.int32

D_MODEL = 1024
HALF = D_MODEL // 2
HEAD_DIM = 64
CONV_CH = 512
FFT_CH = 512
IN_COLS = 3 * CONV_CH + FFT_CH
MEM_LEN = 256
XA_HEADS = 4
XA_HEAD_DIM = D_MODEL // XA_HEADS
N_GROUPS = 4
EXPERTS_PER_GROUP = 8
N_EXPERTS = 32
TOP_K = 2
D_EXPERT = 512
EPS = 1e-6

FFT_N1 = 16
FFT_N2 = 256
FFT_K1_PER_STEP = 4

LANES = 128
MXU_COLS = 256
GRANULE = 8
MIX_ROWS = 1024
TRUNK_ROWS = 512
LOCAL_ROWS = TOP_K * TRUNK_ROWS + N_EXPERTS * GRANULE
LOCAL_GRANULES = LOCAL_ROWS // GRANULE
EXPERT_PAD = 128
EXPERT_ROWS = 1024
X_SLOTS = 3
ROUTER_ROWS = 128
NEG_BIG = -1e30
HI16 = -65536

VMEM_LIMIT = 56 * 1024 * 1024


def _rms(x, g):
    return x * lax.rsqrt(jnp.mean(x * x, axis=-1, keepdims=True) + EPS) * g


def _dot(a, b):
    return jnp.dot(a, b, preferred_element_type=F32)


def _dot_nt(a, b):
    return lax.dot_general(a, b, (((1,), (1,)), ((), ())), preferred_element_type=F32)


def _dot_tn(a, b):
    return lax.dot_general(a, b, (((0,), (0,)), ((), ())), preferred_element_type=F32)


def _pack_halves(left_f32, right_f32):
    lb = lax.bitcast_convert_type(left_f32, I32)
    rb = lax.shift_right_logical(lax.bitcast_convert_type(right_f32, I32), jnp.int32(16))
    return lb | rb


def _unpack_halves(packed_i32):
    left = lax.bitcast_convert_type(packed_i32 & jnp.int32(HI16), F32)
    right = lax.bitcast_convert_type(lax.shift_left(packed_i32, jnp.int32(16)), F32)
    return left.astype(BF16), right.astype(BF16)


def _group_mean_matrix():
    g = np.kron(np.eye(MXU_COLS // HEAD_DIM), np.full((HEAD_DIM, HEAD_DIM), 1.0 / HEAD_DIM))
    return jnp.asarray(g, dtype=BF16)


def _head_mean_square(y, gm):
    sq = (y * y).astype(BF16)
    return jnp.concatenate([_dot(sq[:, c:c + MXU_COLS], gm) for c in range(0, y.shape[1], MXU_COLS)], axis=1)


def _fft_stage2_matrices(seq):
    k1 = np.arange(FFT_N1)[:, None, None]
    k2 = np.arange(FFT_N2)[None, :, None]
    s2 = np.arange(FFT_N2)[None, None, :]
    ang = 2.0 * np.pi * ((s2 * (k1 + FFT_N1 * k2)) % seq) / seq
    c, s = np.cos(ang), np.sin(ang)
    top = np.concatenate([c, s], axis=2)
    bot = np.concatenate([-s, c], axis=2)
    return jnp.asarray(np.concatenate([top, bot], axis=1), dtype=BF16)


def _fft_channel_matrix(seq):
    c = np.arange(HEAD_DIM)
    ang = 2.0 * np.pi * ((c[:, None] * c[None, :]) % HEAD_DIM) / HEAD_DIM
    scale = 1.0 / math.sqrt(seq * HEAD_DIM)
    eye = np.eye(MXU_COLS // HEAD_DIM)
    cs = np.concatenate([np.kron(eye, np.cos(ang)), np.kron(eye, np.sin(ang))], axis=0) * scale
    return jnp.asarray(cs, dtype=BF16)


def _strict_upper(n):
    return jnp.asarray(np.triu(np.ones((n, n)), k=1), dtype=BF16)


def _strict_lower(n):
    return jnp.asarray(np.tril(np.ones((n, n)), k=-1), dtype=BF16)


def _kv_kernel(mem_ref, g_ref, w_ref, o_ref):
    h = _rms(mem_ref[...], g_ref[...]).astype(BF16)
    o_ref[...] = _dot(h, w_ref[...].astype(BF16)).astype(BF16)


def _kv_proj(mem2d, g, w_kv):
    rows = mem2d.shape[0]
    cols = w_kv.shape[1]
    cb = 512
    return pl.pallas_call(
        _kv_kernel,
        grid=(cols // cb,),
        in_specs=[
            pl.BlockSpec((rows, D_MODEL), lambda j: (0, 0)),
            pl.BlockSpec((1, D_MODEL), lambda j: (0, 0)),
            pl.BlockSpec((D_MODEL, cb), lambda j: (0, j)),
        ],
        out_specs=pl.BlockSpec((rows, cb), lambda j: (0, j)),
        out_shape=jax.ShapeDtypeStruct((rows, cols), BF16),
        compiler_params=pltpu.CompilerParams(vmem_limit_bytes=VMEM_LIMIT),
        name="kv_proj",
    )(mem2d, g, w_kv)


def _mixer_in_kernel(x_ref, xp_ref, xn_ref, g_ref, w_ref, cw_ref, cb_ref, hg_ref, gm_ref,
                     conv_ref, uf_ref):
    i = pl.program_id(1)
    n_i = pl.num_programs(1)
    rows = x_ref.shape[0]
    g = g_ref[...]
    h = _rms(x_ref[...], g).astype(BF16)
    u_cv = _dot(h, w_ref[:, CONV_CH:3 * CONV_CH])
    cv = u_cv[:, :CONV_CH] * u_cv[:, CONV_CH:]

    hh = jnp.concatenate([_rms(xp_ref[...], g), _rms(xn_ref[...], g)], axis=0).astype(BF16)
    uh = _dot(hh, w_ref[:, CONV_CH:3 * CONV_CH])
    cvh = uh[:, :CONV_CH] * uh[:, CONV_CH:]
    cv_prev = cvh[7:8, :] * jnp.where(i == 0, 0.0, 1.0)
    cv_next = cvh[8:9, :] * jnp.where(i == n_i - 1, 0.0, 1.0)

    row = lax.broadcasted_iota(I32, cv.shape, 0)
    cv_up = jnp.where(row == 0, cv_prev, pltpu.roll(cv, 1, 0))
    cv_dn = jnp.where(row == rows - 1, cv_next, pltpu.roll(cv, rows - 1, 0))
    z = cw_ref[0:1, :] * cv_up + cw_ref[1:2, :] * cv + cw_ref[2:3, :] * cv_dn + cb_ref[...]
    uf_ref[...] = _dot(h, w_ref[:, 3 * CONV_CH:]).astype(BF16)
    y = _dot(h, w_ref[:, :CONV_CH]) * z
    ms = _head_mean_square(y, gm_ref[...])
    conv_ref[...] = (y * lax.rsqrt(ms + EPS) * hg_ref[...]).astype(BF16)


def _mixer_in(x2d, g, w_in, conv_w, conv_b, hg_conv, gm, batch, seq):
    n_i = seq // MIX_ROWS
    t = x2d.shape[0]
    r8 = MIX_ROWS // 8
    last8 = t // 8 - 1
    return pl.pallas_call(
        _mixer_in_kernel,
        grid=(batch, n_i),
        in_specs=[
            pl.BlockSpec((MIX_ROWS, D_MODEL), lambda b, i: (b * n_i + i, 0)),
            pl.BlockSpec((8, D_MODEL), lambda b, i: (jnp.maximum((b * n_i + i) * r8 - 1, 0), 0)),
            pl.BlockSpec((8, D_MODEL), lambda b, i: (jnp.minimum((b * n_i + i + 1) * r8, last8), 0)),
            pl.BlockSpec((1, D_MODEL), lambda b, i: (0, 0)),
            pl.BlockSpec((D_MODEL, IN_COLS), lambda b, i: (0, 0)),
            pl.BlockSpec((3, CONV_CH), lambda b, i: (0, 0)),
            pl.BlockSpec((1, CONV_CH), lambda b, i: (0, 0)),
            pl.BlockSpec((1, CONV_CH), lambda b, i: (0, 0)),
            pl.BlockSpec((MXU_COLS, MXU_COLS), lambda b, i: (0, 0)),
        ],
        out_specs=[
            pl.BlockSpec((MIX_ROWS, CONV_CH), lambda b, i: (b * n_i + i, 0)),
            pl.BlockSpec((MIX_ROWS, FFT_CH), lambda b, i: (b * n_i + i, 0)),
        ],
        out_shape=[
            jax.ShapeDtypeStruct((t, CONV_CH), BF16),
            jax.ShapeDtypeStruct((t, FFT_CH), BF16),
        ],
        compiler_params=pltpu.CompilerParams(vmem_limit_bytes=VMEM_LIMIT),
        name="mixer_in",
    )(x2d, x2d, x2d, g, w_in, conv_w, conv_b, hg_conv, gm)


_S1_ROWS = 16
_S1_LANES = 128


def _lincomb(terms):
    acc = None
    for coef, val in terms:
        if abs(coef) < 1e-12:
            continue
        if abs(coef - 1.0) < 1e-12:
            term, neg = val, False
        elif abs(coef + 1.0) < 1e-12:
            term, neg = val, True
        else:
            term, neg = coef * val, False
        if acc is None:
            acc = -term if neg else term
        else:
            acc = acc - term if neg else acc + term
    return acc


def _fft_stage1(x_ref, a_ref):
    half = FFT_N1 // 2
    cos = [[math.cos(2 * math.pi * ((k * j) % FFT_N1) / FFT_N1) for j in range(FFT_N1)] for k in range(FFT_N1)]
    sin = [[math.sin(2 * math.pi * ((k * j) % FFT_N1) / FFT_N1) for j in range(FFT_N1)] for k in range(FFT_N1)]

    def body(r, carry):
        r0 = pl.multiple_of(r * _S1_ROWS, _S1_ROWS)
        rows_re = pl.ds(r0, _S1_ROWS)
        rows_im = pl.ds(r0 + FFT_N2, _S1_ROWS)
        for lc in range(0, FFT_CH, _S1_LANES):
            lanes = slice(lc, lc + _S1_LANES)
            xs = [x_ref[j, rows_re, lanes].astype(F32) for j in range(FFT_N1)]
            ev = [None] + [xs[j] + xs[FFT_N1 - j] for j in range(1, half)]
            od = [None] + [xs[j] - xs[FFT_N1 - j] for j in range(1, half)]
            for k in range(half + 1):
                re = _lincomb([(1.0, xs[0]), (cos[k][half], xs[half])]
                              + [(cos[k][j], ev[j]) for j in range(1, half)])
                a_ref[k, rows_re, lanes] = re.astype(BF16)
                if k in (0, half):
                    zero = jnp.zeros_like(re).astype(BF16)
                    a_ref[k, rows_im, lanes] = zero
                else:
                    im = _lincomb([(-sin[k][j], od[j]) for j in range(1, half)])
                    a_ref[k, rows_im, lanes] = im.astype(BF16)
                    a_ref[FFT_N1 - k, rows_re, lanes] = re.astype(BF16)
                    a_ref[FFT_N1 - k, rows_im, lanes] = (-im).astype(BF16)
        return carry

    lax.fori_loop(0, FFT_N2 // _S1_ROWS, body, 0)


def _fourier_kernel(x_ref, m2_ref, cs_ref, gm_ref, hg_ref, o_ref, a_ref, y_ref):
    j = pl.program_id(1)

    @pl.when(j == 0)
    def _():
        _fft_stage1(x_ref, a_ref)

    ris = [_dot(m2_ref[kk], a_ref[j * FFT_K1_PER_STEP + kk]) for kk in range(FFT_K1_PER_STEP)]
    re = jnp.concatenate([ri[:FFT_N2] for ri in ris], axis=0).astype(BF16)
    im = jnp.concatenate([ri[FFT_N2:] for ri in ris], axis=0).astype(BF16)
    y = jnp.concatenate(
        [_dot(re[:, c:c + MXU_COLS], cs_ref[:MXU_COLS, :]) + _dot(im[:, c:c + MXU_COLS], cs_ref[MXU_COLS:, :])
         for c in range(0, FFT_CH, MXU_COLS)], axis=1)
    yn = y * lax.rsqrt(_head_mean_square(y, gm_ref[...]) + EPS) * hg_ref[...]
    for kk in range(FFT_K1_PER_STEP):
        k1 = j * FFT_K1_PER_STEP + kk
        for c in range(FFT_CH // LANES):
            y_ref[c, pl.ds(k1, FFT_N2, stride=FFT_N1), :] = yn[kk * FFT_N2:(kk + 1) * FFT_N2,
                                                               c * LANES:(c + 1) * LANES]

    @pl.when(j == pl.num_programs(1) - 1)
    def _():
        for c in range(FFT_CH // LANES):
            o_ref[:, c * LANES:(c + 1) * LANES] = y_ref[c].astype(BF16)


def _fourier(uf, m2, cs, gm, hg_fft, batch, seq):
    assert seq == FFT_N1 * FFT_N2
    x4 = uf.reshape(batch, FFT_N1, FFT_N2, FFT_CH)
    out = pl.pallas_call(
        _fourier_kernel,
        grid=(batch, FFT_N1 // FFT_K1_PER_STEP),
        in_specs=[
            pl.BlockSpec((None, FFT_N1, FFT_N2, FFT_CH), lambda b, j: (b, 0, 0, 0)),
            pl.BlockSpec((FFT_K1_PER_STEP, 2 * FFT_N2, 2 * FFT_N2), lambda b, j: (j, 0, 0)),
            pl.BlockSpec((2 * MXU_COLS, MXU_COLS), lambda b, j: (0, 0)),
            pl.BlockSpec((MXU_COLS, MXU_COLS), lambda b, j: (0, 0)),
            pl.BlockSpec((1, FFT_CH), lambda b, j: (0, 0)),
        ],
        out_specs=pl.BlockSpec((seq, FFT_CH), lambda b, j: (b, 0)),
        out_shape=jax.ShapeDtypeStruct((batch * seq, FFT_CH), BF16),
        scratch_shapes=[
            pltpu.VMEM((FFT_N1, 2 * FFT_N2, FFT_CH), BF16),
            pltpu.VMEM((FFT_CH // LANES, seq, LANES), F32),
        ],
        compiler_params=pltpu.CompilerParams(
            dimension_semantics=("arbitrary", "arbitrary"), vmem_limit_bytes=VMEM_LIMIT),
        name="fourier",
    )(x4, m2, cs, gm, hg_fft)
    return out


def _first_index_of_max(vals, vmax, row):
    return jnp.min(jnp.where(vals == vmax, row, vals.shape[0]), axis=0, keepdims=True)


def _route(lg):
    cols = lg.shape[1]
    row8 = lax.broadcasted_iota(I32, (EXPERTS_PER_GROUP, cols), 0)
    gl = lg[0:8, :]
    gmax = jnp.max(gl, axis=0, keepdims=True)
    g_w = 1.0 / jnp.sum(jnp.exp(gl - gmax), axis=0, keepdims=True)
    g_idx = _first_index_of_max(gl, gmax, row8)

    el = lg[8:16, :]
    for g in range(1, N_GROUPS):
        el = jnp.where(g_idx == g, lg[8 + 8 * g:16 + 8 * g, :], el)
    emax = jnp.max(el, axis=0, keepdims=True)
    ee = jnp.exp(el - emax)
    e_prob = ee / jnp.sum(ee, axis=0, keepdims=True)
    p1 = jnp.max(e_prob, axis=0, keepdims=True)
    i1 = _first_index_of_max(e_prob, p1, row8)
    rest = jnp.where(row8 == i1, -1.0, e_prob)
    p2 = jnp.max(rest, axis=0, keepdims=True)
    i2 = _first_index_of_max(rest, p2, row8)
    denom = p1 + p2
    e1 = g_idx * EXPERTS_PER_GROUP + i1
    e2 = g_idx * EXPERTS_PER_GROUP + i2
    return e1, e2, g_w * p1 / denom, g_w * p2 / denom


def _trunk_kernel(x_ref, conv_ref, fft_ref, wout_ref, gxa_ref, wq_ref, k_ref, v_ref, wo_ref,
                  gffn_ref, wr_ref, br_ref, tri_ref, ltri_ref,
                  x2_ref, xs_ref, pos_ref, gate_ref, cnt_ref, h3_s, lg_s):
    @pl.when(pl.program_id(0) == 0)
    def _():
        h3_s[...] = jnp.zeros_like(h3_s)
        lg_s[...] = jnp.zeros_like(lg_s)

    h3 = h3_s[...]
    lg = lg_s[...]
    n_tok = lg.shape[1]

    x1 = x_ref[...] + _dot(jnp.concatenate([conv_ref[...], fft_ref[...]], axis=1), wout_ref[...])

    e1, e2, gate1, gate2 = _route(lg)
    row32 = lax.broadcasted_iota(I32, (N_EXPERTS, n_tok), 0)
    hit1 = row32 == e1
    hit2 = row32 == e2
    onehot = jnp.where(hit1 | hit2, 1.0, 0.0)
    before = _dot(onehot.astype(BF16), tri_ref[...])
    cnt = jnp.sum(onehot, axis=1, keepdims=True).astype(I32)
    piece = jnp.left_shift(jnp.right_shift(cnt + (GRANULE - 1), 3), 3)
    piece_b = jnp.broadcast_to(piece.astype(F32), (N_EXPERTS, LANES)).astype(BF16)
    start = _dot(ltri_ref[...], piece_b)[:, 0:1]
    slot = before + start
    pos1 = jnp.sum(jnp.where(hit1, slot, 0.0), axis=0, keepdims=True).astype(I32)
    pos2 = jnp.sum(jnp.where(hit2, slot, 0.0), axis=0, keepdims=True).astype(I32)
    pos_ref[0:1, :] = pos1
    pos_ref[1:2, :] = pos2
    pos_ref[2:8, :] = jnp.zeros((6, n_tok), I32)
    gate_ref[0:1, :] = gate1
    gate_ref[1:2, :] = gate2
    gate_ref[2:8, :] = jnp.zeros((6, n_tok), F32)
    cnt_ref[...] = jnp.broadcast_to(cnt, (N_EXPERTS, LANES))

    h2 = _rms(x1, gxa_ref[...]).astype(BF16)
    q = _dot(h2, wq_ref[...]).astype(BF16)

    r = lax.broadcasted_iota(I32, (LOCAL_ROWS, n_tok), 0)
    perm = jnp.where((r == pos1) | (r == pos2), 1.0, 0.0).astype(BF16)

    outs = []
    for hd in range(XA_HEADS):
        cols = slice(hd * XA_HEAD_DIM, (hd + 1) * XA_HEAD_DIM)
        s = _dot_nt(q[:, cols], k_ref[:, cols]) * (XA_HEAD_DIM ** -0.5)
        s = s - jnp.max(s, axis=-1, keepdims=True)
        p = jnp.exp(s)
        p = p / jnp.sum(p, axis=-1, keepdims=True)
        outs.append(_dot(p.astype(BF16), v_ref[:, cols]).astype(BF16))
        if hd == 1:
            sorted_left = _dot(perm, h3[:, :HALF])

    o = jnp.concatenate(outs, axis=-1)
    x2 = x1 + _dot(o, wo_ref[...])
    x2_ref[...] = x2
    sorted_right = _dot(perm, h3[:, HALF:])
    h3_next = _rms(x2, gffn_ref[...]).astype(BF16)
    xs_ref[...] = _pack_halves(sorted_left, sorted_right)
    h3_s[...] = h3_next
    lg_s[...] = _dot_nt(wr_ref[...], h3_next) + br_ref[...]


def _trunk(x2d, conv_n, fft_n, w_out, g_xa, w_q, kv, w_o, g_ffn, w_r_t, b_r, tri, ltri, seq):
    t = x2d.shape[0]
    n_tiles = t // TRUNK_ROWS
    n_per_batch = seq // TRUNK_ROWS
    const = lambda i: (0, 0)
    dense = lambda i: jnp.minimum(i, n_tiles - 1)
    routed = lambda i: jnp.maximum(i - 1, 0)
    return pl.pallas_call(
        _trunk_kernel,
        grid=(n_tiles + 1,),
        in_specs=[
            pl.BlockSpec((TRUNK_ROWS, D_MODEL), lambda i: (dense(i), 0)),
            pl.BlockSpec((TRUNK_ROWS, CONV_CH), lambda i: (dense(i), 0)),
            pl.BlockSpec((TRUNK_ROWS, FFT_CH), lambda i: (dense(i), 0)),
            pl.BlockSpec((D_MODEL, D_MODEL), const),
            pl.BlockSpec((1, D_MODEL), const),
            pl.BlockSpec((D_MODEL, D_MODEL), const),
            pl.BlockSpec((MEM_LEN, D_MODEL), lambda i: (dense(i) // n_per_batch, 0)),
            pl.BlockSpec((MEM_LEN, D_MODEL), lambda i: (dense(i) // n_per_batch, 1)),
            pl.BlockSpec((D_MODEL, D_MODEL), const),
            pl.BlockSpec((1, D_MODEL), const),
            pl.BlockSpec((ROUTER_ROWS, D_MODEL), const),
            pl.BlockSpec((ROUTER_ROWS, 1), const),
            pl.BlockSpec((TRUNK_ROWS, TRUNK_ROWS), const),
            pl.BlockSpec((N_EXPERTS, N_EXPERTS), const),
        ],
        out_specs=[
            pl.BlockSpec((TRUNK_ROWS, D_MODEL), lambda i: (dense(i), 0)),
            pl.BlockSpec((LOCAL_ROWS, HALF), lambda i: (routed(i), 0)),
            pl.BlockSpec((8, TRUNK_ROWS), lambda i: (0, routed(i))),
            pl.BlockSpec((8, TRUNK_ROWS), lambda i: (0, routed(i))),
            pl.BlockSpec((None, N_EXPERTS, LANES), lambda i: (routed(i), 0, 0)),
        ],
        out_shape=[
            jax.ShapeDtypeStruct((t, D_MODEL), F32),
            jax.ShapeDtypeStruct((n_tiles * LOCAL_ROWS, HALF), I32),
            jax.ShapeDtypeStruct((8, t), I32),
            jax.ShapeDtypeStruct((8, t), F32),
            jax.ShapeDtypeStruct((n_tiles, N_EXPERTS, LANES), I32),
        ],
        scratch_shapes=[
            pltpu.VMEM((TRUNK_ROWS, D_MODEL), BF16),
            pltpu.VMEM((ROUTER_ROWS, TRUNK_ROWS), F32),
        ],
        compiler_params=pltpu.CompilerParams(
            dimension_semantics=("arbitrary",), vmem_limit_bytes=VMEM_LIMIT),
        name="trunk",
    )(x2d, conv_n, fft_n, w_out, g_xa, w_q, kv, kv, w_o, g_ffn, w_r_t, b_r, tri, ltri)


_PLAN_PER_BLOCK = ("blk_e", "first", "next_e", "slot", "nrun", "ngran")
_PLAN_PER_RUN = ("run_src", "run_off", "run_len")


def _plan_layout(n_blk):
    gpb = EXPERT_ROWS // GRANULE
    sizes = [("nblk", 1)] + [(k, n_blk) for k in _PLAN_PER_BLOCK] + [(k, n_blk * gpb) for k in _PLAN_PER_RUN]
    offsets, at = {}, 0
    for name, size in sizes:
        offsets[name] = at
        at += size
    return offsets


class _TableView:
    def __init__(self, ref, offset):
        self._ref, self._offset = ref, offset

    def __getitem__(self, i):
        return self._ref[self._offset + i]


def _experts_kernel(plan_ref, xs_hbm, wg_hbm, wu_hbm, wd_hbm, ys_hbm,
                    wg_buf, wu_buf, wd_buf, xbuf, obuf, wsems, xsems, osems, *, n_blk):
    layout = _plan_layout(n_blk)
    (blk_e_ref, first_ref, next_e_ref, slot_ref, nblk_ref, run_src_ref, run_off_ref, run_len_ref, nrun_ref,
     ngran_ref) = (_TableView(plan_ref, layout[k]) for k in (
         "blk_e", "first", "next_e", "slot", "nblk", "run_src", "run_off", "run_len", "nrun", "ngran"))
    _experts_body(blk_e_ref, first_ref, next_e_ref, slot_ref, nblk_ref,
                  run_src_ref, run_off_ref, run_len_ref, nrun_ref, ngran_ref, n_blk - 1,
                  xs_hbm, wg_hbm, wu_hbm, wd_hbm, ys_hbm,
                  wg_buf, wu_buf, wd_buf, xbuf, obuf, wsems, xsems, osems)


def _experts_body(blk_e_ref, first_ref, next_e_ref, slot_ref, nblk_ref,
                  run_src_ref, run_off_ref, run_len_ref, nrun_ref, ngran_ref, last,
                  xs_hbm, wg_hbm, wu_hbm, wd_hbm, ys_hbm,
                  wg_buf, wu_buf, wd_buf, xbuf, obuf, wsems, xsems, osems):
    n = nblk_ref[0]
    gpb = EXPERT_ROWS // GRANULE

    def fetch_weights(e, s):
        return (pltpu.make_async_copy(wg_hbm.at[e], wg_buf.at[s], wsems.at[0, s]),
                pltpu.make_async_copy(wu_hbm.at[e], wu_buf.at[s], wsems.at[1, s]),
                pltpu.make_async_copy(wd_hbm.at[e], wd_buf.at[s], wsems.at[2, s]))

    def run_copies(b, r, s):
        k = b * gpb + r
        length = run_len_ref[k]
        hbm_rows = pl.ds(run_src_ref[k], length)
        blk_rows = pl.ds(run_off_ref[k], length)
        return (pltpu.make_async_copy(xs_hbm.at[hbm_rows], xbuf.at[s, blk_rows], xsems.at[s]),
                pltpu.make_async_copy(obuf.at[s, blk_rows], ys_hbm.at[hbm_rows], osems.at[s]))

    def start_in(b, s, runs):
        lax.fori_loop(0, runs, lambda r, c: (run_copies(b, r, s)[0].start(), c)[1], 0)

    def start_out(b, s):
        lax.fori_loop(0, nrun_ref[b], lambda r, c: (run_copies(b, r, s)[1].start(), c)[1], 0)

    def wait_in(b, s):
        count = ngran_ref[b]
        pltpu.make_async_copy(xs_hbm.at[pl.ds(0, count)], xbuf.at[s, pl.ds(0, count)], xsems.at[s]).wait()

    def wait_out(b, s):
        count = ngran_ref[b]
        pltpu.make_async_copy(obuf.at[s, pl.ds(0, count)], ys_hbm.at[pl.ds(0, count)], osems.at[s]).wait()

    xbuf[...] = jnp.zeros_like(xbuf)
    for cp in fetch_weights(blk_e_ref[0], 0):
        cp.start()
    start_in(0, 0, nrun_ref[0])
    start_in(1, 1, jnp.where(n > 1, nrun_ref[1], 0))

    def block(i, carry):
        xs = lax.rem(i, X_SLOTS)
        os = i % 2
        ws = slot_ref[i]

        @pl.when(first_ref[i] == 1)
        def _():
            for cp in fetch_weights(blk_e_ref[i], ws):
                cp.wait()

            @pl.when(next_e_ref[i] >= 0)
            def _():
                for cp in fetch_weights(next_e_ref[i], 1 - ws):
                    cp.start()

        @pl.when(i >= 2)
        def _():
            wait_out(i - 2, os)

        ahead = jnp.minimum(i + 2, last)
        start_in(ahead, lax.rem(i + 2, X_SLOTS), jnp.where(i + 2 < n, nrun_ref[ahead], 0))

        wait_in(i, xs)

        def mlp(rows):
            granules = rows // GRANULE
            xl, xr = _unpack_halves(xbuf[xs, :granules].reshape(rows, HALF))
            x = jnp.concatenate([xl, xr], axis=1)
            a = _dot(x, wg_buf[ws].astype(BF16))
            b = _dot(x, wu_buf[ws].astype(BF16))
            hmid = (a * jax.nn.sigmoid(a) * b).astype(BF16)
            y = _dot(hmid, wd_buf[ws].astype(BF16))
            packed = _pack_halves(y[:, :HALF].astype(BF16).astype(F32), y[:, HALF:].astype(BF16).astype(F32))
            obuf[os, :granules] = packed.reshape(granules, GRANULE, HALF)

        pads = -(-ngran_ref[i] // (EXPERT_PAD // GRANULE))
        for k in range(1, EXPERT_ROWS // EXPERT_PAD + 1):
            pl.when(pads == k)(functools.partial(mlp, k * EXPERT_PAD))
        start_out(i, os)
        return carry

    lax.fori_loop(0, n, block, 0)

    @pl.when(n >= 2)
    def _():
        wait_out(n - 2, n % 2)
    wait_out(n - 1, (n - 1) % 2)


def _experts(plan_table, n_blk, xs_loc, w_gate, w_up, w_down):
    granules = xs_loc.reshape(-1, GRANULE, HALF)
    hbm = pl.BlockSpec(memory_space=pl.ANY)
    return pl.pallas_call(
        functools.partial(_experts_kernel, n_blk=n_blk),
        grid_spec=pltpu.PrefetchScalarGridSpec(
            num_scalar_prefetch=1,
            grid=(1,),
            in_specs=[hbm, hbm, hbm, hbm],
            out_specs=hbm,
            scratch_shapes=[
                pltpu.VMEM((2, D_MODEL, D_EXPERT), F32),
                pltpu.VMEM((2, D_MODEL, D_EXPERT), F32),
                pltpu.VMEM((2, D_EXPERT, D_MODEL), F32),
                pltpu.VMEM((X_SLOTS, EXPERT_ROWS // GRANULE, GRANULE, HALF), I32),
                pltpu.VMEM((2, EXPERT_ROWS // GRANULE, GRANULE, HALF), I32),
                pltpu.SemaphoreType.DMA((3, 2)),
                pltpu.SemaphoreType.DMA((X_SLOTS,)),
                pltpu.SemaphoreType.DMA((2,)),
            ],
        ),
        out_shape=jax.ShapeDtypeStruct(granules.shape, I32),
        input_output_aliases={1: 0},
        compiler_params=pltpu.CompilerParams(
            dimension_semantics=("arbitrary",), vmem_limit_bytes=VMEM_LIMIT),
        name="experts",
    )(plan_table, granules, w_gate, w_up, w_down).reshape(xs_loc.shape)


def _combine_kernel(x2_ref, pos_ref, gate_ref, g_ref, ys_ref, o_ref):
    n_tok = x2_ref.shape[0]
    r = lax.broadcasted_iota(I32, (LOCAL_ROWS, n_tok), 0)
    w_t = (jnp.where(r == pos_ref[0:1, :], gate_ref[0:1, :], 0.0)
           + jnp.where(r == pos_ref[1:2, :], gate_ref[1:2, :], 0.0)).astype(BF16)
    yl, yr = _unpack_halves(ys_ref[...])
    moe = jnp.concatenate([_dot_tn(w_t, yl), _dot_tn(w_t, yr)], axis=-1)
    o_ref[...] = _rms(x2_ref[...] + moe, g_ref[...])


def _combine(x2, pos_tk, gates_tk, g_final, ys_loc):
    t = x2.shape[0]
    return pl.pallas_call(
        _combine_kernel,
        grid=(t // TRUNK_ROWS,),
        in_specs=[
            pl.BlockSpec((TRUNK_ROWS, D_MODEL), lambda i: (i, 0)),
            pl.BlockSpec((8, TRUNK_ROWS), lambda i: (0, i)),
            pl.BlockSpec((8, TRUNK_ROWS), lambda i: (0, i)),
            pl.BlockSpec((1, D_MODEL), lambda i: (0, 0)),
            pl.BlockSpec((LOCAL_ROWS, HALF), lambda i: (i, 0)),
        ],
        out_specs=pl.BlockSpec((TRUNK_ROWS, D_MODEL), lambda i: (i, 0)),
        out_shape=jax.ShapeDtypeStruct((t, D_MODEL), F32),
        compiler_params=pltpu.CompilerParams(vmem_limit_bytes=VMEM_LIMIT),
        name="combine",
    )(x2, pos_tk, gates_tk, g_final, ys_loc)


def _router_params(w_rg, b_rg, w_re, b_re):
    w = jnp.zeros((ROUTER_ROWS, D_MODEL), F32)
    w = w.at[0:N_GROUPS].set(w_rg.T).at[8:8 + N_EXPERTS].set(w_re.T)
    b = jnp.zeros((ROUTER_ROWS,), F32)
    b = b.at[0:N_GROUPS].set(b_rg).at[N_GROUPS:8].set(NEG_BIG).at[8:8 + N_EXPERTS].set(b_re)
    return w.astype(BF16), b.reshape(ROUTER_ROWS, 1)


def _plan(cnt, max_padded_rows):
    n_tiles = cnt.shape[0]
    piece = (cnt + GRANULE - 1) // GRANULE * GRANULE
    lend = jnp.cumsum(piece, axis=1)
    lstart = lend - piece
    tot = jnp.sum(piece, axis=0)
    padded = (tot + EXPERT_PAD - 1) // EXPERT_PAD * EXPERT_PAD
    cum_tiles = jnp.cumsum(piece, axis=0)

    per_expert = (padded + EXPERT_ROWS - 1) // EXPERT_ROWS
    blk_end = jnp.cumsum(per_expert)
    pads_per_blk = EXPERT_ROWS // EXPERT_PAD
    n_blk = (max_padded_rows // EXPERT_PAD + N_EXPERTS * (pads_per_blk - 1)) // pads_per_blk + 1
    blk = jnp.arange(n_blk, dtype=I32)
    blk_e = jnp.minimum(jnp.sum((blk_end[None, :] <= blk[:, None]).astype(I32), axis=1), N_EXPERTS - 1)
    nblk = blk_end[-1:]
    valid = blk < nblk

    of_blk_e = blk_e[:, None] == jnp.arange(N_EXPERTS, dtype=I32)[None, :]
    pick = lambda table: jnp.sum(jnp.where(of_blk_e[:, None, :], table[None, :, :], 0), axis=2)
    pick1 = lambda vec: jnp.sum(jnp.where(of_blk_e, vec[None, :], 0), axis=1)
    seg_off = (blk - pick1(blk_end - per_expert)) * EXPERT_ROWS
    off = seg_off[:, None] + jnp.arange(EXPERT_ROWS // GRANULE, dtype=I32)[None, :] * GRANULE
    real = (off < pick1(tot)[:, None]) & valid[:, None]
    ngran = jnp.sum(real.astype(I32), axis=1)
    cum_b = pick(cum_tiles)
    tile_of = jnp.minimum(jnp.sum((cum_b[:, None, :] <= off[:, :, None]).astype(I32), axis=2), n_tiles - 1)
    base = (jnp.arange(n_tiles, dtype=I32) * LOCAL_ROWS)[None, :] + pick(lstart) - (cum_b - pick(piece))
    of_tile = tile_of[:, :, None] == jnp.arange(n_tiles, dtype=I32)
    row = jnp.sum(jnp.where(of_tile, base[:, None, :], 0), axis=2) + off
    gran = jnp.where(real, row, 0) // GRANULE
    g_idx = jnp.arange(EXPERT_ROWS // GRANULE, dtype=I32)
    follows = jnp.concatenate([jnp.zeros((n_blk, 1), bool), gran[:, 1:] == gran[:, :-1] + 1], axis=1)
    run_start = real & jnp.logical_not(follows)
    run_of = jnp.cumsum(run_start.astype(I32), axis=1) - 1
    in_run = (run_of[:, :, None] == g_idx[None, None, :]) & real[:, :, None]
    at_start = in_run & run_start[:, :, None]
    run_len = jnp.sum(in_run.astype(I32), axis=1)
    run_src = jnp.sum(jnp.where(at_start, gran[:, :, None], 0), axis=1)
    run_off = jnp.sum(jnp.where(at_start, g_idx[None, :, None], 0), axis=1)
    nrun = jnp.sum(run_start.astype(I32), axis=1)
    change =jnp.concatenate([jnp.ones((1,), bool), blk_e[1:] != blk_e[:-1]])
    slot = (jnp.cumsum(change.astype(I32)) - 1) % 2
    later = (blk_e[None, :] > blk_e[:, None]) & valid[None, :]
    next_e = jnp.min(jnp.where(later, blk_e[None, :], N_EXPERTS), axis=1)
    next_e = jnp.where(next_e == N_EXPERTS, -1, next_e)
    tables = dict(
        nblk=nblk, blk_e=blk_e, first=change & valid, next_e=next_e, slot=slot, nrun=nrun, ngran=ngran,
        run_src=run_src.reshape(-1), run_off=run_off.reshape(-1), run_len=run_len.reshape(-1),
    )
    order = ("nblk",) + _PLAN_PER_BLOCK + _PLAN_PER_RUN
    return jnp.concatenate([tables[k].astype(I32) for k in order]), n_blk


def _layer(x2d, mem2d, batch, seq, norm_mix_g, w_in, conv_w, conv_b, head_norm_g, w_out,
           norm_xa_g, norm_mem_g, w_q, w_kv, w_o, norm_ffn_g, w_rg, b_rg, w_re, b_re,
           w_gate, w_up, w_down, out_norm_g):
    t = x2d.shape[0]
    n_tiles = t // TRUNK_ROWS
    row = lambda v: v.reshape(1, -1)
    hg = head_norm_g.reshape(-1)
    gm = _group_mean_matrix()

    kv = _kv_proj(mem2d, row(norm_mem_g), w_kv)
    conv_n, uf = _mixer_in(x2d, row(norm_mix_g), w_in.astype(BF16), conv_w, row(conv_b),
                           row(hg[:CONV_CH]), gm, batch, seq)
    fft_n = _fourier(uf.reshape(batch, seq, FFT_CH), _fft_stage2_matrices(seq), _fft_channel_matrix(seq),
                     gm, row(hg[CONV_CH:]), batch, seq)
    w_r_t, b_r = _router_params(w_rg, b_rg, w_re, b_re)
    x2, xs_loc, pos, gates, cnt = _trunk(
        x2d, conv_n, fft_n, w_out.astype(BF16), row(norm_xa_g), w_q.astype(BF16), kv, w_o.astype(BF16),
        row(norm_ffn_g), w_r_t, b_r, _strict_upper(TRUNK_ROWS), _strict_lower(N_EXPERTS), seq)

    max_rows = n_tiles * LOCAL_ROWS + N_EXPERTS * (EXPERT_PAD - GRANULE)
    n_global_rows = -(-max_rows // EXPERT_PAD) * EXPERT_PAD
    plan_table, n_blk = _plan(cnt[:, :, 0], n_global_rows)
    ys_loc = _experts(plan_table, n_blk, xs_loc, w_gate, w_up, w_down)
    return _combine(x2, pos, gates, row(out_norm_g), ys_loc)


def kernel(x, mem, norm_mix_g, w_in, conv_w, conv_b, head_norm_g, w_out, norm_xa_g, norm_mem_g, w_q, w_kv,
           w_o, norm_ffn_g, w_route_group, b_route_group, w_route_expert, b_route_expert, w_gate, w_up,
           w_down, final_norm_g):
    batch, seq, _ = x.shape
    depth = norm_mix_g.shape[0]
    assert depth == 1, "the final norm is fused into the last layer's combine kernel"
    x2d = x.reshape(batch * seq, D_MODEL)
    mem2d = mem.reshape(batch * MEM_LEN, D_MODEL)
    l = 0
    out = _layer(x2d, mem2d, batch, seq, norm_mix_g[l], w_in[l], conv_w[l], conv_b[l], head_norm_g[l],
                 w_out[l], norm_xa_g[l], norm_mem_g[l], w_q[l], w_kv[l], w_o[l], norm_ffn_g[l],
                 w_route_group[l], b_route_group[l], w_route_expert[l], b_route_expert[l],
                 w_gate[l], w_up[l], w_down[l], final_norm_g)
    return out.reshape(batch, seq, D_MODEL)
```

```python
import functools
import math

import numpy as np
import jax
import jax.numpy as jnp
from jax import lax
from jax.experimental import pallas as pl
from jax.experimental.pallas import tpu as pltpu

F32 = jnp.float32
BF16 = jnp.bfloat16
I32 = jnp.int32

D_MODEL = 1024
HALF = D_MODEL // 2
HEAD_DIM = 64
CONV_CH = 512
FFT_CH = 512
IN_COLS = 3 * CONV_CH + FFT_CH
MEM_LEN = 256
XA_HEADS = 4
XA_HEAD_DIM = D_MODEL // XA_HEADS
N_GROUPS = 4
EXPERTS_PER_GROUP = 8
N_EXPERTS = 32
TOP_K = 2
D_EXPERT = 512
EPS = 1e-6

FFT_N1 = 16
FFT_N2 = 256
FFT_K1_PER_STEP = 8

LANES = 128
MXU_COLS = 256
GRANULE = 8
MIX_ROWS = 1024
TRUNK_ROWS = 512
LOCAL_ROWS = TOP_K * TRUNK_ROWS + N_EXPERTS * GRANULE
LOCAL_GRANULES = LOCAL_ROWS // GRANULE
EXPERT_PAD = 128
EXPERT_ROWS = 1024
X_SLOTS = 3
ROUTER_ROWS = 128
NEG_BIG = -1e30
HI16 = -65536

VMEM_LIMIT = 56 * 1024 * 1024


def _rms(x, g):
    return x * lax.rsqrt(jnp.mean(x * x, axis=-1, keepdims=True) + EPS) * g


def _dot(a, b):
    return jnp.dot(a, b, preferred_element_type=F32)


def _dot_nt(a, b):
    return lax.dot_general(a, b, (((1,), (1,)), ((), ())), preferred_element_type=F32)


def _dot_tn(a, b):
    return lax.dot_general(a, b, (((0,), (0,)), ((), ())), preferred_element_type=F32)


def _pack_halves(left_f32, right_f32):
    lb = lax.bitcast_convert_type(left_f32, I32)
    rb = lax.shift_right_logical(lax.bitcast_convert_type(right_f32, I32), jnp.int32(16))
    return lb | rb


def _unpack_halves(packed_i32):
    left = lax.bitcast_convert_type(packed_i32 & jnp.int32(HI16), F32)
    right = lax.bitcast_convert_type(lax.shift_left(packed_i32, jnp.int32(16)), F32)
    return left.astype(BF16), right.astype(BF16)


def _group_mean_matrix():
    g = np.kron(np.eye(MXU_COLS // HEAD_DIM), np.full((HEAD_DIM, HEAD_DIM), 1.0 / HEAD_DIM))
    return jnp.asarray(g, dtype=BF16)


def _head_mean_square(y, gm):
    sq = (y * y).astype(BF16)
    return jnp.concatenate([_dot(sq[:, c:c + MXU_COLS], gm) for c in range(0, y.shape[1], MXU_COLS)], axis=1)


def _fft_stage2_matrices(seq):
    k1 = np.arange(FFT_N1)[:, None, None]
    k2 = np.arange(FFT_N2)[None, :, None]
    s2 = np.arange(FFT_N2)[None, None, :]
    ang = 2.0 * np.pi * ((s2 * (k1 + FFT_N1 * k2)) % seq) / seq
    c, s = np.cos(ang), np.sin(ang)
    top = np.concatenate([c, s], axis=2)
    bot = np.concatenate([-s, c], axis=2)
    return jnp.asarray(np.concatenate([top, bot], axis=1), dtype=BF16)


def _fft_channel_matrix(seq):
    c = np.arange(HEAD_DIM)
    ang = 2.0 * np.pi * ((c[:, None] * c[None, :]) % HEAD_DIM) / HEAD_DIM
    scale = 1.0 / math.sqrt(seq * HEAD_DIM)
    eye = np.eye(MXU_COLS // HEAD_DIM)
    cs = np.concatenate([np.kron(eye, np.cos(ang)), np.kron(eye, np.sin(ang))], axis=0) * scale
    return jnp.asarray(cs, dtype=BF16)


def _strict_upper(n):
    return jnp.asarray(np.triu(np.ones((n, n)), k=1), dtype=BF16)


def _strict_lower(n):
    return jnp.asarray(np.tril(np.ones((n, n)), k=-1), dtype=BF16)


def _kv_kernel(mem_ref, g_ref, w_ref, o_ref):
    h = _rms(mem_ref[...], g_ref[...]).astype(BF16)
    o_ref[...] = _dot(h, w_ref[...].astype(BF16)).astype(BF16)


def _kv_proj(mem2d, g, w_kv):
    rows = mem2d.shape[0]
    cols = w_kv.shape[1]
    cb = 512
    return pl.pallas_call(
        _kv_kernel,
        grid=(cols // cb,),
        in_specs=[
            pl.BlockSpec((rows, D_MODEL), lambda j: (0, 0)),
            pl.BlockSpec((1, D_MODEL), lambda j: (0, 0)),
            pl.BlockSpec((D_MODEL, cb), lambda j: (0, j)),
        ],
        out_specs=pl.BlockSpec((rows, cb), lambda j: (0, j)),
        out_shape=jax.ShapeDtypeStruct((rows, cols), BF16),
        compiler_params=pltpu.CompilerParams(vmem_limit_bytes=VMEM_LIMIT),
        name="kv_proj",
    )(mem2d, g, w_kv)


def _mixer_in_kernel(x_ref, xp_ref, xn_ref, g_ref, w_ref, cw_ref, cb_ref, hg_ref, gm_ref,
                     conv_ref, uf_ref):
    i = pl.program_id(1)
    n_i = pl.num_programs(1)
    rows = x_ref.shape[0]
    g = g_ref[...]
    h = _rms(x_ref[...], g).astype(BF16)
    u_cv = _dot(h, w_ref[:, CONV_CH:3 * CONV_CH])
    cv = u_cv[:, :CONV_CH] * u_cv[:, CONV_CH:]

    hh = jnp.concatenate([_rms(xp_ref[...], g), _rms(xn_ref[...], g)], axis=0).astype(BF16)
    uh = _dot(hh, w_ref[:, CONV_CH:3 * CONV_CH])
    cvh = uh[:, :CONV_CH] * uh[:, CONV_CH:]
    cv_prev = cvh[7:8, :] * jnp.where(i == 0, 0.0, 1.0)
    cv_next = cvh[8:9, :] * jnp.where(i == n_i - 1, 0.0, 1.0)

    row = lax.broadcasted_iota(I32, cv.shape, 0)
    cv_up = jnp.where(row == 0, cv_prev, pltpu.roll(cv, 1, 0))
    cv_dn = jnp.where(row == rows - 1, cv_next, pltpu.roll(cv, rows - 1, 0))
    z = cw_ref[0:1, :] * cv_up + cw_ref[1:2, :] * cv + cw_ref[2:3, :] * cv_dn + cb_ref[...]
    uf_ref[...] = _dot(h, w_ref[:, 3 * CONV_CH:]).astype(BF16)
    y = _dot(h, w_ref[:, :CONV_CH]) * z
    ms = _head_mean_square(y, gm_ref[...])
    conv_ref[...] = (y * lax.rsqrt(ms + EPS) * hg_ref[...]).astype(BF16)


def _mixer_in(x2d, g, w_in, conv_w, conv_b, hg_conv, gm, batch, seq):
    n_i = seq // MIX_ROWS
    t = x2d.shape[0]
    r8 = MIX_ROWS // 8
    last8 = t // 8 - 1
    return pl.pallas_call(
        _mixer_in_kernel,
        grid=(batch, n_i),
        in_specs=[
            pl.BlockSpec((MIX_ROWS, D_MODEL), lambda b, i: (b * n_i + i, 0)),
            pl.BlockSpec((8, D_MODEL), lambda b, i: (jnp.maximum((b * n_i + i) * r8 - 1, 0), 0)),
            pl.BlockSpec((8, D_MODEL), lambda b, i: (jnp.minimum((b * n_i + i + 1) * r8, last8), 0)),
            pl.BlockSpec((1, D_MODEL), lambda b, i: (0, 0)),
            pl.BlockSpec((D_MODEL, IN_COLS), lambda b, i: (0, 0)),
            pl.BlockSpec((3, CONV_CH), lambda b, i: (0, 0)),
            pl.BlockSpec((1, CONV_CH), lambda b, i: (0, 0)),
            pl.BlockSpec((1, CONV_CH), lambda b, i: (0, 0)),
            pl.BlockSpec((MXU_COLS, MXU_COLS), lambda b, i: (0, 0)),
        ],
        out_specs=[
            pl.BlockSpec((MIX_ROWS, CONV_CH), lambda b, i: (b * n_i + i, 0)),
            pl.BlockSpec((MIX_ROWS, FFT_CH), lambda b, i: (b * n_i + i, 0)),
        ],
        out_shape=[
            jax.ShapeDtypeStruct((t, CONV_CH), BF16),
            jax.ShapeDtypeStruct((t, FFT_CH), BF16),
        ],
        compiler_params=pltpu.CompilerParams(vmem_limit_bytes=VMEM_LIMIT),
        name="mixer_in",
    )(x2d, x2d, x2d, g, w_in, conv_w, conv_b, hg_conv, gm)


_S1_ROWS = 16
_S1_LANES = 128


def _lincomb(terms):
    acc = None
    for coef, val in terms:
        if abs(coef) < 1e-12:
            continue
        if abs(coef - 1.0) < 1e-12:
            term, neg = val, False
        elif abs(coef + 1.0) < 1e-12:
            term, neg = val, True
        else:
            term, neg = coef * val, False
        if acc is None:
            acc = -term if neg else term
        else:
            acc = acc - term if neg else acc + term
    return acc


def _fft_stage1(x_ref, a_ref):
    half = FFT_N1 // 2
    cos = [[math.cos(2 * math.pi * ((k * j) % FFT_N1) / FFT_N1) for j in range(FFT_N1)] for k in range(FFT_N1)]
    sin = [[math.sin(2 * math.pi * ((k * j) % FFT_N1) / FFT_N1) for j in range(FFT_N1)] for k in range(FFT_N1)]

    def body(r, carry):
        r0 = pl.multiple_of(r * _S1_ROWS, _S1_ROWS)
        rows_re = pl.ds(r0, _S1_ROWS)
        rows_im = pl.ds(r0 + FFT_N2, _S1_ROWS)
        for lc in range(0, FFT_CH, _S1_LANES):
            lanes = slice(lc, lc + _S1_LANES)
            xs = [x_ref[j, rows_re, lanes].astype(F32) for j in range(FFT_N1)]
            ev = [None] + [xs[j] + xs[FFT_N1 - j] for j in range(1, half)]
            od = [None] + [xs[j] - xs[FFT_N1 - j] for j in range(1, half)]
            for k in range(half + 1):
                re = _lincomb([(1.0, xs[0]), (cos[k][half], xs[half])]
                              + [(cos[k][j], ev[j]) for j in range(1, half)])
                a_ref[k, rows_re, lanes] = re.astype(BF16)
                if k in (0, half):
                    zero = jnp.zeros_like(re).astype(BF16)
                    a_ref[k, rows_im, lanes] = zero
                else:
                    im = _lincomb([(-sin[k][j], od[j]) for j in range(1, half)])
                    a_ref[k, rows_im, lanes] = im.astype(BF16)
                    a_ref[FFT_N1 - k, rows_re, lanes] = re.astype(BF16)
                    a_ref[FFT_N1 - k, rows_im, lanes] = (-im).astype(BF16)
        return carry

    lax.fori_loop(0, FFT_N2 // _S1_ROWS, body, 0)


def _fourier_kernel(x_ref, m2_ref, cs_ref, gm_ref, hg_ref, o_ref, a_ref, y_ref):
    j = pl.program_id(1)

    @pl.when(j == 0)
    def _():
        _fft_stage1(x_ref, a_ref)

    ris = [_dot(m2_ref[kk], a_ref[j * FFT_K1_PER_STEP + kk]) for kk in range(FFT_K1_PER_STEP)]
    re = jnp.concatenate([ri[:FFT_N2] for ri in ris], axis=0).astype(BF16)
    im = jnp.concatenate([ri[FFT_N2:] for ri in ris], axis=0).astype(BF16)
    y = jnp.concatenate(
        [_dot(re[:, c:c + MXU_COLS], cs_ref[:MXU_COLS, :]) + _dot(im[:, c:c + MXU_COLS], cs_ref[MXU_COLS:, :])
         for c in range(0, FFT_CH, MXU_COLS)], axis=1)
    yn = y * lax.rsqrt(_head_mean_square(y, gm_ref[...]) + EPS) * hg_ref[...]
    for kk in range(FFT_K1_PER_STEP):
        k1 = j * FFT_K1_PER_STEP + kk
        for c in range(FFT_CH // LANES):
            y_ref[c, pl.ds(k1, FFT_N2, stride=FFT_N1), :] = yn[kk * FFT_N2:(kk + 1) * FFT_N2,
                                                               c * LANES:(c + 1) * LANES]

    @pl.when(j == pl.num_programs(1) - 1)
    def _():
        for c in range(FFT_CH // LANES):
            o_ref[:, c * LANES:(c + 1) * LANES] = y_ref[c].astype(BF16)


def _fourier(uf, m2, cs, gm, hg_fft, batch, seq):
    assert seq == FFT_N1 * FFT_N2
    x4 = uf.reshape(batch, FFT_N1, FFT_N2, FFT_CH)
    out = pl.pallas_call(
        _fourier_kernel,
        grid=(batch, FFT_N1 // FFT_K1_PER_STEP),
        in_specs=[
            pl.BlockSpec((None, FFT_N1, FFT_N2, FFT_CH), lambda b, j: (b, 0, 0, 0)),
            pl.BlockSpec((FFT_K1_PER_STEP, 2 * FFT_N2, 2 * FFT_N2), lambda b, j: (j, 0, 0)),
            pl.BlockSpec((2 * MXU_COLS, MXU_COLS), lambda b, j: (0, 0)),
            pl.BlockSpec((MXU_COLS, MXU_COLS), lambda b, j: (0, 0)),
            pl.BlockSpec((1, FFT_CH), lambda b, j: (0, 0)),
        ],
        out_specs=pl.BlockSpec((seq, FFT_CH), lambda b, j: (b, 0)),
        out_shape=jax.ShapeDtypeStruct((batch * seq, FFT_CH), BF16),
        scratch_shapes=[
            pltpu.VMEM((FFT_N1, 2 * FFT_N2, FFT_CH), BF16),
            pltpu.VMEM((FFT_CH // LANES, seq, LANES), F32),
        ],
        compiler_params=pltpu.CompilerParams(
            dimension_semantics=("arbitrary", "arbitrary"), vmem_limit_bytes=VMEM_LIMIT),
        name="fourier",
    )(x4, m2, cs, gm, hg_fft)
    return out


def _first_index_of_max(vals, vmax, row):
    return jnp.min(jnp.where(vals == vmax, row, vals.shape[0]), axis=0, keepdims=True)


def _route(lg):
    cols = lg.shape[1]
    row8 = lax.broadcasted_iota(I32, (EXPERTS_PER_GROUP, cols), 0)
    gl = lg[0:8, :]
    gmax = jnp.max(gl, axis=0, keepdims=True)
    g_w = 1.0 / jnp.sum(jnp.exp(gl - gmax), axis=0, keepdims=True)
    g_idx = _first_index_of_max(gl, gmax, row8)

    el = lg[8:16, :]
    for g in range(1, N_GROUPS):
        el = jnp.where(g_idx == g, lg[8 + 8 * g:16 + 8 * g, :], el)
    emax = jnp.max(el, axis=0, keepdims=True)
    ee = jnp.exp(el - emax)
    e_prob = ee / jnp.sum(ee, axis=0, keepdims=True)
    p1 = jnp.max(e_prob, axis=0, keepdims=True)
    i1 = _first_index_of_max(e_prob, p1, row8)
    rest = jnp.where(row8 == i1, -1.0, e_prob)
    p2 = jnp.max(rest, axis=0, keepdims=True)
    i2 = _first_index_of_max(rest, p2, row8)
    denom = p1 + p2
    e1 = g_idx * EXPERTS_PER_GROUP + i1
    e2 = g_idx * EXPERTS_PER_GROUP + i2
    return e1, e2, g_w * p1 / denom, g_w * p2 / denom


def _trunk_kernel(x_ref, conv_ref, fft_ref, wout_ref, gxa_ref, wq_ref, k_ref, v_ref, wo_ref,
                  gffn_ref, wr_ref, br_ref, tri_ref, ltri_ref,
                  x2_ref, xs_ref, pos_ref, gate_ref, cnt_ref, h3_s, lg_s):
    @pl.when(pl.program_id(0) == 0)
    def _():
        h3_s[...] = jnp.zeros_like(h3_s)
        lg_s[...] = jnp.zeros_like(lg_s)

    h3 = h3_s[...]
    lg = lg_s[...]
    n_tok = lg.shape[1]

    x1 = x_ref[...] + _dot(jnp.concatenate([conv_ref[...], fft_ref[...]], axis=1), wout_ref[...])

    e1, e2, gate1, gate2 = _route(lg)
    row32 = lax.broadcasted_iota(I32, (N_EXPERTS, n_tok), 0)
    hit1 = row32 == e1
    hit2 = row32 == e2
    onehot = jnp.where(hit1 | hit2, 1.0, 0.0)
    before = _dot(onehot.astype(BF16), tri_ref[...])
    cnt = jnp.sum(onehot, axis=1, keepdims=True).astype(I32)
    piece = jnp.left_shift(jnp.right_shift(cnt + (GRANULE - 1), 3), 3)
    piece_b = jnp.broadcast_to(piece.astype(F32), (N_EXPERTS, LANES)).astype(BF16)
    start = _dot(ltri_ref[...], piece_b)[:, 0:1]
    slot = before + start
    pos1 = jnp.sum(jnp.where(hit1, slot, 0.0), axis=0, keepdims=True).astype(I32)
    pos2 = jnp.sum(jnp.where(hit2, slot, 0.0), axis=0, keepdims=True).astype(I32)
    pos_ref[0:1, :] = pos1
    pos_ref[1:2, :] = pos2
    pos_ref[2:8, :] = jnp.zeros((6, n_tok), I32)
    gate_ref[0:1, :] = gate1
    gate_ref[1:2, :] = gate2
    gate_ref[2:8, :] = jnp.zeros((6, n_tok), F32)
    cnt_ref[...] = jnp.broadcast_to(cnt, (N_EXPERTS, LANES))

    h2 = _rms(x1, gxa_ref[...]).astype(BF16)
    q = _dot(h2, wq_ref[...]).astype(BF16)

    r = lax.broadcasted_iota(I32, (LOCAL_ROWS, n_tok), 0)
    perm = jnp.where((r == pos1) | (r == pos2), 1.0, 0.0).astype(BF16)

    outs = []
    for hd in range(XA_HEADS):
        cols = slice(hd * XA_HEAD_DIM, (hd + 1) * XA_HEAD_DIM)
        s = _dot_nt(q[:, cols], k_ref[:, cols]) * (XA_HEAD_DIM ** -0.5)
        s = s - jnp.max(s, axis=-1, keepdims=True)
        p = jnp.exp(s)
        p = p / jnp.sum(p, axis=-1, keepdims=True)
        outs.append(_dot(p.astype(BF16), v_ref[:, cols]).astype(BF16))
        if hd == 1:
            sorted_left = _dot(perm, h3[:, :HALF])

    o = jnp.concatenate(outs, axis=-1)
    x2 = x1 + _dot(o, wo_ref[...])
    x2_ref[...] = x2
    sorted_right = _dot(perm, h3[:, HALF:])
    h3_next = _rms(x2, gffn_ref[...]).astype(BF16)
    xs_ref[...] = _pack_halves(sorted_left, sorted_right)
    h3_s[...] = h3_next
    lg_s[...] = _dot_nt(wr_ref[...], h3_next) + br_ref[...]


def _trunk(x2d, conv_n, fft_n, w_out, g_xa, w_q, kv, w_o, g_ffn, w_r_t, b_r, tri, ltri, seq):
    t = x2d.shape[0]
    n_tiles = t // TRUNK_ROWS
    n_per_batch = seq // TRUNK_ROWS
    const = lambda i: (0, 0)
    dense = lambda i: jnp.minimum(i, n_tiles - 1)
    routed = lambda i: jnp.maximum(i - 1, 0)
    return pl.pallas_call(
        _trunk_kernel,
        grid=(n_tiles + 1,),
        in_specs=[
            pl.BlockSpec((TRUNK_ROWS, D_MODEL), lambda i: (dense(i), 0)),
            pl.BlockSpec((TRUNK_ROWS, CONV_CH), lambda i: (dense(i), 0)),
            pl.BlockSpec((TRUNK_ROWS, FFT_CH), lambda i: (dense(i), 0)),
            pl.BlockSpec((D_MODEL, D_MODEL), const),
            pl.BlockSpec((1, D_MODEL), const),
            pl.BlockSpec((D_MODEL, D_MODEL), const),
            pl.BlockSpec((MEM_LEN, D_MODEL), lambda i: (dense(i) // n_per_batch, 0)),
            pl.BlockSpec((MEM_LEN, D_MODEL), lambda i: (dense(i) // n_per_batch, 1)),
            pl.BlockSpec((D_MODEL, D_MODEL), const),
            pl.BlockSpec((1, D_MODEL), const),
            pl.BlockSpec((ROUTER_ROWS, D_MODEL), const),
            pl.BlockSpec((ROUTER_ROWS, 1), const),
            pl.BlockSpec((TRUNK_ROWS, TRUNK_ROWS), const),
            pl.BlockSpec((N_EXPERTS, N_EXPERTS), const),
        ],
        out_specs=[
            pl.BlockSpec((TRUNK_ROWS, D_MODEL), lambda i: (dense(i), 0)),
            pl.BlockSpec((LOCAL_ROWS, HALF), lambda i: (routed(i), 0)),
            pl.BlockSpec((8, TRUNK_ROWS), lambda i: (0, routed(i))),
            pl.BlockSpec((8, TRUNK_ROWS), lambda i: (0, routed(i))),
            pl.BlockSpec((None, N_EXPERTS, LANES), lambda i: (routed(i), 0, 0)),
        ],
        out_shape=[
            jax.ShapeDtypeStruct((t, D_MODEL), F32),
            jax.ShapeDtypeStruct((n_tiles * LOCAL_ROWS, HALF), I32),
            jax.ShapeDtypeStruct((8, t), I32),
            jax.ShapeDtypeStruct((8, t), F32),
            jax.ShapeDtypeStruct((n_tiles, N_EXPERTS, LANES), I32),
        ],
        scratch_shapes=[
            pltpu.VMEM((TRUNK_ROWS, D_MODEL), BF16),
            pltpu.VMEM((ROUTER_ROWS, TRUNK_ROWS), F32),
        ],
        compiler_params=pltpu.CompilerParams(
            dimension_semantics=("arbitrary",), vmem_limit_bytes=VMEM_LIMIT),
        name="trunk",
    )(x2d, conv_n, fft_n, w_out, g_xa, w_q, kv, kv, w_o, g_ffn, w_r_t, b_r, tri, ltri)


_PLAN_PER_BLOCK = ("blk_e", "first", "next_e", "slot", "nrun", "ngran")
_PLAN_PER_RUN = ("run_src", "run_off", "run_len")


def _plan_layout(n_blk):
    gpb = EXPERT_ROWS // GRANULE
    sizes = [("nblk", 1)] + [(k, n_blk) for k in _PLAN_PER_BLOCK] + [(k, n_blk * gpb) for k in _PLAN_PER_RUN]
    offsets, at = {}, 0
    for name, size in sizes:
        offsets[name] = at
        at += size
    return offsets


class _TableView:
    def __init__(self, ref, offset):
        self._ref, self._offset = ref, offset

    def __getitem__(self, i):
        return self._ref[self._offset + i]


def _experts_kernel(plan_ref, xs_hbm, wg_hbm, wu_hbm, wd_hbm, ys_hbm,
                    wg_buf, wu_buf, wd_buf, xbuf, obuf, wsems, xsems, osems, *, n_blk):
    layout = _plan_layout(n_blk)
    (blk_e_ref, first_ref, next_e_ref, slot_ref, nblk_ref, run_src_ref, run_off_ref, run_len_ref, nrun_ref,
     ngran_ref) = (_TableView(plan_ref, layout[k]) for k in (
         "blk_e", "first", "next_e", "slot", "nblk", "run_src", "run_off", "run_len", "nrun", "ngran"))
    _experts_body(blk_e_ref, first_ref, next_e_ref, slot_ref, nblk_ref,
                  run_src_ref, run_off_ref, run_len_ref, nrun_ref, ngran_ref, n_blk - 1,
                  xs_hbm, wg_hbm, wu_hbm, wd_hbm, ys_hbm,
                  wg_buf, wu_buf, wd_buf, xbuf, obuf, wsems, xsems, osems)


def _experts_body(blk_e_ref, first_ref, next_e_ref, slot_ref, nblk_ref,
                  run_src_ref, run_off_ref, run_len_ref, nrun_ref, ngran_ref, last,
                  xs_hbm, wg_hbm, wu_hbm, wd_hbm, ys_hbm,
                  wg_buf, wu_buf, wd_buf, xbuf, obuf, wsems, xsems, osems):
    n = nblk_ref[0]
    gpb = EXPERT_ROWS // GRANULE

    def fetch_weights(e, s):
        return (pltpu.make_async_copy(wg_hbm.at[e], wg_buf.at[s], wsems.at[0, s]),
                pltpu.make_async_copy(wu_hbm.at[e], wu_buf.at[s], wsems.at[1, s]),
                pltpu.make_async_copy(wd_hbm.at[e], wd_buf.at[s], wsems.at[2, s]))

    def run_copies(b, r, s):
        k = b * gpb + r
        length = run_len_ref[k]
        hbm_rows = pl.ds(run_src_ref[k], length)
        blk_rows = pl.ds(run_off_ref[k], length)
        return (pltpu.make_async_copy(xs_hbm.at[hbm_rows], xbuf.at[s, blk_rows], xsems.at[s]),
                pltpu.make_async_copy(obuf.at[s, blk_rows], ys_hbm.at[hbm_rows], osems.at[s]))

    def start_in(b, s, runs):
        lax.fori_loop(0, runs, lambda r, c: (run_copies(b, r, s)[0].start(), c)[1], 0)

    def start_out(b, s):
        lax.fori_loop(0, nrun_ref[b], lambda r, c: (run_copies(b, r, s)[1].start(), c)[1], 0)

    def wait_in(b, s):
        count = ngran_ref[b]
        pltpu.make_async_copy(xs_hbm.at[pl.ds(0, count)], xbuf.at[s, pl.ds(0, count)], xsems.at[s]).wait()

    def wait_out(b, s):
        count = ngran_ref[b]
        pltpu.make_async_copy(obuf.at[s, pl.ds(0, count)], ys_hbm.at[pl.ds(0, count)], osems.at[s]).wait()

    xbuf[...] = jnp.zeros_like(xbuf)
    for cp in fetch_weights(blk_e_ref[0], 0):
        cp.start()
    start_in(0, 0, nrun_ref[0])
    start_in(1, 1, jnp.where(n > 1, nrun_ref[1], 0))

    def block(i, carry):
        xs = lax.rem(i, X_SLOTS)
        os = i % 2
        ws = slot_ref[i]

        @pl.when(first_ref[i] == 1)
        def _():
            for cp in fetch_weights(blk_e_ref[i], ws):
                cp.wait()

            @pl.when(next_e_ref[i] >= 0)
            def _():
                for cp in fetch_weights(next_e_ref[i], 1 - ws):
                    cp.start()

        @pl.when(i >= 2)
        def _():
            wait_out(i - 2, os)

        ahead = jnp.minimum(i + 2, last)
        start_in(ahead, lax.rem(i + 2, X_SLOTS), jnp.where(i + 2 < n, nrun_ref[ahead], 0))

        wait_in(i, xs)

        def mlp(rows):
            granules = rows // GRANULE
            xl, xr = _unpack_halves(xbuf[xs, :granules].reshape(rows, HALF))
            x = jnp.concatenate([xl, xr], axis=1)
            a = _dot(x, wg_buf[ws].astype(BF16))
            b = _dot(x, wu_buf[ws].astype(BF16))
            hmid = (a * jax.nn.sigmoid(a) * b).astype(BF16)
            y = _dot(hmid, wd_buf[ws].astype(BF16))
            packed = _pack_halves(y[:, :HALF].astype(BF16).astype(F32), y[:, HALF:].astype(BF16).astype(F32))
            obuf[os, :granules] = packed.reshape(granules, GRANULE, HALF)

        pads = -(-ngran_ref[i] // (EXPERT_PAD // GRANULE))
        for k in range(1, EXPERT_ROWS // EXPERT_PAD + 1):
            pl.when(pads == k)(functools.partial(mlp, k * EXPERT_PAD))
        start_out(i, os)
        return carry

    lax.fori_loop(0, n, block, 0)

    @pl.when(n >= 2)
    def _():
        wait_out(n - 2, n % 2)
    wait_out(n - 1, (n - 1) % 2)


def _experts(plan_table, n_blk, xs_loc, w_gate, w_up, w_down):
    granules = xs_loc.reshape(-1, GRANULE, HALF)
    hbm = pl.BlockSpec(memory_space=pl.ANY)
    return pl.pallas_call(
        functools.partial(_experts_kernel, n_blk=n_blk),
        grid_spec=pltpu.PrefetchScalarGridSpec(
            num_scalar_prefetch=1,
            grid=(1,),
            in_specs=[hbm, hbm, hbm, hbm],
            out_specs=hbm,
            scratch_shapes=[
                pltpu.VMEM((2, D_MODEL, D_EXPERT), F32),
                pltpu.VMEM((2, D_MODEL, D_EXPERT), F32),
                pltpu.VMEM((2, D_EXPERT, D_MODEL), F32),
                pltpu.VMEM((X_SLOTS, EXPERT_ROWS // GRANULE, GRANULE, HALF), I32),
                pltpu.VMEM((2, EXPERT_ROWS // GRANULE, GRANULE, HALF), I32),
                pltpu.SemaphoreType.DMA((3, 2)),
                pltpu.SemaphoreType.DMA((X_SLOTS,)),
                pltpu.SemaphoreType.DMA((2,)),
            ],
        ),
        out_shape=jax.ShapeDtypeStruct(granules.shape, I32),
        input_output_aliases={1: 0},
        compiler_params=pltpu.CompilerParams(
            dimension_semantics=("arbitrary",), vmem_limit_bytes=VMEM_LIMIT),
        name="experts",
    )(plan_table, granules, w_gate, w_up, w_down).reshape(xs_loc.shape)


def _combine_kernel(x2_ref, pos_ref, gate_ref, g_ref, ys_ref, o_ref):
    n_tok = x2_ref.shape[0]
    r = lax.broadcasted_iota(I32, (LOCAL_ROWS, n_tok), 0)
    w_t = (jnp.where(r == pos_ref[0:1, :], gate_ref[0:1, :], 0.0)
           + jnp.where(r == pos_ref[1:2, :], gate_ref[1:2, :], 0.0)).astype(BF16)
    yl, yr = _unpack_halves(ys_ref[...])
    moe = jnp.concatenate([_dot_tn(w_t, yl), _dot_tn(w_t, yr)], axis=-1)
    o_ref[...] = _rms(x2_ref[...] + moe, g_ref[...])


def _combine(x2, pos_tk, gates_tk, g_final, ys_loc):
    t = x2.shape[0]
    return pl.pallas_call(
        _combine_kernel,
        grid=(t // TRUNK_ROWS,),
        in_specs=[
            pl.BlockSpec((TRUNK_ROWS, D_MODEL), lambda i: (i, 0)),
            pl.BlockSpec((8, TRUNK_ROWS), lambda i: (0, i)),
            pl.BlockSpec((8, TRUNK_ROWS), lambda i: (0, i)),
            pl.BlockSpec((1, D_MODEL), lambda i: (0, 0)),
            pl.BlockSpec((LOCAL_ROWS, HALF), lambda i: (i, 0)),
        ],
        out_specs=pl.BlockSpec((TRUNK_ROWS, D_MODEL), lambda i: (i, 0)),
        out_shape=jax.ShapeDtypeStruct((t, D_MODEL), F32),
        compiler_params=pltpu.CompilerParams(vmem_limit_bytes=VMEM_LIMIT),
        name="combine",
    )(x2, pos_tk, gates_tk, g_final, ys_loc)


def _router_params(w_rg, b_rg, w_re, b_re):
    w = jnp.zeros((ROUTER_ROWS, D_MODEL), F32)
    w = w.at[0:N_GROUPS].set(w_rg.T).at[8:8 + N_EXPERTS].set(w_re.T)
    b = jnp.zeros((ROUTER_ROWS,), F32)
    b = b.at[0:N_GROUPS].set(b_rg).at[N_GROUPS:8].set(NEG_BIG).at[8:8 + N_EXPERTS].set(b_re)
    return w.astype(BF16), b.reshape(ROUTER_ROWS, 1)


def _plan(cnt, max_padded_rows):
    n_tiles = cnt.shape[0]
    piece = (cnt + GRANULE - 1) // GRANULE * GRANULE
    lend = jnp.cumsum(piece, axis=1)
    lstart = lend - piece
    tot = jnp.sum(piece, axis=0)
    padded = (tot + EXPERT_PAD - 1) // EXPERT_PAD * EXPERT_PAD
    cum_tiles = jnp.cumsum(piece, axis=0)

    per_expert = (padded + EXPERT_ROWS - 1) // EXPERT_ROWS
    blk_end = jnp.cumsum(per_expert)
    pads_per_blk = EXPERT_ROWS // EXPERT_PAD
    n_blk = (max_padded_rows // EXPERT_PAD + N_EXPERTS * (pads_per_blk - 1)) // pads_per_blk + 1
    blk = jnp.arange(n_blk, dtype=I32)
    blk_e = jnp.minimum(jnp.sum((blk_end[None, :] <= blk[:, None]).astype(I32), axis=1), N_EXPERTS - 1)
    nblk = blk_end[-1:]
    valid = blk < nblk

    of_blk_e = blk_e[:, None] == jnp.arange(N_EXPERTS, dtype=I32)[None, :]
    pick = lambda table: jnp.sum(jnp.where(of_blk_e[:, None, :], table[None, :, :], 0), axis=2)
    pick1 = lambda vec: jnp.sum(jnp.where(of_blk_e, vec[None, :], 0), axis=1)
    seg_off = (blk - pick1(blk_end - per_expert)) * EXPERT_ROWS
    off = seg_off[:, None] + jnp.arange(EXPERT_ROWS // GRANULE, dtype=I32)[None, :] * GRANULE
    real = (off < pick1(tot)[:, None]) & valid[:, None]
    ngran = jnp.sum(real.astype(I32), axis=1)
    cum_b = pick(cum_tiles)
    tile_of = jnp.minimum(jnp.sum((cum_b[:, None, :] <= off[:, :, None]).astype(I32), axis=2), n_tiles - 1)
    base = (jnp.arange(n_tiles, dtype=I32) * LOCAL_ROWS)[None, :] + pick(lstart) - (cum_b - pick(piece))
    of_tile = tile_of[:, :, None] == jnp.arange(n_tiles, dtype=I32)
    row = jnp.sum(jnp.where(of_tile, base[:, None, :], 0), axis=2) + off
    gran = jnp.where(real, row, 0) // GRANULE
    g_idx = jnp.arange(EXPERT_ROWS // GRANULE, dtype=I32)
    follows = jnp.concatenate([jnp.zeros((n_blk, 1), bool), gran[:, 1:] == gran[:, :-1] + 1], axis=1)
    run_start = real & jnp.logical_not(follows)
    run_of = jnp.cumsum(run_start.astype(I32), axis=1) - 1
    in_run = (run_of[:, :, None] == g_idx[None, None, :]) & real[:, :, None]
    at_start = in_run & run_start[:, :, None]
    run_len = jnp.sum(in_run.astype(I32), axis=1)
    run_src = jnp.sum(jnp.where(at_start, gran[:, :, None], 0), axis=1)
    run_off = jnp.sum(jnp.where(at_start, g_idx[None, :, None], 0), axis=1)
    nrun = jnp.sum(run_start.astype(I32), axis=1)
    change =jnp.concatenate([jnp.ones((1,), bool), blk_e[1:] != blk_e[:-1]])
    slot = (jnp.cumsum(change.astype(I32)) - 1) % 2
    later = (blk_e[None, :] > blk_e[:, None]) & valid[None, :]
    next_e = jnp.min(jnp.where(later, blk_e[None, :], N_EXPERTS), axis=1)
    next_e = jnp.where(next_e == N_EXPERTS, -1, next_e)
    tables = dict(
        nblk=nblk, blk_e=blk_e, first=change & valid, next_e=next_e, slot=slot, nrun=nrun, ngran=ngran,
        run_src=run_src.reshape(-1), run_off=run_off.reshape(-1), run_len=run_len.reshape(-1),
    )
    order = ("nblk",) + _PLAN_PER_BLOCK + _PLAN_PER_RUN
    return jnp.concatenate([tables[k].astype(I32) for k in order]), n_blk


def _layer(x2d, mem2d, batch, seq, norm_mix_g, w_in, conv_w, conv_b, head_norm_g, w_out,
           norm_xa_g, norm_mem_g, w_q, w_kv, w_o, norm_ffn_g, w_rg, b_rg, w_re, b_re,
           w_gate, w_up, w_down, out_norm_g):
    t = x2d.shape[0]
    n_tiles = t // TRUNK_ROWS
    row = lambda v: v.reshape(1, -1)
    hg = head_norm_g.reshape(-1)
    gm = _group_mean_matrix()

    kv = _kv_proj(mem2d, row(norm_mem_g), w_kv)
    conv_n, uf = _mixer_in(x2d, row(norm_mix_g), w_in.astype(BF16), conv_w, row(conv_b),
                           row(hg[:CONV_CH]), gm, batch, seq)
    fft_n = _fourier(uf.reshape(batch, seq, FFT_CH), _fft_stage2_matrices(seq), _fft_channel_matrix(seq),
                     gm, row(hg[CONV_CH:]), batch, seq)
    w_r_t, b_r = _router_params(w_rg, b_rg, w_re, b_re)
    x2, xs_loc, pos, gates, cnt = _trunk(
        x2d, conv_n, fft_n, w_out.astype(BF16), row(norm_xa_g), w_q.astype(BF16), kv, w_o.astype(BF16),
        row(norm_ffn_g), w_r_t, b_r, _strict_upper(TRUNK_ROWS), _strict_lower(N_EXPERTS), seq)

    max_rows = n_tiles * LOCAL_ROWS + N_EXPERTS * (EXPERT_PAD - GRANULE)
    n_global_rows = -(-max_rows // EXPERT_PAD) * EXPERT_PAD
    plan_table, n_blk = _plan(cnt[:, :, 0], n_global_rows)
    ys_loc = _experts(plan_table, n_blk, xs_loc, w_gate, w_up, w_down)
    return _combine(x2, pos, gates, row(out_norm_g), ys_loc)


def kernel(x, mem, norm_mix_g, w_in, conv_w, conv_b, head_norm_g, w_out, norm_xa_g, norm_mem_g, w_q, w_kv,
           w_o, norm_ffn_g, w_route_group, b_route_group, w_route_expert, b_route_expert, w_gate, w_up,
           w_down, final_norm_g):
    batch, seq, _ = x.shape
    depth = norm_mix_g.shape[0]
    assert depth == 1, "the final norm is fused into the last layer's combine kernel"
    x2d = x.reshape(batch * seq, D_MODEL)
    mem2d = mem.reshape(batch * MEM_LEN, D_MODEL)
    l = 0
    out = _layer(x2d, mem2d, batch, seq, norm_mix_g[l], w_in[l], conv_w[l], conv_b[l], head_norm_g[l],
                 w_out[l], norm_xa_g[l], norm_mem_g[l], w_q[l], w_kv[l], w_o[l], norm_ffn_g[l],
                 w_route_group[l], b_route_group[l], w_route_expert[l], b_route_expert[l],
                 w_gate[l], w_up[l], w_down[l], final_norm_g)
    return out.reshape(batch, seq, D_MODEL)
```

```python
import functools
import math

import numpy as np
import jax
import jax.numpy as jnp
from jax import lax
from jax.experimental import pallas as pl
from jax.experimental.pallas import tpu as pltpu

F32 = jnp.float32
BF16 = jnp.bfloat16
I32 = jnp.int32

D_MODEL = 1024
HALF = D_MODEL // 2
HEAD_DIM = 64
CONV_CH = 512
FFT_CH = 512
IN_COLS = 3 * CONV_CH + FFT_CH
MEM_LEN = 256
XA_HEADS = 4
XA_HEAD_DIM = D_MODEL // XA_HEADS
N_GROUPS = 4
EXPERTS_PER_GROUP = 8
N_EXPERTS = 32
TOP_K = 2
D_EXPERT = 512
EPS = 1e-6

FFT_N1 = 16
FFT_N2 = 256
FFT_K1_PER_STEP = 4

LANES = 128
MXU_COLS = 256
SUBLANES = 8
GRANULE = SUBLANES
GRANULE_SHIFT = GRANULE.bit_length() - 1
MIX_ROWS = 1024
TRUNK_ROWS = 512
LOCAL_ROWS = TOP_K * TRUNK_ROWS + N_EXPERTS * GRANULE
COMBINE_TILES = 2
DENSE_TILES = 2
EXPERT_PAD = 128
EXPERT_ROWS = 1024
X_SLOTS = 3
ROUTER_ROWS = 128
EXPERT_LOGIT_ROW0 = SUBLANES
NEG_BIG = -1e30
HI16 = -65536

VMEM_LIMIT = 56 * 1024 * 1024


def _rms(x, g):
    return x * lax.rsqrt(jnp.mean(x * x, axis=-1, keepdims=True) + EPS) * g


def _dot(a, b):
    return jnp.dot(a, b, preferred_element_type=F32)


def _dot_nt(a, b):
    return lax.dot_general(a, b, (((1,), (1,)), ((), ())), preferred_element_type=F32)


def _dot_tn(a, b):
    return lax.dot_general(a, b, (((0,), (0,)), ((), ())), preferred_element_type=F32)


def _pack_halves(left_f32, right_f32):
    lb = lax.bitcast_convert_type(left_f32, I32)
    rb = lax.shift_right_logical(lax.bitcast_convert_type(right_f32, I32), jnp.int32(16))
    return lb | rb


def _unpack_halves(packed_i32):
    left = lax.bitcast_convert_type(packed_i32 & jnp.int32(HI16), F32)
    right = lax.bitcast_convert_type(lax.shift_left(packed_i32, jnp.int32(16)), F32)
    return left.astype(BF16), right.astype(BF16)


def _group_mean_matrix():
    g = np.kron(np.eye(MXU_COLS // HEAD_DIM), np.full((HEAD_DIM, HEAD_DIM), 1.0 / HEAD_DIM))
    return jnp.asarray(g, dtype=BF16)


def _head_mean_square(y, gm):
    sq = (y * y).astype(BF16)
    return jnp.concatenate([_dot(sq[:, c:c + MXU_COLS], gm) for c in range(0, y.shape[1], MXU_COLS)], axis=1)


def _fft_stage2_matrices(seq):
    k1 = np.arange(FFT_N1)[:, None, None]
    k2 = np.arange(FFT_N2)[None, :, None]
    s2 = np.arange(FFT_N2)[None, None, :]
    ang = 2.0 * np.pi * ((s2 * (k1 + FFT_N1 * k2)) % seq) / seq
    c, s = np.cos(ang), np.sin(ang)
    top = np.concatenate([c, s], axis=2)
    bot = np.concatenate([-s, c], axis=2)
    return jnp.asarray(np.concatenate([top, bot], axis=1), dtype=BF16)


def _fft_channel_matrix(seq):
    c = np.arange(HEAD_DIM)
    ang = 2.0 * np.pi * ((c[:, None] * c[None, :]) % HEAD_DIM) / HEAD_DIM
    scale = 1.0 / math.sqrt(seq * HEAD_DIM)
    eye = np.eye(MXU_COLS // HEAD_DIM)
    cs = np.concatenate([np.kron(eye, np.cos(ang)), np.kron(eye, np.sin(ang))], axis=0) * scale
    return jnp.asarray(cs, dtype=BF16)


def _strict_upper(n):
    return jnp.asarray(np.triu(np.ones((n, n)), k=1), dtype=BF16)


def _strict_lower(n):
    return jnp.asarray(np.tril(np.ones((n, n)), k=-1), dtype=BF16)


def _kv_kernel(mem_ref, g_ref, w_ref, o_ref):
    h = _rms(mem_ref[...], g_ref[...]).astype(BF16)
    o_ref[...] = _dot(h, w_ref[...].astype(BF16)).astype(BF16)


def _kv_proj(mem2d, g, w_kv):
    rows = mem2d.shape[0]
    cols = w_kv.shape[1]
    cb = 512
    return pl.pallas_call(
        _kv_kernel,
        grid=(cols // cb,),
        in_specs=[
            pl.BlockSpec((rows, D_MODEL), lambda j: (0, 0)),
            pl.BlockSpec((1, D_MODEL), lambda j: (0, 0)),
            pl.BlockSpec((D_MODEL, cb), lambda j: (0, j)),
        ],
        out_specs=pl.BlockSpec((rows, cb), lambda j: (0, j)),
        out_shape=jax.ShapeDtypeStruct((rows, cols), BF16),
        compiler_params=pltpu.CompilerParams(vmem_limit_bytes=VMEM_LIMIT),
        name="kv_proj",
    )(mem2d, g, w_kv)


def _mixer_in_kernel(x_ref, xp_ref, xn_ref, g_ref, w_ref, cw_ref, cb_ref, hg_ref, gm_ref,
                     conv_ref, uf_ref):
    i = pl.program_id(1)
    n_i = pl.num_programs(1)
    rows = x_ref.shape[0]
    g = g_ref[...]
    h = _rms(x_ref[...], g).astype(BF16)
    u_cv = _dot(h, w_ref[:, CONV_CH:3 * CONV_CH])
    cv = u_cv[:, :CONV_CH] * u_cv[:, CONV_CH:]

    hh = jnp.concatenate([_rms(xp_ref[...], g), _rms(xn_ref[...], g)], axis=0).astype(BF16)
    uh = _dot(hh, w_ref[:, CONV_CH:3 * CONV_CH])
    cvh = uh[:, :CONV_CH] * uh[:, CONV_CH:]
    cv_prev = cvh[SUBLANES - 1:SUBLANES, :] * jnp.where(i == 0, 0.0, 1.0)
    cv_next = cvh[SUBLANES:SUBLANES + 1, :] * jnp.where(i == n_i - 1, 0.0, 1.0)

    row = lax.broadcasted_iota(I32, cv.shape, 0)
    cv_up = jnp.where(row == 0, cv_prev, pltpu.roll(cv, 1, 0))
    cv_dn = jnp.where(row == rows - 1, cv_next, pltpu.roll(cv, rows - 1, 0))
    z = cw_ref[0:1, :] * cv_up + cw_ref[1:2, :] * cv + cw_ref[2:3, :] * cv_dn + cb_ref[...]
    uf_ref[...] = _dot(h, w_ref[:, 3 * CONV_CH:]).astype(BF16)
    y = _dot(h, w_ref[:, :CONV_CH]) * z
    ms = _head_mean_square(y, gm_ref[...])
    conv_ref[...] = (y * lax.rsqrt(ms + EPS) * hg_ref[...]).astype(BF16)


def _mixer_in(x2d, g, w_in, conv_w, conv_b, hg_conv, gm, batch, seq):
    n_i = seq // MIX_ROWS
    t = x2d.shape[0]
    r8 = MIX_ROWS // SUBLANES
    last8 = t // SUBLANES - 1
    return pl.pallas_call(
        _mixer_in_kernel,
        grid=(batch, n_i),
        in_specs=[
            pl.BlockSpec((MIX_ROWS, D_MODEL), lambda b, i: (b * n_i + i, 0)),
            pl.BlockSpec((SUBLANES, D_MODEL), lambda b, i: (jnp.maximum((b * n_i + i) * r8 - 1, 0), 0)),
            pl.BlockSpec((SUBLANES, D_MODEL), lambda b, i: (jnp.minimum((b * n_i + i + 1) * r8, last8), 0)),
            pl.BlockSpec((1, D_MODEL), lambda b, i: (0, 0)),
            pl.BlockSpec((D_MODEL, IN_COLS), lambda b, i: (0, 0)),
            pl.BlockSpec((3, CONV_CH), lambda b, i: (0, 0)),
            pl.BlockSpec((1, CONV_CH), lambda b, i: (0, 0)),
            pl.BlockSpec((1, CONV_CH), lambda b, i: (0, 0)),
            pl.BlockSpec((MXU_COLS, MXU_COLS), lambda b, i: (0, 0)),
        ],
        out_specs=[
            pl.BlockSpec((MIX_ROWS, CONV_CH), lambda b, i: (b * n_i + i, 0)),
            pl.BlockSpec((MIX_ROWS, FFT_CH), lambda b, i: (b * n_i + i, 0)),
        ],
        out_shape=[
            jax.ShapeDtypeStruct((t, CONV_CH), BF16),
            jax.ShapeDtypeStruct((t, FFT_CH), BF16),
        ],
        compiler_params=pltpu.CompilerParams(vmem_limit_bytes=VMEM_LIMIT),
        name="mixer_in",
    )(x2d, x2d, x2d, g, w_in, conv_w, conv_b, hg_conv, gm)


_S1_ROWS = 16
_S1_LANES = 128


def _lincomb(terms):
    acc = None
    for coef, val in terms:
        if abs(coef) < 1e-12:
            continue
        if abs(coef - 1.0) < 1e-12:
            term, neg = val, False
        elif abs(coef + 1.0) < 1e-12:
            term, neg = val, True
        else:
            term, neg = coef * val, False
        if acc is None:
            acc = -term if neg else term
        else:
            acc = acc - term if neg else acc + term
    return acc


def _fft_stage1(x_ref, a_ref):
    half = FFT_N1 // 2
    cos = [[math.cos(2 * math.pi * ((k * j) % FFT_N1) / FFT_N1) for j in range(FFT_N1)] for k in range(FFT_N1)]
    sin = [[math.sin(2 * math.pi * ((k * j) % FFT_N1) / FFT_N1) for j in range(FFT_N1)] for k in range(FFT_N1)]

    def body(r, carry):
        r0 = pl.multiple_of(r * _S1_ROWS, _S1_ROWS)
        rows_re = pl.ds(r0, _S1_ROWS)
        rows_im = pl.ds(r0 + FFT_N2, _S1_ROWS)
        for lc in range(0, FFT_CH, _S1_LANES):
            lanes = slice(lc, lc + _S1_LANES)
            xs = [x_ref[j, rows_re, lanes].astype(F32) for j in range(FFT_N1)]
            ev = [None] + [xs[j] + xs[FFT_N1 - j] for j in range(1, half)]
            od = [None] + [xs[j] - xs[FFT_N1 - j] for j in range(1, half)]
            for k in range(half + 1):
                re = _lincomb([(1.0, xs[0]), (cos[k][half], xs[half])]
                              + [(cos[k][j], ev[j]) for j in range(1, half)])
                a_ref[k, rows_re, lanes] = re.astype(BF16)
                if k in (0, half):
                    zero = jnp.zeros_like(re).astype(BF16)
                    a_ref[k, rows_im, lanes] = zero
                else:
                    im = _lincomb([(-sin[k][j], od[j]) for j in range(1, half)])
                    a_ref[k, rows_im, lanes] = im.astype(BF16)
                    a_ref[FFT_N1 - k, rows_re, lanes] = re.astype(BF16)
                    a_ref[FFT_N1 - k, rows_im, lanes] = (-im).astype(BF16)
        return carry

    lax.fori_loop(0, FFT_N2 // _S1_ROWS, body, 0)


def _fourier_kernel(x_ref, m2_ref, cs_ref, gm_ref, hg_ref, o_ref, a_ref, y_ref):
    j = pl.program_id(1)

    @pl.when(j == 0)
    def _():
        _fft_stage1(x_ref, a_ref)

    ris = [_dot(m2_ref[kk], a_ref[j * FFT_K1_PER_STEP + kk]) for kk in range(FFT_K1_PER_STEP)]
    re = jnp.concatenate([ri[:FFT_N2] for ri in ris], axis=0).astype(BF16)
    im = jnp.concatenate([ri[FFT_N2:] for ri in ris], axis=0).astype(BF16)
    y = jnp.concatenate(
        [_dot(re[:, c:c + MXU_COLS], cs_ref[:MXU_COLS, :]) + _dot(im[:, c:c + MXU_COLS], cs_ref[MXU_COLS:, :])
         for c in range(0, FFT_CH, MXU_COLS)], axis=1)
    yn = y * lax.rsqrt(_head_mean_square(y, gm_ref[...]) + EPS) * hg_ref[...]
    for kk in range(FFT_K1_PER_STEP):
        k1 = j * FFT_K1_PER_STEP + kk
        for c in range(FFT_CH // LANES):
            y_ref[c, pl.ds(k1, FFT_N2, stride=FFT_N1), :] = yn[kk * FFT_N2:(kk + 1) * FFT_N2,
                                                               c * LANES:(c + 1) * LANES]

    @pl.when(j == pl.num_programs(1) - 1)
    def _():
        for c in range(FFT_CH // LANES):
            o_ref[:, c * LANES:(c + 1) * LANES] = y_ref[c].astype(BF16)


def _fourier(uf, m2, cs, gm, hg_fft, batch, seq):
    assert seq == FFT_N1 * FFT_N2
    x4 = uf.reshape(batch, FFT_N1, FFT_N2, FFT_CH)
    out = pl.pallas_call(
        _fourier_kernel,
        grid=(batch, FFT_N1 // FFT_K1_PER_STEP),
        in_specs=[
            pl.BlockSpec((None, FFT_N1, FFT_N2, FFT_CH), lambda b, j: (b, 0, 0, 0)),
            pl.BlockSpec((FFT_K1_PER_STEP, 2 * FFT_N2, 2 * FFT_N2), lambda b, j: (j, 0, 0)),
            pl.BlockSpec((2 * MXU_COLS, MXU_COLS), lambda b, j: (0, 0)),
            pl.BlockSpec((MXU_COLS, MXU_COLS), lambda b, j: (0, 0)),
            pl.BlockSpec((1, FFT_CH), lambda b, j: (0, 0)),
        ],
        out_specs=pl.BlockSpec((seq, FFT_CH), lambda b, j: (b, 0)),
        out_shape=jax.ShapeDtypeStruct((batch * seq, FFT_CH), BF16),
        scratch_shapes=[
            pltpu.VMEM((FFT_N1, 2 * FFT_N2, FFT_CH), BF16),
            pltpu.VMEM((FFT_CH // LANES, seq, LANES), F32),
        ],
        compiler_params=pltpu.CompilerParams(
            dimension_semantics=("arbitrary", "arbitrary"), vmem_limit_bytes=VMEM_LIMIT),
        name="fourier",
    )(x4, m2, cs, gm, hg_fft)
    return out


def _first_index_of_max(vals, vmax, row):
    return jnp.min(jnp.where(vals == vmax, row, vals.shape[0]), axis=0, keepdims=True)


def _route(lg):
    cols = lg.shape[1]
    row8 = lax.broadcasted_iota(I32, (EXPERTS_PER_GROUP, cols), 0)
    gl = lg[0:EXPERT_LOGIT_ROW0, :]
    gmax = jnp.max(gl, axis=0, keepdims=True)
    g_w = 1.0 / jnp.sum(jnp.exp(gl - gmax), axis=0, keepdims=True)
    g_idx = _first_index_of_max(gl, gmax, row8)

    el = lg[EXPERT_LOGIT_ROW0:EXPERT_LOGIT_ROW0 + EXPERTS_PER_GROUP, :]
    for g in range(1, N_GROUPS):
        row0 = EXPERT_LOGIT_ROW0 + g * EXPERTS_PER_GROUP
        el = jnp.where(g_idx == g, lg[row0:row0 + EXPERTS_PER_GROUP, :], el)
    emax = jnp.max(el, axis=0, keepdims=True)
    ee = jnp.exp(el - emax)
    e_prob = ee / jnp.sum(ee, axis=0, keepdims=True)
    p1 = jnp.max(e_prob, axis=0, keepdims=True)
    i1 = _first_index_of_max(e_prob, p1, row8)
    rest = jnp.where(row8 == i1, -1.0, e_prob)
    p2 = jnp.max(rest, axis=0, keepdims=True)
    i2 = _first_index_of_max(rest, p2, row8)
    denom = p1 + p2
    e1 = g_idx * EXPERTS_PER_GROUP + i1
    e2 = g_idx * EXPERTS_PER_GROUP + i2
    return e1, e2, g_w * p1 / denom, g_w * p2 / denom


def _trunk_kernel(x_ref, conv_ref, fft_ref, wout_ref, gxa_ref, wq_ref, k_ref, v_ref, wo_ref,
                  gffn_ref, wr_ref, br_ref, tri_ref, ltri_ref,
                  x2_ref, xs_ref, pos_ref, gate_ref, cnt_ref, h3_s, lg_s):
    @pl.when(pl.program_id(0) == 0)
    def _():
        h3_s[...] = jnp.zeros_like(h3_s)
        lg_s[...] = jnp.zeros_like(lg_s)

    x1 = x_ref[...] + _dot(jnp.concatenate([conv_ref[...], fft_ref[...]], axis=1), wout_ref[...])

    def route_tile(tile):
        toks = slice(tile * TRUNK_ROWS, (tile + 1) * TRUNK_ROWS)
        e1, e2, gate1, gate2 = _route(lg_s[:, toks])
        row32 = lax.broadcasted_iota(I32, (N_EXPERTS, TRUNK_ROWS), 0)
        hit1 = row32 == e1
        hit2 = row32 == e2
        onehot = jnp.where(hit1 | hit2, 1.0, 0.0)
        before = _dot(onehot.astype(BF16), tri_ref[...])
        cnt = jnp.sum(onehot, axis=1, keepdims=True).astype(I32)
        piece = jnp.left_shift(jnp.right_shift(cnt + (GRANULE - 1), GRANULE_SHIFT), GRANULE_SHIFT)
        piece_b = jnp.broadcast_to(piece.astype(F32), (N_EXPERTS, LANES)).astype(BF16)
        start = _dot(ltri_ref[...], piece_b)[:, 0:1]
        slot = before + start
        pos1 = jnp.sum(jnp.where(hit1, slot, 0.0), axis=0, keepdims=True).astype(I32)
        pos2 = jnp.sum(jnp.where(hit2, slot, 0.0), axis=0, keepdims=True).astype(I32)
        pos_ref[0:1, toks] = pos1
        pos_ref[1:2, toks] = pos2
        pos_ref[TOP_K:, toks] = jnp.zeros((SUBLANES - TOP_K, TRUNK_ROWS), I32)
        gate_ref[0:1, toks] = gate1
        gate_ref[1:2, toks] = gate2
        gate_ref[TOP_K:, toks] = jnp.zeros((SUBLANES - TOP_K, TRUNK_ROWS), F32)
        cnt_ref[tile] = jnp.broadcast_to(cnt, (N_EXPERTS, LANES))
        return pos1, pos2

    def perm_of(pos):
        r = lax.broadcasted_iota(I32, (LOCAL_ROWS, TRUNK_ROWS), 0)
        return jnp.where((r == pos[0]) | (r == pos[1]), 1.0, 0.0).astype(BF16)

    def sort_half(tile, perm, lanes):
        return _dot(perm, h3_s[tile * TRUNK_ROWS:(tile + 1) * TRUNK_ROWS, lanes])

    left, right = slice(0, HALF), slice(HALF, D_MODEL)
    positions = [route_tile(tile) for tile in range(DENSE_TILES)]

    h2 = _rms(x1, gxa_ref[...]).astype(BF16)
    q = _dot(h2, wq_ref[...]).astype(BF16)

    perms = [perm_of(pos) for pos in positions]
    fillers = [(tile, lanes) for tile in range(DENSE_TILES) for lanes in (left, right)]
    sorted_rows = {}

    outs = []
    for hd in range(XA_HEADS):
        cols = slice(hd * XA_HEAD_DIM, (hd + 1) * XA_HEAD_DIM)
        s = _dot_nt(q[:, cols], k_ref[:, cols]) * (XA_HEAD_DIM ** -0.5)
        s = s - jnp.max(s, axis=-1, keepdims=True)
        p = jnp.exp(s)
        p = p / jnp.sum(p, axis=-1, keepdims=True)
        outs.append(_dot(p.astype(BF16), v_ref[:, cols]).astype(BF16))
        if hd >= 1 and len(fillers) > 1:
            tile, lanes = fillers.pop(0)
            sorted_rows[tile, lanes.start] = sort_half(tile, perms[tile], lanes)

    o = jnp.concatenate(outs, axis=-1)
    x2 = x1 + _dot(o, wo_ref[...])
    x2_ref[...] = x2
    for tile, lanes in fillers:
        sorted_rows[tile, lanes.start] = sort_half(tile, perms[tile], lanes)
    h3_next = _rms(x2, gffn_ref[...]).astype(BF16)
    for tile in range(DENSE_TILES):
        xs_ref[tile * LOCAL_ROWS:(tile + 1) * LOCAL_ROWS, :] = _pack_halves(sorted_rows[tile, 0],
                                                                            sorted_rows[tile, HALF])
    h3_s[...] = h3_next
    lg_s[...] = _dot_nt(wr_ref[...], h3_next) + br_ref[...]


def _trunk(x2d, conv_n, fft_n, w_out, g_xa, w_q, kv, w_o, g_ffn, w_r_t, b_r, tri, ltri, seq):
    t = x2d.shape[0]
    n_tiles = t // TRUNK_ROWS
    rows = DENSE_TILES * TRUNK_ROWS
    n_dense = t // rows
    n_per_batch = seq // rows
    const = lambda i: (0, 0)
    dense = lambda i: jnp.minimum(i, n_dense - 1)
    routed = lambda i: jnp.maximum(i - 1, 0)
    return pl.pallas_call(
        _trunk_kernel,
        grid=(n_dense + 1,),
        in_specs=[
            pl.BlockSpec((rows, D_MODEL), lambda i: (dense(i), 0)),
            pl.BlockSpec((rows, CONV_CH), lambda i: (dense(i), 0)),
            pl.BlockSpec((rows, FFT_CH), lambda i: (dense(i), 0)),
            pl.BlockSpec((D_MODEL, D_MODEL), const),
            pl.BlockSpec((1, D_MODEL), const),
            pl.BlockSpec((D_MODEL, D_MODEL), const),
            pl.BlockSpec((MEM_LEN, D_MODEL), lambda i: (dense(i) // n_per_batch, 0)),
            pl.BlockSpec((MEM_LEN, D_MODEL), lambda i: (dense(i) // n_per_batch, 1)),
            pl.BlockSpec((D_MODEL, D_MODEL), const),
            pl.BlockSpec((1, D_MODEL), const),
            pl.BlockSpec((ROUTER_ROWS, D_MODEL), const),
            pl.BlockSpec((ROUTER_ROWS, 1), const),
            pl.BlockSpec((TRUNK_ROWS, TRUNK_ROWS), const),
            pl.BlockSpec((N_EXPERTS, N_EXPERTS), const),
        ],
        out_specs=[
            pl.BlockSpec((rows, D_MODEL), lambda i: (dense(i), 0)),
            pl.BlockSpec((DENSE_TILES * LOCAL_ROWS, HALF), lambda i: (routed(i), 0)),
            pl.BlockSpec((SUBLANES, rows), lambda i: (0, routed(i))),
            pl.BlockSpec((SUBLANES, rows), lambda i: (0, routed(i))),
            pl.BlockSpec((DENSE_TILES, N_EXPERTS, LANES), lambda i: (routed(i), 0, 0)),
        ],
        out_shape=[
            jax.ShapeDtypeStruct((t, D_MODEL), F32),
            jax.ShapeDtypeStruct((n_tiles * LOCAL_ROWS, HALF), I32),
            jax.ShapeDtypeStruct((SUBLANES, t), I32),
            jax.ShapeDtypeStruct((SUBLANES, t), F32),
            jax.ShapeDtypeStruct((n_tiles, N_EXPERTS, LANES), I32),
        ],
        scratch_shapes=[
            pltpu.VMEM((rows, D_MODEL), BF16),
            pltpu.VMEM((ROUTER_ROWS, rows), F32),
        ],
        compiler_params=pltpu.CompilerParams(
            dimension_semantics=("arbitrary",), vmem_limit_bytes=VMEM_LIMIT),
        name="trunk",
    )(x2d, conv_n, fft_n, w_out, g_xa, w_q, kv, kv, w_o, g_ffn, w_r_t, b_r, tri, ltri)


_PLAN_PER_BLOCK = ("blk_e", "first", "next_e", "slot", "nrun", "ngran")
_PLAN_PER_RUN = ("run_src", "run_off", "run_len")


def _plan_layout(n_blk, max_runs):
    sizes = [("nblk", 1)] + [(k, n_blk) for k in _PLAN_PER_BLOCK] + [(k, n_blk * max_runs) for k in _PLAN_PER_RUN]
    offsets, at = {}, 0
    for name, size in sizes:
        offsets[name] = at
        at += size
    return offsets


class _TableView:
    def __init__(self, ref, offset):
        self._ref, self._offset = ref, offset

    def __getitem__(self, i):
        return self._ref[self._offset + i]


def _experts_kernel(plan_ref, xs_hbm, wg_hbm, wu_hbm, wd_hbm, ys_hbm,
                    wg_buf, wu_buf, wd_buf, xbuf, obuf, wsems, xsems, osems, *, n_blk, max_runs):
    layout = _plan_layout(n_blk, max_runs)
    (blk_e_ref, first_ref, next_e_ref, slot_ref, nblk_ref, run_src_ref, run_off_ref, run_len_ref, nrun_ref,
     ngran_ref) = (_TableView(plan_ref, layout[k]) for k in (
         "blk_e", "first", "next_e", "slot", "nblk", "run_src", "run_off", "run_len", "nrun", "ngran"))
    _experts_body(blk_e_ref, first_ref, next_e_ref, slot_ref, nblk_ref,
                  run_src_ref, run_off_ref, run_len_ref, nrun_ref, ngran_ref, n_blk - 1, max_runs,
                  xs_hbm, wg_hbm, wu_hbm, wd_hbm, ys_hbm,
                  wg_buf, wu_buf, wd_buf, xbuf, obuf, wsems, xsems, osems)


def _experts_body(blk_e_ref, first_ref, next_e_ref, slot_ref, nblk_ref,
                  run_src_ref, run_off_ref, run_len_ref, nrun_ref, ngran_ref, last, max_runs,
                  xs_hbm, wg_hbm, wu_hbm, wd_hbm, ys_hbm,
                  wg_buf, wu_buf, wd_buf, xbuf, obuf, wsems, xsems, osems):
    n = nblk_ref[0]
    gpb = EXPERT_ROWS // GRANULE

    def fetch_weights(e, s):
        return (pltpu.make_async_copy(wg_hbm.at[e], wg_buf.at[s], wsems.at[0, s]),
                pltpu.make_async_copy(wu_hbm.at[e], wu_buf.at[s], wsems.at[1, s]),
                pltpu.make_async_copy(wd_hbm.at[e], wd_buf.at[s], wsems.at[2, s]))

    def run_copies(b, r, s):
        k = b * max_runs + r
        length = run_len_ref[k]
        hbm_rows = pl.ds(run_src_ref[k], length)
        blk_rows = pl.ds(run_off_ref[k], length)
        return (pltpu.make_async_copy(xs_hbm.at[hbm_rows], xbuf.at[s, blk_rows], xsems.at[s]),
                pltpu.make_async_copy(obuf.at[s, blk_rows], ys_hbm.at[hbm_rows], osems.at[s]))

    def start_in(b, s, runs):
        lax.fori_loop(0, runs, lambda r, c: (run_copies(b, r, s)[0].start(), c)[1], 0)

    def start_out(b, s):
        lax.fori_loop(0, nrun_ref[b], lambda r, c: (run_copies(b, r, s)[1].start(), c)[1], 0)

    def wait_in(b, s):
        count = ngran_ref[b]
        pltpu.make_async_copy(xs_hbm.at[pl.ds(0, count)], xbuf.at[s, pl.ds(0, count)], xsems.at[s]).wait()

    def wait_out(b, s):
        count = ngran_ref[b]
        pltpu.make_async_copy(obuf.at[s, pl.ds(0, count)], ys_hbm.at[pl.ds(0, count)], osems.at[s]).wait()

    xbuf[...] = jnp.zeros_like(xbuf)
    for cp in fetch_weights(blk_e_ref[0], 0):
        cp.start()
    start_in(0, 0, nrun_ref[0])
    start_in(1, 1, jnp.where(n > 1, nrun_ref[1], 0))

    def block(i, carry):
        xs = lax.rem(i, X_SLOTS)
        os = i % 2
        ws = slot_ref[i]

        @pl.when(first_ref[i] == 1)
        def _():
            for cp in fetch_weights(blk_e_ref[i], ws):
                cp.wait()

            @pl.when(next_e_ref[i] >= 0)
            def _():
                for cp in fetch_weights(next_e_ref[i], 1 - ws):
                    cp.start()

        @pl.when(i >= 2)
        def _():
            wait_out(i - 2, os)

        ahead = jnp.minimum(i + 2, last)
        start_in(ahead, lax.rem(i + 2, X_SLOTS), jnp.where(i + 2 < n, nrun_ref[ahead], 0))

        wait_in(i, xs)

        def mlp(rows):
            granules = rows // GRANULE
            xl, xr = _unpack_halves(xbuf[xs, :granules].reshape(rows, HALF))
            x = jnp.concatenate([xl, xr], axis=1)
            a = _dot(x, wg_buf[ws].astype(BF16))
            b = _dot(x, wu_buf[ws].astype(BF16))
            hmid = (a * jax.nn.sigmoid(a) * b).astype(BF16)
            y = _dot(hmid, wd_buf[ws].astype(BF16))
            packed = _pack_halves(y[:, :HALF].astype(BF16).astype(F32), y[:, HALF:].astype(BF16).astype(F32))
            obuf[os, :granules] = packed.reshape(granules, GRANULE, HALF)

        pads = -(-ngran_ref[i] // (EXPERT_PAD // GRANULE))
        for k in range(1, EXPERT_ROWS // EXPERT_PAD + 1):
            pl.when(pads == k)(functools.partial(mlp, k * EXPERT_PAD))
        start_out(i, os)
        return carry

    lax.fori_loop(0, n, block, 0)

    @pl.when(n >= 2)
    def _():
        wait_out(n - 2, n % 2)
    wait_out(n - 1, (n - 1) % 2)


def _experts(plan_table, n_blk, max_runs, xs_loc, w_gate, w_up, w_down):
    granules = xs_loc.reshape(-1, GRANULE, HALF)
    hbm = pl.BlockSpec(memory_space=pl.ANY)
    return pl.pallas_call(
        functools.partial(_experts_kernel, n_blk=n_blk, max_runs=max_runs),
        grid_spec=pltpu.PrefetchScalarGridSpec(
            num_scalar_prefetch=1,
            grid=(1,),
            in_specs=[hbm, hbm, hbm, hbm],
            out_specs=hbm,
            scratch_shapes=[
                pltpu.VMEM((2, D_MODEL, D_EXPERT), F32),
                pltpu.VMEM((2, D_MODEL, D_EXPERT), F32),
                pltpu.VMEM((2, D_EXPERT, D_MODEL), F32),
                pltpu.VMEM((X_SLOTS, EXPERT_ROWS // GRANULE, GRANULE, HALF), I32),
                pltpu.VMEM((2, EXPERT_ROWS // GRANULE, GRANULE, HALF), I32),
                pltpu.SemaphoreType.DMA((3, 2)),
                pltpu.SemaphoreType.DMA((X_SLOTS,)),
                pltpu.SemaphoreType.DMA((2,)),
            ],
        ),
        out_shape=jax.ShapeDtypeStruct(granules.shape, I32),
        input_output_aliases={1: 0},
        compiler_params=pltpu.CompilerParams(
            dimension_semantics=("arbitrary",), vmem_limit_bytes=VMEM_LIMIT),
        name="experts",
    )(plan_table, granules, w_gate, w_up, w_down).reshape(xs_loc.shape)


def _combine_kernel(x2_ref, pos_ref, gate_ref, g_ref, ys_ref, o_ref):
    r = lax.broadcasted_iota(I32, (LOCAL_ROWS, TRUNK_ROWS), 0)
    for tile in range(COMBINE_TILES):
        toks = slice(tile * TRUNK_ROWS, (tile + 1) * TRUNK_ROWS)
        w_t = (jnp.where(r == pos_ref[0:1, toks], gate_ref[0:1, toks], 0.0)
               + jnp.where(r == pos_ref[1:2, toks], gate_ref[1:2, toks], 0.0)).astype(BF16)
        yl, yr = _unpack_halves(ys_ref[tile * LOCAL_ROWS:(tile + 1) * LOCAL_ROWS, :])
        moe = jnp.concatenate([_dot_tn(w_t, yl), _dot_tn(w_t, yr)], axis=-1)
        o_ref[toks, :] = _rms(x2_ref[toks, :] + moe, g_ref[...])


def _combine(x2, pos_tk, gates_tk, g_final, ys_loc):
    t = x2.shape[0]
    rows = COMBINE_TILES * TRUNK_ROWS
    return pl.pallas_call(
        _combine_kernel,
        grid=(t // rows,),
        in_specs=[
            pl.BlockSpec((rows, D_MODEL), lambda i: (i, 0)),
            pl.BlockSpec((SUBLANES, rows), lambda i: (0, i)),
            pl.BlockSpec((SUBLANES, rows), lambda i: (0, i)),
            pl.BlockSpec((1, D_MODEL), lambda i: (0, 0)),
            pl.BlockSpec((COMBINE_TILES * LOCAL_ROWS, HALF), lambda i: (i, 0)),
        ],
        out_specs=pl.BlockSpec((rows, D_MODEL), lambda i: (i, 0)),
        out_shape=jax.ShapeDtypeStruct((t, D_MODEL), F32),
        compiler_params=pltpu.CompilerParams(vmem_limit_bytes=VMEM_LIMIT),
        name="combine",
    )(x2, pos_tk, gates_tk, g_final, ys_loc)


def _router_params(w_rg, b_rg, w_re, b_re):
    w = jnp.zeros((ROUTER_ROWS, D_MODEL), F32)
    w = w.at[0:N_GROUPS].set(w_rg.T).at[EXPERT_LOGIT_ROW0:EXPERT_LOGIT_ROW0 + N_EXPERTS].set(w_re.T)
    b = jnp.zeros((ROUTER_ROWS,), F32)
    b = (b.at[0:N_GROUPS].set(b_rg).at[N_GROUPS:EXPERT_LOGIT_ROW0].set(NEG_BIG)
         .at[EXPERT_LOGIT_ROW0:EXPERT_LOGIT_ROW0 + N_EXPERTS].set(b_re))
    return w.astype(BF16), b.reshape(ROUTER_ROWS, 1)


def _plan(cnt, max_padded_rows):
    n_tiles = cnt.shape[0]
    piece = (cnt + GRANULE - 1) // GRANULE * GRANULE
    lend = jnp.cumsum(piece, axis=1)
    lstart = lend - piece
    tot = jnp.sum(piece, axis=0)
    padded = (tot + EXPERT_PAD - 1) // EXPERT_PAD * EXPERT_PAD
    cum_tiles = jnp.cumsum(piece, axis=0)

    per_expert = (padded + EXPERT_ROWS - 1) // EXPERT_ROWS
    blk_end = jnp.cumsum(per_expert)
    pads_per_blk = EXPERT_ROWS // EXPERT_PAD
    n_blk = (max_padded_rows // EXPERT_PAD + N_EXPERTS * (pads_per_blk - 1)) // pads_per_blk + 1
    blk = jnp.arange(n_blk, dtype=I32)
    blk_e = jnp.minimum(jnp.sum((blk_end[None, :] <= blk[:, None]).astype(I32), axis=1), N_EXPERTS - 1)
    nblk = blk_end[-1:]
    valid = blk < nblk

    of_blk_e = blk_e[:, None] == jnp.arange(N_EXPERTS, dtype=I32)[None, :]
    pick = lambda table: jnp.sum(jnp.where(of_blk_e[:, None, :], table[None, :, :], 0), axis=2)
    pick1 = lambda vec: jnp.sum(jnp.where(of_blk_e, vec[None, :], 0), axis=1)
    seg_off = (blk - pick1(blk_end - per_expert)) * EXPERT_ROWS
    seg_end = jnp.where(valid, jnp.minimum(seg_off + EXPERT_ROWS, pick1(tot)), seg_off)
    ngran = (seg_end - seg_off) // GRANULE
    piece_end = pick(cum_tiles)
    piece_start = piece_end - pick(piece)
    lo = jnp.maximum(piece_start, seg_off[:, None])
    hi = jnp.minimum(piece_end, seg_end[:, None])
    has_run = hi > lo
    tile_base = (jnp.arange(n_tiles, dtype=I32) * LOCAL_ROWS)[None, :] + pick(lstart)
    src_granule = (tile_base + lo - piece_start) // GRANULE
    off_granule = (lo - seg_off[:, None]) // GRANULE
    len_granule = (hi - lo) // GRANULE
    run_of = jnp.cumsum(has_run.astype(I32), axis=1) - 1
    is_run = has_run[:, :, None] & (run_of[:, :, None] == jnp.arange(n_tiles, dtype=I32)[None, None, :])
    compact = lambda v: jnp.sum(jnp.where(is_run, v[:, :, None], 0), axis=1)
    run_src, run_off, run_len = compact(src_granule), compact(off_granule), compact(len_granule)
    nrun = jnp.sum(has_run.astype(I32), axis=1)
    change =jnp.concatenate([jnp.ones((1,), bool), blk_e[1:] != blk_e[:-1]])
    slot = (jnp.cumsum(change.astype(I32)) - 1) % 2
    later = (blk_e[None, :] > blk_e[:, None]) & valid[None, :]
    next_e = jnp.min(jnp.where(later, blk_e[None, :], N_EXPERTS), axis=1)
    next_e = jnp.where(next_e == N_EXPERTS, -1, next_e)
    tables = dict(
        nblk=nblk, blk_e=blk_e, first=change & valid, next_e=next_e, slot=slot, nrun=nrun, ngran=ngran,
        run_src=run_src.reshape(-1), run_off=run_off.reshape(-1), run_len=run_len.reshape(-1),
    )
    order = ("nblk",) + _PLAN_PER_BLOCK + _PLAN_PER_RUN
    return jnp.concatenate([tables[k].astype(I32) for k in order]), n_blk, n_tiles


def _layer(x2d, mem2d, batch, seq, norm_mix_g, w_in, conv_w, conv_b, head_norm_g, w_out,
           norm_xa_g, norm_mem_g, w_q, w_kv, w_o, norm_ffn_g, w_rg, b_rg, w_re, b_re,
           w_gate, w_up, w_down, out_norm_g):
    t = x2d.shape[0]
    n_tiles = t // TRUNK_ROWS
    row = lambda v: v.reshape(1, -1)
    hg = head_norm_g.reshape(-1)
    gm = _group_mean_matrix()

    kv = _kv_proj(mem2d, row(norm_mem_g), w_kv)
    conv_n, uf = _mixer_in(x2d, row(norm_mix_g), w_in.astype(BF16), conv_w, row(conv_b),
                           row(hg[:CONV_CH]), gm, batch, seq)
    fft_n = _fourier(uf.reshape(batch, seq, FFT_CH), _fft_stage2_matrices(seq), _fft_channel_matrix(seq),
                     gm, row(hg[CONV_CH:]), batch, seq)
    w_r_t, b_r = _router_params(w_rg, b_rg, w_re, b_re)
    x2, xs_loc, pos, gates, cnt = _trunk(
        x2d, conv_n, fft_n, w_out.astype(BF16), row(norm_xa_g), w_q.astype(BF16), kv, w_o.astype(BF16),
        row(norm_ffn_g), w_r_t, b_r, _strict_upper(TRUNK_ROWS), _strict_lower(N_EXPERTS), seq)

    max_rows = n_tiles * LOCAL_ROWS + N_EXPERTS * (EXPERT_PAD - GRANULE)
    n_global_rows = -(-max_rows // EXPERT_PAD) * EXPERT_PAD
    plan_table, n_blk, max_runs = _plan(cnt[:, :, 0], n_global_rows)
    ys_loc = _experts(plan_table, n_blk, max_runs, xs_loc, w_gate, w_up, w_down)
    return _combine(x2, pos, gates, row(out_norm_g), ys_loc)


def kernel(x, mem, norm_mix_g, w_in, conv_w, conv_b, head_norm_g, w_out, norm_xa_g, norm_mem_g, w_q, w_kv,
           w_o, norm_ffn_g, w_route_group, b_route_group, w_route_expert, b_route_expert, w_gate, w_up,
           w_down, final_norm_g):
    batch, seq, _ = x.shape
    depth = norm_mix_g.shape[0]
    assert depth == 1, "the final norm is fused into the last layer's combine kernel"
    x2d = x.reshape(batch * seq, D_MODEL)
    mem2d = mem.reshape(batch * MEM_LEN, D_MODEL)
    l = 0
    out = _layer(x2d, mem2d, batch, seq, norm_mix_g[l], w_in[l], conv_w[l], conv_b[l], head_norm_g[l],
                 w_out[l], norm_xa_g[l], norm_mem_g[l], w_q[l], w_kv[l], w_o[l], norm_ffn_g[l],
                 w_route_group[l], b_route_group[l], w_route_expert[l], b_route_expert[l],
                 w_gate[l], w_up[l], w_down[l], final_norm_g)
    return out.reshape(batch, seq, D_MODEL)
```

```python
import functools
import math

import numpy as np
import jax
import jax.numpy as jnp
from jax import lax
from jax.experimental import pallas as pl
from jax.experimental.pallas import tpu as pltpu

F32 = jnp.float32
BF16 = jnp.bfloat16
I32 = jnp.int32

D_MODEL = 1024
HALF = D_MODEL // 2
HEAD_DIM = 64
CONV_CH = 512
FFT_CH = 512
IN_COLS = 3 * CONV_CH + FFT_CH
MEM_LEN = 256
XA_HEADS = 4
XA_HEAD_DIM = D_MODEL // XA_HEADS
N_GROUPS = 4
EXPERTS_PER_GROUP = 8
N_EXPERTS = 32
TOP_K = 2
D_EXPERT = 512
EPS = 1e-6

FFT_N1 = 16
FFT_N2 = 256
FFT_K1_PER_STEP = 4

LANES = 128
MXU_COLS = 256
SUBLANES = 8
GRANULE = SUBLANES
GRANULE_SHIFT = GRANULE.bit_length() - 1
MIX_ROWS = 1024
TRUNK_ROWS = 512
LOCAL_ROWS = TOP_K * TRUNK_ROWS + N_EXPERTS * GRANULE
COMBINE_TILES = 2
EXPERT_PAD = 128
EXPERT_ROWS = 1536
X_SLOTS = 3
ROUTER_ROWS = 128
EXPERT_LOGIT_ROW0 = SUBLANES
NEG_BIG = -1e30
HI16 = -65536

VMEM_LIMIT = 56 * 1024 * 1024


def _rms(x, g):
    return x * lax.rsqrt(jnp.mean(x * x, axis=-1, keepdims=True) + EPS) * g


def _dot(a, b):
    return jnp.dot(a, b, preferred_element_type=F32)


def _dot_nt(a, b):
    return lax.dot_general(a, b, (((1,), (1,)), ((), ())), preferred_element_type=F32)


def _dot_tn(a, b):
    return lax.dot_general(a, b, (((0,), (0,)), ((), ())), preferred_element_type=F32)


def _pack_halves(left_f32, right_f32):
    lb = lax.bitcast_convert_type(left_f32, I32)
    rb = lax.shift_right_logical(lax.bitcast_convert_type(right_f32, I32), jnp.int32(16))
    return lb | rb


def _unpack_halves(packed_i32):
    left = lax.bitcast_convert_type(packed_i32 & jnp.int32(HI16), F32)
    right = lax.bitcast_convert_type(lax.shift_left(packed_i32, jnp.int32(16)), F32)
    return left.astype(BF16), right.astype(BF16)


def _group_mean_matrix():
    g = np.kron(np.eye(MXU_COLS // HEAD_DIM), np.full((HEAD_DIM, HEAD_DIM), 1.0 / HEAD_DIM))
    return jnp.asarray(g, dtype=BF16)


def _head_mean_square(y, gm):
    sq = (y * y).astype(BF16)
    return jnp.concatenate([_dot(sq[:, c:c + MXU_COLS], gm) for c in range(0, y.shape[1], MXU_COLS)], axis=1)


def _fft_stage2_matrices(seq):
    k1 = np.arange(FFT_N1)[:, None, None]
    k2 = np.arange(FFT_N2)[None, :, None]
    s2 = np.arange(FFT_N2)[None, None, :]
    ang = 2.0 * np.pi * ((s2 * (k1 + FFT_N1 * k2)) % seq) / seq
    c, s = np.cos(ang), np.sin(ang)
    top = np.concatenate([c, s], axis=2)
    bot = np.concatenate([-s, c], axis=2)
    return jnp.asarray(np.concatenate([top, bot], axis=1), dtype=BF16)


def _fft_channel_matrix(seq):
    c = np.arange(HEAD_DIM)
    ang = 2.0 * np.pi * ((c[:, None] * c[None, :]) % HEAD_DIM) / HEAD_DIM
    scale = 1.0 / math.sqrt(seq * HEAD_DIM)
    eye = np.eye(MXU_COLS // HEAD_DIM)
    cs = np.concatenate([np.kron(eye, np.cos(ang)), np.kron(eye, np.sin(ang))], axis=0) * scale
    return jnp.asarray(cs, dtype=BF16)


def _strict_upper(n):
    return jnp.asarray(np.triu(np.ones((n, n)), k=1), dtype=BF16)


def _strict_lower(n):
    return jnp.asarray(np.tril(np.ones((n, n)), k=-1), dtype=BF16)


def _kv_kernel(mem_ref, g_ref, w_ref, o_ref):
    h = _rms(mem_ref[...], g_ref[...]).astype(BF16)
    o_ref[...] = _dot(h, w_ref[...].astype(BF16)).astype(BF16)


def _kv_proj(mem2d, g, w_kv):
    rows = mem2d.shape[0]
    cols = w_kv.shape[1]
    cb = 512
    return pl.pallas_call(
        _kv_kernel,
        grid=(cols // cb,),
        in_specs=[
            pl.BlockSpec((rows, D_MODEL), lambda j: (0, 0)),
            pl.BlockSpec((1, D_MODEL), lambda j: (0, 0)),
            pl.BlockSpec((D_MODEL, cb), lambda j: (0, j)),
        ],
        out_specs=pl.BlockSpec((rows, cb), lambda j: (0, j)),
        out_shape=jax.ShapeDtypeStruct((rows, cols), BF16),
        compiler_params=pltpu.CompilerParams(vmem_limit_bytes=VMEM_LIMIT),
        name="kv_proj",
    )(mem2d, g, w_kv)


def _mixer_in_kernel(x_ref, xp_ref, xn_ref, g_ref, w_ref, cw_ref, cb_ref, hg_ref, gm_ref,
                     conv_ref, uf_ref):
    i = pl.program_id(1)
    n_i = pl.num_programs(1)
    rows = x_ref.shape[0]
    g = g_ref[...]
    h = _rms(x_ref[...], g).astype(BF16)
    u_cv = _dot(h, w_ref[:, CONV_CH:3 * CONV_CH])
    cv = u_cv[:, :CONV_CH] * u_cv[:, CONV_CH:]

    hh = jnp.concatenate([_rms(xp_ref[...], g), _rms(xn_ref[...], g)], axis=0).astype(BF16)
    uh = _dot(hh, w_ref[:, CONV_CH:3 * CONV_CH])
    cvh = uh[:, :CONV_CH] * uh[:, CONV_CH:]
    cv_prev = cvh[SUBLANES - 1:SUBLANES, :] * jnp.where(i == 0, 0.0, 1.0)
    cv_next = cvh[SUBLANES:SUBLANES + 1, :] * jnp.where(i == n_i - 1, 0.0, 1.0)

    row = lax.broadcasted_iota(I32, cv.shape, 0)
    cv_up = jnp.where(row == 0, cv_prev, pltpu.roll(cv, 1, 0))
    cv_dn = jnp.where(row == rows - 1, cv_next, pltpu.roll(cv, rows - 1, 0))
    z = cw_ref[0:1, :] * cv_up + cw_ref[1:2, :] * cv + cw_ref[2:3, :] * cv_dn + cb_ref[...]
    uf_ref[...] = _dot(h, w_ref[:, 3 * CONV_CH:]).astype(BF16)
    y = _dot(h, w_ref[:, :CONV_CH]) * z
    ms = _head_mean_square(y, gm_ref[...])
    conv_ref[...] = (y * lax.rsqrt(ms + EPS) * hg_ref[...]).astype(BF16)


def _mixer_in(x2d, g, w_in, conv_w, conv_b, hg_conv, gm, batch, seq):
    n_i = seq // MIX_ROWS
    t = x2d.shape[0]
    r8 = MIX_ROWS // SUBLANES
    last8 = t // SUBLANES - 1
    return pl.pallas_call(
        _mixer_in_kernel,
        grid=(batch, n_i),
        in_specs=[
            pl.BlockSpec((MIX_ROWS, D_MODEL), lambda b, i: (b * n_i + i, 0)),
            pl.BlockSpec((SUBLANES, D_MODEL), lambda b, i: (jnp.maximum((b * n_i + i) * r8 - 1, 0), 0)),
            pl.BlockSpec((SUBLANES, D_MODEL), lambda b, i: (jnp.minimum((b * n_i + i + 1) * r8, last8), 0)),
            pl.BlockSpec((1, D_MODEL), lambda b, i: (0, 0)),
            pl.BlockSpec((D_MODEL, IN_COLS), lambda b, i: (0, 0)),
            pl.BlockSpec((3, CONV_CH), lambda b, i: (0, 0)),
            pl.BlockSpec((1, CONV_CH), lambda b, i: (0, 0)),
            pl.BlockSpec((1, CONV_CH), lambda b, i: (0, 0)),
            pl.BlockSpec((MXU_COLS, MXU_COLS), lambda b, i: (0, 0)),
        ],
        out_specs=[
            pl.BlockSpec((MIX_ROWS, CONV_CH), lambda b, i: (b * n_i + i, 0)),
            pl.BlockSpec((MIX_ROWS, FFT_CH), lambda b, i: (b * n_i + i, 0)),
        ],
        out_shape=[
            jax.ShapeDtypeStruct((t, CONV_CH), BF16),
            jax.ShapeDtypeStruct((t, FFT_CH), BF16),
        ],
        compiler_params=pltpu.CompilerParams(vmem_limit_bytes=VMEM_LIMIT),
        name="mixer_in",
    )(x2d, x2d, x2d, g, w_in, conv_w, conv_b, hg_conv, gm)


_S1_ROWS = 16
_S1_LANES = 128


def _lincomb(terms):
    acc = None
    for coef, val in terms:
        if abs(coef) < 1e-12:
            continue
        if abs(coef - 1.0) < 1e-12:
            term, neg = val, False
        elif abs(coef + 1.0) < 1e-12:
            term, neg = val, True
        else:
            term, neg = coef * val, False
        if acc is None:
            acc = -term if neg else term
        else:
            acc = acc - term if neg else acc + term
    return acc


def _fft_stage1(x_ref, a_ref):
    half = FFT_N1 // 2
    cos = [[math.cos(2 * math.pi * ((k * j) % FFT_N1) / FFT_N1) for j in range(FFT_N1)] for k in range(FFT_N1)]
    sin = [[math.sin(2 * math.pi * ((k * j) % FFT_N1) / FFT_N1) for j in range(FFT_N1)] for k in range(FFT_N1)]

    def body(r, carry):
        r0 = pl.multiple_of(r * _S1_ROWS, _S1_ROWS)
        rows_re = pl.ds(r0, _S1_ROWS)
        rows_im = pl.ds(r0 + FFT_N2, _S1_ROWS)
        for lc in range(0, FFT_CH, _S1_LANES):
            lanes = slice(lc, lc + _S1_LANES)
            xs = [x_ref[j, rows_re, lanes].astype(F32) for j in range(FFT_N1)]
            ev = [None] + [xs[j] + xs[FFT_N1 - j] for j in range(1, half)]
            od = [None] + [xs[j] - xs[FFT_N1 - j] for j in range(1, half)]
            for k in range(half + 1):
                re = _lincomb([(1.0, xs[0]), (cos[k][half], xs[half])]
                              + [(cos[k][j], ev[j]) for j in range(1, half)])
                a_ref[k, rows_re, lanes] = re.astype(BF16)
                if k in (0, half):
                    zero = jnp.zeros_like(re).astype(BF16)
                    a_ref[k, rows_im, lanes] = zero
                else:
                    im = _lincomb([(-sin[k][j], od[j]) for j in range(1, half)])
                    a_ref[k, rows_im, lanes] = im.astype(BF16)
                    a_ref[FFT_N1 - k, rows_re, lanes] = re.astype(BF16)
                    a_ref[FFT_N1 - k, rows_im, lanes] = (-im).astype(BF16)
        return carry

    lax.fori_loop(0, FFT_N2 // _S1_ROWS, body, 0)


def _fourier_kernel(x_ref, m2_ref, cs_ref, gm_ref, hg_ref, o_ref, a_ref, y_ref):
    j = pl.program_id(1)

    @pl.when(j == 0)
    def _():
        _fft_stage1(x_ref, a_ref)

    ris = [_dot(m2_ref[kk], a_ref[j * FFT_K1_PER_STEP + kk]) for kk in range(FFT_K1_PER_STEP)]
    re = jnp.concatenate([ri[:FFT_N2] for ri in ris], axis=0).astype(BF16)
    im = jnp.concatenate([ri[FFT_N2:] for ri in ris], axis=0).astype(BF16)
    y = jnp.concatenate(
        [_dot(re[:, c:c + MXU_COLS], cs_ref[:MXU_COLS, :]) + _dot(im[:, c:c + MXU_COLS], cs_ref[MXU_COLS:, :])
         for c in range(0, FFT_CH, MXU_COLS)], axis=1)
    yn = y * lax.rsqrt(_head_mean_square(y, gm_ref[...]) + EPS) * hg_ref[...]
    for kk in range(FFT_K1_PER_STEP):
        k1 = j * FFT_K1_PER_STEP + kk
        for c in range(FFT_CH // LANES):
            y_ref[c, pl.ds(k1, FFT_N2, stride=FFT_N1), :] = yn[kk * FFT_N2:(kk + 1) * FFT_N2,
                                                               c * LANES:(c + 1) * LANES]

    @pl.when(j == pl.num_programs(1) - 1)
    def _():
        for c in range(FFT_CH // LANES):
            o_ref[:, c * LANES:(c + 1) * LANES] = y_ref[c].astype(BF16)


def _fourier(uf, m2, cs, gm, hg_fft, batch, seq):
    assert seq == FFT_N1 * FFT_N2
    x4 = uf.reshape(batch, FFT_N1, FFT_N2, FFT_CH)
    out = pl.pallas_call(
        _fourier_kernel,
        grid=(batch, FFT_N1 // FFT_K1_PER_STEP),
        in_specs=[
            pl.BlockSpec((None, FFT_N1, FFT_N2, FFT_CH), lambda b, j: (b, 0, 0, 0)),
            pl.BlockSpec((FFT_K1_PER_STEP, 2 * FFT_N2, 2 * FFT_N2), lambda b, j: (j, 0, 0)),
            pl.BlockSpec((2 * MXU_COLS, MXU_COLS), lambda b, j: (0, 0)),
            pl.BlockSpec((MXU_COLS, MXU_COLS), lambda b, j: (0, 0)),
            pl.BlockSpec((1, FFT_CH), lambda b, j: (0, 0)),
        ],
        out_specs=pl.BlockSpec((seq, FFT_CH), lambda b, j: (b, 0)),
        out_shape=jax.ShapeDtypeStruct((batch * seq, FFT_CH), BF16),
        scratch_shapes=[
            pltpu.VMEM((FFT_N1, 2 * FFT_N2, FFT_CH), BF16),
            pltpu.VMEM((FFT_CH // LANES, seq, LANES), F32),
        ],
        compiler_params=pltpu.CompilerParams(
            dimension_semantics=("arbitrary", "arbitrary"), vmem_limit_bytes=VMEM_LIMIT),
        name="fourier",
    )(x4, m2, cs, gm, hg_fft)
    return out


def _first_index_of_max(vals, vmax, row):
    return jnp.min(jnp.where(vals == vmax, row, vals.shape[0]), axis=0, keepdims=True)


def _route(lg):
    cols = lg.shape[1]
    row8 = lax.broadcasted_iota(I32, (EXPERTS_PER_GROUP, cols), 0)
    gl = lg[0:EXPERT_LOGIT_ROW0, :]
    gmax = jnp.max(gl, axis=0, keepdims=True)
    g_w = 1.0 / jnp.sum(jnp.exp(gl - gmax), axis=0, keepdims=True)
    g_idx = _first_index_of_max(gl, gmax, row8)

    el = lg[EXPERT_LOGIT_ROW0:EXPERT_LOGIT_ROW0 + EXPERTS_PER_GROUP, :]
    for g in range(1, N_GROUPS):
        row0 = EXPERT_LOGIT_ROW0 + g * EXPERTS_PER_GROUP
        el = jnp.where(g_idx == g, lg[row0:row0 + EXPERTS_PER_GROUP, :], el)
    emax = jnp.max(el, axis=0, keepdims=True)
    ee = jnp.exp(el - emax)
    e_prob = ee / jnp.sum(ee, axis=0, keepdims=True)
    p1 = jnp.max(e_prob, axis=0, keepdims=True)
    i1 = _first_index_of_max(e_prob, p1, row8)
    rest = jnp.where(row8 == i1, -1.0, e_prob)
    p2 = jnp.max(rest, axis=0, keepdims=True)
    i2 = _first_index_of_max(rest, p2, row8)
    denom = p1 + p2
    e1 = g_idx * EXPERTS_PER_GROUP + i1
    e2 = g_idx * EXPERTS_PER_GROUP + i2
    return e1, e2, g_w * p1 / denom, g_w * p2 / denom


def _trunk_kernel(x_ref, conv_ref, fft_ref, wout_ref, gxa_ref, wq_ref, k_ref, v_ref, wo_ref,
                  gffn_ref, wr_ref, br_ref, tri_ref, ltri_ref,
                  x2_ref, xs_ref, pos_ref, gate_ref, cnt_ref, h3_s, lg_s):
    @pl.when(pl.program_id(0) == 0)
    def _():
        h3_s[...] = jnp.zeros_like(h3_s)
        lg_s[...] = jnp.zeros_like(lg_s)

    h3 = h3_s[...]
    lg = lg_s[...]
    n_tok = lg.shape[1]

    x1 = x_ref[...] + _dot(jnp.concatenate([conv_ref[...], fft_ref[...]], axis=1), wout_ref[...])

    e1, e2, gate1, gate2 = _route(lg)
    row32 = lax.broadcasted_iota(I32, (N_EXPERTS, n_tok), 0)
    hit1 = row32 == e1
    hit2 = row32 == e2
    onehot = jnp.where(hit1 | hit2, 1.0, 0.0)
    before = _dot(onehot.astype(BF16), tri_ref[...])
    cnt = jnp.sum(onehot, axis=1, keepdims=True).astype(I32)
    piece = jnp.left_shift(jnp.right_shift(cnt + (GRANULE - 1), GRANULE_SHIFT), GRANULE_SHIFT)
    piece_b = jnp.broadcast_to(piece.astype(F32), (N_EXPERTS, LANES)).astype(BF16)
    start = _dot(ltri_ref[...], piece_b)[:, 0:1]
    slot = before + start
    pos1 = jnp.sum(jnp.where(hit1, slot, 0.0), axis=0, keepdims=True).astype(I32)
    pos2 = jnp.sum(jnp.where(hit2, slot, 0.0), axis=0, keepdims=True).astype(I32)
    pos_ref[0:1, :] = pos1
    pos_ref[1:2, :] = pos2
    pos_ref[TOP_K:, :] = jnp.zeros((SUBLANES - TOP_K, n_tok), I32)
    gate_ref[0:1, :] = gate1
    gate_ref[1:2, :] = gate2
    gate_ref[TOP_K:, :] = jnp.zeros((SUBLANES - TOP_K, n_tok), F32)
    cnt_ref[...] = jnp.broadcast_to(cnt, (N_EXPERTS, LANES))

    h2 = _rms(x1, gxa_ref[...]).astype(BF16)
    q = _dot(h2, wq_ref[...]).astype(BF16)

    r = lax.broadcasted_iota(I32, (LOCAL_ROWS, n_tok), 0)
    perm = jnp.where((r == pos1) | (r == pos2), 1.0, 0.0).astype(BF16)

    outs = []
    for hd in range(XA_HEADS):
        cols = slice(hd * XA_HEAD_DIM, (hd + 1) * XA_HEAD_DIM)
        s = _dot_nt(q[:, cols], k_ref[:, cols]) * (XA_HEAD_DIM ** -0.5)
        s = s - jnp.max(s, axis=-1, keepdims=True)
        p = jnp.exp(s)
        p = p / jnp.sum(p, axis=-1, keepdims=True)
        outs.append(_dot(p.astype(BF16), v_ref[:, cols]).astype(BF16))
        if hd == 1:
            sorted_left = _dot(perm, h3[:, :HALF])

    o = jnp.concatenate(outs, axis=-1)
    x2 = x1 + _dot(o, wo_ref[...])
    x2_ref[...] = x2
    sorted_right = _dot(perm, h3[:, HALF:])
    h3_next = _rms(x2, gffn_ref[...]).astype(BF16)
    xs_ref[...] = _pack_halves(sorted_left, sorted_right)
    h3_s[...] = h3_next
    lg_s[...] = _dot_nt(wr_ref[...], h3_next) + br_ref[...]


def _trunk(x2d, conv_n, fft_n, w_out, g_xa, w_q, kv, w_o, g_ffn, w_r_t, b_r, tri, ltri, seq):
    t = x2d.shape[0]
    n_tiles = t // TRUNK_ROWS
    n_per_batch = seq // TRUNK_ROWS
    const = lambda i: (0, 0)
    dense = lambda i: jnp.minimum(i, n_tiles - 1)
    routed = lambda i: jnp.maximum(i - 1, 0)
    return pl.pallas_call(
        _trunk_kernel,
        grid=(n_tiles + 1,),
        in_specs=[
            pl.BlockSpec((TRUNK_ROWS, D_MODEL), lambda i: (dense(i), 0)),
            pl.BlockSpec((TRUNK_ROWS, CONV_CH), lambda i: (dense(i), 0)),
            pl.BlockSpec((TRUNK_ROWS, FFT_CH), lambda i: (dense(i), 0)),
            pl.BlockSpec((D_MODEL, D_MODEL), const),
            pl.BlockSpec((1, D_MODEL), const),
            pl.BlockSpec((D_MODEL, D_MODEL), const),
            pl.BlockSpec((MEM_LEN, D_MODEL), lambda i: (dense(i) // n_per_batch, 0)),
            pl.BlockSpec((MEM_LEN, D_MODEL), lambda i: (dense(i) // n_per_batch, 1)),
            pl.BlockSpec((D_MODEL, D_MODEL), const),
            pl.BlockSpec((1, D_MODEL), const),
            pl.BlockSpec((ROUTER_ROWS, D_MODEL), const),
            pl.BlockSpec((ROUTER_ROWS, 1), const),
            pl.BlockSpec((TRUNK_ROWS, TRUNK_ROWS), const),
            pl.BlockSpec((N_EXPERTS, N_EXPERTS), const),
        ],
        out_specs=[
            pl.BlockSpec((TRUNK_ROWS, D_MODEL), lambda i: (dense(i), 0)),
            pl.BlockSpec((LOCAL_ROWS, HALF), lambda i: (routed(i), 0)),
            pl.BlockSpec((SUBLANES, TRUNK_ROWS), lambda i: (0, routed(i))),
            pl.BlockSpec((SUBLANES, TRUNK_ROWS), lambda i: (0, routed(i))),
            pl.BlockSpec((None, N_EXPERTS, LANES), lambda i: (routed(i), 0, 0)),
        ],
        out_shape=[
            jax.ShapeDtypeStruct((t, D_MODEL), F32),
            jax.ShapeDtypeStruct((n_tiles * LOCAL_ROWS, HALF), I32),
            jax.ShapeDtypeStruct((SUBLANES, t), I32),
            jax.ShapeDtypeStruct((SUBLANES, t), F32),
            jax.ShapeDtypeStruct((n_tiles, N_EXPERTS, LANES), I32),
        ],
        scratch_shapes=[
            pltpu.VMEM((TRUNK_ROWS, D_MODEL), BF16),
            pltpu.VMEM((ROUTER_ROWS, TRUNK_ROWS), F32),
        ],
        compiler_params=pltpu.CompilerParams(
            dimension_semantics=("arbitrary",), vmem_limit_bytes=VMEM_LIMIT),
        name="trunk",
    )(x2d, conv_n, fft_n, w_out, g_xa, w_q, kv, kv, w_o, g_ffn, w_r_t, b_r, tri, ltri)


_PLAN_PER_BLOCK = ("blk_e", "first", "next_e", "slot", "nrun", "ngran")
_PLAN_PER_RUN = ("run_src", "run_off", "run_len")


def _plan_layout(n_blk, max_runs):
    sizes = [("nblk", 1)] + [(k, n_blk) for k in _PLAN_PER_BLOCK] + [(k, n_blk * max_runs) for k in _PLAN_PER_RUN]
    offsets, at = {}, 0
    for name, size in sizes:
        offsets[name] = at
        at += size
    return offsets


class _TableView:
    def __init__(self, ref, offset):
        self._ref, self._offset = ref, offset

    def __getitem__(self, i):
        return self._ref[self._offset + i]


def _experts_kernel(plan_ref, xs_hbm, wg_hbm, wu_hbm, wd_hbm, ys_hbm,
                    wg_buf, wu_buf, wd_buf, xbuf, obuf, wsems, xsems, osems, *, n_blk, max_runs):
    layout = _plan_layout(n_blk, max_runs)
    (blk_e_ref, first_ref, next_e_ref, slot_ref, nblk_ref, run_src_ref, run_off_ref, run_len_ref, nrun_ref,
     ngran_ref) = (_TableView(plan_ref, layout[k]) for k in (
         "blk_e", "first", "next_e", "slot", "nblk", "run_src", "run_off", "run_len", "nrun", "ngran"))
    _experts_body(blk_e_ref, first_ref, next_e_ref, slot_ref, nblk_ref,
                  run_src_ref, run_off_ref, run_len_ref, nrun_ref, ngran_ref, n_blk - 1, max_runs,
                  xs_hbm, wg_hbm, wu_hbm, wd_hbm, ys_hbm,
                  wg_buf, wu_buf, wd_buf, xbuf, obuf, wsems, xsems, osems)


def _experts_body(blk_e_ref, first_ref, next_e_ref, slot_ref, nblk_ref,
                  run_src_ref, run_off_ref, run_len_ref, nrun_ref, ngran_ref, last, max_runs,
                  xs_hbm, wg_hbm, wu_hbm, wd_hbm, ys_hbm,
                  wg_buf, wu_buf, wd_buf, xbuf, obuf, wsems, xsems, osems):
    n = nblk_ref[0]
    gpb = EXPERT_ROWS // GRANULE

    def fetch_weights(e, s):
        return (pltpu.make_async_copy(wg_hbm.at[e], wg_buf.at[s], wsems.at[0, s]),
                pltpu.make_async_copy(wu_hbm.at[e], wu_buf.at[s], wsems.at[1, s]),
                pltpu.make_async_copy(wd_hbm.at[e], wd_buf.at[s], wsems.at[2, s]))

    def run_copies(b, r, s):
        k = b * max_runs + r
        length = run_len_ref[k]
        hbm_rows = pl.ds(run_src_ref[k], length)
        blk_rows = pl.ds(run_off_ref[k], length)
        return (pltpu.make_async_copy(xs_hbm.at[hbm_rows], xbuf.at[s, blk_rows], xsems.at[s]),
                pltpu.make_async_copy(obuf.at[s, blk_rows], ys_hbm.at[hbm_rows], osems.at[s]))

    def start_in(b, s, runs):
        lax.fori_loop(0, runs, lambda r, c: (run_copies(b, r, s)[0].start(), c)[1], 0)

    def start_out(b, s):
        lax.fori_loop(0, nrun_ref[b], lambda r, c: (run_copies(b, r, s)[1].start(), c)[1], 0)

    def wait_in(b, s):
        count = ngran_ref[b]
        pltpu.make_async_copy(xs_hbm.at[pl.ds(0, count)], xbuf.at[s, pl.ds(0, count)], xsems.at[s]).wait()

    def wait_out(b, s):
        count = ngran_ref[b]
        pltpu.make_async_copy(obuf.at[s, pl.ds(0, count)], ys_hbm.at[pl.ds(0, count)], osems.at[s]).wait()

    xbuf[...] = jnp.zeros_like(xbuf)
    for cp in fetch_weights(blk_e_ref[0], 0):
        cp.start()
    start_in(0, 0, nrun_ref[0])
    start_in(1, 1, jnp.where(n > 1, nrun_ref[1], 0))

    def block(i, carry):
        xs = lax.rem(i, X_SLOTS)
        os = i % 2
        ws = slot_ref[i]

        @pl.when(first_ref[i] == 1)
        def _():
            for cp in fetch_weights(blk_e_ref[i], ws):
                cp.wait()

            @pl.when(next_e_ref[i] >= 0)
            def _():
                for cp in fetch_weights(next_e_ref[i], 1 - ws):
                    cp.start()

        @pl.when(i >= 2)
        def _():
            wait_out(i - 2, os)

        ahead = jnp.minimum(i + 2, last)
        start_in(ahead, lax.rem(i + 2, X_SLOTS), jnp.where(i + 2 < n, nrun_ref[ahead], 0))

        wait_in(i, xs)

        def mlp(rows):
            granules = rows // GRANULE
            xl, xr = _unpack_halves(xbuf[xs, :granules].reshape(rows, HALF))
            x = jnp.concatenate([xl, xr], axis=1)
            a = _dot(x, wg_buf[ws].astype(BF16))
            b = _dot(x, wu_buf[ws].astype(BF16))
            hmid = (a * jax.nn.sigmoid(a) * b).astype(BF16)
            y = _dot(hmid, wd_buf[ws].astype(BF16))
            packed = _pack_halves(y[:, :HALF].astype(BF16).astype(F32), y[:, HALF:].astype(BF16).astype(F32))
            obuf[os, :granules] = packed.reshape(granules, GRANULE, HALF)

        pads = -(-ngran_ref[i] // (EXPERT_PAD // GRANULE))
        for k in range(1, EXPERT_ROWS // EXPERT_PAD + 1):
            pl.when(pads == k)(functools.partial(mlp, k * EXPERT_PAD))
        start_out(i, os)
        return carry

    lax.fori_loop(0, n, block, 0)

    @pl.when(n >= 2)
    def _():
        wait_out(n - 2, n % 2)
    wait_out(n - 1, (n - 1) % 2)


def _experts(plan_table, n_blk, max_runs, xs_loc, w_gate, w_up, w_down):
    granules = xs_loc.reshape(-1, GRANULE, HALF)
    hbm = pl.BlockSpec(memory_space=pl.ANY)
    return pl.pallas_call(
        functools.partial(_experts_kernel, n_blk=n_blk, max_runs=max_runs),
        grid_spec=pltpu.PrefetchScalarGridSpec(
            num_scalar_prefetch=1,
            grid=(1,),
            in_specs=[hbm, hbm, hbm, hbm],
            out_specs=hbm,
            scratch_shapes=[
                pltpu.VMEM((2, D_MODEL, D_EXPERT), F32),
                pltpu.VMEM((2, D_MODEL, D_EXPERT), F32),
                pltpu.VMEM((2, D_EXPERT, D_MODEL), F32),
                pltpu.VMEM((X_SLOTS, EXPERT_ROWS // GRANULE, GRANULE, HALF), I32),
                pltpu.VMEM((2, EXPERT_ROWS // GRANULE, GRANULE, HALF), I32),
                pltpu.SemaphoreType.DMA((3, 2)),
                pltpu.SemaphoreType.DMA((X_SLOTS,)),
                pltpu.SemaphoreType.DMA((2,)),
            ],
        ),
        out_shape=jax.ShapeDtypeStruct(granules.shape, I32),
        input_output_aliases={1: 0},
        compiler_params=pltpu.CompilerParams(
            dimension_semantics=("arbitrary",), vmem_limit_bytes=VMEM_LIMIT),
        name="experts",
    )(plan_table, granules, w_gate, w_up, w_down).reshape(xs_loc.shape)


def _combine_kernel(x2_ref, pos_ref, gate_ref, g_ref, ys_ref, o_ref):
    r = lax.broadcasted_iota(I32, (LOCAL_ROWS, TRUNK_ROWS), 0)
    for tile in range(COMBINE_TILES):
        toks = slice(tile * TRUNK_ROWS, (tile + 1) * TRUNK_ROWS)
        w_t = (jnp.where(r == pos_ref[0:1, toks], gate_ref[0:1, toks], 0.0)
               + jnp.where(r == pos_ref[1:2, toks], gate_ref[1:2, toks], 0.0)).astype(BF16)
        yl, yr = _unpack_halves(ys_ref[tile * LOCAL_ROWS:(tile + 1) * LOCAL_ROWS, :])
        moe = jnp.concatenate([_dot_tn(w_t, yl), _dot_tn(w_t, yr)], axis=-1)
        o_ref[toks, :] = _rms(x2_ref[toks, :] + moe, g_ref[...])


def _combine(x2, pos_tk, gates_tk, g_final, ys_loc):
    t = x2.shape[0]
    rows = COMBINE_TILES * TRUNK_ROWS
    return pl.pallas_call(
        _combine_kernel,
        grid=(t // rows,),
        in_specs=[
            pl.BlockSpec((rows, D_MODEL), lambda i: (i, 0)),
            pl.BlockSpec((SUBLANES, rows), lambda i: (0, i)),
            pl.BlockSpec((SUBLANES, rows), lambda i: (0, i)),
            pl.BlockSpec((1, D_MODEL), lambda i: (0, 0)),
            pl.BlockSpec((COMBINE_TILES * LOCAL_ROWS, HALF), lambda i: (i, 0)),
        ],
        out_specs=pl.BlockSpec((rows, D_MODEL), lambda i: (i, 0)),
        out_shape=jax.ShapeDtypeStruct((t, D_MODEL), F32),
        compiler_params=pltpu.CompilerParams(vmem_limit_bytes=VMEM_LIMIT),
        name="combine",
    )(x2, pos_tk, gates_tk, g_final, ys_loc)


def _router_params(w_rg, b_rg, w_re, b_re):
    w = jnp.zeros((ROUTER_ROWS, D_MODEL), F32)
    w = w.at[0:N_GROUPS].set(w_rg.T).at[EXPERT_LOGIT_ROW0:EXPERT_LOGIT_ROW0 + N_EXPERTS].set(w_re.T)
    b = jnp.zeros((ROUTER_ROWS,), F32)
    b = (b.at[0:N_GROUPS].set(b_rg).at[N_GROUPS:EXPERT_LOGIT_ROW0].set(NEG_BIG)
         .at[EXPERT_LOGIT_ROW0:EXPERT_LOGIT_ROW0 + N_EXPERTS].set(b_re))
    return w.astype(BF16), b.reshape(ROUTER_ROWS, 1)


def _plan(cnt, max_padded_rows):
    n_tiles = cnt.shape[0]
    piece = (cnt + GRANULE - 1) // GRANULE * GRANULE
    lend = jnp.cumsum(piece, axis=1)
    lstart = lend - piece
    tot = jnp.sum(piece, axis=0)
    padded = (tot + EXPERT_PAD - 1) // EXPERT_PAD * EXPERT_PAD
    cum_tiles = jnp.cumsum(piece, axis=0)

    per_expert = (padded + EXPERT_ROWS - 1) // EXPERT_ROWS
    blk_end = jnp.cumsum(per_expert)
    pads_per_blk = EXPERT_ROWS // EXPERT_PAD
    n_blk = (max_padded_rows // EXPERT_PAD + N_EXPERTS * (pads_per_blk - 1)) // pads_per_blk + 1
    blk = jnp.arange(n_blk, dtype=I32)
    blk_e = jnp.minimum(jnp.sum((blk_end[None, :] <= blk[:, None]).astype(I32), axis=1), N_EXPERTS - 1)
    nblk = blk_end[-1:]
    valid = blk < nblk

    of_blk_e = blk_e[:, None] == jnp.arange(N_EXPERTS, dtype=I32)[None, :]
    pick = lambda table: jnp.sum(jnp.where(of_blk_e[:, None, :], table[None, :, :], 0), axis=2)
    pick1 = lambda vec: jnp.sum(jnp.where(of_blk_e, vec[None, :], 0), axis=1)
    seg_off = (blk - pick1(blk_end - per_expert)) * EXPERT_ROWS
    seg_end = jnp.where(valid, jnp.minimum(seg_off + EXPERT_ROWS, pick1(tot)), seg_off)
    ngran = (seg_end - seg_off) // GRANULE
    piece_end = pick(cum_tiles)
    piece_start = piece_end - pick(piece)
    lo = jnp.maximum(piece_start, seg_off[:, None])
    hi = jnp.minimum(piece_end, seg_end[:, None])
    has_run = hi > lo
    tile_base = (jnp.arange(n_tiles, dtype=I32) * LOCAL_ROWS)[None, :] + pick(lstart)
    src_granule = (tile_base + lo - piece_start) // GRANULE
    off_granule = (lo - seg_off[:, None]) // GRANULE
    len_granule = (hi - lo) // GRANULE
    run_of = jnp.cumsum(has_run.astype(I32), axis=1) - 1
    is_run = has_run[:, :, None] & (run_of[:, :, None] == jnp.arange(n_tiles, dtype=I32)[None, None, :])
    compact = lambda v: jnp.sum(jnp.where(is_run, v[:, :, None], 0), axis=1)
    run_src, run_off, run_len = compact(src_granule), compact(off_granule), compact(len_granule)
    nrun = jnp.sum(has_run.astype(I32), axis=1)
    change =jnp.concatenate([jnp.ones((1,), bool), blk_e[1:] != blk_e[:-1]])
    slot = (jnp.cumsum(change.astype(I32)) - 1) % 2
    later = (blk_e[None, :] > blk_e[:, None]) & valid[None, :]
    next_e = jnp.min(jnp.where(later, blk_e[None, :], N_EXPERTS), axis=1)
    next_e = jnp.where(next_e == N_EXPERTS, -1, next_e)
    tables = dict(
        nblk=nblk, blk_e=blk_e, first=change & valid, next_e=next_e, slot=slot, nrun=nrun, ngran=ngran,
        run_src=run_src.reshape(-1), run_off=run_off.reshape(-1), run_len=run_len.reshape(-1),
    )
    order = ("nblk",) + _PLAN_PER_BLOCK + _PLAN_PER_RUN
    return jnp.concatenate([tables[k].astype(I32) for k in order]), n_blk, n_tiles


def _layer(x2d, mem2d, batch, seq, norm_mix_g, w_in, conv_w, conv_b, head_norm_g, w_out,
           norm_xa_g, norm_mem_g, w_q, w_kv, w_o, norm_ffn_g, w_rg, b_rg, w_re, b_re,
           w_gate, w_up, w_down, out_norm_g):
    t = x2d.shape[0]
    n_tiles = t // TRUNK_ROWS
    row = lambda v: v.reshape(1, -1)
    hg = head_norm_g.reshape(-1)
    gm = _group_mean_matrix()

    kv = _kv_proj(mem2d, row(norm_mem_g), w_kv)
    conv_n, uf = _mixer_in(x2d, row(norm_mix_g), w_in.astype(BF16), conv_w, row(conv_b),
                           row(hg[:CONV_CH]), gm, batch, seq)
    fft_n = _fourier(uf.reshape(batch, seq, FFT_CH), _fft_stage2_matrices(seq), _fft_channel_matrix(seq),
                     gm, row(hg[CONV_CH:]), batch, seq)
    w_r_t, b_r = _router_params(w_rg, b_rg, w_re, b_re)
    x2, xs_loc, pos, gates, cnt = _trunk(
        x2d, conv_n, fft_n, w_out.astype(BF16), row(norm_xa_g), w_q.astype(BF16), kv, w_o.astype(BF16),
        row(norm_ffn_g), w_r_t, b_r, _strict_upper(TRUNK_ROWS), _strict_lower(N_EXPERTS), seq)

    max_rows = n_tiles * LOCAL_ROWS + N_EXPERTS * (EXPERT_PAD - GRANULE)
    n_global_rows = -(-max_rows // EXPERT_PAD) * EXPERT_PAD
    plan_table, n_blk, max_runs = _plan(cnt[:, :, 0], n_global_rows)
    ys_loc = _experts(plan_table, n_blk, max_runs, xs_loc, w_gate, w_up, w_down)
    return _combine(x2, pos, gates, row(out_norm_g), ys_loc)


def kernel(x, mem, norm_mix_g, w_in, conv_w, conv_b, head_norm_g, w_out, norm_xa_g, norm_mem_g, w_q, w_kv,
           w_o, norm_ffn_g, w_route_group, b_route_group, w_route_expert, b_route_expert, w_gate, w_up,
           w_down, final_norm_g):
    batch, seq, _ = x.shape
    depth = norm_mix_g.shape[0]
    assert depth == 1, "the final norm is fused into the last layer's combine kernel"
    x2d = x.reshape(batch * seq, D_MODEL)
    mem2d = mem.reshape(batch * MEM_LEN, D_MODEL)
    l = 0
    out = _layer(x2d, mem2d, batch, seq, norm_mix_g[l], w_in[l], conv_w[l], conv_b[l], head_norm_g[l],
                 w_out[l], norm_xa_g[l], norm_mem_g[l], w_q[l], w_kv[l], w_o[l], norm_ffn_g[l],
                 w_route_group[l], b_route_group[l], w_route_expert[l], b_route_expert[l],
                 w_gate[l], w_up[l], w_down[l], final_norm_g)
    return out.reshape(batch, seq, D_MODEL)
```

```python
import functools
import math

import numpy as np
import jax
import jax.numpy as jnp
from jax import lax
from jax.experimental import pallas as pl
from jax.experimental.pallas import tpu as pltpu

F32 = jnp.float32
BF16 = jnp.bfloat16
I32 = jnp.int32

D_MODEL = 1024
HALF = D_MODEL // 2
HEAD_DIM = 64
CONV_CH = 512
FFT_CH = 512
IN_COLS = 3 * CONV_CH + FFT_CH
MEM_LEN = 256
XA_HEADS = 4
XA_HEAD_DIM = D_MODEL // XA_HEADS
N_GROUPS = 4
EXPERTS_PER_GROUP = 8
N_EXPERTS = 32
TOP_K = 2
D_EXPERT = 512
EPS = 1e-6

FFT_N1 = 16
FFT_N2 = 256
FFT_K1_PER_STEP = 4

LANES = 128
MXU_COLS = 256
SUBLANES = 8
GRANULE = SUBLANES
GRANULE_SHIFT = GRANULE.bit_length() - 1
MIX_ROWS = 1024
TRUNK_ROWS = 512
LOCAL_ROWS = TOP_K * TRUNK_ROWS + N_EXPERTS * GRANULE
COMBINE_TILES = 2
EXPERT_PAD = 128
EXPERT_ROWS = 1024
X_SLOTS = 3
W_SLOTS = 3
ROUTER_ROWS = 128
EXPERT_LOGIT_ROW0 = SUBLANES
NEG_BIG = -1e30
HI16 = -65536

VMEM_LIMIT = 56 * 1024 * 1024


def _rms(x, g):
    return x * lax.rsqrt(jnp.mean(x * x, axis=-1, keepdims=True) + EPS) * g


def _dot(a, b):
    return jnp.dot(a, b, preferred_element_type=F32)


def _dot_nt(a, b):
    return lax.dot_general(a, b, (((1,), (1,)), ((), ())), preferred_element_type=F32)


def _dot_tn(a, b):
    return lax.dot_general(a, b, (((0,), (0,)), ((), ())), preferred_element_type=F32)


def _pack_halves(left_f32, right_f32):
    lb = lax.bitcast_convert_type(left_f32, I32)
    rb = lax.shift_right_logical(lax.bitcast_convert_type(right_f32, I32), jnp.int32(16))
    return lb | rb


def _unpack_halves(packed_i32):
    left = lax.bitcast_convert_type(packed_i32 & jnp.int32(HI16), F32)
    right = lax.bitcast_convert_type(lax.shift_left(packed_i32, jnp.int32(16)), F32)
    return left.astype(BF16), right.astype(BF16)


def _group_mean_matrix():
    g = np.kron(np.eye(MXU_COLS // HEAD_DIM), np.full((HEAD_DIM, HEAD_DIM), 1.0 / HEAD_DIM))
    return jnp.asarray(g, dtype=BF16)


def _head_mean_square(y, gm):
    sq = (y * y).astype(BF16)
    return jnp.concatenate([_dot(sq[:, c:c + MXU_COLS], gm) for c in range(0, y.shape[1], MXU_COLS)], axis=1)


def _fft_stage2_matrices(seq):
    k1 = np.arange(FFT_N1)[:, None, None]
    k2 = np.arange(FFT_N2)[None, :, None]
    s2 = np.arange(FFT_N2)[None, None, :]
    ang = 2.0 * np.pi * ((s2 * (k1 + FFT_N1 * k2)) % seq) / seq
    c, s = np.cos(ang), np.sin(ang)
    top = np.concatenate([c, s], axis=2)
    bot = np.concatenate([-s, c], axis=2)
    return jnp.asarray(np.concatenate([top, bot], axis=1), dtype=BF16)


def _fft_channel_matrix(seq):
    c = np.arange(HEAD_DIM)
    ang = 2.0 * np.pi * ((c[:, None] * c[None, :]) % HEAD_DIM) / HEAD_DIM
    scale = 1.0 / math.sqrt(seq * HEAD_DIM)
    eye = np.eye(MXU_COLS // HEAD_DIM)
    cs = np.concatenate([np.kron(eye, np.cos(ang)), np.kron(eye, np.sin(ang))], axis=0) * scale
    return jnp.asarray(cs, dtype=BF16)


def _strict_upper(n):
    return jnp.asarray(np.triu(np.ones((n, n)), k=1), dtype=BF16)


def _strict_lower(n):
    return jnp.asarray(np.tril(np.ones((n, n)), k=-1), dtype=BF16)


def _kv_kernel(mem_ref, g_ref, w_ref, o_ref):
    h = _rms(mem_ref[...], g_ref[...]).astype(BF16)
    o_ref[...] = _dot(h, w_ref[...].astype(BF16)).astype(BF16)


def _kv_proj(mem2d, g, w_kv):
    rows = mem2d.shape[0]
    cols = w_kv.shape[1]
    cb = 512
    return pl.pallas_call(
        _kv_kernel,
        grid=(cols // cb,),
        in_specs=[
            pl.BlockSpec((rows, D_MODEL), lambda j: (0, 0)),
            pl.BlockSpec((1, D_MODEL), lambda j: (0, 0)),
            pl.BlockSpec((D_MODEL, cb), lambda j: (0, j)),
        ],
        out_specs=pl.BlockSpec((rows, cb), lambda j: (0, j)),
        out_shape=jax.ShapeDtypeStruct((rows, cols), BF16),
        compiler_params=pltpu.CompilerParams(vmem_limit_bytes=VMEM_LIMIT),
        name="kv_proj",
    )(mem2d, g, w_kv)


def _mixer_in_kernel(x_ref, xp_ref, xn_ref, g_ref, w_ref, cw_ref, cb_ref, hg_ref, gm_ref,
                     conv_ref, uf_ref):
    i = pl.program_id(1)
    n_i = pl.num_programs(1)
    rows = x_ref.shape[0]
    g = g_ref[...]
    h = _rms(x_ref[...], g).astype(BF16)
    u_cv = _dot(h, w_ref[:, CONV_CH:3 * CONV_CH])
    cv = u_cv[:, :CONV_CH] * u_cv[:, CONV_CH:]

    hh = jnp.concatenate([_rms(xp_ref[...], g), _rms(xn_ref[...], g)], axis=0).astype(BF16)
    uh = _dot(hh, w_ref[:, CONV_CH:3 * CONV_CH])
    cvh = uh[:, :CONV_CH] * uh[:, CONV_CH:]
    cv_prev = cvh[SUBLANES - 1:SUBLANES, :] * jnp.where(i == 0, 0.0, 1.0)
    cv_next = cvh[SUBLANES:SUBLANES + 1, :] * jnp.where(i == n_i - 1, 0.0, 1.0)

    row = lax.broadcasted_iota(I32, cv.shape, 0)
    cv_up = jnp.where(row == 0, cv_prev, pltpu.roll(cv, 1, 0))
    cv_dn = jnp.where(row == rows - 1, cv_next, pltpu.roll(cv, rows - 1, 0))
    z = cw_ref[0:1, :] * cv_up + cw_ref[1:2, :] * cv + cw_ref[2:3, :] * cv_dn + cb_ref[...]
    uf_ref[...] = _dot(h, w_ref[:, 3 * CONV_CH:]).astype(BF16)
    y = _dot(h, w_ref[:, :CONV_CH]) * z
    ms = _head_mean_square(y, gm_ref[...])
    conv_ref[...] = (y * lax.rsqrt(ms + EPS) * hg_ref[...]).astype(BF16)


def _mixer_in(x2d, g, w_in, conv_w, conv_b, hg_conv, gm, batch, seq):
    n_i = seq // MIX_ROWS
    t = x2d.shape[0]
    r8 = MIX_ROWS // SUBLANES
    last8 = t // SUBLANES - 1
    return pl.pallas_call(
        _mixer_in_kernel,
        grid=(batch, n_i),
        in_specs=[
            pl.BlockSpec((MIX_ROWS, D_MODEL), lambda b, i: (b * n_i + i, 0)),
            pl.BlockSpec((SUBLANES, D_MODEL), lambda b, i: (jnp.maximum((b * n_i + i) * r8 - 1, 0), 0)),
            pl.BlockSpec((SUBLANES, D_MODEL), lambda b, i: (jnp.minimum((b * n_i + i + 1) * r8, last8), 0)),
            pl.BlockSpec((1, D_MODEL), lambda b, i: (0, 0)),
            pl.BlockSpec((D_MODEL, IN_COLS), lambda b, i: (0, 0)),
            pl.BlockSpec((3, CONV_CH), lambda b, i: (0, 0)),
            pl.BlockSpec((1, CONV_CH), lambda b, i: (0, 0)),
            pl.BlockSpec((1, CONV_CH), lambda b, i: (0, 0)),
            pl.BlockSpec((MXU_COLS, MXU_COLS), lambda b, i: (0, 0)),
        ],
        out_specs=[
            pl.BlockSpec((MIX_ROWS, CONV_CH), lambda b, i: (b * n_i + i, 0)),
            pl.BlockSpec((MIX_ROWS, FFT_CH), lambda b, i: (b * n_i + i, 0)),
        ],
        out_shape=[
            jax.ShapeDtypeStruct((t, CONV_CH), BF16),
            jax.ShapeDtypeStruct((t, FFT_CH), BF16),
        ],
        compiler_params=pltpu.CompilerParams(vmem_limit_bytes=VMEM_LIMIT),
        name="mixer_in",
    )(x2d, x2d, x2d, g, w_in, conv_w, conv_b, hg_conv, gm)


_S1_ROWS = 16
_S1_LANES = 128


def _lincomb(terms):
    acc = None
    for coef, val in terms:
        if abs(coef) < 1e-12:
            continue
        if abs(coef - 1.0) < 1e-12:
            term, neg = val, False
        elif abs(coef + 1.0) < 1e-12:
            term, neg = val, True
        else:
            term, neg = coef * val, False
        if acc is None:
            acc = -term if neg else term
        else:
            acc = acc - term if neg else acc + term
    return acc


def _fft_stage1(x_ref, a_ref):
    half = FFT_N1 // 2
    cos = [[math.cos(2 * math.pi * ((k * j) % FFT_N1) / FFT_N1) for j in range(FFT_N1)] for k in range(FFT_N1)]
    sin = [[math.sin(2 * math.pi * ((k * j) % FFT_N1) / FFT_N1) for j in range(FFT_N1)] for k in range(FFT_N1)]

    def body(r, carry):
        r0 = pl.multiple_of(r * _S1_ROWS, _S1_ROWS)
        rows_re = pl.ds(r0, _S1_ROWS)
        rows_im = pl.ds(r0 + FFT_N2, _S1_ROWS)
        for lc in range(0, FFT_CH, _S1_LANES):
            lanes = slice(lc, lc + _S1_LANES)
            xs = [x_ref[j, rows_re, lanes].astype(F32) for j in range(FFT_N1)]
            ev = [None] + [xs[j] + xs[FFT_N1 - j] for j in range(1, half)]
            od = [None] + [xs[j] - xs[FFT_N1 - j] for j in range(1, half)]
            for k in range(half + 1):
                re = _lincomb([(1.0, xs[0]), (cos[k][half], xs[half])]
                              + [(cos[k][j], ev[j]) for j in range(1, half)])
                a_ref[k, rows_re, lanes] = re.astype(BF16)
                if k in (0, half):
                    zero = jnp.zeros_like(re).astype(BF16)
                    a_ref[k, rows_im, lanes] = zero
                else:
                    im = _lincomb([(-sin[k][j], od[j]) for j in range(1, half)])
                    a_ref[k, rows_im, lanes] = im.astype(BF16)
                    a_ref[FFT_N1 - k, rows_re, lanes] = re.astype(BF16)
                    a_ref[FFT_N1 - k, rows_im, lanes] = (-im).astype(BF16)
        return carry

    lax.fori_loop(0, FFT_N2 // _S1_ROWS, body, 0)


def _fourier_kernel(x_ref, m2_ref, cs_ref, gm_ref, hg_ref, o_ref, a_ref, y_ref):
    j = pl.program_id(1)

    @pl.when(j == 0)
    def _():
        _fft_stage1(x_ref, a_ref)

    ris = [_dot(m2_ref[kk], a_ref[j * FFT_K1_PER_STEP + kk]) for kk in range(FFT_K1_PER_STEP)]
    re = jnp.concatenate([ri[:FFT_N2] for ri in ris], axis=0).astype(BF16)
    im = jnp.concatenate([ri[FFT_N2:] for ri in ris], axis=0).astype(BF16)
    y = jnp.concatenate(
        [_dot(re[:, c:c + MXU_COLS], cs_ref[:MXU_COLS, :]) + _dot(im[:, c:c + MXU_COLS], cs_ref[MXU_COLS:, :])
         for c in range(0, FFT_CH, MXU_COLS)], axis=1)
    yn = y * lax.rsqrt(_head_mean_square(y, gm_ref[...]) + EPS) * hg_ref[...]
    for kk in range(FFT_K1_PER_STEP):
        k1 = j * FFT_K1_PER_STEP + kk
        for c in range(FFT_CH // LANES):
            y_ref[c, pl.ds(k1, FFT_N2, stride=FFT_N1), :] = yn[kk * FFT_N2:(kk + 1) * FFT_N2,
                                                               c * LANES:(c + 1) * LANES]

    @pl.when(j == pl.num_programs(1) - 1)
    def _():
        for c in range(FFT_CH // LANES):
            o_ref[:, c * LANES:(c + 1) * LANES] = y_ref[c].astype(BF16)


def _fourier(uf, m2, cs, gm, hg_fft, batch, seq):
    assert seq == FFT_N1 * FFT_N2
    x4 = uf.reshape(batch, FFT_N1, FFT_N2, FFT_CH)
    out = pl.pallas_call(
        _fourier_kernel,
        grid=(batch, FFT_N1 // FFT_K1_PER_STEP),
        in_specs=[
            pl.BlockSpec((None, FFT_N1, FFT_N2, FFT_CH), lambda b, j: (b, 0, 0, 0)),
            pl.BlockSpec((FFT_K1_PER_STEP, 2 * FFT_N2, 2 * FFT_N2), lambda b, j: (j, 0, 0)),
            pl.BlockSpec((2 * MXU_COLS, MXU_COLS), lambda b, j: (0, 0)),
            pl.BlockSpec((MXU_COLS, MXU_COLS), lambda b, j: (0, 0)),
            pl.BlockSpec((1, FFT_CH), lambda b, j: (0, 0)),
        ],
        out_specs=pl.BlockSpec((seq, FFT_CH), lambda b, j: (b, 0)),
        out_shape=jax.ShapeDtypeStruct((batch * seq, FFT_CH), BF16),
        scratch_shapes=[
            pltpu.VMEM((FFT_N1, 2 * FFT_N2, FFT_CH), BF16),
            pltpu.VMEM((FFT_CH // LANES, seq, LANES), F32),
        ],
        compiler_params=pltpu.CompilerParams(
            dimension_semantics=("arbitrary", "arbitrary"), vmem_limit_bytes=VMEM_LIMIT),
        name="fourier",
    )(x4, m2, cs, gm, hg_fft)
    return out


def _first_index_of_max(vals, vmax, row):
    return jnp.min(jnp.where(vals == vmax, row, vals.shape[0]), axis=0, keepdims=True)


def _route(lg):
    cols = lg.shape[1]
    row8 = lax.broadcasted_iota(I32, (EXPERTS_PER_GROUP, cols), 0)
    gl = lg[0:EXPERT_LOGIT_ROW0, :]
    gmax = jnp.max(gl, axis=0, keepdims=True)
    g_w = 1.0 / jnp.sum(jnp.exp(gl - gmax), axis=0, keepdims=True)
    g_idx = _first_index_of_max(gl, gmax, row8)

    el = lg[EXPERT_LOGIT_ROW0:EXPERT_LOGIT_ROW0 + EXPERTS_PER_GROUP, :]
    for g in range(1, N_GROUPS):
        row0 = EXPERT_LOGIT_ROW0 + g * EXPERTS_PER_GROUP
        el = jnp.where(g_idx == g, lg[row0:row0 + EXPERTS_PER_GROUP, :], el)
    emax = jnp.max(el, axis=0, keepdims=True)
    ee = jnp.exp(el - emax)
    e_prob = ee / jnp.sum(ee, axis=0, keepdims=True)
    p1 = jnp.max(e_prob, axis=0, keepdims=True)
    i1 = _first_index_of_max(e_prob, p1, row8)
    rest = jnp.where(row8 == i1, -1.0, e_prob)
    p2 = jnp.max(rest, axis=0, keepdims=True)
    i2 = _first_index_of_max(rest, p2, row8)
    denom = p1 + p2
    e1 = g_idx * EXPERTS_PER_GROUP + i1
    e2 = g_idx * EXPERTS_PER_GROUP + i2
    return e1, e2, g_w * p1 / denom, g_w * p2 / denom


def _trunk_kernel(x_ref, conv_ref, fft_ref, wout_ref, gxa_ref, wq_ref, k_ref, v_ref, wo_ref,
                  gffn_ref, wr_ref, br_ref, tri_ref, ltri_ref,
                  x2_ref, xs_ref, pos_ref, gate_ref, cnt_ref, h3_s, lg_s):
    @pl.when(pl.program_id(0) == 0)
    def _():
        h3_s[...] = jnp.zeros_like(h3_s)
        lg_s[...] = jnp.zeros_like(lg_s)

    h3 = h3_s[...]
    lg = lg_s[...]
    n_tok = lg.shape[1]

    x1 = x_ref[...] + _dot(jnp.concatenate([conv_ref[...], fft_ref[...]], axis=1), wout_ref[...])

    e1, e2, gate1, gate2 = _route(lg)
    row32 = lax.broadcasted_iota(I32, (N_EXPERTS, n_tok), 0)
    hit1 = row32 == e1
    hit2 = row32 == e2
    onehot = jnp.where(hit1 | hit2, 1.0, 0.0)
    before = _dot(onehot.astype(BF16), tri_ref[...])
    cnt = jnp.sum(onehot, axis=1, keepdims=True).astype(I32)
    piece = jnp.left_shift(jnp.right_shift(cnt + (GRANULE - 1), GRANULE_SHIFT), GRANULE_SHIFT)
    piece_b = jnp.broadcast_to(piece.astype(F32), (N_EXPERTS, LANES)).astype(BF16)
    start = _dot(ltri_ref[...], piece_b)[:, 0:1]
    slot = before + start
    pos1 = jnp.sum(jnp.where(hit1, slot, 0.0), axis=0, keepdims=True).astype(I32)
    pos2 = jnp.sum(jnp.where(hit2, slot, 0.0), axis=0, keepdims=True).astype(I32)
    pos_ref[0:1, :] = pos1
    pos_ref[1:2, :] = pos2
    pos_ref[TOP_K:, :] = jnp.zeros((SUBLANES - TOP_K, n_tok), I32)
    gate_ref[0:1, :] = gate1
    gate_ref[1:2, :] = gate2
    gate_ref[TOP_K:, :] = jnp.zeros((SUBLANES - TOP_K, n_tok), F32)
    cnt_ref[...] = jnp.broadcast_to(cnt, (N_EXPERTS, LANES))

    h2 = _rms(x1, gxa_ref[...]).astype(BF16)
    q = _dot(h2, wq_ref[...]).astype(BF16)

    r = lax.broadcasted_iota(I32, (LOCAL_ROWS, n_tok), 0)
    perm = jnp.where((r == pos1) | (r == pos2), 1.0, 0.0).astype(BF16)

    outs = []
    for hd in range(XA_HEADS):
        cols = slice(hd * XA_HEAD_DIM, (hd + 1) * XA_HEAD_DIM)
        s = _dot_nt(q[:, cols], k_ref[:, cols]) * (XA_HEAD_DIM ** -0.5)
        s = s - jnp.max(s, axis=-1, keepdims=True)
        p = jnp.exp(s)
        p = p / jnp.sum(p, axis=-1, keepdims=True)
        outs.append(_dot(p.astype(BF16), v_ref[:, cols]).astype(BF16))
        if hd == 1:
            sorted_left = _dot(perm, h3[:, :HALF])

    o = jnp.concatenate(outs, axis=-1)
    x2 = x1 + _dot(o, wo_ref[...])
    x2_ref[...] = x2
    sorted_right = _dot(perm, h3[:, HALF:])
    h3_next = _rms(x2, gffn_ref[...]).astype(BF16)
    xs_ref[...] = _pack_halves(sorted_left, sorted_right)
    h3_s[...] = h3_next
    lg_s[...] = _dot_nt(wr_ref[...], h3_next) + br_ref[...]


def _trunk(x2d, conv_n, fft_n, w_out, g_xa, w_q, kv, w_o, g_ffn, w_r_t, b_r, tri, ltri, seq):
    t = x2d.shape[0]
    n_tiles = t // TRUNK_ROWS
    n_per_batch = seq // TRUNK_ROWS
    const = lambda i: (0, 0)
    dense = lambda i: jnp.minimum(i, n_tiles - 1)
    routed = lambda i: jnp.maximum(i - 1, 0)
    return pl.pallas_call(
        _trunk_kernel,
        grid=(n_tiles + 1,),
        in_specs=[
            pl.BlockSpec((TRUNK_ROWS, D_MODEL), lambda i: (dense(i), 0)),
            pl.BlockSpec((TRUNK_ROWS, CONV_CH), lambda i: (dense(i), 0)),
            pl.BlockSpec((TRUNK_ROWS, FFT_CH), lambda i: (dense(i), 0)),
            pl.BlockSpec((D_MODEL, D_MODEL), const),
            pl.BlockSpec((1, D_MODEL), const),
            pl.BlockSpec((D_MODEL, D_MODEL), const),
            pl.BlockSpec((MEM_LEN, D_MODEL), lambda i: (dense(i) // n_per_batch, 0)),
            pl.BlockSpec((MEM_LEN, D_MODEL), lambda i: (dense(i) // n_per_batch, 1)),
            pl.BlockSpec((D_MODEL, D_MODEL), const),
            pl.BlockSpec((1, D_MODEL), const),
            pl.BlockSpec((ROUTER_ROWS, D_MODEL), const),
            pl.BlockSpec((ROUTER_ROWS, 1), const),
            pl.BlockSpec((TRUNK_ROWS, TRUNK_ROWS), const),
            pl.BlockSpec((N_EXPERTS, N_EXPERTS), const),
        ],
        out_specs=[
            pl.BlockSpec((TRUNK_ROWS, D_MODEL), lambda i: (dense(i), 0)),
            pl.BlockSpec((LOCAL_ROWS, HALF), lambda i: (routed(i), 0)),
            pl.BlockSpec((SUBLANES, TRUNK_ROWS), lambda i: (0, routed(i))),
            pl.BlockSpec((SUBLANES, TRUNK_ROWS), lambda i: (0, routed(i))),
            pl.BlockSpec((None, N_EXPERTS, LANES), lambda i: (routed(i), 0, 0)),
        ],
        out_shape=[
            jax.ShapeDtypeStruct((t, D_MODEL), F32),
            jax.ShapeDtypeStruct((n_tiles * LOCAL_ROWS, HALF), I32),
            jax.ShapeDtypeStruct((SUBLANES, t), I32),
            jax.ShapeDtypeStruct((SUBLANES, t), F32),
            jax.ShapeDtypeStruct((n_tiles, N_EXPERTS, LANES), I32),
        ],
        scratch_shapes=[
            pltpu.VMEM((TRUNK_ROWS, D_MODEL), BF16),
            pltpu.VMEM((ROUTER_ROWS, TRUNK_ROWS), F32),
        ],
        compiler_params=pltpu.CompilerParams(
            dimension_semantics=("arbitrary",), vmem_limit_bytes=VMEM_LIMIT),
        name="trunk",
    )(x2d, conv_n, fft_n, w_out, g_xa, w_q, kv, kv, w_o, g_ffn, w_r_t, b_r, tri, ltri)


_PLAN_PER_BLOCK = ("blk_e", "first", "next_e", "next1_e", "slot", "nrun", "ngran")
_PLAN_PER_RUN = ("run_src", "run_off", "run_len")


def _plan_layout(n_blk, max_runs):
    sizes = [("nblk", 1)] + [(k, n_blk) for k in _PLAN_PER_BLOCK] + [(k, n_blk * max_runs) for k in _PLAN_PER_RUN]
    offsets, at = {}, 0
    for name, size in sizes:
        offsets[name] = at
        at += size
    return offsets


class _TableView:
    def __init__(self, ref, offset):
        self._ref, self._offset = ref, offset

    def __getitem__(self, i):
        return self._ref[self._offset + i]


def _experts_kernel(plan_ref, xs_hbm, wg_hbm, wu_hbm, wd_hbm, ys_hbm,
                    wg_buf, wu_buf, wd_buf, xbuf, obuf, wsems, xsems, osems, *, n_blk, max_runs):
    layout = _plan_layout(n_blk, max_runs)
    (blk_e_ref, first_ref, next_e_ref, slot_ref, nblk_ref, run_src_ref, run_off_ref, run_len_ref, nrun_ref,
     ngran_ref) = (_TableView(plan_ref, layout[k]) for k in (
         "blk_e", "first", "next_e", "slot", "nblk", "run_src", "run_off", "run_len", "nrun", "ngran"))
    second_e = _TableView(plan_ref, layout["next1_e"])[0]
    _experts_body(blk_e_ref, first_ref, next_e_ref, slot_ref, nblk_ref,
                  run_src_ref, run_off_ref, run_len_ref, nrun_ref, ngran_ref, n_blk - 1, max_runs, second_e,
                  xs_hbm, wg_hbm, wu_hbm, wd_hbm, ys_hbm,
                  wg_buf, wu_buf, wd_buf, xbuf, obuf, wsems, xsems, osems)


def _experts_body(blk_e_ref, first_ref, next_e_ref, slot_ref, nblk_ref,
                  run_src_ref, run_off_ref, run_len_ref, nrun_ref, ngran_ref, last, max_runs, second_e,
                  xs_hbm, wg_hbm, wu_hbm, wd_hbm, ys_hbm,
                  wg_buf, wu_buf, wd_buf, xbuf, obuf, wsems, xsems, osems):
    n = nblk_ref[0]
    gpb = EXPERT_ROWS // GRANULE

    def fetch_weights(e, s):
        return (pltpu.make_async_copy(wg_hbm.at[e], wg_buf.at[s], wsems.at[0, s]),
                pltpu.make_async_copy(wu_hbm.at[e], wu_buf.at[s], wsems.at[1, s]),
                pltpu.make_async_copy(wd_hbm.at[e], wd_buf.at[s], wsems.at[2, s]))

    def run_copies(b, r, s):
        k = b * max_runs + r
        length = run_len_ref[k]
        hbm_rows = pl.ds(run_src_ref[k], length)
        blk_rows = pl.ds(run_off_ref[k], length)
        return (pltpu.make_async_copy(xs_hbm.at[hbm_rows], xbuf.at[s, blk_rows], xsems.at[s]),
                pltpu.make_async_copy(obuf.at[s, blk_rows], ys_hbm.at[hbm_rows], osems.at[s]))

    def start_in(b, s, runs):
        lax.fori_loop(0, runs, lambda r, c: (run_copies(b, r, s)[0].start(), c)[1], 0)

    def start_out(b, s):
        lax.fori_loop(0, nrun_ref[b], lambda r, c: (run_copies(b, r, s)[1].start(), c)[1], 0)

    def wait_in(b, s):
        count = ngran_ref[b]
        pltpu.make_async_copy(xs_hbm.at[pl.ds(0, count)], xbuf.at[s, pl.ds(0, count)], xsems.at[s]).wait()

    def wait_out(b, s):
        count = ngran_ref[b]
        pltpu.make_async_copy(obuf.at[s, pl.ds(0, count)], ys_hbm.at[pl.ds(0, count)], osems.at[s]).wait()

    xbuf[...] = jnp.zeros_like(xbuf)
    for cp in fetch_weights(blk_e_ref[0], 0):
        cp.start()

    @pl.when(second_e >= 0)
    def _():
        for cp in fetch_weights(second_e, 1):
            cp.start()
    start_in(0, 0, nrun_ref[0])
    start_in(1, 1, jnp.where(n > 1, nrun_ref[1], 0))

    def block(i, carry):
        xs = lax.rem(i, X_SLOTS)
        os = i % 2
        ws = slot_ref[i]

        @pl.when(first_ref[i] == 1)
        def _():
            for cp in fetch_weights(blk_e_ref[i], ws):
                cp.wait()

            @pl.when(next_e_ref[i] >= 0)
            def _():
                for cp in fetch_weights(next_e_ref[i], lax.rem(ws + W_SLOTS - 1, W_SLOTS)):
                    cp.start()

        @pl.when(i >= 2)
        def _():
            wait_out(i - 2, os)

        ahead = jnp.minimum(i + 2, last)
        start_in(ahead, lax.rem(i + 2, X_SLOTS), jnp.where(i + 2 < n, nrun_ref[ahead], 0))

        wait_in(i, xs)

        def mlp(rows):
            granules = rows // GRANULE
            xl, xr = _unpack_halves(xbuf[xs, :granules].reshape(rows, HALF))
            x = jnp.concatenate([xl, xr], axis=1)
            a = _dot(x, wg_buf[ws].astype(BF16))
            b = _dot(x, wu_buf[ws].astype(BF16))
            hmid = (a * jax.nn.sigmoid(a) * b).astype(BF16)
            y = _dot(hmid, wd_buf[ws].astype(BF16))
            packed = _pack_halves(y[:, :HALF].astype(BF16).astype(F32), y[:, HALF:].astype(BF16).astype(F32))
            obuf[os, :granules] = packed.reshape(granules, GRANULE, HALF)

        pads = -(-ngran_ref[i] // (EXPERT_PAD // GRANULE))
        for k in range(1, EXPERT_ROWS // EXPERT_PAD + 1):
            pl.when(pads == k)(functools.partial(mlp, k * EXPERT_PAD))
        start_out(i, os)
        return carry

    lax.fori_loop(0, n, block, 0)

    @pl.when(n >= 2)
    def _():
        wait_out(n - 2, n % 2)
    wait_out(n - 1, (n - 1) % 2)


def _experts(plan_table, n_blk, max_runs, xs_loc, w_gate, w_up, w_down):
    granules = xs_loc.reshape(-1, GRANULE, HALF)
    hbm = pl.BlockSpec(memory_space=pl.ANY)
    return pl.pallas_call(
        functools.partial(_experts_kernel, n_blk=n_blk, max_runs=max_runs),
        grid_spec=pltpu.PrefetchScalarGridSpec(
            num_scalar_prefetch=1,
            grid=(1,),
            in_specs=[hbm, hbm, hbm, hbm],
            out_specs=hbm,
            scratch_shapes=[
                pltpu.VMEM((W_SLOTS, D_MODEL, D_EXPERT), F32),
                pltpu.VMEM((W_SLOTS, D_MODEL, D_EXPERT), F32),
                pltpu.VMEM((W_SLOTS, D_EXPERT, D_MODEL), F32),
                pltpu.VMEM((X_SLOTS, EXPERT_ROWS // GRANULE, GRANULE, HALF), I32),
                pltpu.VMEM((2, EXPERT_ROWS // GRANULE, GRANULE, HALF), I32),
                pltpu.SemaphoreType.DMA((3, W_SLOTS)),
                pltpu.SemaphoreType.DMA((X_SLOTS,)),
                pltpu.SemaphoreType.DMA((2,)),
            ],
        ),
        out_shape=jax.ShapeDtypeStruct(granules.shape, I32),
        input_output_aliases={1: 0},
        compiler_params=pltpu.CompilerParams(
            dimension_semantics=("arbitrary",), vmem_limit_bytes=VMEM_LIMIT),
        name="experts",
    )(plan_table, granules, w_gate, w_up, w_down).reshape(xs_loc.shape)


def _combine_kernel(x2_ref, pos_ref, gate_ref, g_ref, ys_ref, o_ref):
    r = lax.broadcasted_iota(I32, (LOCAL_ROWS, TRUNK_ROWS), 0)
    for tile in range(COMBINE_TILES):
        toks = slice(tile * TRUNK_ROWS, (tile + 1) * TRUNK_ROWS)
        w_t = (jnp.where(r == pos_ref[0:1, toks], gate_ref[0:1, toks], 0.0)
               + jnp.where(r == pos_ref[1:2, toks], gate_ref[1:2, toks], 0.0)).astype(BF16)
        yl, yr = _unpack_halves(ys_ref[tile * LOCAL_ROWS:(tile + 1) * LOCAL_ROWS, :])
        moe = jnp.concatenate([_dot_tn(w_t, yl), _dot_tn(w_t, yr)], axis=-1)
        o_ref[toks, :] = _rms(x2_ref[toks, :] + moe, g_ref[...])


def _combine(x2, pos_tk, gates_tk, g_final, ys_loc):
    t = x2.shape[0]
    rows = COMBINE_TILES * TRUNK_ROWS
    return pl.pallas_call(
        _combine_kernel,
        grid=(t // rows,),
        in_specs=[
            pl.BlockSpec((rows, D_MODEL), lambda i: (i, 0)),
            pl.BlockSpec((SUBLANES, rows), lambda i: (0, i)),
            pl.BlockSpec((SUBLANES, rows), lambda i: (0, i)),
            pl.BlockSpec((1, D_MODEL), lambda i: (0, 0)),
            pl.BlockSpec((COMBINE_TILES * LOCAL_ROWS, HALF), lambda i: (i, 0)),
        ],
        out_specs=pl.BlockSpec((rows, D_MODEL), lambda i: (i, 0)),
        out_shape=jax.ShapeDtypeStruct((t, D_MODEL), F32),
        compiler_params=pltpu.CompilerParams(vmem_limit_bytes=VMEM_LIMIT),
        name="combine",
    )(x2, pos_tk, gates_tk, g_final, ys_loc)


def _router_params(w_rg, b_rg, w_re, b_re):
    w = jnp.zeros((ROUTER_ROWS, D_MODEL), F32)
    w = w.at[0:N_GROUPS].set(w_rg.T).at[EXPERT_LOGIT_ROW0:EXPERT_LOGIT_ROW0 + N_EXPERTS].set(w_re.T)
    b = jnp.zeros((ROUTER_ROWS,), F32)
    b = (b.at[0:N_GROUPS].set(b_rg).at[N_GROUPS:EXPERT_LOGIT_ROW0].set(NEG_BIG)
         .at[EXPERT_LOGIT_ROW0:EXPERT_LOGIT_ROW0 + N_EXPERTS].set(b_re))
    return w.astype(BF16), b.reshape(ROUTER_ROWS, 1)


def _plan(cnt, max_padded_rows):
    n_tiles = cnt.shape[0]
    piece = (cnt + GRANULE - 1) // GRANULE * GRANULE
    lend = jnp.cumsum(piece, axis=1)
    lstart = lend - piece
    tot = jnp.sum(piece, axis=0)
    padded = (tot + EXPERT_PAD - 1) // EXPERT_PAD * EXPERT_PAD
    cum_tiles = jnp.cumsum(piece, axis=0)

    per_expert = (padded + EXPERT_ROWS - 1) // EXPERT_ROWS
    blk_end = jnp.cumsum(per_expert)
    pads_per_blk = EXPERT_ROWS // EXPERT_PAD
    n_blk = (max_padded_rows // EXPERT_PAD + N_EXPERTS * (pads_per_blk - 1)) // pads_per_blk + 1
    blk = jnp.arange(n_blk, dtype=I32)
    blk_e = jnp.minimum(jnp.sum((blk_end[None, :] <= blk[:, None]).astype(I32), axis=1), N_EXPERTS - 1)
    nblk = blk_end[-1:]
    valid = blk < nblk

    of_blk_e = blk_e[:, None] == jnp.arange(N_EXPERTS, dtype=I32)[None, :]
    pick = lambda table: jnp.sum(jnp.where(of_blk_e[:, None, :], table[None, :, :], 0), axis=2)
    pick1 = lambda vec: jnp.sum(jnp.where(of_blk_e, vec[None, :], 0), axis=1)
    seg_off = (blk - pick1(blk_end - per_expert)) * EXPERT_ROWS
    seg_end = jnp.where(valid, jnp.minimum(seg_off + EXPERT_ROWS, pick1(tot)), seg_off)
    ngran = (seg_end - seg_off) // GRANULE
    piece_end = pick(cum_tiles)
    piece_start = piece_end - pick(piece)
    lo = jnp.maximum(piece_start, seg_off[:, None])
    hi = jnp.minimum(piece_end, seg_end[:, None])
    has_run = hi > lo
    tile_base = (jnp.arange(n_tiles, dtype=I32) * LOCAL_ROWS)[None, :] + pick(lstart)
    src_granule = (tile_base + lo - piece_start) // GRANULE
    off_granule = (lo - seg_off[:, None]) // GRANULE
    len_granule = (hi - lo) // GRANULE
    run_of = jnp.cumsum(has_run.astype(I32), axis=1) - 1
    is_run = has_run[:, :, None] & (run_of[:, :, None] == jnp.arange(n_tiles, dtype=I32)[None, None, :])
    compact = lambda v: jnp.sum(jnp.where(is_run, v[:, :, None], 0), axis=1)
    run_src, run_off, run_len = compact(src_granule), compact(off_granule), compact(len_granule)
    nrun = jnp.sum(has_run.astype(I32), axis=1)
    change = jnp.concatenate([jnp.ones((1,), bool), blk_e[1:] != blk_e[:-1]])
    ordinal = jnp.cumsum(change.astype(I32)) - 1
    slot = ordinal % W_SLOTS

    def expert_ahead(k):
        hit = (ordinal[None, :] == ordinal[:, None] + k) & valid[None, :]
        e = jnp.min(jnp.where(hit, blk_e[None, :], N_EXPERTS), axis=1)
        return jnp.where(e == N_EXPERTS, -1, e)

    tables = dict(
        nblk=nblk, blk_e=blk_e, first=change & valid, next_e=expert_ahead(W_SLOTS - 1),
        next1_e=expert_ahead(1), slot=slot, nrun=nrun, ngran=ngran,
        run_src=run_src.reshape(-1), run_off=run_off.reshape(-1), run_len=run_len.reshape(-1),
    )
    order = ("nblk",) + _PLAN_PER_BLOCK + _PLAN_PER_RUN
    return jnp.concatenate([tables[k].astype(I32) for k in order]), n_blk, n_tiles


def _layer(x2d, mem2d, batch, seq, norm_mix_g, w_in, conv_w, conv_b, head_norm_g, w_out,
           norm_xa_g, norm_mem_g, w_q, w_kv, w_o, norm_ffn_g, w_rg, b_rg, w_re, b_re,
           w_gate, w_up, w_down, out_norm_g):
    t = x2d.shape[0]
    n_tiles = t // TRUNK_ROWS
    row = lambda v: v.reshape(1, -1)
    hg = head_norm_g.reshape(-1)
    gm = _group_mean_matrix()

    kv = _kv_proj(mem2d, row(norm_mem_g), w_kv)
    conv_n, uf = _mixer_in(x2d, row(norm_mix_g), w_in.astype(BF16), conv_w, row(conv_b),
                           row(hg[:CONV_CH]), gm, batch, seq)
    fft_n = _fourier(uf.reshape(batch, seq, FFT_CH), _fft_stage2_matrices(seq), _fft_channel_matrix(seq),
                     gm, row(hg[CONV_CH:]), batch, seq)
    w_r_t, b_r = _router_params(w_rg, b_rg, w_re, b_re)
    x2, xs_loc, pos, gates, cnt = _trunk(
        x2d, conv_n, fft_n, w_out.astype(BF16), row(norm_xa_g), w_q.astype(BF16), kv, w_o.astype(BF16),
        row(norm_ffn_g), w_r_t, b_r, _strict_upper(TRUNK_ROWS), _strict_lower(N_EXPERTS), seq)

    max_rows = n_tiles * LOCAL_ROWS + N_EXPERTS * (EXPERT_PAD - GRANULE)
    n_global_rows = -(-max_rows // EXPERT_PAD) * EXPERT_PAD
    plan_table, n_blk, max_runs = _plan(cnt[:, :, 0], n_global_rows)
    ys_loc = _experts(plan_table, n_blk, max_runs, xs_loc, w_gate, w_up, w_down)
    return _combine(x2, pos, gates, row(out_norm_g), ys_loc)


def kernel(x, mem, norm_mix_g, w_in, conv_w, conv_b, head_norm_g, w_out, norm_xa_g, norm_mem_g, w_q, w_kv,
           w_o, norm_ffn_g, w_route_group, b_route_group, w_route_expert, b_route_expert, w_gate, w_up,
           w_down, final_norm_g):
    batch, seq, _ = x.shape
    depth = norm_mix_g.shape[0]
    assert depth == 1, "the final norm is fused into the last layer's combine kernel"
    x2d = x.reshape(batch * seq, D_MODEL)
    mem2d = mem.reshape(batch * MEM_LEN, D_MODEL)
    l = 0
    out = _layer(x2d, mem2d, batch, seq, norm_mix_g[l], w_in[l], conv_w[l], conv_b[l], head_norm_g[l],
                 w_out[l], norm_xa_g[l], norm_mem_g[l], w_q[l], w_kv[l], w_o[l], norm_ffn_g[l],
                 w_route_group[l], b_route_group[l], w_route_expert[l], b_route_expert[l],
                 w_gate[l], w_up[l], w_down[l], final_norm_g)
    return out.reshape(batch, seq, D_MODEL)
```

```python
import functools
import math

import numpy as np
import jax
import jax.numpy as jnp
from jax import lax
from jax.experimental import pallas as pl
from jax.experimental.pallas import tpu as pltpu

F32 = jnp.float32
BF16 = jnp.bfloat16
I32 = jnp.int32

D_MODEL = 1024
HALF = D_MODEL // 2
HEAD_DIM = 64
CONV_CH = 512
FFT_CH = 512
IN_COLS = 3 * CONV_CH + FFT_CH
MEM_LEN = 256
XA_HEADS = 4
XA_HEAD_DIM = D_MODEL // XA_HEADS
N_GROUPS = 4
EXPERTS_PER_GROUP = 8
N_EXPERTS = 32
TOP_K = 2
D_EXPERT = 512
EPS = 1e-6

FFT_N1 = 16
FFT_N2 = 256
FFT_K1_PER_STEP = 4

LANES = 128
MXU_COLS = 256
SUBLANES = 8
GRANULE = SUBLANES
GRANULE_SHIFT = GRANULE.bit_length() - 1
MIX_ROWS = 1024
TRUNK_ROWS = 512
LOCAL_ROWS = TOP_K * TRUNK_ROWS + N_EXPERTS * GRANULE
COMBINE_TILES = 2
EXPERT_PAD = 128
EXPERT_ROWS = 768
X_SLOTS = 3
ROUTER_ROWS = 128
EXPERT_LOGIT_ROW0 = SUBLANES
NEG_BIG = -1e30
HI16 = -65536

VMEM_LIMIT = 56 * 1024 * 1024


def _rms(x, g):
    return x * lax.rsqrt(jnp.mean(x * x, axis=-1, keepdims=True) + EPS) * g


def _dot(a, b):
    return jnp.dot(a, b, preferred_element_type=F32)


def _dot_nt(a, b):
    return lax.dot_general(a, b, (((1,), (1,)), ((), ())), preferred_element_type=F32)


def _dot_tn(a, b):
    return lax.dot_general(a, b, (((0,), (0,)), ((), ())), preferred_element_type=F32)


def _pack_halves(left_f32, right_f32):
    lb = lax.bitcast_convert_type(left_f32, I32)
    rb = lax.shift_right_logical(lax.bitcast_convert_type(right_f32, I32), jnp.int32(16))
    return lb | rb


def _unpack_halves(packed_i32):
    left = lax.bitcast_convert_type(packed_i32 & jnp.int32(HI16), F32)
    right = lax.bitcast_convert_type(lax.shift_left(packed_i32, jnp.int32(16)), F32)
    return left.astype(BF16), right.astype(BF16)


def _group_mean_matrix():
    g = np.kron(np.eye(MXU_COLS // HEAD_DIM), np.full((HEAD_DIM, HEAD_DIM), 1.0 / HEAD_DIM))
    return jnp.asarray(g, dtype=BF16)


def _head_mean_square(y, gm):
    sq = (y * y).astype(BF16)
    return jnp.concatenate([_dot(sq[:, c:c + MXU_COLS], gm) for c in range(0, y.shape[1], MXU_COLS)], axis=1)


def _fft_stage2_matrices(seq):
    k1 = np.arange(FFT_N1)[:, None, None]
    k2 = np.arange(FFT_N2)[None, :, None]
    s2 = np.arange(FFT_N2)[None, None, :]
    ang = 2.0 * np.pi * ((s2 * (k1 + FFT_N1 * k2)) % seq) / seq
    c, s = np.cos(ang), np.sin(ang)
    top = np.concatenate([c, s], axis=2)
    bot = np.concatenate([-s, c], axis=2)
    return jnp.asarray(np.concatenate([top, bot], axis=1), dtype=BF16)


def _fft_channel_matrix(seq):
    c = np.arange(HEAD_DIM)
    ang = 2.0 * np.pi * ((c[:, None] * c[None, :]) % HEAD_DIM) / HEAD_DIM
    scale = 1.0 / math.sqrt(seq * HEAD_DIM)
    eye = np.eye(MXU_COLS // HEAD_DIM)
    cs = np.concatenate([np.kron(eye, np.cos(ang)), np.kron(eye, np.sin(ang))], axis=0) * scale
    return jnp.asarray(cs, dtype=BF16)


def _strict_upper(n):
    return jnp.asarray(np.triu(np.ones((n, n)), k=1), dtype=BF16)


def _strict_lower(n):
    return jnp.asarray(np.tril(np.ones((n, n)), k=-1), dtype=BF16)


def _kv_kernel(mem_ref, g_ref, w_ref, o_ref):
    h = _rms(mem_ref[...], g_ref[...]).astype(BF16)
    o_ref[...] = _dot(h, w_ref[...].astype(BF16)).astype(BF16)


def _kv_proj(mem2d, g, w_kv):
    rows = mem2d.shape[0]
    cols = w_kv.shape[1]
    cb = 512
    return pl.pallas_call(
        _kv_kernel,
        grid=(cols // cb,),
        in_specs=[
            pl.BlockSpec((rows, D_MODEL), lambda j: (0, 0)),
            pl.BlockSpec((1, D_MODEL), lambda j: (0, 0)),
            pl.BlockSpec((D_MODEL, cb), lambda j: (0, j)),
        ],
        out_specs=pl.BlockSpec((rows, cb), lambda j: (0, j)),
        out_shape=jax.ShapeDtypeStruct((rows, cols), BF16),
        compiler_params=pltpu.CompilerParams(vmem_limit_bytes=VMEM_LIMIT),
        name="kv_proj",
    )(mem2d, g, w_kv)


def _mixer_in_kernel(x_ref, xp_ref, xn_ref, g_ref, w_ref, cw_ref, cb_ref, hg_ref, gm_ref,
                     conv_ref, uf_ref):
    i = pl.program_id(1)
    n_i = pl.num_programs(1)
    rows = x_ref.shape[0]
    g = g_ref[...]
    h = _rms(x_ref[...], g).astype(BF16)
    u_cv = _dot(h, w_ref[:, CONV_CH:3 * CONV_CH])
    cv = u_cv[:, :CONV_CH] * u_cv[:, CONV_CH:]

    hh = jnp.concatenate([_rms(xp_ref[...], g), _rms(xn_ref[...], g)], axis=0).astype(BF16)
    uh = _dot(hh, w_ref[:, CONV_CH:3 * CONV_CH])
    cvh = uh[:, :CONV_CH] * uh[:, CONV_CH:]
    cv_prev = cvh[SUBLANES - 1:SUBLANES, :] * jnp.where(i == 0, 0.0, 1.0)
    cv_next = cvh[SUBLANES:SUBLANES + 1, :] * jnp.where(i == n_i - 1, 0.0, 1.0)

    row = lax.broadcasted_iota(I32, cv.shape, 0)
    cv_up = jnp.where(row == 0, cv_prev, pltpu.roll(cv, 1, 0))
    cv_dn = jnp.where(row == rows - 1, cv_next, pltpu.roll(cv, rows - 1, 0))
    z = cw_ref[0:1, :] * cv_up + cw_ref[1:2, :] * cv + cw_ref[2:3, :] * cv_dn + cb_ref[...]
    uf_ref[...] = _dot(h, w_ref[:, 3 * CONV_CH:]).astype(BF16)
    y = _dot(h, w_ref[:, :CONV_CH]) * z
    ms = _head_mean_square(y, gm_ref[...])
    conv_ref[...] = (y * lax.rsqrt(ms + EPS) * hg_ref[...]).astype(BF16)


def _mixer_in(x2d, g, w_in, conv_w, conv_b, hg_conv, gm, batch, seq):
    n_i = seq // MIX_ROWS
    t = x2d.shape[0]
    r8 = MIX_ROWS // SUBLANES
    last8 = t // SUBLANES - 1
    return pl.pallas_call(
        _mixer_in_kernel,
        grid=(batch, n_i),
        in_specs=[
            pl.BlockSpec((MIX_ROWS, D_MODEL), lambda b, i: (b * n_i + i, 0)),
            pl.BlockSpec((SUBLANES, D_MODEL), lambda b, i: (jnp.maximum((b * n_i + i) * r8 - 1, 0), 0)),
            pl.BlockSpec((SUBLANES, D_MODEL), lambda b, i: (jnp.minimum((b * n_i + i + 1) * r8, last8), 0)),
            pl.BlockSpec((1, D_MODEL), lambda b, i: (0, 0)),
            pl.BlockSpec((D_MODEL, IN_COLS), lambda b, i: (0, 0)),
            pl.BlockSpec((3, CONV_CH), lambda b, i: (0, 0)),
            pl.BlockSpec((1, CONV_CH), lambda b, i: (0, 0)),
            pl.BlockSpec((1, CONV_CH), lambda b, i: (0, 0)),
            pl.BlockSpec((MXU_COLS, MXU_COLS), lambda b, i: (0, 0)),
        ],
        out_specs=[
            pl.BlockSpec((MIX_ROWS, CONV_CH), lambda b, i: (b * n_i + i, 0)),
            pl.BlockSpec((MIX_ROWS, FFT_CH), lambda b, i: (b * n_i + i, 0)),
        ],
        out_shape=[
            jax.ShapeDtypeStruct((t, CONV_CH), BF16),
            jax.ShapeDtypeStruct((t, FFT_CH), BF16),
        ],
        compiler_params=pltpu.CompilerParams(vmem_limit_bytes=VMEM_LIMIT),
        name="mixer_in",
    )(x2d, x2d, x2d, g, w_in, conv_w, conv_b, hg_conv, gm)


_S1_ROWS = 16
_S1_LANES = 128


def _lincomb(terms):
    acc = None
    for coef, val in terms:
        if abs(coef) < 1e-12:
            continue
        if abs(coef - 1.0) < 1e-12:
            term, neg = val, False
        elif abs(coef + 1.0) < 1e-12:
            term, neg = val, True
        else:
            term, neg = coef * val, False
        if acc is None:
            acc = -term if neg else term
        else:
            acc = acc - term if neg else acc + term
    return acc


def _fft_stage1(x_ref, a_ref):
    half = FFT_N1 // 2
    cos = [[math.cos(2 * math.pi * ((k * j) % FFT_N1) / FFT_N1) for j in range(FFT_N1)] for k in range(FFT_N1)]
    sin = [[math.sin(2 * math.pi * ((k * j) % FFT_N1) / FFT_N1) for j in range(FFT_N1)] for k in range(FFT_N1)]

    def body(r, carry):
        r0 = pl.multiple_of(r * _S1_ROWS, _S1_ROWS)
        rows_re = pl.ds(r0, _S1_ROWS)
        rows_im = pl.ds(r0 + FFT_N2, _S1_ROWS)
        for lc in range(0, FFT_CH, _S1_LANES):
            lanes = slice(lc, lc + _S1_LANES)
            xs = [x_ref[j, rows_re, lanes].astype(F32) for j in range(FFT_N1)]
            ev = [None] + [xs[j] + xs[FFT_N1 - j] for j in range(1, half)]
            od = [None] + [xs[j] - xs[FFT_N1 - j] for j in range(1, half)]
            for k in range(half + 1):
                re = _lincomb([(1.0, xs[0]), (cos[k][half], xs[half])]
                              + [(cos[k][j], ev[j]) for j in range(1, half)])
                a_ref[k, rows_re, lanes] = re.astype(BF16)
                if k in (0, half):
                    zero = jnp.zeros_like(re).astype(BF16)
                    a_ref[k, rows_im, lanes] = zero
                else:
                    im = _lincomb([(-sin[k][j], od[j]) for j in range(1, half)])
                    a_ref[k, rows_im, lanes] = im.astype(BF16)
                    a_ref[FFT_N1 - k, rows_re, lanes] = re.astype(BF16)
                    a_ref[FFT_N1 - k, rows_im, lanes] = (-im).astype(BF16)
        return carry

    lax.fori_loop(0, FFT_N2 // _S1_ROWS, body, 0)


def _fourier_kernel(x_ref, m2_ref, cs_ref, gm_ref, hg_ref, o_ref, a_ref, y_ref):
    j = pl.program_id(1)

    @pl.when(j == 0)
    def _():
        _fft_stage1(x_ref, a_ref)

    ris = [_dot(m2_ref[kk], a_ref[j * FFT_K1_PER_STEP + kk]) for kk in range(FFT_K1_PER_STEP)]
    re = jnp.concatenate([ri[:FFT_N2] for ri in ris], axis=0).astype(BF16)
    im = jnp.concatenate([ri[FFT_N2:] for ri in ris], axis=0).astype(BF16)
    y = jnp.concatenate(
        [_dot(re[:, c:c + MXU_COLS], cs_ref[:MXU_COLS, :]) + _dot(im[:, c:c + MXU_COLS], cs_ref[MXU_COLS:, :])
         for c in range(0, FFT_CH, MXU_COLS)], axis=1)
    yn = y * lax.rsqrt(_head_mean_square(y, gm_ref[...]) + EPS) * hg_ref[...]
    for kk in range(FFT_K1_PER_STEP):
        k1 = j * FFT_K1_PER_STEP + kk
        for c in range(FFT_CH // LANES):
            y_ref[c, pl.ds(k1, FFT_N2, stride=FFT_N1), :] = yn[kk * FFT_N2:(kk + 1) * FFT_N2,
                                                               c * LANES:(c + 1) * LANES]

    @pl.when(j == pl.num_programs(1) - 1)
    def _():
        for c in range(FFT_CH // LANES):
            o_ref[:, c * LANES:(c + 1) * LANES] = y_ref[c].astype(BF16)


def _fourier(uf, m2, cs, gm, hg_fft, batch, seq):
    assert seq == FFT_N1 * FFT_N2
    x4 = uf.reshape(batch, FFT_N1, FFT_N2, FFT_CH)
    out = pl.pallas_call(
        _fourier_kernel,
        grid=(batch, FFT_N1 // FFT_K1_PER_STEP),
        in_specs=[
            pl.BlockSpec((None, FFT_N1, FFT_N2, FFT_CH), lambda b, j: (b, 0, 0, 0)),
            pl.BlockSpec((FFT_K1_PER_STEP, 2 * FFT_N2, 2 * FFT_N2), lambda b, j: (j, 0, 0)),
            pl.BlockSpec((2 * MXU_COLS, MXU_COLS), lambda b, j: (0, 0)),
            pl.BlockSpec((MXU_COLS, MXU_COLS), lambda b, j: (0, 0)),
            pl.BlockSpec((1, FFT_CH), lambda b, j: (0, 0)),
        ],
        out_specs=pl.BlockSpec((seq, FFT_CH), lambda b, j: (b, 0)),
        out_shape=jax.ShapeDtypeStruct((batch * seq, FFT_CH), BF16),
        scratch_shapes=[
            pltpu.VMEM((FFT_N1, 2 * FFT_N2, FFT_CH), BF16),
            pltpu.VMEM((FFT_CH // LANES, seq, LANES), F32),
        ],
        compiler_params=pltpu.CompilerParams(
            dimension_semantics=("arbitrary", "arbitrary"), vmem_limit_bytes=VMEM_LIMIT),
        name="fourier",
    )(x4, m2, cs, gm, hg_fft)
    return out


def _first_index_of_max(vals, vmax, row):
    return jnp.min(jnp.where(vals == vmax, row, vals.shape[0]), axis=0, keepdims=True)


def _route(lg):
    cols = lg.shape[1]
    row8 = lax.broadcasted_iota(I32, (EXPERTS_PER_GROUP, cols), 0)
    gl = lg[0:EXPERT_LOGIT_ROW0, :]
    gmax = jnp.max(gl, axis=0, keepdims=True)
    g_w = 1.0 / jnp.sum(jnp.exp(gl - gmax), axis=0, keepdims=True)
    g_idx = _first_index_of_max(gl, gmax, row8)

    el = lg[EXPERT_LOGIT_ROW0:EXPERT_LOGIT_ROW0 + EXPERTS_PER_GROUP, :]
    for g in range(1, N_GROUPS):
        row0 = EXPERT_LOGIT_ROW0 + g * EXPERTS_PER_GROUP
        el = jnp.where(g_idx == g, lg[row0:row0 + EXPERTS_PER_GROUP, :], el)
    emax = jnp.max(el, axis=0, keepdims=True)
    ee = jnp.exp(el - emax)
    e_prob = ee / jnp.sum(ee, axis=0, keepdims=True)
    p1 = jnp.max(e_prob, axis=0, keepdims=True)
    i1 = _first_index_of_max(e_prob, p1, row8)
    rest = jnp.where(row8 == i1, -1.0, e_prob)
    p2 = jnp.max(rest, axis=0, keepdims=True)
    i2 = _first_index_of_max(rest, p2, row8)
    denom = p1 + p2
    e1 = g_idx * EXPERTS_PER_GROUP + i1
    e2 = g_idx * EXPERTS_PER_GROUP + i2
    return e1, e2, g_w * p1 / denom, g_w * p2 / denom


def _trunk_kernel(x_ref, conv_ref, fft_ref, wout_ref, gxa_ref, wq_ref, k_ref, v_ref, wo_ref,
                  gffn_ref, wr_ref, br_ref, tri_ref, ltri_ref,
                  x2_ref, xs_ref, pos_ref, gate_ref, cnt_ref, h3_s, lg_s):
    @pl.when(pl.program_id(0) == 0)
    def _():
        h3_s[...] = jnp.zeros_like(h3_s)
        lg_s[...] = jnp.zeros_like(lg_s)

    h3 = h3_s[...]
    lg = lg_s[...]
    n_tok = lg.shape[1]

    x1 = x_ref[...] + _dot(jnp.concatenate([conv_ref[...], fft_ref[...]], axis=1), wout_ref[...])

    e1, e2, gate1, gate2 = _route(lg)
    row32 = lax.broadcasted_iota(I32, (N_EXPERTS, n_tok), 0)
    hit1 = row32 == e1
    hit2 = row32 == e2
    onehot = jnp.where(hit1 | hit2, 1.0, 0.0)
    before = _dot(onehot.astype(BF16), tri_ref[...])
    cnt = jnp.sum(onehot, axis=1, keepdims=True).astype(I32)
    piece = jnp.left_shift(jnp.right_shift(cnt + (GRANULE - 1), GRANULE_SHIFT), GRANULE_SHIFT)
    piece_b = jnp.broadcast_to(piece.astype(F32), (N_EXPERTS, LANES)).astype(BF16)
    start = _dot(ltri_ref[...], piece_b)[:, 0:1]
    slot = before + start
    pos1 = jnp.sum(jnp.where(hit1, slot, 0.0), axis=0, keepdims=True).astype(I32)
    pos2 = jnp.sum(jnp.where(hit2, slot, 0.0), axis=0, keepdims=True).astype(I32)
    pos_ref[0:1, :] = pos1
    pos_ref[1:2, :] = pos2
    pos_ref[TOP_K:, :] = jnp.zeros((SUBLANES - TOP_K, n_tok), I32)
    gate_ref[0:1, :] = gate1
    gate_ref[1:2, :] = gate2
    gate_ref[TOP_K:, :] = jnp.zeros((SUBLANES - TOP_K, n_tok), F32)
    cnt_ref[...] = jnp.broadcast_to(cnt, (N_EXPERTS, LANES))

    h2 = _rms(x1, gxa_ref[...]).astype(BF16)
    q = _dot(h2, wq_ref[...]).astype(BF16)

    r = lax.broadcasted_iota(I32, (LOCAL_ROWS, n_tok), 0)
    perm = jnp.where((r == pos1) | (r == pos2), 1.0, 0.0).astype(BF16)

    outs = []
    for hd in range(XA_HEADS):
        cols = slice(hd * XA_HEAD_DIM, (hd + 1) * XA_HEAD_DIM)
        s = _dot_nt(q[:, cols], k_ref[:, cols]) * (XA_HEAD_DIM ** -0.5)
        s = s - jnp.max(s, axis=-1, keepdims=True)
        p = jnp.exp(s)
        p = p / jnp.sum(p, axis=-1, keepdims=True)
        outs.append(_dot(p.astype(BF16), v_ref[:, cols]).astype(BF16))
        if hd == 1:
            sorted_left = _dot(perm, h3[:, :HALF])

    o = jnp.concatenate(outs, axis=-1)
    x2 = x1 + _dot(o, wo_ref[...])
    x2_ref[...] = x2
    sorted_right = _dot(perm, h3[:, HALF:])
    h3_next = _rms(x2, gffn_ref[...]).astype(BF16)
    xs_ref[...] = _pack_halves(sorted_left, sorted_right)
    h3_s[...] = h3_next
    lg_s[...] = _dot_nt(wr_ref[...], h3_next) + br_ref[...]


def _trunk(x2d, conv_n, fft_n, w_out, g_xa, w_q, kv, w_o, g_ffn, w_r_t, b_r, tri, ltri, seq):
    t = x2d.shape[0]
    n_tiles = t // TRUNK_ROWS
    n_per_batch = seq // TRUNK_ROWS
    const = lambda i: (0, 0)
    dense = lambda i: jnp.minimum(i, n_tiles - 1)
    routed = lambda i: jnp.maximum(i - 1, 0)
    return pl.pallas_call(
        _trunk_kernel,
        grid=(n_tiles + 1,),
        in_specs=[
            pl.BlockSpec((TRUNK_ROWS, D_MODEL), lambda i: (dense(i), 0)),
            pl.BlockSpec((TRUNK_ROWS, CONV_CH), lambda i: (dense(i), 0)),
            pl.BlockSpec((TRUNK_ROWS, FFT_CH), lambda i: (dense(i), 0)),
            pl.BlockSpec((D_MODEL, D_MODEL), const),
            pl.BlockSpec((1, D_MODEL), const),
            pl.BlockSpec((D_MODEL, D_MODEL), const),
            pl.BlockSpec((MEM_LEN, D_MODEL), lambda i: (dense(i) // n_per_batch, 0)),
            pl.BlockSpec((MEM_LEN, D_MODEL), lambda i: (dense(i) // n_per_batch, 1)),
            pl.BlockSpec((D_MODEL, D_MODEL), const),
            pl.BlockSpec((1, D_MODEL), const),
            pl.BlockSpec((ROUTER_ROWS, D_MODEL), const),
            pl.BlockSpec((ROUTER_ROWS, 1), const),
            pl.BlockSpec((TRUNK_ROWS, TRUNK_ROWS), const),
            pl.BlockSpec((N_EXPERTS, N_EXPERTS), const),
        ],
        out_specs=[
            pl.BlockSpec((TRUNK_ROWS, D_MODEL), lambda i: (dense(i), 0)),
            pl.BlockSpec((LOCAL_ROWS, HALF), lambda i: (routed(i), 0)),
            pl.BlockSpec((SUBLANES, TRUNK_ROWS), lambda i: (0, routed(i))),
            pl.BlockSpec((SUBLANES, TRUNK_ROWS), lambda i: (0, routed(i))),
            pl.BlockSpec((None, N_EXPERTS, LANES), lambda i: (routed(i), 0, 0)),
        ],
        out_shape=[
            jax.ShapeDtypeStruct((t, D_MODEL), F32),
            jax.ShapeDtypeStruct((n_tiles * LOCAL_ROWS, HALF), I32),
            jax.ShapeDtypeStruct((SUBLANES, t), I32),
            jax.ShapeDtypeStruct((SUBLANES, t), F32),
            jax.ShapeDtypeStruct((n_tiles, N_EXPERTS, LANES), I32),
        ],
        scratch_shapes=[
            pltpu.VMEM((TRUNK_ROWS, D_MODEL), BF16),
            pltpu.VMEM((ROUTER_ROWS, TRUNK_ROWS), F32),
        ],
        compiler_params=pltpu.CompilerParams(
            dimension_semantics=("arbitrary",), vmem_limit_bytes=VMEM_LIMIT),
        name="trunk",
    )(x2d, conv_n, fft_n, w_out, g_xa, w_q, kv, kv, w_o, g_ffn, w_r_t, b_r, tri, ltri)


_PLAN_PER_BLOCK = ("blk_e", "first", "next_e", "slot", "nrun", "ngran")
_PLAN_PER_RUN = ("run_src", "run_off", "run_len")


def _plan_layout(n_blk, max_runs):
    sizes = [("nblk", 1)] + [(k, n_blk) for k in _PLAN_PER_BLOCK] + [(k, n_blk * max_runs) for k in _PLAN_PER_RUN]
    offsets, at = {}, 0
    for name, size in sizes:
        offsets[name] = at
        at += size
    return offsets


class _TableView:
    def __init__(self, ref, offset):
        self._ref, self._offset = ref, offset

    def __getitem__(self, i):
        return self._ref[self._offset + i]


def _experts_kernel(plan_ref, xs_hbm, wg_hbm, wu_hbm, wd_hbm, ys_hbm,
                    wg_buf, wu_buf, wd_buf, xbuf, obuf, wsems, xsems, osems, *, n_blk, max_runs):
    layout = _plan_layout(n_blk, max_runs)
    (blk_e_ref, first_ref, next_e_ref, slot_ref, nblk_ref, run_src_ref, run_off_ref, run_len_ref, nrun_ref,
     ngran_ref) = (_TableView(plan_ref, layout[k]) for k in (
         "blk_e", "first", "next_e", "slot", "nblk", "run_src", "run_off", "run_len", "nrun", "ngran"))
    _experts_body(blk_e_ref, first_ref, next_e_ref, slot_ref, nblk_ref,
                  run_src_ref, run_off_ref, run_len_ref, nrun_ref, ngran_ref, n_blk - 1, max_runs,
                  xs_hbm, wg_hbm, wu_hbm, wd_hbm, ys_hbm,
                  wg_buf, wu_buf, wd_buf, xbuf, obuf, wsems, xsems, osems)


def _experts_body(blk_e_ref, first_ref, next_e_ref, slot_ref, nblk_ref,
                  run_src_ref, run_off_ref, run_len_ref, nrun_ref, ngran_ref, last, max_runs,
                  xs_hbm, wg_hbm, wu_hbm, wd_hbm, ys_hbm,
                  wg_buf, wu_buf, wd_buf, xbuf, obuf, wsems, xsems, osems):
    n = nblk_ref[0]
    gpb = EXPERT_ROWS // GRANULE

    def fetch_weights(e, s):
        return (pltpu.make_async_copy(wg_hbm.at[e], wg_buf.at[s], wsems.at[0, s]),
                pltpu.make_async_copy(wu_hbm.at[e], wu_buf.at[s], wsems.at[1, s]),
                pltpu.make_async_copy(wd_hbm.at[e], wd_buf.at[s], wsems.at[2, s]))

    def run_copies(b, r, s):
        k = b * max_runs + r
        length = run_len_ref[k]
        hbm_rows = pl.ds(run_src_ref[k], length)
        blk_rows = pl.ds(run_off_ref[k], length)
        return (pltpu.make_async_copy(xs_hbm.at[hbm_rows], xbuf.at[s, blk_rows], xsems.at[s]),
                pltpu.make_async_copy(obuf.at[s, blk_rows], ys_hbm.at[hbm_rows], osems.at[s]))

    def start_in(b, s, runs):
        lax.fori_loop(0, runs, lambda r, c: (run_copies(b, r, s)[0].start(), c)[1], 0)

    def start_out(b, s):
        lax.fori_loop(0, nrun_ref[b], lambda r, c: (run_copies(b, r, s)[1].start(), c)[1], 0)

    def wait_in(b, s):
        count = ngran_ref[b]
        pltpu.make_async_copy(xs_hbm.at[pl.ds(0, count)], xbuf.at[s, pl.ds(0, count)], xsems.at[s]).wait()

    def wait_out(b, s):
        count = ngran_ref[b]
        pltpu.make_async_copy(obuf.at[s, pl.ds(0, count)], ys_hbm.at[pl.ds(0, count)], osems.at[s]).wait()

    xbuf[...] = jnp.zeros_like(xbuf)
    for cp in fetch_weights(blk_e_ref[0], 0):
        cp.start()
    start_in(0, 0, nrun_ref[0])
    start_in(1, 1, jnp.where(n > 1, nrun_ref[1], 0))

    def block(i, carry):
        xs = lax.rem(i, X_SLOTS)
        os = i % 2
        ws = slot_ref[i]

        @pl.when(first_ref[i] == 1)
        def _():
            for cp in fetch_weights(blk_e_ref[i], ws):
                cp.wait()

            @pl.when(next_e_ref[i] >= 0)
            def _():
                for cp in fetch_weights(next_e_ref[i], 1 - ws):
                    cp.start()

        @pl.when(i >= 2)
        def _():
            wait_out(i - 2, os)

        ahead = jnp.minimum(i + 2, last)
        start_in(ahead, lax.rem(i + 2, X_SLOTS), jnp.where(i + 2 < n, nrun_ref[ahead], 0))

        wait_in(i, xs)

        def mlp(rows):
            granules = rows // GRANULE
            xl, xr = _unpack_halves(xbuf[xs, :granules].reshape(rows, HALF))
            x = jnp.concatenate([xl, xr], axis=1)
            a = _dot(x, wg_buf[ws].astype(BF16))
            b = _dot(x, wu_buf[ws].astype(BF16))
            hmid = (a * jax.nn.sigmoid(a) * b).astype(BF16)
            y = _dot(hmid, wd_buf[ws].astype(BF16))
            packed = _pack_halves(y[:, :HALF].astype(BF16).astype(F32), y[:, HALF:].astype(BF16).astype(F32))
            obuf[os, :granules] = packed.reshape(granules, GRANULE, HALF)

        pads = -(-ngran_ref[i] // (EXPERT_PAD // GRANULE))
        for k in range(1, EXPERT_ROWS // EXPERT_PAD + 1):
            pl.when(pads == k)(functools.partial(mlp, k * EXPERT_PAD))
        start_out(i, os)
        return carry

    lax.fori_loop(0, n, block, 0)

    @pl.when(n >= 2)
    def _():
        wait_out(n - 2, n % 2)
    wait_out(n - 1, (n - 1) % 2)


def _experts(plan_table, n_blk, max_runs, xs_loc, w_gate, w_up, w_down):
    granules = xs_loc.reshape(-1, GRANULE, HALF)
    hbm = pl.BlockSpec(memory_space=pl.ANY)
    return pl.pallas_call(
        functools.partial(_experts_kernel, n_blk=n_blk, max_runs=max_runs),
        grid_spec=pltpu.PrefetchScalarGridSpec(
            num_scalar_prefetch=1,
            grid=(1,),
            in_specs=[hbm, hbm, hbm, hbm],
            out_specs=hbm,
            scratch_shapes=[
                pltpu.VMEM((2, D_MODEL, D_EXPERT), F32),
                pltpu.VMEM((2, D_MODEL, D_EXPERT), F32),
                pltpu.VMEM((2, D_EXPERT, D_MODEL), F32),
                pltpu.VMEM((X_SLOTS, EXPERT_ROWS // GRANULE, GRANULE, HALF), I32),
                pltpu.VMEM((2, EXPERT_ROWS // GRANULE, GRANULE, HALF), I32),
                pltpu.SemaphoreType.DMA((3, 2)),
                pltpu.SemaphoreType.DMA((X_SLOTS,)),
                pltpu.SemaphoreType.DMA((2,)),
            ],
        ),
        out_shape=jax.ShapeDtypeStruct(granules.shape, I32),
        input_output_aliases={1: 0},
        compiler_params=pltpu.CompilerParams(
            dimension_semantics=("arbitrary",), vmem_limit_bytes=VMEM_LIMIT),
        name="experts",
    )(plan_table, granules, w_gate, w_up, w_down).reshape(xs_loc.shape)


def _combine_kernel(x2_ref, pos_ref, gate_ref, g_ref, ys_ref, o_ref):
    r = lax.broadcasted_iota(I32, (LOCAL_ROWS, TRUNK_ROWS), 0)
    for tile in range(COMBINE_TILES):
        toks = slice(tile * TRUNK_ROWS, (tile + 1) * TRUNK_ROWS)
        w_t = (jnp.where(r == pos_ref[0:1, toks], gate_ref[0:1, toks], 0.0)
               + jnp.where(r == pos_ref[1:2, toks], gate_ref[1:2, toks], 0.0)).astype(BF16)
        yl, yr = _unpack_halves(ys_ref[tile * LOCAL_ROWS:(tile + 1) * LOCAL_ROWS, :])
        moe = jnp.concatenate([_dot_tn(w_t, yl), _dot_tn(w_t, yr)], axis=-1)
        o_ref[toks, :] = _rms(x2_ref[toks, :] + moe, g_ref[...])


def _combine(x2, pos_tk, gates_tk, g_final, ys_loc):
    t = x2.shape[0]
    rows = COMBINE_TILES * TRUNK_ROWS
    return pl.pallas_call(
        _combine_kernel,
        grid=(t // rows,),
        in_specs=[
            pl.BlockSpec((rows, D_MODEL), lambda i: (i, 0)),
            pl.BlockSpec((SUBLANES, rows), lambda i: (0, i)),
            pl.BlockSpec((SUBLANES, rows), lambda i: (0, i)),
            pl.BlockSpec((1, D_MODEL), lambda i: (0, 0)),
            pl.BlockSpec((COMBINE_TILES * LOCAL_ROWS, HALF), lambda i: (i, 0)),
        ],
        out_specs=pl.BlockSpec((rows, D_MODEL), lambda i: (i, 0)),
        out_shape=jax.ShapeDtypeStruct((t, D_MODEL), F32),
        compiler_params=pltpu.CompilerParams(vmem_limit_bytes=VMEM_LIMIT),
        name="combine",
    )(x2, pos_tk, gates_tk, g_final, ys_loc)


def _router_params(w_rg, b_rg, w_re, b_re):
    w = jnp.zeros((ROUTER_ROWS, D_MODEL), F32)
    w = w.at[0:N_GROUPS].set(w_rg.T).at[EXPERT_LOGIT_ROW0:EXPERT_LOGIT_ROW0 + N_EXPERTS].set(w_re.T)
    b = jnp.zeros((ROUTER_ROWS,), F32)
    b = (b.at[0:N_GROUPS].set(b_rg).at[N_GROUPS:EXPERT_LOGIT_ROW0].set(NEG_BIG)
         .at[EXPERT_LOGIT_ROW0:EXPERT_LOGIT_ROW0 + N_EXPERTS].set(b_re))
    return w.astype(BF16), b.reshape(ROUTER_ROWS, 1)


def _plan(cnt, max_padded_rows):
    n_tiles = cnt.shape[0]
    piece = (cnt + GRANULE - 1) // GRANULE * GRANULE
    lend = jnp.cumsum(piece, axis=1)
    lstart = lend - piece
    tot = jnp.sum(piece, axis=0)
    padded = (tot + EXPERT_PAD - 1) // EXPERT_PAD * EXPERT_PAD
    cum_tiles = jnp.cumsum(piece, axis=0)

    per_expert = (padded + EXPERT_ROWS - 1) // EXPERT_ROWS
    blk_end = jnp.cumsum(per_expert)
    pads_per_blk = EXPERT_ROWS // EXPERT_PAD
    n_blk = (max_padded_rows // EXPERT_PAD + N_EXPERTS * (pads_per_blk - 1)) // pads_per_blk + 1
    blk = jnp.arange(n_blk, dtype=I32)
    blk_e = jnp.minimum(jnp.sum((blk_end[None, :] <= blk[:, None]).astype(I32), axis=1), N_EXPERTS - 1)
    nblk = blk_end[-1:]
    valid = blk < nblk

    of_blk_e = blk_e[:, None] == jnp.arange(N_EXPERTS, dtype=I32)[None, :]
    pick = lambda table: jnp.sum(jnp.where(of_blk_e[:, None, :], table[None, :, :], 0), axis=2)
    pick1 = lambda vec: jnp.sum(jnp.where(of_blk_e, vec[None, :], 0), axis=1)
    seg_off = (blk - pick1(blk_end - per_expert)) * EXPERT_ROWS
    seg_end = jnp.where(valid, jnp.minimum(seg_off + EXPERT_ROWS, pick1(tot)), seg_off)
    ngran = (seg_end - seg_off) // GRANULE
    piece_end = pick(cum_tiles)
    piece_start = piece_end - pick(piece)
    lo = jnp.maximum(piece_start, seg_off[:, None])
    hi = jnp.minimum(piece_end, seg_end[:, None])
    has_run = hi > lo
    tile_base = (jnp.arange(n_tiles, dtype=I32) * LOCAL_ROWS)[None, :] + pick(lstart)
    src_granule = (tile_base + lo - piece_start) // GRANULE
    off_granule = (lo - seg_off[:, None]) // GRANULE
    len_granule = (hi - lo) // GRANULE
    run_of = jnp.cumsum(has_run.astype(I32), axis=1) - 1
    is_run = has_run[:, :, None] & (run_of[:, :, None] == jnp.arange(n_tiles, dtype=I32)[None, None, :])
    compact = lambda v: jnp.sum(jnp.where(is_run, v[:, :, None], 0), axis=1)
    run_src, run_off, run_len = compact(src_granule), compact(off_granule), compact(len_granule)
    nrun = jnp.sum(has_run.astype(I32), axis=1)
    change =jnp.concatenate([jnp.ones((1,), bool), blk_e[1:] != blk_e[:-1]])
    slot = (jnp.cumsum(change.astype(I32)) - 1) % 2
    later = (blk_e[None, :] > blk_e[:, None]) & valid[None, :]
    next_e = jnp.min(jnp.where(later, blk_e[None, :], N_EXPERTS), axis=1)
    next_e = jnp.where(next_e == N_EXPERTS, -1, next_e)
    tables = dict(
        nblk=nblk, blk_e=blk_e, first=change & valid, next_e=next_e, slot=slot, nrun=nrun, ngran=ngran,
        run_src=run_src.reshape(-1), run_off=run_off.reshape(-1), run_len=run_len.reshape(-1),
    )
    order = ("nblk",) + _PLAN_PER_BLOCK + _PLAN_PER_RUN
    return jnp.concatenate([tables[k].astype(I32) for k in order]), n_blk, n_tiles


def _layer(x2d, mem2d, batch, seq, norm_mix_g, w_in, conv_w, conv_b, head_norm_g, w_out,
           norm_xa_g, norm_mem_g, w_q, w_kv, w_o, norm_ffn_g, w_rg, b_rg, w_re, b_re,
           w_gate, w_up, w_down, out_norm_g):
    t = x2d.shape[0]
    n_tiles = t // TRUNK_ROWS
    row = lambda v: v.reshape(1, -1)
    hg = head_norm_g.reshape(-1)
    gm = _group_mean_matrix()

    kv = _kv_proj(mem2d, row(norm_mem_g), w_kv)
    conv_n, uf = _mixer_in(x2d, row(norm_mix_g), w_in.astype(BF16), conv_w, row(conv_b),
                           row(hg[:CONV_CH]), gm, batch, seq)
    fft_n = _fourier(uf.reshape(batch, seq, FFT_CH), _fft_stage2_matrices(seq), _fft_channel_matrix(seq),
                     gm, row(hg[CONV_CH:]), batch, seq)
    w_r_t, b_r = _router_params(w_rg, b_rg, w_re, b_re)
    x2, xs_loc, pos, gates, cnt = _trunk(
        x2d, conv_n, fft_n, w_out.astype(BF16), row(norm_xa_g), w_q.astype(BF16), kv, w_o.astype(BF16),
        row(norm_ffn_g), w_r_t, b_r, _strict_upper(TRUNK_ROWS), _strict_lower(N_EXPERTS), seq)

    max_rows = n_tiles * LOCAL_ROWS + N_EXPERTS * (EXPERT_PAD - GRANULE)
    n_global_rows = -(-max_rows // EXPERT_PAD) * EXPERT_PAD
    plan_table, n_blk, max_runs = _plan(cnt[:, :, 0], n_global_rows)
    ys_loc = _experts(plan_table, n_blk, max_runs, xs_loc, w_gate, w_up, w_down)
    return _combine(x2, pos, gates, row(out_norm_g), ys_loc)


def kernel(x, mem, norm_mix_g, w_in, conv_w, conv_b, head_norm_g, w_out, norm_xa_g, norm_mem_g, w_q, w_kv,
           w_o, norm_ffn_g, w_route_group, b_route_group, w_route_expert, b_route_expert, w_gate, w_up,
           w_down, final_norm_g):
    batch, seq, _ = x.shape
    depth = norm_mix_g.shape[0]
    assert depth == 1, "the final norm is fused into the last layer's combine kernel"
    x2d = x.reshape(batch * seq, D_MODEL)
    mem2d = mem.reshape(batch * MEM_LEN, D_MODEL)
    l = 0
    out = _layer(x2d, mem2d, batch, seq, norm_mix_g[l], w_in[l], conv_w[l], conv_b[l], head_norm_g[l],
                 w_out[l], norm_xa_g[l], norm_mem_g[l], w_q[l], w_kv[l], w_o[l], norm_ffn_g[l],
                 w_route_group[l], b_route_group[l], w_route_expert[l], b_route_expert[l],
                 w_gate[l], w_up[l], w_down[l], final_norm_g)
    return out.reshape(batch, seq, D_MODEL)
```

```python
import functools
import math

import numpy as np
import jax
import jax.numpy as jnp
from jax import lax
from jax.experimental import pallas as pl
from jax.experimental.pallas import tpu as pltpu

F32 = jnp.float32
BF16 = jnp.bfloat16
I32 = jnp.int32

D_MODEL = 1024
HALF = D_MODEL // 2
HEAD_DIM = 64
CONV_CH = 512
FFT_CH = 512
IN_COLS = 3 * CONV_CH + FFT_CH
MEM_LEN = 256
XA_HEADS = 4
XA_HEAD_DIM = D_MODEL // XA_HEADS
N_GROUPS = 4
EXPERTS_PER_GROUP = 8
N_EXPERTS = 32
TOP_K = 2
D_EXPERT = 512
EPS = 1e-6

FFT_N1 = 16
FFT_N2 = 256
FFT_K1_PER_STEP = 4

LANES = 128
MXU_COLS = 256
SUBLANES = 8
GRANULE = SUBLANES
GRANULE_SHIFT = GRANULE.bit_length() - 1
MIX_ROWS = 1024
TRUNK_ROWS = 512
LOCAL_ROWS = TOP_K * TRUNK_ROWS + N_EXPERTS * GRANULE
COMBINE_TILES = 2
EXPERT_PAD = 128
EXPERT_ROWS = 640
X_SLOTS = 3
ROUTER_ROWS = 128
EXPERT_LOGIT_ROW0 = SUBLANES
NEG_BIG = -1e30
HI16 = -65536

VMEM_LIMIT = 56 * 1024 * 1024


def _rms(x, g):
    return x * lax.rsqrt(jnp.mean(x * x, axis=-1, keepdims=True) + EPS) * g


def _dot(a, b):
    return jnp.dot(a, b, preferred_element_type=F32)


def _dot_nt(a, b):
    return lax.dot_general(a, b, (((1,), (1,)), ((), ())), preferred_element_type=F32)


def _dot_tn(a, b):
    return lax.dot_general(a, b, (((0,), (0,)), ((), ())), preferred_element_type=F32)


def _pack_halves(left_f32, right_f32):
    lb = lax.bitcast_convert_type(left_f32, I32)
    rb = lax.shift_right_logical(lax.bitcast_convert_type(right_f32, I32), jnp.int32(16))
    return lb | rb


def _unpack_halves(packed_i32):
    left = lax.bitcast_convert_type(packed_i32 & jnp.int32(HI16), F32)
    right = lax.bitcast_convert_type(lax.shift_left(packed_i32, jnp.int32(16)), F32)
    return left.astype(BF16), right.astype(BF16)


def _group_mean_matrix():
    g = np.kron(np.eye(MXU_COLS // HEAD_DIM), np.full((HEAD_DIM, HEAD_DIM), 1.0 / HEAD_DIM))
    return jnp.asarray(g, dtype=BF16)


def _head_mean_square(y, gm):
    sq = (y * y).astype(BF16)
    return jnp.concatenate([_dot(sq[:, c:c + MXU_COLS], gm) for c in range(0, y.shape[1], MXU_COLS)], axis=1)


def _fft_stage2_matrices(seq):
    k1 = np.arange(FFT_N1)[:, None, None]
    k2 = np.arange(FFT_N2)[None, :, None]
    s2 = np.arange(FFT_N2)[None, None, :]
    ang = 2.0 * np.pi * ((s2 * (k1 + FFT_N1 * k2)) % seq) / seq
    c, s = np.cos(ang), np.sin(ang)
    top = np.concatenate([c, s], axis=2)
    bot = np.concatenate([-s, c], axis=2)
    return jnp.asarray(np.concatenate([top, bot], axis=1), dtype=BF16)


def _fft_channel_matrix(seq):
    c = np.arange(HEAD_DIM)
    ang = 2.0 * np.pi * ((c[:, None] * c[None, :]) % HEAD_DIM) / HEAD_DIM
    scale = 1.0 / math.sqrt(seq * HEAD_DIM)
    eye = np.eye(MXU_COLS // HEAD_DIM)
    cs = np.concatenate([np.kron(eye, np.cos(ang)), np.kron(eye, np.sin(ang))], axis=0) * scale
    return jnp.asarray(cs, dtype=BF16)


def _strict_upper(n):
    return jnp.asarray(np.triu(np.ones((n, n)), k=1), dtype=BF16)


def _strict_lower(n):
    return jnp.asarray(np.tril(np.ones((n, n)), k=-1), dtype=BF16)


def _kv_kernel(mem_ref, g_ref, w_ref, o_ref):
    h = _rms(mem_ref[...], g_ref[...]).astype(BF16)
    o_ref[...] = _dot(h, w_ref[...].astype(BF16)).astype(BF16)


def _kv_proj(mem2d, g, w_kv):
    rows = mem2d.shape[0]
    cols = w_kv.shape[1]
    cb = 512
    return pl.pallas_call(
        _kv_kernel,
        grid=(cols // cb,),
        in_specs=[
            pl.BlockSpec((rows, D_MODEL), lambda j: (0, 0)),
            pl.BlockSpec((1, D_MODEL), lambda j: (0, 0)),
            pl.BlockSpec((D_MODEL, cb), lambda j: (0, j)),
        ],
        out_specs=pl.BlockSpec((rows, cb), lambda j: (0, j)),
        out_shape=jax.ShapeDtypeStruct((rows, cols), BF16),
        compiler_params=pltpu.CompilerParams(vmem_limit_bytes=VMEM_LIMIT),
        name="kv_proj",
    )(mem2d, g, w_kv)


def _mixer_in_kernel(x_ref, xp_ref, xn_ref, g_ref, w_ref, cw_ref, cb_ref, hg_ref, gm_ref,
                     conv_ref, uf_ref):
    i = pl.program_id(1)
    n_i = pl.num_programs(1)
    rows = x_ref.shape[0]
    g = g_ref[...]
    h = _rms(x_ref[...], g).astype(BF16)
    u_cv = _dot(h, w_ref[:, CONV_CH:3 * CONV_CH])
    cv = u_cv[:, :CONV_CH] * u_cv[:, CONV_CH:]

    hh = jnp.concatenate([_rms(xp_ref[...], g), _rms(xn_ref[...], g)], axis=0).astype(BF16)
    uh = _dot(hh, w_ref[:, CONV_CH:3 * CONV_CH])
    cvh = uh[:, :CONV_CH] * uh[:, CONV_CH:]
    cv_prev = cvh[SUBLANES - 1:SUBLANES, :] * jnp.where(i == 0, 0.0, 1.0)
    cv_next = cvh[SUBLANES:SUBLANES + 1, :] * jnp.where(i == n_i - 1, 0.0, 1.0)

    row = lax.broadcasted_iota(I32, cv.shape, 0)
    cv_up = jnp.where(row == 0, cv_prev, pltpu.roll(cv, 1, 0))
    cv_dn = jnp.where(row == rows - 1, cv_next, pltpu.roll(cv, rows - 1, 0))
    z = cw_ref[0:1, :] * cv_up + cw_ref[1:2, :] * cv + cw_ref[2:3, :] * cv_dn + cb_ref[...]
    uf_ref[...] = _dot(h, w_ref[:, 3 * CONV_CH:]).astype(BF16)
    y = _dot(h, w_ref[:, :CONV_CH]) * z
    ms = _head_mean_square(y, gm_ref[...])
    conv_ref[...] = (y * lax.rsqrt(ms + EPS) * hg_ref[...]).astype(BF16)


def _mixer_in(x2d, g, w_in, conv_w, conv_b, hg_conv, gm, batch, seq):
    n_i = seq // MIX_ROWS
    t = x2d.shape[0]
    r8 = MIX_ROWS // SUBLANES
    last8 = t // SUBLANES - 1
    return pl.pallas_call(
        _mixer_in_kernel,
        grid=(batch, n_i),
        in_specs=[
            pl.BlockSpec((MIX_ROWS, D_MODEL), lambda b, i: (b * n_i + i, 0)),
            pl.BlockSpec((SUBLANES, D_MODEL), lambda b, i: (jnp.maximum((b * n_i + i) * r8 - 1, 0), 0)),
            pl.BlockSpec((SUBLANES, D_MODEL), lambda b, i: (jnp.minimum((b * n_i + i + 1) * r8, last8), 0)),
            pl.BlockSpec((1, D_MODEL), lambda b, i: (0, 0)),
            pl.BlockSpec((D_MODEL, IN_COLS), lambda b, i: (0, 0)),
            pl.BlockSpec((3, CONV_CH), lambda b, i: (0, 0)),
            pl.BlockSpec((1, CONV_CH), lambda b, i: (0, 0)),
            pl.BlockSpec((1, CONV_CH), lambda b, i: (0, 0)),
            pl.BlockSpec((MXU_COLS, MXU_COLS), lambda b, i: (0, 0)),
        ],
        out_specs=[
            pl.BlockSpec((MIX_ROWS, CONV_CH), lambda b, i: (b * n_i + i, 0)),
            pl.BlockSpec((MIX_ROWS, FFT_CH), lambda b, i: (b * n_i + i, 0)),
        ],
        out_shape=[
            jax.ShapeDtypeStruct((t, CONV_CH), BF16),
            jax.ShapeDtypeStruct((t, FFT_CH), BF16),
        ],
        compiler_params=pltpu.CompilerParams(vmem_limit_bytes=VMEM_LIMIT),
        name="mixer_in",
    )(x2d, x2d, x2d, g, w_in, conv_w, conv_b, hg_conv, gm)


_S1_ROWS = 16
_S1_LANES = 128


def _lincomb(terms):
    acc = None
    for coef, val in terms:
        if abs(coef) < 1e-12:
            continue
        if abs(coef - 1.0) < 1e-12:
            term, neg = val, False
        elif abs(coef + 1.0) < 1e-12:
            term, neg = val, True
        else:
            term, neg = coef * val, False
        if acc is None:
            acc = -term if neg else term
        else:
            acc = acc - term if neg else acc + term
    return acc


def _fft_stage1(x_ref, a_ref):
    half = FFT_N1 // 2
    cos = [[math.cos(2 * math.pi * ((k * j) % FFT_N1) / FFT_N1) for j in range(FFT_N1)] for k in range(FFT_N1)]
    sin = [[math.sin(2 * math.pi * ((k * j) % FFT_N1) / FFT_N1) for j in range(FFT_N1)] for k in range(FFT_N1)]

    def body(r, carry):
        r0 = pl.multiple_of(r * _S1_ROWS, _S1_ROWS)
        rows_re = pl.ds(r0, _S1_ROWS)
        rows_im = pl.ds(r0 + FFT_N2, _S1_ROWS)
        for lc in range(0, FFT_CH, _S1_LANES):
            lanes = slice(lc, lc + _S1_LANES)
            xs = [x_ref[j, rows_re, lanes].astype(F32) for j in range(FFT_N1)]
            ev = [None] + [xs[j] + xs[FFT_N1 - j] for j in range(1, half)]
            od = [None] + [xs[j] - xs[FFT_N1 - j] for j in range(1, half)]
            for k in range(half + 1):
                re = _lincomb([(1.0, xs[0]), (cos[k][half], xs[half])]
                              + [(cos[k][j], ev[j]) for j in range(1, half)])
                a_ref[k, rows_re, lanes] = re.astype(BF16)
                if k in (0, half):
                    zero = jnp.zeros_like(re).astype(BF16)
                    a_ref[k, rows_im, lanes] = zero
                else:
                    im = _lincomb([(-sin[k][j], od[j]) for j in range(1, half)])
                    a_ref[k, rows_im, lanes] = im.astype(BF16)
                    a_ref[FFT_N1 - k, rows_re, lanes] = re.astype(BF16)
                    a_ref[FFT_N1 - k, rows_im, lanes] = (-im).astype(BF16)
        return carry

    lax.fori_loop(0, FFT_N2 // _S1_ROWS, body, 0)


def _fourier_kernel(x_ref, m2_ref, cs_ref, gm_ref, hg_ref, o_ref, a_ref, y_ref):
    j = pl.program_id(1)

    @pl.when(j == 0)
    def _():
        _fft_stage1(x_ref, a_ref)

    ris = [_dot(m2_ref[kk], a_ref[j * FFT_K1_PER_STEP + kk]) for kk in range(FFT_K1_PER_STEP)]
    re = jnp.concatenate([ri[:FFT_N2] for ri in ris], axis=0).astype(BF16)
    im = jnp.concatenate([ri[FFT_N2:] for ri in ris], axis=0).astype(BF16)
    y = jnp.concatenate(
        [_dot(re[:, c:c + MXU_COLS], cs_ref[:MXU_COLS, :]) + _dot(im[:, c:c + MXU_COLS], cs_ref[MXU_COLS:, :])
         for c in range(0, FFT_CH, MXU_COLS)], axis=1)
    yn = y * lax.rsqrt(_head_mean_square(y, gm_ref[...]) + EPS) * hg_ref[...]
    for kk in range(FFT_K1_PER_STEP):
        k1 = j * FFT_K1_PER_STEP + kk
        for c in range(FFT_CH // LANES):
            y_ref[c, pl.ds(k1, FFT_N2, stride=FFT_N1), :] = yn[kk * FFT_N2:(kk + 1) * FFT_N2,
                                                               c * LANES:(c + 1) * LANES]

    @pl.when(j == pl.num_programs(1) - 1)
    def _():
        for c in range(FFT_CH // LANES):
            o_ref[:, c * LANES:(c + 1) * LANES] = y_ref[c].astype(BF16)


def _fourier(uf, m2, cs, gm, hg_fft, batch, seq):
    assert seq == FFT_N1 * FFT_N2
    x4 = uf.reshape(batch, FFT_N1, FFT_N2, FFT_CH)
    out = pl.pallas_call(
        _fourier_kernel,
        grid=(batch, FFT_N1 // FFT_K1_PER_STEP),
        in_specs=[
            pl.BlockSpec((None, FFT_N1, FFT_N2, FFT_CH), lambda b, j: (b, 0, 0, 0)),
            pl.BlockSpec((FFT_K1_PER_STEP, 2 * FFT_N2, 2 * FFT_N2), lambda b, j: (j, 0, 0)),
            pl.BlockSpec((2 * MXU_COLS, MXU_COLS), lambda b, j: (0, 0)),
            pl.BlockSpec((MXU_COLS, MXU_COLS), lambda b, j: (0, 0)),
            pl.BlockSpec((1, FFT_CH), lambda b, j: (0, 0)),
        ],
        out_specs=pl.BlockSpec((seq, FFT_CH), lambda b, j: (b, 0)),
        out_shape=jax.ShapeDtypeStruct((batch * seq, FFT_CH), BF16),
        scratch_shapes=[
            pltpu.VMEM((FFT_N1, 2 * FFT_N2, FFT_CH), BF16),
            pltpu.VMEM((FFT_CH // LANES, seq, LANES), F32),
        ],
        compiler_params=pltpu.CompilerParams(
            dimension_semantics=("arbitrary", "arbitrary"), vmem_limit_bytes=VMEM_LIMIT),
        name="fourier",
    )(x4, m2, cs, gm, hg_fft)
    return out


def _first_index_of_max(vals, vmax, row):
    return jnp.min(jnp.where(vals == vmax, row, vals.shape[0]), axis=0, keepdims=True)


def _route(lg):
    cols = lg.shape[1]
    row8 = lax.broadcasted_iota(I32, (EXPERTS_PER_GROUP, cols), 0)
    gl = lg[0:EXPERT_LOGIT_ROW0, :]
    gmax = jnp.max(gl, axis=0, keepdims=True)
    g_w = 1.0 / jnp.sum(jnp.exp(gl - gmax), axis=0, keepdims=True)
    g_idx = _first_index_of_max(gl, gmax, row8)

    el = lg[EXPERT_LOGIT_ROW0:EXPERT_LOGIT_ROW0 + EXPERTS_PER_GROUP, :]
    for g in range(1, N_GROUPS):
        row0 = EXPERT_LOGIT_ROW0 + g * EXPERTS_PER_GROUP
        el = jnp.where(g_idx == g, lg[row0:row0 + EXPERTS_PER_GROUP, :], el)
    emax = jnp.max(el, axis=0, keepdims=True)
    ee = jnp.exp(el - emax)
    e_prob = ee / jnp.sum(ee, axis=0, keepdims=True)
    p1 = jnp.max(e_prob, axis=0, keepdims=True)
    i1 = _first_index_of_max(e_prob, p1, row8)
    rest = jnp.where(row8 == i1, -1.0, e_prob)
    p2 = jnp.max(rest, axis=0, keepdims=True)
    i2 = _first_index_of_max(rest, p2, row8)
    denom = p1 + p2
    e1 = g_idx * EXPERTS_PER_GROUP + i1
    e2 = g_idx * EXPERTS_PER_GROUP + i2
    return e1, e2, g_w * p1 / denom, g_w * p2 / denom


def _trunk_kernel(x_ref, conv_ref, fft_ref, wout_ref, gxa_ref, wq_ref, k_ref, v_ref, wo_ref,
                  gffn_ref, wr_ref, br_ref, tri_ref, ltri_ref,
                  x2_ref, xs_ref, pos_ref, gate_ref, cnt_ref, h3_s, lg_s):
    @pl.when(pl.program_id(0) == 0)
    def _():
        h3_s[...] = jnp.zeros_like(h3_s)
        lg_s[...] = jnp.zeros_like(lg_s)

    h3 = h3_s[...]
    lg = lg_s[...]
    n_tok = lg.shape[1]

    x1 = x_ref[...] + _dot(jnp.concatenate([conv_ref[...], fft_ref[...]], axis=1), wout_ref[...])

    e1, e2, gate1, gate2 = _route(lg)
    row32 = lax.broadcasted_iota(I32, (N_EXPERTS, n_tok), 0)
    hit1 = row32 == e1
    hit2 = row32 == e2
    onehot = jnp.where(hit1 | hit2, 1.0, 0.0)
    before = _dot(onehot.astype(BF16), tri_ref[...])
    cnt = jnp.sum(onehot, axis=1, keepdims=True).astype(I32)
    piece = jnp.left_shift(jnp.right_shift(cnt + (GRANULE - 1), GRANULE_SHIFT), GRANULE_SHIFT)
    piece_b = jnp.broadcast_to(piece.astype(F32), (N_EXPERTS, LANES)).astype(BF16)
    start = _dot(ltri_ref[...], piece_b)[:, 0:1]
    slot = before + start
    pos1 = jnp.sum(jnp.where(hit1, slot, 0.0), axis=0, keepdims=True).astype(I32)
    pos2 = jnp.sum(jnp.where(hit2, slot, 0.0), axis=0, keepdims=True).astype(I32)
    pos_ref[0:1, :] = pos1
    pos_ref[1:2, :] = pos2
    pos_ref[TOP_K:, :] = jnp.zeros((SUBLANES - TOP_K, n_tok), I32)
    gate_ref[0:1, :] = gate1
    gate_ref[1:2, :] = gate2
    gate_ref[TOP_K:, :] = jnp.zeros((SUBLANES - TOP_K, n_tok), F32)
    cnt_ref[...] = jnp.broadcast_to(cnt, (N_EXPERTS, LANES))

    h2 = _rms(x1, gxa_ref[...]).astype(BF16)
    q = _dot(h2, wq_ref[...]).astype(BF16)

    r = lax.broadcasted_iota(I32, (LOCAL_ROWS, n_tok), 0)
    perm = jnp.where((r == pos1) | (r == pos2), 1.0, 0.0).astype(BF16)

    outs = []
    for hd in range(XA_HEADS):
        cols = slice(hd * XA_HEAD_DIM, (hd + 1) * XA_HEAD_DIM)
        s = _dot_nt(q[:, cols], k_ref[:, cols]) * (XA_HEAD_DIM ** -0.5)
        s = s - jnp.max(s, axis=-1, keepdims=True)
        p = jnp.exp(s)
        p = p / jnp.sum(p, axis=-1, keepdims=True)
        outs.append(_dot(p.astype(BF16), v_ref[:, cols]).astype(BF16))
        if hd == 1:
            sorted_left = _dot(perm, h3[:, :HALF])

    o = jnp.concatenate(outs, axis=-1)
    x2 = x1 + _dot(o, wo_ref[...])
    x2_ref[...] = x2
    sorted_right = _dot(perm, h3[:, HALF:])
    h3_next = _rms(x2, gffn_ref[...]).astype(BF16)
    xs_ref[...] = _pack_halves(sorted_left, sorted_right)
    h3_s[...] = h3_next
    lg_s[...] = _dot_nt(wr_ref[...], h3_next) + br_ref[...]


def _trunk(x2d, conv_n, fft_n, w_out, g_xa, w_q, kv, w_o, g_ffn, w_r_t, b_r, tri, ltri, seq):
    t = x2d.shape[0]
    n_tiles = t // TRUNK_ROWS
    n_per_batch = seq // TRUNK_ROWS
    const = lambda i: (0, 0)
    dense = lambda i: jnp.minimum(i, n_tiles - 1)
    routed = lambda i: jnp.maximum(i - 1, 0)
    return pl.pallas_call(
        _trunk_kernel,
        grid=(n_tiles + 1,),
        in_specs=[
            pl.BlockSpec((TRUNK_ROWS, D_MODEL), lambda i: (dense(i), 0)),
            pl.BlockSpec((TRUNK_ROWS, CONV_CH), lambda i: (dense(i), 0)),
            pl.BlockSpec((TRUNK_ROWS, FFT_CH), lambda i: (dense(i), 0)),
            pl.BlockSpec((D_MODEL, D_MODEL), const),
            pl.BlockSpec((1, D_MODEL), const),
            pl.BlockSpec((D_MODEL, D_MODEL), const),
            pl.BlockSpec((MEM_LEN, D_MODEL), lambda i: (dense(i) // n_per_batch, 0)),
            pl.BlockSpec((MEM_LEN, D_MODEL), lambda i: (dense(i) // n_per_batch, 1)),
            pl.BlockSpec((D_MODEL, D_MODEL), const),
            pl.BlockSpec((1, D_MODEL), const),
            pl.BlockSpec((ROUTER_ROWS, D_MODEL), const),
            pl.BlockSpec((ROUTER_ROWS, 1), const),
            pl.BlockSpec((TRUNK_ROWS, TRUNK_ROWS), const),
            pl.BlockSpec((N_EXPERTS, N_EXPERTS), const),
        ],
        out_specs=[
            pl.BlockSpec((TRUNK_ROWS, D_MODEL), lambda i: (dense(i), 0)),
            pl.BlockSpec((LOCAL_ROWS, HALF), lambda i: (routed(i), 0)),
            pl.BlockSpec((SUBLANES, TRUNK_ROWS), lambda i: (0, routed(i))),
            pl.BlockSpec((SUBLANES, TRUNK_ROWS), lambda i: (0, routed(i))),
            pl.BlockSpec((None, N_EXPERTS, LANES), lambda i: (routed(i), 0, 0)),
        ],
        out_shape=[
            jax.ShapeDtypeStruct((t, D_MODEL), F32),
            jax.ShapeDtypeStruct((n_tiles * LOCAL_ROWS, HALF), I32),
            jax.ShapeDtypeStruct((SUBLANES, t), I32),
            jax.ShapeDtypeStruct((SUBLANES, t), F32),
            jax.ShapeDtypeStruct((n_tiles, N_EXPERTS, LANES), I32),
        ],
        scratch_shapes=[
            pltpu.VMEM((TRUNK_ROWS, D_MODEL), BF16),
            pltpu.VMEM((ROUTER_ROWS, TRUNK_ROWS), F32),
        ],
        compiler_params=pltpu.CompilerParams(
            dimension_semantics=("arbitrary",), vmem_limit_bytes=VMEM_LIMIT),
        name="trunk",
    )(x2d, conv_n, fft_n, w_out, g_xa, w_q, kv, kv, w_o, g_ffn, w_r_t, b_r, tri, ltri)


_PLAN_PER_BLOCK = ("blk_e", "first", "next_e", "slot", "nrun", "ngran")
_PLAN_PER_RUN = ("run_src", "run_off", "run_len")


def _plan_layout(n_blk, max_runs):
    sizes = [("nblk", 1)] + [(k, n_blk) for k in _PLAN_PER_BLOCK] + [(k, n_blk * max_runs) for k in _PLAN_PER_RUN]
    offsets, at = {}, 0
    for name, size in sizes:
        offsets[name] = at
        at += size
    return offsets


class _TableView:
    def __init__(self, ref, offset):
        self._ref, self._offset = ref, offset

    def __getitem__(self, i):
        return self._ref[self._offset + i]


def _experts_kernel(plan_ref, xs_hbm, wg_hbm, wu_hbm, wd_hbm, ys_hbm,
                    wg_buf, wu_buf, wd_buf, xbuf, obuf, wsems, xsems, osems, *, n_blk, max_runs):
    layout = _plan_layout(n_blk, max_runs)
    (blk_e_ref, first_ref, next_e_ref, slot_ref, nblk_ref, run_src_ref, run_off_ref, run_len_ref, nrun_ref,
     ngran_ref) = (_TableView(plan_ref, layout[k]) for k in (
         "blk_e", "first", "next_e", "slot", "nblk", "run_src", "run_off", "run_len", "nrun", "ngran"))
    _experts_body(blk_e_ref, first_ref, next_e_ref, slot_ref, nblk_ref,
                  run_src_ref, run_off_ref, run_len_ref, nrun_ref, ngran_ref, n_blk - 1, max_runs,
                  xs_hbm, wg_hbm, wu_hbm, wd_hbm, ys_hbm,
                  wg_buf, wu_buf, wd_buf, xbuf, obuf, wsems, xsems, osems)


def _experts_body(blk_e_ref, first_ref, next_e_ref, slot_ref, nblk_ref,
                  run_src_ref, run_off_ref, run_len_ref, nrun_ref, ngran_ref, last, max_runs,
                  xs_hbm, wg_hbm, wu_hbm, wd_hbm, ys_hbm,
                  wg_buf, wu_buf, wd_buf, xbuf, obuf, wsems, xsems, osems):
    n = nblk_ref[0]
    gpb = EXPERT_ROWS // GRANULE

    def fetch_weights(e, s):
        return (pltpu.make_async_copy(wg_hbm.at[e], wg_buf.at[s], wsems.at[0, s]),
                pltpu.make_async_copy(wu_hbm.at[e], wu_buf.at[s], wsems.at[1, s]),
                pltpu.make_async_copy(wd_hbm.at[e], wd_buf.at[s], wsems.at[2, s]))

    def run_copies(b, r, s):
        k = b * max_runs + r
        length = run_len_ref[k]
        hbm_rows = pl.ds(run_src_ref[k], length)
        blk_rows = pl.ds(run_off_ref[k], length)
        return (pltpu.make_async_copy(xs_hbm.at[hbm_rows], xbuf.at[s, blk_rows], xsems.at[s]),
                pltpu.make_async_copy(obuf.at[s, blk_rows], ys_hbm.at[hbm_rows], osems.at[s]))

    def start_in(b, s, runs):
        lax.fori_loop(0, runs, lambda r, c: (run_copies(b, r, s)[0].start(), c)[1], 0)

    def start_out(b, s):
        lax.fori_loop(0, nrun_ref[b], lambda r, c: (run_copies(b, r, s)[1].start(), c)[1], 0)

    def wait_in(b, s):
        count = ngran_ref[b]
        pltpu.make_async_copy(xs_hbm.at[pl.ds(0, count)], xbuf.at[s, pl.ds(0, count)], xsems.at[s]).wait()

    def wait_out(b, s):
        count = ngran_ref[b]
        pltpu.make_async_copy(obuf.at[s, pl.ds(0, count)], ys_hbm.at[pl.ds(0, count)], osems.at[s]).wait()

    xbuf[...] = jnp.zeros_like(xbuf)
    for cp in fetch_weights(blk_e_ref[0], 0):
        cp.start()
    start_in(0, 0, nrun_ref[0])
    start_in(1, 1, jnp.where(n > 1, nrun_ref[1], 0))

    def block(i, carry):
        xs = lax.rem(i, X_SLOTS)
        os = i % 2
        ws = slot_ref[i]

        @pl.when(first_ref[i] == 1)
        def _():
            for cp in fetch_weights(blk_e_ref[i], ws):
                cp.wait()

            @pl.when(next_e_ref[i] >= 0)
            def _():
                for cp in fetch_weights(next_e_ref[i], 1 - ws):
                    cp.start()

        @pl.when(i >= 2)
        def _():
            wait_out(i - 2, os)

        ahead = jnp.minimum(i + 2, last)
        start_in(ahead, lax.rem(i + 2, X_SLOTS), jnp.where(i + 2 < n, nrun_ref[ahead], 0))

        wait_in(i, xs)

        def mlp(rows):
            granules = rows // GRANULE
            xl, xr = _unpack_halves(xbuf[xs, :granules].reshape(rows, HALF))
            x = jnp.concatenate([xl, xr], axis=1)
            a = _dot(x, wg_buf[ws].astype(BF16))
            b = _dot(x, wu_buf[ws].astype(BF16))
            hmid = (a * jax.nn.sigmoid(a) * b).astype(BF16)
            y = _dot(hmid, wd_buf[ws].astype(BF16))
            packed = _pack_halves(y[:, :HALF].astype(BF16).astype(F32), y[:, HALF:].astype(BF16).astype(F32))
            obuf[os, :granules] = packed.reshape(granules, GRANULE, HALF)

        pads = -(-ngran_ref[i] // (EXPERT_PAD // GRANULE))
        for k in range(1, EXPERT_ROWS // EXPERT_PAD + 1):
            pl.when(pads == k)(functools.partial(mlp, k * EXPERT_PAD))
        start_out(i, os)
        return carry

    lax.fori_loop(0, n, block, 0)

    @pl.when(n >= 2)
    def _():
        wait_out(n - 2, n % 2)
    wait_out(n - 1, (n - 1) % 2)


def _experts(plan_table, n_blk, max_runs, xs_loc, w_gate, w_up, w_down):
    granules = xs_loc.reshape(-1, GRANULE, HALF)
    hbm = pl.BlockSpec(memory_space=pl.ANY)
    return pl.pallas_call(
        functools.partial(_experts_kernel, n_blk=n_blk, max_runs=max_runs),
        grid_spec=pltpu.PrefetchScalarGridSpec(
            num_scalar_prefetch=1,
            grid=(1,),
            in_specs=[hbm, hbm, hbm, hbm],
            out_specs=hbm,
            scratch_shapes=[
                pltpu.VMEM((2, D_MODEL, D_EXPERT), F32),
                pltpu.VMEM((2, D_MODEL, D_EXPERT), F32),
                pltpu.VMEM((2, D_EXPERT, D_MODEL), F32),
                pltpu.VMEM((X_SLOTS, EXPERT_ROWS // GRANULE, GRANULE, HALF), I32),
                pltpu.VMEM((2, EXPERT_ROWS // GRANULE, GRANULE, HALF), I32),
                pltpu.SemaphoreType.DMA((3, 2)),
                pltpu.SemaphoreType.DMA((X_SLOTS,)),
                pltpu.SemaphoreType.DMA((2,)),
            ],
        ),
        out_shape=jax.ShapeDtypeStruct(granules.shape, I32),
        input_output_aliases={1: 0},
        compiler_params=pltpu.CompilerParams(
            dimension_semantics=("arbitrary",), vmem_limit_bytes=VMEM_LIMIT),
        name="experts",
    )(plan_table, granules, w_gate, w_up, w_down).reshape(xs_loc.shape)


def _combine_kernel(x2_ref, pos_ref, gate_ref, g_ref, ys_ref, o_ref):
    r = lax.broadcasted_iota(I32, (LOCAL_ROWS, TRUNK_ROWS), 0)
    for tile in range(COMBINE_TILES):
        toks = slice(tile * TRUNK_ROWS, (tile + 1) * TRUNK_ROWS)
        w_t = (jnp.where(r == pos_ref[0:1, toks], gate_ref[0:1, toks], 0.0)
               + jnp.where(r == pos_ref[1:2, toks], gate_ref[1:2, toks], 0.0)).astype(BF16)
        yl, yr = _unpack_halves(ys_ref[tile * LOCAL_ROWS:(tile + 1) * LOCAL_ROWS, :])
        moe = jnp.concatenate([_dot_tn(w_t, yl), _dot_tn(w_t, yr)], axis=-1)
        o_ref[toks, :] = _rms(x2_ref[toks, :] + moe, g_ref[...])


def _combine(x2, pos_tk, gates_tk, g_final, ys_loc):
    t = x2.shape[0]
    rows = COMBINE_TILES * TRUNK_ROWS
    return pl.pallas_call(
        _combine_kernel,
        grid=(t // rows,),
        in_specs=[
            pl.BlockSpec((rows, D_MODEL), lambda i: (i, 0)),
            pl.BlockSpec((SUBLANES, rows), lambda i: (0, i)),
            pl.BlockSpec((SUBLANES, rows), lambda i: (0, i)),
            pl.BlockSpec((1, D_MODEL), lambda i: (0, 0)),
            pl.BlockSpec((COMBINE_TILES * LOCAL_ROWS, HALF), lambda i: (i, 0)),
        ],
        out_specs=pl.BlockSpec((rows, D_MODEL), lambda i: (i, 0)),
        out_shape=jax.ShapeDtypeStruct((t, D_MODEL), F32),
        compiler_params=pltpu.CompilerParams(vmem_limit_bytes=VMEM_LIMIT),
        name="combine",
    )(x2, pos_tk, gates_tk, g_final, ys_loc)


def _router_params(w_rg, b_rg, w_re, b_re):
    w = jnp.zeros((ROUTER_ROWS, D_MODEL), F32)
    w = w.at[0:N_GROUPS].set(w_rg.T).at[EXPERT_LOGIT_ROW0:EXPERT_LOGIT_ROW0 + N_EXPERTS].set(w_re.T)
    b = jnp.zeros((ROUTER_ROWS,), F32)
    b = (b.at[0:N_GROUPS].set(b_rg).at[N_GROUPS:EXPERT_LOGIT_ROW0].set(NEG_BIG)
         .at[EXPERT_LOGIT_ROW0:EXPERT_LOGIT_ROW0 + N_EXPERTS].set(b_re))
    return w.astype(BF16), b.reshape(ROUTER_ROWS, 1)


def _plan(cnt, max_padded_rows):
    n_tiles = cnt.shape[0]
    piece = (cnt + GRANULE - 1) // GRANULE * GRANULE
    lend = jnp.cumsum(piece, axis=1)
    lstart = lend - piece
    tot = jnp.sum(piece, axis=0)
    padded = (tot + EXPERT_PAD - 1) // EXPERT_PAD * EXPERT_PAD
    cum_tiles = jnp.cumsum(piece, axis=0)

    per_expert = (padded + EXPERT_ROWS - 1) // EXPERT_ROWS
    blk_end = jnp.cumsum(per_expert)
    pads_per_blk = EXPERT_ROWS // EXPERT_PAD
    n_blk = (max_padded_rows // EXPERT_PAD + N_EXPERTS * (pads_per_blk - 1)) // pads_per_blk + 1
    blk = jnp.arange(n_blk, dtype=I32)
    blk_e = jnp.minimum(jnp.sum((blk_end[None, :] <= blk[:, None]).astype(I32), axis=1), N_EXPERTS - 1)
    nblk = blk_end[-1:]
    valid = blk < nblk

    of_blk_e = blk_e[:, None] == jnp.arange(N_EXPERTS, dtype=I32)[None, :]
    pick = lambda table: jnp.sum(jnp.where(of_blk_e[:, None, :], table[None, :, :], 0), axis=2)
    pick1 = lambda vec: jnp.sum(jnp.where(of_blk_e, vec[None, :], 0), axis=1)
    seg_off = (blk - pick1(blk_end - per_expert)) * EXPERT_ROWS
    seg_end = jnp.where(valid, jnp.minimum(seg_off + EXPERT_ROWS, pick1(tot)), seg_off)
    ngran = (seg_end - seg_off) // GRANULE
    piece_end = pick(cum_tiles)
    piece_start = piece_end - pick(piece)
    lo = jnp.maximum(piece_start, seg_off[:, None])
    hi = jnp.minimum(piece_end, seg_end[:, None])
    has_run = hi > lo
    tile_base = (jnp.arange(n_tiles, dtype=I32) * LOCAL_ROWS)[None, :] + pick(lstart)
    src_granule = (tile_base + lo - piece_start) // GRANULE
    off_granule = (lo - seg_off[:, None]) // GRANULE
    len_granule = (hi - lo) // GRANULE
    run_of = jnp.cumsum(has_run.astype(I32), axis=1) - 1
    is_run = has_run[:, :, None] & (run_of[:, :, None] == jnp.arange(n_tiles, dtype=I32)[None, None, :])
    compact = lambda v: jnp.sum(jnp.where(is_run, v[:, :, None], 0), axis=1)
    run_src, run_off, run_len = compact(src_granule), compact(off_granule), compact(len_granule)
    nrun = jnp.sum(has_run.astype(I32), axis=1)
    change =jnp.concatenate([jnp.ones((1,), bool), blk_e[1:] != blk_e[:-1]])
    slot = (jnp.cumsum(change.astype(I32)) - 1) % 2
    later = (blk_e[None, :] > blk_e[:, None]) & valid[None, :]
    next_e = jnp.min(jnp.where(later, blk_e[None, :], N_EXPERTS), axis=1)
    next_e = jnp.where(next_e == N_EXPERTS, -1, next_e)
    tables = dict(
        nblk=nblk, blk_e=blk_e, first=change & valid, next_e=next_e, slot=slot, nrun=nrun, ngran=ngran,
        run_src=run_src.reshape(-1), run_off=run_off.reshape(-1), run_len=run_len.reshape(-1),
    )
    order = ("nblk",) + _PLAN_PER_BLOCK + _PLAN_PER_RUN
    return jnp.concatenate([tables[k].astype(I32) for k in order]), n_blk, n_tiles


def _layer(x2d, mem2d, batch, seq, norm_mix_g, w_in, conv_w, conv_b, head_norm_g, w_out,
           norm_xa_g, norm_mem_g, w_q, w_kv, w_o, norm_ffn_g, w_rg, b_rg, w_re, b_re,
           w_gate, w_up, w_down, out_norm_g):
    t = x2d.shape[0]
    n_tiles = t // TRUNK_ROWS
    row = lambda v: v.reshape(1, -1)
    hg = head_norm_g.reshape(-1)
    gm = _group_mean_matrix()

    kv = _kv_proj(mem2d, row(norm_mem_g), w_kv)
    conv_n, uf = _mixer_in(x2d, row(norm_mix_g), w_in.astype(BF16), conv_w, row(conv_b),
                           row(hg[:CONV_CH]), gm, batch, seq)
    fft_n = _fourier(uf.reshape(batch, seq, FFT_CH), _fft_stage2_matrices(seq), _fft_channel_matrix(seq),
                     gm, row(hg[CONV_CH:]), batch, seq)
    w_r_t, b_r = _router_params(w_rg, b_rg, w_re, b_re)
    x2, xs_loc, pos, gates, cnt = _trunk(
        x2d, conv_n, fft_n, w_out.astype(BF16), row(norm_xa_g), w_q.astype(BF16), kv, w_o.astype(BF16),
        row(norm_ffn_g), w_r_t, b_r, _strict_upper(TRUNK_ROWS), _strict_lower(N_EXPERTS), seq)

    max_rows = n_tiles * LOCAL_ROWS + N_EXPERTS * (EXPERT_PAD - GRANULE)
    n_global_rows = -(-max_rows // EXPERT_PAD) * EXPERT_PAD
    plan_table, n_blk, max_runs = _plan(cnt[:, :, 0], n_global_rows)
    ys_loc = _experts(plan_table, n_blk, max_runs, xs_loc, w_gate, w_up, w_down)
    return _combine(x2, pos, gates, row(out_norm_g), ys_loc)


def kernel(x, mem, norm_mix_g, w_in, conv_w, conv_b, head_norm_g, w_out, norm_xa_g, norm_mem_g, w_q, w_kv,
           w_o, norm_ffn_g, w_route_group, b_route_group, w_route_expert, b_route_expert, w_gate, w_up,
           w_down, final_norm_g):
    batch, seq, _ = x.shape
    depth = norm_mix_g.shape[0]
    assert depth == 1, "the final norm is fused into the last layer's combine kernel"
    x2d = x.reshape(batch * seq, D_MODEL)
    mem2d = mem.reshape(batch * MEM_LEN, D_MODEL)
    l = 0
    out = _layer(x2d, mem2d, batch, seq, norm_mix_g[l], w_in[l], conv_w[l], conv_b[l], head_norm_g[l],
                 w_out[l], norm_xa_g[l], norm_mem_g[l], w_q[l], w_kv[l], w_o[l], norm_ffn_g[l],
                 w_route_group[l], b_route_group[l], w_route_expert[l], b_route_expert[l],
                 w_gate[l], w_up[l], w_down[l], final_norm_g)
    return out.reshape(batch, seq, D_MODEL)
```

```python
import functools
import math

import numpy as np
import jax
import jax.numpy as jnp
from jax import lax
from jax.experimental import pallas as pl
from jax.experimental.pallas import tpu as pltpu

F32 = jnp.float32
BF16 = jnp.bfloat16
I32 = jnp.int32

D_MODEL = 1024
HALF = D_MODEL // 2
HEAD_DIM = 64
CONV_CH = 512
FFT_CH = 512
IN_COLS = 3 * CONV_CH + FFT_CH
MEM_LEN = 256
XA_HEADS = 4
XA_HEAD_DIM = D_MODEL // XA_HEADS
N_GROUPS = 4
EXPERTS_PER_GROUP = 8
N_EXPERTS = 32
TOP_K = 2
D_EXPERT = 512
EPS = 1e-6

FFT_N1 = 16
FFT_N2 = 256
FFT_K1_PER_STEP = 4

LANES = 128
MXU_COLS = 256
SUBLANES = 8
GRANULE = SUBLANES
GRANULE_SHIFT = GRANULE.bit_length() - 1
MIX_ROWS = 1024
TRUNK_ROWS = 512
LOCAL_ROWS = TOP_K * TRUNK_ROWS + N_EXPERTS * GRANULE
COMBINE_TILES = 2
EXPERT_PAD = 128
EXPERT_ROWS = 512
X_SLOTS = 3
ROUTER_ROWS = 128
EXPERT_LOGIT_ROW0 = SUBLANES
NEG_BIG = -1e30
HI16 = -65536

VMEM_LIMIT = 56 * 1024 * 1024


def _rms(x, g):
    return x * lax.rsqrt(jnp.mean(x * x, axis=-1, keepdims=True) + EPS) * g


def _dot(a, b):
    return jnp.dot(a, b, preferred_element_type=F32)


def _dot_nt(a, b):
    return lax.dot_general(a, b, (((1,), (1,)), ((), ())), preferred_element_type=F32)


def _dot_tn(a, b):
    return lax.dot_general(a, b, (((0,), (0,)), ((), ())), preferred_element_type=F32)


def _pack_halves(left_f32, right_f32):
    lb = lax.bitcast_convert_type(left_f32, I32)
    rb = lax.shift_right_logical(lax.bitcast_convert_type(right_f32, I32), jnp.int32(16))
    return lb | rb


def _unpack_halves(packed_i32):
    left = lax.bitcast_convert_type(packed_i32 & jnp.int32(HI16), F32)
    right = lax.bitcast_convert_type(lax.shift_left(packed_i32, jnp.int32(16)), F32)
    return left.astype(BF16), right.astype(BF16)


def _group_mean_matrix():
    g = np.kron(np.eye(MXU_COLS // HEAD_DIM), np.full((HEAD_DIM, HEAD_DIM), 1.0 / HEAD_DIM))
    return jnp.asarray(g, dtype=BF16)


def _head_mean_square(y, gm):
    sq = (y * y).astype(BF16)
    return jnp.concatenate([_dot(sq[:, c:c + MXU_COLS], gm) for c in range(0, y.shape[1], MXU_COLS)], axis=1)


def _fft_stage2_matrices(seq):
    k1 = np.arange(FFT_N1)[:, None, None]
    k2 = np.arange(FFT_N2)[None, :, None]
    s2 = np.arange(FFT_N2)[None, None, :]
    ang = 2.0 * np.pi * ((s2 * (k1 + FFT_N1 * k2)) % seq) / seq
    c, s = np.cos(ang), np.sin(ang)
    top = np.concatenate([c, s], axis=2)
    bot = np.concatenate([-s, c], axis=2)
    return jnp.asarray(np.concatenate([top, bot], axis=1), dtype=BF16)


def _fft_channel_matrix(seq):
    c = np.arange(HEAD_DIM)
    ang = 2.0 * np.pi * ((c[:, None] * c[None, :]) % HEAD_DIM) / HEAD_DIM
    scale = 1.0 / math.sqrt(seq * HEAD_DIM)
    eye = np.eye(MXU_COLS // HEAD_DIM)
    cs = np.concatenate([np.kron(eye, np.cos(ang)), np.kron(eye, np.sin(ang))], axis=0) * scale
    return jnp.asarray(cs, dtype=BF16)


def _strict_upper(n):
    return jnp.asarray(np.triu(np.ones((n, n)), k=1), dtype=BF16)


def _strict_lower(n):
    return jnp.asarray(np.tril(np.ones((n, n)), k=-1), dtype=BF16)


def _kv_kernel(mem_ref, g_ref, w_ref, o_ref):
    h = _rms(mem_ref[...], g_ref[...]).astype(BF16)
    o_ref[...] = _dot(h, w_ref[...].astype(BF16)).astype(BF16)


def _kv_proj(mem2d, g, w_kv):
    rows = mem2d.shape[0]
    cols = w_kv.shape[1]
    cb = 512
    return pl.pallas_call(
        _kv_kernel,
        grid=(cols // cb,),
        in_specs=[
            pl.BlockSpec((rows, D_MODEL), lambda j: (0, 0)),
            pl.BlockSpec((1, D_MODEL), lambda j: (0, 0)),
            pl.BlockSpec((D_MODEL, cb), lambda j: (0, j)),
        ],
        out_specs=pl.BlockSpec((rows, cb), lambda j: (0, j)),
        out_shape=jax.ShapeDtypeStruct((rows, cols), BF16),
        compiler_params=pltpu.CompilerParams(vmem_limit_bytes=VMEM_LIMIT),
        name="kv_proj",
    )(mem2d, g, w_kv)


def _mixer_in_kernel(x_ref, xp_ref, xn_ref, g_ref, w_ref, cw_ref, cb_ref, hg_ref, gm_ref,
                     conv_ref, uf_ref):
    i = pl.program_id(1)
    n_i = pl.num_programs(1)
    rows = x_ref.shape[0]
    g = g_ref[...]
    h = _rms(x_ref[...], g).astype(BF16)
    u_cv = _dot(h, w_ref[:, CONV_CH:3 * CONV_CH])
    cv = u_cv[:, :CONV_CH] * u_cv[:, CONV_CH:]

    hh = jnp.concatenate([_rms(xp_ref[...], g), _rms(xn_ref[...], g)], axis=0).astype(BF16)
    uh = _dot(hh, w_ref[:, CONV_CH:3 * CONV_CH])
    cvh = uh[:, :CONV_CH] * uh[:, CONV_CH:]
    cv_prev = cvh[SUBLANES - 1:SUBLANES, :] * jnp.where(i == 0, 0.0, 1.0)
    cv_next = cvh[SUBLANES:SUBLANES + 1, :] * jnp.where(i == n_i - 1, 0.0, 1.0)

    row = lax.broadcasted_iota(I32, cv.shape, 0)
    cv_up = jnp.where(row == 0, cv_prev, pltpu.roll(cv, 1, 0))
    cv_dn = jnp.where(row == rows - 1, cv_next, pltpu.roll(cv, rows - 1, 0))
    z = cw_ref[0:1, :] * cv_up + cw_ref[1:2, :] * cv + cw_ref[2:3, :] * cv_dn + cb_ref[...]
    uf_ref[...] = _dot(h, w_ref[:, 3 * CONV_CH:]).astype(BF16)
    y = _dot(h, w_ref[:, :CONV_CH]) * z
    ms = _head_mean_square(y, gm_ref[...])
    conv_ref[...] = (y * lax.rsqrt(ms + EPS) * hg_ref[...]).astype(BF16)


def _mixer_in(x2d, g, w_in, conv_w, conv_b, hg_conv, gm, batch, seq):
    n_i = seq // MIX_ROWS
    t = x2d.shape[0]
    r8 = MIX_ROWS // SUBLANES
    last8 = t // SUBLANES - 1
    return pl.pallas_call(
        _mixer_in_kernel,
        grid=(batch, n_i),
        in_specs=[
            pl.BlockSpec((MIX_ROWS, D_MODEL), lambda b, i: (b * n_i + i, 0)),
            pl.BlockSpec((SUBLANES, D_MODEL), lambda b, i: (jnp.maximum((b * n_i + i) * r8 - 1, 0), 0)),
            pl.BlockSpec((SUBLANES, D_MODEL), lambda b, i: (jnp.minimum((b * n_i + i + 1) * r8, last8), 0)),
            pl.BlockSpec((1, D_MODEL), lambda b, i: (0, 0)),
            pl.BlockSpec((D_MODEL, IN_COLS), lambda b, i: (0, 0)),
            pl.BlockSpec((3, CONV_CH), lambda b, i: (0, 0)),
            pl.BlockSpec((1, CONV_CH), lambda b, i: (0, 0)),
            pl.BlockSpec((1, CONV_CH), lambda b, i: (0, 0)),
            pl.BlockSpec((MXU_COLS, MXU_COLS), lambda b, i: (0, 0)),
        ],
        out_specs=[
            pl.BlockSpec((MIX_ROWS, CONV_CH), lambda b, i: (b * n_i + i, 0)),
            pl.BlockSpec((MIX_ROWS, FFT_CH), lambda b, i: (b * n_i + i, 0)),
        ],
        out_shape=[
            jax.ShapeDtypeStruct((t, CONV_CH), BF16),
            jax.ShapeDtypeStruct((t, FFT_CH), BF16),
        ],
        compiler_params=pltpu.CompilerParams(vmem_limit_bytes=VMEM_LIMIT),
        name="mixer_in",
    )(x2d, x2d, x2d, g, w_in, conv_w, conv_b, hg_conv, gm)


_S1_ROWS = 16
_S1_LANES = 128


def _lincomb(terms):
    acc = None
    for coef, val in terms:
        if abs(coef) < 1e-12:
            continue
        if abs(coef - 1.0) < 1e-12:
            term, neg = val, False
        elif abs(coef + 1.0) < 1e-12:
            term, neg = val, True
        else:
            term, neg = coef * val, False
        if acc is None:
            acc = -term if neg else term
        else:
            acc = acc - term if neg else acc + term
    return acc


def _fft_stage1(x_ref, a_ref):
    half = FFT_N1 // 2
    cos = [[math.cos(2 * math.pi * ((k * j) % FFT_N1) / FFT_N1) for j in range(FFT_N1)] for k in range(FFT_N1)]
    sin = [[math.sin(2 * math.pi * ((k * j) % FFT_N1) / FFT_N1) for j in range(FFT_N1)] for k in range(FFT_N1)]

    def body(r, carry):
        r0 = pl.multiple_of(r * _S1_ROWS, _S1_ROWS)
        rows_re = pl.ds(r0, _S1_ROWS)
        rows_im = pl.ds(r0 + FFT_N2, _S1_ROWS)
        for lc in range(0, FFT_CH, _S1_LANES):
            lanes = slice(lc, lc + _S1_LANES)
            xs = [x_ref[j, rows_re, lanes].astype(F32) for j in range(FFT_N1)]
            ev = [None] + [xs[j] + xs[FFT_N1 - j] for j in range(1, half)]
            od = [None] + [xs[j] - xs[FFT_N1 - j] for j in range(1, half)]
            for k in range(half + 1):
                re = _lincomb([(1.0, xs[0]), (cos[k][half], xs[half])]
                              + [(cos[k][j], ev[j]) for j in range(1, half)])
                a_ref[k, rows_re, lanes] = re.astype(BF16)
                if k in (0, half):
                    zero = jnp.zeros_like(re).astype(BF16)
                    a_ref[k, rows_im, lanes] = zero
                else:
                    im = _lincomb([(-sin[k][j], od[j]) for j in range(1, half)])
                    a_ref[k, rows_im, lanes] = im.astype(BF16)
                    a_ref[FFT_N1 - k, rows_re, lanes] = re.astype(BF16)
                    a_ref[FFT_N1 - k, rows_im, lanes] = (-im).astype(BF16)
        return carry

    lax.fori_loop(0, FFT_N2 // _S1_ROWS, body, 0)


def _fourier_kernel(x_ref, m2_ref, cs_ref, gm_ref, hg_ref, o_ref, a_ref, y_ref):
    j = pl.program_id(1)

    @pl.when(j == 0)
    def _():
        _fft_stage1(x_ref, a_ref)

    ris = [_dot(m2_ref[kk], a_ref[j * FFT_K1_PER_STEP + kk]) for kk in range(FFT_K1_PER_STEP)]
    re = jnp.concatenate([ri[:FFT_N2] for ri in ris], axis=0).astype(BF16)
    im = jnp.concatenate([ri[FFT_N2:] for ri in ris], axis=0).astype(BF16)
    y = jnp.concatenate(
        [_dot(re[:, c:c + MXU_COLS], cs_ref[:MXU_COLS, :]) + _dot(im[:, c:c + MXU_COLS], cs_ref[MXU_COLS:, :])
         for c in range(0, FFT_CH, MXU_COLS)], axis=1)
    yn = y * lax.rsqrt(_head_mean_square(y, gm_ref[...]) + EPS) * hg_ref[...]
    for kk in range(FFT_K1_PER_STEP):
        k1 = j * FFT_K1_PER_STEP + kk
        for c in range(FFT_CH // LANES):
            y_ref[c, pl.ds(k1, FFT_N2, stride=FFT_N1), :] = yn[kk * FFT_N2:(kk + 1) * FFT_N2,
                                                               c * LANES:(c + 1) * LANES]

    @pl.when(j == pl.num_programs(1) - 1)
    def _():
        for c in range(FFT_CH // LANES):
            o_ref[:, c * LANES:(c + 1) * LANES] = y_ref[c].astype(BF16)


def _fourier(uf, m2, cs, gm, hg_fft, batch, seq):
    assert seq == FFT_N1 * FFT_N2
    x4 = uf.reshape(batch, FFT_N1, FFT_N2, FFT_CH)
    out = pl.pallas_call(
        _fourier_kernel,
        grid=(batch, FFT_N1 // FFT_K1_PER_STEP),
        in_specs=[
            pl.BlockSpec((None, FFT_N1, FFT_N2, FFT_CH), lambda b, j: (b, 0, 0, 0)),
            pl.BlockSpec((FFT_K1_PER_STEP, 2 * FFT_N2, 2 * FFT_N2), lambda b, j: (j, 0, 0)),
            pl.BlockSpec((2 * MXU_COLS, MXU_COLS), lambda b, j: (0, 0)),
            pl.BlockSpec((MXU_COLS, MXU_COLS), lambda b, j: (0, 0)),
            pl.BlockSpec((1, FFT_CH), lambda b, j: (0, 0)),
        ],
        out_specs=pl.BlockSpec((seq, FFT_CH), lambda b, j: (b, 0)),
        out_shape=jax.ShapeDtypeStruct((batch * seq, FFT_CH), BF16),
        scratch_shapes=[
            pltpu.VMEM((FFT_N1, 2 * FFT_N2, FFT_CH), BF16),
            pltpu.VMEM((FFT_CH // LANES, seq, LANES), F32),
        ],
        compiler_params=pltpu.CompilerParams(
            dimension_semantics=("arbitrary", "arbitrary"), vmem_limit_bytes=VMEM_LIMIT),
        name="fourier",
    )(x4, m2, cs, gm, hg_fft)
    return out


def _first_index_of_max(vals, vmax, row):
    return jnp.min(jnp.where(vals == vmax, row, vals.shape[0]), axis=0, keepdims=True)


def _route(lg):
    cols = lg.shape[1]
    row8 = lax.broadcasted_iota(I32, (EXPERTS_PER_GROUP, cols), 0)
    gl = lg[0:EXPERT_LOGIT_ROW0, :]
    gmax = jnp.max(gl, axis=0, keepdims=True)
    g_w = 1.0 / jnp.sum(jnp.exp(gl - gmax), axis=0, keepdims=True)
    g_idx = _first_index_of_max(gl, gmax, row8)

    el = lg[EXPERT_LOGIT_ROW0:EXPERT_LOGIT_ROW0 + EXPERTS_PER_GROUP, :]
    for g in range(1, N_GROUPS):
        row0 = EXPERT_LOGIT_ROW0 + g * EXPERTS_PER_GROUP
        el = jnp.where(g_idx == g, lg[row0:row0 + EXPERTS_PER_GROUP, :], el)
    emax = jnp.max(el, axis=0, keepdims=True)
    ee = jnp.exp(el - emax)
    e_prob = ee / jnp.sum(ee, axis=0, keepdims=True)
    p1 = jnp.max(e_prob, axis=0, keepdims=True)
    i1 = _first_index_of_max(e_prob, p1, row8)
    rest = jnp.where(row8 == i1, -1.0, e_prob)
    p2 = jnp.max(rest, axis=0, keepdims=True)
    i2 = _first_index_of_max(rest, p2, row8)
    denom = p1 + p2
    e1 = g_idx * EXPERTS_PER_GROUP + i1
    e2 = g_idx * EXPERTS_PER_GROUP + i2
    return e1, e2, g_w * p1 / denom, g_w * p2 / denom


def _trunk_kernel(x_ref, conv_ref, fft_ref, wout_ref, gxa_ref, wq_ref, k_ref, v_ref, wo_ref,
                  gffn_ref, wr_ref, br_ref, tri_ref, ltri_ref,
                  x2_ref, xs_ref, pos_ref, gate_ref, cnt_ref, h3_s, lg_s):
    @pl.when(pl.program_id(0) == 0)
    def _():
        h3_s[...] = jnp.zeros_like(h3_s)
        lg_s[...] = jnp.zeros_like(lg_s)

    h3 = h3_s[...]
    lg = lg_s[...]
    n_tok = lg.shape[1]

    x1 = x_ref[...] + _dot(jnp.concatenate([conv_ref[...], fft_ref[...]], axis=1), wout_ref[...])

    e1, e2, gate1, gate2 = _route(lg)
    row32 = lax.broadcasted_iota(I32, (N_EXPERTS, n_tok), 0)
    hit1 = row32 == e1
    hit2 = row32 == e2
    onehot = jnp.where(hit1 | hit2, 1.0, 0.0)
    before = _dot(onehot.astype(BF16), tri_ref[...])
    cnt = jnp.sum(onehot, axis=1, keepdims=True).astype(I32)
    piece = jnp.left_shift(jnp.right_shift(cnt + (GRANULE - 1), GRANULE_SHIFT), GRANULE_SHIFT)
    piece_b = jnp.broadcast_to(piece.astype(F32), (N_EXPERTS, LANES)).astype(BF16)
    start = _dot(ltri_ref[...], piece_b)[:, 0:1]
    slot = before + start
    pos1 = jnp.sum(jnp.where(hit1, slot, 0.0), axis=0, keepdims=True).astype(I32)
    pos2 = jnp.sum(jnp.where(hit2, slot, 0.0), axis=0, keepdims=True).astype(I32)
    pos_ref[0:1, :] = pos1
    pos_ref[1:2, :] = pos2
    pos_ref[TOP_K:, :] = jnp.zeros((SUBLANES - TOP_K, n_tok), I32)
    gate_ref[0:1, :] = gate1
    gate_ref[1:2, :] = gate2
    gate_ref[TOP_K:, :] = jnp.zeros((SUBLANES - TOP_K, n_tok), F32)
    cnt_ref[...] = jnp.broadcast_to(cnt, (N_EXPERTS, LANES))

    h2 = _rms(x1, gxa_ref[...]).astype(BF16)
    q = _dot(h2, wq_ref[...]).astype(BF16)

    r = lax.broadcasted_iota(I32, (LOCAL_ROWS, n_tok), 0)
    perm = jnp.where((r == pos1) | (r == pos2), 1.0, 0.0).astype(BF16)

    outs = []
    for hd in range(XA_HEADS):
        cols = slice(hd * XA_HEAD_DIM, (hd + 1) * XA_HEAD_DIM)
        s = _dot_nt(q[:, cols], k_ref[:, cols]) * (XA_HEAD_DIM ** -0.5)
        s = s - jnp.max(s, axis=-1, keepdims=True)
        p = jnp.exp(s)
        p = p / jnp.sum(p, axis=-1, keepdims=True)
        outs.append(_dot(p.astype(BF16), v_ref[:, cols]).astype(BF16))
        if hd == 1:
            sorted_left = _dot(perm, h3[:, :HALF])

    o = jnp.concatenate(outs, axis=-1)
    x2 = x1 + _dot(o, wo_ref[...])
    x2_ref[...] = x2
    sorted_right = _dot(perm, h3[:, HALF:])
    h3_next = _rms(x2, gffn_ref[...]).astype(BF16)
    xs_ref[...] = _pack_halves(sorted_left, sorted_right)
    h3_s[...] = h3_next
    lg_s[...] = _dot_nt(wr_ref[...], h3_next) + br_ref[...]


def _trunk(x2d, conv_n, fft_n, w_out, g_xa, w_q, kv, w_o, g_ffn, w_r_t, b_r, tri, ltri, seq):
    t = x2d.shape[0]
    n_tiles = t // TRUNK_ROWS
    n_per_batch = seq // TRUNK_ROWS
    const = lambda i: (0, 0)
    dense = lambda i: jnp.minimum(i, n_tiles - 1)
    routed = lambda i: jnp.maximum(i - 1, 0)
    return pl.pallas_call(
        _trunk_kernel,
        grid=(n_tiles + 1,),
        in_specs=[
            pl.BlockSpec((TRUNK_ROWS, D_MODEL), lambda i: (dense(i), 0)),
            pl.BlockSpec((TRUNK_ROWS, CONV_CH), lambda i: (dense(i), 0)),
            pl.BlockSpec((TRUNK_ROWS, FFT_CH), lambda i: (dense(i), 0)),
            pl.BlockSpec((D_MODEL, D_MODEL), const),
            pl.BlockSpec((1, D_MODEL), const),
            pl.BlockSpec((D_MODEL, D_MODEL), const),
            pl.BlockSpec((MEM_LEN, D_MODEL), lambda i: (dense(i) // n_per_batch, 0)),
            pl.BlockSpec((MEM_LEN, D_MODEL), lambda i: (dense(i) // n_per_batch, 1)),
            pl.BlockSpec((D_MODEL, D_MODEL), const),
            pl.BlockSpec((1, D_MODEL), const),
            pl.BlockSpec((ROUTER_ROWS, D_MODEL), const),
            pl.BlockSpec((ROUTER_ROWS, 1), const),
            pl.BlockSpec((TRUNK_ROWS, TRUNK_ROWS), const),
            pl.BlockSpec((N_EXPERTS, N_EXPERTS), const),
        ],
        out_specs=[
            pl.BlockSpec((TRUNK_ROWS, D_MODEL), lambda i: (dense(i), 0)),
            pl.BlockSpec((LOCAL_ROWS, HALF), lambda i: (routed(i), 0)),
            pl.BlockSpec((SUBLANES, TRUNK_ROWS), lambda i: (0, routed(i))),
            pl.BlockSpec((SUBLANES, TRUNK_ROWS), lambda i: (0, routed(i))),
            pl.BlockSpec((None, N_EXPERTS, LANES), lambda i: (routed(i), 0, 0)),
        ],
        out_shape=[
            jax.ShapeDtypeStruct((t, D_MODEL), F32),
            jax.ShapeDtypeStruct((n_tiles * LOCAL_ROWS, HALF), I32),
            jax.ShapeDtypeStruct((SUBLANES, t), I32),
            jax.ShapeDtypeStruct((SUBLANES, t), F32),
            jax.ShapeDtypeStruct((n_tiles, N_EXPERTS, LANES), I32),
        ],
        scratch_shapes=[
            pltpu.VMEM((TRUNK_ROWS, D_MODEL), BF16),
            pltpu.VMEM((ROUTER_ROWS, TRUNK_ROWS), F32),
        ],
        compiler_params=pltpu.CompilerParams(
            dimension_semantics=("arbitrary",), vmem_limit_bytes=VMEM_LIMIT),
        name="trunk",
    )(x2d, conv_n, fft_n, w_out, g_xa, w_q, kv, kv, w_o, g_ffn, w_r_t, b_r, tri, ltri)


_PLAN_PER_BLOCK = ("blk_e", "first", "next_e", "slot", "nrun", "ngran")
_PLAN_PER_RUN = ("run_src", "run_off", "run_len")


def _plan_layout(n_blk, max_runs):
    sizes = [("nblk", 1)] + [(k, n_blk) for k in _PLAN_PER_BLOCK] + [(k, n_blk * max_runs) for k in _PLAN_PER_RUN]
    offsets, at = {}, 0
    for name, size in sizes:
        offsets[name] = at
        at += size
    return offsets


class _TableView:
    def __init__(self, ref, offset):
        self._ref, self._offset = ref, offset

    def __getitem__(self, i):
        return self._ref[self._offset + i]


def _experts_kernel(plan_ref, xs_hbm, wg_hbm, wu_hbm, wd_hbm, ys_hbm,
                    wg_buf, wu_buf, wd_buf, xbuf, obuf, wsems, xsems, osems, *, n_blk, max_runs):
    layout = _plan_layout(n_blk, max_runs)
    (blk_e_ref, first_ref, next_e_ref, slot_ref, nblk_ref, run_src_ref, run_off_ref, run_len_ref, nrun_ref,
     ngran_ref) = (_TableView(plan_ref, layout[k]) for k in (
         "blk_e", "first", "next_e", "slot", "nblk", "run_src", "run_off", "run_len", "nrun", "ngran"))
    _experts_body(blk_e_ref, first_ref, next_e_ref, slot_ref, nblk_ref,
                  run_src_ref, run_off_ref, run_len_ref, nrun_ref, ngran_ref, n_blk - 1, max_runs,
                  xs_hbm, wg_hbm, wu_hbm, wd_hbm, ys_hbm,
                  wg_buf, wu_buf, wd_buf, xbuf, obuf, wsems, xsems, osems)


def _experts_body(blk_e_ref, first_ref, next_e_ref, slot_ref, nblk_ref,
                  run_src_ref, run_off_ref, run_len_ref, nrun_ref, ngran_ref, last, max_runs,
                  xs_hbm, wg_hbm, wu_hbm, wd_hbm, ys_hbm,
                  wg_buf, wu_buf, wd_buf, xbuf, obuf, wsems, xsems, osems):
    n = nblk_ref[0]
    gpb = EXPERT_ROWS // GRANULE

    def fetch_weights(e, s):
        return (pltpu.make_async_copy(wg_hbm.at[e], wg_buf.at[s], wsems.at[0, s]),
                pltpu.make_async_copy(wu_hbm.at[e], wu_buf.at[s], wsems.at[1, s]),
                pltpu.make_async_copy(wd_hbm.at[e], wd_buf.at[s], wsems.at[2, s]))

    def run_copies(b, r, s):
        k = b * max_runs + r
        length = run_len_ref[k]
        hbm_rows = pl.ds(run_src_ref[k], length)
        blk_rows = pl.ds(run_off_ref[k], length)
        return (pltpu.make_async_copy(xs_hbm.at[hbm_rows], xbuf.at[s, blk_rows], xsems.at[s]),
                pltpu.make_async_copy(obuf.at[s, blk_rows], ys_hbm.at[hbm_rows], osems.at[s]))

    def start_in(b, s, runs):
        lax.fori_loop(0, runs, lambda r, c: (run_copies(b, r, s)[0].start(), c)[1], 0)

    def start_out(b, s):
        lax.fori_loop(0, nrun_ref[b], lambda r, c: (run_copies(b, r, s)[1].start(), c)[1], 0)

    def wait_in(b, s):
        count = ngran_ref[b]
        pltpu.make_async_copy(xs_hbm.at[pl.ds(0, count)], xbuf.at[s, pl.ds(0, count)], xsems.at[s]).wait()

    def wait_out(b, s):
        count = ngran_ref[b]
        pltpu.make_async_copy(obuf.at[s, pl.ds(0, count)], ys_hbm.at[pl.ds(0, count)], osems.at[s]).wait()

    xbuf[...] = jnp.zeros_like(xbuf)
    for cp in fetch_weights(blk_e_ref[0], 0):
        cp.start()
    start_in(0, 0, nrun_ref[0])
    start_in(1, 1, jnp.where(n > 1, nrun_ref[1], 0))

    def block(i, carry):
        xs = lax.rem(i, X_SLOTS)
        os = i % 2
        ws = slot_ref[i]

        @pl.when(first_ref[i] == 1)
        def _():
            for cp in fetch_weights(blk_e_ref[i], ws):
                cp.wait()

            @pl.when(next_e_ref[i] >= 0)
            def _():
                for cp in fetch_weights(next_e_ref[i], 1 - ws):
                    cp.start()

        @pl.when(i >= 2)
        def _():
            wait_out(i - 2, os)

        ahead = jnp.minimum(i + 2, last)
        start_in(ahead, lax.rem(i + 2, X_SLOTS), jnp.where(i + 2 < n, nrun_ref[ahead], 0))

        wait_in(i, xs)

        def mlp(rows):
            granules = rows // GRANULE
            xl, xr = _unpack_halves(xbuf[xs, :granules].reshape(rows, HALF))
            x = jnp.concatenate([xl, xr], axis=1)
            a = _dot(x, wg_buf[ws].astype(BF16))
            b = _dot(x, wu_buf[ws].astype(BF16))
            hmid = (a * jax.nn.sigmoid(a) * b).astype(BF16)
            y = _dot(hmid, wd_buf[ws].astype(BF16))
            packed = _pack_halves(y[:, :HALF].astype(BF16).astype(F32), y[:, HALF:].astype(BF16).astype(F32))
            obuf[os, :granules] = packed.reshape(granules, GRANULE, HALF)

        pads = -(-ngran_ref[i] // (EXPERT_PAD // GRANULE))
        for k in range(1, EXPERT_ROWS // EXPERT_PAD + 1):
            pl.when(pads == k)(functools.partial(mlp, k * EXPERT_PAD))
        start_out(i, os)
        return carry

    lax.fori_loop(0, n, block, 0)

    @pl.when(n >= 2)
    def _():
        wait_out(n - 2, n % 2)
    wait_out(n - 1, (n - 1) % 2)


def _experts(plan_table, n_blk, max_runs, xs_loc, w_gate, w_up, w_down):
    granules = xs_loc.reshape(-1, GRANULE, HALF)
    hbm = pl.BlockSpec(memory_space=pl.ANY)
    return pl.pallas_call(
        functools.partial(_experts_kernel, n_blk=n_blk, max_runs=max_runs),
        grid_spec=pltpu.PrefetchScalarGridSpec(
            num_scalar_prefetch=1,
            grid=(1,),
            in_specs=[hbm, hbm, hbm, hbm],
            out_specs=hbm,
            scratch_shapes=[
                pltpu.VMEM((2, D_MODEL, D_EXPERT), F32),
                pltpu.VMEM((2, D_MODEL, D_EXPERT), F32),
                pltpu.VMEM((2, D_EXPERT, D_MODEL), F32),
                pltpu.VMEM((X_SLOTS, EXPERT_ROWS // GRANULE, GRANULE, HALF), I32),
                pltpu.VMEM((2, EXPERT_ROWS // GRANULE, GRANULE, HALF), I32),
                pltpu.SemaphoreType.DMA((3, 2)),
                pltpu.SemaphoreType.DMA((X_SLOTS,)),
                pltpu.SemaphoreType.DMA((2,)),
            ],
        ),
        out_shape=jax.ShapeDtypeStruct(granules.shape, I32),
        input_output_aliases={1: 0},
        compiler_params=pltpu.CompilerParams(
            dimension_semantics=("arbitrary",), vmem_limit_bytes=VMEM_LIMIT),
        name="experts",
    )(plan_table, granules, w_gate, w_up, w_down).reshape(xs_loc.shape)


def _combine_kernel(x2_ref, pos_ref, gate_ref, g_ref, ys_ref, o_ref):
    r = lax.broadcasted_iota(I32, (LOCAL_ROWS, TRUNK_ROWS), 0)
    for tile in range(COMBINE_TILES):
        toks = slice(tile * TRUNK_ROWS, (tile + 1) * TRUNK_ROWS)
        w_t = (jnp.where(r == pos_ref[0:1, toks], gate_ref[0:1, toks], 0.0)
               + jnp.where(r == pos_ref[1:2, toks], gate_ref[1:2, toks], 0.0)).astype(BF16)
        yl, yr = _unpack_halves(ys_ref[tile * LOCAL_ROWS:(tile + 1) * LOCAL_ROWS, :])
        moe = jnp.concatenate([_dot_tn(w_t, yl), _dot_tn(w_t, yr)], axis=-1)
        o_ref[toks, :] = _rms(x2_ref[toks, :] + moe, g_ref[...])


def _combine(x2, pos_tk, gates_tk, g_final, ys_loc):
    t = x2.shape[0]
    rows = COMBINE_TILES * TRUNK_ROWS
    return pl.pallas_call(
        _combine_kernel,
        grid=(t // rows,),
        in_specs=[
            pl.BlockSpec((rows, D_MODEL), lambda i: (i, 0)),
            pl.BlockSpec((SUBLANES, rows), lambda i: (0, i)),
            pl.BlockSpec((SUBLANES, rows), lambda i: (0, i)),
            pl.BlockSpec((1, D_MODEL), lambda i: (0, 0)),
            pl.BlockSpec((COMBINE_TILES * LOCAL_ROWS, HALF), lambda i: (i, 0)),
        ],
        out_specs=pl.BlockSpec((rows, D_MODEL), lambda i: (i, 0)),
        out_shape=jax.ShapeDtypeStruct((t, D_MODEL), F32),
        compiler_params=pltpu.CompilerParams(vmem_limit_bytes=VMEM_LIMIT),
        name="combine",
    )(x2, pos_tk, gates_tk, g_final, ys_loc)


def _router_params(w_rg, b_rg, w_re, b_re):
    w = jnp.zeros((ROUTER_ROWS, D_MODEL), F32)
    w = w.at[0:N_GROUPS].set(w_rg.T).at[EXPERT_LOGIT_ROW0:EXPERT_LOGIT_ROW0 + N_EXPERTS].set(w_re.T)
    b = jnp.zeros((ROUTER_ROWS,), F32)
    b = (b.at[0:N_GROUPS].set(b_rg).at[N_GROUPS:EXPERT_LOGIT_ROW0].set(NEG_BIG)
         .at[EXPERT_LOGIT_ROW0:EXPERT_LOGIT_ROW0 + N_EXPERTS].set(b_re))
    return w.astype(BF16), b.reshape(ROUTER_ROWS, 1)


def _plan(cnt, max_padded_rows):
    n_tiles = cnt.shape[0]
    piece = (cnt + GRANULE - 1) // GRANULE * GRANULE
    lend = jnp.cumsum(piece, axis=1)
    lstart = lend - piece
    tot = jnp.sum(piece, axis=0)
    padded = (tot + EXPERT_PAD - 1) // EXPERT_PAD * EXPERT_PAD
    cum_tiles = jnp.cumsum(piece, axis=0)

    per_expert = (padded + EXPERT_ROWS - 1) // EXPERT_ROWS
    blk_end = jnp.cumsum(per_expert)
    pads_per_blk = EXPERT_ROWS // EXPERT_PAD
    n_blk = (max_padded_rows // EXPERT_PAD + N_EXPERTS * (pads_per_blk - 1)) // pads_per_blk + 1
    blk = jnp.arange(n_blk, dtype=I32)
    blk_e = jnp.minimum(jnp.sum((blk_end[None, :] <= blk[:, None]).astype(I32), axis=1), N_EXPERTS - 1)
    nblk = blk_end[-1:]
    valid = blk < nblk

    of_blk_e = blk_e[:, None] == jnp.arange(N_EXPERTS, dtype=I32)[None, :]
    pick = lambda table: jnp.sum(jnp.where(of_blk_e[:, None, :], table[None, :, :], 0), axis=2)
    pick1 = lambda vec: jnp.sum(jnp.where(of_blk_e, vec[None, :], 0), axis=1)
    seg_off = (blk - pick1(blk_end - per_expert)) * EXPERT_ROWS
    seg_end = jnp.where(valid, jnp.minimum(seg_off + EXPERT_ROWS, pick1(tot)), seg_off)
    ngran = (seg_end - seg_off) // GRANULE
    piece_end = pick(cum_tiles)
    piece_start = piece_end - pick(piece)
    lo = jnp.maximum(piece_start, seg_off[:, None])
    hi = jnp.minimum(piece_end, seg_end[:, None])
    has_run = hi > lo
    tile_base = (jnp.arange(n_tiles, dtype=I32) * LOCAL_ROWS)[None, :] + pick(lstart)
    src_granule = (tile_base + lo - piece_start) // GRANULE
    off_granule = (lo - seg_off[:, None]) // GRANULE
    len_granule = (hi - lo) // GRANULE
    run_of = jnp.cumsum(has_run.astype(I32), axis=1) - 1
    is_run = has_run[:, :, None] & (run_of[:, :, None] == jnp.arange(n_tiles, dtype=I32)[None, None, :])
    compact = lambda v: jnp.sum(jnp.where(is_run, v[:, :, None], 0), axis=1)
    run_src, run_off, run_len = compact(src_granule), compact(off_granule), compact(len_granule)
    nrun = jnp.sum(has_run.astype(I32), axis=1)
    change =jnp.concatenate([jnp.ones((1,), bool), blk_e[1:] != blk_e[:-1]])
    slot = (jnp.cumsum(change.astype(I32)) - 1) % 2
    later = (blk_e[None, :] > blk_e[:, None]) & valid[None, :]
    next_e = jnp.min(jnp.where(later, blk_e[None, :], N_EXPERTS), axis=1)
    next_e = jnp.where(next_e == N_EXPERTS, -1, next_e)
    tables = dict(
        nblk=nblk, blk_e=blk_e, first=change & valid, next_e=next_e, slot=slot, nrun=nrun, ngran=ngran,
        run_src=run_src.reshape(-1), run_off=run_off.reshape(-1), run_len=run_len.reshape(-1),
    )
    order = ("nblk",) + _PLAN_PER_BLOCK + _PLAN_PER_RUN
    return jnp.concatenate([tables[k].astype(I32) for k in order]), n_blk, n_tiles


def _layer(x2d, mem2d, batch, seq, norm_mix_g, w_in, conv_w, conv_b, head_norm_g, w_out,
           norm_xa_g, norm_mem_g, w_q, w_kv, w_o, norm_ffn_g, w_rg, b_rg, w_re, b_re,
           w_gate, w_up, w_down, out_norm_g):
    t = x2d.shape[0]
    n_tiles = t // TRUNK_ROWS
    row = lambda v: v.reshape(1, -1)
    hg = head_norm_g.reshape(-1)
    gm = _group_mean_matrix()

    kv = _kv_proj(mem2d, row(norm_mem_g), w_kv)
    conv_n, uf = _mixer_in(x2d, row(norm_mix_g), w_in.astype(BF16), conv_w, row(conv_b),
                           row(hg[:CONV_CH]), gm, batch, seq)
    fft_n = _fourier(uf.reshape(batch, seq, FFT_CH), _fft_stage2_matrices(seq), _fft_channel_matrix(seq),
                     gm, row(hg[CONV_CH:]), batch, seq)
    w_r_t, b_r = _router_params(w_rg, b_rg, w_re, b_re)
    x2, xs_loc, pos, gates, cnt = _trunk(
        x2d, conv_n, fft_n, w_out.astype(BF16), row(norm_xa_g), w_q.astype(BF16), kv, w_o.astype(BF16),
        row(norm_ffn_g), w_r_t, b_r, _strict_upper(TRUNK_ROWS), _strict_lower(N_EXPERTS), seq)

    max_rows = n_tiles * LOCAL_ROWS + N_EXPERTS * (EXPERT_PAD - GRANULE)
    n_global_rows = -(-max_rows // EXPERT_PAD) * EXPERT_PAD
    plan_table, n_blk, max_runs = _plan(cnt[:, :, 0], n_global_rows)
    ys_loc = _experts(plan_table, n_blk, max_runs, xs_loc, w_gate, w_up, w_down)
    return _combine(x2, pos, gates, row(out_norm_g), ys_loc)


def kernel(x, mem, norm_mix_g, w_in, conv_w, conv_b, head_norm_g, w_out, norm_xa_g, norm_mem_g, w_q, w_kv,
           w_o, norm_ffn_g, w_route_group, b_route_group, w_route_expert, b_route_expert, w_gate, w_up,
           w_down, final_norm_g):
    batch, seq, _ = x.shape
    depth = norm_mix_g.shape[0]
    assert depth == 1, "the final norm is fused into the last layer's combine kernel"
    x2d = x.reshape(batch * seq, D_MODEL)
    mem2d = mem.reshape(batch * MEM_LEN, D_MODEL)
    l = 0
    out = _layer(x2d, mem2d, batch, seq, norm_mix_g[l], w_in[l], conv_w[l], conv_b[l], head_norm_g[l],
                 w_out[l], norm_xa_g[l], norm_mem_g[l], w_q[l], w_kv[l], w_o[l], norm_ffn_g[l],
                 w_route_group[l], b_route_group[l], w_route_expert[l], b_route_expert[l],
                 w_gate[l], w_up[l], w_down[l], final_norm_g)
    return out.reshape(batch, seq, D_MODEL)
```

```python
import functools
import math

import numpy as np
import jax
import jax.numpy as jnp
from jax import lax
from jax.experimental import pallas as pl
from jax.experimental.pallas import tpu as pltpu

F32 = jnp.float32
BF16 = jnp.bfloat16
I32 = jnp.int32

D_MODEL = 1024
HALF = D_MODEL // 2
HEAD_DIM = 64
CONV_CH = 512
FFT_CH = 512
IN_COLS = 3 * CONV_CH + FFT_CH
MEM_LEN = 256
XA_HEADS = 4
XA_HEAD_DIM = D_MODEL // XA_HEADS
N_GROUPS = 4
EXPERTS_PER_GROUP = 8
N_EXPERTS = 32
TOP_K = 2
D_EXPERT = 512
EPS = 1e-6

FFT_N1 = 16
FFT_N2 = 256
FFT_K1_PER_STEP = 4

LANES = 128
MXU_COLS = 256
SUBLANES = 8
GRANULE = SUBLANES
GRANULE_SHIFT = GRANULE.bit_length() - 1
MIX_ROWS = 1024
TRUNK_ROWS = 512
LOCAL_ROWS = TOP_K * TRUNK_ROWS + N_EXPERTS * GRANULE
COMBINE_TILES = 2
EXPERT_PAD = 128
EXPERT_ROWS = 640
X_SLOTS = 3
ROUTER_ROWS = 128
EXPERT_LOGIT_ROW0 = SUBLANES
NEG_BIG = -1e30
HI16 = -65536

VMEM_LIMIT = 56 * 1024 * 1024


def _rms(x, g):
    return x * lax.rsqrt(jnp.mean(x * x, axis=-1, keepdims=True) + EPS) * g


def _dot(a, b):
    return jnp.dot(a, b, preferred_element_type=F32)


def _dot_nt(a, b):
    return lax.dot_general(a, b, (((1,), (1,)), ((), ())), preferred_element_type=F32)


def _dot_tn(a, b):
    return lax.dot_general(a, b, (((0,), (0,)), ((), ())), preferred_element_type=F32)


def _pack_halves(left_f32, right_f32):
    lb = lax.bitcast_convert_type(left_f32, I32)
    rb = lax.shift_right_logical(lax.bitcast_convert_type(right_f32, I32), jnp.int32(16))
    return lb | rb


def _unpack_halves(packed_i32):
    left = lax.bitcast_convert_type(packed_i32 & jnp.int32(HI16), F32)
    right = lax.bitcast_convert_type(lax.shift_left(packed_i32, jnp.int32(16)), F32)
    return left.astype(BF16), right.astype(BF16)


def _group_mean_matrix():
    g = np.kron(np.eye(MXU_COLS // HEAD_DIM), np.full((HEAD_DIM, HEAD_DIM), 1.0 / HEAD_DIM))
    return jnp.asarray(g, dtype=BF16)


def _head_mean_square(y, gm):
    sq = (y * y).astype(BF16)
    return jnp.concatenate([_dot(sq[:, c:c + MXU_COLS], gm) for c in range(0, y.shape[1], MXU_COLS)], axis=1)


def _fft_stage2_matrices(seq):
    k1 = np.arange(FFT_N1)[:, None, None]
    k2 = np.arange(FFT_N2)[None, :, None]
    s2 = np.arange(FFT_N2)[None, None, :]
    ang = 2.0 * np.pi * ((s2 * (k1 + FFT_N1 * k2)) % seq) / seq
    c, s = np.cos(ang), np.sin(ang)
    top = np.concatenate([c, s], axis=2)
    bot = np.concatenate([-s, c], axis=2)
    return jnp.asarray(np.concatenate([top, bot], axis=1), dtype=BF16)


def _fft_channel_matrix(seq):
    c = np.arange(HEAD_DIM)
    ang = 2.0 * np.pi * ((c[:, None] * c[None, :]) % HEAD_DIM) / HEAD_DIM
    scale = 1.0 / math.sqrt(seq * HEAD_DIM)
    eye = np.eye(MXU_COLS // HEAD_DIM)
    cs = np.concatenate([np.kron(eye, np.cos(ang)), np.kron(eye, np.sin(ang))], axis=0) * scale
    return jnp.asarray(cs, dtype=BF16)


def _strict_upper(n):
    return jnp.asarray(np.triu(np.ones((n, n)), k=1), dtype=BF16)


def _strict_lower(n):
    return jnp.asarray(np.tril(np.ones((n, n)), k=-1), dtype=BF16)


def _kv_kernel(mem_ref, g_ref, w_ref, o_ref):
    h = _rms(mem_ref[...], g_ref[...]).astype(BF16)
    o_ref[...] = _dot(h, w_ref[...].astype(BF16)).astype(BF16)


def _kv_proj(mem2d, g, w_kv):
    rows = mem2d.shape[0]
    cols = w_kv.shape[1]
    cb = 512
    return pl.pallas_call(
        _kv_kernel,
        grid=(cols // cb,),
        in_specs=[
            pl.BlockSpec((rows, D_MODEL), lambda j: (0, 0)),
            pl.BlockSpec((1, D_MODEL), lambda j: (0, 0)),
            pl.BlockSpec((D_MODEL, cb), lambda j: (0, j)),
        ],
        out_specs=pl.BlockSpec((rows, cb), lambda j: (0, j)),
        out_shape=jax.ShapeDtypeStruct((rows, cols), BF16),
        compiler_params=pltpu.CompilerParams(vmem_limit_bytes=VMEM_LIMIT),
        name="kv_proj",
    )(mem2d, g, w_kv)


def _mixer_in_kernel(x_ref, xp_ref, xn_ref, g_ref, w_ref, cw_ref, cb_ref, hg_ref, gm_ref,
                     conv_ref, uf_ref):
    i = pl.program_id(1)
    n_i = pl.num_programs(1)
    rows = x_ref.shape[0]
    g = g_ref[...]
    h = _rms(x_ref[...], g).astype(BF16)
    u_cv = _dot(h, w_ref[:, CONV_CH:3 * CONV_CH])
    cv = u_cv[:, :CONV_CH] * u_cv[:, CONV_CH:]

    hh = jnp.concatenate([_rms(xp_ref[...], g), _rms(xn_ref[...], g)], axis=0).astype(BF16)
    uh = _dot(hh, w_ref[:, CONV_CH:3 * CONV_CH])
    cvh = uh[:, :CONV_CH] * uh[:, CONV_CH:]
    cv_prev = cvh[SUBLANES - 1:SUBLANES, :] * jnp.where(i == 0, 0.0, 1.0)
    cv_next = cvh[SUBLANES:SUBLANES + 1, :] * jnp.where(i == n_i - 1, 0.0, 1.0)

    row = lax.broadcasted_iota(I32, cv.shape, 0)
    cv_up = jnp.where(row == 0, cv_prev, pltpu.roll(cv, 1, 0))
    cv_dn = jnp.where(row == rows - 1, cv_next, pltpu.roll(cv, rows - 1, 0))
    z = cw_ref[0:1, :] * cv_up + cw_ref[1:2, :] * cv + cw_ref[2:3, :] * cv_dn + cb_ref[...]
    uf_ref[...] = _dot(h, w_ref[:, 3 * CONV_CH:]).astype(BF16)
    y = _dot(h, w_ref[:, :CONV_CH]) * z
    ms = _head_mean_square(y, gm_ref[...])
    conv_ref[...] = (y * lax.rsqrt(ms + EPS) * hg_ref[...]).astype(BF16)


def _mixer_in(x2d, g, w_in, conv_w, conv_b, hg_conv, gm, batch, seq):
    n_i = seq // MIX_ROWS
    t = x2d.shape[0]
    r8 = MIX_ROWS // SUBLANES
    last8 = t // SUBLANES - 1
    return pl.pallas_call(
        _mixer_in_kernel,
        grid=(batch, n_i),
        in_specs=[
            pl.BlockSpec((MIX_ROWS, D_MODEL), lambda b, i: (b * n_i + i, 0)),
            pl.BlockSpec((SUBLANES, D_MODEL), lambda b, i: (jnp.maximum((b * n_i + i) * r8 - 1, 0), 0)),
            pl.BlockSpec((SUBLANES, D_MODEL), lambda b, i: (jnp.minimum((b * n_i + i + 1) * r8, last8), 0)),
            pl.BlockSpec((1, D_MODEL), lambda b, i: (0, 0)),
            pl.BlockSpec((D_MODEL, IN_COLS), lambda b, i: (0, 0)),
            pl.BlockSpec((3, CONV_CH), lambda b, i: (0, 0)),
            pl.BlockSpec((1, CONV_CH), lambda b, i: (0, 0)),
            pl.BlockSpec((1, CONV_CH), lambda b, i: (0, 0)),
            pl.BlockSpec((MXU_COLS, MXU_COLS), lambda b, i: (0, 0)),
        ],
        out_specs=[
            pl.BlockSpec((MIX_ROWS, CONV_CH), lambda b, i: (b * n_i + i, 0)),
            pl.BlockSpec((MIX_ROWS, FFT_CH), lambda b, i: (b * n_i + i, 0)),
        ],
        out_shape=[
            jax.ShapeDtypeStruct((t, CONV_CH), BF16),
            jax.ShapeDtypeStruct((t, FFT_CH), BF16),
        ],
        compiler_params=pltpu.CompilerParams(vmem_limit_bytes=VMEM_LIMIT),
        name="mixer_in",
    )(x2d, x2d, x2d, g, w_in, conv_w, conv_b, hg_conv, gm)


_S1_ROWS = 16
_S1_LANES = 128


def _lincomb(terms):
    acc = None
    for coef, val in terms:
        if abs(coef) < 1e-12:
            continue
        if abs(coef - 1.0) < 1e-12:
            term, neg = val, False
        elif abs(coef + 1.0) < 1e-12:
            term, neg = val, True
        else:
            term, neg = coef * val, False
        if acc is None:
            acc = -term if neg else term
        else:
            acc = acc - term if neg else acc + term
    return acc


def _fft_stage1(x_ref, a_ref):
    half = FFT_N1 // 2
    cos = [[math.cos(2 * math.pi * ((k * j) % FFT_N1) / FFT_N1) for j in range(FFT_N1)] for k in range(FFT_N1)]
    sin = [[math.sin(2 * math.pi * ((k * j) % FFT_N1) / FFT_N1) for j in range(FFT_N1)] for k in range(FFT_N1)]

    def body(r, carry):
        r0 = pl.multiple_of(r * _S1_ROWS, _S1_ROWS)
        rows_re = pl.ds(r0, _S1_ROWS)
        rows_im = pl.ds(r0 + FFT_N2, _S1_ROWS)
        for lc in range(0, FFT_CH, _S1_LANES):
            lanes = slice(lc, lc + _S1_LANES)
            xs = [x_ref[j, rows_re, lanes].astype(F32) for j in range(FFT_N1)]
            ev = [None] + [xs[j] + xs[FFT_N1 - j] for j in range(1, half)]
            od = [None] + [xs[j] - xs[FFT_N1 - j] for j in range(1, half)]
            for k in range(half + 1):
                re = _lincomb([(1.0, xs[0]), (cos[k][half], xs[half])]
                              + [(cos[k][j], ev[j]) for j in range(1, half)])
                a_ref[k, rows_re, lanes] = re.astype(BF16)
                if k in (0, half):
                    zero = jnp.zeros_like(re).astype(BF16)
                    a_ref[k, rows_im, lanes] = zero
                else:
                    im = _lincomb([(-sin[k][j], od[j]) for j in range(1, half)])
                    a_ref[k, rows_im, lanes] = im.astype(BF16)
                    a_ref[FFT_N1 - k, rows_re, lanes] = re.astype(BF16)
                    a_ref[FFT_N1 - k, rows_im, lanes] = (-im).astype(BF16)
        return carry

    lax.fori_loop(0, FFT_N2 // _S1_ROWS, body, 0)


def _fourier_kernel(x_ref, m2_ref, cs_ref, gm_ref, hg_ref, o_ref, a_ref, y_ref):
    j = pl.program_id(1)

    @pl.when(j == 0)
    def _():
        _fft_stage1(x_ref, a_ref)

    ris = [_dot(m2_ref[kk], a_ref[j * FFT_K1_PER_STEP + kk]) for kk in range(FFT_K1_PER_STEP)]
    re = jnp.concatenate([ri[:FFT_N2] for ri in ris], axis=0).astype(BF16)
    im = jnp.concatenate([ri[FFT_N2:] for ri in ris], axis=0).astype(BF16)
    y = jnp.concatenate(
        [_dot(re[:, c:c + MXU_COLS], cs_ref[:MXU_COLS, :]) + _dot(im[:, c:c + MXU_COLS], cs_ref[MXU_COLS:, :])
         for c in range(0, FFT_CH, MXU_COLS)], axis=1)
    yn = y * lax.rsqrt(_head_mean_square(y, gm_ref[...]) + EPS) * hg_ref[...]
    for kk in range(FFT_K1_PER_STEP):
        k1 = j * FFT_K1_PER_STEP + kk
        for c in range(FFT_CH // LANES):
            y_ref[c, pl.ds(k1, FFT_N2, stride=FFT_N1), :] = yn[kk * FFT_N2:(kk + 1) * FFT_N2,
                                                               c * LANES:(c + 1) * LANES]

    @pl.when(j == pl.num_programs(1) - 1)
    def _():
        for c in range(FFT_CH // LANES):
            o_ref[:, c * LANES:(c + 1) * LANES] = y_ref[c].astype(BF16)


def _fourier(uf, m2, cs, gm, hg_fft, batch, seq):
    assert seq == FFT_N1 * FFT_N2
    x4 = uf.reshape(batch, FFT_N1, FFT_N2, FFT_CH)
    out = pl.pallas_call(
        _fourier_kernel,
        grid=(batch, FFT_N1 // FFT_K1_PER_STEP),
        in_specs=[
            pl.BlockSpec((None, FFT_N1, FFT_N2, FFT_CH), lambda b, j: (b, 0, 0, 0)),
            pl.BlockSpec((FFT_K1_PER_STEP, 2 * FFT_N2, 2 * FFT_N2), lambda b, j: (j, 0, 0)),
            pl.BlockSpec((2 * MXU_COLS, MXU_COLS), lambda b, j: (0, 0)),
            pl.BlockSpec((MXU_COLS, MXU_COLS), lambda b, j: (0, 0)),
            pl.BlockSpec((1, FFT_CH), lambda b, j: (0, 0)),
        ],
        out_specs=pl.BlockSpec((seq, FFT_CH), lambda b, j: (b, 0)),
        out_shape=jax.ShapeDtypeStruct((batch * seq, FFT_CH), BF16),
        scratch_shapes=[
            pltpu.VMEM((FFT_N1, 2 * FFT_N2, FFT_CH), BF16),
            pltpu.VMEM((FFT_CH // LANES, seq, LANES), F32),
        ],
        compiler_params=pltpu.CompilerParams(
            dimension_semantics=("arbitrary", "arbitrary"), vmem_limit_bytes=VMEM_LIMIT),
        name="fourier",
    )(x4, m2, cs, gm, hg_fft)
    return out


def _first_index_of_max(vals, vmax, row):
    return jnp.min(jnp.where(vals == vmax, row, vals.shape[0]), axis=0, keepdims=True)


def _route(lg):
    cols = lg.shape[1]
    row8 = lax.broadcasted_iota(I32, (EXPERTS_PER_GROUP, cols), 0)
    gl = lg[0:EXPERT_LOGIT_ROW0, :]
    gmax = jnp.max(gl, axis=0, keepdims=True)
    g_w = 1.0 / jnp.sum(jnp.exp(gl - gmax), axis=0, keepdims=True)
    g_idx = _first_index_of_max(gl, gmax, row8)

    el = lg[EXPERT_LOGIT_ROW0:EXPERT_LOGIT_ROW0 + EXPERTS_PER_GROUP, :]
    for g in range(1, N_GROUPS):
        row0 = EXPERT_LOGIT_ROW0 + g * EXPERTS_PER_GROUP
        el = jnp.where(g_idx == g, lg[row0:row0 + EXPERTS_PER_GROUP, :], el)
    emax = jnp.max(el, axis=0, keepdims=True)
    ee = jnp.exp(el - emax)
    e_prob = ee / jnp.sum(ee, axis=0, keepdims=True)
    p1 = jnp.max(e_prob, axis=0, keepdims=True)
    i1 = _first_index_of_max(e_prob, p1, row8)
    rest = jnp.where(row8 == i1, -1.0, e_prob)
    p2 = jnp.max(rest, axis=0, keepdims=True)
    i2 = _first_index_of_max(rest, p2, row8)
    denom = p1 + p2
    e1 = g_idx * EXPERTS_PER_GROUP + i1
    e2 = g_idx * EXPERTS_PER_GROUP + i2
    return e1, e2, g_w * p1 / denom, g_w * p2 / denom


def _trunk_kernel(x_ref, conv_ref, fft_ref, wout_ref, gxa_ref, wq_ref, k_ref, v_ref, wo_ref,
                  gffn_ref, wr_ref, br_ref, tri_ref, ltri_ref,
                  x2_ref, xs_ref, pos_ref, gate_ref, cnt_ref, h3_s, lg_s):
    @pl.when(pl.program_id(0) == 0)
    def _():
        h3_s[...] = jnp.zeros_like(h3_s)
        lg_s[...] = jnp.zeros_like(lg_s)

    h3 = h3_s[...]
    lg = lg_s[...]
    n_tok = lg.shape[1]

    x1 = x_ref[...] + _dot(jnp.concatenate([conv_ref[...], fft_ref[...]], axis=1), wout_ref[...])

    e1, e2, gate1, gate2 = _route(lg)
    row32 = lax.broadcasted_iota(I32, (N_EXPERTS, n_tok), 0)
    hit1 = row32 == e1
    hit2 = row32 == e2
    onehot = jnp.where(hit1 | hit2, 1.0, 0.0)
    before = _dot(onehot.astype(BF16), tri_ref[...])
    cnt = jnp.sum(onehot, axis=1, keepdims=True).astype(I32)
    piece = jnp.left_shift(jnp.right_shift(cnt + (GRANULE - 1), GRANULE_SHIFT), GRANULE_SHIFT)
    piece_b = jnp.broadcast_to(piece.astype(F32), (N_EXPERTS, LANES)).astype(BF16)
    start = _dot(ltri_ref[...], piece_b)[:, 0:1]
    slot = before + start
    pos1 = jnp.sum(jnp.where(hit1, slot, 0.0), axis=0, keepdims=True).astype(I32)
    pos2 = jnp.sum(jnp.where(hit2, slot, 0.0), axis=0, keepdims=True).astype(I32)
    pos_ref[0:1, :] = pos1
    pos_ref[1:2, :] = pos2
    pos_ref[TOP_K:, :] = jnp.zeros((SUBLANES - TOP_K, n_tok), I32)
    gate_ref[0:1, :] = gate1
    gate_ref[1:2, :] = gate2
    gate_ref[TOP_K:, :] = jnp.zeros((SUBLANES - TOP_K, n_tok), F32)
    cnt_ref[...] = jnp.broadcast_to(cnt, (N_EXPERTS, LANES))

    h2 = _rms(x1, gxa_ref[...]).astype(BF16)
    q = _dot(h2, wq_ref[...]).astype(BF16)

    r = lax.broadcasted_iota(I32, (LOCAL_ROWS, n_tok), 0)
    perm = jnp.where((r == pos1) | (r == pos2), 1.0, 0.0).astype(BF16)

    outs = []
    for hd in range(XA_HEADS):
        cols = slice(hd * XA_HEAD_DIM, (hd + 1) * XA_HEAD_DIM)
        s = _dot_nt(q[:, cols], k_ref[:, cols]) * (XA_HEAD_DIM ** -0.5)
        s = s - jnp.max(s, axis=-1, keepdims=True)
        p = jnp.exp(s)
        p = p / jnp.sum(p, axis=-1, keepdims=True)
        outs.append(_dot(p.astype(BF16), v_ref[:, cols]).astype(BF16))
        if hd == 1:
            sorted_left = _dot(perm, h3[:, :HALF])

    o = jnp.concatenate(outs, axis=-1)
    x2 = x1 + _dot(o, wo_ref[...])
    x2_ref[...] = x2
    sorted_right = _dot(perm, h3[:, HALF:])
    h3_next = _rms(x2, gffn_ref[...]).astype(BF16)
    xs_ref[...] = _pack_halves(sorted_left, sorted_right)
    h3_s[...] = h3_next
    lg_s[...] = _dot_nt(wr_ref[...], h3_next) + br_ref[...]


def _trunk(x2d, conv_n, fft_n, w_out, g_xa, w_q, kv, w_o, g_ffn, w_r_t, b_r, tri, ltri, seq):
    t = x2d.shape[0]
    n_tiles = t // TRUNK_ROWS
    n_per_batch = seq // TRUNK_ROWS
    const = lambda i: (0, 0)
    dense = lambda i: jnp.minimum(i, n_tiles - 1)
    routed = lambda i: jnp.maximum(i - 1, 0)
    return pl.pallas_call(
        _trunk_kernel,
        grid=(n_tiles + 1,),
        in_specs=[
            pl.BlockSpec((TRUNK_ROWS, D_MODEL), lambda i: (dense(i), 0)),
            pl.BlockSpec((TRUNK_ROWS, CONV_CH), lambda i: (dense(i), 0)),
            pl.BlockSpec((TRUNK_ROWS, FFT_CH), lambda i: (dense(i), 0)),
            pl.BlockSpec((D_MODEL, D_MODEL), const),
            pl.BlockSpec((1, D_MODEL), const),
            pl.BlockSpec((D_MODEL, D_MODEL), const),
            pl.BlockSpec((MEM_LEN, D_MODEL), lambda i: (dense(i) // n_per_batch, 0)),
            pl.BlockSpec((MEM_LEN, D_MODEL), lambda i: (dense(i) // n_per_batch, 1)),
            pl.BlockSpec((D_MODEL, D_MODEL), const),
            pl.BlockSpec((1, D_MODEL), const),
            pl.BlockSpec((ROUTER_ROWS, D_MODEL), const),
            pl.BlockSpec((ROUTER_ROWS, 1), const),
            pl.BlockSpec((TRUNK_ROWS, TRUNK_ROWS), const),
            pl.BlockSpec((N_EXPERTS, N_EXPERTS), const),
        ],
        out_specs=[
            pl.BlockSpec((TRUNK_ROWS, D_MODEL), lambda i: (dense(i), 0)),
            pl.BlockSpec((LOCAL_ROWS, HALF), lambda i: (routed(i), 0)),
            pl.BlockSpec((SUBLANES, TRUNK_ROWS), lambda i: (0, routed(i))),
            pl.BlockSpec((SUBLANES, TRUNK_ROWS), lambda i: (0, routed(i))),
            pl.BlockSpec((None, N_EXPERTS, LANES), lambda i: (routed(i), 0, 0)),
        ],
        out_shape=[
            jax.ShapeDtypeStruct((t, D_MODEL), F32),
            jax.ShapeDtypeStruct((n_tiles * LOCAL_ROWS, HALF), I32),
            jax.ShapeDtypeStruct((SUBLANES, t), I32),
            jax.ShapeDtypeStruct((SUBLANES, t), F32),
            jax.ShapeDtypeStruct((n_tiles, N_EXPERTS, LANES), I32),
        ],
        scratch_shapes=[
            pltpu.VMEM((TRUNK_ROWS, D_MODEL), BF16),
            pltpu.VMEM((ROUTER_ROWS, TRUNK_ROWS), F32),
        ],
        compiler_params=pltpu.CompilerParams(
            dimension_semantics=("arbitrary",), vmem_limit_bytes=VMEM_LIMIT),
        name="trunk",
    )(x2d, conv_n, fft_n, w_out, g_xa, w_q, kv, kv, w_o, g_ffn, w_r_t, b_r, tri, ltri)


_PLAN_PER_BLOCK = ("blk_e", "first", "next_e", "slot", "nrun", "ngran")
_PLAN_PER_RUN = ("run_src", "run_off", "run_len")


def _plan_layout(n_blk, max_runs):
    sizes = [("nblk", 1)] + [(k, n_blk) for k in _PLAN_PER_BLOCK] + [(k, n_blk * max_runs) for k in _PLAN_PER_RUN]
    offsets, at = {}, 0
    for name, size in sizes:
        offsets[name] = at
        at += size
    return offsets


class _TableView:
    def __init__(self, ref, offset):
        self._ref, self._offset = ref, offset

    def __getitem__(self, i):
        return self._ref[self._offset + i]


def _experts_kernel(plan_ref, xs_hbm, wg_hbm, wu_hbm, wd_hbm, ys_hbm,
                    wg_buf, wu_buf, wd_buf, xbuf, obuf, wsems, xsems, osems, *, n_blk, max_runs):
    layout = _plan_layout(n_blk, max_runs)
    (blk_e_ref, first_ref, next_e_ref, slot_ref, nblk_ref, run_src_ref, run_off_ref, run_len_ref, nrun_ref,
     ngran_ref) = (_TableView(plan_ref, layout[k]) for k in (
         "blk_e", "first", "next_e", "slot", "nblk", "run_src", "run_off", "run_len", "nrun", "ngran"))
    _experts_body(blk_e_ref, first_ref, next_e_ref, slot_ref, nblk_ref,
                  run_src_ref, run_off_ref, run_len_ref, nrun_ref, ngran_ref, n_blk - 1, max_runs,
                  xs_hbm, wg_hbm, wu_hbm, wd_hbm, ys_hbm,
                  wg_buf, wu_buf, wd_buf, xbuf, obuf, wsems, xsems, osems)


def _experts_body(blk_e_ref, first_ref, next_e_ref, slot_ref, nblk_ref,
                  run_src_ref, run_off_ref, run_len_ref, nrun_ref, ngran_ref, last, max_runs,
                  xs_hbm, wg_hbm, wu_hbm, wd_hbm, ys_hbm,
                  wg_buf, wu_buf, wd_buf, xbuf, obuf, wsems, xsems, osems):
    n = nblk_ref[0]
    gpb = EXPERT_ROWS // GRANULE

    def fetch_weights(e, s):
        return (pltpu.make_async_copy(wg_hbm.at[e], wg_buf.at[s], wsems.at[0, s]),
                pltpu.make_async_copy(wu_hbm.at[e], wu_buf.at[s], wsems.at[1, s]),
                pltpu.make_async_copy(wd_hbm.at[e], wd_buf.at[s], wsems.at[2, s]))

    def run_copies(b, r, s):
        k = b * max_runs + r
        length = run_len_ref[k]
        hbm_rows = pl.ds(run_src_ref[k], length)
        blk_rows = pl.ds(run_off_ref[k], length)
        return (pltpu.make_async_copy(xs_hbm.at[hbm_rows], xbuf.at[s, blk_rows], xsems.at[s]),
                pltpu.make_async_copy(obuf.at[s, blk_rows], ys_hbm.at[hbm_rows], osems.at[s]))

    def start_in(b, s, runs):
        lax.fori_loop(0, runs, lambda r, c: (run_copies(b, r, s)[0].start(), c)[1], 0)

    def start_out(b, s):
        lax.fori_loop(0, nrun_ref[b], lambda r, c: (run_copies(b, r, s)[1].start(), c)[1], 0)

    def wait_in(b, s):
        count = ngran_ref[b]
        pltpu.make_async_copy(xs_hbm.at[pl.ds(0, count)], xbuf.at[s, pl.ds(0, count)], xsems.at[s]).wait()

    def wait_out(b, s):
        count = ngran_ref[b]
        pltpu.make_async_copy(obuf.at[s, pl.ds(0, count)], ys_hbm.at[pl.ds(0, count)], osems.at[s]).wait()

    xbuf[...] = jnp.zeros_like(xbuf)
    for cp in fetch_weights(blk_e_ref[0], 0):
        cp.start()
    start_in(0, 0, nrun_ref[0])
    start_in(1, 1, jnp.where(n > 1, nrun_ref[1], 0))

    def block(i, carry):
        xs = lax.rem(i, X_SLOTS)
        os = i % 2
        ws = slot_ref[i]

        @pl.when(first_ref[i] == 1)
        def _():
            for cp in fetch_weights(blk_e_ref[i], ws):
                cp.wait()

            @pl.when(next_e_ref[i] >= 0)
            def _():
                for cp in fetch_weights(next_e_ref[i], 1 - ws):
                    cp.start()

        @pl.when(i >= 2)
        def _():
            wait_out(i - 2, os)

        ahead = jnp.minimum(i + 2, last)
        start_in(ahead, lax.rem(i + 2, X_SLOTS), jnp.where(i + 2 < n, nrun_ref[ahead], 0))

        wait_in(i, xs)

        def mlp(rows):
            granules = rows // GRANULE
            xl, xr = _unpack_halves(xbuf[xs, :granules].reshape(rows, HALF))
            x = jnp.concatenate([xl, xr], axis=1)
            a = _dot(x, wg_buf[ws].astype(BF16))
            b = _dot(x, wu_buf[ws].astype(BF16))
            hmid = (a * jax.nn.sigmoid(a) * b).astype(BF16)
            y = _dot(hmid, wd_buf[ws].astype(BF16))
            packed = _pack_halves(y[:, :HALF].astype(BF16).astype(F32), y[:, HALF:].astype(BF16).astype(F32))
            obuf[os, :granules] = packed.reshape(granules, GRANULE, HALF)

        pads = -(-ngran_ref[i] // (EXPERT_PAD // GRANULE))
        for k in range(1, EXPERT_ROWS // EXPERT_PAD + 1):
            pl.when(pads == k)(functools.partial(mlp, k * EXPERT_PAD))
        start_out(i, os)
        return carry

    lax.fori_loop(0, n, block, 0)

    @pl.when(n >= 2)
    def _():
        wait_out(n - 2, n % 2)
    wait_out(n - 1, (n - 1) % 2)


def _experts(plan_table, n_blk, max_runs, xs_loc, w_gate, w_up, w_down):
    granules = xs_loc.reshape(-1, GRANULE, HALF)
    hbm = pl.BlockSpec(memory_space=pl.ANY)
    return pl.pallas_call(
        functools.partial(_experts_kernel, n_blk=n_blk, max_runs=max_runs),
        grid_spec=pltpu.PrefetchScalarGridSpec(
            num_scalar_prefetch=1,
            grid=(1,),
            in_specs=[hbm, hbm, hbm, hbm],
            out_specs=hbm,
            scratch_shapes=[
                pltpu.VMEM((2, D_MODEL, D_EXPERT), F32),
                pltpu.VMEM((2, D_MODEL, D_EXPERT), F32),
                pltpu.VMEM((2, D_EXPERT, D_MODEL), F32),
                pltpu.VMEM((X_SLOTS, EXPERT_ROWS // GRANULE, GRANULE, HALF), I32),
                pltpu.VMEM((2, EXPERT_ROWS // GRANULE, GRANULE, HALF), I32),
                pltpu.SemaphoreType.DMA((3, 2)),
                pltpu.SemaphoreType.DMA((X_SLOTS,)),
                pltpu.SemaphoreType.DMA((2,)),
            ],
        ),
        out_shape=jax.ShapeDtypeStruct(granules.shape, I32),
        input_output_aliases={1: 0},
        compiler_params=pltpu.CompilerParams(
            dimension_semantics=("arbitrary",), vmem_limit_bytes=VMEM_LIMIT),
        name="experts",
    )(plan_table, granules, w_gate, w_up, w_down).reshape(xs_loc.shape)


def _combine_kernel(x2_ref, pos_ref, gate_ref, g_ref, ys_ref, o_ref):
    r = lax.broadcasted_iota(I32, (LOCAL_ROWS, TRUNK_ROWS), 0)
    for tile in range(COMBINE_TILES):
        toks = slice(tile * TRUNK_ROWS, (tile + 1) * TRUNK_ROWS)
        w_t = (jnp.where(r == pos_ref[0:1, toks], gate_ref[0:1, toks], 0.0)
               + jnp.where(r == pos_ref[1:2, toks], gate_ref[1:2, toks], 0.0)).astype(BF16)
        yl, yr = _unpack_halves(ys_ref[tile * LOCAL_ROWS:(tile + 1) * LOCAL_ROWS, :])
        moe = jnp.concatenate([_dot_tn(w_t, yl), _dot_tn(w_t, yr)], axis=-1)
        o_ref[toks, :] = _rms(x2_ref[toks, :] + moe, g_ref[...])


def _combine(x2, pos_tk, gates_tk, g_final, ys_loc):
    t = x2.shape[0]
    rows = COMBINE_TILES * TRUNK_ROWS
    return pl.pallas_call(
        _combine_kernel,
        grid=(t // rows,),
        in_specs=[
            pl.BlockSpec((rows, D_MODEL), lambda i: (i, 0)),
            pl.BlockSpec((SUBLANES, rows), lambda i: (0, i)),
            pl.BlockSpec((SUBLANES, rows), lambda i: (0, i)),
            pl.BlockSpec((1, D_MODEL), lambda i: (0, 0)),
            pl.BlockSpec((COMBINE_TILES * LOCAL_ROWS, HALF), lambda i: (i, 0)),
        ],
        out_specs=pl.BlockSpec((rows, D_MODEL), lambda i: (i, 0)),
        out_shape=jax.ShapeDtypeStruct((t, D_MODEL), F32),
        compiler_params=pltpu.CompilerParams(vmem_limit_bytes=VMEM_LIMIT),
        name="combine",
    )(x2, pos_tk, gates_tk, g_final, ys_loc)


def _router_params(w_rg, b_rg, w_re, b_re):
    group_pad = EXPERT_LOGIT_ROW0 - N_GROUPS
    tail = ROUTER_ROWS - EXPERT_LOGIT_ROW0 - N_EXPERTS
    w = jnp.concatenate([w_rg.T, jnp.zeros((group_pad, D_MODEL), F32), w_re.T, jnp.zeros((tail, D_MODEL), F32)])
    b = jnp.concatenate([b_rg, jnp.full((group_pad,), NEG_BIG, F32), b_re, jnp.zeros((tail,), F32)])
    return w.astype(BF16), b.reshape(ROUTER_ROWS, 1)


def _plan(cnt, max_padded_rows):
    n_tiles = cnt.shape[0]
    piece = (cnt + GRANULE - 1) // GRANULE * GRANULE
    lend = jnp.cumsum(piece, axis=1)
    lstart = lend - piece
    tot = jnp.sum(piece, axis=0)
    padded = (tot + EXPERT_PAD - 1) // EXPERT_PAD * EXPERT_PAD
    cum_tiles = jnp.cumsum(piece, axis=0)

    per_expert = (padded + EXPERT_ROWS - 1) // EXPERT_ROWS
    blk_end = jnp.cumsum(per_expert)
    pads_per_blk = EXPERT_ROWS // EXPERT_PAD
    n_blk = (max_padded_rows // EXPERT_PAD + N_EXPERTS * (pads_per_blk - 1)) // pads_per_blk + 1
    blk = jnp.arange(n_blk, dtype=I32)
    blk_e = jnp.minimum(jnp.sum((blk_end[None, :] <= blk[:, None]).astype(I32), axis=1), N_EXPERTS - 1)
    nblk = blk_end[-1:]
    valid = blk < nblk

    of_blk_e = blk_e[:, None] == jnp.arange(N_EXPERTS, dtype=I32)[None, :]
    pick = lambda table: jnp.sum(jnp.where(of_blk_e[:, None, :], table[None, :, :], 0), axis=2)
    pick1 = lambda vec: jnp.sum(jnp.where(of_blk_e, vec[None, :], 0), axis=1)
    seg_off = (blk - pick1(blk_end - per_expert)) * EXPERT_ROWS
    seg_end = jnp.where(valid, jnp.minimum(seg_off + EXPERT_ROWS, pick1(tot)), seg_off)
    ngran = (seg_end - seg_off) // GRANULE
    piece_end = pick(cum_tiles)
    piece_start = piece_end - pick(piece)
    lo = jnp.maximum(piece_start, seg_off[:, None])
    hi = jnp.minimum(piece_end, seg_end[:, None])
    has_run = hi > lo
    tile_base = (jnp.arange(n_tiles, dtype=I32) * LOCAL_ROWS)[None, :] + pick(lstart)
    src_granule = (tile_base + lo - piece_start) // GRANULE
    off_granule = (lo - seg_off[:, None]) // GRANULE
    len_granule = (hi - lo) // GRANULE
    run_of = jnp.cumsum(has_run.astype(I32), axis=1) - 1
    is_run = has_run[:, :, None] & (run_of[:, :, None] == jnp.arange(n_tiles, dtype=I32)[None, None, :])
    compact = lambda v: jnp.sum(jnp.where(is_run, v[:, :, None], 0), axis=1)
    run_src, run_off, run_len = compact(src_granule), compact(off_granule), compact(len_granule)
    nrun = jnp.sum(has_run.astype(I32), axis=1)
    change =jnp.concatenate([jnp.ones((1,), bool), blk_e[1:] != blk_e[:-1]])
    slot = (jnp.cumsum(change.astype(I32)) - 1) % 2
    later = (blk_e[None, :] > blk_e[:, None]) & valid[None, :]
    next_e = jnp.min(jnp.where(later, blk_e[None, :], N_EXPERTS), axis=1)
    next_e = jnp.where(next_e == N_EXPERTS, -1, next_e)
    tables = dict(
        nblk=nblk, blk_e=blk_e, first=change & valid, next_e=next_e, slot=slot, nrun=nrun, ngran=ngran,
        run_src=run_src.reshape(-1), run_off=run_off.reshape(-1), run_len=run_len.reshape(-1),
    )
    order = ("nblk",) + _PLAN_PER_BLOCK + _PLAN_PER_RUN
    return jnp.concatenate([tables[k].astype(I32) for k in order]), n_blk, n_tiles


def _layer(x2d, mem2d, batch, seq, norm_mix_g, w_in, conv_w, conv_b, head_norm_g, w_out,
           norm_xa_g, norm_mem_g, w_q, w_kv, w_o, norm_ffn_g, w_rg, b_rg, w_re, b_re,
           w_gate, w_up, w_down, out_norm_g):
    t = x2d.shape[0]
    n_tiles = t // TRUNK_ROWS
    row = lambda v: v.reshape(1, -1)
    hg = head_norm_g.reshape(-1)
    gm = _group_mean_matrix()

    kv = _kv_proj(mem2d, row(norm_mem_g), w_kv)
    conv_n, uf = _mixer_in(x2d, row(norm_mix_g), w_in.astype(BF16), conv_w, row(conv_b),
                           row(hg[:CONV_CH]), gm, batch, seq)
    fft_n = _fourier(uf.reshape(batch, seq, FFT_CH), _fft_stage2_matrices(seq), _fft_channel_matrix(seq),
                     gm, row(hg[CONV_CH:]), batch, seq)
    w_r_t, b_r = _router_params(w_rg, b_rg, w_re, b_re)
    x2, xs_loc, pos, gates, cnt = _trunk(
        x2d, conv_n, fft_n, w_out.astype(BF16), row(norm_xa_g), w_q.astype(BF16), kv, w_o.astype(BF16),
        row(norm_ffn_g), w_r_t, b_r, _strict_upper(TRUNK_ROWS), _strict_lower(N_EXPERTS), seq)

    max_rows = n_tiles * LOCAL_ROWS + N_EXPERTS * (EXPERT_PAD - GRANULE)
    n_global_rows = -(-max_rows // EXPERT_PAD) * EXPERT_PAD
    plan_table, n_blk, max_runs = _plan(cnt[:, :, 0], n_global_rows)
    ys_loc = _experts(plan_table, n_blk, max_runs, xs_loc, w_gate, w_up, w_down)
    return _combine(x2, pos, gates, row(out_norm_g), ys_loc)


def kernel(x, mem, norm_mix_g, w_in, conv_w, conv_b, head_norm_g, w_out, norm_xa_g, norm_mem_g, w_q, w_kv,
           w_o, norm_ffn_g, w_route_group, b_route_group, w_route_expert, b_route_expert, w_gate, w_up,
           w_down, final_norm_g):
    batch, seq, _ = x.shape
    depth = norm_mix_g.shape[0]
    assert depth == 1, "the final norm is fused into the last layer's combine kernel"
    x2d = x.reshape(batch * seq, D_MODEL)
    mem2d = mem.reshape(batch * MEM_LEN, D_MODEL)
    l = 0
    out = _layer(x2d, mem2d, batch, seq, norm_mix_g[l], w_in[l], conv_w[l], conv_b[l], head_norm_g[l],
                 w_out[l], norm_xa_g[l], norm_mem_g[l], w_q[l], w_kv[l], w_o[l], norm_ffn_g[l],
                 w_route_group[l], b_route_group[l], w_route_expert[l], b_route_expert[l],
                 w_gate[l], w_up[l], w_down[l], final_norm_g)
    return out.reshape(batch, seq, D_MODEL)
```
